```python
import jax, jax.numpy as jnp
from jax import lax
import numpy as np

D_MODEL = 1024
BATCH = 8
SEQ = 8192
DEPTH = 2

N_META = 16
MIX_WIDTH = D_MODEL
POOL_WIDTH = MIX_WIDTH // 2
POOL_GROUPS = 4
POOL_GROUP_DIM = POOL_WIDTH // POOL_GROUPS
POOL_WINDOWS = (2, 4, 8, 16)
GLA_WIDTH = MIX_WIDTH - POOL_WIDTH
GLA_HEADS = 4
GLA_KEY_WIDTH = GLA_WIDTH // 2
GLA_DK = GLA_KEY_WIDTH // GLA_HEADS
GLA_DV = GLA_WIDTH // GLA_HEADS
GLA_GATE_RANK = 16
GLA_GATE_TEMP = 16.0
GLA_CHUNK = 64
GLA_PAD = GLA_CHUNK - N_META
D_FF = 2816
IN_PROJ_COLS = POOL_WIDTH + 2 * GLA_KEY_WIDTH + 2 * GLA_WIDTH + GLA_GATE_RANK
DEEPNORM_ALPHA = (2.0 * DEPTH) ** 0.25
DEEPNORM_BETA = (8.0 * DEPTH) ** -0.25
LN_EPS = 1e-5
RMS_EPS = 1e-6

kernel_name = "hymba_pool_gla_macaron_deepnorm"


def layer_norm(x, g, b):
    xf = x.astype(jnp.float32)
    mu = jnp.mean(xf, axis=-1, keepdims=True)
    var = jnp.mean(jnp.square(xf - mu), axis=-1, keepdims=True)
    y = (xf - mu) * lax.rsqrt(var + LN_EPS)
    return (y * g.astype(jnp.float32) + b.astype(jnp.float32)).astype(x.dtype)


def swiglu(x, w_gate, w_up, w_down):
    return (jax.nn.silu(x @ w_gate) * (x @ w_up)) @ w_down


def pool_mixer(u, w_pool, pool_scale):
    B_, T, _ = u.shape
    uf = u.astype(jnp.float32)
    cs = jnp.cumsum(uf, axis=1)
    t = jnp.arange(T)
    outs = []
    for gi, w in enumerate(POOL_WINDOWS):
        sl = slice(gi * POOL_GROUP_DIM, (gi + 1) * POOL_GROUP_DIM)
        c = cs[..., sl]
        shifted = jnp.pad(c, ((0, 0), (w, 0), (0, 0)))[:, :T]
        cnt = jnp.minimum(t + 1, w).astype(jnp.float32)
        mean = (c - shifted) / cnt[None, :, None]
        outs.append(mean - uf[..., sl])
    p = jnp.stack(outs, axis=2).astype(u.dtype)
    p = jnp.einsum('btgc,gcd->btgd', p, w_pool).reshape(B_, T, POOL_WIDTH)
    return p * pool_scale


def gla_chunked(q, k, v, log_a):
    B_, H, Tp, dk = q.shape
    dv = v.shape[-1]
    n = Tp // GLA_CHUNK

    def chunks(z):
        return jnp.moveaxis(z.reshape(B_, H, n, GLA_CHUNK, z.shape[-1]), 2, 0)

    mask = jnp.tril(jnp.ones((GLA_CHUNK, GLA_CHUNK), dtype=bool))[:, :, None]

    def step(S, xs):
        qc, kc, vc, gc = xs
        b = jnp.cumsum(gc, axis=2)
        o_inter = jnp.einsum('bhid,bhde->bhie', qc * jnp.exp(b), S)
        diff = b[:, :, :, None, :] - b[:, :, None, :, :]
        decay = jnp.where(mask, jnp.exp(jnp.where(mask, diff, 0.0)), 0.0)
        A = jnp.einsum('bhid,bhjd,bhijd->bhij', qc, kc, decay)
        o_intra = jnp.einsum('bhij,bhje->bhie', A, vc)
        b_last = b[:, :, -1:, :]
        S_new = S * jnp.exp(b_last[:, :, 0, :])[..., None] + jnp.einsum(
            'bhjd,bhje->bhde', kc * jnp.exp(b_last - b), vc)
        return S_new, o_inter + o_intra

    S0 = jnp.zeros((B_, H, dk, dv), jnp.float32)
    _, o = lax.scan(step, S0, (chunks(q), chunks(k), chunks(v), chunks(log_a)))
    return jnp.moveaxis(o, 0, 2).reshape(B_, H, Tp, dv)


def gla_mixer(q, k, v, r, g_lr, w_gate_up, b_gate, gla_norm_g):
    B_, T, _ = q.shape
    log_a = jax.nn.log_sigmoid((g_lr @ w_gate_up + b_gate).astype(jnp.float32)) / GLA_GATE_TEMP

    def heads(z, d):
        z = z.astype(jnp.float32).reshape(B_, T, GLA_HEADS, d).transpose(0, 2, 1, 3)
        return jnp.pad(z, ((0, 0), (0, 0), (GLA_PAD, 0), (0, 0)))

    qh = heads(q, GLA_DK) * (GLA_DK ** -0.5)
    kh = heads(k, GLA_DK)
    vh = heads(v, GLA_DV)
    gh = heads(log_a, GLA_DK)
    o = gla_chunked(qh, kh, vh, gh)[:, :, GLA_PAD:]
    o = o.transpose(0, 2, 1, 3)
    o = o * lax.rsqrt(jnp.mean(jnp.square(o), axis=-1, keepdims=True) + RMS_EPS)
    o = o.reshape(B_, T, GLA_WIDTH) * gla_norm_g.astype(jnp.float32)
    return (o * jax.nn.silu(r.astype(jnp.float32))).astype(q.dtype)


def _fwd_setup_inputs(seed: int = 0) -> dict:
    key = jax.random.key(seed)
    ks = jax.random.split(key, 24)
    f32 = jnp.float32

    def nrm(k, shape, scale):
        return jax.random.normal(k, shape, f32) * scale

    L = DEPTH
    return {
        "x": nrm(ks[0], (BATCH, SEQ, D_MODEL), 1.0),
        "meta_tokens": nrm(ks[1], (N_META, D_MODEL), 1.0),
        "ffn1_w_gate": nrm(ks[2], (L, D_MODEL, D_FF), D_MODEL ** -0.5),
        "ffn1_w_up": nrm(ks[3], (L, D_MODEL, D_FF), D_MODEL ** -0.5),
        "ffn1_w_down": nrm(ks[4], (L, D_FF, D_MODEL), DEEPNORM_BETA * D_FF ** -0.5),
        "ln1_g": 1.0 + nrm(ks[5], (L, D_MODEL), 0.02),
        "ln1_b": nrm(ks[6], (L, D_MODEL), 0.02),
        "w_in": nrm(ks[7], (L, D_MODEL, IN_PROJ_COLS), D_MODEL ** -0.5),
        "w_gate_up": nrm(ks[8], (L, GLA_GATE_RANK, GLA_KEY_WIDTH), GLA_GATE_RANK ** -0.5),
        "b_gate": nrm(ks[9], (L, GLA_KEY_WIDTH), 0.1),
        "w_pool": nrm(ks[10], (L, POOL_GROUPS, POOL_GROUP_DIM, POOL_GROUP_DIM), POOL_GROUP_DIM ** -0.5),
        "pool_scale": 1.0 + nrm(ks[11], (L, POOL_WIDTH), 0.1),
        "gla_norm_g": 1.0 + nrm(ks[12], (L, GLA_WIDTH), 0.02),
        "w_out": nrm(ks[13], (L, MIX_WIDTH, D_MODEL), DEEPNORM_BETA * MIX_WIDTH ** -0.5),
        "ln2_g": 1.0 + nrm(ks[14], (L, D_MODEL), 0.02),
        "ln2_b": nrm(ks[15], (L, D_MODEL), 0.02),
        "ffn2_w_gate": nrm(ks[16], (L, D_MODEL, D_FF), D_MODEL ** -0.5),
        "ffn2_w_up": nrm(ks[17], (L, D_MODEL, D_FF), D_MODEL ** -0.5),
        "ffn2_w_down": nrm(ks[18], (L, D_FF, D_MODEL), DEEPNORM_BETA * D_FF ** -0.5),
        "ln3_g": 1.0 + nrm(ks[19], (L, D_MODEL), 0.02),
        "ln3_b": nrm(ks[20], (L, D_MODEL), 0.02),
    }


def _fwd_reference(x, meta_tokens, ffn1_w_gate, ffn1_w_up, ffn1_w_down, ln1_g, ln1_b,
              w_in, w_gate_up, b_gate, w_pool, pool_scale, gla_norm_g, w_out,
              ln2_g, ln2_b, ffn2_w_gate, ffn2_w_up, ffn2_w_down, ln3_g, ln3_b):
    B_ = x.shape[0]
    meta = jnp.broadcast_to(meta_tokens[None].astype(x.dtype), (B_, N_META, D_MODEL))
    h = jnp.concatenate([meta, x], axis=1)
    a = DEEPNORM_ALPHA
    s0 = POOL_WIDTH
    s1 = s0 + GLA_KEY_WIDTH
    s2 = s1 + GLA_KEY_WIDTH
    s3 = s2 + GLA_WIDTH
    s4 = s3 + GLA_WIDTH
    for l in range(DEPTH):
        h = layer_norm(a * h + 0.5 * swiglu(h, ffn1_w_gate[l], ffn1_w_up[l], ffn1_w_down[l]),
                       ln1_g[l], ln1_b[l])
        z = h @ w_in[l]
        u_pool = z[..., :s0]
        q, k, v, r, g_lr = z[..., s0:s1], z[..., s1:s2], z[..., s2:s3], z[..., s3:s4], z[..., s4:]
        y_pool = pool_mixer(u_pool, w_pool[l], pool_scale[l])
        y_gla = gla_mixer(q, k, v, r, g_lr, w_gate_up[l], b_gate[l], gla_norm_g[l])
        y = jnp.concatenate([y_pool, y_gla], axis=-1) @ w_out[l]
        h = layer_norm(a * h + y, ln2_g[l], ln2_b[l])
        h = layer_norm(a * h + 0.5 * swiglu(h, ffn2_w_gate[l], ffn2_w_up[l], ffn2_w_down[l]),
                       ln3_g[l], ln3_b[l])
    return h[:, N_META:]


import jax as _jax
import jax.numpy as _jnp

TWIN_FORMAT = 'train_step'
FWD_PARAMS = ['x', 'meta_tokens', 'ffn1_w_gate', 'ffn1_w_up', 'ffn1_w_down', 'ln1_g', 'ln1_b', 'w_in', 'w_gate_up', 'b_gate', 'w_pool', 'pool_scale', 'gla_norm_g', 'w_out', 'ln2_g', 'ln2_b', 'ffn2_w_gate', 'ffn2_w_up', 'ffn2_w_down', 'ln3_g', 'ln3_b']
TWIN_WEIGHTS = ['meta_tokens', 'ffn1_w_gate', 'ffn1_w_up', 'ffn1_w_down', 'ln1_g', 'ln1_b', 'w_in', 'w_gate_up', 'b_gate', 'w_pool', 'pool_scale', 'gla_norm_g', 'w_out', 'ln2_g', 'ln2_b', 'ffn2_w_gate', 'ffn2_w_up', 'ffn2_w_down', 'ln3_g', 'ln3_b']
TWIN_DIFF_INPUT = 'x'
TWIN_INPUTS = ['x', 'meta_tokens', 'ffn1_w_gate', 'ffn1_w_up', 'ffn1_w_down', 'ln1_g', 'ln1_b', 'w_in', 'w_gate_up', 'b_gate', 'w_pool', 'pool_scale', 'gla_norm_g', 'w_out', 'ln2_g', 'ln2_b', 'ffn2_w_gate', 'ffn2_w_up', 'ffn2_w_down', 'ln3_g', 'ln3_b', 'loss_target', 'm_meta_tokens', 'm_ffn1_w_gate', 'm_ffn1_w_up', 'm_ffn1_w_down', 'm_ln1_g', 'm_ln1_b', 'm_w_in', 'm_w_gate_up', 'm_b_gate', 'm_w_pool', 'm_pool_scale', 'm_gla_norm_g', 'm_w_out', 'm_ln2_g', 'm_ln2_b', 'm_ffn2_w_gate', 'm_ffn2_w_up', 'm_ffn2_w_down', 'm_ln3_g', 'm_ln3_b', 'v_meta_tokens', 'v_ffn1_w_gate', 'v_ffn1_w_up', 'v_ffn1_w_down', 'v_ln1_g', 'v_ln1_b', 'v_w_in', 'v_w_gate_up', 'v_b_gate', 'v_w_pool', 'v_pool_scale', 'v_gla_norm_g', 'v_w_out', 'v_ln2_g', 'v_ln2_b', 'v_ffn2_w_gate', 'v_ffn2_w_up', 'v_ffn2_w_down', 'v_ln3_g', 'v_ln3_b']
TWIN_OUTPUTS = ['loss', 'grad_x', 'grad_meta_tokens', 'grad_ffn1_w_gate', 'grad_ffn1_w_up', 'grad_ffn1_w_down', 'grad_ln1_g', 'grad_ln1_b', 'grad_w_in', 'grad_w_gate_up', 'grad_b_gate', 'grad_w_pool', 'grad_pool_scale', 'grad_gla_norm_g', 'grad_w_out', 'grad_ln2_g', 'grad_ln2_b', 'grad_ffn2_w_gate', 'grad_ffn2_w_up', 'grad_ffn2_w_down', 'grad_ln3_g', 'grad_ln3_b', 'delta_meta_tokens', 'delta_ffn1_w_gate', 'delta_ffn1_w_up', 'delta_ffn1_w_down', 'delta_ln1_g', 'delta_ln1_b', 'delta_w_in', 'delta_w_gate_up', 'delta_b_gate', 'delta_w_pool', 'delta_pool_scale', 'delta_gla_norm_g', 'delta_w_out', 'delta_ln2_g', 'delta_ln2_b', 'delta_ffn2_w_gate', 'delta_ffn2_w_up', 'delta_ffn2_w_down', 'delta_ln3_g', 'delta_ln3_b', 'new_m_meta_tokens', 'new_m_ffn1_w_gate', 'new_m_ffn1_w_up', 'new_m_ffn1_w_down', 'new_m_ln1_g', 'new_m_ln1_b', 'new_m_w_in', 'new_m_w_gate_up', 'new_m_b_gate', 'new_m_w_pool', 'new_m_pool_scale', 'new_m_gla_norm_g', 'new_m_w_out', 'new_m_ln2_g', 'new_m_ln2_b', 'new_m_ffn2_w_gate', 'new_m_ffn2_w_up', 'new_m_ffn2_w_down', 'new_m_ln3_g', 'new_m_ln3_b', 'new_v_meta_tokens', 'new_v_ffn1_w_gate', 'new_v_ffn1_w_up', 'new_v_ffn1_w_down', 'new_v_ln1_g', 'new_v_ln1_b', 'new_v_w_in', 'new_v_w_gate_up', 'new_v_b_gate', 'new_v_w_pool', 'new_v_pool_scale', 'new_v_gla_norm_g', 'new_v_w_out', 'new_v_ln2_g', 'new_v_ln2_b', 'new_v_ffn2_w_gate', 'new_v_ffn2_w_up', 'new_v_ffn2_w_down', 'new_v_ln3_g', 'new_v_ln3_b']
TWIN_LEAF_KINDS = {'loss': 'loss', 'grad_x': 'grad_x', 'grad_meta_tokens': 'grad_w', 'grad_ffn1_w_gate': 'grad_w', 'grad_ffn1_w_up': 'grad_w', 'grad_ffn1_w_down': 'grad_w', 'grad_ln1_g': 'grad_w', 'grad_ln1_b': 'grad_w', 'grad_w_in': 'grad_w', 'grad_w_gate_up': 'grad_w', 'grad_b_gate': 'grad_w', 'grad_w_pool': 'grad_w', 'grad_pool_scale': 'grad_w', 'grad_gla_norm_g': 'grad_w', 'grad_w_out': 'grad_w', 'grad_ln2_g': 'grad_w', 'grad_ln2_b': 'grad_w', 'grad_ffn2_w_gate': 'grad_w', 'grad_ffn2_w_up': 'grad_w', 'grad_ffn2_w_down': 'grad_w', 'grad_ln3_g': 'grad_w', 'grad_ln3_b': 'grad_w', 'delta_meta_tokens': 'delta_w', 'delta_ffn1_w_gate': 'delta_w', 'delta_ffn1_w_up': 'delta_w', 'delta_ffn1_w_down': 'delta_w', 'delta_ln1_g': 'delta_w', 'delta_ln1_b': 'delta_w', 'delta_w_in': 'delta_w', 'delta_w_gate_up': 'delta_w', 'delta_b_gate': 'delta_w', 'delta_w_pool': 'delta_w', 'delta_pool_scale': 'delta_w', 'delta_gla_norm_g': 'delta_w', 'delta_w_out': 'delta_w', 'delta_ln2_g': 'delta_w', 'delta_ln2_b': 'delta_w', 'delta_ffn2_w_gate': 'delta_w', 'delta_ffn2_w_up': 'delta_w', 'delta_ffn2_w_down': 'delta_w', 'delta_ln3_g': 'delta_w', 'delta_ln3_b': 'delta_w', 'new_m_meta_tokens': 'new_m', 'new_m_ffn1_w_gate': 'new_m', 'new_m_ffn1_w_up': 'new_m', 'new_m_ffn1_w_down': 'new_m', 'new_m_ln1_g': 'new_m', 'new_m_ln1_b': 'new_m', 'new_m_w_in': 'new_m', 'new_m_w_gate_up': 'new_m', 'new_m_b_gate': 'new_m', 'new_m_w_pool': 'new_m', 'new_m_pool_scale': 'new_m', 'new_m_gla_norm_g': 'new_m', 'new_m_w_out': 'new_m', 'new_m_ln2_g': 'new_m', 'new_m_ln2_b': 'new_m', 'new_m_ffn2_w_gate': 'new_m', 'new_m_ffn2_w_up': 'new_m', 'new_m_ffn2_w_down': 'new_m', 'new_m_ln3_g': 'new_m', 'new_m_ln3_b': 'new_m', 'new_v_meta_tokens': 'new_v', 'new_v_ffn1_w_gate': 'new_v', 'new_v_ffn1_w_up': 'new_v', 'new_v_ffn1_w_down': 'new_v', 'new_v_ln1_g': 'new_v', 'new_v_ln1_b': 'new_v', 'new_v_w_in': 'new_v', 'new_v_w_gate_up': 'new_v', 'new_v_b_gate': 'new_v', 'new_v_w_pool': 'new_v', 'new_v_pool_scale': 'new_v', 'new_v_gla_norm_g': 'new_v', 'new_v_w_out': 'new_v', 'new_v_ln2_g': 'new_v', 'new_v_ln2_b': 'new_v', 'new_v_ffn2_w_gate': 'new_v', 'new_v_ffn2_w_up': 'new_v', 'new_v_ffn2_w_down': 'new_v', 'new_v_ln3_g': 'new_v', 'new_v_ln3_b': 'new_v'}


def _forward(args):
    return _fwd_reference(*[args[k] for k in FWD_PARAMS])


def _output_shape():
    def fwd():
        inp = _fwd_setup_inputs(0)
        return _fwd_reference(*[inp[k] for k in FWD_PARAMS])
    out = _jax.eval_shape(fwd)
    return out.shape, out.dtype

N_MICROBATCH = 1
ADAM_LR = 0.001
ADAM_B1 = 0.9
ADAM_B2 = 0.999
ADAM_EPS = 1e-08
ADAM_WD = 0.01
ADAM_STEP = 10
PER_EXAMPLE_BATCH_AXIS = {'x': 0, 'loss_target': 0}
SHARED_INPUTS = []
_WEIGHT_DTYPES = {'meta_tokens': _jnp.float32, 'ffn1_w_gate': _jnp.float32, 'ffn1_w_up': _jnp.float32, 'ffn1_w_down': _jnp.float32, 'ln1_g': _jnp.float32, 'ln1_b': _jnp.float32, 'w_in': _jnp.float32, 'w_gate_up': _jnp.float32, 'b_gate': _jnp.float32, 'w_pool': _jnp.float32, 'pool_scale': _jnp.float32, 'gla_norm_g': _jnp.float32, 'w_out': _jnp.float32, 'ln2_g': _jnp.float32, 'ln2_b': _jnp.float32, 'ffn2_w_gate': _jnp.float32, 'ffn2_w_up': _jnp.float32, 'ffn2_w_down': _jnp.float32, 'ln3_g': _jnp.float32, 'ln3_b': _jnp.float32}
MOMENT_SCALE = {'meta_tokens': 3.779455e-03, 'ffn1_w_gate': 1.738770e-02, 'ffn1_w_up': 1.683033e-02, 'ffn1_w_down': 5.582430e-02, 'ln1_g': 1.866194e+00, 'ln1_b': 9.545578e-01, 'w_in': 6.650889e-02, 'w_gate_up': 8.441465e-03, 'b_gate': 3.336287e-02, 'w_pool': 7.884542e-02, 'pool_scale': 7.834585e-02, 'gla_norm_g': 5.390698e-02, 'w_out': 1.342801e-01, 'ln2_g': 2.051221e+00, 'ln2_b': 1.011693e+00, 'ffn2_w_gate': 1.673368e-02, 'ffn2_w_up': 1.625165e-02, 'ffn2_w_down': 5.382275e-02, 'ln3_g': 4.542175e+01, 'ln3_b': 2.047702e+00}


def _to_microbatches(a, axis):
    t = _jnp.moveaxis(a, axis, 0)
    t = t.reshape((N_MICROBATCH, t.shape[0] // N_MICROBATCH) + t.shape[1:])
    return _jnp.moveaxis(t, 1, axis + 1)


def setup_inputs(seed: int = 0) -> dict:
    inp = _fwd_setup_inputs(seed)
    key = _jax.random.fold_in(_jax.random.key(seed), 7919)
    shape, _ = _output_shape()
    out = dict(inp)
    out["loss_target"] = _jax.random.normal(_jax.random.fold_in(key, 0), shape, _jnp.float32)
    for i, name in enumerate(TWIN_WEIGHTS):
        w = inp[name].astype(_jnp.float32)
        if MOMENT_SCALE is None:
            s = _jnp.sqrt(_jnp.mean(_jnp.square(w)) + 1e-30)
        else:
            s = MOMENT_SCALE[name]
        km, kv = _jax.random.split(_jax.random.fold_in(key, i + 1))
        out[name] = w
        out["m_" + name] = s * _jax.random.normal(km, w.shape, _jnp.float32)
        out["v_" + name] = (s * s) * _jax.random.uniform(kv, w.shape, _jnp.float32, 0.5, 1.5)
    if N_MICROBATCH > 1:
        for name, axis in PER_EXAMPLE_BATCH_AXIS.items():
            out[name] = _to_microbatches(out[name], axis)
    return {'x': out['x'], 'meta_tokens': out['meta_tokens'], 'ffn1_w_gate': out['ffn1_w_gate'], 'ffn1_w_up': out['ffn1_w_up'], 'ffn1_w_down': out['ffn1_w_down'], 'ln1_g': out['ln1_g'], 'ln1_b': out['ln1_b'], 'w_in': out['w_in'], 'w_gate_up': out['w_gate_up'], 'b_gate': out['b_gate'], 'w_pool': out['w_pool'], 'pool_scale': out['pool_scale'], 'gla_norm_g': out['gla_norm_g'], 'w_out': out['w_out'], 'ln2_g': out['ln2_g'], 'ln2_b': out['ln2_b'], 'ffn2_w_gate': out['ffn2_w_gate'], 'ffn2_w_up': out['ffn2_w_up'], 'ffn2_w_down': out['ffn2_w_down'], 'ln3_g': out['ln3_g'], 'ln3_b': out['ln3_b'], 'loss_target': out['loss_target'], 'm_meta_tokens': out['m_meta_tokens'], 'm_ffn1_w_gate': out['m_ffn1_w_gate'], 'm_ffn1_w_up': out['m_ffn1_w_up'], 'm_ffn1_w_down': out['m_ffn1_w_down'], 'm_ln1_g': out['m_ln1_g'], 'm_ln1_b': out['m_ln1_b'], 'm_w_in': out['m_w_in'], 'm_w_gate_up': out['m_w_gate_up'], 'm_b_gate': out['m_b_gate'], 'm_w_pool': out['m_w_pool'], 'm_pool_scale': out['m_pool_scale'], 'm_gla_norm_g': out['m_gla_norm_g'], 'm_w_out': out['m_w_out'], 'm_ln2_g': out['m_ln2_g'], 'm_ln2_b': out['m_ln2_b'], 'm_ffn2_w_gate': out['m_ffn2_w_gate'], 'm_ffn2_w_up': out['m_ffn2_w_up'], 'm_ffn2_w_down': out['m_ffn2_w_down'], 'm_ln3_g': out['m_ln3_g'], 'm_ln3_b': out['m_ln3_b'], 'v_meta_tokens': out['v_meta_tokens'], 'v_ffn1_w_gate': out['v_ffn1_w_gate'], 'v_ffn1_w_up': out['v_ffn1_w_up'], 'v_ffn1_w_down': out['v_ffn1_w_down'], 'v_ln1_g': out['v_ln1_g'], 'v_ln1_b': out['v_ln1_b'], 'v_w_in': out['v_w_in'], 'v_w_gate_up': out['v_w_gate_up'], 'v_b_gate': out['v_b_gate'], 'v_w_pool': out['v_w_pool'], 'v_pool_scale': out['v_pool_scale'], 'v_gla_norm_g': out['v_gla_norm_g'], 'v_w_out': out['v_w_out'], 'v_ln2_g': out['v_ln2_g'], 'v_ln2_b': out['v_ln2_b'], 'v_ffn2_w_gate': out['v_ffn2_w_gate'], 'v_ffn2_w_up': out['v_ffn2_w_up'], 'v_ffn2_w_down': out['v_ffn2_w_down'], 'v_ln3_g': out['v_ln3_g'], 'v_ln3_b': out['v_ln3_b']}


def _loss(weights, diff, rest, loss_target):
    with _jax.named_scope("forward"):
        args = {**rest, TWIN_DIFF_INPUT: diff, **{k: w.astype(_WEIGHT_DTYPES[k]) for k, w in weights.items()}}
        y = _forward(args)
    with _jax.named_scope("loss_head"):
        err = _jnp.square(y.astype(_jnp.float32) - loss_target)
        return 0.5 * _jnp.sum(_jnp.mean(err, axis=-1)) if err.ndim else 0.5 * err


def _adamw(w, g, m, v):
    m = ADAM_B1 * m + (1.0 - ADAM_B1) * g
    v = ADAM_B2 * v + (1.0 - ADAM_B2) * _jnp.square(g)
    m_hat = m / (1.0 - ADAM_B1 ** ADAM_STEP)
    v_hat = v / (1.0 - ADAM_B2 ** ADAM_STEP)
    delta = -ADAM_LR * (m_hat / (_jnp.sqrt(v_hat) + ADAM_EPS) + ADAM_WD * w)
    return delta, m, v


def reference(x, meta_tokens, ffn1_w_gate, ffn1_w_up, ffn1_w_down, ln1_g, ln1_b, w_in, w_gate_up, b_gate, w_pool, pool_scale, gla_norm_g, w_out, ln2_g, ln2_b, ffn2_w_gate, ffn2_w_up, ffn2_w_down, ln3_g, ln3_b, loss_target, m_meta_tokens, m_ffn1_w_gate, m_ffn1_w_up, m_ffn1_w_down, m_ln1_g, m_ln1_b, m_w_in, m_w_gate_up, m_b_gate, m_w_pool, m_pool_scale, m_gla_norm_g, m_w_out, m_ln2_g, m_ln2_b, m_ffn2_w_gate, m_ffn2_w_up, m_ffn2_w_down, m_ln3_g, m_ln3_b, v_meta_tokens, v_ffn1_w_gate, v_ffn1_w_up, v_ffn1_w_down, v_ln1_g, v_ln1_b, v_w_in, v_w_gate_up, v_b_gate, v_w_pool, v_pool_scale, v_gla_norm_g, v_w_out, v_ln2_g, v_ln2_b, v_ffn2_w_gate, v_ffn2_w_up, v_ffn2_w_down, v_ln3_g, v_ln3_b):
    given = dict(x=x, meta_tokens=meta_tokens, ffn1_w_gate=ffn1_w_gate, ffn1_w_up=ffn1_w_up, ffn1_w_down=ffn1_w_down, ln1_g=ln1_g, ln1_b=ln1_b, w_in=w_in, w_gate_up=w_gate_up, b_gate=b_gate, w_pool=w_pool, pool_scale=pool_scale, gla_norm_g=gla_norm_g, w_out=w_out, ln2_g=ln2_g, ln2_b=ln2_b, ffn2_w_gate=ffn2_w_gate, ffn2_w_up=ffn2_w_up, ffn2_w_down=ffn2_w_down, ln3_g=ln3_g, ln3_b=ln3_b, loss_target=loss_target, m_meta_tokens=m_meta_tokens, m_ffn1_w_gate=m_ffn1_w_gate, m_ffn1_w_up=m_ffn1_w_up, m_ffn1_w_down=m_ffn1_w_down, m_ln1_g=m_ln1_g, m_ln1_b=m_ln1_b, m_w_in=m_w_in, m_w_gate_up=m_w_gate_up, m_b_gate=m_b_gate, m_w_pool=m_w_pool, m_pool_scale=m_pool_scale, m_gla_norm_g=m_gla_norm_g, m_w_out=m_w_out, m_ln2_g=m_ln2_g, m_ln2_b=m_ln2_b, m_ffn2_w_gate=m_ffn2_w_gate, m_ffn2_w_up=m_ffn2_w_up, m_ffn2_w_down=m_ffn2_w_down, m_ln3_g=m_ln3_g, m_ln3_b=m_ln3_b, v_meta_tokens=v_meta_tokens, v_ffn1_w_gate=v_ffn1_w_gate, v_ffn1_w_up=v_ffn1_w_up, v_ffn1_w_down=v_ffn1_w_down, v_ln1_g=v_ln1_g, v_ln1_b=v_ln1_b, v_w_in=v_w_in, v_w_gate_up=v_w_gate_up, v_b_gate=v_b_gate, v_w_pool=v_w_pool, v_pool_scale=v_pool_scale, v_gla_norm_g=v_gla_norm_g, v_w_out=v_w_out, v_ln2_g=v_ln2_g, v_ln2_b=v_ln2_b, v_ffn2_w_gate=v_ffn2_w_gate, v_ffn2_w_up=v_ffn2_w_up, v_ffn2_w_down=v_ffn2_w_down, v_ln3_g=v_ln3_g, v_ln3_b=v_ln3_b)
    weights = {n: given[n] for n in TWIN_WEIGHTS}
    shared = {n: given[n] for n in SHARED_INPUTS}
    per_example = {n: given[n] for n in ['x']}
    grad_fn = _jax.value_and_grad(_loss, argnums=(0, 1))

    def one_microbatch(ex, loss_target):
        ex = dict(ex)
        diff = ex.pop(TWIN_DIFF_INPUT)
        return grad_fn(weights, diff, {**shared, **ex}, loss_target)

    if N_MICROBATCH == 1:
        loss, (grad_w, grad_x) = one_microbatch(per_example, given["loss_target"])
    else:
        def body(carry, xs):
            loss_sum, grad_sum = carry
            l_k, (gw_k, gx_k) = one_microbatch(xs[0], xs[1])
            with _jax.named_scope("update"):
                return (loss_sum + l_k, _jax.tree.map(_jnp.add, grad_sum, gw_k)), gx_k

        init = (_jnp.zeros((), _jnp.float32), _jax.tree.map(_jnp.zeros_like, weights))
        (loss, grad_w), grad_x = _jax.lax.scan(body, init, (per_example, given["loss_target"]))
    with _jax.named_scope("update"):
        delta_w, new_m, new_v = {}, {}, {}
        for n in TWIN_WEIGHTS:
            delta_w[n], new_m[n], new_v[n] = _adamw(weights[n], grad_w[n], given["m_" + n], given["v_" + n])
    return (loss, grad_x, *[grad_w[n] for n in TWIN_WEIGHTS], *[delta_w[n] for n in TWIN_WEIGHTS],
            *[new_m[n] for n in TWIN_WEIGHTS], *[new_v[n] for n in TWIN_WEIGHTS])
```

```python
import functools

import jax
import jax.numpy as jnp
from jax import lax
from jax.experimental import pallas as pl
from jax.experimental.pallas import tpu as pltpu

F32 = jnp.float32
BF16 = jnp.bfloat16
MESH = pl.DeviceIdType.MESH

N_META = 16
POOL_WINDOWS = (2, 4, 8, 16)
POOL_HALO = 16
N_HEADS = 4
GLA_GATE_TEMP = 16.0
CHUNK = 64
LN_EPS = 1e-5
RMS_EPS = 1e-6
ADAM_LR = 0.001
ADAM_B1 = 0.9
ADAM_B2 = 0.999
ADAM_EPS = 1e-08
ADAM_WD = 0.01
ADAM_STEP = 10
LANE = 128
ROW_TILE = 640
N_SHARD = 4
N_DEV = 8

BIG = ("ffn1_w_gate", "ffn1_w_up", "ffn1_w_down", "w_in", "w_gate_up", "w_out",
       "ffn2_w_gate", "ffn2_w_up", "ffn2_w_down")
SMALL = ("ln1_g", "ln1_b", "b_gate", "w_pool", "pool_scale", "gla_norm_g", "ln2_g", "ln2_b", "ln3_g", "ln3_b")
WEIGHTS = ("meta_tokens", "ffn1_w_gate", "ffn1_w_up", "ffn1_w_down", "ln1_g", "ln1_b", "w_in", "w_gate_up",
           "b_gate", "w_pool", "pool_scale", "gla_norm_g", "w_out", "ln2_g", "ln2_b", "ffn2_w_gate",
           "ffn2_w_up", "ffn2_w_down", "ln3_g", "ln3_b")


def _tc_call(body, **kw):
    return pl.pallas_call(body, **kw)


def _comm_call(body, **kw):
    return pl.pallas_call(body, **kw)


def _seq(n):
    return pltpu.CompilerParams(dimension_semantics=("arbitrary",) * n)


def _mm(a, b):
    return jnp.dot(a.astype(BF16), b.astype(BF16), preferred_element_type=F32)


def _mm_nt(a, b):
    return lax.dot_general(a.astype(BF16), b.astype(BF16), (((1,), (1,)), ((), ())), preferred_element_type=F32)


def _mm_tn(a, b):
    return lax.dot_general(a.astype(BF16), b.astype(BF16), (((0,), (0,)), ((), ())), preferred_element_type=F32)


def _mm_f32(a, b):
    return jnp.dot(a, b, precision=lax.Precision.HIGHEST, preferred_element_type=F32)


def _row_tile(t):
    tm = min(ROW_TILE, t)
    while t % tm:
        tm -= LANE
    return tm


def _silu_parts(g):
    sg = jax.nn.sigmoid(g)
    return sg, g * sg


def _ln_stats(r):
    mu = jnp.mean(r, axis=-1, keepdims=True)
    rc = r - mu
    var = jnp.mean(rc * rc, axis=-1, keepdims=True)
    rs = lax.rsqrt(var + LN_EPS)
    return rc * rs, rs


def _ln_bwd(dy, xh, rs, gam):
    dyg = dy * gam
    c1 = jnp.mean(dyg, axis=-1, keepdims=True)
    c2 = jnp.mean(dyg * xh, axis=-1, keepdims=True)
    return rs * (dyg - c1 - xh * c2)


def _ffn_fwd(xin, gam_in, bet_in, wg, wu, wd, layer, ln_g, ln_b, alpha, name):
    t, d = xin.shape
    nj, _, _, tf = wg.shape
    tm = _row_tile(t)

    def body(x_ref, gi_ref, bi_ref, wg_ref, wu_ref, wd_ref, xhat_ref, rstd_ref, hb_ref, go_ref, uo_ref, acc, hbs):
        j = pl.program_id(1)

        @pl.when(j == 0)
        def _():
            hb = (x_ref[...] * gi_ref[...] + bi_ref[...]).astype(BF16)
            hbs[...] = hb
            hb_ref[...] = hb
            acc[...] = jnp.zeros_like(acc)

        hb = hbs[...]
        g = jnp.dot(hb, wg_ref[...], preferred_element_type=F32)
        u = jnp.dot(hb, wu_ref[...], preferred_element_type=F32)
        _, sl = _silu_parts(g)
        go_ref[...] = g.astype(BF16)
        uo_ref[...] = u.astype(BF16)
        acc[...] += jnp.dot((sl * u).astype(BF16), wd_ref[...], preferred_element_type=F32)

        @pl.when(j == nj - 1)
        def _():
            h = x_ref[...] * gi_ref[...] + bi_ref[...]
            xhat, rs = _ln_stats(alpha * h + 0.5 * acc[...])
            xhat_ref[...] = xhat
            rstd_ref[...] = rs

    row = lambda i, j: (i, 0)
    vec = pl.BlockSpec((1, d), lambda i, j: (0, 0))
    return _tc_call(
        body, name=name, grid=(t // tm, nj),
        in_specs=[pl.BlockSpec((tm, d), row), vec, vec,
                  pl.BlockSpec((None, None, d, tf), lambda i, j: (j, layer, 0, 0)),
                  pl.BlockSpec((None, None, d, tf), lambda i, j: (j, layer, 0, 0)),
                  pl.BlockSpec((None, None, tf, d), lambda i, j: (j, layer, 0, 0))],
        out_specs=[pl.BlockSpec((tm, d), row), pl.BlockSpec((tm, 1), row), pl.BlockSpec((tm, d), row),
                   pl.BlockSpec((None, tm, tf), lambda i, j: (j, i, 0)),
                   pl.BlockSpec((None, tm, tf), lambda i, j: (j, i, 0))],
        out_shape=[jax.ShapeDtypeStruct((t, d), F32), jax.ShapeDtypeStruct((t, 1), F32),
                   jax.ShapeDtypeStruct((t, d), BF16), jax.ShapeDtypeStruct((nj, t, tf), BF16),
                   jax.ShapeDtypeStruct((nj, t, tf), BF16)],
        scratch_shapes=[pltpu.VMEM((tm, d), F32), pltpu.VMEM((tm, d), BF16)],
        compiler_params=_seq(2),
    )(xin, gam_in, bet_in, wg, wu, wd)


def _ffn_bwd(dh, xhat, rstd, ln_g, gb, ub, wg, wu, wd, layer, alpha, name):
    t, d = dh.shape
    nj, _, _, tf = wg.shape
    tm = _row_tile(t)

    def body(dh_ref, xh_ref, rs_ref, g_ref, gb_ref, ub_ref, wg_ref, wu_ref, wd_ref,
             dhin_ref, df_ref, dg_ref, du_ref, act_ref, dgam_ref, dbet_ref, dr_s, df_s, acc):
        i = pl.program_id(0)
        j = pl.program_id(1)

        @pl.when(j == 0)
        def _():
            dy = dh_ref[...]
            xh = xh_ref[...]
            dr = _ln_bwd(dy, xh, rs_ref[...], g_ref[...])
            dr_s[...] = dr
            dfb = (0.5 * dr).astype(BF16)
            df_s[...] = dfb
            df_ref[...] = dfb
            acc[...] = jnp.zeros_like(acc)

            @pl.when(i == 0)
            def _():
                dgam_ref[...] = jnp.zeros_like(dgam_ref)
                dbet_ref[...] = jnp.zeros_like(dbet_ref)

            dgam_ref[...] += jnp.sum(dy * xh, axis=0, keepdims=True)
            dbet_ref[...] += jnp.sum(dy, axis=0, keepdims=True)

        dact = _mm_nt(df_s[...], wd_ref[...])
        g = gb_ref[...].astype(F32)
        u = ub_ref[...].astype(F32)
        sg, sl = _silu_parts(g)
        dg = (dact * u * (sg * (1.0 + g * (1.0 - sg)))).astype(BF16)
        du = (dact * sl).astype(BF16)
        dg_ref[...] = dg
        du_ref[...] = du
        act_ref[...] = (sl * u).astype(BF16)
        acc[...] += _mm_nt(dg, wg_ref[...]) + _mm_nt(du, wu_ref[...])

        @pl.when(j == nj - 1)
        def _():
            dhin_ref[...] = alpha * dr_s[...] + acc[...]

    row = lambda i, j: (i, 0)
    col = pl.BlockSpec((None, tm, tf), lambda i, j: (j, i, 0))
    vec = pl.BlockSpec((1, d), lambda i, j: (0, 0))
    ff = jax.ShapeDtypeStruct((nj, t, tf), BF16)
    return _tc_call(
        body, name=name, grid=(t // tm, nj),
        in_specs=[pl.BlockSpec((tm, d), row), pl.BlockSpec((tm, d), row), pl.BlockSpec((tm, 1), row), vec,
                  col, col,
                  pl.BlockSpec((None, None, d, tf), lambda i, j: (j, layer, 0, 0)),
                  pl.BlockSpec((None, None, d, tf), lambda i, j: (j, layer, 0, 0)),
                  pl.BlockSpec((None, None, tf, d), lambda i, j: (j, layer, 0, 0))],
        out_specs=[pl.BlockSpec((tm, d), row), pl.BlockSpec((tm, d), row),
                   col, col, col, vec, vec],
        out_shape=[jax.ShapeDtypeStruct((t, d), F32), jax.ShapeDtypeStruct((t, d), BF16), ff, ff, ff,
                   jax.ShapeDtypeStruct((1, d), F32), jax.ShapeDtypeStruct((1, d), F32)],
        scratch_shapes=[pltpu.VMEM((tm, d), F32), pltpu.VMEM((tm, d), BF16), pltpu.VMEM((tm, d), F32)],
        compiler_params=_seq(2),
    )(dh, xhat, rstd, ln_g, gb, ub, wg, wu, wd)


def _wgrad(a, b, tmm, tn, shard_major, name):
    t = a.shape[-2]
    m = a.shape[-1] * (a.shape[0] if a.ndim == 3 else 1)
    n = b.shape[-1] * (b.shape[0] if b.ndim == 3 else 1)
    tk = _row_tile(t)
    nk = t // tk

    def body(a_ref, b_ref, o_ref, acc):
        k = pl.program_id(2)

        @pl.when(k == 0)
        def _():
            acc[...] = jnp.zeros_like(acc)

        acc[...] += _mm_tn(a_ref[...], b_ref[...])

        @pl.when(k == nk - 1)
        def _():
            o_ref[...] = acc[...].astype(o_ref.dtype)

    if shard_major:
        out_spec = pl.BlockSpec((None, tmm, tn), lambda i, j, k: (j, i, 0))
        out_shape = jax.ShapeDtypeStruct((n // tn, m, tn), BF16)
    else:
        out_spec = pl.BlockSpec((tmm, tn), lambda i, j, k: (i, j))
        out_shape = jax.ShapeDtypeStruct((m, n), BF16)
    return _tc_call(
        body, name=name, grid=(m // tmm, n // tn, nk),
        in_specs=[pl.BlockSpec((None, tk, tmm), lambda i, j, k: (i, k, 0)) if a.ndim == 3
                  else pl.BlockSpec((tk, tmm), lambda i, j, k: (k, i)),
                  pl.BlockSpec((None, tk, tn), lambda i, j, k: (j, k, 0)) if b.ndim == 3
                  else pl.BlockSpec((tk, tn), lambda i, j, k: (k, j))],
        out_specs=out_spec, out_shape=out_shape,
        scratch_shapes=[pltpu.VMEM((tmm, tn), F32)],
        compiler_params=_seq(3),
    )(a, b)


def _inproj_fwd(xhat, gam, bet, w_main, w_lr, wgu, b_gate, widths, name):
    t, d = xhat.shape
    tm = _row_tile(t)
    kw = wgu.shape[1]
    offs = [0]
    for w in widths:
        offs.append(offs[-1] + w)

    def body(x_ref, g_ref, b_ref, wm_ref, wl_ref, wgu_ref, bg_ref, *outs):
        piece_refs, (zg_ref, la_ref, hb_ref) = outs[:len(widths)], outs[len(widths):]
        hb = (x_ref[...] * g_ref[...] + b_ref[...]).astype(BF16)
        hb_ref[...] = hb
        for p, ref in enumerate(piece_refs):
            ref[...] = jnp.dot(hb, wm_ref[:, offs[p]:offs[p + 1]], preferred_element_type=F32)
        zg = jnp.dot(hb, wl_ref[...], preferred_element_type=F32)
        zg_ref[...] = zg
        logit = _mm(zg, wgu_ref[...]) + bg_ref[...]
        la_ref[...] = (jnp.minimum(logit, 0.0) - jnp.log(1.0 + jnp.exp(-jnp.abs(logit)))) * (1.0 / GLA_GATE_TEMP)

    row = lambda i: (i, 0)
    full = lambda a: pl.BlockSpec(a.shape, lambda i: (0,) * a.ndim)
    out_w = list(widths) + [LANE, kw]
    return _tc_call(
        body, name=name, grid=(t // tm,),
        in_specs=[pl.BlockSpec((tm, d), row), full(gam), full(bet), full(w_main), full(w_lr), full(wgu), full(b_gate)],
        out_specs=[pl.BlockSpec((tm, w), row) for w in out_w] + [pl.BlockSpec((tm, d), row)],
        out_shape=[jax.ShapeDtypeStruct((t, w), F32) for w in out_w] + [jax.ShapeDtypeStruct((t, d), BF16)],
        compiler_params=_seq(1),
    )(xhat, gam, bet, w_main, w_lr, wgu, b_gate)


def _inproj_bwd(dh_part, pieces, dzg, w_main, w_lr, name):
    t, d = dh_part.shape
    tm = _row_tile(t)
    widths = [p.shape[1] for p in pieces]
    offs = [0]
    for w in widths:
        offs.append(offs[-1] + w)

    def body(*refs):
        dhp_ref = refs[0]
        p_refs = refs[1:1 + len(widths)]
        dzg_ref, wm_ref, wl_ref, dh_ref, dz_ref = refs[1 + len(widths):]
        acc = dhp_ref[...] + _mm_nt(dzg_ref[...], wl_ref[...])
        for p, ref in enumerate(p_refs):
            v = ref[...]
            dz_ref[:, offs[p]:offs[p + 1]] = v
            acc += _mm_nt(v, wm_ref[:, offs[p]:offs[p + 1]])
        dh_ref[...] = acc

    row = lambda i: (i, 0)
    full = lambda a: pl.BlockSpec(a.shape, lambda i: (0,) * a.ndim)
    return _tc_call(
        body, name=name, grid=(t // tm,),
        in_specs=[pl.BlockSpec((tm, d), row)] + [pl.BlockSpec((tm, w), row) for w in widths]
                 + [pl.BlockSpec((tm, LANE), row), full(w_main), full(w_lr)],
        out_specs=[pl.BlockSpec((tm, d), row), pl.BlockSpec((tm, offs[-1]), row)],
        out_shape=[jax.ShapeDtypeStruct((t, d), F32), jax.ShapeDtypeStruct((t, offs[-1]), BF16)],
        compiler_params=_seq(1),
    )(dh_part, *pieces, dzg, w_main, w_lr)


def _pool_cnt(tile, tm, w):
    t = tile * tm + lax.broadcasted_iota(jnp.int32, (tm, 1), 0)
    return jnp.minimum(t + 1, w).astype(F32)


def _pool_fwd(u, wp, scale, name):
    t, pw = u.shape
    tm = _row_tile(t)
    gd = wp.shape[1]

    def body(u_ref, wp_ref, sc_ref, y_ref, p_ref, ext):
        i = pl.program_id(0)

        @pl.when(i == 0)
        def _():
            ext[0:POOL_HALO, :] = jnp.zeros((POOL_HALO, pw), F32)

        ext[POOL_HALO:POOL_HALO + tm, :] = u_ref[...]
        for gi, w in enumerate(POOL_WINDOWS):
            cols = slice(gi * gd, (gi + 1) * gd)
            s = ext[pl.ds(POOL_HALO, tm), cols]
            tot = s
            for back in range(1, w):
                tot = tot + ext[pl.ds(POOL_HALO - back, tm), cols]
            p = (tot / _pool_cnt(i, tm, w) - s).astype(BF16)
            p_ref[:, cols] = p
            y_ref[:, cols] = (jnp.dot(p, wp_ref[gi], preferred_element_type=F32) * sc_ref[:, cols]).astype(BF16)
        ext[0:POOL_HALO, :] = ext[tm:tm + POOL_HALO, :]

    row = lambda i: (i, 0)
    return _tc_call(
        body, name=name, grid=(t // tm,),
        in_specs=[pl.BlockSpec((tm, pw), row), pl.BlockSpec(wp.shape, lambda i: (0, 0, 0)),
                  pl.BlockSpec((1, pw), lambda i: (0, 0))],
        out_specs=[pl.BlockSpec((tm, pw), row), pl.BlockSpec((tm, pw), row)],
        out_shape=[jax.ShapeDtypeStruct((t, pw), BF16), jax.ShapeDtypeStruct((t, pw), BF16)],
        scratch_shapes=[pltpu.VMEM((tm + POOL_HALO, pw), F32)],
        compiler_params=_seq(1),
    )(u, wp, scale)


def _pool_bwd(dy, pb, wp, scale, name):
    t, pw = dy.shape
    tm = _row_tile(t)
    nt = t // tm
    gd = wp.shape[1]

    def body(dy_ref, p_ref, wp_ref, sc_ref, du_ref, dwp_ref, dsc_ref, ext):
        i = pl.program_id(0)
        tile = nt - 1 - i

        @pl.when(i == 0)
        def _():
            ext[tm:tm + POOL_HALO, :] = jnp.zeros((POOL_HALO, pw), F32)
            dwp_ref[...] = jnp.zeros_like(dwp_ref)
            dsc_ref[...] = jnp.zeros_like(dsc_ref)

        dps = []
        for gi, w in enumerate(POOL_WINDOWS):
            cols = slice(gi * gd, (gi + 1) * gd)
            dyv = dy_ref[:, cols]
            p = p_ref[:, cols]
            dpre = (dyv * sc_ref[:, cols]).astype(BF16)
            dsc_ref[:, cols] += jnp.sum(dyv * jnp.dot(p, wp_ref[gi], preferred_element_type=F32), axis=0, keepdims=True)
            dwp_ref[gi] += _mm_tn(p, dpre)
            dp = _mm_nt(dpre, wp_ref[gi])
            dps.append(dp)
            ext[0:tm, cols] = dp / _pool_cnt(tile, tm, w)
        for gi, w in enumerate(POOL_WINDOWS):
            cols = slice(gi * gd, (gi + 1) * gd)
            tot = ext[pl.ds(0, tm), cols]
            for fwd in range(1, w):
                tot = tot + ext[pl.ds(fwd, tm), cols]
            du_ref[:, cols] = (tot - dps[gi]).astype(BF16)
        ext[tm:tm + POOL_HALO, :] = ext[0:POOL_HALO, :]

    row = lambda i: (nt - 1 - i, 0)
    return _tc_call(
        body, name=name, grid=(nt,),
        in_specs=[pl.BlockSpec((tm, pw), row), pl.BlockSpec((tm, pw), row),
                  pl.BlockSpec(wp.shape, lambda i: (0, 0, 0)), pl.BlockSpec((1, pw), lambda i: (0, 0))],
        out_specs=[pl.BlockSpec((tm, pw), row), pl.BlockSpec(wp.shape, lambda i: (0, 0, 0)),
                   pl.BlockSpec((1, pw), lambda i: (0, 0))],
        out_shape=[jax.ShapeDtypeStruct((t, pw), BF16), jax.ShapeDtypeStruct(wp.shape, F32),
                   jax.ShapeDtypeStruct((1, pw), F32)],
        scratch_shapes=[pltpu.VMEM((tm + POOL_HALO, pw), F32)],
        compiler_params=_seq(1),
    )(dy, pb, wp, scale)


def _gla_masks(kw, vw):
    dk, dv = kw // N_HEADS, vw // N_HEADS
    lane_k = lax.broadcasted_iota(jnp.int32, (1, kw), 1)
    lane_v = lax.broadcasted_iota(jnp.int32, (1, vw), 1)
    hk = [((lane_k >= h * dk) & (lane_k < (h + 1) * dk)).astype(F32) for h in range(N_HEADS)]
    hv = [((lane_v >= h * dv) & (lane_v < (h + 1) * dv)).astype(F32) for h in range(N_HEADS)]
    r = lax.broadcasted_iota(jnp.int32, (CHUNK, CHUNK), 0)
    c = lax.broadcasted_iota(jnp.int32, (CHUNK, CHUNK), 1)
    tril = r >= c
    rs = lax.broadcasted_iota(jnp.int32, (N_HEADS * CHUNK, CHUNK), 0) & (CHUNK - 1)
    stril = rs >= lax.broadcasted_iota(jnp.int32, (N_HEADS * CHUNK, CHUNK), 1)
    return hk, hv, tril, stril


def _block_diag(x, hk, dv):
    return jnp.concatenate([x[h * dv:(h + 1) * dv, :] * hk[h] for h in range(N_HEADS)], axis=0)


def _gla_fwd(q, k, v, loga, r, gnorm, name):
    t, kw = q.shape
    vw = v.shape[1]
    dk, dv = kw // N_HEADS, vw // N_HEADS
    tm = _row_tile(t)
    nc = tm // CHUNK
    qscale = dk ** -0.5

    def body(q_ref, k_ref, v_ref, la_ref, r_ref, gn_ref, o_ref, y_ref, sall_ref, st):
        @pl.when(pl.program_id(0) == 0)
        def _():
            st[...] = jnp.zeros_like(st)

        hk, hv, tril, stril = _gla_masks(kw, vw)
        trif = tril.astype(F32)

        def chunk(c, carry):
            rows = pl.ds(pl.multiple_of(c * CHUNK, CHUNK), CHUNK)
            la = la_ref[rows, :]
            b = _mm_f32(trif, la)
            bl = jnp.sum(la, axis=0, keepdims=True)
            qb = q_ref[rows, :] * (qscale * jnp.exp(b))
            kk = k_ref[rows, :]
            kb = kk * jnp.exp(-b)
            kl = kk * jnp.exp(bl - b)
            vv = v_ref[rows, :]
            s_t = st[...]
            compact = s_t[0:dv, :]
            for h in range(1, N_HEADS):
                compact = compact + s_t[h * dv:(h + 1) * dv, :]
            sall_ref[c] = compact
            qx = jnp.concatenate([qb * hk[h] for h in range(N_HEADS)], axis=0)
            a = jnp.where(stril, _mm_nt(qx, kb), 0.0).astype(BF16)
            o_inter = _mm_nt(qb, s_t)
            for h in range(N_HEADS):
                vs = slice(h * dv, (h + 1) * dv)
                o_ref[rows, vs] = o_inter[:, vs] + _mm(a[h * CHUNK:(h + 1) * CHUNK, :], vv[:, vs])
            st[...] = s_t * jnp.exp(bl) + _block_diag(_mm_tn(vv, kl), hk, dv)
            return carry

        lax.fori_loop(0, nc, chunk, 0)
        for h in range(N_HEADS):
            vs = slice(h * dv, (h + 1) * dv)
            oh = o_ref[:, vs]
            on = oh * lax.rsqrt(jnp.mean(oh * oh, axis=-1, keepdims=True) + RMS_EPS)
            _, sl = _silu_parts(r_ref[:, vs])
            y_ref[:, vs] = (on * gn_ref[:, vs] * sl).astype(BF16)

    row = lambda i: (i, 0)
    return _tc_call(
        body, name=name, grid=(t // tm,),
        in_specs=[pl.BlockSpec((tm, kw), row), pl.BlockSpec((tm, kw), row), pl.BlockSpec((tm, vw), row),
                  pl.BlockSpec((tm, kw), row), pl.BlockSpec((tm, vw), row), pl.BlockSpec((1, vw), lambda i: (0, 0))],
        out_specs=[pl.BlockSpec((tm, vw), row), pl.BlockSpec((tm, vw), row),
                   pl.BlockSpec((nc, dv, kw), lambda i: (i, 0, 0))],
        out_shape=[jax.ShapeDtypeStruct((t, vw), F32), jax.ShapeDtypeStruct((t, vw), BF16),
                   jax.ShapeDtypeStruct((t // CHUNK, dv, kw), F32)],
        scratch_shapes=[pltpu.VMEM((vw, kw), F32)],
        compiler_params=_seq(1),
    )(q, k, v, loga, r, gnorm)


def _gla_bwd(dy, o, r, gnorm, q, k, v, loga, zg, sall, wgu, name):
    t, kw = q.shape
    vw = v.shape[1]
    dk, dv = kw // N_HEADS, vw // N_HEADS
    tm = _row_tile(t)
    nt = t // tm
    nc = tm // CHUNK
    qscale = dk ** -0.5

    def body(dy_ref, o_ref, r_ref, gn_ref, q_ref, k_ref, v_ref, la_ref, zg_ref, sall_ref, wgu_ref,
             dq_ref, dk_ref, dv_ref, dr_ref, dzg_ref, dwgu_ref, dbg_ref, dgn_ref, dst, do_s):
        @pl.when(pl.program_id(0) == 0)
        def _():
            dst[...] = jnp.zeros_like(dst)
            dwgu_ref[...] = jnp.zeros_like(dwgu_ref)
            dbg_ref[...] = jnp.zeros_like(dbg_ref)
            dgn_ref[...] = jnp.zeros_like(dgn_ref)

        for h in range(N_HEADS):
            vs = slice(h * dv, (h + 1) * dv)
            oh = o_ref[:, vs]
            rinv = lax.rsqrt(jnp.mean(oh * oh, axis=-1, keepdims=True) + RMS_EPS)
            on = oh * rinv
            rr = r_ref[:, vs]
            sg, sl = _silu_parts(rr)
            dyv = dy_ref[:, vs]
            gn = gn_ref[:, vs]
            dgn_ref[:, vs] += jnp.sum(dyv * on * sl, axis=0, keepdims=True)
            dr_ref[:, vs] = (dyv * on * gn * (sg * (1.0 + rr * (1.0 - sg)))).astype(BF16)
            don = dyv * gn * sl
            do_s[:, vs] = rinv * (don - on * jnp.mean(don * on, axis=-1, keepdims=True))

        hk, hv, tril, stril = _gla_masks(kw, vw)
        trif = tril.astype(F32)
        triuf = (lax.broadcasted_iota(jnp.int32, (CHUNK, CHUNK), 0)
                 <= lax.broadcasted_iota(jnp.int32, (CHUNK, CHUNK), 1)).astype(F32)
        last_row = lax.broadcasted_iota(jnp.int32, (CHUNK, 1), 0) == CHUNK - 1

        def chunk(idx, carry):
            c = nc - 1 - idx
            rows = pl.ds(pl.multiple_of(c * CHUNK, CHUNK), CHUNK)
            la = la_ref[rows, :]
            b = _mm_f32(trif, la)
            bl = jnp.sum(la, axis=0, keepdims=True)
            eb = jnp.exp(b)
            enb = jnp.exp(-b)
            ebl = jnp.exp(bl - b)
            el = jnp.exp(bl)
            qb = q_ref[rows, :] * (qscale * eb)
            kk = k_ref[rows, :]
            kb = kk * enb
            kl = kk * ebl
            vv = v_ref[rows, :]
            do = do_s[rows, :]
            compact = sall_ref[c]
            s_t = jnp.concatenate([compact * hk[h] for h in range(N_HEADS)], axis=0)
            ds_t = dst[...]
            qx = jnp.concatenate([qb * hk[h] for h in range(N_HEADS)], axis=0)
            dox = jnp.concatenate([do * hv[h] for h in range(N_HEADS)], axis=0)
            a = jnp.where(stril, _mm_nt(qx, kb), 0.0).astype(BF16)
            da = jnp.where(stril, _mm_nt(dox, vv), 0.0).astype(BF16)
            dv_ref[rows, :] = (_mm_tn(a, dox) + _mm_nt(kl, ds_t)).astype(BF16)
            dak = _mm(da, kb)
            dqb = _mm(do, s_t)
            for h in range(N_HEADS):
                dqb = dqb + dak[h * CHUNK:(h + 1) * CHUNK, :] * hk[h]
            dkb = _mm_tn(da, qx)
            dkl = _mm(vv, ds_t)
            dbl = jnp.sum(dkl * kl, axis=0, keepdims=True) + el * jnp.sum(ds_t * s_t, axis=0, keepdims=True)
            dst[...] = ds_t * el + _block_diag(_mm_tn(do, qb), hk, dv)
            dq_ref[rows, :] = (dqb * (qscale * eb)).astype(BF16)
            dk_ref[rows, :] = (dkb * enb + dkl * ebl).astype(BF16)
            db = dqb * qb - dkb * kb - dkl * kl + jnp.where(last_row, dbl, 0.0)
            dla = _mm_f32(triuf, db)
            dlogit = dla * (1.0 / GLA_GATE_TEMP) * (1.0 - jnp.exp(GLA_GATE_TEMP * la))
            dzg_ref[rows, :] = _mm_nt(dlogit, wgu_ref[...]).astype(BF16)
            dwgu_ref[...] += _mm_tn(zg_ref[rows, :], dlogit)
            dbg_ref[...] += jnp.sum(dlogit, axis=0, keepdims=True)
            return carry

        lax.fori_loop(0, nc, chunk, 0)

    row = lambda i: (nt - 1 - i, 0)
    const = lambda i: (0, 0)
    return _tc_call(
        body, name=name, grid=(nt,),
        in_specs=[pl.BlockSpec((tm, vw), row), pl.BlockSpec((tm, vw), row), pl.BlockSpec((tm, vw), row),
                  pl.BlockSpec((1, vw), const), pl.BlockSpec((tm, kw), row), pl.BlockSpec((tm, kw), row),
                  pl.BlockSpec((tm, vw), row), pl.BlockSpec((tm, kw), row), pl.BlockSpec((tm, LANE), row),
                  pl.BlockSpec((nc, dv, kw), lambda i: (nt - 1 - i, 0, 0)), pl.BlockSpec((LANE, kw), const)],
        out_specs=[pl.BlockSpec((tm, kw), row), pl.BlockSpec((tm, kw), row), pl.BlockSpec((tm, vw), row),
                   pl.BlockSpec((tm, vw), row), pl.BlockSpec((tm, LANE), row), pl.BlockSpec((LANE, kw), const),
                   pl.BlockSpec((1, kw), const), pl.BlockSpec((1, vw), const)],
        out_shape=[jax.ShapeDtypeStruct((t, kw), BF16), jax.ShapeDtypeStruct((t, kw), BF16),
                   jax.ShapeDtypeStruct((t, vw), BF16), jax.ShapeDtypeStruct((t, vw), BF16),
                   jax.ShapeDtypeStruct((t, LANE), BF16), jax.ShapeDtypeStruct((LANE, kw), F32),
                   jax.ShapeDtypeStruct((1, kw), F32), jax.ShapeDtypeStruct((1, vw), F32)],
        scratch_shapes=[pltpu.VMEM((vw, kw), F32), pltpu.VMEM((tm, vw), F32)],
        compiler_params=_seq(1),
    )(dy, o, r, gnorm, q, k, v, loga, zg, sall, wgu)


def _outproj_fwd(yp, yg, w_out, xhat, gam, bet, alpha, name):
    t, d = xhat.shape
    pw = yp.shape[1]
    tm = _row_tile(t)

    def body(yp_ref, yg_ref, w_ref, x_ref, g_ref, b_ref, xhat_ref, rstd_ref):
        h = x_ref[...] * g_ref[...] + b_ref[...]
        y = (jnp.dot(yp_ref[...], w_ref[0:pw, :], preferred_element_type=F32)
             + jnp.dot(yg_ref[...], w_ref[pw:, :], preferred_element_type=F32))
        xh, rs = _ln_stats(alpha * h + y)
        xhat_ref[...] = xh
        rstd_ref[...] = rs

    row = lambda i: (i, 0)
    vec = pl.BlockSpec((1, d), lambda i: (0, 0))
    return _tc_call(
        body, name=name, grid=(t // tm,),
        in_specs=[pl.BlockSpec((tm, pw), row), pl.BlockSpec((tm, yg.shape[1]), row),
                  pl.BlockSpec(w_out.shape, lambda i: (0, 0)), pl.BlockSpec((tm, d), row), vec, vec],
        out_specs=[pl.BlockSpec((tm, d), row), pl.BlockSpec((tm, 1), row)],
        out_shape=[jax.ShapeDtypeStruct((t, d), F32), jax.ShapeDtypeStruct((t, 1), F32)],
        compiler_params=_seq(1),
    )(yp, yg, w_out, xhat, gam, bet)


def _outproj_bwd(dh, xhat, rstd, ln_g, w_out, pw, alpha, name):
    t, d = dh.shape
    tm = _row_tile(t)
    gw = w_out.shape[0] - pw

    def body(dh_ref, xh_ref, rs_ref, g_ref, w_ref, dyb_ref, dyp_ref, dyg_ref, dres_ref, dgam_ref, dbet_ref):
        @pl.when(pl.program_id(0) == 0)
        def _():
            dgam_ref[...] = jnp.zeros_like(dgam_ref)
            dbet_ref[...] = jnp.zeros_like(dbet_ref)

        dy = dh_ref[...]
        xh = xh_ref[...]
        dr = _ln_bwd(dy, xh, rs_ref[...], g_ref[...])
        dgam_ref[...] += jnp.sum(dy * xh, axis=0, keepdims=True)
        dbet_ref[...] += jnp.sum(dy, axis=0, keepdims=True)
        drb = dr.astype(BF16)
        dyb_ref[...] = drb
        dres_ref[...] = alpha * dr
        dyp_ref[...] = _mm_nt(drb, w_ref[0:pw, :])
        dyg_ref[...] = _mm_nt(drb, w_ref[pw:, :])

    row = lambda i: (i, 0)
    vec = pl.BlockSpec((1, d), lambda i: (0, 0))
    return _tc_call(
        body, name=name, grid=(t // tm,),
        in_specs=[pl.BlockSpec((tm, d), row), pl.BlockSpec((tm, d), row), pl.BlockSpec((tm, 1), row), vec,
                  pl.BlockSpec(w_out.shape, lambda i: (0, 0))],
        out_specs=[pl.BlockSpec((tm, d), row), pl.BlockSpec((tm, pw), row), pl.BlockSpec((tm, gw), row),
                   pl.BlockSpec((tm, d), row), vec, vec],
        out_shape=[jax.ShapeDtypeStruct((t, d), BF16), jax.ShapeDtypeStruct((t, pw), F32),
                   jax.ShapeDtypeStruct((t, gw), F32), jax.ShapeDtypeStruct((t, d), F32),
                   jax.ShapeDtypeStruct((1, d), F32), jax.ShapeDtypeStruct((1, d), F32)],
        compiler_params=_seq(1),
    )(dh, xhat, rstd, ln_g, w_out)


def _loss_head(xhat, gam, bet, target, n_rows, name):
    t, d = xhat.shape
    tm = _row_tile(t)

    def body(x_ref, g_ref, b_ref, t_ref, dy_ref, loss_ref):
        i = pl.program_id(0)

        @pl.when(i == 0)
        def _():
            loss_ref[...] = jnp.zeros_like(loss_ref)

        rowi = i * tm + lax.broadcasted_iota(jnp.int32, (tm, 1), 0)
        live = (rowi >= N_META) & (rowi < N_META + n_rows)
        diff = jnp.where(live, x_ref[...] * g_ref[...] + b_ref[...] - t_ref[...], 0.0)
        dy_ref[...] = diff * (1.0 / d)
        loss_ref[...] += jnp.sum(diff * diff) * (0.5 / d)

    row = lambda i: (i, 0)
    vec = pl.BlockSpec((1, d), lambda i: (0, 0))
    return _tc_call(
        body, name=name, grid=(t // tm,),
        in_specs=[pl.BlockSpec((tm, d), row), vec, vec, pl.BlockSpec((tm, d), row)],
        out_specs=[pl.BlockSpec((tm, d), row), pl.BlockSpec((8, LANE), lambda i: (0, 0))],
        out_shape=[jax.ShapeDtypeStruct((t, d), F32), jax.ShapeDtypeStruct((8, LANE), F32)],
        compiler_params=_seq(1),
    )(xhat, gam, bet, target)


def _rows_block(r, c):
    best = r
    for cand in range(8, r, 8):
        if r % cand == 0 and cand * c * 4 <= (1 << 20):
            best = cand
    return best if best * c * 4 <= (4 << 20) else r


def _sum_slots(recv, name):
    nl, ns, r, c = recv.shape
    tr = _rows_block(r, c)

    def body(r_ref, o_ref):
        acc = r_ref[0].astype(F32)
        for s in range(1, ns):
            acc = acc + r_ref[s].astype(F32)
        o_ref[...] = acc

    return _tc_call(
        body, name=name, grid=(nl, r // tr),
        in_specs=[pl.BlockSpec((None, ns, tr, c), lambda l, i: (l, 0, i, 0))],
        out_specs=pl.BlockSpec((None, tr, c), lambda l, i: (l, i, 0)),
        out_shape=jax.ShapeDtypeStruct((nl, r, c), F32),
        compiler_params=_seq(2),
    )(recv)


def _adamw(w, terms, m, v, name):
    nl, r, c = w.shape
    tr = _rows_block(r, c)
    nterm = len(terms)

    def body(*refs):
        w_ref = refs[0]
        t_refs = refs[1:1 + nterm]
        m_ref, v_ref, g_ref, d_ref, nm_ref, nv_ref = refs[1 + nterm:]
        g = t_refs[0][...]
        for tr_ in t_refs[1:]:
            g = g + tr_[...]
        nm = ADAM_B1 * m_ref[...] + (1.0 - ADAM_B1) * g
        nv = ADAM_B2 * v_ref[...] + (1.0 - ADAM_B2) * jnp.square(g)
        m_hat = nm / (1.0 - ADAM_B1 ** ADAM_STEP)
        v_hat = nv / (1.0 - ADAM_B2 ** ADAM_STEP)
        g_ref[...] = g
        d_ref[...] = -ADAM_LR * (m_hat / (jnp.sqrt(v_hat) + ADAM_EPS) + ADAM_WD * w_ref[...])
        nm_ref[...] = nm
        nv_ref[...] = nv

    spec = pl.BlockSpec((None, tr, c), lambda l, i: (l, i, 0))
    shp = jax.ShapeDtypeStruct((nl, r, c), F32)
    return _tc_call(
        body, name=name, grid=(nl, r // tr),
        in_specs=[spec] * (3 + nterm), out_specs=[spec] * 4, out_shape=[shp] * 4,
        compiler_params=_seq(2),
    )(w, *terms, m, v)


XY_RELATIONS = ((1, 0), (0, 1), (1, 1))
ALL_RELATIONS = tuple((fx, fy, fc) for fx in (0, 1) for fy in (0, 1) for fc in (0, 1) if fx or fy or fc)


def _flip(v, f):
    return 1 - v if f else v


def _any_spec(n):
    return [pl.BlockSpec(memory_space=pl.ANY)] * n


def _all_gather(shards):
    n = len(shards)

    def body(*refs):
        ins, outs = refs[:n], refs[n:2 * n]
        send_sems, recv_sems, local_sems = refs[2 * n:]
        x, y, c = lax.axis_index("x"), lax.axis_index("y"), lax.axis_index("c")
        me = 2 * x + y
        local = [pltpu.make_async_copy(ins[k], outs[k].at[me], local_sems.at[k]) for k in range(n)]
        for cp in local:
            cp.start()
        sends = []
        for k in range(n):
            for r, (fx, fy) in enumerate(XY_RELATIONS):
                cp = pltpu.make_async_remote_copy(
                    src_ref=ins[k], dst_ref=outs[k].at[me], send_sem=send_sems.at[k, r], recv_sem=recv_sems.at[k, r],
                    device_id=(_flip(x, fx), _flip(y, fy), c), device_id_type=MESH)
                cp.start()
                sends.append(cp)
        for k in range(n):
            for r, (fx, fy) in enumerate(XY_RELATIONS):
                src_slot = 2 * _flip(x, fx) + _flip(y, fy)
                pltpu.make_async_remote_copy(
                    src_ref=ins[k], dst_ref=outs[k].at[src_slot], send_sem=send_sems.at[k, r],
                    recv_sem=recv_sems.at[k, r], device_id=(_flip(x, fx), _flip(y, fy), c),
                    device_id_type=MESH).wait_recv()
        for cp in sends:
            cp.wait_send()
        for cp in local:
            cp.wait()

    return _comm_call(
        body, name="gather_weights",
        in_specs=_any_spec(n), out_specs=_any_spec(n),
        out_shape=[jax.ShapeDtypeStruct((N_SHARD,) + s.shape, s.dtype) for s in shards],
        scratch_shapes=[pltpu.SemaphoreType.DMA((n, 3)), pltpu.SemaphoreType.DMA((n, 3)),
                        pltpu.SemaphoreType.DMA((n,))],
    )(*shards)


def _scatter_grads(grads, groups, small):
    n = len(grads)
    ngroup = 1 + max(g for g, _ in groups)
    nlayer = [1 + max(l for g, l in groups if g == gi) for gi in range(ngroup)]

    def body(*refs):
        ins, small_ref = refs[:n], refs[n]
        outs, small_out = refs[n + 1:n + 1 + ngroup], refs[n + 1 + ngroup]
        send_sems, recv_sems, local_sems, ssend, srecv, slocal = refs[n + 2 + ngroup:]
        x, y, c = lax.axis_index("x"), lax.axis_index("y"), lax.axis_index("c")
        me = 2 * x + y
        me_all = 4 * x + 2 * y + c
        started = []
        for i, (g, l) in enumerate(groups):
            cp = pltpu.make_async_copy(ins[i].at[me], outs[g].at[l, me], local_sems.at[i])
            cp.start()
            started.append(cp)
        cp = pltpu.make_async_copy(small_ref, small_out.at[me_all], slocal)
        cp.start()
        started.append(cp)
        sends = []
        for i, (g, l) in enumerate(groups):
            for r, (fx, fy) in enumerate(XY_RELATIONS):
                px, py = _flip(x, fx), _flip(y, fy)
                cp = pltpu.make_async_remote_copy(
                    src_ref=ins[i].at[2 * px + py], dst_ref=outs[g].at[l, me], send_sem=send_sems.at[i, r],
                    recv_sem=recv_sems.at[i, r], device_id=(px, py, c), device_id_type=MESH)
                cp.start()
                sends.append(cp)
        for r, (fx, fy, fc) in enumerate(ALL_RELATIONS):
            cp = pltpu.make_async_remote_copy(
                src_ref=small_ref, dst_ref=small_out.at[me_all], send_sem=ssend.at[r], recv_sem=srecv.at[r],
                device_id=(_flip(x, fx), _flip(y, fy), _flip(c, fc)), device_id_type=MESH)
            cp.start()
            sends.append(cp)
        for i, (g, l) in enumerate(groups):
            for r, (fx, fy) in enumerate(XY_RELATIONS):
                px, py = _flip(x, fx), _flip(y, fy)
                pltpu.make_async_remote_copy(
                    src_ref=ins[i].at[me], dst_ref=outs[g].at[l, 2 * px + py], send_sem=send_sems.at[i, r],
                    recv_sem=recv_sems.at[i, r], device_id=(px, py, c), device_id_type=MESH).wait_recv()
        for r, (fx, fy, fc) in enumerate(ALL_RELATIONS):
            px, py, pc = _flip(x, fx), _flip(y, fy), _flip(c, fc)
            pltpu.make_async_remote_copy(
                src_ref=small_ref, dst_ref=small_out.at[4 * px + 2 * py + pc], send_sem=ssend.at[r],
                recv_sem=srecv.at[r], device_id=(px, py, pc), device_id_type=MESH).wait_recv()
        for cp in sends:
            cp.wait_send()
        for cp in started:
            cp.wait()

    out_shape = []
    for gi in range(ngroup):
        shp = next(grads[i].shape for i, (g, _) in enumerate(groups) if g == gi)
        out_shape.append(jax.ShapeDtypeStruct((nlayer[gi],) + shp, grads[0].dtype))
    out_shape.append(jax.ShapeDtypeStruct((N_DEV,) + small.shape, small.dtype))
    nrel = len(ALL_RELATIONS)
    return _comm_call(
        body, name="scatter_grads",
        in_specs=_any_spec(n + 1), out_specs=_any_spec(ngroup + 1), out_shape=out_shape,
        scratch_shapes=[pltpu.SemaphoreType.DMA((n, 3)), pltpu.SemaphoreType.DMA((n, 3)),
                        pltpu.SemaphoreType.DMA((n,)), pltpu.SemaphoreType.DMA((nrel,)),
                        pltpu.SemaphoreType.DMA((nrel,)), pltpu.SemaphoreType.DMA(())],
    )(*grads, small)


def _swap_sibling(parts):
    n = len(parts)

    def body(*refs):
        ins, outs = refs[:n], refs[n:2 * n]
        send_sems, recv_sems = refs[2 * n:]
        sib = (lax.axis_index("x"), lax.axis_index("y"), 1 - lax.axis_index("c"))
        cps = [pltpu.make_async_remote_copy(src_ref=ins[k], dst_ref=outs[k], send_sem=send_sems.at[k],
                                            recv_sem=recv_sems.at[k], device_id=sib, device_id_type=MESH)
               for k in range(n)]
        for cp in cps:
            cp.start()
        for cp in cps:
            cp.wait_recv()
        for cp in cps:
            cp.wait_send()

    return _comm_call(
        body, name="swap_sibling",
        in_specs=_any_spec(n), out_specs=_any_spec(n),
        out_shape=[jax.ShapeDtypeStruct(p.shape, p.dtype) for p in parts],
        scratch_shapes=[pltpu.SemaphoreType.DMA((n,)), pltpu.SemaphoreType.DMA((n,))],
    )(*parts)


def _col_shards(a, n=N_SHARD):
    r, c = a.shape
    return a.reshape(r, n, c // n).transpose(1, 0, 2)


def _from_col_shards(a):
    n, r, cs = a.shape
    return a.transpose(1, 0, 2).reshape(r, n * cs)


def kernel(x, meta_tokens, ffn1_w_gate, ffn1_w_up, ffn1_w_down, ln1_g, ln1_b, w_in, w_gate_up, b_gate, w_pool, pool_scale, gla_norm_g, w_out, ln2_g, ln2_b, ffn2_w_gate, ffn2_w_up, ffn2_w_down, ln3_g, ln3_b, loss_target, m_meta_tokens, m_ffn1_w_gate, m_ffn1_w_up, m_ffn1_w_down, m_ln1_g, m_ln1_b, m_w_in, m_w_gate_up, m_b_gate, m_w_pool, m_pool_scale, m_gla_norm_g, m_w_out, m_ln2_g, m_ln2_b, m_ffn2_w_gate, m_ffn2_w_up, m_ffn2_w_down, m_ln3_g, m_ln3_b, v_meta_tokens, v_ffn1_w_gate, v_ffn1_w_up, v_ffn1_w_down, v_ln1_g, v_ln1_b, v_w_in, v_w_gate_up, v_b_gate, v_w_pool, v_pool_scale, v_gla_norm_g, v_w_out, v_ln2_g, v_ln2_b, v_ffn2_w_gate, v_ffn2_w_up, v_ffn2_w_down, v_ln3_g, v_ln3_b):
    w = dict(meta_tokens=meta_tokens, ffn1_w_gate=ffn1_w_gate, ffn1_w_up=ffn1_w_up, ffn1_w_down=ffn1_w_down,
             ln1_g=ln1_g, ln1_b=ln1_b, w_in=w_in, w_gate_up=w_gate_up, b_gate=b_gate, w_pool=w_pool,
             pool_scale=pool_scale, gla_norm_g=gla_norm_g, w_out=w_out, ln2_g=ln2_g, ln2_b=ln2_b,
             ffn2_w_gate=ffn2_w_gate, ffn2_w_up=ffn2_w_up, ffn2_w_down=ffn2_w_down, ln3_g=ln3_g, ln3_b=ln3_b)
    mom1 = dict(meta_tokens=m_meta_tokens, ffn1_w_gate=m_ffn1_w_gate, ffn1_w_up=m_ffn1_w_up,
                ffn1_w_down=m_ffn1_w_down, ln1_g=m_ln1_g, ln1_b=m_ln1_b, w_in=m_w_in, w_gate_up=m_w_gate_up,
                b_gate=m_b_gate, w_pool=m_w_pool, pool_scale=m_pool_scale, gla_norm_g=m_gla_norm_g, w_out=m_w_out,
                ln2_g=m_ln2_g, ln2_b=m_ln2_b, ffn2_w_gate=m_ffn2_w_gate, ffn2_w_up=m_ffn2_w_up,
                ffn2_w_down=m_ffn2_w_down, ln3_g=m_ln3_g, ln3_b=m_ln3_b)
    mom2 = dict(meta_tokens=v_meta_tokens, ffn1_w_gate=v_ffn1_w_gate, ffn1_w_up=v_ffn1_w_up,
                ffn1_w_down=v_ffn1_w_down, ln1_g=v_ln1_g, ln1_b=v_ln1_b, w_in=v_w_in, w_gate_up=v_w_gate_up,
                b_gate=v_b_gate, w_pool=v_w_pool, pool_scale=v_pool_scale, gla_norm_g=v_gla_norm_g, w_out=v_w_out,
                ln2_g=v_ln2_g, ln2_b=v_ln2_b, ffn2_w_gate=v_ffn2_w_gate, ffn2_w_up=v_ffn2_w_up,
                ffn2_w_down=v_ffn2_w_down, ln3_g=v_ln3_g, ln3_b=v_ln3_b)

    xs = x[0]
    s_len, d = xs.shape
    nl = ln1_g.shape[0]
    alpha = (2.0 * nl) ** 0.25
    t_real = N_META + s_len
    t_pad = -(-t_real // LANE) * LANE
    pw = pool_scale.shape[1]
    kw = b_gate.shape[1]
    vw = gla_norm_g.shape[1]
    rank = w_gate_up.shape[1]
    widths = (pw, kw, kw, vw, vw)
    n_main = sum(widths)
    dff_s = ffn1_w_gate.shape[2]

    gathered = _all_gather([meta_tokens] + [w[n].astype(BF16) for n in BIG])
    meta_full = _from_col_shards(gathered[0])
    wa = dict(zip(BIG, gathered[1:]))
    w_main, w_lr, wgu, wout = [], [], [], []
    for l in range(nl):
        wi = _from_col_shards(wa["w_in"][:, l])
        w_main.append(wi[:, :n_main])
        w_lr.append(jnp.pad(wi[:, n_main:], ((0, 0), (0, LANE - rank))))
        wgu.append(jnp.pad(_from_col_shards(wa["w_gate_up"][:, l]), ((0, LANE - rank), (0, 0))))
        wout.append(wa["w_out"][:, l].reshape(-1, d))
    wp16 = w_pool.astype(BF16)
    ones = jnp.ones((1, d), F32)
    zeros = jnp.zeros((1, d), F32)

    h0 = jnp.concatenate([meta_full, xs, jnp.zeros((t_pad - t_real, d), F32)], axis=0)
    target = jnp.concatenate([jnp.zeros((N_META, d), F32), loss_target[0], jnp.zeros((t_pad - t_real, d), F32)], axis=0)

    saved = []
    cur, cur_g, cur_b = h0, ones, zeros
    for l in range(nl):
        s = {"in": (cur, cur_g)}
        xh1, rs1, hb0, g1, u1 = _ffn_fwd(cur, cur_g, cur_b, wa["ffn1_w_gate"], wa["ffn1_w_up"], wa["ffn1_w_down"], l,
                                         ln1_g[l:l + 1], ln1_b[l:l + 1], alpha, f"ffn1_fwd_{l}")
        up, q, k, v, r, zg, la, hb1 = _inproj_fwd(xh1, ln1_g[l:l + 1], ln1_b[l:l + 1], w_main[l], w_lr[l], wgu[l],
                                                  b_gate[l:l + 1], widths, f"inproj_fwd_{l}")
        yp, pb = _pool_fwd(up, wp16[l], pool_scale[l:l + 1], f"pool_fwd_{l}")
        o, yg, sall = _gla_fwd(q, k, v, la, r, gla_norm_g[l:l + 1], f"gla_fwd_{l}")
        xh2, rs2 = _outproj_fwd(yp, yg, wout[l], xh1, ln1_g[l:l + 1], ln1_b[l:l + 1], alpha, f"outproj_fwd_{l}")
        xh3, rs3, hb2, g2, u2 = _ffn_fwd(xh2, ln2_g[l:l + 1], ln2_b[l:l + 1], wa["ffn2_w_gate"], wa["ffn2_w_up"],
                                         wa["ffn2_w_down"], l, ln3_g[l:l + 1], ln3_b[l:l + 1], alpha, f"ffn2_fwd_{l}")
        s.update(xh1=xh1, rs1=rs1, hb0=hb0, g1=g1, u1=u1, q=q, k=k, v=v, r=r, zg=zg, la=la, hb1=hb1, yp=yp, pb=pb,
                 o=o, yg=yg, sall=sall, xh2=xh2, rs2=rs2, xh3=xh3, rs3=rs3, hb2=hb2, g2=g2, u2=u2)
        saved.append(s)
        cur, cur_g, cur_b = xh3, ln3_g[l:l + 1], ln3_b[l:l + 1]

    dh, loss_acc = _loss_head(cur, cur_g, cur_b, target, s_len, "loss_head")
    loss = lax.psum(loss_acc[0, 0], ("x", "y", "c"))

    big_grads = {n: [None] * nl for n in BIG}
    small_grads = {n: [None] * nl for n in SMALL}
    for l in reversed(range(nl)):
        s = saved[l]
        dh, dfb, dgb, dub, act, dgam, dbet = _ffn_bwd(dh, s["xh3"], s["rs3"], ln3_g[l:l + 1], s["g2"], s["u2"],
                                                      wa["ffn2_w_gate"], wa["ffn2_w_up"], wa["ffn2_w_down"], l, alpha,
                                                      f"ffn2_bwd_{l}")
        small_grads["ln3_g"][l], small_grads["ln3_b"][l] = dgam, dbet
        big_grads["ffn2_w_gate"][l] = _wgrad(s["hb2"], dgb, d, dff_s, True, f"ffn2_dwg_{l}")
        big_grads["ffn2_w_up"][l] = _wgrad(s["hb2"], dub, d, dff_s, True, f"ffn2_dwu_{l}")
        big_grads["ffn2_w_down"][l] = _wgrad(act, dfb, dff_s, d, False, f"ffn2_dwd_{l}").reshape(N_SHARD, dff_s, d)

        dyb, dyp, dyg, dres, dgam, dbet = _outproj_bwd(dh, s["xh2"], s["rs2"], ln2_g[l:l + 1], wout[l], pw, alpha,
                                                       f"outproj_bwd_{l}")
        small_grads["ln2_g"][l], small_grads["ln2_b"][l] = dgam, dbet
        dwo = jnp.concatenate([_wgrad(s["yp"], dyb, pw, d, False, f"dwout_pool_{l}"),
                               _wgrad(s["yg"], dyb, vw, d, False, f"dwout_gla_{l}")], axis=0)
        big_grads["w_out"][l] = dwo.reshape(N_SHARD, -1, d)
        dq, dk, dv, dr, dzg, dwgu, dbg, dgn = _gla_bwd(dyg, s["o"], s["r"], gla_norm_g[l:l + 1], s["q"], s["k"],
                                                       s["v"], s["la"], s["zg"], s["sall"], wgu[l], f"gla_bwd_{l}")
        dup, dwp, dsc = _pool_bwd(dyp, s["pb"], wp16[l], pool_scale[l:l + 1], f"pool_bwd_{l}")
        small_grads["b_gate"][l], small_grads["gla_norm_g"][l] = dbg, dgn
        small_grads["w_pool"][l], small_grads["pool_scale"][l] = dwp, dsc
        big_grads["w_gate_up"][l] = _col_shards(dwgu[:rank].astype(BF16))
        dh, dz = _inproj_bwd(dres, [dup, dq, dk, dv, dr], dzg, w_main[l], w_lr[l], f"inproj_bwd_{l}")
        dwi = jnp.concatenate([_wgrad(s["hb1"], dz, d, 512, False, f"dwin_main_{l}"),
                               _wgrad(s["hb1"], dzg, d, LANE, False, f"dwin_lr_{l}")[:, :rank]], axis=1)
        big_grads["w_in"][l] = _col_shards(dwi)

        xin, gin = s["in"]
        dh, dfb, dgb, dub, act, dgam, dbet = _ffn_bwd(dh, s["xh1"], s["rs1"], ln1_g[l:l + 1], s["g1"], s["u1"],
                                                      wa["ffn1_w_gate"], wa["ffn1_w_up"], wa["ffn1_w_down"], l, alpha,
                                                      f"ffn1_bwd_{l}")
        small_grads["ln1_g"][l], small_grads["ln1_b"][l] = dgam, dbet
        big_grads["ffn1_w_gate"][l] = _wgrad(s["hb0"], dgb, d, dff_s, True, f"ffn1_dwg_{l}")
        big_grads["ffn1_w_up"][l] = _wgrad(s["hb0"], dub, d, dff_s, True, f"ffn1_dwu_{l}")
        big_grads["ffn1_w_down"][l] = _wgrad(act, dfb, dff_s, d, False, f"ffn1_dwd_{l}").reshape(N_SHARD, dff_s, d)

    grad_x = dh[N_META:t_real][None]
    dmeta = _col_shards(dh[:N_META].astype(BF16))

    sharded = ("meta_tokens",) + BIG
    grads, groups = [dmeta], [(0, 0)]
    for gi, n in enumerate(BIG):
        for l in range(nl):
            grads.append(big_grads[n][l])
            groups.append((gi + 1, l))

    def pack(parts):
        flat = jnp.concatenate([parts[n].reshape(-1) for n in SMALL])
        return flat.reshape(-1, LANE)

    small_vec = pack({n: jnp.stack(small_grads[n]) for n in SMALL})
    *recv, small_recv = _scatter_grads(grads, groups, small_vec)
    partial = [_sum_slots(rc, f"sum_{n}") for n, rc in zip(sharded, recv)]
    other = _swap_sibling(partial)

    results = {}
    for n, mine, theirs in zip(sharded, partial, other):
        shp = w[n].shape
        as3 = lambda a: a.reshape(mine.shape)
        outs = _adamw(as3(w[n]), [mine, theirs], as3(mom1[n]), as3(mom2[n]), f"adamw_{n}")
        results[n] = [o.reshape(shp) for o in outs]
    small_terms = [small_recv[i][None] for i in range(N_DEV)]
    souts = _adamw(pack(w)[None], small_terms, pack(mom1)[None], pack(mom2)[None], "adamw_small")
    off = 0
    for n in SMALL:
        size = w[n].size
        results[n] = [o.reshape(-1)[off:off + size].reshape(w[n].shape) for o in souts]
        off += size

    out = [loss, grad_x]
    for part in range(4):
        out += [results[n][part] for n in WEIGHTS]
    return tuple(out)
```

```python
import functools

import jax
import jax.numpy as jnp
from jax import lax
from jax.experimental import pallas as pl
from jax.experimental.pallas import tpu as pltpu

F32 = jnp.float32
BF16 = jnp.bfloat16
MESH = pl.DeviceIdType.MESH

N_META = 16
POOL_WINDOWS = (2, 4, 8, 16)
POOL_HALO = 16
N_HEADS = 4
GLA_GATE_TEMP = 16.0
CHUNK = 64
LN_EPS = 1e-5
RMS_EPS = 1e-6
ADAM_LR = 0.001
ADAM_B1 = 0.9
ADAM_B2 = 0.999
ADAM_EPS = 1e-08
ADAM_WD = 0.01
ADAM_STEP = 10
LANE = 128
ROW_TILE = 640
N_SHARD = 4
N_DEV = 8

BIG = ("ffn1_w_gate", "ffn1_w_up", "ffn1_w_down", "w_in", "w_gate_up", "w_out",
       "ffn2_w_gate", "ffn2_w_up", "ffn2_w_down")
SMALL = ("ln1_g", "ln1_b", "b_gate", "w_pool", "pool_scale", "gla_norm_g", "ln2_g", "ln2_b", "ln3_g", "ln3_b")
WEIGHTS = ("meta_tokens", "ffn1_w_gate", "ffn1_w_up", "ffn1_w_down", "ln1_g", "ln1_b", "w_in", "w_gate_up",
           "b_gate", "w_pool", "pool_scale", "gla_norm_g", "w_out", "ln2_g", "ln2_b", "ffn2_w_gate",
           "ffn2_w_up", "ffn2_w_down", "ln3_g", "ln3_b")


def _tc_call(body, **kw):
    return pl.pallas_call(body, **kw)


def _comm_call(body, **kw):
    return pl.pallas_call(body, **kw)


def _seq(n):
    return pltpu.CompilerParams(dimension_semantics=("arbitrary",) * n)


def _mm(a, b):
    return jnp.dot(a.astype(BF16), b.astype(BF16), preferred_element_type=F32)


def _mm_nt(a, b):
    return lax.dot_general(a.astype(BF16), b.astype(BF16), (((1,), (1,)), ((), ())), preferred_element_type=F32)


def _mm_tn(a, b):
    return lax.dot_general(a.astype(BF16), b.astype(BF16), (((0,), (0,)), ((), ())), preferred_element_type=F32)


def _mm_f32(a, b):
    return jnp.dot(a, b, precision=lax.Precision.HIGHEST, preferred_element_type=F32)


def _row_tile(t):
    tm = min(ROW_TILE, t)
    while t % tm:
        tm -= LANE
    return tm


def _silu_parts(g):
    sg = jax.nn.sigmoid(g)
    return sg, g * sg


def _ln_stats(r):
    mu = jnp.mean(r, axis=-1, keepdims=True)
    rc = r - mu
    var = jnp.mean(rc * rc, axis=-1, keepdims=True)
    rs = lax.rsqrt(var + LN_EPS)
    return rc * rs, rs


def _ln_bwd(dy, xh, rs, gam):
    dyg = dy * gam
    c1 = jnp.mean(dyg, axis=-1, keepdims=True)
    c2 = jnp.mean(dyg * xh, axis=-1, keepdims=True)
    return rs * (dyg - c1 - xh * c2)


def _ffn_fwd(xin, gam_in, bet_in, wg, wu, wd, alpha, name):
    t, d = xin.shape
    nj, _, tf = wg.shape
    tm = _row_tile(t)

    def body(x_ref, gi_ref, bi_ref, wg_ref, wu_ref, wd_ref, xhat_ref, rstd_ref, hb_ref, go_ref, uo_ref, acc, hbs):
        j = pl.program_id(1)

        @pl.when(j == 0)
        def _():
            hb = (x_ref[...] * gi_ref[...] + bi_ref[...]).astype(BF16)
            hbs[...] = hb
            hb_ref[...] = hb
            acc[...] = jnp.zeros_like(acc)

        hb = hbs[...]
        g = jnp.dot(hb, wg_ref[...], preferred_element_type=F32)
        u = jnp.dot(hb, wu_ref[...], preferred_element_type=F32)
        _, sl = _silu_parts(g)
        go_ref[...] = g.astype(BF16)
        uo_ref[...] = u.astype(BF16)
        acc[...] += jnp.dot((sl * u).astype(BF16), wd_ref[...], preferred_element_type=F32)

        @pl.when(j == nj - 1)
        def _():
            h = x_ref[...] * gi_ref[...] + bi_ref[...]
            xhat, rs = _ln_stats(alpha * h + 0.5 * acc[...])
            xhat_ref[...] = xhat
            rstd_ref[...] = rs

    row = lambda i, j: (i, 0)
    vec = pl.BlockSpec((1, d), lambda i, j: (0, 0))
    return _tc_call(
        body, name=name, grid=(t // tm, nj),
        in_specs=[pl.BlockSpec((tm, d), row), vec, vec,
                  pl.BlockSpec((None, d, tf), lambda i, j: (j, 0, 0)),
                  pl.BlockSpec((None, d, tf), lambda i, j: (j, 0, 0)),
                  pl.BlockSpec((None, tf, d), lambda i, j: (j, 0, 0))],
        out_specs=[pl.BlockSpec((tm, d), row), pl.BlockSpec((tm, 1), row), pl.BlockSpec((tm, d), row),
                   pl.BlockSpec((None, tm, tf), lambda i, j: (j, i, 0)),
                   pl.BlockSpec((None, tm, tf), lambda i, j: (j, i, 0))],
        out_shape=[jax.ShapeDtypeStruct((t, d), F32), jax.ShapeDtypeStruct((t, 1), F32),
                   jax.ShapeDtypeStruct((t, d), BF16), jax.ShapeDtypeStruct((nj, t, tf), BF16),
                   jax.ShapeDtypeStruct((nj, t, tf), BF16)],
        scratch_shapes=[pltpu.VMEM((tm, d), F32), pltpu.VMEM((tm, d), BF16)],
        compiler_params=_seq(2),
    )(xin, gam_in, bet_in, wg, wu, wd)


def _ffn_bwd(dh, xhat, rstd, ln_g, gb, ub, wg, wu, wd, alpha, name):
    t, d = dh.shape
    nj, _, tf = wg.shape
    tm = _row_tile(t)

    def body(dh_ref, xh_ref, rs_ref, g_ref, gb_ref, ub_ref, wg_ref, wu_ref, wd_ref,
             dhin_ref, df_ref, dg_ref, du_ref, act_ref, dgam_ref, dbet_ref, dr_s, df_s, acc):
        i = pl.program_id(0)
        j = pl.program_id(1)

        @pl.when(j == 0)
        def _():
            dy = dh_ref[...]
            xh = xh_ref[...]
            dr = _ln_bwd(dy, xh, rs_ref[...], g_ref[...])
            dr_s[...] = dr
            dfb = (0.5 * dr).astype(BF16)
            df_s[...] = dfb
            df_ref[...] = dfb
            acc[...] = jnp.zeros_like(acc)

            @pl.when(i == 0)
            def _():
                dgam_ref[...] = jnp.zeros_like(dgam_ref)
                dbet_ref[...] = jnp.zeros_like(dbet_ref)

            dgam_ref[...] += jnp.sum(dy * xh, axis=0, keepdims=True)
            dbet_ref[...] += jnp.sum(dy, axis=0, keepdims=True)

        dact = _mm_nt(df_s[...], wd_ref[...])
        g = gb_ref[...].astype(F32)
        u = ub_ref[...].astype(F32)
        sg, sl = _silu_parts(g)
        dg = (dact * u * (sg * (1.0 + g * (1.0 - sg)))).astype(BF16)
        du = (dact * sl).astype(BF16)
        dg_ref[...] = dg
        du_ref[...] = du
        act_ref[...] = (sl * u).astype(BF16)
        acc[...] += _mm_nt(dg, wg_ref[...]) + _mm_nt(du, wu_ref[...])

        @pl.when(j == nj - 1)
        def _():
            dhin_ref[...] = alpha * dr_s[...] + acc[...]

    row = lambda i, j: (i, 0)
    col = pl.BlockSpec((None, tm, tf), lambda i, j: (j, i, 0))
    vec = pl.BlockSpec((1, d), lambda i, j: (0, 0))
    ff = jax.ShapeDtypeStruct((nj, t, tf), BF16)
    return _tc_call(
        body, name=name, grid=(t // tm, nj),
        in_specs=[pl.BlockSpec((tm, d), row), pl.BlockSpec((tm, d), row), pl.BlockSpec((tm, 1), row), vec,
                  col, col,
                  pl.BlockSpec((None, d, tf), lambda i, j: (j, 0, 0)),
                  pl.BlockSpec((None, d, tf), lambda i, j: (j, 0, 0)),
                  pl.BlockSpec((None, tf, d), lambda i, j: (j, 0, 0))],
        out_specs=[pl.BlockSpec((tm, d), row), pl.BlockSpec((tm, d), row),
                   col, col, col, vec, vec],
        out_shape=[jax.ShapeDtypeStruct((t, d), F32), jax.ShapeDtypeStruct((t, d), BF16), ff, ff, ff,
                   jax.ShapeDtypeStruct((1, d), F32), jax.ShapeDtypeStruct((1, d), F32)],
        scratch_shapes=[pltpu.VMEM((tm, d), F32), pltpu.VMEM((tm, d), BF16), pltpu.VMEM((tm, d), F32)],
        compiler_params=_seq(2),
    )(dh, xhat, rstd, ln_g, gb, ub, wg, wu, wd)


def _wgrad(a, b, tmm, tn, shard_major, name):
    t = a.shape[-2]
    m = a.shape[-1] * (a.shape[0] if a.ndim == 3 else 1)
    n = b.shape[-1] * (b.shape[0] if b.ndim == 3 else 1)
    tk = _row_tile(t)
    nk = t // tk

    def body(a_ref, b_ref, o_ref, acc):
        k = pl.program_id(2)

        @pl.when(k == 0)
        def _():
            acc[...] = jnp.zeros_like(acc)

        acc[...] += _mm_tn(a_ref[...], b_ref[...])

        @pl.when(k == nk - 1)
        def _():
            o_ref[...] = acc[...].astype(o_ref.dtype)

    if shard_major:
        out_spec = pl.BlockSpec((None, tmm, tn), lambda i, j, k: (j, i, 0))
        out_shape = jax.ShapeDtypeStruct((n // tn, m, tn), BF16)
    else:
        out_spec = pl.BlockSpec((tmm, tn), lambda i, j, k: (i, j))
        out_shape = jax.ShapeDtypeStruct((m, n), BF16)
    return _tc_call(
        body, name=name, grid=(m // tmm, n // tn, nk),
        in_specs=[pl.BlockSpec((None, tk, tmm), lambda i, j, k: (i, k, 0)) if a.ndim == 3
                  else pl.BlockSpec((tk, tmm), lambda i, j, k: (k, i)),
                  pl.BlockSpec((None, tk, tn), lambda i, j, k: (j, k, 0)) if b.ndim == 3
                  else pl.BlockSpec((tk, tn), lambda i, j, k: (k, j))],
        out_specs=out_spec, out_shape=out_shape,
        scratch_shapes=[pltpu.VMEM((tmm, tn), F32)],
        compiler_params=_seq(3),
    )(a, b)


def _inproj_fwd(xhat, gam, bet, w_main, w_lr, wgu, b_gate, widths, name):
    t, d = xhat.shape
    tm = _row_tile(t)
    kw = wgu.shape[1]
    offs = [0]
    for w in widths:
        offs.append(offs[-1] + w)

    def body(x_ref, g_ref, b_ref, wm_ref, wl_ref, wgu_ref, bg_ref, *outs):
        piece_refs, (zg_ref, la_ref, hb_ref) = outs[:len(widths)], outs[len(widths):]
        hb = (x_ref[...] * g_ref[...] + b_ref[...]).astype(BF16)
        hb_ref[...] = hb
        for p, ref in enumerate(piece_refs):
            ref[...] = jnp.dot(hb, wm_ref[:, offs[p]:offs[p + 1]], preferred_element_type=F32)
        zg = jnp.dot(hb, wl_ref[...], preferred_element_type=F32)
        zg_ref[...] = zg
        logit = _mm(zg, wgu_ref[...]) + bg_ref[...]
        la_ref[...] = (jnp.minimum(logit, 0.0) - jnp.log(1.0 + jnp.exp(-jnp.abs(logit)))) * (1.0 / GLA_GATE_TEMP)

    row = lambda i: (i, 0)
    full = lambda a: pl.BlockSpec(a.shape, lambda i: (0,) * a.ndim)
    out_w = list(widths) + [LANE, kw]
    return _tc_call(
        body, name=name, grid=(t // tm,),
        in_specs=[pl.BlockSpec((tm, d), row), full(gam), full(bet), full(w_main), full(w_lr), full(wgu), full(b_gate)],
        out_specs=[pl.BlockSpec((tm, w), row) for w in out_w] + [pl.BlockSpec((tm, d), row)],
        out_shape=[jax.ShapeDtypeStruct((t, w), F32) for w in out_w] + [jax.ShapeDtypeStruct((t, d), BF16)],
        compiler_params=_seq(1),
    )(xhat, gam, bet, w_main, w_lr, wgu, b_gate)


def _inproj_bwd(dh_part, pieces, dzg, w_main, w_lr, name):
    t, d = dh_part.shape
    tm = _row_tile(t)
    widths = [p.shape[1] for p in pieces]
    offs = [0]
    for w in widths:
        offs.append(offs[-1] + w)

    def body(*refs):
        dhp_ref = refs[0]
        p_refs = refs[1:1 + len(widths)]
        dzg_ref, wm_ref, wl_ref, dh_ref, dz_ref = refs[1 + len(widths):]
        acc = dhp_ref[...] + _mm_nt(dzg_ref[...], wl_ref[...])
        for p, ref in enumerate(p_refs):
            v = ref[...]
            dz_ref[:, offs[p]:offs[p + 1]] = v
            acc += _mm_nt(v, wm_ref[:, offs[p]:offs[p + 1]])
        dh_ref[...] = acc

    row = lambda i: (i, 0)
    full = lambda a: pl.BlockSpec(a.shape, lambda i: (0,) * a.ndim)
    return _tc_call(
        body, name=name, grid=(t // tm,),
        in_specs=[pl.BlockSpec((tm, d), row)] + [pl.BlockSpec((tm, w), row) for w in widths]
                 + [pl.BlockSpec((tm, LANE), row), full(w_main), full(w_lr)],
        out_specs=[pl.BlockSpec((tm, d), row), pl.BlockSpec((tm, offs[-1]), row)],
        out_shape=[jax.ShapeDtypeStruct((t, d), F32), jax.ShapeDtypeStruct((t, offs[-1]), BF16)],
        compiler_params=_seq(1),
    )(dh_part, *pieces, dzg, w_main, w_lr)


def _pool_cnt(tile, tm, w):
    t = tile * tm + lax.broadcasted_iota(jnp.int32, (tm, 1), 0)
    return jnp.minimum(t + 1, w).astype(F32)


def _pool_fwd(u, wp, scale, name):
    t, pw = u.shape
    tm = _row_tile(t)
    gd = wp.shape[1]

    def body(u_ref, wp_ref, sc_ref, y_ref, p_ref, ext):
        i = pl.program_id(0)

        @pl.when(i == 0)
        def _():
            ext[0:POOL_HALO, :] = jnp.zeros((POOL_HALO, pw), F32)

        ext[POOL_HALO:POOL_HALO + tm, :] = u_ref[...]
        for gi, w in enumerate(POOL_WINDOWS):
            cols = slice(gi * gd, (gi + 1) * gd)
            s = ext[pl.ds(POOL_HALO, tm), cols]
            tot = s
            for back in range(1, w):
                tot = tot + ext[pl.ds(POOL_HALO - back, tm), cols]
            p = (tot / _pool_cnt(i, tm, w) - s).astype(BF16)
            p_ref[:, cols] = p
            y_ref[:, cols] = (jnp.dot(p, wp_ref[gi], preferred_element_type=F32) * sc_ref[:, cols]).astype(BF16)
        ext[0:POOL_HALO, :] = ext[tm:tm + POOL_HALO, :]

    row = lambda i: (i, 0)
    return _tc_call(
        body, name=name, grid=(t // tm,),
        in_specs=[pl.BlockSpec((tm, pw), row), pl.BlockSpec(wp.shape, lambda i: (0, 0, 0)),
                  pl.BlockSpec((1, pw), lambda i: (0, 0))],
        out_specs=[pl.BlockSpec((tm, pw), row), pl.BlockSpec((tm, pw), row)],
        out_shape=[jax.ShapeDtypeStruct((t, pw), BF16), jax.ShapeDtypeStruct((t, pw), BF16)],
        scratch_shapes=[pltpu.VMEM((tm + POOL_HALO, pw), F32)],
        compiler_params=_seq(1),
    )(u, wp, scale)


def _pool_bwd(dy, pb, wp, scale, name):
    t, pw = dy.shape
    tm = _row_tile(t)
    nt = t // tm
    gd = wp.shape[1]

    def body(dy_ref, p_ref, wp_ref, sc_ref, du_ref, dwp_ref, dsc_ref, ext):
        i = pl.program_id(0)
        tile = nt - 1 - i

        @pl.when(i == 0)
        def _():
            ext[tm:tm + POOL_HALO, :] = jnp.zeros((POOL_HALO, pw), F32)
            dwp_ref[...] = jnp.zeros_like(dwp_ref)
            dsc_ref[...] = jnp.zeros_like(dsc_ref)

        dps = []
        for gi, w in enumerate(POOL_WINDOWS):
            cols = slice(gi * gd, (gi + 1) * gd)
            dyv = dy_ref[:, cols]
            p = p_ref[:, cols]
            dpre = (dyv * sc_ref[:, cols]).astype(BF16)
            dsc_ref[:, cols] += jnp.sum(dyv * jnp.dot(p, wp_ref[gi], preferred_element_type=F32), axis=0, keepdims=True)
            dwp_ref[gi] += _mm_tn(p, dpre)
            dp = _mm_nt(dpre, wp_ref[gi])
            dps.append(dp)
            ext[0:tm, cols] = dp / _pool_cnt(tile, tm, w)
        for gi, w in enumerate(POOL_WINDOWS):
            cols = slice(gi * gd, (gi + 1) * gd)
            tot = ext[pl.ds(0, tm), cols]
            for fwd in range(1, w):
                tot = tot + ext[pl.ds(fwd, tm), cols]
            du_ref[:, cols] = (tot - dps[gi]).astype(BF16)
        ext[tm:tm + POOL_HALO, :] = ext[0:POOL_HALO, :]

    row = lambda i: (nt - 1 - i, 0)
    return _tc_call(
        body, name=name, grid=(nt,),
        in_specs=[pl.BlockSpec((tm, pw), row), pl.BlockSpec((tm, pw), row),
                  pl.BlockSpec(wp.shape, lambda i: (0, 0, 0)), pl.BlockSpec((1, pw), lambda i: (0, 0))],
        out_specs=[pl.BlockSpec((tm, pw), row), pl.BlockSpec(wp.shape, lambda i: (0, 0, 0)),
                   pl.BlockSpec((1, pw), lambda i: (0, 0))],
        out_shape=[jax.ShapeDtypeStruct((t, pw), BF16), jax.ShapeDtypeStruct(wp.shape, F32),
                   jax.ShapeDtypeStruct((1, pw), F32)],
        scratch_shapes=[pltpu.VMEM((tm + POOL_HALO, pw), F32)],
        compiler_params=_seq(1),
    )(dy, pb, wp, scale)


def _gla_masks(kw, vw):
    dk, dv = kw // N_HEADS, vw // N_HEADS
    lane_k = lax.broadcasted_iota(jnp.int32, (1, kw), 1)
    lane_v = lax.broadcasted_iota(jnp.int32, (1, vw), 1)
    hk = [((lane_k >= h * dk) & (lane_k < (h + 1) * dk)).astype(F32) for h in range(N_HEADS)]
    hv = [((lane_v >= h * dv) & (lane_v < (h + 1) * dv)).astype(F32) for h in range(N_HEADS)]
    r = lax.broadcasted_iota(jnp.int32, (CHUNK, CHUNK), 0)
    c = lax.broadcasted_iota(jnp.int32, (CHUNK, CHUNK), 1)
    tril = r >= c
    rs = lax.broadcasted_iota(jnp.int32, (N_HEADS * CHUNK, CHUNK), 0) & (CHUNK - 1)
    stril = rs >= lax.broadcasted_iota(jnp.int32, (N_HEADS * CHUNK, CHUNK), 1)
    return hk, hv, tril, stril


def _block_diag(x, hk, dv):
    return jnp.concatenate([x[h * dv:(h + 1) * dv, :] * hk[h] for h in range(N_HEADS)], axis=0)


def _gla_fwd(q, k, v, loga, r, gnorm, name):
    t, kw = q.shape
    vw = v.shape[1]
    dk, dv = kw // N_HEADS, vw // N_HEADS
    tm = _row_tile(t)
    nc = tm // CHUNK
    qscale = dk ** -0.5

    def body(q_ref, k_ref, v_ref, la_ref, r_ref, gn_ref, o_ref, y_ref, sall_ref, st):
        @pl.when(pl.program_id(0) == 0)
        def _():
            st[...] = jnp.zeros_like(st)

        hk, hv, tril, stril = _gla_masks(kw, vw)
        trif = tril.astype(F32)

        def chunk(c, carry):
            rows = pl.ds(pl.multiple_of(c * CHUNK, CHUNK), CHUNK)
            la = la_ref[rows, :]
            b = _mm_f32(trif, la)
            bl = jnp.sum(la, axis=0, keepdims=True)
            qb = q_ref[rows, :] * (qscale * jnp.exp(b))
            kk = k_ref[rows, :]
            kb = kk * jnp.exp(-b)
            kl = kk * jnp.exp(bl - b)
            vv = v_ref[rows, :]
            s_t = st[...]
            compact = s_t[0:dv, :]
            for h in range(1, N_HEADS):
                compact = compact + s_t[h * dv:(h + 1) * dv, :]
            sall_ref[c] = compact
            qx = jnp.concatenate([qb * hk[h] for h in range(N_HEADS)], axis=0)
            a = jnp.where(stril, _mm_nt(qx, kb), 0.0).astype(BF16)
            o_inter = _mm_nt(qb, s_t)
            for h in range(N_HEADS):
                vs = slice(h * dv, (h + 1) * dv)
                o_ref[rows, vs] = o_inter[:, vs] + _mm(a[h * CHUNK:(h + 1) * CHUNK, :], vv[:, vs])
            st[...] = s_t * jnp.exp(bl) + _block_diag(_mm_tn(vv, kl), hk, dv)
            return carry

        lax.fori_loop(0, nc, chunk, 0)
        for h in range(N_HEADS):
            vs = slice(h * dv, (h + 1) * dv)
            oh = o_ref[:, vs]
            on = oh * lax.rsqrt(jnp.mean(oh * oh, axis=-1, keepdims=True) + RMS_EPS)
            _, sl = _silu_parts(r_ref[:, vs])
            y_ref[:, vs] = (on * gn_ref[:, vs] * sl).astype(BF16)

    row = lambda i: (i, 0)
    return _tc_call(
        body, name=name, grid=(t // tm,),
        in_specs=[pl.BlockSpec((tm, kw), row), pl.BlockSpec((tm, kw), row), pl.BlockSpec((tm, vw), row),
                  pl.BlockSpec((tm, kw), row), pl.BlockSpec((tm, vw), row), pl.BlockSpec((1, vw), lambda i: (0, 0))],
        out_specs=[pl.BlockSpec((tm, vw), row), pl.BlockSpec((tm, vw), row),
                   pl.BlockSpec((nc, dv, kw), lambda i: (i, 0, 0))],
        out_shape=[jax.ShapeDtypeStruct((t, vw), F32), jax.ShapeDtypeStruct((t, vw), BF16),
                   jax.ShapeDtypeStruct((t // CHUNK, dv, kw), F32)],
        scratch_shapes=[pltpu.VMEM((vw, kw), F32)],
        compiler_params=_seq(1),
    )(q, k, v, loga, r, gnorm)


def _gla_bwd(dy, o, r, gnorm, q, k, v, loga, zg, sall, wgu, name):
    t, kw = q.shape
    vw = v.shape[1]
    dk, dv = kw // N_HEADS, vw // N_HEADS
    tm = _row_tile(t)
    nt = t // tm
    nc = tm // CHUNK
    qscale = dk ** -0.5

    def body(dy_ref, o_ref, r_ref, gn_ref, q_ref, k_ref, v_ref, la_ref, zg_ref, sall_ref, wgu_ref,
             dq_ref, dk_ref, dv_ref, dr_ref, dzg_ref, dwgu_ref, dbg_ref, dgn_ref, dst, do_s):
        @pl.when(pl.program_id(0) == 0)
        def _():
            dst[...] = jnp.zeros_like(dst)
            dwgu_ref[...] = jnp.zeros_like(dwgu_ref)
            dbg_ref[...] = jnp.zeros_like(dbg_ref)
            dgn_ref[...] = jnp.zeros_like(dgn_ref)

        for h in range(N_HEADS):
            vs = slice(h * dv, (h + 1) * dv)
            oh = o_ref[:, vs]
            rinv = lax.rsqrt(jnp.mean(oh * oh, axis=-1, keepdims=True) + RMS_EPS)
            on = oh * rinv
            rr = r_ref[:, vs]
            sg, sl = _silu_parts(rr)
            dyv = dy_ref[:, vs]
            gn = gn_ref[:, vs]
            dgn_ref[:, vs] += jnp.sum(dyv * on * sl, axis=0, keepdims=True)
            dr_ref[:, vs] = (dyv * on * gn * (sg * (1.0 + rr * (1.0 - sg)))).astype(BF16)
            don = dyv * gn * sl
            do_s[:, vs] = rinv * (don - on * jnp.mean(don * on, axis=-1, keepdims=True))

        hk, hv, tril, stril = _gla_masks(kw, vw)
        trif = tril.astype(F32)
        triuf = (lax.broadcasted_iota(jnp.int32, (CHUNK, CHUNK), 0)
                 <= lax.broadcasted_iota(jnp.int32, (CHUNK, CHUNK), 1)).astype(F32)
        last_row = lax.broadcasted_iota(jnp.int32, (CHUNK, 1), 0) == CHUNK - 1

        def chunk(idx, carry):
            c = nc - 1 - idx
            rows = pl.ds(pl.multiple_of(c * CHUNK, CHUNK), CHUNK)
            la = la_ref[rows, :]
            b = _mm_f32(trif, la)
            bl = jnp.sum(la, axis=0, keepdims=True)
            eb = jnp.exp(b)
            enb = jnp.exp(-b)
            ebl = jnp.exp(bl - b)
            el = jnp.exp(bl)
            qb = q_ref[rows, :] * (qscale * eb)
            kk = k_ref[rows, :]
            kb = kk * enb
            kl = kk * ebl
            vv = v_ref[rows, :]
            do = do_s[rows, :]
            compact = sall_ref[c]
            s_t = jnp.concatenate([compact * hk[h] for h in range(N_HEADS)], axis=0)
            ds_t = dst[...]
            qx = jnp.concatenate([qb * hk[h] for h in range(N_HEADS)], axis=0)
            dox = jnp.concatenate([do * hv[h] for h in range(N_HEADS)], axis=0)
            a = jnp.where(stril, _mm_nt(qx, kb), 0.0).astype(BF16)
            da = jnp.where(stril, _mm_nt(dox, vv), 0.0).astype(BF16)
            dv_ref[rows, :] = (_mm_tn(a, dox) + _mm_nt(kl, ds_t)).astype(BF16)
            dak = _mm(da, kb)
            dqb = _mm(do, s_t)
            for h in range(N_HEADS):
                dqb = dqb + dak[h * CHUNK:(h + 1) * CHUNK, :] * hk[h]
            dkb = _mm_tn(da, qx)
            dkl = _mm(vv, ds_t)
            dbl = jnp.sum(dkl * kl, axis=0, keepdims=True) + el * jnp.sum(ds_t * s_t, axis=0, keepdims=True)
            dst[...] = ds_t * el + _block_diag(_mm_tn(do, qb), hk, dv)
            dq_ref[rows, :] = (dqb * (qscale * eb)).astype(BF16)
            dk_ref[rows, :] = (dkb * enb + dkl * ebl).astype(BF16)
            db = dqb * qb - dkb * kb - dkl * kl + jnp.where(last_row, dbl, 0.0)
            dla = _mm_f32(triuf, db)
            dlogit = dla * (1.0 / GLA_GATE_TEMP) * (1.0 - jnp.exp(GLA_GATE_TEMP * la))
            dzg_ref[rows, :] = _mm_nt(dlogit, wgu_ref[...]).astype(BF16)
            dwgu_ref[...] += _mm_tn(zg_ref[rows, :], dlogit)
            dbg_ref[...] += jnp.sum(dlogit, axis=0, keepdims=True)
            return carry

        lax.fori_loop(0, nc, chunk, 0)

    row = lambda i: (nt - 1 - i, 0)
    const = lambda i: (0, 0)
    return _tc_call(
        body, name=name, grid=(nt,),
        in_specs=[pl.BlockSpec((tm, vw), row), pl.BlockSpec((tm, vw), row), pl.BlockSpec((tm, vw), row),
                  pl.BlockSpec((1, vw), const), pl.BlockSpec((tm, kw), row), pl.BlockSpec((tm, kw), row),
                  pl.BlockSpec((tm, vw), row), pl.BlockSpec((tm, kw), row), pl.BlockSpec((tm, LANE), row),
                  pl.BlockSpec((nc, dv, kw), lambda i: (nt - 1 - i, 0, 0)), pl.BlockSpec((LANE, kw), const)],
        out_specs=[pl.BlockSpec((tm, kw), row), pl.BlockSpec((tm, kw), row), pl.BlockSpec((tm, vw), row),
                   pl.BlockSpec((tm, vw), row), pl.BlockSpec((tm, LANE), row), pl.BlockSpec((LANE, kw), const),
                   pl.BlockSpec((1, kw), const), pl.BlockSpec((1, vw), const)],
        out_shape=[jax.ShapeDtypeStruct((t, kw), BF16), jax.ShapeDtypeStruct((t, kw), BF16),
                   jax.ShapeDtypeStruct((t, vw), BF16), jax.ShapeDtypeStruct((t, vw), BF16),
                   jax.ShapeDtypeStruct((t, LANE), BF16), jax.ShapeDtypeStruct((LANE, kw), F32),
                   jax.ShapeDtypeStruct((1, kw), F32), jax.ShapeDtypeStruct((1, vw), F32)],
        scratch_shapes=[pltpu.VMEM((vw, kw), F32), pltpu.VMEM((tm, vw), F32)],
        compiler_params=_seq(1),
    )(dy, o, r, gnorm, q, k, v, loga, zg, sall, wgu)


def _outproj_fwd(yp, yg, w_out, xhat, gam, bet, alpha, name):
    t, d = xhat.shape
    pw = yp.shape[1]
    tm = _row_tile(t)

    def body(yp_ref, yg_ref, w_ref, x_ref, g_ref, b_ref, xhat_ref, rstd_ref):
        h = x_ref[...] * g_ref[...] + b_ref[...]
        y = (jnp.dot(yp_ref[...], w_ref[0:pw, :], preferred_element_type=F32)
             + jnp.dot(yg_ref[...], w_ref[pw:, :], preferred_element_type=F32))
        xh, rs = _ln_stats(alpha * h + y)
        xhat_ref[...] = xh
        rstd_ref[...] = rs

    row = lambda i: (i, 0)
    vec = pl.BlockSpec((1, d), lambda i: (0, 0))
    return _tc_call(
        body, name=name, grid=(t // tm,),
        in_specs=[pl.BlockSpec((tm, pw), row), pl.BlockSpec((tm, yg.shape[1]), row),
                  pl.BlockSpec(w_out.shape, lambda i: (0, 0)), pl.BlockSpec((tm, d), row), vec, vec],
        out_specs=[pl.BlockSpec((tm, d), row), pl.BlockSpec((tm, 1), row)],
        out_shape=[jax.ShapeDtypeStruct((t, d), F32), jax.ShapeDtypeStruct((t, 1), F32)],
        compiler_params=_seq(1),
    )(yp, yg, w_out, xhat, gam, bet)


def _outproj_bwd(dh, xhat, rstd, ln_g, w_out, pw, alpha, name):
    t, d = dh.shape
    tm = _row_tile(t)
    gw = w_out.shape[0] - pw

    def body(dh_ref, xh_ref, rs_ref, g_ref, w_ref, dyb_ref, dyp_ref, dyg_ref, dres_ref, dgam_ref, dbet_ref):
        @pl.when(pl.program_id(0) == 0)
        def _():
            dgam_ref[...] = jnp.zeros_like(dgam_ref)
            dbet_ref[...] = jnp.zeros_like(dbet_ref)

        dy = dh_ref[...]
        xh = xh_ref[...]
        dr = _ln_bwd(dy, xh, rs_ref[...], g_ref[...])
        dgam_ref[...] += jnp.sum(dy * xh, axis=0, keepdims=True)
        dbet_ref[...] += jnp.sum(dy, axis=0, keepdims=True)
        drb = dr.astype(BF16)
        dyb_ref[...] = drb
        dres_ref[...] = alpha * dr
        dyp_ref[...] = _mm_nt(drb, w_ref[0:pw, :])
        dyg_ref[...] = _mm_nt(drb, w_ref[pw:, :])

    row = lambda i: (i, 0)
    vec = pl.BlockSpec((1, d), lambda i: (0, 0))
    return _tc_call(
        body, name=name, grid=(t // tm,),
        in_specs=[pl.BlockSpec((tm, d), row), pl.BlockSpec((tm, d), row), pl.BlockSpec((tm, 1), row), vec,
                  pl.BlockSpec(w_out.shape, lambda i: (0, 0))],
        out_specs=[pl.BlockSpec((tm, d), row), pl.BlockSpec((tm, pw), row), pl.BlockSpec((tm, gw), row),
                   pl.BlockSpec((tm, d), row), vec, vec],
        out_shape=[jax.ShapeDtypeStruct((t, d), BF16), jax.ShapeDtypeStruct((t, pw), F32),
                   jax.ShapeDtypeStruct((t, gw), F32), jax.ShapeDtypeStruct((t, d), F32),
                   jax.ShapeDtypeStruct((1, d), F32), jax.ShapeDtypeStruct((1, d), F32)],
        compiler_params=_seq(1),
    )(dh, xhat, rstd, ln_g, w_out)


def _loss_head(xhat, gam, bet, target, n_rows, name):
    t, d = xhat.shape
    tm = _row_tile(t)

    def body(x_ref, g_ref, b_ref, t_ref, dy_ref, loss_ref):
        i = pl.program_id(0)

        @pl.when(i == 0)
        def _():
            loss_ref[...] = jnp.zeros_like(loss_ref)

        rowi = i * tm + lax.broadcasted_iota(jnp.int32, (tm, 1), 0)
        live = (rowi >= N_META) & (rowi < N_META + n_rows)
        diff = jnp.where(live, x_ref[...] * g_ref[...] + b_ref[...] - t_ref[...], 0.0)
        dy_ref[...] = diff * (1.0 / d)
        loss_ref[...] += jnp.sum(diff * diff) * (0.5 / d)

    row = lambda i: (i, 0)
    vec = pl.BlockSpec((1, d), lambda i: (0, 0))
    return _tc_call(
        body, name=name, grid=(t // tm,),
        in_specs=[pl.BlockSpec((tm, d), row), vec, vec, pl.BlockSpec((tm, d), row)],
        out_specs=[pl.BlockSpec((tm, d), row), pl.BlockSpec((8, LANE), lambda i: (0, 0))],
        out_shape=[jax.ShapeDtypeStruct((t, d), F32), jax.ShapeDtypeStruct((8, LANE), F32)],
        compiler_params=_seq(1),
    )(xhat, gam, bet, target)


def _rows_block(r, c):
    best = r
    for cand in range(8, r, 8):
        if r % cand == 0 and cand * c * 4 <= (1 << 20):
            best = cand
    return best if best * c * 4 <= (4 << 20) else r


def _sum_slots(recvs, name):
    nl = len(recvs)
    ns, r, c = recvs[0].shape
    tr = _rows_block(r, c)

    def body(*refs):
        o_ref = refs[nl]
        for l in range(nl):
            acc = refs[l][0].astype(F32)
            for s in range(1, ns):
                acc = acc + refs[l][s].astype(F32)
            o_ref[l] = acc

    return _tc_call(
        body, name=name, grid=(r // tr,),
        in_specs=[pl.BlockSpec((ns, tr, c), lambda i: (0, i, 0))] * nl,
        out_specs=pl.BlockSpec((nl, tr, c), lambda i: (0, i, 0)),
        out_shape=jax.ShapeDtypeStruct((nl, r, c), F32),
        compiler_params=_seq(1),
    )(*recvs)


def _adamw(w, terms, m, v, name):
    nl, r, c = w.shape
    tr = _rows_block(r, c)
    nterm = len(terms)

    def body(*refs):
        w_ref = refs[0]
        t_refs = refs[1:1 + nterm]
        m_ref, v_ref, g_ref, d_ref, nm_ref, nv_ref = refs[1 + nterm:]
        g = t_refs[0][...]
        for tr_ in t_refs[1:]:
            g = g + tr_[...]
        nm = ADAM_B1 * m_ref[...] + (1.0 - ADAM_B1) * g
        nv = ADAM_B2 * v_ref[...] + (1.0 - ADAM_B2) * jnp.square(g)
        m_hat = nm / (1.0 - ADAM_B1 ** ADAM_STEP)
        v_hat = nv / (1.0 - ADAM_B2 ** ADAM_STEP)
        g_ref[...] = g
        d_ref[...] = -ADAM_LR * (m_hat / (jnp.sqrt(v_hat) + ADAM_EPS) + ADAM_WD * w_ref[...])
        nm_ref[...] = nm
        nv_ref[...] = nv

    spec = pl.BlockSpec((None, tr, c), lambda l, i: (l, i, 0))
    shp = jax.ShapeDtypeStruct((nl, r, c), F32)
    return _tc_call(
        body, name=name, grid=(nl, r // tr),
        in_specs=[spec] * (3 + nterm), out_specs=[spec] * 4, out_shape=[shp] * 4,
        compiler_params=_seq(2),
    )(w, *terms, m, v)


XY_RELATIONS = ((1, 0, 0), (0, 1, 0), (1, 1, 0))
ALL_RELATIONS = tuple((fx, fy, fc) for fx in (0, 1) for fy in (0, 1) for fc in (0, 1) if fx or fy or fc)
HBM_SPEC = pl.BlockSpec(memory_space=pltpu.HBM)
SEM_SPEC = pl.BlockSpec(memory_space=pltpu.SEMAPHORE)
DATAFLOW = pltpu.SideEffectType.DATAFLOW_SIDE_EFFECTING


def _split_call(body, **kw):
    return pl.pallas_call(body, **kw)


def _flip(v, f):
    return 1 - v if f else v


def _any_spec(n):
    return [pl.BlockSpec(memory_space=pl.ANY)] * n


def _relations(kind):
    return ALL_RELATIONS if kind == "bcast" else XY_RELATIONS


def _copies(kind, arr, land, sems):
    x, y, c = lax.axis_index("x"), lax.axis_index("y"), lax.axis_index("c")
    out = []
    for (fx, fy, fc), (send_sem, recv_sem) in zip(_relations(kind), sems):
        px, py, pc = _flip(x, fx), _flip(y, fy), _flip(c, fc)
        if kind == "bcast":
            mine, theirs = 4 * x + 2 * y + c, 4 * px + 2 * py + pc
        else:
            mine, theirs = 2 * x + y, 2 * px + py
        src = arr.at[theirs] if kind == "scatter" else arr
        both = dict(src_ref=src, send_sem=send_sem, recv_sem=recv_sem, device_id=(px, py, pc), device_id_type=MESH)
        out.append((pltpu.make_async_remote_copy(dst_ref=land.at[mine], **both),
                    pltpu.make_async_remote_copy(dst_ref=land.at[theirs], **both)))
    return out


def _sem_pairs(kinds, sems):
    out, at = [], 0
    for kind in kinds:
        nrel = len(_relations(kind))
        out.append([(sems[at + 2 * r], sems[at + 2 * r + 1]) for r in range(nrel)])
        at += 2 * nrel
    return out


def _exchange_start(name, kinds, arrs, lands):
    n = len(arrs)
    nsem = sum(2 * len(_relations(kd)) for kd in kinds)

    def body(*refs):
        a_refs, l_refs = refs[:n], refs[n:2 * n]
        pairs = _sem_pairs(kinds, refs[2 * n:2 * n + nsem])
        token = refs[-1]
        for k in range(n):
            for send, _ in _copies(kinds[k], a_refs[k], l_refs[k], pairs[k]):
                send.start()
        token[...] = jnp.zeros_like(token)

    thru = [pltpu.HBM(a.shape, a.dtype) for a in list(arrs) + list(lands)]
    outs = _split_call(
        body, name=name,
        out_shape=(*[pltpu.SemaphoreType.DMA(())] * nsem, *thru, jax.ShapeDtypeStruct((8, LANE), F32)),
        in_specs=[HBM_SPEC] * (2 * n),
        out_specs=(*[SEM_SPEC] * nsem, *[HBM_SPEC] * (2 * n), pl.BlockSpec(memory_space=pltpu.VMEM)),
        input_output_aliases={i: nsem + i for i in range(2 * n)},
        compiler_params=pltpu.CompilerParams(has_side_effects=DATAFLOW),
    )(*[pltpu.with_memory_space_constraint(a, pltpu.HBM) for a in list(arrs) + list(lands)])
    return dict(kinds=kinds, sems=outs[:nsem], arrs=outs[nsem:nsem + n], lands=outs[nsem + n:nsem + 2 * n],
                token=outs[-1])


def _exchange_wait(name, st, after):
    kinds = st["kinds"]
    n = len(kinds)
    nsem = len(st["sems"])

    def body(*refs):
        a_refs, l_refs = refs[:n], refs[n:2 * n]
        pairs = _sem_pairs(kinds, refs[2 * n:2 * n + nsem])
        for k in range(n):
            for _, arrival in _copies(kinds[k], a_refs[k], l_refs[k], pairs[k]):
                arrival.wait_send()
                arrival.wait_recv()

    ins = list(st["arrs"]) + list(st["lands"])
    outs = _split_call(
        body, name=name,
        out_shape=[pltpu.HBM(a.shape, a.dtype) for a in ins],
        in_specs=[HBM_SPEC] * (2 * n) + [SEM_SPEC] * nsem + [pl.BlockSpec(memory_space=pl.ANY)],
        out_specs=[HBM_SPEC] * (2 * n),
        input_output_aliases={i: i for i in range(2 * n)},
        compiler_params=pltpu.CompilerParams(has_side_effects=DATAFLOW),
    )(*ins, *st["sems"], after)
    return outs[n:]


def _landing(own, slot, nslot):
    return lax.dynamic_update_slice(lax.empty((nslot,) + own.shape, own.dtype), own[None], (slot,) + (0,) * own.ndim)


def _swap_sibling(parts):
    n = len(parts)

    def body(*refs):
        ins, outs = refs[:n], refs[n:2 * n]
        send_sems, recv_sems = refs[2 * n:]
        sib = (lax.axis_index("x"), lax.axis_index("y"), 1 - lax.axis_index("c"))
        cps = [pltpu.make_async_remote_copy(src_ref=ins[k], dst_ref=outs[k], send_sem=send_sems.at[k],
                                            recv_sem=recv_sems.at[k], device_id=sib, device_id_type=MESH)
               for k in range(n)]
        for cp in cps:
            cp.start()
        for cp in cps:
            cp.wait_recv()
        for cp in cps:
            cp.wait_send()

    return _comm_call(
        body, name="swap_sibling",
        in_specs=_any_spec(n), out_specs=_any_spec(n),
        out_shape=[jax.ShapeDtypeStruct(p.shape, p.dtype) for p in parts],
        scratch_shapes=[pltpu.SemaphoreType.DMA((n,)), pltpu.SemaphoreType.DMA((n,))],
    )(*parts)


def _col_shards(a, n=N_SHARD):
    r, c = a.shape
    return a.reshape(r, n, c // n).transpose(1, 0, 2)


def _from_col_shards(a):
    n, r, cs = a.shape
    return a.transpose(1, 0, 2).reshape(r, n * cs)


def kernel(x, meta_tokens, ffn1_w_gate, ffn1_w_up, ffn1_w_down, ln1_g, ln1_b, w_in, w_gate_up, b_gate, w_pool, pool_scale, gla_norm_g, w_out, ln2_g, ln2_b, ffn2_w_gate, ffn2_w_up, ffn2_w_down, ln3_g, ln3_b, loss_target, m_meta_tokens, m_ffn1_w_gate, m_ffn1_w_up, m_ffn1_w_down, m_ln1_g, m_ln1_b, m_w_in, m_w_gate_up, m_b_gate, m_w_pool, m_pool_scale, m_gla_norm_g, m_w_out, m_ln2_g, m_ln2_b, m_ffn2_w_gate, m_ffn2_w_up, m_ffn2_w_down, m_ln3_g, m_ln3_b, v_meta_tokens, v_ffn1_w_gate, v_ffn1_w_up, v_ffn1_w_down, v_ln1_g, v_ln1_b, v_w_in, v_w_gate_up, v_b_gate, v_w_pool, v_pool_scale, v_gla_norm_g, v_w_out, v_ln2_g, v_ln2_b, v_ffn2_w_gate, v_ffn2_w_up, v_ffn2_w_down, v_ln3_g, v_ln3_b):
    w = dict(meta_tokens=meta_tokens, ffn1_w_gate=ffn1_w_gate, ffn1_w_up=ffn1_w_up, ffn1_w_down=ffn1_w_down,
             ln1_g=ln1_g, ln1_b=ln1_b, w_in=w_in, w_gate_up=w_gate_up, b_gate=b_gate, w_pool=w_pool,
             pool_scale=pool_scale, gla_norm_g=gla_norm_g, w_out=w_out, ln2_g=ln2_g, ln2_b=ln2_b,
             ffn2_w_gate=ffn2_w_gate, ffn2_w_up=ffn2_w_up, ffn2_w_down=ffn2_w_down, ln3_g=ln3_g, ln3_b=ln3_b)
    mom1 = dict(meta_tokens=m_meta_tokens, ffn1_w_gate=m_ffn1_w_gate, ffn1_w_up=m_ffn1_w_up,
                ffn1_w_down=m_ffn1_w_down, ln1_g=m_ln1_g, ln1_b=m_ln1_b, w_in=m_w_in, w_gate_up=m_w_gate_up,
                b_gate=m_b_gate, w_pool=m_w_pool, pool_scale=m_pool_scale, gla_norm_g=m_gla_norm_g, w_out=m_w_out,
                ln2_g=m_ln2_g, ln2_b=m_ln2_b, ffn2_w_gate=m_ffn2_w_gate, ffn2_w_up=m_ffn2_w_up,
                ffn2_w_down=m_ffn2_w_down, ln3_g=m_ln3_g, ln3_b=m_ln3_b)
    mom2 = dict(meta_tokens=v_meta_tokens, ffn1_w_gate=v_ffn1_w_gate, ffn1_w_up=v_ffn1_w_up,
                ffn1_w_down=v_ffn1_w_down, ln1_g=v_ln1_g, ln1_b=v_ln1_b, w_in=v_w_in, w_gate_up=v_w_gate_up,
                b_gate=v_b_gate, w_pool=v_w_pool, pool_scale=v_pool_scale, gla_norm_g=v_gla_norm_g, w_out=v_w_out,
                ln2_g=v_ln2_g, ln2_b=v_ln2_b, ffn2_w_gate=v_ffn2_w_gate, ffn2_w_up=v_ffn2_w_up,
                ffn2_w_down=v_ffn2_w_down, ln3_g=v_ln3_g, ln3_b=v_ln3_b)

    xs = x[0]
    s_len, d = xs.shape
    nl = ln1_g.shape[0]
    alpha = (2.0 * nl) ** 0.25
    t_real = N_META + s_len
    t_pad = -(-t_real // LANE) * LANE
    pw = pool_scale.shape[1]
    kw = b_gate.shape[1]
    vw = gla_norm_g.shape[1]
    rank = w_gate_up.shape[1]
    widths = (pw, kw, kw, vw, vw)
    n_main = sum(widths)
    dff_s = ffn1_w_gate.shape[2]

    me_xy = 2 * lax.axis_index("x") + lax.axis_index("y")
    me_all = 2 * me_xy + lax.axis_index("c")
    ffn1_names = ("ffn1_w_gate", "ffn1_w_up", "ffn1_w_down")
    mix_names = ("w_out", "w_gate_up", "w_in")
    ffn2_names = ("ffn2_w_gate", "ffn2_w_up", "ffn2_w_down")

    stages = [[("meta_tokens", None)] + [(n, 0) for n in ffn1_names], [(n, 0) for n in mix_names + ffn2_names]]
    stages += [[(n, l) for n in BIG] for l in range(1, nl)]
    gathers = []
    for si, items in enumerate(stages):
        own = [meta_tokens if l is None else w[n][l].astype(BF16) for n, l in items]
        gathers.append(_exchange_start(f"gather_start_{si}", ["gather"] * len(items), own,
                                       [_landing(a, me_xy, N_SHARD) for a in own]))
    all_started = functools.reduce(jnp.add, [g["token"] for g in gathers])
    wa = {}

    def arrive(si, after):
        for item, a in zip(stages[si], _exchange_wait(f"gather_wait_{si}", gathers[si], after)):
            wa[item] = a

    def mixer_weights(l):
        wi = _from_col_shards(wa["w_in", l])
        return dict(w_main=wi[:, :n_main], w_lr=jnp.pad(wi[:, n_main:], ((0, 0), (0, LANE - rank))),
                    wgu=jnp.pad(_from_col_shards(wa["w_gate_up", l]), ((0, LANE - rank), (0, 0))),
                    wout=wa["w_out", l].reshape(-1, d))

    wp16 = w_pool.astype(BF16)
    ones = jnp.ones((1, d), F32)
    zeros = jnp.zeros((1, d), F32)
    target = jnp.concatenate([jnp.zeros((N_META, d), F32), loss_target[0], jnp.zeros((t_pad - t_real, d), F32)], axis=0)

    arrive(0, all_started)
    meta_full = _from_col_shards(wa["meta_tokens", None])
    h0 = jnp.concatenate([meta_full, xs, jnp.zeros((t_pad - t_real, d), F32)], axis=0)

    saved, mw = [], []
    cur, cur_g, cur_b = h0, ones, zeros
    for l in range(nl):
        s = {}
        xh1, rs1, hb0, g1, u1 = _ffn_fwd(cur, cur_g, cur_b, wa["ffn1_w_gate", l], wa["ffn1_w_up", l],
                                         wa["ffn1_w_down", l], alpha, f"ffn1_fwd_{l}")
        if l == 0:
            arrive(1, xh1)
        mw.append(mixer_weights(l))
        up, q, k, v, r, zg, la, hb1 = _inproj_fwd(xh1, ln1_g[l:l + 1], ln1_b[l:l + 1], mw[l]["w_main"], mw[l]["w_lr"],
                                                  mw[l]["wgu"], b_gate[l:l + 1], widths, f"inproj_fwd_{l}")
        yp, pb = _pool_fwd(up, wp16[l], pool_scale[l:l + 1], f"pool_fwd_{l}")
        o, yg, sall = _gla_fwd(q, k, v, la, r, gla_norm_g[l:l + 1], f"gla_fwd_{l}")
        xh2, rs2 = _outproj_fwd(yp, yg, mw[l]["wout"], xh1, ln1_g[l:l + 1], ln1_b[l:l + 1], alpha, f"outproj_fwd_{l}")
        if l + 1 < nl:
            arrive(l + 2, xh2)
        xh3, rs3, hb2, g2, u2 = _ffn_fwd(xh2, ln2_g[l:l + 1], ln2_b[l:l + 1], wa["ffn2_w_gate", l], wa["ffn2_w_up", l],
                                         wa["ffn2_w_down", l], alpha, f"ffn2_fwd_{l}")
        s.update(xh1=xh1, rs1=rs1, hb0=hb0, g1=g1, u1=u1, q=q, k=k, v=v, r=r, zg=zg, la=la, hb1=hb1, yp=yp, pb=pb,
                 o=o, yg=yg, sall=sall, xh2=xh2, rs2=rs2, xh3=xh3, rs3=rs3, hb2=hb2, g2=g2, u2=u2)
        saved.append(s)
        cur, cur_g, cur_b = xh3, ln3_g[l:l + 1], ln3_b[l:l + 1]

    dh, loss_acc = _loss_head(cur, cur_g, cur_b, target, s_len, "loss_head")
    loss = lax.psum(loss_acc[0, 0], ("x", "y", "c"))

    small_grads = {n: [None] * nl for n in SMALL}
    scatters = []

    def depart(name, items, grads, kinds=None):
        lands = [_landing(g if kd == "bcast" else lax.dynamic_index_in_dim(g, me_xy, 0, keepdims=False),
                          me_all if kd == "bcast" else me_xy, N_DEV if kd == "bcast" else N_SHARD)
                 for g, kd in zip(grads, kinds or ["scatter"] * len(grads))]
        st = _exchange_start(name, kinds or ["scatter"] * len(grads), grads, lands)
        scatters.append((name, items, st))
        return st["token"][0:1, 0:1]

    for l in reversed(range(nl)):
        s = saved[l]
        dh, dfb, dgb, dub, act, dgam, dbet = _ffn_bwd(dh, s["xh3"], s["rs3"], ln3_g[l:l + 1], s["g2"], s["u2"],
                                                      wa["ffn2_w_gate", l], wa["ffn2_w_up", l], wa["ffn2_w_down", l],
                                                      alpha, f"ffn2_bwd_{l}")
        small_grads["ln3_g"][l], small_grads["ln3_b"][l] = dgam, dbet
        gone = depart(f"scatter_start_ffn2_{l}", [(n, l) for n in ffn2_names],
                      [_wgrad(s["hb2"], dgb, d, dff_s, True, f"ffn2_dwg_{l}"),
                       _wgrad(s["hb2"], dub, d, dff_s, True, f"ffn2_dwu_{l}"),
                       _wgrad(act, dfb, dff_s, d, False, f"ffn2_dwd_{l}").reshape(N_SHARD, dff_s, d)])

        dyb, dyp, dyg, dres, dgam, dbet = _outproj_bwd(dh, s["xh2"], s["rs2"], ln2_g[l:l + 1] + gone, mw[l]["wout"],
                                                       pw, alpha, f"outproj_bwd_{l}")
        small_grads["ln2_g"][l], small_grads["ln2_b"][l] = dgam, dbet
        dwo = jnp.concatenate([_wgrad(s["yp"], dyb, pw, d, False, f"dwout_pool_{l}"),
                               _wgrad(s["yg"], dyb, vw, d, False, f"dwout_gla_{l}")], axis=0)
        dq, dk, dv, dr, dzg, dwgu, dbg, dgn = _gla_bwd(dyg, s["o"], s["r"], gla_norm_g[l:l + 1], s["q"], s["k"],
                                                       s["v"], s["la"], s["zg"], s["sall"], mw[l]["wgu"],
                                                       f"gla_bwd_{l}")
        dup, dwp, dsc = _pool_bwd(dyp, s["pb"], wp16[l], pool_scale[l:l + 1], f"pool_bwd_{l}")
        small_grads["b_gate"][l], small_grads["gla_norm_g"][l] = dbg, dgn
        small_grads["w_pool"][l], small_grads["pool_scale"][l] = dwp, dsc
        dh, dz = _inproj_bwd(dres, [dup, dq, dk, dv, dr], dzg, mw[l]["w_main"], mw[l]["w_lr"], f"inproj_bwd_{l}")
        dwi = jnp.concatenate([_wgrad(s["hb1"], dz, d, 512, False, f"dwin_main_{l}"),
                               _wgrad(s["hb1"], dzg, d, LANE, False, f"dwin_lr_{l}")[:, :rank]], axis=1)
        gone = depart(f"scatter_start_mix_{l}", [(n, l) for n in mix_names],
                      [dwo.reshape(N_SHARD, -1, d), _col_shards(dwgu[:rank].astype(BF16)), _col_shards(dwi)])

        dh, dfb, dgb, dub, act, dgam, dbet = _ffn_bwd(dh, s["xh1"], s["rs1"], ln1_g[l:l + 1] + gone, s["g1"], s["u1"],
                                                      wa["ffn1_w_gate", l], wa["ffn1_w_up", l], wa["ffn1_w_down", l],
                                                      alpha, f"ffn1_bwd_{l}")
        small_grads["ln1_g"][l], small_grads["ln1_b"][l] = dgam, dbet
        gone = depart(f"scatter_start_ffn1_{l}", [(n, l) for n in ffn1_names],
                      [_wgrad(s["hb0"], dgb, d, dff_s, True, f"ffn1_dwg_{l}"),
                       _wgrad(s["hb0"], dub, d, dff_s, True, f"ffn1_dwu_{l}"),
                       _wgrad(act, dfb, dff_s, d, False, f"ffn1_dwd_{l}").reshape(N_SHARD, dff_s, d)])
        if l:
            ln3_g = ln3_g.at[l - 1:l].add(gone)

    grad_x = dh[N_META:t_real][None]

    def pack(parts):
        flat = jnp.concatenate([parts[n].reshape(-1) for n in SMALL])
        return flat.reshape(-1, LANE)

    small_vec = pack({n: jnp.stack(small_grads[n]) for n in SMALL})
    depart("scatter_start_last", [("meta_tokens", 0), ("small", 0)],
           [_col_shards(dh[:N_META].astype(BF16)), small_vec], ["scatter", "bcast"])
    all_gone = scatters[-1][2]["token"]

    recv = {}
    for name, items, st in scatters:
        for item, a in zip(items, _exchange_wait(name.replace("start", "wait"), st, all_gone)):
            recv[item] = a
    small_recv = recv["small", 0]
    sharded = ("meta_tokens",) + BIG
    partial = [_sum_slots([recv[n, l] for l in range(1 if n == "meta_tokens" else nl)], f"sum_{n}") for n in sharded]
    other = _swap_sibling(partial)

    results = {}
    for n, mine, theirs in zip(sharded, partial, other):
        shp = w[n].shape
        as3 = lambda a: a.reshape(mine.shape)
        outs = _adamw(as3(w[n]), [mine, theirs], as3(mom1[n]), as3(mom2[n]), f"adamw_{n}")
        results[n] = [o.reshape(shp) for o in outs]
    small_terms = [small_recv[i][None] for i in range(N_DEV)]
    souts = _adamw(pack(w)[None], small_terms, pack(mom1)[None], pack(mom2)[None], "adamw_small")
    off = 0
    for n in SMALL:
        size = w[n].size
        results[n] = [o.reshape(-1)[off:off + size].reshape(w[n].shape) for o in souts]
        off += size

    out = [loss, grad_x]
    for part in range(4):
        out += [results[n][part] for n in WEIGHTS]
    return tuple(out)
```

```python
import functools

import jax
import jax.numpy as jnp
from jax import lax
from jax.experimental import pallas as pl
from jax.experimental.pallas import tpu as pltpu

F32 = jnp.float32
BF16 = jnp.bfloat16
MESH = pl.DeviceIdType.MESH

N_META = 16
POOL_WINDOWS = (2, 4, 8, 16)
POOL_HALO = 16
N_HEADS = 4
GLA_GATE_TEMP = 16.0
CHUNK = 64
LN_EPS = 1e-5
RMS_EPS = 1e-6
ADAM_LR = 0.001
ADAM_B1 = 0.9
ADAM_B2 = 0.999
ADAM_EPS = 1e-08
ADAM_WD = 0.01
ADAM_STEP = 10
LANE = 128
ROW_TILE = 640
N_SHARD = 4
N_DEV = 8

BIG = ("ffn1_w_gate", "ffn1_w_up", "ffn1_w_down", "w_in", "w_gate_up", "w_out",
       "ffn2_w_gate", "ffn2_w_up", "ffn2_w_down")
SMALL = ("ln1_g", "ln1_b", "b_gate", "w_pool", "pool_scale", "gla_norm_g", "ln2_g", "ln2_b", "ln3_g", "ln3_b")
WEIGHTS = ("meta_tokens", "ffn1_w_gate", "ffn1_w_up", "ffn1_w_down", "ln1_g", "ln1_b", "w_in", "w_gate_up",
           "b_gate", "w_pool", "pool_scale", "gla_norm_g", "w_out", "ln2_g", "ln2_b", "ffn2_w_gate",
           "ffn2_w_up", "ffn2_w_down", "ln3_g", "ln3_b")


def _tc_call(body, **kw):
    return pl.pallas_call(body, **kw)


def _comm_call(body, **kw):
    return pl.pallas_call(body, **kw)


def _seq(n):
    return pltpu.CompilerParams(dimension_semantics=("arbitrary",) * n)


def _mm(a, b):
    return jnp.dot(a.astype(BF16), b.astype(BF16), preferred_element_type=F32)


def _mm_nt(a, b):
    return lax.dot_general(a.astype(BF16), b.astype(BF16), (((1,), (1,)), ((), ())), preferred_element_type=F32)


def _mm_tn(a, b):
    return lax.dot_general(a.astype(BF16), b.astype(BF16), (((0,), (0,)), ((), ())), preferred_element_type=F32)


def _mm_f32(a, b):
    return jnp.dot(a, b, precision=lax.Precision.HIGHEST, preferred_element_type=F32)


def _row_tile(t):
    tm = min(ROW_TILE, t)
    while t % tm:
        tm -= LANE
    return tm


def _silu_parts(g):
    sg = jax.nn.sigmoid(g)
    return sg, g * sg


def _ln_stats(r):
    mu = jnp.mean(r, axis=-1, keepdims=True)
    rc = r - mu
    var = jnp.mean(rc * rc, axis=-1, keepdims=True)
    rs = lax.rsqrt(var + LN_EPS)
    return rc * rs, rs


def _ln_bwd(dy, xh, rs, gam):
    dyg = dy * gam
    c1 = jnp.mean(dyg, axis=-1, keepdims=True)
    c2 = jnp.mean(dyg * xh, axis=-1, keepdims=True)
    return rs * (dyg - c1 - xh * c2)


def _ffn_fwd(xin, gam_in, bet_in, wg, wu, wd, alpha, name):
    t, d = xin.shape
    nj, tf, _ = wg.shape
    tm = _row_tile(t)

    def body(x_ref, gi_ref, bi_ref, wg_ref, wu_ref, wd_ref, xhat_ref, rstd_ref, hb_ref, go_ref, uo_ref, acc, hbs):
        j = pl.program_id(1)

        @pl.when(j == 0)
        def _():
            hb = (x_ref[...] * gi_ref[...] + bi_ref[...]).astype(BF16)
            hbs[...] = hb
            hb_ref[...] = hb
            acc[...] = jnp.zeros_like(acc)

        hb = hbs[...]
        g = _mm_nt(hb, wg_ref[...])
        u = _mm_nt(hb, wu_ref[...])
        _, sl = _silu_parts(g)
        go_ref[...] = g.astype(BF16)
        uo_ref[...] = u.astype(BF16)
        acc[...] += jnp.dot((sl * u).astype(BF16), wd_ref[...], preferred_element_type=F32)

        @pl.when(j == nj - 1)
        def _():
            h = x_ref[...] * gi_ref[...] + bi_ref[...]
            xhat, rs = _ln_stats(alpha * h + 0.5 * acc[...])
            xhat_ref[...] = xhat
            rstd_ref[...] = rs

    row = lambda i, j: (i, 0)
    vec = pl.BlockSpec((1, d), lambda i, j: (0, 0))
    return _tc_call(
        body, name=name, grid=(t // tm, nj),
        in_specs=[pl.BlockSpec((tm, d), row), vec, vec] + [pl.BlockSpec((None, tf, d), lambda i, j: (j, 0, 0))] * 3,
        out_specs=[pl.BlockSpec((tm, d), row), pl.BlockSpec((tm, 1), row), pl.BlockSpec((tm, d), row),
                   pl.BlockSpec((None, tm, tf), lambda i, j: (j, i, 0)),
                   pl.BlockSpec((None, tm, tf), lambda i, j: (j, i, 0))],
        out_shape=[jax.ShapeDtypeStruct((t, d), F32), jax.ShapeDtypeStruct((t, 1), F32),
                   jax.ShapeDtypeStruct((t, d), BF16), jax.ShapeDtypeStruct((nj, t, tf), BF16),
                   jax.ShapeDtypeStruct((nj, t, tf), BF16)],
        scratch_shapes=[pltpu.VMEM((tm, d), F32), pltpu.VMEM((tm, d), BF16)],
        compiler_params=_seq(2),
    )(xin, gam_in, bet_in, wg, wu, wd)


def _ffn_bwd(dh, xhat, rstd, ln_g, gb, ub, wg, wu, wd, alpha, name):
    t, d = dh.shape
    nj, tf, _ = wg.shape
    tm = _row_tile(t)

    def body(dh_ref, xh_ref, rs_ref, g_ref, gb_ref, ub_ref, wg_ref, wu_ref, wd_ref,
             dhin_ref, df_ref, dg_ref, du_ref, act_ref, dgam_ref, dbet_ref, dr_s, df_s, acc):
        i = pl.program_id(0)
        j = pl.program_id(1)

        @pl.when(j == 0)
        def _():
            dy = dh_ref[...]
            xh = xh_ref[...]
            dr = _ln_bwd(dy, xh, rs_ref[...], g_ref[...])
            dr_s[...] = dr
            dfb = (0.5 * dr).astype(BF16)
            df_s[...] = dfb
            df_ref[...] = dfb
            acc[...] = jnp.zeros_like(acc)

            @pl.when(i == 0)
            def _():
                dgam_ref[...] = jnp.zeros_like(dgam_ref)
                dbet_ref[...] = jnp.zeros_like(dbet_ref)

            dgam_ref[...] += jnp.sum(dy * xh, axis=0, keepdims=True)
            dbet_ref[...] += jnp.sum(dy, axis=0, keepdims=True)

        dact = _mm_nt(df_s[...], wd_ref[...])
        g = gb_ref[...].astype(F32)
        u = ub_ref[...].astype(F32)
        sg, sl = _silu_parts(g)
        dg = (dact * u * (sg * (1.0 + g * (1.0 - sg)))).astype(BF16)
        du = (dact * sl).astype(BF16)
        dg_ref[...] = dg
        du_ref[...] = du
        act_ref[...] = (sl * u).astype(BF16)
        acc[...] += _mm(dg, wg_ref[...]) + _mm(du, wu_ref[...])

        @pl.when(j == nj - 1)
        def _():
            dhin_ref[...] = alpha * dr_s[...] + acc[...]

    row = lambda i, j: (i, 0)
    col = pl.BlockSpec((None, tm, tf), lambda i, j: (j, i, 0))
    vec = pl.BlockSpec((1, d), lambda i, j: (0, 0))
    ff = jax.ShapeDtypeStruct((nj, t, tf), BF16)
    return _tc_call(
        body, name=name, grid=(t // tm, nj),
        in_specs=[pl.BlockSpec((tm, d), row), pl.BlockSpec((tm, d), row), pl.BlockSpec((tm, 1), row), vec,
                  col, col] + [pl.BlockSpec((None, tf, d), lambda i, j: (j, 0, 0))] * 3,
        out_specs=[pl.BlockSpec((tm, d), row), pl.BlockSpec((tm, d), row),
                   col, col, col, vec, vec],
        out_shape=[jax.ShapeDtypeStruct((t, d), F32), jax.ShapeDtypeStruct((t, d), BF16), ff, ff, ff,
                   jax.ShapeDtypeStruct((1, d), F32), jax.ShapeDtypeStruct((1, d), F32)],
        scratch_shapes=[pltpu.VMEM((tm, d), F32), pltpu.VMEM((tm, d), BF16), pltpu.VMEM((tm, d), F32)],
        compiler_params=_seq(2),
    )(dh, xhat, rstd, ln_g, gb, ub, wg, wu, wd)


def _wgrad(a, b, tmm, tn, name, after=None):
    t = a.shape[-2]
    m = a.shape[-1] * (a.shape[0] if a.ndim == 3 else 1)
    n = b.shape[-1] * (b.shape[0] if b.ndim == 3 else 1)
    tk = _row_tile(t)
    nk = t // tk
    extra = [] if after is None else [after]

    def body(a_ref, b_ref, *rest):
        o_ref, acc = rest[len(extra):]
        k = pl.program_id(2)

        @pl.when(k == 0)
        def _():
            acc[...] = jnp.zeros_like(acc)

        acc[...] += _mm_tn(a_ref[...], b_ref[...])

        @pl.when(k == nk - 1)
        def _():
            o_ref[...] = acc[...].astype(o_ref.dtype)

    return _tc_call(
        body, name=name, grid=(m // tmm, n // tn, nk),
        in_specs=[pl.BlockSpec((None, tk, tmm), lambda i, j, k: (i, k, 0)) if a.ndim == 3
                  else pl.BlockSpec((tk, tmm), lambda i, j, k: (k, i)),
                  pl.BlockSpec((None, tk, tn), lambda i, j, k: (j, k, 0)) if b.ndim == 3
                  else pl.BlockSpec((tk, tn), lambda i, j, k: (k, j))] + [pl.BlockSpec(memory_space=pl.ANY)] * len(extra),
        out_specs=pl.BlockSpec((tmm, tn), lambda i, j, k: (i, j)),
        out_shape=jax.ShapeDtypeStruct((m, n), BF16),
        scratch_shapes=[pltpu.VMEM((tmm, tn), F32)],
        compiler_params=_seq(3),
    )(a, b, *extra)


def _inproj_fwd(xhat, gam, bet, w_main, w_lr, wgu, b_gate, widths, name):
    t, d = xhat.shape
    tm = _row_tile(t)
    kw = wgu.shape[1]
    offs = [0]
    for w in widths:
        offs.append(offs[-1] + w)

    def body(x_ref, g_ref, b_ref, wm_ref, wl_ref, wgu_ref, bg_ref, *outs):
        piece_refs, (zg_ref, la_ref, hb_ref) = outs[:len(widths)], outs[len(widths):]
        hb = (x_ref[...] * g_ref[...] + b_ref[...]).astype(BF16)
        hb_ref[...] = hb
        for p, ref in enumerate(piece_refs):
            ref[...] = _mm_nt(hb, wm_ref[offs[p]:offs[p + 1], :])
        zg = _mm_nt(hb, wl_ref[...])
        zg_ref[...] = zg
        logit = _mm(zg, wgu_ref[...]) + bg_ref[...]
        la_ref[...] = (jnp.minimum(logit, 0.0) - jnp.log(1.0 + jnp.exp(-jnp.abs(logit)))) * (1.0 / GLA_GATE_TEMP)

    row = lambda i: (i, 0)
    full = lambda a: pl.BlockSpec(a.shape, lambda i: (0,) * a.ndim)
    out_w = list(widths) + [LANE, kw]
    return _tc_call(
        body, name=name, grid=(t // tm,),
        in_specs=[pl.BlockSpec((tm, d), row), full(gam), full(bet), full(w_main), full(w_lr), full(wgu), full(b_gate)],
        out_specs=[pl.BlockSpec((tm, w), row) for w in out_w] + [pl.BlockSpec((tm, d), row)],
        out_shape=[jax.ShapeDtypeStruct((t, w), F32) for w in out_w] + [jax.ShapeDtypeStruct((t, d), BF16)],
        compiler_params=_seq(1),
    )(xhat, gam, bet, w_main, w_lr, wgu, b_gate)


def _inproj_bwd(dh_part, pieces, dzg, w_main, w_lr, name):
    t, d = dh_part.shape
    tm = _row_tile(t)
    widths = [p.shape[1] for p in pieces]
    offs = [0]
    for w in widths:
        offs.append(offs[-1] + w)

    def body(*refs):
        dhp_ref = refs[0]
        p_refs = refs[1:1 + len(widths)]
        dzg_ref, wm_ref, wl_ref, dh_ref, dz_ref = refs[1 + len(widths):]
        acc = dhp_ref[...] + _mm(dzg_ref[...], wl_ref[...])
        for p, ref in enumerate(p_refs):
            v = ref[...]
            dz_ref[:, offs[p]:offs[p + 1]] = v
            acc += _mm(v, wm_ref[offs[p]:offs[p + 1], :])
        dh_ref[...] = acc

    row = lambda i: (i, 0)
    full = lambda a: pl.BlockSpec(a.shape, lambda i: (0,) * a.ndim)
    return _tc_call(
        body, name=name, grid=(t // tm,),
        in_specs=[pl.BlockSpec((tm, d), row)] + [pl.BlockSpec((tm, w), row) for w in widths]
                 + [pl.BlockSpec((tm, LANE), row), full(w_main), full(w_lr)],
        out_specs=[pl.BlockSpec((tm, d), row), pl.BlockSpec((tm, offs[-1]), row)],
        out_shape=[jax.ShapeDtypeStruct((t, d), F32), jax.ShapeDtypeStruct((t, offs[-1]), BF16)],
        compiler_params=_seq(1),
    )(dh_part, *pieces, dzg, w_main, w_lr)


def _pool_cnt(tile, tm, w):
    t = tile * tm + lax.broadcasted_iota(jnp.int32, (tm, 1), 0)
    return jnp.minimum(t + 1, w).astype(F32)


def _pool_fwd(u, wp, scale, name):
    t, pw = u.shape
    tm = _row_tile(t)
    gd = wp.shape[1]

    def body(u_ref, wp_ref, sc_ref, y_ref, p_ref, ext):
        i = pl.program_id(0)

        @pl.when(i == 0)
        def _():
            ext[0:POOL_HALO, :] = jnp.zeros((POOL_HALO, pw), F32)

        ext[POOL_HALO:POOL_HALO + tm, :] = u_ref[...]
        for gi, w in enumerate(POOL_WINDOWS):
            cols = slice(gi * gd, (gi + 1) * gd)
            s = ext[pl.ds(POOL_HALO, tm), cols]
            tot = s
            for back in range(1, w):
                tot = tot + ext[pl.ds(POOL_HALO - back, tm), cols]
            p = (tot / _pool_cnt(i, tm, w) - s).astype(BF16)
            p_ref[:, cols] = p
            y_ref[:, cols] = (jnp.dot(p, wp_ref[gi], preferred_element_type=F32) * sc_ref[:, cols]).astype(BF16)
        ext[0:POOL_HALO, :] = ext[tm:tm + POOL_HALO, :]

    row = lambda i: (i, 0)
    return _tc_call(
        body, name=name, grid=(t // tm,),
        in_specs=[pl.BlockSpec((tm, pw), row), pl.BlockSpec(wp.shape, lambda i: (0, 0, 0)),
                  pl.BlockSpec((1, pw), lambda i: (0, 0))],
        out_specs=[pl.BlockSpec((tm, pw), row), pl.BlockSpec((tm, pw), row)],
        out_shape=[jax.ShapeDtypeStruct((t, pw), BF16), jax.ShapeDtypeStruct((t, pw), BF16)],
        scratch_shapes=[pltpu.VMEM((tm + POOL_HALO, pw), F32)],
        compiler_params=_seq(1),
    )(u, wp, scale)


def _pool_bwd(dy, pb, wp, scale, name):
    t, pw = dy.shape
    tm = _row_tile(t)
    nt = t // tm
    gd = wp.shape[1]

    def body(dy_ref, p_ref, wp_ref, sc_ref, du_ref, dwp_ref, dsc_ref, ext):
        i = pl.program_id(0)
        tile = nt - 1 - i

        @pl.when(i == 0)
        def _():
            ext[tm:tm + POOL_HALO, :] = jnp.zeros((POOL_HALO, pw), F32)
            dwp_ref[...] = jnp.zeros_like(dwp_ref)
            dsc_ref[...] = jnp.zeros_like(dsc_ref)

        dps = []
        for gi, w in enumerate(POOL_WINDOWS):
            cols = slice(gi * gd, (gi + 1) * gd)
            dyv = dy_ref[:, cols]
            p = p_ref[:, cols]
            dpre = (dyv * sc_ref[:, cols]).astype(BF16)
            dsc_ref[:, cols] += jnp.sum(dyv * jnp.dot(p, wp_ref[gi], preferred_element_type=F32), axis=0, keepdims=True)
            dwp_ref[gi] += _mm_tn(p, dpre)
            dp = _mm_nt(dpre, wp_ref[gi])
            dps.append(dp)
            ext[0:tm, cols] = dp / _pool_cnt(tile, tm, w)
        for gi, w in enumerate(POOL_WINDOWS):
            cols = slice(gi * gd, (gi + 1) * gd)
            tot = ext[pl.ds(0, tm), cols]
            for fwd in range(1, w):
                tot = tot + ext[pl.ds(fwd, tm), cols]
            du_ref[:, cols] = (tot - dps[gi]).astype(BF16)
        ext[tm:tm + POOL_HALO, :] = ext[0:POOL_HALO, :]

    row = lambda i: (nt - 1 - i, 0)
    return _tc_call(
        body, name=name, grid=(nt,),
        in_specs=[pl.BlockSpec((tm, pw), row), pl.BlockSpec((tm, pw), row),
                  pl.BlockSpec(wp.shape, lambda i: (0, 0, 0)), pl.BlockSpec((1, pw), lambda i: (0, 0))],
        out_specs=[pl.BlockSpec((tm, pw), row), pl.BlockSpec(wp.shape, lambda i: (0, 0, 0)),
                   pl.BlockSpec((1, pw), lambda i: (0, 0))],
        out_shape=[jax.ShapeDtypeStruct((t, pw), BF16), jax.ShapeDtypeStruct(wp.shape, F32),
                   jax.ShapeDtypeStruct((1, pw), F32)],
        scratch_shapes=[pltpu.VMEM((tm + POOL_HALO, pw), F32)],
        compiler_params=_seq(1),
    )(dy, pb, wp, scale)


def _gla_masks(kw, vw):
    dk, dv = kw // N_HEADS, vw // N_HEADS
    lane_k = lax.broadcasted_iota(jnp.int32, (1, kw), 1)
    lane_v = lax.broadcasted_iota(jnp.int32, (1, vw), 1)
    hk = [((lane_k >= h * dk) & (lane_k < (h + 1) * dk)).astype(F32) for h in range(N_HEADS)]
    hv = [((lane_v >= h * dv) & (lane_v < (h + 1) * dv)).astype(F32) for h in range(N_HEADS)]
    r = lax.broadcasted_iota(jnp.int32, (CHUNK, CHUNK), 0)
    c = lax.broadcasted_iota(jnp.int32, (CHUNK, CHUNK), 1)
    tril = r >= c
    rs = lax.broadcasted_iota(jnp.int32, (N_HEADS * CHUNK, CHUNK), 0) & (CHUNK - 1)
    stril = rs >= lax.broadcasted_iota(jnp.int32, (N_HEADS * CHUNK, CHUNK), 1)
    return hk, hv, tril, stril


def _block_diag(x, hk, dv):
    return jnp.concatenate([x[h * dv:(h + 1) * dv, :] * hk[h] for h in range(N_HEADS)], axis=0)


def _gla_fwd(q, k, v, loga, r, gnorm, name):
    t, kw = q.shape
    vw = v.shape[1]
    dk, dv = kw // N_HEADS, vw // N_HEADS
    tm = _row_tile(t)
    nc = tm // CHUNK
    qscale = dk ** -0.5

    def body(q_ref, k_ref, v_ref, la_ref, r_ref, gn_ref, o_ref, y_ref, sall_ref, st):
        @pl.when(pl.program_id(0) == 0)
        def _():
            st[...] = jnp.zeros_like(st)

        hk, hv, tril, stril = _gla_masks(kw, vw)
        trif = tril.astype(F32)

        def chunk(c, carry):
            rows = pl.ds(pl.multiple_of(c * CHUNK, CHUNK), CHUNK)
            la = la_ref[rows, :]
            b = _mm_f32(trif, la)
            bl = jnp.sum(la, axis=0, keepdims=True)
            qb = q_ref[rows, :] * (qscale * jnp.exp(b))
            kk = k_ref[rows, :]
            kb = kk * jnp.exp(-b)
            kl = kk * jnp.exp(bl - b)
            vv = v_ref[rows, :]
            s_t = st[...]
            compact = s_t[0:dv, :]
            for h in range(1, N_HEADS):
                compact = compact + s_t[h * dv:(h + 1) * dv, :]
            sall_ref[c] = compact
            qx = jnp.concatenate([qb * hk[h] for h in range(N_HEADS)], axis=0)
            a = jnp.where(stril, _mm_nt(qx, kb), 0.0).astype(BF16)
            o_inter = _mm_nt(qb, s_t)
            for h in range(N_HEADS):
                vs = slice(h * dv, (h + 1) * dv)
                o_ref[rows, vs] = o_inter[:, vs] + _mm(a[h * CHUNK:(h + 1) * CHUNK, :], vv[:, vs])
            st[...] = s_t * jnp.exp(bl) + _block_diag(_mm_tn(vv, kl), hk, dv)
            return carry

        lax.fori_loop(0, nc, chunk, 0)
        for h in range(N_HEADS):
            vs = slice(h * dv, (h + 1) * dv)
            oh = o_ref[:, vs]
            on = oh * lax.rsqrt(jnp.mean(oh * oh, axis=-1, keepdims=True) + RMS_EPS)
            _, sl = _silu_parts(r_ref[:, vs])
            y_ref[:, vs] = (on * gn_ref[:, vs] * sl).astype(BF16)

    row = lambda i: (i, 0)
    return _tc_call(
        body, name=name, grid=(t // tm,),
        in_specs=[pl.BlockSpec((tm, kw), row), pl.BlockSpec((tm, kw), row), pl.BlockSpec((tm, vw), row),
                  pl.BlockSpec((tm, kw), row), pl.BlockSpec((tm, vw), row), pl.BlockSpec((1, vw), lambda i: (0, 0))],
        out_specs=[pl.BlockSpec((tm, vw), row), pl.BlockSpec((tm, vw), row),
                   pl.BlockSpec((nc, dv, kw), lambda i: (i, 0, 0))],
        out_shape=[jax.ShapeDtypeStruct((t, vw), F32), jax.ShapeDtypeStruct((t, vw), BF16),
                   jax.ShapeDtypeStruct((t // CHUNK, dv, kw), F32)],
        scratch_shapes=[pltpu.VMEM((vw, kw), F32)],
        compiler_params=_seq(1),
    )(q, k, v, loga, r, gnorm)


def _gla_bwd(dy, o, r, gnorm, q, k, v, loga, zg, sall, wgu, name):
    t, kw = q.shape
    vw = v.shape[1]
    dk, dv = kw // N_HEADS, vw // N_HEADS
    tm = _row_tile(t)
    nt = t // tm
    nc = tm // CHUNK
    qscale = dk ** -0.5

    def body(dy_ref, o_ref, r_ref, gn_ref, q_ref, k_ref, v_ref, la_ref, zg_ref, sall_ref, wgu_ref,
             dq_ref, dk_ref, dv_ref, dr_ref, dzg_ref, dwgu_ref, dbg_ref, dgn_ref, dst, do_s):
        @pl.when(pl.program_id(0) == 0)
        def _():
            dst[...] = jnp.zeros_like(dst)
            dwgu_ref[...] = jnp.zeros_like(dwgu_ref)
            dbg_ref[...] = jnp.zeros_like(dbg_ref)
            dgn_ref[...] = jnp.zeros_like(dgn_ref)

        for h in range(N_HEADS):
            vs = slice(h * dv, (h + 1) * dv)
            oh = o_ref[:, vs]
            rinv = lax.rsqrt(jnp.mean(oh * oh, axis=-1, keepdims=True) + RMS_EPS)
            on = oh * rinv
            rr = r_ref[:, vs]
            sg, sl = _silu_parts(rr)
            dyv = dy_ref[:, vs]
            gn = gn_ref[:, vs]
            dgn_ref[:, vs] += jnp.sum(dyv * on * sl, axis=0, keepdims=True)
            dr_ref[:, vs] = (dyv * on * gn * (sg * (1.0 + rr * (1.0 - sg)))).astype(BF16)
            don = dyv * gn * sl
            do_s[:, vs] = rinv * (don - on * jnp.mean(don * on, axis=-1, keepdims=True))

        hk, hv, tril, stril = _gla_masks(kw, vw)
        trif = tril.astype(F32)
        triuf = (lax.broadcasted_iota(jnp.int32, (CHUNK, CHUNK), 0)
                 <= lax.broadcasted_iota(jnp.int32, (CHUNK, CHUNK), 1)).astype(F32)
        last_row = lax.broadcasted_iota(jnp.int32, (CHUNK, 1), 0) == CHUNK - 1

        def chunk(idx, carry):
            c = nc - 1 - idx
            rows = pl.ds(pl.multiple_of(c * CHUNK, CHUNK), CHUNK)
            la = la_ref[rows, :]
            b = _mm_f32(trif, la)
            bl = jnp.sum(la, axis=0, keepdims=True)
            eb = jnp.exp(b)
            enb = jnp.exp(-b)
            ebl = jnp.exp(bl - b)
            el = jnp.exp(bl)
            qb = q_ref[rows, :] * (qscale * eb)
            kk = k_ref[rows, :]
            kb = kk * enb
            kl = kk * ebl
            vv = v_ref[rows, :]
            do = do_s[rows, :]
            compact = sall_ref[c]
            s_t = jnp.concatenate([compact * hk[h] for h in range(N_HEADS)], axis=0)
            ds_t = dst[...]
            qx = jnp.concatenate([qb * hk[h] for h in range(N_HEADS)], axis=0)
            dox = jnp.concatenate([do * hv[h] for h in range(N_HEADS)], axis=0)
            a = jnp.where(stril, _mm_nt(qx, kb), 0.0).astype(BF16)
            da = jnp.where(stril, _mm_nt(dox, vv), 0.0).astype(BF16)
            dv_ref[rows, :] = (_mm_tn(a, dox) + _mm_nt(kl, ds_t)).astype(BF16)
            dak = _mm(da, kb)
            dqb = _mm(do, s_t)
            for h in range(N_HEADS):
                dqb = dqb + dak[h * CHUNK:(h + 1) * CHUNK, :] * hk[h]
            dkb = _mm_tn(da, qx)
            dkl = _mm(vv, ds_t)
            dbl = jnp.sum(dkl * kl, axis=0, keepdims=True) + el * jnp.sum(ds_t * s_t, axis=0, keepdims=True)
            dst[...] = ds_t * el + _block_diag(_mm_tn(do, qb), hk, dv)
            dq_ref[rows, :] = (dqb * (qscale * eb)).astype(BF16)
            dk_ref[rows, :] = (dkb * enb + dkl * ebl).astype(BF16)
            db = dqb * qb - dkb * kb - dkl * kl + jnp.where(last_row, dbl, 0.0)
            dla = _mm_f32(triuf, db)
            dlogit = dla * (1.0 / GLA_GATE_TEMP) * (1.0 - jnp.exp(GLA_GATE_TEMP * la))
            dzg_ref[rows, :] = _mm_nt(dlogit, wgu_ref[...]).astype(BF16)
            dwgu_ref[...] += _mm_tn(zg_ref[rows, :], dlogit)
            dbg_ref[...] += jnp.sum(dlogit, axis=0, keepdims=True)
            return carry

        lax.fori_loop(0, nc, chunk, 0)

    row = lambda i: (nt - 1 - i, 0)
    const = lambda i: (0, 0)
    return _tc_call(
        body, name=name, grid=(nt,),
        in_specs=[pl.BlockSpec((tm, vw), row), pl.BlockSpec((tm, vw), row), pl.BlockSpec((tm, vw), row),
                  pl.BlockSpec((1, vw), const), pl.BlockSpec((tm, kw), row), pl.BlockSpec((tm, kw), row),
                  pl.BlockSpec((tm, vw), row), pl.BlockSpec((tm, kw), row), pl.BlockSpec((tm, LANE), row),
                  pl.BlockSpec((nc, dv, kw), lambda i: (nt - 1 - i, 0, 0)), pl.BlockSpec((LANE, kw), const)],
        out_specs=[pl.BlockSpec((tm, kw), row), pl.BlockSpec((tm, kw), row), pl.BlockSpec((tm, vw), row),
                   pl.BlockSpec((tm, vw), row), pl.BlockSpec((tm, LANE), row), pl.BlockSpec((LANE, kw), const),
                   pl.BlockSpec((1, kw), const), pl.BlockSpec((1, vw), const)],
        out_shape=[jax.ShapeDtypeStruct((t, kw), BF16), jax.ShapeDtypeStruct((t, kw), BF16),
                   jax.ShapeDtypeStruct((t, vw), BF16), jax.ShapeDtypeStruct((t, vw), BF16),
                   jax.ShapeDtypeStruct((t, LANE), BF16), jax.ShapeDtypeStruct((LANE, kw), F32),
                   jax.ShapeDtypeStruct((1, kw), F32), jax.ShapeDtypeStruct((1, vw), F32)],
        scratch_shapes=[pltpu.VMEM((vw, kw), F32), pltpu.VMEM((tm, vw), F32)],
        compiler_params=_seq(1),
    )(dy, o, r, gnorm, q, k, v, loga, zg, sall, wgu)


def _outproj_fwd(yp, yg, w_out, xhat, gam, bet, alpha, name):
    t, d = xhat.shape
    pw = yp.shape[1]
    tm = _row_tile(t)

    def body(yp_ref, yg_ref, w_ref, x_ref, g_ref, b_ref, xhat_ref, rstd_ref):
        h = x_ref[...] * g_ref[...] + b_ref[...]
        y = (jnp.dot(yp_ref[...], w_ref[0:pw, :], preferred_element_type=F32)
             + jnp.dot(yg_ref[...], w_ref[pw:, :], preferred_element_type=F32))
        xh, rs = _ln_stats(alpha * h + y)
        xhat_ref[...] = xh
        rstd_ref[...] = rs

    row = lambda i: (i, 0)
    vec = pl.BlockSpec((1, d), lambda i: (0, 0))
    return _tc_call(
        body, name=name, grid=(t // tm,),
        in_specs=[pl.BlockSpec((tm, pw), row), pl.BlockSpec((tm, yg.shape[1]), row),
                  pl.BlockSpec(w_out.shape, lambda i: (0, 0)), pl.BlockSpec((tm, d), row), vec, vec],
        out_specs=[pl.BlockSpec((tm, d), row), pl.BlockSpec((tm, 1), row)],
        out_shape=[jax.ShapeDtypeStruct((t, d), F32), jax.ShapeDtypeStruct((t, 1), F32)],
        compiler_params=_seq(1),
    )(yp, yg, w_out, xhat, gam, bet)


def _outproj_bwd(dh, xhat, rstd, ln_g, w_out, pw, alpha, name):
    t, d = dh.shape
    tm = _row_tile(t)
    gw = w_out.shape[0] - pw

    def body(dh_ref, xh_ref, rs_ref, g_ref, w_ref, dyb_ref, dyp_ref, dyg_ref, dres_ref, dgam_ref, dbet_ref):
        @pl.when(pl.program_id(0) == 0)
        def _():
            dgam_ref[...] = jnp.zeros_like(dgam_ref)
            dbet_ref[...] = jnp.zeros_like(dbet_ref)

        dy = dh_ref[...]
        xh = xh_ref[...]
        dr = _ln_bwd(dy, xh, rs_ref[...], g_ref[...])
        dgam_ref[...] += jnp.sum(dy * xh, axis=0, keepdims=True)
        dbet_ref[...] += jnp.sum(dy, axis=0, keepdims=True)
        drb = dr.astype(BF16)
        dyb_ref[...] = drb
        dres_ref[...] = alpha * dr
        dyp_ref[...] = _mm_nt(drb, w_ref[0:pw, :])
        dyg_ref[...] = _mm_nt(drb, w_ref[pw:, :])

    row = lambda i: (i, 0)
    vec = pl.BlockSpec((1, d), lambda i: (0, 0))
    return _tc_call(
        body, name=name, grid=(t // tm,),
        in_specs=[pl.BlockSpec((tm, d), row), pl.BlockSpec((tm, d), row), pl.BlockSpec((tm, 1), row), vec,
                  pl.BlockSpec(w_out.shape, lambda i: (0, 0))],
        out_specs=[pl.BlockSpec((tm, d), row), pl.BlockSpec((tm, pw), row), pl.BlockSpec((tm, gw), row),
                   pl.BlockSpec((tm, d), row), vec, vec],
        out_shape=[jax.ShapeDtypeStruct((t, d), BF16), jax.ShapeDtypeStruct((t, pw), F32),
                   jax.ShapeDtypeStruct((t, gw), F32), jax.ShapeDtypeStruct((t, d), F32),
                   jax.ShapeDtypeStruct((1, d), F32), jax.ShapeDtypeStruct((1, d), F32)],
        compiler_params=_seq(1),
    )(dh, xhat, rstd, ln_g, w_out)


def _loss_head(xhat, gam, bet, target, n_rows, name):
    t, d = xhat.shape
    tm = _row_tile(t)

    def body(x_ref, g_ref, b_ref, t_ref, dy_ref, loss_ref):
        i = pl.program_id(0)

        @pl.when(i == 0)
        def _():
            loss_ref[...] = jnp.zeros_like(loss_ref)

        rowi = i * tm + lax.broadcasted_iota(jnp.int32, (tm, 1), 0)
        live = (rowi >= N_META) & (rowi < N_META + n_rows)
        diff = jnp.where(live, x_ref[...] * g_ref[...] + b_ref[...] - t_ref[...], 0.0)
        dy_ref[...] = diff * (1.0 / d)
        loss_ref[...] += jnp.sum(diff * diff) * (0.5 / d)

    row = lambda i: (i, 0)
    vec = pl.BlockSpec((1, d), lambda i: (0, 0))
    return _tc_call(
        body, name=name, grid=(t // tm,),
        in_specs=[pl.BlockSpec((tm, d), row), vec, vec, pl.BlockSpec((tm, d), row)],
        out_specs=[pl.BlockSpec((tm, d), row), pl.BlockSpec((8, LANE), lambda i: (0, 0))],
        out_shape=[jax.ShapeDtypeStruct((t, d), F32), jax.ShapeDtypeStruct((8, LANE), F32)],
        compiler_params=_seq(1),
    )(xhat, gam, bet, target)


def _rows_block(r, c):
    best = r
    for cand in range(8, r, 8):
        if r % cand == 0 and cand * c * 4 <= (1 << 20):
            best = cand
    return best if best * c * 4 <= (4 << 20) else r


def _sum_slots(recvs, name, layers_side_by_side=False):
    nl = len(recvs)
    ns, r, c = recvs[0].shape
    tr = _rows_block(r, c)

    def body(*refs):
        o_ref = refs[nl]
        for l in range(nl):
            acc = refs[l][0].astype(F32)
            for s in range(1, ns):
                acc = acc + refs[l][s].astype(F32)
            if layers_side_by_side:
                o_ref[0, :, l * c:(l + 1) * c] = acc
            else:
                o_ref[l] = acc

    out = (1, r, nl * c) if layers_side_by_side else (nl, r, c)
    return _tc_call(
        body, name=name, grid=(r // tr,),
        in_specs=[pl.BlockSpec((ns, tr, c), lambda i: (0, i, 0))] * nl,
        out_specs=pl.BlockSpec((out[0], tr, out[2]), lambda i: (0, i, 0)),
        out_shape=jax.ShapeDtypeStruct(out, F32),
        compiler_params=_seq(1),
    )(*recvs)


def _adamw(w, terms, m, v, name):
    nl, r, c = w.shape
    tc = c
    while tc % (2 * LANE) == 0 and tc > 4 * LANE:
        tc //= 2
    tr = _rows_block(r, tc)
    nterm = len(terms)

    def body(*refs):
        w_ref = refs[0]
        t_refs = refs[1:1 + nterm]
        m_ref, v_ref, g_ref, d_ref, nm_ref, nv_ref = refs[1 + nterm:]
        g = t_refs[0][...]
        for tr_ in t_refs[1:]:
            g = g + tr_[...]
        nm = ADAM_B1 * m_ref[...] + (1.0 - ADAM_B1) * g
        nv = ADAM_B2 * v_ref[...] + (1.0 - ADAM_B2) * jnp.square(g)
        m_hat = nm / (1.0 - ADAM_B1 ** ADAM_STEP)
        v_hat = nv / (1.0 - ADAM_B2 ** ADAM_STEP)
        g_ref[...] = g
        d_ref[...] = -ADAM_LR * (m_hat / (jnp.sqrt(v_hat) + ADAM_EPS) + ADAM_WD * w_ref[...])
        nm_ref[...] = nm
        nv_ref[...] = nv

    spec = pl.BlockSpec((None, tr, tc), lambda l, i, j: (l, i, j))
    shp = jax.ShapeDtypeStruct((nl, r, c), F32)
    return _tc_call(
        body, name=name, grid=(nl, r // tr, c // tc),
        in_specs=[spec] * (3 + nterm), out_specs=[spec] * 4, out_shape=[shp] * 4,
        compiler_params=_seq(3),
    )(w, *terms, m, v)


XY_RELATIONS = ((1, 0, 0), (0, 1, 0), (1, 1, 0))
ALL_RELATIONS = tuple((fx, fy, fc) for fx in (0, 1) for fy in (0, 1) for fc in (0, 1) if fx or fy or fc)
HBM_SPEC = pl.BlockSpec(memory_space=pltpu.HBM)
SEM_SPEC = pl.BlockSpec(memory_space=pltpu.SEMAPHORE)
DATAFLOW = pltpu.SideEffectType.DATAFLOW_SIDE_EFFECTING


def _split_call(body, **kw):
    return pl.pallas_call(body, **kw)


def _flip(v, f):
    return 1 - v if f else v


def _any_spec(n):
    return [pl.BlockSpec(memory_space=pl.ANY)] * n


def _relations(kind):
    return ALL_RELATIONS if kind == "bcast" else XY_RELATIONS


def _copies(kind, arr, land, sems):
    x, y, c = lax.axis_index("x"), lax.axis_index("y"), lax.axis_index("c")
    out = []
    for (fx, fy, fc), (send_sem, recv_sem) in zip(_relations(kind), sems):
        px, py, pc = _flip(x, fx), _flip(y, fy), _flip(c, fc)
        if kind == "bcast":
            mine, theirs = 4 * x + 2 * y + c, 4 * px + 2 * py + pc
        else:
            mine, theirs = 2 * x + y, 2 * px + py
        src = arr.at[theirs] if kind == "scatter" else arr
        both = dict(src_ref=src, send_sem=send_sem, recv_sem=recv_sem, device_id=(px, py, pc), device_id_type=MESH)
        out.append((pltpu.make_async_remote_copy(dst_ref=land.at[mine], **both),
                    pltpu.make_async_remote_copy(dst_ref=land.at[theirs], **both)))
    return out


def _sem_pairs(kinds, sems):
    out, at = [], 0
    for kind in kinds:
        nrel = len(_relations(kind))
        out.append([(sems[at + 2 * r], sems[at + 2 * r + 1]) for r in range(nrel)])
        at += 2 * nrel
    return out


def _exchange_start(name, kinds, arrs, lands):
    n = len(arrs)
    nsem = sum(2 * len(_relations(kd)) for kd in kinds)

    def body(*refs):
        a_refs, l_refs = refs[:n], refs[n:2 * n]
        pairs = _sem_pairs(kinds, refs[2 * n:2 * n + nsem])
        token = refs[-1]
        for k in range(n):
            for send, _ in _copies(kinds[k], a_refs[k], l_refs[k], pairs[k]):
                send.start()
        token[...] = jnp.zeros_like(token)

    thru = [pltpu.HBM(a.shape, a.dtype) for a in list(arrs) + list(lands)]
    outs = _split_call(
        body, name=name,
        out_shape=(*[pltpu.SemaphoreType.DMA(())] * nsem, *thru, jax.ShapeDtypeStruct((8, LANE), F32)),
        in_specs=[HBM_SPEC] * (2 * n),
        out_specs=(*[SEM_SPEC] * nsem, *[HBM_SPEC] * (2 * n), pl.BlockSpec(memory_space=pltpu.VMEM)),
        input_output_aliases={i: nsem + i for i in range(2 * n)},
        compiler_params=pltpu.CompilerParams(has_side_effects=DATAFLOW),
    )(*[pltpu.with_memory_space_constraint(a, pltpu.HBM) for a in list(arrs) + list(lands)])
    return dict(kinds=kinds, sems=outs[:nsem], arrs=outs[nsem:nsem + n], lands=outs[nsem + n:nsem + 2 * n],
                token=outs[-1])


def _exchange_wait(name, st, after):
    kinds = st["kinds"]
    n = len(kinds)
    nsem = len(st["sems"])

    def body(*refs):
        a_refs, l_refs = refs[:n], refs[n:2 * n]
        pairs = _sem_pairs(kinds, refs[2 * n:2 * n + nsem])
        for k in range(n):
            for _, arrival in _copies(kinds[k], a_refs[k], l_refs[k], pairs[k]):
                arrival.wait_send()
                arrival.wait_recv()
        refs[-1][...] = jnp.zeros_like(refs[-1])

    ins = list(st["arrs"]) + list(st["lands"])
    outs = _split_call(
        body, name=name,
        out_shape=[pltpu.HBM(a.shape, a.dtype) for a in ins] + [jax.ShapeDtypeStruct((8, LANE), F32)],
        in_specs=[HBM_SPEC] * (2 * n) + [SEM_SPEC] * nsem + [pl.BlockSpec(memory_space=pl.ANY)],
        out_specs=[HBM_SPEC] * (2 * n) + [pl.BlockSpec(memory_space=pltpu.VMEM)],
        input_output_aliases={i: i for i in range(2 * n)},
        compiler_params=pltpu.CompilerParams(has_side_effects=DATAFLOW),
    )(*ins, *st["sems"], after)
    return outs[n:2 * n], outs[-1]


def _landing(own, slot, nslot):
    return lax.dynamic_update_slice(lax.empty((nslot,) + own.shape, own.dtype), own[None], (slot,) + (0,) * own.ndim)


def _swap_sibling(parts, name):
    n = len(parts)

    def body(*refs):
        ins, outs = refs[:n], refs[n:2 * n]
        send_sems, recv_sems = refs[2 * n:]
        sib = (lax.axis_index("x"), lax.axis_index("y"), 1 - lax.axis_index("c"))
        cps = [pltpu.make_async_remote_copy(src_ref=ins[k], dst_ref=outs[k], send_sem=send_sems.at[k],
                                            recv_sem=recv_sems.at[k], device_id=sib, device_id_type=MESH)
               for k in range(n)]
        for cp in cps:
            cp.start()
        for cp in cps:
            cp.wait_recv()
        for cp in cps:
            cp.wait_send()

    return _comm_call(
        body, name=name,
        in_specs=_any_spec(n), out_specs=_any_spec(n),
        out_shape=[jax.ShapeDtypeStruct(p.shape, p.dtype) for p in parts],
        scratch_shapes=[pltpu.SemaphoreType.DMA((n,)), pltpu.SemaphoreType.DMA((n,))],
    )(*parts)


def _col_shards(a, n=N_SHARD):
    r, c = a.shape
    return a.reshape(r, n, c // n).transpose(1, 0, 2)


def _from_col_shards(a):
    n, r, cs = a.shape
    return a.transpose(1, 0, 2).reshape(r, n * cs)


def kernel(x, meta_tokens, ffn1_w_gate, ffn1_w_up, ffn1_w_down, ln1_g, ln1_b, w_in, w_gate_up, b_gate, w_pool, pool_scale, gla_norm_g, w_out, ln2_g, ln2_b, ffn2_w_gate, ffn2_w_up, ffn2_w_down, ln3_g, ln3_b, loss_target, m_meta_tokens, m_ffn1_w_gate, m_ffn1_w_up, m_ffn1_w_down, m_ln1_g, m_ln1_b, m_w_in, m_w_gate_up, m_b_gate, m_w_pool, m_pool_scale, m_gla_norm_g, m_w_out, m_ln2_g, m_ln2_b, m_ffn2_w_gate, m_ffn2_w_up, m_ffn2_w_down, m_ln3_g, m_ln3_b, v_meta_tokens, v_ffn1_w_gate, v_ffn1_w_up, v_ffn1_w_down, v_ln1_g, v_ln1_b, v_w_in, v_w_gate_up, v_b_gate, v_w_pool, v_pool_scale, v_gla_norm_g, v_w_out, v_ln2_g, v_ln2_b, v_ffn2_w_gate, v_ffn2_w_up, v_ffn2_w_down, v_ln3_g, v_ln3_b):
    w = dict(meta_tokens=meta_tokens, ffn1_w_gate=ffn1_w_gate, ffn1_w_up=ffn1_w_up, ffn1_w_down=ffn1_w_down,
             ln1_g=ln1_g, ln1_b=ln1_b, w_in=w_in, w_gate_up=w_gate_up, b_gate=b_gate, w_pool=w_pool,
             pool_scale=pool_scale, gla_norm_g=gla_norm_g, w_out=w_out, ln2_g=ln2_g, ln2_b=ln2_b,
             ffn2_w_gate=ffn2_w_gate, ffn2_w_up=ffn2_w_up, ffn2_w_down=ffn2_w_down, ln3_g=ln3_g, ln3_b=ln3_b)
    mom1 = dict(meta_tokens=m_meta_tokens, ffn1_w_gate=m_ffn1_w_gate, ffn1_w_up=m_ffn1_w_up,
                ffn1_w_down=m_ffn1_w_down, ln1_g=m_ln1_g, ln1_b=m_ln1_b, w_in=m_w_in, w_gate_up=m_w_gate_up,
                b_gate=m_b_gate, w_pool=m_w_pool, pool_scale=m_pool_scale, gla_norm_g=m_gla_norm_g, w_out=m_w_out,
                ln2_g=m_ln2_g, ln2_b=m_ln2_b, ffn2_w_gate=m_ffn2_w_gate, ffn2_w_up=m_ffn2_w_up,
                ffn2_w_down=m_ffn2_w_down, ln3_g=m_ln3_g, ln3_b=m_ln3_b)
    mom2 = dict(meta_tokens=v_meta_tokens, ffn1_w_gate=v_ffn1_w_gate, ffn1_w_up=v_ffn1_w_up,
                ffn1_w_down=v_ffn1_w_down, ln1_g=v_ln1_g, ln1_b=v_ln1_b, w_in=v_w_in, w_gate_up=v_w_gate_up,
                b_gate=v_b_gate, w_pool=v_w_pool, pool_scale=v_pool_scale, gla_norm_g=v_gla_norm_g, w_out=v_w_out,
                ln2_g=v_ln2_g, ln2_b=v_ln2_b, ffn2_w_gate=v_ffn2_w_gate, ffn2_w_up=v_ffn2_w_up,
                ffn2_w_down=v_ffn2_w_down, ln3_g=v_ln3_g, ln3_b=v_ln3_b)

    xs = x[0]
    s_len, d = xs.shape
    nl = ln1_g.shape[0]
    alpha = (2.0 * nl) ** 0.25
    t_real = N_META + s_len
    t_pad = -(-t_real // LANE) * LANE
    pw = pool_scale.shape[1]
    kw = b_gate.shape[1]
    vw = gla_norm_g.shape[1]
    rank = w_gate_up.shape[1]
    widths = (pw, kw, kw, vw, vw)
    n_main = sum(widths)
    dff_s = ffn1_w_gate.shape[2]

    me_xy = 2 * lax.axis_index("x") + lax.axis_index("y")
    me_all = 2 * me_xy + lax.axis_index("c")
    ffn1_names = ("ffn1_w_gate", "ffn1_w_up", "ffn1_w_down")
    mix_names = ("w_out", "w_gate_up", "w_in")
    ffn2_names = ("ffn2_w_gate", "ffn2_w_up", "ffn2_w_down")

    gate_up = ("ffn1_w_gate", "ffn1_w_up", "ffn2_w_gate", "ffn2_w_up")

    def stored(n, a):
        if n in gate_up:
            return jnp.swapaxes(a, 1, 2)
        return jnp.transpose(a, (2, 0, 1)) if n == "w_in" else a

    def as_given(n, a):
        if n in gate_up:
            return jnp.swapaxes(a, 1, 2)
        if n == "w_in":
            return jnp.transpose(a.reshape(-1, nl, d), (1, 2, 0))
        return a.reshape(w[n].shape)

    stages = [[("meta_tokens", None)] + [(n, 0) for n in ffn1_names], [(n, 0) for n in mix_names + ffn2_names]]
    stages += [[(n, l) for n in BIG] for l in range(1, nl)]
    gathers, wa = {}, {}

    def start_gather(si, dep=None):
        own = []
        for n, l in stages[si]:
            a = meta_tokens if l is None else (stored(n, w[n])[:, l] if n == "w_in" else stored(n, w[n])[l])
            own.append(a if l is None else (a if dep is None else a + dep).astype(BF16))
        gathers[si] = _exchange_start(f"gather_start_{si}", ["gather"] * len(own), own,
                                      [_landing(a, me_xy, N_SHARD) for a in own])
        return gathers[si]["token"]

    def arrive(si, after):
        lands, token = _exchange_wait(f"gather_wait_{si}", gathers[si], after)
        for item, a in zip(stages[si], lands):
            wa[item] = a
        return token

    def mixer_weights(l):
        wi = wa["w_in", l].reshape(-1, d)
        return dict(w_main=wi[:n_main], w_lr=jnp.pad(wi[n_main:], ((0, LANE - rank), (0, 0))),
                    wgu=jnp.pad(_from_col_shards(wa["w_gate_up", l]), ((0, LANE - rank), (0, 0))),
                    wout=wa["w_out", l].reshape(-1, d))

    wp16 = w_pool.astype(BF16)
    ones = jnp.ones((1, d), F32)
    zeros = jnp.zeros((1, d), F32)
    target = jnp.concatenate([jnp.zeros((N_META, d), F32), loss_target[0], jnp.zeros((t_pad - t_real, d), F32)], axis=0)

    landed = arrive(0, start_gather(0) + start_gather(1))
    meta_full = _from_col_shards(wa["meta_tokens", None])
    h0 = jnp.concatenate([meta_full, xs, jnp.zeros((t_pad - t_real, d), F32)], axis=0)

    saved, mw = [], []
    cur, cur_g, cur_b = h0, ones, zeros
    for l in range(nl):
        s = {}
        xh1, rs1, hb0, g1, u1 = _ffn_fwd(cur, cur_g, cur_b, wa["ffn1_w_gate", l], wa["ffn1_w_up", l],
                                         wa["ffn1_w_down", l], alpha, f"ffn1_fwd_{l}")
        if l == 0:
            landed = arrive(1, xh1)
        gam1 = ln1_g[l:l + 1]
        if l + 2 < len(stages):
            gam1 = gam1 + start_gather(l + 2, landed[0:1, 0:1])[0:1, 0:1]
        mw.append(mixer_weights(l))
        up, q, k, v, r, zg, la, hb1 = _inproj_fwd(xh1, gam1, ln1_b[l:l + 1], mw[l]["w_main"], mw[l]["w_lr"],
                                                  mw[l]["wgu"], b_gate[l:l + 1], widths, f"inproj_fwd_{l}")
        yp, pb = _pool_fwd(up, wp16[l], pool_scale[l:l + 1], f"pool_fwd_{l}")
        o, yg, sall = _gla_fwd(q, k, v, la, r, gla_norm_g[l:l + 1], f"gla_fwd_{l}")
        xh2, rs2 = _outproj_fwd(yp, yg, mw[l]["wout"], xh1, ln1_g[l:l + 1], ln1_b[l:l + 1], alpha, f"outproj_fwd_{l}")
        if l + 1 < nl:
            landed = arrive(l + 2, xh2)
        xh3, rs3, hb2, g2, u2 = _ffn_fwd(xh2, ln2_g[l:l + 1], ln2_b[l:l + 1], wa["ffn2_w_gate", l], wa["ffn2_w_up", l],
                                         wa["ffn2_w_down", l], alpha, f"ffn2_fwd_{l}")
        s.update(xh1=xh1, rs1=rs1, hb0=hb0, g1=g1, u1=u1, q=q, k=k, v=v, r=r, zg=zg, la=la, hb1=hb1, yp=yp, pb=pb,
                 o=o, yg=yg, sall=sall, xh2=xh2, rs2=rs2, xh3=xh3, rs3=rs3, hb2=hb2, g2=g2, u2=u2)
        saved.append(s)
        cur, cur_g, cur_b = xh3, ln3_g[l:l + 1], ln3_b[l:l + 1]

    dh, loss_acc = _loss_head(cur, cur_g, cur_b, target, s_len, "loss_head")
    loss = lax.psum(loss_acc[0, 0], ("x", "y", "c"))

    small_grads = {n: [None] * nl for n in SMALL}
    scatters = []

    def depart(name, items, grads, kinds=None):
        lands = [_landing(g if kd == "bcast" else lax.dynamic_index_in_dim(g, me_xy, 0, keepdims=False),
                          me_all if kd == "bcast" else me_xy, N_DEV if kd == "bcast" else N_SHARD)
                 for g, kd in zip(grads, kinds or ["scatter"] * len(grads))]
        st = _exchange_start(name, kinds or ["scatter"] * len(grads), grads, lands)
        scatters.append((name, items, st))
        return st["token"]

    def pack(parts):
        flat = jnp.concatenate([parts[n].reshape(-1) for n in SMALL])
        return flat.reshape(-1, LANE)

    def ffn_wgrads(l, names, hb, dgb, dub, act, dfb):
        tag = names[0][:4]
        return [_wgrad(dgb, hb, dff_s, d, f"{tag}_dwg_{l}").reshape(N_SHARD, dff_s, d),
                _wgrad(dub, hb, dff_s, d, f"{tag}_dwu_{l}").reshape(N_SHARD, dff_s, d),
                _wgrad(act, dfb, dff_s, d, f"{tag}_dwd_{l}").reshape(N_SHARD, dff_s, d)]

    late = []
    for l in reversed(range(nl)):
        s = saved[l]
        dh, dfb, dgb, dub, act, dgam, dbet = _ffn_bwd(dh, s["xh3"], s["rs3"], ln3_g[l:l + 1], s["g2"], s["u2"],
                                                      wa["ffn2_w_gate", l], wa["ffn2_w_up", l], wa["ffn2_w_down", l],
                                                      alpha, f"ffn2_bwd_{l}")
        small_grads["ln3_g"][l], small_grads["ln3_b"][l] = dgam, dbet
        gone = depart(f"scatter_start_ffn2_{l}", [(n, l) for n in ffn2_names],
                      ffn_wgrads(l, ffn2_names, s["hb2"], dgb, dub, act, dfb))

        dyb, dyp, dyg, dres, dgam, dbet = _outproj_bwd(dh, s["xh2"], s["rs2"], ln2_g[l:l + 1] + gone[0:1, 0:1],
                                                       mw[l]["wout"], pw, alpha, f"outproj_bwd_{l}")
        small_grads["ln2_g"][l], small_grads["ln2_b"][l] = dgam, dbet
        dwo = jnp.concatenate([_wgrad(s["yp"], dyb, pw, d, f"dwout_pool_{l}"),
                               _wgrad(s["yg"], dyb, vw, d, f"dwout_gla_{l}")], axis=0)
        dq, dk, dv, dr, dzg, dwgu, dbg, dgn = _gla_bwd(dyg, s["o"], s["r"], gla_norm_g[l:l + 1], s["q"], s["k"],
                                                       s["v"], s["la"], s["zg"], s["sall"], mw[l]["wgu"],
                                                       f"gla_bwd_{l}")
        dup, dwp, dsc = _pool_bwd(dyp, s["pb"], wp16[l], pool_scale[l:l + 1], f"pool_bwd_{l}")
        small_grads["b_gate"][l], small_grads["gla_norm_g"][l] = dbg, dgn
        small_grads["w_pool"][l], small_grads["pool_scale"][l] = dwp, dsc
        dh, dz = _inproj_bwd(dres, [dup, dq, dk, dv, dr], dzg, mw[l]["w_main"], mw[l]["w_lr"], f"inproj_bwd_{l}")
        dwi = jnp.concatenate([_wgrad(dz, s["hb1"], 4 * LANE, d, f"dwin_main_{l}"),
                               _wgrad(dzg, s["hb1"], LANE, d, f"dwin_lr_{l}")[:rank]], axis=0)
        gone = depart(f"scatter_start_mix_{l}", [(n, l) for n in mix_names],
                      [dwo.reshape(N_SHARD, -1, d), _col_shards(dwgu[:rank].astype(BF16)),
                       dwi.reshape(N_SHARD, -1, d)])

        dh, dfb, dgb, dub, act, dgam, dbet = _ffn_bwd(dh, s["xh1"], s["rs1"], ln1_g[l:l + 1] + gone[0:1, 0:1],
                                                      s["g1"], s["u1"], wa["ffn1_w_gate", l], wa["ffn1_w_up", l],
                                                      wa["ffn1_w_down", l], alpha, f"ffn1_bwd_{l}")
        small_grads["ln1_g"][l], small_grads["ln1_b"][l] = dgam, dbet
        if l:
            gone = depart(f"scatter_start_ffn1_{l}", [(n, l) for n in ffn1_names],
                          ffn_wgrads(l, ffn1_names, s["hb0"], dgb, dub, act, dfb))
            ln3_g = ln3_g.at[l - 1:l].add(gone[0:1, 0:1])
            continue
        grad_x = dh[N_META:t_real][None]
        small_vec = pack({n: jnp.stack(small_grads[n]) for n in SMALL})
        gone = depart("scatter_start_rest", [("meta_tokens", 0), ("small", 0)],
                      [_col_shards(dh[:N_META].astype(BF16)), small_vec], ["scatter", "bcast"])
        for n, a, b in zip(ffn1_names, (dgb, dub, act), (s["hb0"], s["hb0"], dfb)):
            g = _wgrad(a, b, dff_s, d, f"{n}_grad_{l}", after=gone).reshape(N_SHARD, dff_s, d)
            gone = depart(f"scatter_start_{n}", [(n, l)], [g])
            late.append(scatters.pop())

    recv, results = {}, {}

    def collect(group, after):
        for name, items, st in group:
            for item, a in zip(items, _exchange_wait(name.replace("start", "wait"), st, after)[0]):
                recv[item] = a

    def reduce_and_update(names, tag):
        partial = [_sum_slots([recv[n, l] for l in range(1 if n == "meta_tokens" else nl)], f"sum_{n}", n == "w_in")
                   for n in names]
        for n, mine, theirs in zip(names, partial, _swap_sibling(partial, f"swap_sibling_{tag}")):
            fit = lambda a: stored(n, a).reshape(mine.shape)
            outs = _adamw(fit(w[n]), [mine, theirs], fit(mom1[n]), fit(mom2[n]), f"adamw_{n}")
            results[n] = [as_given(n, o) for o in outs]

    collect(scatters, gone)
    reduce_and_update([n for n in ("meta_tokens",) + BIG if n not in ffn1_names], "early")
    small_terms = [recv["small", 0][i][None] for i in range(N_DEV)]
    souts = _adamw(pack(w)[None], small_terms, pack(mom1)[None], pack(mom2)[None], "adamw_small")
    off = 0
    for n in SMALL:
        size = w[n].size
        results[n] = [o.reshape(-1)[off:off + size].reshape(w[n].shape) for o in souts]
        off += size
    collect(late, souts[0])
    reduce_and_update(ffn1_names, "late")

    out = [loss, grad_x]
    for part in range(4):
        out += [results[n][part] for n in WEIGHTS]
    return tuple(out)
```

```python
import functools

import jax
import jax.numpy as jnp
from jax import lax
from jax.experimental import pallas as pl
from jax.experimental.pallas import tpu as pltpu

F32 = jnp.float32
BF16 = jnp.bfloat16
MESH = pl.DeviceIdType.MESH

N_META = 16
POOL_WINDOWS = (2, 4, 8, 16)
POOL_HALO = 16
N_HEADS = 4
GLA_GATE_TEMP = 16.0
CHUNK = 128
CHUNK_UNROLL = 5
LN_EPS = 1e-5
RMS_EPS = 1e-6
ADAM_LR = 0.001
ADAM_B1 = 0.9
ADAM_B2 = 0.999
ADAM_EPS = 1e-08
ADAM_WD = 0.01
ADAM_STEP = 10
LANE = 128
BF16_ROWS = 16
ROW_TILE = 640
WGRAD_K_MAX = 2176
N_SHARD = 4
N_DEV = 8

BIG = ("ffn1_w_gate", "ffn1_w_up", "ffn1_w_down", "w_in", "w_gate_up", "w_out",
       "ffn2_w_gate", "ffn2_w_up", "ffn2_w_down")
SMALL = ("ln1_g", "ln1_b", "b_gate", "w_pool", "pool_scale", "gla_norm_g", "ln2_g", "ln2_b", "ln3_g", "ln3_b")
WEIGHTS = ("meta_tokens", "ffn1_w_gate", "ffn1_w_up", "ffn1_w_down", "ln1_g", "ln1_b", "w_in", "w_gate_up",
           "b_gate", "w_pool", "pool_scale", "gla_norm_g", "w_out", "ln2_g", "ln2_b", "ffn2_w_gate",
           "ffn2_w_up", "ffn2_w_down", "ln3_g", "ln3_b")


def _tc_call(body, **kw):
    return pl.pallas_call(body, **kw)


def _comm_call(body, **kw):
    return pl.pallas_call(body, **kw)


def _seq(n):
    return pltpu.CompilerParams(dimension_semantics=("arbitrary",) * n)


def _mm(a, b):
    return jnp.dot(a.astype(BF16), b.astype(BF16), preferred_element_type=F32)


def _mm_nt(a, b):
    return lax.dot_general(a.astype(BF16), b.astype(BF16), (((1,), (1,)), ((), ())), preferred_element_type=F32)


def _mm_tn(a, b):
    return lax.dot_general(a.astype(BF16), b.astype(BF16), (((0,), (0,)), ((), ())), preferred_element_type=F32)


def _mm_f32(a, b):
    return jnp.dot(a, b, precision=lax.Precision.HIGHEST, preferred_element_type=F32)


def _row_tile(t):
    tm = min(ROW_TILE, t)
    while t % tm:
        tm -= LANE
    return tm


def _silu_parts(g):
    sg = jax.nn.sigmoid(g)
    return sg, g * sg


def _ln_stats(r):
    mu = jnp.mean(r, axis=-1, keepdims=True)
    rc = r - mu
    var = jnp.mean(rc * rc, axis=-1, keepdims=True)
    rs = lax.rsqrt(var + LN_EPS)
    return rc * rs, rs


def _ln_bwd(dy, xh, rs, gam):
    dyg = dy * gam
    c1 = jnp.mean(dyg, axis=-1, keepdims=True)
    c2 = jnp.mean(dyg * xh, axis=-1, keepdims=True)
    return rs * (dyg - c1 - xh * c2)


def _ffn_fwd(xin, gam_in, bet_in, wg, wu, wd, alpha, name):
    t, d = xin.shape
    nj, tf, _ = wg.shape
    tm = _row_tile(t)

    def body(x_ref, gi_ref, bi_ref, wg_ref, wu_ref, wd_ref, xhat_ref, rstd_ref, hb_ref, go_ref, uo_ref, acc, hbs):
        j = pl.program_id(1)

        @pl.when(j == 0)
        def _():
            hb = (x_ref[...] * gi_ref[...] + bi_ref[...]).astype(BF16)
            hbs[...] = hb
            hb_ref[...] = hb
            acc[...] = jnp.zeros_like(acc)

        hb = hbs[...]
        g = _mm_nt(hb, wg_ref[...])
        u = _mm_nt(hb, wu_ref[...])
        _, sl = _silu_parts(g)
        go_ref[...] = g.astype(BF16)
        uo_ref[...] = u.astype(BF16)
        acc[...] += jnp.dot((sl * u).astype(BF16), wd_ref[...], preferred_element_type=F32)

        @pl.when(j == nj - 1)
        def _():
            h = x_ref[...] * gi_ref[...] + bi_ref[...]
            xhat, rs = _ln_stats(alpha * h + 0.5 * acc[...])
            xhat_ref[...] = xhat
            rstd_ref[...] = rs

    row = lambda i, j: (i, 0)
    vec = pl.BlockSpec((1, d), lambda i, j: (0, 0))
    return _tc_call(
        body, name=name, grid=(t // tm, nj),
        in_specs=[pl.BlockSpec((tm, d), row), vec, vec] + [pl.BlockSpec((None, tf, d), lambda i, j: (j, 0, 0))] * 3,
        out_specs=[pl.BlockSpec((tm, d), row), pl.BlockSpec((tm, 1), row), pl.BlockSpec((tm, d), row),
                   pl.BlockSpec((None, tm, tf), lambda i, j: (j, i, 0)),
                   pl.BlockSpec((None, tm, tf), lambda i, j: (j, i, 0))],
        out_shape=[jax.ShapeDtypeStruct((t, d), F32), jax.ShapeDtypeStruct((t, 1), F32),
                   jax.ShapeDtypeStruct((t, d), BF16), jax.ShapeDtypeStruct((nj, t, tf), BF16),
                   jax.ShapeDtypeStruct((nj, t, tf), BF16)],
        scratch_shapes=[pltpu.VMEM((tm, d), F32), pltpu.VMEM((tm, d), BF16)],
        compiler_params=_seq(2),
    )(xin, gam_in, bet_in, wg, wu, wd)


def _ffn_bwd(dh, xhat, rstd, ln_g, gb, ub, wg, wu, wd, alpha, name):
    t, d = dh.shape
    nj, tf, _ = wg.shape
    tm = _row_tile(t)

    def body(dh_ref, xh_ref, rs_ref, g_ref, gb_ref, ub_ref, wg_ref, wu_ref, wd_ref,
             dhin_ref, df_ref, dg_ref, du_ref, act_ref, dgam_ref, dbet_ref, dr_s, df_s, acc):
        i = pl.program_id(0)
        j = pl.program_id(1)

        @pl.when(j == 0)
        def _():
            dy = dh_ref[...]
            xh = xh_ref[...]
            dr = _ln_bwd(dy, xh, rs_ref[...], g_ref[...])
            dr_s[...] = dr
            dfb = (0.5 * dr).astype(BF16)
            df_s[...] = dfb
            df_ref[...] = dfb
            acc[...] = jnp.zeros_like(acc)

            @pl.when(i == 0)
            def _():
                dgam_ref[...] = jnp.zeros_like(dgam_ref)
                dbet_ref[...] = jnp.zeros_like(dbet_ref)

            dgam_ref[...] += jnp.sum(dy * xh, axis=0, keepdims=True)
            dbet_ref[...] += jnp.sum(dy, axis=0, keepdims=True)

        dact = _mm_nt(df_s[...], wd_ref[...])
        g = gb_ref[...].astype(F32)
        u = ub_ref[...].astype(F32)
        sg, sl = _silu_parts(g)
        dg = (dact * u * (sg * (1.0 + g * (1.0 - sg)))).astype(BF16)
        du = (dact * sl).astype(BF16)
        dg_ref[...] = dg
        du_ref[...] = du
        act_ref[...] = (sl * u).astype(BF16)
        acc[...] += _mm(dg, wg_ref[...]) + _mm(du, wu_ref[...])

        @pl.when(j == nj - 1)
        def _():
            dhin_ref[...] = alpha * dr_s[...] + acc[...]

    row = lambda i, j: (i, 0)
    col = pl.BlockSpec((None, tm, tf), lambda i, j: (j, i, 0))
    vec = pl.BlockSpec((1, d), lambda i, j: (0, 0))
    ff = jax.ShapeDtypeStruct((nj, t, tf), BF16)
    return _tc_call(
        body, name=name, grid=(t // tm, nj),
        in_specs=[pl.BlockSpec((tm, d), row), pl.BlockSpec((tm, d), row), pl.BlockSpec((tm, 1), row), vec,
                  col, col] + [pl.BlockSpec((None, tf, d), lambda i, j: (j, 0, 0))] * 3,
        out_specs=[pl.BlockSpec((tm, d), row), pl.BlockSpec((tm, d), row),
                   col, col, col, vec, vec],
        out_shape=[jax.ShapeDtypeStruct((t, d), F32), jax.ShapeDtypeStruct((t, d), BF16), ff, ff, ff,
                   jax.ShapeDtypeStruct((1, d), F32), jax.ShapeDtypeStruct((1, d), F32)],
        scratch_shapes=[pltpu.VMEM((tm, d), F32), pltpu.VMEM((tm, d), BF16), pltpu.VMEM((tm, d), F32)],
        compiler_params=_seq(2),
    )(dh, xhat, rstd, ln_g, gb, ub, wg, wu, wd)


def _wgrad(a, b, tmm, tn, name, after=None):
    t = a.shape[-2]
    m = a.shape[-1] * (a.shape[0] if a.ndim == 3 else 1)
    n = b.shape[-1] * (b.shape[0] if b.ndim == 3 else 1)
    tk = max(k for k in range(BF16_ROWS, WGRAD_K_MAX + 1, BF16_ROWS) if t % k == 0)
    nk = t // tk
    extra = [] if after is None else [after]

    def body(a_ref, b_ref, *rest):
        o_ref, acc = rest[len(extra):]
        k = pl.program_id(2)

        @pl.when(k == 0)
        def _():
            acc[...] = jnp.zeros_like(acc)

        acc[...] += _mm_tn(a_ref[...], b_ref[...])

        @pl.when(k == nk - 1)
        def _():
            o_ref[...] = acc[...].astype(o_ref.dtype)

    return _tc_call(
        body, name=name, grid=(m // tmm, n // tn, nk),
        in_specs=[pl.BlockSpec((None, tk, tmm), lambda i, j, k: (i, k, 0)) if a.ndim == 3
                  else pl.BlockSpec((tk, tmm), lambda i, j, k: (k, i)),
                  pl.BlockSpec((None, tk, tn), lambda i, j, k: (j, k, 0)) if b.ndim == 3
                  else pl.BlockSpec((tk, tn), lambda i, j, k: (k, j))] + [pl.BlockSpec(memory_space=pl.ANY)] * len(extra),
        out_specs=pl.BlockSpec((tmm, tn), lambda i, j, k: (i, j)),
        out_shape=jax.ShapeDtypeStruct((m, n), BF16),
        scratch_shapes=[pltpu.VMEM((tmm, tn), F32)],
        compiler_params=_seq(3),
    )(a, b, *extra)


def _inproj_fwd(xhat, gam, bet, w_main, w_lr, wgu, b_gate, widths, name):
    t, d = xhat.shape
    tm = _row_tile(t)
    kw = wgu.shape[1]
    offs = [0]
    for w in widths:
        offs.append(offs[-1] + w)

    def body(x_ref, g_ref, b_ref, wm_ref, wl_ref, wgu_ref, bg_ref, *outs):
        piece_refs, (zg_ref, la_ref, hb_ref) = outs[:len(widths)], outs[len(widths):]
        hb = (x_ref[...] * g_ref[...] + b_ref[...]).astype(BF16)
        hb_ref[...] = hb
        for p, ref in enumerate(piece_refs):
            ref[...] = _mm_nt(hb, wm_ref[offs[p]:offs[p + 1], :])
        zg = _mm_nt(hb, wl_ref[...])
        zg_ref[...] = zg
        logit = _mm(zg, wgu_ref[...]) + bg_ref[...]
        la_ref[...] = (jnp.minimum(logit, 0.0) - jnp.log(1.0 + jnp.exp(-jnp.abs(logit)))) * (1.0 / GLA_GATE_TEMP)

    row = lambda i: (i, 0)
    full = lambda a: pl.BlockSpec(a.shape, lambda i: (0,) * a.ndim)
    out_w = list(widths) + [LANE, kw]
    return _tc_call(
        body, name=name, grid=(t // tm,),
        in_specs=[pl.BlockSpec((tm, d), row), full(gam), full(bet), full(w_main), full(w_lr), full(wgu), full(b_gate)],
        out_specs=[pl.BlockSpec((tm, w), row) for w in out_w] + [pl.BlockSpec((tm, d), row)],
        out_shape=[jax.ShapeDtypeStruct((t, w), F32) for w in out_w] + [jax.ShapeDtypeStruct((t, d), BF16)],
        compiler_params=_seq(1),
    )(xhat, gam, bet, w_main, w_lr, wgu, b_gate)


def _inproj_bwd(dh_part, pieces, dzg, w_main, w_lr, name):
    t, d = dh_part.shape
    tm = _row_tile(t)
    widths = [p.shape[1] for p in pieces]
    offs = [0]
    for w in widths:
        offs.append(offs[-1] + w)

    def body(*refs):
        dhp_ref = refs[0]
        p_refs = refs[1:1 + len(widths)]
        dzg_ref, wm_ref, wl_ref, dh_ref, dz_ref = refs[1 + len(widths):]
        acc = dhp_ref[...] + _mm(dzg_ref[...], wl_ref[...])
        for p, ref in enumerate(p_refs):
            v = ref[...]
            dz_ref[:, offs[p]:offs[p + 1]] = v
            acc += _mm(v, wm_ref[offs[p]:offs[p + 1], :])
        dh_ref[...] = acc

    row = lambda i: (i, 0)
    full = lambda a: pl.BlockSpec(a.shape, lambda i: (0,) * a.ndim)
    return _tc_call(
        body, name=name, grid=(t // tm,),
        in_specs=[pl.BlockSpec((tm, d), row)] + [pl.BlockSpec((tm, w), row) for w in widths]
                 + [pl.BlockSpec((tm, LANE), row), full(w_main), full(w_lr)],
        out_specs=[pl.BlockSpec((tm, d), row), pl.BlockSpec((tm, offs[-1]), row)],
        out_shape=[jax.ShapeDtypeStruct((t, d), F32), jax.ShapeDtypeStruct((t, offs[-1]), BF16)],
        compiler_params=_seq(1),
    )(dh_part, *pieces, dzg, w_main, w_lr)


def _pool_cnt(tile, tm, w):
    t = tile * tm + lax.broadcasted_iota(jnp.int32, (tm, 1), 0)
    return jnp.minimum(t + 1, w).astype(F32)


def _pool_fwd(u, wp, scale, name):
    t, pw = u.shape
    tm = _row_tile(t)
    gd = wp.shape[1]

    def body(u_ref, wp_ref, sc_ref, y_ref, p_ref, ext):
        i = pl.program_id(0)

        @pl.when(i == 0)
        def _():
            ext[0:POOL_HALO, :] = jnp.zeros((POOL_HALO, pw), F32)

        ext[POOL_HALO:POOL_HALO + tm, :] = u_ref[...]
        for gi, w in enumerate(POOL_WINDOWS):
            cols = slice(gi * gd, (gi + 1) * gd)
            s = ext[pl.ds(POOL_HALO, tm), cols]
            tot = s
            for back in range(1, w):
                tot = tot + ext[pl.ds(POOL_HALO - back, tm), cols]
            p = (tot / _pool_cnt(i, tm, w) - s).astype(BF16)
            p_ref[:, cols] = p
            y_ref[:, cols] = (jnp.dot(p, wp_ref[gi], preferred_element_type=F32) * sc_ref[:, cols]).astype(BF16)
        ext[0:POOL_HALO, :] = ext[tm:tm + POOL_HALO, :]

    row = lambda i: (i, 0)
    return _tc_call(
        body, name=name, grid=(t // tm,),
        in_specs=[pl.BlockSpec((tm, pw), row), pl.BlockSpec(wp.shape, lambda i: (0, 0, 0)),
                  pl.BlockSpec((1, pw), lambda i: (0, 0))],
        out_specs=[pl.BlockSpec((tm, pw), row), pl.BlockSpec((tm, pw), row)],
        out_shape=[jax.ShapeDtypeStruct((t, pw), BF16), jax.ShapeDtypeStruct((t, pw), BF16)],
        scratch_shapes=[pltpu.VMEM((tm + POOL_HALO, pw), F32)],
        compiler_params=_seq(1),
    )(u, wp, scale)


def _pool_bwd(dy, pb, wp, scale, name):
    t, pw = dy.shape
    tm = _row_tile(t)
    nt = t // tm
    gd = wp.shape[1]

    def body(dy_ref, p_ref, wp_ref, sc_ref, du_ref, dwp_ref, dsc_ref, ext):
        i = pl.program_id(0)
        tile = nt - 1 - i

        @pl.when(i == 0)
        def _():
            ext[tm:tm + POOL_HALO, :] = jnp.zeros((POOL_HALO, pw), F32)
            dwp_ref[...] = jnp.zeros_like(dwp_ref)
            dsc_ref[...] = jnp.zeros_like(dsc_ref)

        dps = []
        for gi, w in enumerate(POOL_WINDOWS):
            cols = slice(gi * gd, (gi + 1) * gd)
            dyv = dy_ref[:, cols]
            p = p_ref[:, cols]
            dpre = (dyv * sc_ref[:, cols]).astype(BF16)
            dsc_ref[:, cols] += jnp.sum(dyv * jnp.dot(p, wp_ref[gi], preferred_element_type=F32), axis=0, keepdims=True)
            dwp_ref[gi] += _mm_tn(p, dpre)
            dp = _mm_nt(dpre, wp_ref[gi])
            dps.append(dp)
            ext[0:tm, cols] = dp / _pool_cnt(tile, tm, w)
        for gi, w in enumerate(POOL_WINDOWS):
            cols = slice(gi * gd, (gi + 1) * gd)
            tot = ext[pl.ds(0, tm), cols]
            for fwd in range(1, w):
                tot = tot + ext[pl.ds(fwd, tm), cols]
            du_ref[:, cols] = (tot - dps[gi]).astype(BF16)
        ext[tm:tm + POOL_HALO, :] = ext[0:POOL_HALO, :]

    row = lambda i: (nt - 1 - i, 0)
    return _tc_call(
        body, name=name, grid=(nt,),
        in_specs=[pl.BlockSpec((tm, pw), row), pl.BlockSpec((tm, pw), row),
                  pl.BlockSpec(wp.shape, lambda i: (0, 0, 0)), pl.BlockSpec((1, pw), lambda i: (0, 0))],
        out_specs=[pl.BlockSpec((tm, pw), row), pl.BlockSpec(wp.shape, lambda i: (0, 0, 0)),
                   pl.BlockSpec((1, pw), lambda i: (0, 0))],
        out_shape=[jax.ShapeDtypeStruct((t, pw), BF16), jax.ShapeDtypeStruct(wp.shape, F32),
                   jax.ShapeDtypeStruct((1, pw), F32)],
        scratch_shapes=[pltpu.VMEM((tm + POOL_HALO, pw), F32)],
        compiler_params=_seq(1),
    )(dy, pb, wp, scale)


def _gla_masks(kw, vw):
    dk, dv = kw // N_HEADS, vw // N_HEADS
    lane_k = lax.broadcasted_iota(jnp.int32, (1, kw), 1)
    lane_v = lax.broadcasted_iota(jnp.int32, (1, vw), 1)
    hk = [((lane_k >= h * dk) & (lane_k < (h + 1) * dk)).astype(F32) for h in range(N_HEADS)]
    hv = [((lane_v >= h * dv) & (lane_v < (h + 1) * dv)).astype(F32) for h in range(N_HEADS)]
    r = lax.broadcasted_iota(jnp.int32, (CHUNK, CHUNK), 0)
    c = lax.broadcasted_iota(jnp.int32, (CHUNK, CHUNK), 1)
    tril = r >= c
    rs = lax.broadcasted_iota(jnp.int32, (N_HEADS * CHUNK, CHUNK), 0) & (CHUNK - 1)
    stril = rs >= lax.broadcasted_iota(jnp.int32, (N_HEADS * CHUNK, CHUNK), 1)
    return hk, hv, tril, stril


def _block_diag(x, hk, dv):
    return jnp.concatenate([x[h * dv:(h + 1) * dv, :] * hk[h] for h in range(N_HEADS)], axis=0)


def _gla_fwd(q, k, v, loga, r, gnorm, name):
    t, kw = q.shape
    vw = v.shape[1]
    dk, dv = kw // N_HEADS, vw // N_HEADS
    tm = _row_tile(t)
    nc = tm // CHUNK
    qscale = dk ** -0.5

    def body(q_ref, k_ref, v_ref, la_ref, r_ref, gn_ref, o_ref, y_ref, sall_ref, st):
        @pl.when(pl.program_id(0) == 0)
        def _():
            st[...] = jnp.zeros_like(st)

        hk, hv, tril, stril = _gla_masks(kw, vw)
        trif = tril.astype(F32)

        def chunk(c, carry):
            rows = pl.ds(pl.multiple_of(c * CHUNK, CHUNK), CHUNK)
            la = la_ref[rows, :]
            b = _mm_f32(trif, la)
            bl = jnp.sum(la, axis=0, keepdims=True)
            qb = q_ref[rows, :] * (qscale * jnp.exp(b))
            kk = k_ref[rows, :]
            kb = kk * jnp.exp(-b)
            kl = kk * jnp.exp(bl - b)
            vv = v_ref[rows, :]
            s_t = st[...]
            compact = s_t[0:dv, :]
            for h in range(1, N_HEADS):
                compact = compact + s_t[h * dv:(h + 1) * dv, :]
            sall_ref[c] = compact
            qx = jnp.concatenate([qb * hk[h] for h in range(N_HEADS)], axis=0)
            a = jnp.where(stril, _mm_nt(qx, kb), 0.0).astype(BF16)
            o_inter = _mm_nt(qb, s_t)
            for h in range(N_HEADS):
                vs = slice(h * dv, (h + 1) * dv)
                o_ref[rows, vs] = o_inter[:, vs] + _mm(a[h * CHUNK:(h + 1) * CHUNK, :], vv[:, vs])
            st[...] = s_t * jnp.exp(bl) + _block_diag(_mm_tn(vv, kl), hk, dv)
            return carry

        lax.fori_loop(0, nc, chunk, 0, unroll=CHUNK_UNROLL)
        for h in range(N_HEADS):
            vs = slice(h * dv, (h + 1) * dv)
            oh = o_ref[:, vs]
            on = oh * lax.rsqrt(jnp.mean(oh * oh, axis=-1, keepdims=True) + RMS_EPS)
            _, sl = _silu_parts(r_ref[:, vs])
            y_ref[:, vs] = (on * gn_ref[:, vs] * sl).astype(BF16)

    row = lambda i: (i, 0)
    return _tc_call(
        body, name=name, grid=(t // tm,),
        in_specs=[pl.BlockSpec((tm, kw), row), pl.BlockSpec((tm, kw), row), pl.BlockSpec((tm, vw), row),
                  pl.BlockSpec((tm, kw), row), pl.BlockSpec((tm, vw), row), pl.BlockSpec((1, vw), lambda i: (0, 0))],
        out_specs=[pl.BlockSpec((tm, vw), row), pl.BlockSpec((tm, vw), row),
                   pl.BlockSpec((nc, dv, kw), lambda i: (i, 0, 0))],
        out_shape=[jax.ShapeDtypeStruct((t, vw), F32), jax.ShapeDtypeStruct((t, vw), BF16),
                   jax.ShapeDtypeStruct((t // CHUNK, dv, kw), F32)],
        scratch_shapes=[pltpu.VMEM((vw, kw), F32)],
        compiler_params=_seq(1),
    )(q, k, v, loga, r, gnorm)


def _gla_bwd(dy, o, r, gnorm, q, k, v, loga, zg, sall, wgu, name):
    t, kw = q.shape
    vw = v.shape[1]
    dk, dv = kw // N_HEADS, vw // N_HEADS
    tm = _row_tile(t)
    nt = t // tm
    nc = tm // CHUNK
    qscale = dk ** -0.5

    def body(dy_ref, o_ref, r_ref, gn_ref, q_ref, k_ref, v_ref, la_ref, zg_ref, sall_ref, wgu_ref,
             dq_ref, dk_ref, dv_ref, dr_ref, dzg_ref, dwgu_ref, dbg_ref, dgn_ref, dst, do_s):
        @pl.when(pl.program_id(0) == 0)
        def _():
            dst[...] = jnp.zeros_like(dst)
            dwgu_ref[...] = jnp.zeros_like(dwgu_ref)
            dbg_ref[...] = jnp.zeros_like(dbg_ref)
            dgn_ref[...] = jnp.zeros_like(dgn_ref)

        for h in range(N_HEADS):
            vs = slice(h * dv, (h + 1) * dv)
            oh = o_ref[:, vs]
            rinv = lax.rsqrt(jnp.mean(oh * oh, axis=-1, keepdims=True) + RMS_EPS)
            on = oh * rinv
            rr = r_ref[:, vs]
            sg, sl = _silu_parts(rr)
            dyv = dy_ref[:, vs]
            gn = gn_ref[:, vs]
            dgn_ref[:, vs] += jnp.sum(dyv * on * sl, axis=0, keepdims=True)
            dr_ref[:, vs] = (dyv * on * gn * (sg * (1.0 + rr * (1.0 - sg)))).astype(BF16)
            don = dyv * gn * sl
            do_s[:, vs] = rinv * (don - on * jnp.mean(don * on, axis=-1, keepdims=True))

        hk, hv, tril, stril = _gla_masks(kw, vw)
        trif = tril.astype(F32)
        triuf = (lax.broadcasted_iota(jnp.int32, (CHUNK, CHUNK), 0)
                 <= lax.broadcasted_iota(jnp.int32, (CHUNK, CHUNK), 1)).astype(F32)
        last_row = lax.broadcasted_iota(jnp.int32, (CHUNK, 1), 0) == CHUNK - 1

        def chunk(idx, carry):
            c = nc - 1 - idx
            rows = pl.ds(pl.multiple_of(c * CHUNK, CHUNK), CHUNK)
            la = la_ref[rows, :]
            b = _mm_f32(trif, la)
            bl = jnp.sum(la, axis=0, keepdims=True)
            eb = jnp.exp(b)
            enb = jnp.exp(-b)
            ebl = jnp.exp(bl - b)
            el = jnp.exp(bl)
            qb = q_ref[rows, :] * (qscale * eb)
            kk = k_ref[rows, :]
            kb = kk * enb
            kl = kk * ebl
            vv = v_ref[rows, :]
            do = do_s[rows, :]
            compact = sall_ref[c]
            s_t = jnp.concatenate([compact * hk[h] for h in range(N_HEADS)], axis=0)
            ds_t = dst[...]
            qx = jnp.concatenate([qb * hk[h] for h in range(N_HEADS)], axis=0)
            dox = jnp.concatenate([do * hv[h] for h in range(N_HEADS)], axis=0)
            a = jnp.where(stril, _mm_nt(qx, kb), 0.0).astype(BF16)
            da = jnp.where(stril, _mm_nt(dox, vv), 0.0).astype(BF16)
            dv_ref[rows, :] = (_mm_tn(a, dox) + _mm_nt(kl, ds_t)).astype(BF16)
            dak = _mm(da, kb)
            dqb = _mm(do, s_t)
            for h in range(N_HEADS):
                dqb = dqb + dak[h * CHUNK:(h + 1) * CHUNK, :] * hk[h]
            dkb = _mm_tn(da, qx)
            dkl = _mm(vv, ds_t)
            dbl = jnp.sum(dkl * kl, axis=0, keepdims=True) + el * jnp.sum(ds_t * s_t, axis=0, keepdims=True)
            dst[...] = ds_t * el + _block_diag(_mm_tn(do, qb), hk, dv)
            dq_ref[rows, :] = (dqb * (qscale * eb)).astype(BF16)
            dk_ref[rows, :] = (dkb * enb + dkl * ebl).astype(BF16)
            db = dqb * qb - dkb * kb - dkl * kl + jnp.where(last_row, dbl, 0.0)
            dla = _mm_f32(triuf, db)
            dlogit = dla * (1.0 / GLA_GATE_TEMP) * (1.0 - jnp.exp(GLA_GATE_TEMP * la))
            dzg_ref[rows, :] = _mm_nt(dlogit, wgu_ref[...]).astype(BF16)
            dwgu_ref[...] += _mm_tn(zg_ref[rows, :], dlogit)
            dbg_ref[...] += jnp.sum(dlogit, axis=0, keepdims=True)
            return carry

        lax.fori_loop(0, nc, chunk, 0, unroll=CHUNK_UNROLL)

    row = lambda i: (nt - 1 - i, 0)
    const = lambda i: (0, 0)
    return _tc_call(
        body, name=name, grid=(nt,),
        in_specs=[pl.BlockSpec((tm, vw), row), pl.BlockSpec((tm, vw), row), pl.BlockSpec((tm, vw), row),
                  pl.BlockSpec((1, vw), const), pl.BlockSpec((tm, kw), row), pl.BlockSpec((tm, kw), row),
                  pl.BlockSpec((tm, vw), row), pl.BlockSpec((tm, kw), row), pl.BlockSpec((tm, LANE), row),
                  pl.BlockSpec((nc, dv, kw), lambda i: (nt - 1 - i, 0, 0)), pl.BlockSpec((LANE, kw), const)],
        out_specs=[pl.BlockSpec((tm, kw), row), pl.BlockSpec((tm, kw), row), pl.BlockSpec((tm, vw), row),
                   pl.BlockSpec((tm, vw), row), pl.BlockSpec((tm, LANE), row), pl.BlockSpec((LANE, kw), const),
                   pl.BlockSpec((1, kw), const), pl.BlockSpec((1, vw), const)],
        out_shape=[jax.ShapeDtypeStruct((t, kw), BF16), jax.ShapeDtypeStruct((t, kw), BF16),
                   jax.ShapeDtypeStruct((t, vw), BF16), jax.ShapeDtypeStruct((t, vw), BF16),
                   jax.ShapeDtypeStruct((t, LANE), BF16), jax.ShapeDtypeStruct((LANE, kw), F32),
                   jax.ShapeDtypeStruct((1, kw), F32), jax.ShapeDtypeStruct((1, vw), F32)],
        scratch_shapes=[pltpu.VMEM((vw, kw), F32), pltpu.VMEM((tm, vw), F32)],
        compiler_params=_seq(1),
    )(dy, o, r, gnorm, q, k, v, loga, zg, sall, wgu)


def _outproj_fwd(yp, yg, w_out, xhat, gam, bet, alpha, name):
    t, d = xhat.shape
    pw = yp.shape[1]
    tm = _row_tile(t)

    def body(yp_ref, yg_ref, w_ref, x_ref, g_ref, b_ref, xhat_ref, rstd_ref):
        h = x_ref[...] * g_ref[...] + b_ref[...]
        y = (jnp.dot(yp_ref[...], w_ref[0:pw, :], preferred_element_type=F32)
             + jnp.dot(yg_ref[...], w_ref[pw:, :], preferred_element_type=F32))
        xh, rs = _ln_stats(alpha * h + y)
        xhat_ref[...] = xh
        rstd_ref[...] = rs

    row = lambda i: (i, 0)
    vec = pl.BlockSpec((1, d), lambda i: (0, 0))
    return _tc_call(
        body, name=name, grid=(t // tm,),
        in_specs=[pl.BlockSpec((tm, pw), row), pl.BlockSpec((tm, yg.shape[1]), row),
                  pl.BlockSpec(w_out.shape, lambda i: (0, 0)), pl.BlockSpec((tm, d), row), vec, vec],
        out_specs=[pl.BlockSpec((tm, d), row), pl.BlockSpec((tm, 1), row)],
        out_shape=[jax.ShapeDtypeStruct((t, d), F32), jax.ShapeDtypeStruct((t, 1), F32)],
        compiler_params=_seq(1),
    )(yp, yg, w_out, xhat, gam, bet)


def _outproj_bwd(dh, xhat, rstd, ln_g, w_out, pw, alpha, name):
    t, d = dh.shape
    tm = _row_tile(t)
    gw = w_out.shape[0] - pw

    def body(dh_ref, xh_ref, rs_ref, g_ref, w_ref, dyb_ref, dyp_ref, dyg_ref, dres_ref, dgam_ref, dbet_ref):
        @pl.when(pl.program_id(0) == 0)
        def _():
            dgam_ref[...] = jnp.zeros_like(dgam_ref)
            dbet_ref[...] = jnp.zeros_like(dbet_ref)

        dy = dh_ref[...]
        xh = xh_ref[...]
        dr = _ln_bwd(dy, xh, rs_ref[...], g_ref[...])
        dgam_ref[...] += jnp.sum(dy * xh, axis=0, keepdims=True)
        dbet_ref[...] += jnp.sum(dy, axis=0, keepdims=True)
        drb = dr.astype(BF16)
        dyb_ref[...] = drb
        dres_ref[...] = alpha * dr
        dyp_ref[...] = _mm_nt(drb, w_ref[0:pw, :])
        dyg_ref[...] = _mm_nt(drb, w_ref[pw:, :])

    row = lambda i: (i, 0)
    vec = pl.BlockSpec((1, d), lambda i: (0, 0))
    return _tc_call(
        body, name=name, grid=(t // tm,),
        in_specs=[pl.BlockSpec((tm, d), row), pl.BlockSpec((tm, d), row), pl.BlockSpec((tm, 1), row), vec,
                  pl.BlockSpec(w_out.shape, lambda i: (0, 0))],
        out_specs=[pl.BlockSpec((tm, d), row), pl.BlockSpec((tm, pw), row), pl.BlockSpec((tm, gw), row),
                   pl.BlockSpec((tm, d), row), vec, vec],
        out_shape=[jax.ShapeDtypeStruct((t, d), BF16), jax.ShapeDtypeStruct((t, pw), F32),
                   jax.ShapeDtypeStruct((t, gw), F32), jax.ShapeDtypeStruct((t, d), F32),
                   jax.ShapeDtypeStruct((1, d), F32), jax.ShapeDtypeStruct((1, d), F32)],
        compiler_params=_seq(1),
    )(dh, xhat, rstd, ln_g, w_out)


def _loss_head(xhat, gam, bet, target, n_rows, name):
    t, d = xhat.shape
    tm = _row_tile(t)

    def body(x_ref, g_ref, b_ref, t_ref, dy_ref, loss_ref):
        i = pl.program_id(0)

        @pl.when(i == 0)
        def _():
            loss_ref[...] = jnp.zeros_like(loss_ref)

        rowi = i * tm + lax.broadcasted_iota(jnp.int32, (tm, 1), 0)
        live = (rowi >= N_META) & (rowi < N_META + n_rows)
        diff = jnp.where(live, x_ref[...] * g_ref[...] + b_ref[...] - t_ref[...], 0.0)
        dy_ref[...] = diff * (1.0 / d)
        loss_ref[...] += jnp.sum(diff * diff) * (0.5 / d)

    row = lambda i: (i, 0)
    vec = pl.BlockSpec((1, d), lambda i: (0, 0))
    return _tc_call(
        body, name=name, grid=(t // tm,),
        in_specs=[pl.BlockSpec((tm, d), row), vec, vec, pl.BlockSpec((tm, d), row)],
        out_specs=[pl.BlockSpec((tm, d), row), pl.BlockSpec((8, LANE), lambda i: (0, 0))],
        out_shape=[jax.ShapeDtypeStruct((t, d), F32), jax.ShapeDtypeStruct((8, LANE), F32)],
        compiler_params=_seq(1),
    )(xhat, gam, bet, target)


def _rows_block(r, c):
    best = r
    for cand in range(8, r, 8):
        if r % cand == 0 and cand * c * 4 <= (1 << 20):
            best = cand
    return best if best * c * 4 <= (4 << 20) else r


def _sum_slots(recvs, name, layers_side_by_side=False):
    nl = len(recvs)
    ns, r, c = recvs[0].shape
    tr = _rows_block(r, c)

    def body(*refs):
        o_ref = refs[nl]
        for l in range(nl):
            acc = refs[l][0].astype(F32)
            for s in range(1, ns):
                acc = acc + refs[l][s].astype(F32)
            if layers_side_by_side:
                o_ref[0, :, l * c:(l + 1) * c] = acc
            else:
                o_ref[l] = acc

    out = (1, r, nl * c) if layers_side_by_side else (nl, r, c)
    return _tc_call(
        body, name=name, grid=(r // tr,),
        in_specs=[pl.BlockSpec((ns, tr, c), lambda i: (0, i, 0))] * nl,
        out_specs=pl.BlockSpec((out[0], tr, out[2]), lambda i: (0, i, 0)),
        out_shape=jax.ShapeDtypeStruct(out, F32),
        compiler_params=_seq(1),
    )(*recvs)


def _adamw(w, terms, m, v, name):
    nl, r, c = w.shape
    tc = c
    while tc % (2 * LANE) == 0 and tc > 4 * LANE:
        tc //= 2
    tr = _rows_block(r, tc)
    nterm = len(terms)

    def body(*refs):
        w_ref = refs[0]
        t_refs = refs[1:1 + nterm]
        m_ref, v_ref, g_ref, d_ref, nm_ref, nv_ref = refs[1 + nterm:]
        g = t_refs[0][...]
        for tr_ in t_refs[1:]:
            g = g + tr_[...]
        nm = ADAM_B1 * m_ref[...] + (1.0 - ADAM_B1) * g
        nv = ADAM_B2 * v_ref[...] + (1.0 - ADAM_B2) * jnp.square(g)
        m_hat = nm / (1.0 - ADAM_B1 ** ADAM_STEP)
        v_hat = nv / (1.0 - ADAM_B2 ** ADAM_STEP)
        g_ref[...] = g
        d_ref[...] = -ADAM_LR * (m_hat / (jnp.sqrt(v_hat) + ADAM_EPS) + ADAM_WD * w_ref[...])
        nm_ref[...] = nm
        nv_ref[...] = nv

    spec = pl.BlockSpec((None, tr, tc), lambda l, i, j: (l, i, j))
    shp = jax.ShapeDtypeStruct((nl, r, c), F32)
    return _tc_call(
        body, name=name, grid=(nl, r // tr, c // tc),
        in_specs=[spec] * (3 + nterm), out_specs=[spec] * 4, out_shape=[shp] * 4,
        compiler_params=_seq(3),
    )(w, *terms, m, v)


XY_RELATIONS = ((1, 0, 0), (0, 1, 0), (1, 1, 0))
ALL_RELATIONS = tuple((fx, fy, fc) for fx in (0, 1) for fy in (0, 1) for fc in (0, 1) if fx or fy or fc)
HBM_SPEC = pl.BlockSpec(memory_space=pltpu.HBM)
SEM_SPEC = pl.BlockSpec(memory_space=pltpu.SEMAPHORE)
DATAFLOW = pltpu.SideEffectType.DATAFLOW_SIDE_EFFECTING


def _split_call(body, **kw):
    return pl.pallas_call(body, **kw)


def _flip(v, f):
    return 1 - v if f else v


def _any_spec(n):
    return [pl.BlockSpec(memory_space=pl.ANY)] * n


def _relations(kind):
    return ALL_RELATIONS if kind == "bcast" else XY_RELATIONS


def _copies(kind, arr, land, sems):
    x, y, c = lax.axis_index("x"), lax.axis_index("y"), lax.axis_index("c")
    out = []
    for (fx, fy, fc), (send_sem, recv_sem) in zip(_relations(kind), sems):
        px, py, pc = _flip(x, fx), _flip(y, fy), _flip(c, fc)
        if kind == "bcast":
            mine, theirs = 4 * x + 2 * y + c, 4 * px + 2 * py + pc
        else:
            mine, theirs = 2 * x + y, 2 * px + py
        src = arr.at[theirs] if kind == "scatter" else arr
        both = dict(src_ref=src, send_sem=send_sem, recv_sem=recv_sem, device_id=(px, py, pc), device_id_type=MESH)
        out.append((pltpu.make_async_remote_copy(dst_ref=land.at[mine], **both),
                    pltpu.make_async_remote_copy(dst_ref=land.at[theirs], **both)))
    return out


def _sem_pairs(kinds, sems):
    out, at = [], 0
    for kind in kinds:
        nrel = len(_relations(kind))
        out.append([(sems[at + 2 * r], sems[at + 2 * r + 1]) for r in range(nrel)])
        at += 2 * nrel
    return out


def _exchange_start(name, kinds, arrs, lands):
    n = len(arrs)
    nsem = sum(2 * len(_relations(kd)) for kd in kinds)

    def body(*refs):
        a_refs, l_refs = refs[:n], refs[n:2 * n]
        pairs = _sem_pairs(kinds, refs[2 * n:2 * n + nsem])
        token = refs[-1]
        for k in range(n):
            for send, _ in _copies(kinds[k], a_refs[k], l_refs[k], pairs[k]):
                send.start()
        token[...] = jnp.zeros_like(token)

    thru = [pltpu.HBM(a.shape, a.dtype) for a in list(arrs) + list(lands)]
    outs = _split_call(
        body, name=name,
        out_shape=(*[pltpu.SemaphoreType.DMA(())] * nsem, *thru, jax.ShapeDtypeStruct((8, LANE), F32)),
        in_specs=[HBM_SPEC] * (2 * n),
        out_specs=(*[SEM_SPEC] * nsem, *[HBM_SPEC] * (2 * n), pl.BlockSpec(memory_space=pltpu.VMEM)),
        input_output_aliases={i: nsem + i for i in range(2 * n)},
        compiler_params=pltpu.CompilerParams(has_side_effects=DATAFLOW),
    )(*[pltpu.with_memory_space_constraint(a, pltpu.HBM) for a in list(arrs) + list(lands)])
    return dict(kinds=kinds, sems=outs[:nsem], arrs=outs[nsem:nsem + n], lands=outs[nsem + n:nsem + 2 * n],
                token=outs[-1])


def _exchange_wait(name, st, after):
    kinds = st["kinds"]
    n = len(kinds)
    nsem = len(st["sems"])

    def body(*refs):
        a_refs, l_refs = refs[:n], refs[n:2 * n]
        pairs = _sem_pairs(kinds, refs[2 * n:2 * n + nsem])
        for k in range(n):
            for _, arrival in _copies(kinds[k], a_refs[k], l_refs[k], pairs[k]):
                arrival.wait_send()
                arrival.wait_recv()
        refs[-1][...] = jnp.zeros_like(refs[-1])

    ins = list(st["arrs"]) + list(st["lands"])
    outs = _split_call(
        body, name=name,
        out_shape=[pltpu.HBM(a.shape, a.dtype) for a in ins] + [jax.ShapeDtypeStruct((8, LANE), F32)],
        in_specs=[HBM_SPEC] * (2 * n) + [SEM_SPEC] * nsem + [pl.BlockSpec(memory_space=pl.ANY)],
        out_specs=[HBM_SPEC] * (2 * n) + [pl.BlockSpec(memory_space=pltpu.VMEM)],
        input_output_aliases={i: i for i in range(2 * n)},
        compiler_params=pltpu.CompilerParams(has_side_effects=DATAFLOW),
    )(*ins, *st["sems"], after)
    return outs[n:2 * n], outs[-1]


def _landing(own, slot, nslot):
    return lax.dynamic_update_slice(lax.empty((nslot,) + own.shape, own.dtype), own[None], (slot,) + (0,) * own.ndim)


def _swap_sibling(parts, name):
    n = len(parts)

    def body(*refs):
        ins, outs = refs[:n], refs[n:2 * n]
        send_sems, recv_sems = refs[2 * n:]
        sib = (lax.axis_index("x"), lax.axis_index("y"), 1 - lax.axis_index("c"))
        cps = [pltpu.make_async_remote_copy(src_ref=ins[k], dst_ref=outs[k], send_sem=send_sems.at[k],
                                            recv_sem=recv_sems.at[k], device_id=sib, device_id_type=MESH)
               for k in range(n)]
        for cp in cps:
            cp.start()
        for cp in cps:
            cp.wait_recv()
        for cp in cps:
            cp.wait_send()

    return _comm_call(
        body, name=name,
        in_specs=_any_spec(n), out_specs=_any_spec(n),
        out_shape=[jax.ShapeDtypeStruct(p.shape, p.dtype) for p in parts],
        scratch_shapes=[pltpu.SemaphoreType.DMA((n,)), pltpu.SemaphoreType.DMA((n,))],
    )(*parts)


def _col_shards(a, n=N_SHARD):
    r, c = a.shape
    return a.reshape(r, n, c // n).transpose(1, 0, 2)


def _from_col_shards(a):
    n, r, cs = a.shape
    return a.transpose(1, 0, 2).reshape(r, n * cs)


def kernel(x, meta_tokens, ffn1_w_gate, ffn1_w_up, ffn1_w_down, ln1_g, ln1_b, w_in, w_gate_up, b_gate, w_pool, pool_scale, gla_norm_g, w_out, ln2_g, ln2_b, ffn2_w_gate, ffn2_w_up, ffn2_w_down, ln3_g, ln3_b, loss_target, m_meta_tokens, m_ffn1_w_gate, m_ffn1_w_up, m_ffn1_w_down, m_ln1_g, m_ln1_b, m_w_in, m_w_gate_up, m_b_gate, m_w_pool, m_pool_scale, m_gla_norm_g, m_w_out, m_ln2_g, m_ln2_b, m_ffn2_w_gate, m_ffn2_w_up, m_ffn2_w_down, m_ln3_g, m_ln3_b, v_meta_tokens, v_ffn1_w_gate, v_ffn1_w_up, v_ffn1_w_down, v_ln1_g, v_ln1_b, v_w_in, v_w_gate_up, v_b_gate, v_w_pool, v_pool_scale, v_gla_norm_g, v_w_out, v_ln2_g, v_ln2_b, v_ffn2_w_gate, v_ffn2_w_up, v_ffn2_w_down, v_ln3_g, v_ln3_b):
    w = dict(meta_tokens=meta_tokens, ffn1_w_gate=ffn1_w_gate, ffn1_w_up=ffn1_w_up, ffn1_w_down=ffn1_w_down,
             ln1_g=ln1_g, ln1_b=ln1_b, w_in=w_in, w_gate_up=w_gate_up, b_gate=b_gate, w_pool=w_pool,
             pool_scale=pool_scale, gla_norm_g=gla_norm_g, w_out=w_out, ln2_g=ln2_g, ln2_b=ln2_b,
             ffn2_w_gate=ffn2_w_gate, ffn2_w_up=ffn2_w_up, ffn2_w_down=ffn2_w_down, ln3_g=ln3_g, ln3_b=ln3_b)
    mom1 = dict(meta_tokens=m_meta_tokens, ffn1_w_gate=m_ffn1_w_gate, ffn1_w_up=m_ffn1_w_up,
                ffn1_w_down=m_ffn1_w_down, ln1_g=m_ln1_g, ln1_b=m_ln1_b, w_in=m_w_in, w_gate_up=m_w_gate_up,
                b_gate=m_b_gate, w_pool=m_w_pool, pool_scale=m_pool_scale, gla_norm_g=m_gla_norm_g, w_out=m_w_out,
                ln2_g=m_ln2_g, ln2_b=m_ln2_b, ffn2_w_gate=m_ffn2_w_gate, ffn2_w_up=m_ffn2_w_up,
                ffn2_w_down=m_ffn2_w_down, ln3_g=m_ln3_g, ln3_b=m_ln3_b)
    mom2 = dict(meta_tokens=v_meta_tokens, ffn1_w_gate=v_ffn1_w_gate, ffn1_w_up=v_ffn1_w_up,
                ffn1_w_down=v_ffn1_w_down, ln1_g=v_ln1_g, ln1_b=v_ln1_b, w_in=v_w_in, w_gate_up=v_w_gate_up,
                b_gate=v_b_gate, w_pool=v_w_pool, pool_scale=v_pool_scale, gla_norm_g=v_gla_norm_g, w_out=v_w_out,
                ln2_g=v_ln2_g, ln2_b=v_ln2_b, ffn2_w_gate=v_ffn2_w_gate, ffn2_w_up=v_ffn2_w_up,
                ffn2_w_down=v_ffn2_w_down, ln3_g=v_ln3_g, ln3_b=v_ln3_b)

    xs = x[0]
    s_len, d = xs.shape
    nl = ln1_g.shape[0]
    alpha = (2.0 * nl) ** 0.25
    t_real = N_META + s_len
    t_pad = -(-t_real // LANE) * LANE
    pw = pool_scale.shape[1]
    kw = b_gate.shape[1]
    vw = gla_norm_g.shape[1]
    rank = w_gate_up.shape[1]
    widths = (pw, kw, kw, vw, vw)
    n_main = sum(widths)
    dff_s = ffn1_w_gate.shape[2]

    me_xy = 2 * lax.axis_index("x") + lax.axis_index("y")
    me_all = 2 * me_xy + lax.axis_index("c")
    ffn1_names = ("ffn1_w_gate", "ffn1_w_up", "ffn1_w_down")
    mix_names = ("w_out", "w_gate_up", "w_in")
    ffn2_names = ("ffn2_w_gate", "ffn2_w_up", "ffn2_w_down")

    gate_up = ("ffn1_w_gate", "ffn1_w_up", "ffn2_w_gate", "ffn2_w_up")

    def stored(n, a):
        if n in gate_up:
            return jnp.swapaxes(a, 1, 2)
        return jnp.transpose(a, (2, 0, 1)) if n == "w_in" else a

    def as_given(n, a):
        if n in gate_up:
            return jnp.swapaxes(a, 1, 2)
        if n == "w_in":
            return jnp.transpose(a.reshape(-1, nl, d), (1, 2, 0))
        return a.reshape(w[n].shape)

    stages = [[("meta_tokens", None)], [(n, 0) for n in ffn1_names], [(n, 0) for n in mix_names + ffn2_names]]
    stages += [[(n, l) for n in BIG] for l in range(1, nl)]
    gathers, wa = {}, {}

    def start_gather(si, dep=None):
        own = []
        for n, l in stages[si]:
            a = meta_tokens if l is None else (stored(n, w[n])[:, l] if n == "w_in" else stored(n, w[n])[l])
            own.append(a if l is None else (a if dep is None else a + dep).astype(BF16))
        gathers[si] = _exchange_start(f"gather_start_{si}", ["gather"] * len(own), own,
                                      [_landing(a, me_xy, N_SHARD) for a in own])
        return gathers[si]["token"]

    def arrive(si, after):
        lands, token = _exchange_wait(f"gather_wait_{si}", gathers[si], after)
        for item, a in zip(stages[si], lands):
            wa[item] = a
        return token

    def mixer_weights(l):
        wi = wa["w_in", l].reshape(-1, d)
        return dict(w_main=wi[:n_main], w_lr=jnp.pad(wi[n_main:], ((0, LANE - rank), (0, 0))),
                    wgu=jnp.pad(_from_col_shards(wa["w_gate_up", l]), ((0, LANE - rank), (0, 0))),
                    wout=wa["w_out", l].reshape(-1, d))

    wp16 = w_pool.astype(BF16)
    ones = jnp.ones((1, d), F32)
    zeros = jnp.zeros((1, d), F32)
    target = jnp.concatenate([jnp.zeros((N_META, d), F32), loss_target[0], jnp.zeros((t_pad - t_real, d), F32)], axis=0)

    arrive(0, start_gather(0) + start_gather(1) + start_gather(2))
    meta_full = _from_col_shards(wa["meta_tokens", None])
    h0 = jnp.concatenate([meta_full, xs, jnp.zeros((t_pad - t_real, d), F32)], axis=0)
    landed = arrive(1, h0[:8, :LANE] + target[:8, :LANE])

    saved, mw = [], []
    cur, cur_g, cur_b = h0, ones, zeros
    for l in range(nl):
        s = {}
        xh1, rs1, hb0, g1, u1 = _ffn_fwd(cur, cur_g, cur_b, wa["ffn1_w_gate", l], wa["ffn1_w_up", l],
                                         wa["ffn1_w_down", l], alpha, f"ffn1_fwd_{l}")
        if l == 0:
            landed = arrive(2, xh1)
        gam1 = ln1_g[l:l + 1]
        if l + 1 < nl:
            gam1 = gam1 + start_gather(l + 3, landed[0:1, 0:1])[0:1, 0:1]
        mw.append(mixer_weights(l))
        up, q, k, v, r, zg, la, hb1 = _inproj_fwd(xh1, gam1, ln1_b[l:l + 1], mw[l]["w_main"], mw[l]["w_lr"],
                                                  mw[l]["wgu"], b_gate[l:l + 1], widths, f"inproj_fwd_{l}")
        yp, pb = _pool_fwd(up, wp16[l], pool_scale[l:l + 1], f"pool_fwd_{l}")
        o, yg, sall = _gla_fwd(q, k, v, la, r, gla_norm_g[l:l + 1], f"gla_fwd_{l}")
        xh2, rs2 = _outproj_fwd(yp, yg, mw[l]["wout"], xh1, ln1_g[l:l + 1], ln1_b[l:l + 1], alpha, f"outproj_fwd_{l}")
        if l + 1 < nl:
            landed = arrive(l + 3, xh2)
        xh3, rs3, hb2, g2, u2 = _ffn_fwd(xh2, ln2_g[l:l + 1], ln2_b[l:l + 1], wa["ffn2_w_gate", l], wa["ffn2_w_up", l],
                                         wa["ffn2_w_down", l], alpha, f"ffn2_fwd_{l}")
        s.update(xh1=xh1, rs1=rs1, hb0=hb0, g1=g1, u1=u1, q=q, k=k, v=v, r=r, zg=zg, la=la, hb1=hb1, yp=yp, pb=pb,
                 o=o, yg=yg, sall=sall, xh2=xh2, rs2=rs2, xh3=xh3, rs3=rs3, hb2=hb2, g2=g2, u2=u2)
        saved.append(s)
        cur, cur_g, cur_b = xh3, ln3_g[l:l + 1], ln3_b[l:l + 1]

    dh, loss_acc = _loss_head(cur, cur_g, cur_b, target, s_len, "loss_head")
    loss = lax.psum(loss_acc[0, 0], ("x", "y", "c"))

    small_grads = {n: [None] * nl for n in SMALL}
    scatters = []

    def depart(name, items, grads, kinds=None):
        lands = [_landing(g if kd == "bcast" else lax.dynamic_index_in_dim(g, me_xy, 0, keepdims=False),
                          me_all if kd == "bcast" else me_xy, N_DEV if kd == "bcast" else N_SHARD)
                 for g, kd in zip(grads, kinds or ["scatter"] * len(grads))]
        st = _exchange_start(name, kinds or ["scatter"] * len(grads), grads, lands)
        scatters.append((name, items, st))
        return st["token"]

    def pack(parts):
        flat = jnp.concatenate([parts[n].reshape(-1) for n in SMALL])
        return flat.reshape(-1, LANE)

    def ffn_wgrads(l, names, hb, dgb, dub, act, dfb):
        tag = names[0][:4]
        return [_wgrad(dgb, hb, dff_s, d, f"{tag}_dwg_{l}").reshape(N_SHARD, dff_s, d),
                _wgrad(dub, hb, dff_s, d, f"{tag}_dwu_{l}").reshape(N_SHARD, dff_s, d),
                _wgrad(act, dfb, dff_s, d, f"{tag}_dwd_{l}").reshape(N_SHARD, dff_s, d)]

    late = []
    for l in reversed(range(nl)):
        s = saved[l]
        dh, dfb, dgb, dub, act, dgam, dbet = _ffn_bwd(dh, s["xh3"], s["rs3"], ln3_g[l:l + 1], s["g2"], s["u2"],
                                                      wa["ffn2_w_gate", l], wa["ffn2_w_up", l], wa["ffn2_w_down", l],
                                                      alpha, f"ffn2_bwd_{l}")
        small_grads["ln3_g"][l], small_grads["ln3_b"][l] = dgam, dbet
        gone = depart(f"scatter_start_ffn2_{l}", [(n, l) for n in ffn2_names],
                      ffn_wgrads(l, ffn2_names, s["hb2"], dgb, dub, act, dfb))

        dyb, dyp, dyg, dres, dgam, dbet = _outproj_bwd(dh, s["xh2"], s["rs2"], ln2_g[l:l + 1] + gone[0:1, 0:1],
                                                       mw[l]["wout"], pw, alpha, f"outproj_bwd_{l}")
        small_grads["ln2_g"][l], small_grads["ln2_b"][l] = dgam, dbet
        dwo = jnp.concatenate([_wgrad(s["yp"], dyb, pw, d, f"dwout_pool_{l}"),
                               _wgrad(s["yg"], dyb, vw, d, f"dwout_gla_{l}")], axis=0)
        dq, dk, dv, dr, dzg, dwgu, dbg, dgn = _gla_bwd(dyg, s["o"], s["r"], gla_norm_g[l:l + 1], s["q"], s["k"],
                                                       s["v"], s["la"], s["zg"], s["sall"], mw[l]["wgu"],
                                                       f"gla_bwd_{l}")
        dup, dwp, dsc = _pool_bwd(dyp, s["pb"], wp16[l], pool_scale[l:l + 1], f"pool_bwd_{l}")
        small_grads["b_gate"][l], small_grads["gla_norm_g"][l] = dbg, dgn
        small_grads["w_pool"][l], small_grads["pool_scale"][l] = dwp, dsc
        dh, dz = _inproj_bwd(dres, [dup, dq, dk, dv, dr], dzg, mw[l]["w_main"], mw[l]["w_lr"], f"inproj_bwd_{l}")
        dwi = jnp.concatenate([_wgrad(dz, s["hb1"], 4 * LANE, d, f"dwin_main_{l}"),
                               _wgrad(dzg, s["hb1"], LANE, d, f"dwin_lr_{l}")[:rank]], axis=0)
        gone = depart(f"scatter_start_mix_{l}", [(n, l) for n in mix_names],
                      [dwo.reshape(N_SHARD, -1, d), _col_shards(dwgu[:rank].astype(BF16)),
                       dwi.reshape(N_SHARD, -1, d)])

        dh, dfb, dgb, dub, act, dgam, dbet = _ffn_bwd(dh, s["xh1"], s["rs1"], ln1_g[l:l + 1] + gone[0:1, 0:1],
                                                      s["g1"], s["u1"], wa["ffn1_w_gate", l], wa["ffn1_w_up", l],
                                                      wa["ffn1_w_down", l], alpha, f"ffn1_bwd_{l}")
        small_grads["ln1_g"][l], small_grads["ln1_b"][l] = dgam, dbet
        if l:
            gone = depart(f"scatter_start_ffn1_{l}", [(n, l) for n in ffn1_names],
                          ffn_wgrads(l, ffn1_names, s["hb0"], dgb, dub, act, dfb))
            ln3_g = ln3_g.at[l - 1:l].add(gone[0:1, 0:1])
            continue
        grad_x = dh[N_META:t_real][None]
        small_vec = pack({n: jnp.stack(small_grads[n]) for n in SMALL})
        gone = depart("scatter_start_rest", [("meta_tokens", 0), ("small", 0)],
                      [_col_shards(dh[:N_META].astype(BF16)), small_vec], ["scatter", "bcast"])
        for n, a, b in zip(ffn1_names, (dgb, dub, act), (s["hb0"], s["hb0"], dfb)):
            g = _wgrad(a, b, dff_s, d, f"{n}_grad_{l}", after=gone).reshape(N_SHARD, dff_s, d)
            gone = depart(f"scatter_start_{n}", [(n, l)], [g])
            late.append(scatters.pop())

    recv, results = {}, {}

    def collect(group, after):
        for name, items, st in group:
            for item, a in zip(items, _exchange_wait(name.replace("start", "wait"), st, after)[0]):
                recv[item] = a

    def reduce_and_update(names, tag):
        partial = [_sum_slots([recv[n, l] for l in range(1 if n == "meta_tokens" else nl)], f"sum_{n}", n == "w_in")
                   for n in names]
        for n, mine, theirs in zip(names, partial, _swap_sibling(partial, f"swap_sibling_{tag}")):
            fit = lambda a: stored(n, a).reshape(mine.shape)
            outs = _adamw(fit(w[n]), [mine, theirs], fit(mom1[n]), fit(mom2[n]), f"adamw_{n}")
            results[n] = [as_given(n, o) for o in outs]

    collect(scatters, gone)
    reduce_and_update([n for n in ("meta_tokens",) + BIG if n not in ffn1_names], "early")
    small_terms = [recv["small", 0][i][None] for i in range(N_DEV)]
    souts = _adamw(pack(w)[None], small_terms, pack(mom1)[None], pack(mom2)[None], "adamw_small")
    off = 0
    for n in SMALL:
        size = w[n].size
        results[n] = [o.reshape(-1)[off:off + size].reshape(w[n].shape) for o in souts]
        off += size
    collect(late, souts[0])
    reduce_and_update(ffn1_names, "late")

    out = [loss, grad_x]
    for part in range(4):
        out += [results[n][part] for n in WEIGHTS]
    return tuple(out)
```

```python
import functools

import jax
import jax.numpy as jnp
from jax import lax
from jax.experimental import pallas as pl
from jax.experimental.pallas import tpu as pltpu

F32 = jnp.float32
BF16 = jnp.bfloat16
MESH = pl.DeviceIdType.MESH

N_META = 16
POOL_WINDOWS = (2, 4, 8, 16)
POOL_HALO = 16
N_HEADS = 4
GLA_GATE_TEMP = 16.0
CHUNK = 128
CHUNK_UNROLL = 5
LN_EPS = 1e-5
RMS_EPS = 1e-6
ADAM_LR = 0.001
ADAM_B1 = 0.9
ADAM_B2 = 0.999
ADAM_EPS = 1e-08
ADAM_WD = 0.01
ADAM_STEP = 10
LANE = 128
BF16_ROWS = 16
ROW_TILE = 640
WGRAD_K_MAX = 2176
N_SHARD = 4
N_DEV = 8

BIG = ("ffn1_w_gate", "ffn1_w_up", "ffn1_w_down", "w_in", "w_gate_up", "w_out",
       "ffn2_w_gate", "ffn2_w_up", "ffn2_w_down")
SMALL = ("ln1_g", "ln1_b", "b_gate", "w_pool", "pool_scale", "gla_norm_g", "ln2_g", "ln2_b", "ln3_g", "ln3_b")
WEIGHTS = ("meta_tokens", "ffn1_w_gate", "ffn1_w_up", "ffn1_w_down", "ln1_g", "ln1_b", "w_in", "w_gate_up",
           "b_gate", "w_pool", "pool_scale", "gla_norm_g", "w_out", "ln2_g", "ln2_b", "ffn2_w_gate",
           "ffn2_w_up", "ffn2_w_down", "ln3_g", "ln3_b")


def _tc_call(body, **kw):
    return pl.pallas_call(body, **kw)


def _comm_call(body, **kw):
    return pl.pallas_call(body, **kw)


def _seq(n):
    return pltpu.CompilerParams(dimension_semantics=("arbitrary",) * n)


def _mm(a, b):
    return jnp.dot(a.astype(BF16), b.astype(BF16), preferred_element_type=F32)


def _mm_nt(a, b):
    return lax.dot_general(a.astype(BF16), b.astype(BF16), (((1,), (1,)), ((), ())), preferred_element_type=F32)


def _mm_tn(a, b):
    return lax.dot_general(a.astype(BF16), b.astype(BF16), (((0,), (0,)), ((), ())), preferred_element_type=F32)


def _mm_f32(a, b):
    return jnp.dot(a, b, precision=lax.Precision.HIGHEST, preferred_element_type=F32)


def _row_tile(t):
    tm = min(ROW_TILE, t)
    while t % tm:
        tm -= LANE
    return tm


def _silu_parts(g):
    sg = jax.nn.sigmoid(g)
    return sg, g * sg


def _ln_stats(r):
    mu = jnp.mean(r, axis=-1, keepdims=True)
    rc = r - mu
    var = jnp.mean(rc * rc, axis=-1, keepdims=True)
    rs = lax.rsqrt(var + LN_EPS)
    return rc * rs, rs


def _ln_bwd(dy, xh, rs, gam):
    dyg = dy * gam
    c1 = jnp.mean(dyg, axis=-1, keepdims=True)
    c2 = jnp.mean(dyg * xh, axis=-1, keepdims=True)
    return rs * (dyg - c1 - xh * c2)


def _ffn_fwd(xin, gam_in, bet_in, wg, wu, wd, alpha, name):
    t, d = xin.shape
    nj, tf, _ = wg.shape
    tm = _row_tile(t)

    def body(x_ref, gi_ref, bi_ref, wg_ref, wu_ref, wd_ref, xhat_ref, rstd_ref, hb_ref, go_ref, uo_ref, acc, hbs):
        j = pl.program_id(1)

        @pl.when(j == 0)
        def _():
            hb = (x_ref[...] * gi_ref[...] + bi_ref[...]).astype(BF16)
            hbs[...] = hb
            hb_ref[...] = hb
            acc[...] = jnp.zeros_like(acc)

        hb = hbs[...]
        g = _mm_nt(hb, wg_ref[...])
        u = _mm_nt(hb, wu_ref[...])
        _, sl = _silu_parts(g)
        go_ref[...] = g.astype(BF16)
        uo_ref[...] = u.astype(BF16)
        acc[...] += jnp.dot((sl * u).astype(BF16), wd_ref[...], preferred_element_type=F32)

        @pl.when(j == nj - 1)
        def _():
            h = x_ref[...] * gi_ref[...] + bi_ref[...]
            xhat, rs = _ln_stats(alpha * h + 0.5 * acc[...])
            xhat_ref[...] = xhat
            rstd_ref[...] = rs

    row = lambda i, j: (i, 0)
    vec = pl.BlockSpec((1, d), lambda i, j: (0, 0))
    return _tc_call(
        body, name=name, grid=(t // tm, nj),
        in_specs=[pl.BlockSpec((tm, d), row), vec, vec] + [pl.BlockSpec((None, tf, d), lambda i, j: (j, 0, 0))] * 3,
        out_specs=[pl.BlockSpec((tm, d), row), pl.BlockSpec((tm, 1), row), pl.BlockSpec((tm, d), row),
                   pl.BlockSpec((None, tm, tf), lambda i, j: (j, i, 0)),
                   pl.BlockSpec((None, tm, tf), lambda i, j: (j, i, 0))],
        out_shape=[jax.ShapeDtypeStruct((t, d), F32), jax.ShapeDtypeStruct((t, 1), F32),
                   jax.ShapeDtypeStruct((t, d), BF16), jax.ShapeDtypeStruct((nj, t, tf), BF16),
                   jax.ShapeDtypeStruct((nj, t, tf), BF16)],
        scratch_shapes=[pltpu.VMEM((tm, d), F32), pltpu.VMEM((tm, d), BF16)],
        compiler_params=_seq(2),
    )(xin, gam_in, bet_in, wg, wu, wd)


def _ffn_bwd(dh, xhat, rstd, ln_g, gb, ub, wg, wu, wd, alpha, name):
    t, d = dh.shape
    nj, tf, _ = wg.shape
    tm = _row_tile(t)

    def body(dh_ref, xh_ref, rs_ref, g_ref, gb_ref, ub_ref, wg_ref, wu_ref, wd_ref,
             dhin_ref, df_ref, dg_ref, du_ref, act_ref, dgam_ref, dbet_ref, dr_s, df_s, acc):
        i = pl.program_id(0)
        j = pl.program_id(1)

        @pl.when(j == 0)
        def _():
            dy = dh_ref[...]
            xh = xh_ref[...]
            dr = _ln_bwd(dy, xh, rs_ref[...], g_ref[...])
            dr_s[...] = dr
            dfb = (0.5 * dr).astype(BF16)
            df_s[...] = dfb
            df_ref[...] = dfb
            acc[...] = jnp.zeros_like(acc)

            @pl.when(i == 0)
            def _():
                dgam_ref[...] = jnp.zeros_like(dgam_ref)
                dbet_ref[...] = jnp.zeros_like(dbet_ref)

            dgam_ref[...] += jnp.sum(dy * xh, axis=0, keepdims=True)
            dbet_ref[...] += jnp.sum(dy, axis=0, keepdims=True)

        dact = _mm_nt(df_s[...], wd_ref[...])
        g = gb_ref[...].astype(F32)
        u = ub_ref[...].astype(F32)
        sg, sl = _silu_parts(g)
        dg = (dact * u * (sg * (1.0 + g * (1.0 - sg)))).astype(BF16)
        du = (dact * sl).astype(BF16)
        dg_ref[...] = dg
        du_ref[...] = du
        act_ref[...] = (sl * u).astype(BF16)
        acc[...] += _mm(dg, wg_ref[...]) + _mm(du, wu_ref[...])

        @pl.when(j == nj - 1)
        def _():
            dhin_ref[...] = alpha * dr_s[...] + acc[...]

    row = lambda i, j: (i, 0)
    col = pl.BlockSpec((None, tm, tf), lambda i, j: (j, i, 0))
    vec = pl.BlockSpec((1, d), lambda i, j: (0, 0))
    ff = jax.ShapeDtypeStruct((nj, t, tf), BF16)
    return _tc_call(
        body, name=name, grid=(t // tm, nj),
        in_specs=[pl.BlockSpec((tm, d), row), pl.BlockSpec((tm, d), row), pl.BlockSpec((tm, 1), row), vec,
                  col, col] + [pl.BlockSpec((None, tf, d), lambda i, j: (j, 0, 0))] * 3,
        out_specs=[pl.BlockSpec((tm, d), row), pl.BlockSpec((tm, d), row),
                   col, col, col, vec, vec],
        out_shape=[jax.ShapeDtypeStruct((t, d), F32), jax.ShapeDtypeStruct((t, d), BF16), ff, ff, ff,
                   jax.ShapeDtypeStruct((1, d), F32), jax.ShapeDtypeStruct((1, d), F32)],
        scratch_shapes=[pltpu.VMEM((tm, d), F32), pltpu.VMEM((tm, d), BF16), pltpu.VMEM((tm, d), F32)],
        compiler_params=_seq(2),
    )(dh, xhat, rstd, ln_g, gb, ub, wg, wu, wd)


def _wgrad(a, b, tmm, tn, name, after=None):
    t = a.shape[-2]
    m = a.shape[-1] * (a.shape[0] if a.ndim == 3 else 1)
    n = b.shape[-1] * (b.shape[0] if b.ndim == 3 else 1)
    tk = max(k for k in range(BF16_ROWS, WGRAD_K_MAX + 1, BF16_ROWS) if t % k == 0)
    nk = t // tk
    extra = [] if after is None else [after]

    def body(a_ref, b_ref, *rest):
        o_ref, acc = rest[len(extra):]
        k = pl.program_id(2)

        @pl.when(k == 0)
        def _():
            acc[...] = jnp.zeros_like(acc)

        acc[...] += _mm_tn(a_ref[...], b_ref[...])

        @pl.when(k == nk - 1)
        def _():
            o_ref[...] = acc[...].astype(o_ref.dtype)

    return _tc_call(
        body, name=name, grid=(m // tmm, n // tn, nk),
        in_specs=[pl.BlockSpec((None, tk, tmm), lambda i, j, k: (i, k, 0)) if a.ndim == 3
                  else pl.BlockSpec((tk, tmm), lambda i, j, k: (k, i)),
                  pl.BlockSpec((None, tk, tn), lambda i, j, k: (j, k, 0)) if b.ndim == 3
                  else pl.BlockSpec((tk, tn), lambda i, j, k: (k, j))] + [pl.BlockSpec(memory_space=pl.ANY)] * len(extra),
        out_specs=pl.BlockSpec((tmm, tn), lambda i, j, k: (i, j)),
        out_shape=jax.ShapeDtypeStruct((m, n), BF16),
        scratch_shapes=[pltpu.VMEM((tmm, tn), F32)],
        compiler_params=_seq(3),
    )(a, b, *extra)


def _inproj_fwd(xhat, gam, bet, w_main, w_lr, wgu, b_gate, widths, name):
    t, d = xhat.shape
    tm = _row_tile(t)
    kw = wgu.shape[1]
    offs = [0]
    for w in widths:
        offs.append(offs[-1] + w)

    def body(x_ref, g_ref, b_ref, wm_ref, wl_ref, wgu_ref, bg_ref, *outs):
        piece_refs, (zg_ref, la_ref, hb_ref) = outs[:len(widths)], outs[len(widths):]
        hb = (x_ref[...] * g_ref[...] + b_ref[...]).astype(BF16)
        hb_ref[...] = hb
        for p, ref in enumerate(piece_refs):
            ref[...] = _mm_nt(hb, wm_ref[offs[p]:offs[p + 1], :])
        zg = _mm_nt(hb, wl_ref[...])
        zg_ref[...] = zg
        logit = _mm(zg, wgu_ref[...]) + bg_ref[...]
        la_ref[...] = (jnp.minimum(logit, 0.0) - jnp.log(1.0 + jnp.exp(-jnp.abs(logit)))) * (1.0 / GLA_GATE_TEMP)

    row = lambda i: (i, 0)
    full = lambda a: pl.BlockSpec(a.shape, lambda i: (0,) * a.ndim)
    out_w = list(widths) + [LANE, kw]
    return _tc_call(
        body, name=name, grid=(t // tm,),
        in_specs=[pl.BlockSpec((tm, d), row), full(gam), full(bet), full(w_main), full(w_lr), full(wgu), full(b_gate)],
        out_specs=[pl.BlockSpec((tm, w), row) for w in out_w] + [pl.BlockSpec((tm, d), row)],
        out_shape=[jax.ShapeDtypeStruct((t, w), F32) for w in out_w] + [jax.ShapeDtypeStruct((t, d), BF16)],
        compiler_params=_seq(1),
    )(xhat, gam, bet, w_main, w_lr, wgu, b_gate)


def _inproj_bwd(dh_part, pieces, dzg, w_main, w_lr, name):
    t, d = dh_part.shape
    tm = _row_tile(t)
    widths = [p.shape[1] for p in pieces]
    offs = [0]
    for w in widths:
        offs.append(offs[-1] + w)

    def body(*refs):
        dhp_ref = refs[0]
        p_refs = refs[1:1 + len(widths)]
        dzg_ref, wm_ref, wl_ref, dh_ref, dz_ref = refs[1 + len(widths):]
        acc = dhp_ref[...] + _mm(dzg_ref[...], wl_ref[...])
        for p, ref in enumerate(p_refs):
            v = ref[...]
            dz_ref[:, offs[p]:offs[p + 1]] = v
            acc += _mm(v, wm_ref[offs[p]:offs[p + 1], :])
        dh_ref[...] = acc

    row = lambda i: (i, 0)
    full = lambda a: pl.BlockSpec(a.shape, lambda i: (0,) * a.ndim)
    return _tc_call(
        body, name=name, grid=(t // tm,),
        in_specs=[pl.BlockSpec((tm, d), row)] + [pl.BlockSpec((tm, w), row) for w in widths]
                 + [pl.BlockSpec((tm, LANE), row), full(w_main), full(w_lr)],
        out_specs=[pl.BlockSpec((tm, d), row), pl.BlockSpec((tm, offs[-1]), row)],
        out_shape=[jax.ShapeDtypeStruct((t, d), F32), jax.ShapeDtypeStruct((t, offs[-1]), BF16)],
        compiler_params=_seq(1),
    )(dh_part, *pieces, dzg, w_main, w_lr)


def _pool_cnt(tile, tm, w):
    t = tile * tm + lax.broadcasted_iota(jnp.int32, (tm, 1), 0)
    return jnp.minimum(t + 1, w).astype(F32)


def _pool_fwd(u, wp, scale, name):
    t, pw = u.shape
    tm = _row_tile(t)
    gd = wp.shape[1]

    def body(u_ref, wp_ref, sc_ref, y_ref, p_ref, ext):
        i = pl.program_id(0)

        @pl.when(i == 0)
        def _():
            ext[0:POOL_HALO, :] = jnp.zeros((POOL_HALO, pw), F32)

        ext[POOL_HALO:POOL_HALO + tm, :] = u_ref[...]
        for gi, w in enumerate(POOL_WINDOWS):
            cols = slice(gi * gd, (gi + 1) * gd)
            s = ext[pl.ds(POOL_HALO, tm), cols]
            tot = s
            for back in range(1, w):
                tot = tot + ext[pl.ds(POOL_HALO - back, tm), cols]
            p = (tot / _pool_cnt(i, tm, w) - s).astype(BF16)
            p_ref[:, cols] = p
            y_ref[:, cols] = (jnp.dot(p, wp_ref[gi], preferred_element_type=F32) * sc_ref[:, cols]).astype(BF16)
        ext[0:POOL_HALO, :] = ext[tm:tm + POOL_HALO, :]

    row = lambda i: (i, 0)
    return _tc_call(
        body, name=name, grid=(t // tm,),
        in_specs=[pl.BlockSpec((tm, pw), row), pl.BlockSpec(wp.shape, lambda i: (0, 0, 0)),
                  pl.BlockSpec((1, pw), lambda i: (0, 0))],
        out_specs=[pl.BlockSpec((tm, pw), row), pl.BlockSpec((tm, pw), row)],
        out_shape=[jax.ShapeDtypeStruct((t, pw), BF16), jax.ShapeDtypeStruct((t, pw), BF16)],
        scratch_shapes=[pltpu.VMEM((tm + POOL_HALO, pw), F32)],
        compiler_params=_seq(1),
    )(u, wp, scale)


def _pool_bwd(dy, pb, wp, scale, name):
    t, pw = dy.shape
    tm = _row_tile(t)
    nt = t // tm
    gd = wp.shape[1]

    def body(dy_ref, p_ref, wp_ref, sc_ref, du_ref, dwp_ref, dsc_ref, ext):
        i = pl.program_id(0)
        tile = nt - 1 - i

        @pl.when(i == 0)
        def _():
            ext[tm:tm + POOL_HALO, :] = jnp.zeros((POOL_HALO, pw), F32)
            dwp_ref[...] = jnp.zeros_like(dwp_ref)
            dsc_ref[...] = jnp.zeros_like(dsc_ref)

        dps = []
        for gi, w in enumerate(POOL_WINDOWS):
            cols = slice(gi * gd, (gi + 1) * gd)
            dyv = dy_ref[:, cols]
            p = p_ref[:, cols]
            dpre = (dyv * sc_ref[:, cols]).astype(BF16)
            dsc_ref[:, cols] += jnp.sum(dyv * jnp.dot(p, wp_ref[gi], preferred_element_type=F32), axis=0, keepdims=True)
            dwp_ref[gi] += _mm_tn(p, dpre)
            dp = _mm_nt(dpre, wp_ref[gi])
            dps.append(dp)
            ext[0:tm, cols] = dp / _pool_cnt(tile, tm, w)
        for gi, w in enumerate(POOL_WINDOWS):
            cols = slice(gi * gd, (gi + 1) * gd)
            tot = ext[pl.ds(0, tm), cols]
            for fwd in range(1, w):
                tot = tot + ext[pl.ds(fwd, tm), cols]
            du_ref[:, cols] = (tot - dps[gi]).astype(BF16)
        ext[tm:tm + POOL_HALO, :] = ext[0:POOL_HALO, :]

    row = lambda i: (nt - 1 - i, 0)
    return _tc_call(
        body, name=name, grid=(nt,),
        in_specs=[pl.BlockSpec((tm, pw), row), pl.BlockSpec((tm, pw), row),
                  pl.BlockSpec(wp.shape, lambda i: (0, 0, 0)), pl.BlockSpec((1, pw), lambda i: (0, 0))],
        out_specs=[pl.BlockSpec((tm, pw), row), pl.BlockSpec(wp.shape, lambda i: (0, 0, 0)),
                   pl.BlockSpec((1, pw), lambda i: (0, 0))],
        out_shape=[jax.ShapeDtypeStruct((t, pw), BF16), jax.ShapeDtypeStruct(wp.shape, F32),
                   jax.ShapeDtypeStruct((1, pw), F32)],
        scratch_shapes=[pltpu.VMEM((tm + POOL_HALO, pw), F32)],
        compiler_params=_seq(1),
    )(dy, pb, wp, scale)


def _gla_masks(kw, vw):
    dk, dv = kw // N_HEADS, vw // N_HEADS
    lane_k = lax.broadcasted_iota(jnp.int32, (1, kw), 1)
    lane_v = lax.broadcasted_iota(jnp.int32, (1, vw), 1)
    hk = [((lane_k >= h * dk) & (lane_k < (h + 1) * dk)).astype(F32) for h in range(N_HEADS)]
    hv = [((lane_v >= h * dv) & (lane_v < (h + 1) * dv)).astype(F32) for h in range(N_HEADS)]
    r = lax.broadcasted_iota(jnp.int32, (CHUNK, CHUNK), 0)
    c = lax.broadcasted_iota(jnp.int32, (CHUNK, CHUNK), 1)
    tril = r >= c
    rs = lax.broadcasted_iota(jnp.int32, (N_HEADS * CHUNK, CHUNK), 0) & (CHUNK - 1)
    stril = rs >= lax.broadcasted_iota(jnp.int32, (N_HEADS * CHUNK, CHUNK), 1)
    return hk, hv, tril, stril


def _block_diag(x, hk, dv):
    return jnp.concatenate([x[h * dv:(h + 1) * dv, :] * hk[h] for h in range(N_HEADS)], axis=0)


def _gla_fwd(q, k, v, loga, r, gnorm, name):
    t, kw = q.shape
    vw = v.shape[1]
    dk, dv = kw // N_HEADS, vw // N_HEADS
    tm = _row_tile(t)
    nc = tm // CHUNK
    qscale = dk ** -0.5

    def body(q_ref, k_ref, v_ref, la_ref, r_ref, gn_ref, o_ref, y_ref, sall_ref, st):
        @pl.when(pl.program_id(0) == 0)
        def _():
            st[...] = jnp.zeros_like(st)

        hk, hv, tril, stril = _gla_masks(kw, vw)
        trif = tril.astype(F32)

        def chunk(c, carry):
            rows = pl.ds(pl.multiple_of(c * CHUNK, CHUNK), CHUNK)
            la = la_ref[rows, :]
            b = _mm_f32(trif, la)
            bl = jnp.sum(la, axis=0, keepdims=True)
            qb = q_ref[rows, :] * (qscale * jnp.exp(b))
            kk = k_ref[rows, :]
            kb = kk * jnp.exp(-b)
            kl = kk * jnp.exp(bl - b)
            vv = v_ref[rows, :]
            s_t = st[...]
            compact = s_t[0:dv, :]
            for h in range(1, N_HEADS):
                compact = compact + s_t[h * dv:(h + 1) * dv, :]
            sall_ref[c] = compact
            qx = jnp.concatenate([qb * hk[h] for h in range(N_HEADS)], axis=0)
            a = jnp.where(stril, _mm_nt(qx, kb), 0.0).astype(BF16)
            o_inter = _mm_nt(qb, s_t)
            for h in range(N_HEADS):
                vs = slice(h * dv, (h + 1) * dv)
                o_ref[rows, vs] = o_inter[:, vs] + _mm(a[h * CHUNK:(h + 1) * CHUNK, :], vv[:, vs])
            st[...] = s_t * jnp.exp(bl) + _block_diag(_mm_tn(vv, kl), hk, dv)
            return carry

        lax.fori_loop(0, nc, chunk, 0, unroll=CHUNK_UNROLL)
        for h in range(N_HEADS):
            vs = slice(h * dv, (h + 1) * dv)
            oh = o_ref[:, vs]
            on = oh * lax.rsqrt(jnp.mean(oh * oh, axis=-1, keepdims=True) + RMS_EPS)
            _, sl = _silu_parts(r_ref[:, vs])
            y_ref[:, vs] = (on * gn_ref[:, vs] * sl).astype(BF16)

    row = lambda i: (i, 0)
    return _tc_call(
        body, name=name, grid=(t // tm,),
        in_specs=[pl.BlockSpec((tm, kw), row), pl.BlockSpec((tm, kw), row), pl.BlockSpec((tm, vw), row),
                  pl.BlockSpec((tm, kw), row), pl.BlockSpec((tm, vw), row), pl.BlockSpec((1, vw), lambda i: (0, 0))],
        out_specs=[pl.BlockSpec((tm, vw), row), pl.BlockSpec((tm, vw), row),
                   pl.BlockSpec((nc, dv, kw), lambda i: (i, 0, 0))],
        out_shape=[jax.ShapeDtypeStruct((t, vw), F32), jax.ShapeDtypeStruct((t, vw), BF16),
                   jax.ShapeDtypeStruct((t // CHUNK, dv, kw), F32)],
        scratch_shapes=[pltpu.VMEM((vw, kw), F32)],
        compiler_params=_seq(1),
    )(q, k, v, loga, r, gnorm)


def _gla_bwd(dy, o, r, gnorm, q, k, v, loga, zg, sall, wgu, name):
    t, kw = q.shape
    vw = v.shape[1]
    dk, dv = kw // N_HEADS, vw // N_HEADS
    tm = _row_tile(t)
    nt = t // tm
    nc = tm // CHUNK
    qscale = dk ** -0.5

    def body(dy_ref, o_ref, r_ref, gn_ref, q_ref, k_ref, v_ref, la_ref, zg_ref, sall_ref, wgu_ref,
             dq_ref, dk_ref, dv_ref, dr_ref, dzg_ref, dwgu_ref, dbg_ref, dgn_ref, dst, do_s):
        @pl.when(pl.program_id(0) == 0)
        def _():
            dst[...] = jnp.zeros_like(dst)
            dwgu_ref[...] = jnp.zeros_like(dwgu_ref)
            dbg_ref[...] = jnp.zeros_like(dbg_ref)
            dgn_ref[...] = jnp.zeros_like(dgn_ref)

        for h in range(N_HEADS):
            vs = slice(h * dv, (h + 1) * dv)
            oh = o_ref[:, vs]
            rinv = lax.rsqrt(jnp.mean(oh * oh, axis=-1, keepdims=True) + RMS_EPS)
            on = oh * rinv
            rr = r_ref[:, vs]
            sg, sl = _silu_parts(rr)
            dyv = dy_ref[:, vs]
            gn = gn_ref[:, vs]
            dgn_ref[:, vs] += jnp.sum(dyv * on * sl, axis=0, keepdims=True)
            dr_ref[:, vs] = (dyv * on * gn * (sg * (1.0 + rr * (1.0 - sg)))).astype(BF16)
            don = dyv * gn * sl
            do_s[:, vs] = rinv * (don - on * jnp.mean(don * on, axis=-1, keepdims=True))

        hk, hv, tril, stril = _gla_masks(kw, vw)
        trif = tril.astype(F32)
        triuf = (lax.broadcasted_iota(jnp.int32, (CHUNK, CHUNK), 0)
                 <= lax.broadcasted_iota(jnp.int32, (CHUNK, CHUNK), 1)).astype(F32)
        last_row = lax.broadcasted_iota(jnp.int32, (CHUNK, 1), 0) == CHUNK - 1

        def chunk(idx, carry):
            c = nc - 1 - idx
            rows = pl.ds(pl.multiple_of(c * CHUNK, CHUNK), CHUNK)
            la = la_ref[rows, :]
            b = _mm_f32(trif, la)
            bl = jnp.sum(la, axis=0, keepdims=True)
            eb = jnp.exp(b)
            enb = jnp.exp(-b)
            ebl = jnp.exp(bl - b)
            el = jnp.exp(bl)
            qb = q_ref[rows, :] * (qscale * eb)
            kk = k_ref[rows, :]
            kb = kk * enb
            kl = kk * ebl
            vv = v_ref[rows, :]
            do = do_s[rows, :]
            compact = sall_ref[c]
            s_t = jnp.concatenate([compact * hk[h] for h in range(N_HEADS)], axis=0)
            ds_t = dst[...]
            qx = jnp.concatenate([qb * hk[h] for h in range(N_HEADS)], axis=0)
            dox = jnp.concatenate([do * hv[h] for h in range(N_HEADS)], axis=0)
            a = jnp.where(stril, _mm_nt(qx, kb), 0.0).astype(BF16)
            da = jnp.where(stril, _mm_nt(dox, vv), 0.0).astype(BF16)
            dv_ref[rows, :] = (_mm_tn(a, dox) + _mm_nt(kl, ds_t)).astype(BF16)
            dak = _mm(da, kb)
            dqb = _mm(do, s_t)
            for h in range(N_HEADS):
                dqb = dqb + dak[h * CHUNK:(h + 1) * CHUNK, :] * hk[h]
            dkb = _mm_tn(da, qx)
            dkl = _mm(vv, ds_t)
            dbl = jnp.sum(dkl * kl, axis=0, keepdims=True) + el * jnp.sum(ds_t * s_t, axis=0, keepdims=True)
            dst[...] = ds_t * el + _block_diag(_mm_tn(do, qb), hk, dv)
            dq_ref[rows, :] = (dqb * (qscale * eb)).astype(BF16)
            dk_ref[rows, :] = (dkb * enb + dkl * ebl).astype(BF16)
            db = dqb * qb - dkb * kb - dkl * kl + jnp.where(last_row, dbl, 0.0)
            dla = _mm_f32(triuf, db)
            dlogit = dla * (1.0 / GLA_GATE_TEMP) * (1.0 - jnp.exp(GLA_GATE_TEMP * la))
            dzg_ref[rows, :] = _mm_nt(dlogit, wgu_ref[...]).astype(BF16)
            dwgu_ref[...] += _mm_tn(zg_ref[rows, :], dlogit)
            dbg_ref[...] += jnp.sum(dlogit, axis=0, keepdims=True)
            return carry

        lax.fori_loop(0, nc, chunk, 0, unroll=CHUNK_UNROLL)

    row = lambda i: (nt - 1 - i, 0)
    const = lambda i: (0, 0)
    return _tc_call(
        body, name=name, grid=(nt,),
        in_specs=[pl.BlockSpec((tm, vw), row), pl.BlockSpec((tm, vw), row), pl.BlockSpec((tm, vw), row),
                  pl.BlockSpec((1, vw), const), pl.BlockSpec((tm, kw), row), pl.BlockSpec((tm, kw), row),
                  pl.BlockSpec((tm, vw), row), pl.BlockSpec((tm, kw), row), pl.BlockSpec((tm, LANE), row),
                  pl.BlockSpec((nc, dv, kw), lambda i: (nt - 1 - i, 0, 0)), pl.BlockSpec((LANE, kw), const)],
        out_specs=[pl.BlockSpec((tm, kw), row), pl.BlockSpec((tm, kw), row), pl.BlockSpec((tm, vw), row),
                   pl.BlockSpec((tm, vw), row), pl.BlockSpec((tm, LANE), row), pl.BlockSpec((LANE, kw), const),
                   pl.BlockSpec((1, kw), const), pl.BlockSpec((1, vw), const)],
        out_shape=[jax.ShapeDtypeStruct((t, kw), BF16), jax.ShapeDtypeStruct((t, kw), BF16),
                   jax.ShapeDtypeStruct((t, vw), BF16), jax.ShapeDtypeStruct((t, vw), BF16),
                   jax.ShapeDtypeStruct((t, LANE), BF16), jax.ShapeDtypeStruct((LANE, kw), F32),
                   jax.ShapeDtypeStruct((1, kw), F32), jax.ShapeDtypeStruct((1, vw), F32)],
        scratch_shapes=[pltpu.VMEM((vw, kw), F32), pltpu.VMEM((tm, vw), F32)],
        compiler_params=_seq(1),
    )(dy, o, r, gnorm, q, k, v, loga, zg, sall, wgu)


def _outproj_fwd(yp, yg, w_out, xhat, gam, bet, alpha, name):
    t, d = xhat.shape
    pw = yp.shape[1]
    tm = _row_tile(t)

    def body(yp_ref, yg_ref, w_ref, x_ref, g_ref, b_ref, xhat_ref, rstd_ref):
        h = x_ref[...] * g_ref[...] + b_ref[...]
        y = (jnp.dot(yp_ref[...], w_ref[0:pw, :], preferred_element_type=F32)
             + jnp.dot(yg_ref[...], w_ref[pw:, :], preferred_element_type=F32))
        xh, rs = _ln_stats(alpha * h + y)
        xhat_ref[...] = xh
        rstd_ref[...] = rs

    row = lambda i: (i, 0)
    vec = pl.BlockSpec((1, d), lambda i: (0, 0))
    return _tc_call(
        body, name=name, grid=(t // tm,),
        in_specs=[pl.BlockSpec((tm, pw), row), pl.BlockSpec((tm, yg.shape[1]), row),
                  pl.BlockSpec(w_out.shape, lambda i: (0, 0)), pl.BlockSpec((tm, d), row), vec, vec],
        out_specs=[pl.BlockSpec((tm, d), row), pl.BlockSpec((tm, 1), row)],
        out_shape=[jax.ShapeDtypeStruct((t, d), F32), jax.ShapeDtypeStruct((t, 1), F32)],
        compiler_params=_seq(1),
    )(yp, yg, w_out, xhat, gam, bet)


def _outproj_bwd(dh, xhat, rstd, ln_g, w_out, pw, alpha, name):
    t, d = dh.shape
    tm = _row_tile(t)
    gw = w_out.shape[0] - pw

    def body(dh_ref, xh_ref, rs_ref, g_ref, w_ref, dyb_ref, dyp_ref, dyg_ref, dres_ref, dgam_ref, dbet_ref):
        @pl.when(pl.program_id(0) == 0)
        def _():
            dgam_ref[...] = jnp.zeros_like(dgam_ref)
            dbet_ref[...] = jnp.zeros_like(dbet_ref)

        dy = dh_ref[...]
        xh = xh_ref[...]
        dr = _ln_bwd(dy, xh, rs_ref[...], g_ref[...])
        dgam_ref[...] += jnp.sum(dy * xh, axis=0, keepdims=True)
        dbet_ref[...] += jnp.sum(dy, axis=0, keepdims=True)
        drb = dr.astype(BF16)
        dyb_ref[...] = drb
        dres_ref[...] = alpha * dr
        dyp_ref[...] = _mm_nt(drb, w_ref[0:pw, :])
        dyg_ref[...] = _mm_nt(drb, w_ref[pw:, :])

    row = lambda i: (i, 0)
    vec = pl.BlockSpec((1, d), lambda i: (0, 0))
    return _tc_call(
        body, name=name, grid=(t // tm,),
        in_specs=[pl.BlockSpec((tm, d), row), pl.BlockSpec((tm, d), row), pl.BlockSpec((tm, 1), row), vec,
                  pl.BlockSpec(w_out.shape, lambda i: (0, 0))],
        out_specs=[pl.BlockSpec((tm, d), row), pl.BlockSpec((tm, pw), row), pl.BlockSpec((tm, gw), row),
                   pl.BlockSpec((tm, d), row), vec, vec],
        out_shape=[jax.ShapeDtypeStruct((t, d), BF16), jax.ShapeDtypeStruct((t, pw), F32),
                   jax.ShapeDtypeStruct((t, gw), F32), jax.ShapeDtypeStruct((t, d), F32),
                   jax.ShapeDtypeStruct((1, d), F32), jax.ShapeDtypeStruct((1, d), F32)],
        compiler_params=_seq(1),
    )(dh, xhat, rstd, ln_g, w_out)


def _loss_head(xhat, gam, bet, target, n_rows, name):
    t, d = xhat.shape
    tm = _row_tile(t)

    def body(x_ref, g_ref, b_ref, t_ref, dy_ref, loss_ref):
        i = pl.program_id(0)

        @pl.when(i == 0)
        def _():
            loss_ref[...] = jnp.zeros_like(loss_ref)

        rowi = i * tm + lax.broadcasted_iota(jnp.int32, (tm, 1), 0)
        live = (rowi >= N_META) & (rowi < N_META + n_rows)
        diff = jnp.where(live, x_ref[...] * g_ref[...] + b_ref[...] - t_ref[...], 0.0)
        dy_ref[...] = diff * (1.0 / d)
        loss_ref[...] += jnp.sum(diff * diff) * (0.5 / d)

    row = lambda i: (i, 0)
    vec = pl.BlockSpec((1, d), lambda i: (0, 0))
    return _tc_call(
        body, name=name, grid=(t // tm,),
        in_specs=[pl.BlockSpec((tm, d), row), vec, vec, pl.BlockSpec((tm, d), row)],
        out_specs=[pl.BlockSpec((tm, d), row), pl.BlockSpec((8, LANE), lambda i: (0, 0))],
        out_shape=[jax.ShapeDtypeStruct((t, d), F32), jax.ShapeDtypeStruct((8, LANE), F32)],
        compiler_params=_seq(1),
    )(xhat, gam, bet, target)


def _rows_block(r, c):
    best = r
    for cand in range(8, r, 8):
        if r % cand == 0 and cand * c * 4 <= (1 << 20):
            best = cand
    return best if best * c * 4 <= (4 << 20) else r


def _sum_slots(recvs, name, layers_side_by_side=False):
    nl = len(recvs)
    ns, r, c = recvs[0].shape
    tr = _rows_block(r, c)

    def body(*refs):
        o_ref = refs[nl]
        for l in range(nl):
            acc = refs[l][0].astype(F32)
            for s in range(1, ns):
                acc = acc + refs[l][s].astype(F32)
            if layers_side_by_side:
                o_ref[0, :, l * c:(l + 1) * c] = acc
            else:
                o_ref[l] = acc

    out = (1, r, nl * c) if layers_side_by_side else (nl, r, c)
    return _tc_call(
        body, name=name, grid=(r // tr,),
        in_specs=[pl.BlockSpec((ns, tr, c), lambda i: (0, i, 0))] * nl,
        out_specs=pl.BlockSpec((out[0], tr, out[2]), lambda i: (0, i, 0)),
        out_shape=jax.ShapeDtypeStruct(out, F32),
        compiler_params=_seq(1),
    )(*recvs)


def _adamw(w, terms, m, v, name):
    nl, r, c = w.shape
    tc = c
    while tc % (2 * LANE) == 0 and tc > 4 * LANE:
        tc //= 2
    tr = _rows_block(r, tc)
    nterm = len(terms)

    def body(*refs):
        w_ref = refs[0]
        t_refs = refs[1:1 + nterm]
        m_ref, v_ref, g_ref, d_ref, nm_ref, nv_ref = refs[1 + nterm:]
        g = t_refs[0][...]
        for tr_ in t_refs[1:]:
            g = g + tr_[...]
        nm = ADAM_B1 * m_ref[...] + (1.0 - ADAM_B1) * g
        nv = ADAM_B2 * v_ref[...] + (1.0 - ADAM_B2) * jnp.square(g)
        m_hat = nm / (1.0 - ADAM_B1 ** ADAM_STEP)
        v_hat = nv / (1.0 - ADAM_B2 ** ADAM_STEP)
        g_ref[...] = g
        d_ref[...] = -ADAM_LR * (m_hat / (jnp.sqrt(v_hat) + ADAM_EPS) + ADAM_WD * w_ref[...])
        nm_ref[...] = nm
        nv_ref[...] = nv

    spec = pl.BlockSpec((None, tr, tc), lambda l, i, j: (l, i, j))
    shp = jax.ShapeDtypeStruct((nl, r, c), F32)
    return _tc_call(
        body, name=name, grid=(nl, r // tr, c // tc),
        in_specs=[spec] * (3 + nterm), out_specs=[spec] * 4, out_shape=[shp] * 4,
        compiler_params=_seq(3),
    )(w, *terms, m, v)


XY_RELATIONS = ((1, 0, 0), (0, 1, 0), (1, 1, 0))
ALL_RELATIONS = tuple((fx, fy, fc) for fx in (0, 1) for fy in (0, 1) for fc in (0, 1) if fx or fy or fc)
HBM_SPEC = pl.BlockSpec(memory_space=pltpu.HBM)
SEM_SPEC = pl.BlockSpec(memory_space=pltpu.SEMAPHORE)
DATAFLOW = pltpu.SideEffectType.DATAFLOW_SIDE_EFFECTING


def _split_call(body, **kw):
    return pl.pallas_call(body, **kw)


def _flip(v, f):
    return 1 - v if f else v


def _any_spec(n):
    return [pl.BlockSpec(memory_space=pl.ANY)] * n


def _relations(kind):
    return ALL_RELATIONS if kind == "bcast" else XY_RELATIONS


def _copies(kind, arr, land, sems):
    x, y, c = lax.axis_index("x"), lax.axis_index("y"), lax.axis_index("c")
    out = []
    for (fx, fy, fc), (send_sem, recv_sem) in zip(_relations(kind), sems):
        px, py, pc = _flip(x, fx), _flip(y, fy), _flip(c, fc)
        if kind == "bcast":
            mine, theirs = 4 * x + 2 * y + c, 4 * px + 2 * py + pc
        else:
            mine, theirs = 2 * x + y, 2 * px + py
        src, to_mine, to_theirs = arr, land.at[mine], land.at[theirs]
        if kind == "scatter":
            src = arr.at[theirs]
        if kind == "gather_half":
            rows = _my_half(arr.shape[0], c)
            src, to_mine, to_theirs = arr.at[rows], land.at[mine, rows], land.at[theirs, rows]
        both = dict(src_ref=src, send_sem=send_sem, recv_sem=recv_sem, device_id=(px, py, pc), device_id_type=MESH)
        out.append((pltpu.make_async_remote_copy(dst_ref=to_mine, **both),
                    pltpu.make_async_remote_copy(dst_ref=to_theirs, **both)))
    return out


def _my_half(nrows, c):
    return pl.ds(c * (nrows // 2), nrows // 2)


def _share_halves(name, kinds, lands):
    ks = [k for k, kd in enumerate(kinds) if kd == "gather_half"]
    n = len(ks)

    def body(*refs):
        l_refs = refs[n:2 * n]
        send_sems, recv_sems = refs[2 * n:]
        x, y, c = lax.axis_index("x"), lax.axis_index("y"), lax.axis_index("c")
        copies = []
        for i in range(n):
            nrows = l_refs[i].shape[1]
            for r, (fx, fy, _) in enumerate(XY_RELATIONS):
                slot = 2 * _flip(x, fx) + _flip(y, fy)
                both = dict(src_ref=l_refs[i].at[slot, _my_half(nrows, c)], send_sem=send_sems.at[i, r],
                            recv_sem=recv_sems.at[i, r], device_id=(x, y, 1 - c), device_id_type=MESH)
                copies.append((pltpu.make_async_remote_copy(dst_ref=l_refs[i].at[slot, _my_half(nrows, c)], **both),
                               pltpu.make_async_remote_copy(dst_ref=l_refs[i].at[slot, _my_half(nrows, 1 - c)], **both)))
        for send, _ in copies:
            send.start()
        for _, arrival in copies:
            arrival.wait_recv()
        for send, _ in copies:
            send.wait_send()

    outs = _comm_call(
        body, name=name,
        in_specs=_any_spec(n), out_specs=_any_spec(n),
        out_shape=[jax.ShapeDtypeStruct(lands[k].shape, lands[k].dtype) for k in ks],
        input_output_aliases={i: i for i in range(n)},
        scratch_shapes=[pltpu.SemaphoreType.DMA((n, 3)), pltpu.SemaphoreType.DMA((n, 3))],
    )(*[lands[k] for k in ks])
    lands = list(lands)
    for k, o in zip(ks, outs):
        lands[k] = o
    return lands


def _sem_pairs(kinds, sems):
    out, at = [], 0
    for kind in kinds:
        nrel = len(_relations(kind))
        out.append([(sems[at + 2 * r], sems[at + 2 * r + 1]) for r in range(nrel)])
        at += 2 * nrel
    return out


def _exchange_start(name, kinds, arrs, lands):
    n = len(arrs)
    nsem = sum(2 * len(_relations(kd)) for kd in kinds)

    def body(*refs):
        a_refs, l_refs = refs[:n], refs[n:2 * n]
        pairs = _sem_pairs(kinds, refs[2 * n:2 * n + nsem])
        token = refs[-1]
        for k in range(n):
            for send, _ in _copies(kinds[k], a_refs[k], l_refs[k], pairs[k]):
                send.start()
        token[...] = jnp.zeros_like(token)

    thru = [pltpu.HBM(a.shape, a.dtype) for a in list(arrs) + list(lands)]
    outs = _split_call(
        body, name=name,
        out_shape=(*[pltpu.SemaphoreType.DMA(())] * nsem, *thru, jax.ShapeDtypeStruct((8, LANE), F32)),
        in_specs=[HBM_SPEC] * (2 * n),
        out_specs=(*[SEM_SPEC] * nsem, *[HBM_SPEC] * (2 * n), pl.BlockSpec(memory_space=pltpu.VMEM)),
        input_output_aliases={i: nsem + i for i in range(2 * n)},
        compiler_params=pltpu.CompilerParams(has_side_effects=DATAFLOW),
    )(*[pltpu.with_memory_space_constraint(a, pltpu.HBM) for a in list(arrs) + list(lands)])
    return dict(kinds=kinds, sems=outs[:nsem], arrs=outs[nsem:nsem + n], lands=outs[nsem + n:nsem + 2 * n],
                token=outs[-1])


def _exchange_wait(name, st, after):
    kinds = st["kinds"]
    n = len(kinds)
    nsem = len(st["sems"])

    def body(*refs):
        a_refs, l_refs = refs[:n], refs[n:2 * n]
        pairs = _sem_pairs(kinds, refs[2 * n:2 * n + nsem])
        for k in range(n):
            for _, arrival in _copies(kinds[k], a_refs[k], l_refs[k], pairs[k]):
                arrival.wait_send()
                arrival.wait_recv()
        refs[-1][...] = jnp.zeros_like(refs[-1])

    ins = list(st["arrs"]) + list(st["lands"])
    outs = _split_call(
        body, name=name,
        out_shape=[pltpu.HBM(a.shape, a.dtype) for a in ins] + [jax.ShapeDtypeStruct((8, LANE), F32)],
        in_specs=[HBM_SPEC] * (2 * n) + [SEM_SPEC] * nsem + [pl.BlockSpec(memory_space=pl.ANY)],
        out_specs=[HBM_SPEC] * (2 * n) + [pl.BlockSpec(memory_space=pltpu.VMEM)],
        input_output_aliases={i: i for i in range(2 * n)},
        compiler_params=pltpu.CompilerParams(has_side_effects=DATAFLOW),
    )(*ins, *st["sems"], after)
    return outs[n:2 * n], outs[-1]


def _landing(own, slot, nslot):
    return lax.dynamic_update_slice(lax.empty((nslot,) + own.shape, own.dtype), own[None], (slot,) + (0,) * own.ndim)


def _swap_sibling(parts, name):
    n = len(parts)

    def body(*refs):
        ins, outs = refs[:n], refs[n:2 * n]
        send_sems, recv_sems = refs[2 * n:]
        sib = (lax.axis_index("x"), lax.axis_index("y"), 1 - lax.axis_index("c"))
        cps = [pltpu.make_async_remote_copy(src_ref=ins[k], dst_ref=outs[k], send_sem=send_sems.at[k],
                                            recv_sem=recv_sems.at[k], device_id=sib, device_id_type=MESH)
               for k in range(n)]
        for cp in cps:
            cp.start()
        for cp in cps:
            cp.wait_recv()
        for cp in cps:
            cp.wait_send()

    return _comm_call(
        body, name=name,
        in_specs=_any_spec(n), out_specs=_any_spec(n),
        out_shape=[jax.ShapeDtypeStruct(p.shape, p.dtype) for p in parts],
        scratch_shapes=[pltpu.SemaphoreType.DMA((n,)), pltpu.SemaphoreType.DMA((n,))],
    )(*parts)


def _col_shards(a, n=N_SHARD):
    r, c = a.shape
    return a.reshape(r, n, c // n).transpose(1, 0, 2)


def _from_col_shards(a):
    n, r, cs = a.shape
    return a.transpose(1, 0, 2).reshape(r, n * cs)


def kernel(x, meta_tokens, ffn1_w_gate, ffn1_w_up, ffn1_w_down, ln1_g, ln1_b, w_in, w_gate_up, b_gate, w_pool, pool_scale, gla_norm_g, w_out, ln2_g, ln2_b, ffn2_w_gate, ffn2_w_up, ffn2_w_down, ln3_g, ln3_b, loss_target, m_meta_tokens, m_ffn1_w_gate, m_ffn1_w_up, m_ffn1_w_down, m_ln1_g, m_ln1_b, m_w_in, m_w_gate_up, m_b_gate, m_w_pool, m_pool_scale, m_gla_norm_g, m_w_out, m_ln2_g, m_ln2_b, m_ffn2_w_gate, m_ffn2_w_up, m_ffn2_w_down, m_ln3_g, m_ln3_b, v_meta_tokens, v_ffn1_w_gate, v_ffn1_w_up, v_ffn1_w_down, v_ln1_g, v_ln1_b, v_w_in, v_w_gate_up, v_b_gate, v_w_pool, v_pool_scale, v_gla_norm_g, v_w_out, v_ln2_g, v_ln2_b, v_ffn2_w_gate, v_ffn2_w_up, v_ffn2_w_down, v_ln3_g, v_ln3_b):
    w = dict(meta_tokens=meta_tokens, ffn1_w_gate=ffn1_w_gate, ffn1_w_up=ffn1_w_up, ffn1_w_down=ffn1_w_down,
             ln1_g=ln1_g, ln1_b=ln1_b, w_in=w_in, w_gate_up=w_gate_up, b_gate=b_gate, w_pool=w_pool,
             pool_scale=pool_scale, gla_norm_g=gla_norm_g, w_out=w_out, ln2_g=ln2_g, ln2_b=ln2_b,
             ffn2_w_gate=ffn2_w_gate, ffn2_w_up=ffn2_w_up, ffn2_w_down=ffn2_w_down, ln3_g=ln3_g, ln3_b=ln3_b)
    mom1 = dict(meta_tokens=m_meta_tokens, ffn1_w_gate=m_ffn1_w_gate, ffn1_w_up=m_ffn1_w_up,
                ffn1_w_down=m_ffn1_w_down, ln1_g=m_ln1_g, ln1_b=m_ln1_b, w_in=m_w_in, w_gate_up=m_w_gate_up,
                b_gate=m_b_gate, w_pool=m_w_pool, pool_scale=m_pool_scale, gla_norm_g=m_gla_norm_g, w_out=m_w_out,
                ln2_g=m_ln2_g, ln2_b=m_ln2_b, ffn2_w_gate=m_ffn2_w_gate, ffn2_w_up=m_ffn2_w_up,
                ffn2_w_down=m_ffn2_w_down, ln3_g=m_ln3_g, ln3_b=m_ln3_b)
    mom2 = dict(meta_tokens=v_meta_tokens, ffn1_w_gate=v_ffn1_w_gate, ffn1_w_up=v_ffn1_w_up,
                ffn1_w_down=v_ffn1_w_down, ln1_g=v_ln1_g, ln1_b=v_ln1_b, w_in=v_w_in, w_gate_up=v_w_gate_up,
                b_gate=v_b_gate, w_pool=v_w_pool, pool_scale=v_pool_scale, gla_norm_g=v_gla_norm_g, w_out=v_w_out,
                ln2_g=v_ln2_g, ln2_b=v_ln2_b, ffn2_w_gate=v_ffn2_w_gate, ffn2_w_up=v_ffn2_w_up,
                ffn2_w_down=v_ffn2_w_down, ln3_g=v_ln3_g, ln3_b=v_ln3_b)

    xs = x[0]
    s_len, d = xs.shape
    nl = ln1_g.shape[0]
    alpha = (2.0 * nl) ** 0.25
    t_real = N_META + s_len
    t_pad = -(-t_real // LANE) * LANE
    pw = pool_scale.shape[1]
    kw = b_gate.shape[1]
    vw = gla_norm_g.shape[1]
    rank = w_gate_up.shape[1]
    widths = (pw, kw, kw, vw, vw)
    n_main = sum(widths)
    dff_s = ffn1_w_gate.shape[2]

    me_xy = 2 * lax.axis_index("x") + lax.axis_index("y")
    me_all = 2 * me_xy + lax.axis_index("c")
    ffn1_names = ("ffn1_w_gate", "ffn1_w_up", "ffn1_w_down")
    mix_names = ("w_out", "w_gate_up", "w_in")
    ffn2_names = ("ffn2_w_gate", "ffn2_w_up", "ffn2_w_down")

    gate_up = ("ffn1_w_gate", "ffn1_w_up", "ffn2_w_gate", "ffn2_w_up")

    def stored(n, a):
        if n in gate_up:
            return jnp.swapaxes(a, 1, 2)
        return jnp.transpose(a, (2, 0, 1)) if n == "w_in" else a

    def as_given(n, a):
        if n in gate_up:
            return jnp.swapaxes(a, 1, 2)
        if n == "w_in":
            return jnp.transpose(a.reshape(-1, nl, d), (1, 2, 0))
        return a.reshape(w[n].shape)

    stages = [[("meta_tokens", None)], [(n, 0) for n in ffn1_names], [(n, 0) for n in mix_names + ffn2_names]]
    stages += [[(n, l) for n in BIG] for l in range(1, nl)]
    gathers, wa = {}, {}

    halved = ffn1_names + ffn2_names + ("w_out",)

    def start_gather(si, dep=None):
        own = []
        for n, l in stages[si]:
            a = meta_tokens if l is None else (stored(n, w[n])[:, l] if n == "w_in" else stored(n, w[n])[l])
            a = a if dep is None else a + dep
            own.append(a if l is None else a.astype(BF16))
        gathers[si] = _exchange_start(f"gather_start_{si}", ["gather_half" if n in halved else "gather"
                                                             for n, _ in stages[si]], own,
                                      [_landing(a, me_xy, N_SHARD) for a in own])
        return gathers[si]["token"]

    def arrive(si, after):
        lands, token = _exchange_wait(f"gather_wait_{si}", gathers[si], after)
        if any(kd == "gather_half" for kd in gathers[si]["kinds"]):
            lands = _share_halves(f"gather_share_{si}", gathers[si]["kinds"], lands)
        for item, a in zip(stages[si], lands):
            wa[item] = a
        return token

    def mixer_weights(l):
        wi = wa["w_in", l].reshape(-1, d)
        return dict(w_main=wi[:n_main], w_lr=jnp.pad(wi[n_main:], ((0, LANE - rank), (0, 0))),
                    wgu=jnp.pad(_from_col_shards(wa["w_gate_up", l]), ((0, LANE - rank), (0, 0))),
                    wout=wa["w_out", l].reshape(-1, d))

    wp16 = w_pool.astype(BF16)
    ones = jnp.ones((1, d), F32)
    zeros = jnp.zeros((1, d), F32)
    target = jnp.concatenate([jnp.zeros((N_META, d), F32), loss_target[0], jnp.zeros((t_pad - t_real, d), F32)], axis=0)

    started = start_gather(0)
    for si in range(1, len(stages)):
        started = start_gather(si, started[0:1, 0:1])
    arrive(0, started)
    meta_full = _from_col_shards(wa["meta_tokens", None])
    h0 = jnp.concatenate([meta_full, xs, jnp.zeros((t_pad - t_real, d), F32)], axis=0)
    arrive(1, h0[:8, :LANE] + target[:8, :LANE])

    saved, mw = [], []
    cur, cur_g, cur_b = h0, ones, zeros
    for l in range(nl):
        s = {}
        xh1, rs1, hb0, g1, u1 = _ffn_fwd(cur, cur_g, cur_b, wa["ffn1_w_gate", l], wa["ffn1_w_up", l],
                                         wa["ffn1_w_down", l], alpha, f"ffn1_fwd_{l}")
        if l == 0:
            arrive(2, xh1)
        mw.append(mixer_weights(l))
        up, q, k, v, r, zg, la, hb1 = _inproj_fwd(xh1, ln1_g[l:l + 1], ln1_b[l:l + 1], mw[l]["w_main"], mw[l]["w_lr"],
                                                  mw[l]["wgu"], b_gate[l:l + 1], widths, f"inproj_fwd_{l}")
        yp, pb = _pool_fwd(up, wp16[l], pool_scale[l:l + 1], f"pool_fwd_{l}")
        o, yg, sall = _gla_fwd(q, k, v, la, r, gla_norm_g[l:l + 1], f"gla_fwd_{l}")
        xh2, rs2 = _outproj_fwd(yp, yg, mw[l]["wout"], xh1, ln1_g[l:l + 1], ln1_b[l:l + 1], alpha, f"outproj_fwd_{l}")
        if l + 1 < nl:
            arrive(l + 3, xh2)
        xh3, rs3, hb2, g2, u2 = _ffn_fwd(xh2, ln2_g[l:l + 1], ln2_b[l:l + 1], wa["ffn2_w_gate", l], wa["ffn2_w_up", l],
                                         wa["ffn2_w_down", l], alpha, f"ffn2_fwd_{l}")
        s.update(xh1=xh1, rs1=rs1, hb0=hb0, g1=g1, u1=u1, q=q, k=k, v=v, r=r, zg=zg, la=la, hb1=hb1, yp=yp, pb=pb,
                 o=o, yg=yg, sall=sall, xh2=xh2, rs2=rs2, xh3=xh3, rs3=rs3, hb2=hb2, g2=g2, u2=u2)
        saved.append(s)
        cur, cur_g, cur_b = xh3, ln3_g[l:l + 1], ln3_b[l:l + 1]

    dh, loss_acc = _loss_head(cur, cur_g, cur_b, target, s_len, "loss_head")
    loss = lax.psum(loss_acc[0, 0], ("x", "y", "c"))

    small_grads = {n: [None] * nl for n in SMALL}
    scatters = []

    def depart(name, items, grads, kinds=None):
        lands = [_landing(g if kd == "bcast" else lax.dynamic_index_in_dim(g, me_xy, 0, keepdims=False),
                          me_all if kd == "bcast" else me_xy, N_DEV if kd == "bcast" else N_SHARD)
                 for g, kd in zip(grads, kinds or ["scatter"] * len(grads))]
        st = _exchange_start(name, kinds or ["scatter"] * len(grads), grads, lands)
        scatters.append((name, items, st))
        return st["token"]

    def pack(parts):
        flat = jnp.concatenate([parts[n].reshape(-1) for n in SMALL])
        return flat.reshape(-1, LANE)

    def ffn_wgrads(l, names, hb, dgb, dub, act, dfb):
        tag = names[0][:4]
        return [_wgrad(dgb, hb, dff_s, d, f"{tag}_dwg_{l}").reshape(N_SHARD, dff_s, d),
                _wgrad(dub, hb, dff_s, d, f"{tag}_dwu_{l}").reshape(N_SHARD, dff_s, d),
                _wgrad(act, dfb, dff_s, d, f"{tag}_dwd_{l}").reshape(N_SHARD, dff_s, d)]

    late = []
    for l in reversed(range(nl)):
        s = saved[l]
        dh, dfb, dgb, dub, act, dgam, dbet = _ffn_bwd(dh, s["xh3"], s["rs3"], ln3_g[l:l + 1], s["g2"], s["u2"],
                                                      wa["ffn2_w_gate", l], wa["ffn2_w_up", l], wa["ffn2_w_down", l],
                                                      alpha, f"ffn2_bwd_{l}")
        small_grads["ln3_g"][l], small_grads["ln3_b"][l] = dgam, dbet
        gone = depart(f"scatter_start_ffn2_{l}", [(n, l) for n in ffn2_names],
                      ffn_wgrads(l, ffn2_names, s["hb2"], dgb, dub, act, dfb))

        dyb, dyp, dyg, dres, dgam, dbet = _outproj_bwd(dh, s["xh2"], s["rs2"], ln2_g[l:l + 1] + gone[0:1, 0:1],
                                                       mw[l]["wout"], pw, alpha, f"outproj_bwd_{l}")
        small_grads["ln2_g"][l], small_grads["ln2_b"][l] = dgam, dbet
        dwo = jnp.concatenate([_wgrad(s["yp"], dyb, pw, d, f"dwout_pool_{l}"),
                               _wgrad(s["yg"], dyb, vw, d, f"dwout_gla_{l}")], axis=0)
        dq, dk, dv, dr, dzg, dwgu, dbg, dgn = _gla_bwd(dyg, s["o"], s["r"], gla_norm_g[l:l + 1], s["q"], s["k"],
                                                       s["v"], s["la"], s["zg"], s["sall"], mw[l]["wgu"],
                                                       f"gla_bwd_{l}")
        dup, dwp, dsc = _pool_bwd(dyp, s["pb"], wp16[l], pool_scale[l:l + 1], f"pool_bwd_{l}")
        small_grads["b_gate"][l], small_grads["gla_norm_g"][l] = dbg, dgn
        small_grads["w_pool"][l], small_grads["pool_scale"][l] = dwp, dsc
        dh, dz = _inproj_bwd(dres, [dup, dq, dk, dv, dr], dzg, mw[l]["w_main"], mw[l]["w_lr"], f"inproj_bwd_{l}")
        dwi = jnp.concatenate([_wgrad(dz, s["hb1"], 4 * LANE, d, f"dwin_main_{l}"),
                               _wgrad(dzg, s["hb1"], LANE, d, f"dwin_lr_{l}")[:rank]], axis=0)
        gone = depart(f"scatter_start_mix_{l}", [(n, l) for n in mix_names],
                      [dwo.reshape(N_SHARD, -1, d), _col_shards(dwgu[:rank].astype(BF16)),
                       dwi.reshape(N_SHARD, -1, d)])

        dh, dfb, dgb, dub, act, dgam, dbet = _ffn_bwd(dh, s["xh1"], s["rs1"], ln1_g[l:l + 1] + gone[0:1, 0:1],
                                                      s["g1"], s["u1"], wa["ffn1_w_gate", l], wa["ffn1_w_up", l],
                                                      wa["ffn1_w_down", l], alpha, f"ffn1_bwd_{l}")
        small_grads["ln1_g"][l], small_grads["ln1_b"][l] = dgam, dbet
        if l:
            gone = depart(f"scatter_start_ffn1_{l}", [(n, l) for n in ffn1_names],
                          ffn_wgrads(l, ffn1_names, s["hb0"], dgb, dub, act, dfb))
            ln3_g = ln3_g.at[l - 1:l].add(gone[0:1, 0:1])
            continue
        grad_x = dh[N_META:t_real][None]
        small_vec = pack({n: jnp.stack(small_grads[n]) for n in SMALL})
        gone = depart("scatter_start_rest", [("meta_tokens", 0), ("small", 0)],
                      [_col_shards(dh[:N_META].astype(BF16)), small_vec], ["scatter", "bcast"])
        for n, a, b in zip(ffn1_names, (dgb, dub, act), (s["hb0"], s["hb0"], dfb)):
            g = _wgrad(a, b, dff_s, d, f"{n}_grad_{l}", after=gone).reshape(N_SHARD, dff_s, d)
            gone = depart(f"scatter_start_{n}", [(n, l)], [g])
            late.append(scatters.pop())

    recv, results, firsts = {}, {}, []

    def collect(group, after):
        for name, items, st in group:
            for item, a in zip(items, _exchange_wait(name.replace("start", "wait"), st, after)[0]):
                recv[item] = a

    def reduce_and_update(names, tag):
        partial = [_sum_slots([recv[n, l] for l in range(1 if n == "meta_tokens" else nl)], f"sum_{n}", n == "w_in")
                   for n in names]
        for n, mine, theirs in zip(names, partial, _swap_sibling(partial, f"swap_sibling_{tag}")):
            fit = lambda a: stored(n, a).reshape(mine.shape)
            outs = _adamw(fit(w[n]), [mine, theirs], fit(mom1[n]), fit(mom2[n]), f"adamw_{n}")
            results[n] = [as_given(n, o) for o in outs]
            firsts.append(outs[1][0, 0, 0])

    collect(scatters, gone)
    early = [n for n in ("meta_tokens",) + BIG if n not in ffn1_names]
    reduce_and_update(early, "early")
    small_terms = [recv["small", 0][i][None] for i in range(N_DEV)]
    souts = _adamw(pack(w)[None], small_terms, pack(mom1)[None], pack(mom2)[None], "adamw_small")
    off = 0
    for n in SMALL:
        size = w[n].size
        results[n] = [o.reshape(-1)[off:off + size].reshape(w[n].shape) for o in souts]
        off += size
    collect(late, souts[0][0, :8] + functools.reduce(jnp.add, firsts))
    reduce_and_update(ffn1_names, "late")

    out = [loss, grad_x]
    for part in range(4):
        out += [results[n][part] for n in WEIGHTS]
    return tuple(out)
```

```python
import functools

import jax
import jax.numpy as jnp
from jax import lax
from jax.experimental import pallas as pl
from jax.experimental.pallas import tpu as pltpu

F32 = jnp.float32
BF16 = jnp.bfloat16
MESH = pl.DeviceIdType.MESH

N_META = 16
POOL_WINDOWS = (2, 4, 8, 16)
POOL_HALO = 16
N_HEADS = 4
GLA_GATE_TEMP = 16.0
CHUNK = 128
CHUNK_UNROLL = 5
LN_EPS = 1e-5
RMS_EPS = 1e-6
ADAM_LR = 0.001
ADAM_B1 = 0.9
ADAM_B2 = 0.999
ADAM_EPS = 1e-08
ADAM_WD = 0.01
ADAM_STEP = 10
LANE = 128
BF16_ROWS = 16
ROW_TILE = 640
FFN_ROW_TILE = 320
FFN_CHUNKS = 2
WGRAD_K_MAX = 2176
N_SHARD = 4
N_DEV = 8

BIG = ("ffn1_w_gate", "ffn1_w_up", "ffn1_w_down", "w_in", "w_gate_up", "w_out",
       "ffn2_w_gate", "ffn2_w_up", "ffn2_w_down")
SMALL = ("ln1_g", "ln1_b", "b_gate", "w_pool", "pool_scale", "gla_norm_g", "ln2_g", "ln2_b", "ln3_g", "ln3_b")
WEIGHTS = ("meta_tokens", "ffn1_w_gate", "ffn1_w_up", "ffn1_w_down", "ln1_g", "ln1_b", "w_in", "w_gate_up",
           "b_gate", "w_pool", "pool_scale", "gla_norm_g", "w_out", "ln2_g", "ln2_b", "ffn2_w_gate",
           "ffn2_w_up", "ffn2_w_down", "ln3_g", "ln3_b")


def _tc_call(body, **kw):
    return pl.pallas_call(body, **kw)


def _comm_call(body, **kw):
    return pl.pallas_call(body, **kw)


def _seq(n):
    return pltpu.CompilerParams(dimension_semantics=("arbitrary",) * n)


def _mm(a, b):
    return jnp.dot(a.astype(BF16), b.astype(BF16), preferred_element_type=F32)


def _mm_nt(a, b):
    return lax.dot_general(a.astype(BF16), b.astype(BF16), (((1,), (1,)), ((), ())), preferred_element_type=F32)


def _mm_tn(a, b):
    return lax.dot_general(a.astype(BF16), b.astype(BF16), (((0,), (0,)), ((), ())), preferred_element_type=F32)


def _mm_f32(a, b):
    return jnp.dot(a, b, precision=lax.Precision.HIGHEST, preferred_element_type=F32)


def _row_tile(t, most=None):
    tm = min(most or ROW_TILE, t)
    while t % tm:
        tm -= LANE
    return tm


def _silu_parts(g):
    sg = jax.nn.sigmoid(g)
    return sg, g * sg


def _ln_stats(r):
    mu = jnp.mean(r, axis=-1, keepdims=True)
    rc = r - mu
    var = jnp.mean(rc * rc, axis=-1, keepdims=True)
    rs = lax.rsqrt(var + LN_EPS)
    return rc * rs, rs


def _ln_bwd(dy, xh, rs, gam):
    dyg = dy * gam
    c1 = jnp.mean(dyg, axis=-1, keepdims=True)
    c2 = jnp.mean(dyg * xh, axis=-1, keepdims=True)
    return rs * (dyg - c1 - xh * c2)


def _ffn_fwd(xin, gam_in, bet_in, wg, wu, wd, alpha, name):
    t, d = xin.shape
    nj, tf, _ = wg.shape
    tm = _row_tile(t, FFN_ROW_TILE)

    def body(x_ref, gi_ref, bi_ref, wg_ref, wu_ref, wd_ref, xhat_ref, rstd_ref, hb_ref, go_ref, uo_ref, acc, hbs):
        j = pl.program_id(1)

        @pl.when(j == 0)
        def _():
            hb = (x_ref[...] * gi_ref[...] + bi_ref[...]).astype(BF16)
            hbs[...] = hb
            hb_ref[...] = hb
            acc[...] = jnp.zeros_like(acc)

        hb = hbs[...]
        g = _mm_nt(hb, wg_ref[...])
        u = _mm_nt(hb, wu_ref[...])
        _, sl = _silu_parts(g)
        go_ref[...] = g.astype(BF16)
        uo_ref[...] = u.astype(BF16)
        acc[...] += jnp.dot((sl * u).astype(BF16), wd_ref[...], preferred_element_type=F32)

        @pl.when(j == nj - 1)
        def _():
            h = x_ref[...] * gi_ref[...] + bi_ref[...]
            xhat, rs = _ln_stats(alpha * h + 0.5 * acc[...])
            xhat_ref[...] = xhat
            rstd_ref[...] = rs

    row = lambda i, j: (i, 0)
    vec = pl.BlockSpec((1, d), lambda i, j: (0, 0))
    return _tc_call(
        body, name=name, grid=(t // tm, nj),
        in_specs=[pl.BlockSpec((tm, d), row), vec, vec] + [pl.BlockSpec((None, tf, d), lambda i, j: (j, 0, 0))] * 3,
        out_specs=[pl.BlockSpec((tm, d), row), pl.BlockSpec((tm, 1), row), pl.BlockSpec((tm, d), row),
                   pl.BlockSpec((None, tm, tf), lambda i, j: (j, i, 0)),
                   pl.BlockSpec((None, tm, tf), lambda i, j: (j, i, 0))],
        out_shape=[jax.ShapeDtypeStruct((t, d), F32), jax.ShapeDtypeStruct((t, 1), F32),
                   jax.ShapeDtypeStruct((t, d), BF16), jax.ShapeDtypeStruct((nj, t, tf), BF16),
                   jax.ShapeDtypeStruct((nj, t, tf), BF16)],
        scratch_shapes=[pltpu.VMEM((tm, d), F32), pltpu.VMEM((tm, d), BF16)],
        compiler_params=_seq(2),
    )(xin, gam_in, bet_in, wg, wu, wd)


def _ffn_bwd(dh, xhat, rstd, ln_g, gb, ub, wg, wu, wd, alpha, name):
    t, d = dh.shape
    nj, tf, _ = wg.shape
    tm = _row_tile(t, FFN_ROW_TILE)

    def body(dh_ref, xh_ref, rs_ref, g_ref, gb_ref, ub_ref, wg_ref, wu_ref, wd_ref,
             dhin_ref, df_ref, dg_ref, du_ref, act_ref, dgam_ref, dbet_ref, dr_s, df_s, acc):
        i = pl.program_id(0)
        j = pl.program_id(1)

        @pl.when(j == 0)
        def _():
            dy = dh_ref[...]
            xh = xh_ref[...]
            dr = _ln_bwd(dy, xh, rs_ref[...], g_ref[...])
            dr_s[...] = dr
            dfb = (0.5 * dr).astype(BF16)
            df_s[...] = dfb
            df_ref[...] = dfb
            acc[...] = jnp.zeros_like(acc)

            @pl.when(i == 0)
            def _():
                dgam_ref[...] = jnp.zeros_like(dgam_ref)
                dbet_ref[...] = jnp.zeros_like(dbet_ref)

            dgam_ref[...] += jnp.sum(dy * xh, axis=0, keepdims=True)
            dbet_ref[...] += jnp.sum(dy, axis=0, keepdims=True)

        dact = _mm_nt(df_s[...], wd_ref[...])
        g = gb_ref[...].astype(F32)
        u = ub_ref[...].astype(F32)
        sg, sl = _silu_parts(g)
        dg = (dact * u * (sg * (1.0 + g * (1.0 - sg)))).astype(BF16)
        du = (dact * sl).astype(BF16)
        dg_ref[...] = dg
        du_ref[...] = du
        act_ref[...] = (sl * u).astype(BF16)
        acc[...] += _mm(dg, wg_ref[...]) + _mm(du, wu_ref[...])

        @pl.when(j == nj - 1)
        def _():
            dhin_ref[...] = alpha * dr_s[...] + acc[...]

    row = lambda i, j: (i, 0)
    col = pl.BlockSpec((None, tm, tf), lambda i, j: (j, i, 0))
    vec = pl.BlockSpec((1, d), lambda i, j: (0, 0))
    ff = jax.ShapeDtypeStruct((nj, t, tf), BF16)
    return _tc_call(
        body, name=name, grid=(t // tm, nj),
        in_specs=[pl.BlockSpec((tm, d), row), pl.BlockSpec((tm, d), row), pl.BlockSpec((tm, 1), row), vec,
                  col, col] + [pl.BlockSpec((None, tf, d), lambda i, j: (j, 0, 0))] * 3,
        out_specs=[pl.BlockSpec((tm, d), row), pl.BlockSpec((tm, d), row),
                   col, col, col, vec, vec],
        out_shape=[jax.ShapeDtypeStruct((t, d), F32), jax.ShapeDtypeStruct((t, d), BF16), ff, ff, ff,
                   jax.ShapeDtypeStruct((1, d), F32), jax.ShapeDtypeStruct((1, d), F32)],
        scratch_shapes=[pltpu.VMEM((tm, d), F32), pltpu.VMEM((tm, d), BF16), pltpu.VMEM((tm, d), F32)],
        compiler_params=_seq(2),
    )(dh, xhat, rstd, ln_g, gb, ub, wg, wu, wd)


def _wgrad(a, b, tmm, tn, name, after=None):
    t = a.shape[-2]
    m = a.shape[-1] * (a.shape[0] if a.ndim == 3 else 1)
    n = b.shape[-1] * (b.shape[0] if b.ndim == 3 else 1)
    tk = max(k for k in range(BF16_ROWS, WGRAD_K_MAX + 1, BF16_ROWS) if t % k == 0)
    nk = t // tk
    extra = [] if after is None else [after]

    def body(a_ref, b_ref, *rest):
        o_ref, acc = rest[len(extra):]
        k = pl.program_id(2)

        @pl.when(k == 0)
        def _():
            acc[...] = jnp.zeros_like(acc)

        acc[...] += _mm_tn(a_ref[...], b_ref[...])

        @pl.when(k == nk - 1)
        def _():
            o_ref[...] = acc[...].astype(o_ref.dtype)

    return _tc_call(
        body, name=name, grid=(m // tmm, n // tn, nk),
        in_specs=[pl.BlockSpec((None, tk, tmm), lambda i, j, k: (i, k, 0)) if a.ndim == 3
                  else pl.BlockSpec((tk, tmm), lambda i, j, k: (k, i)),
                  pl.BlockSpec((None, tk, tn), lambda i, j, k: (j, k, 0)) if b.ndim == 3
                  else pl.BlockSpec((tk, tn), lambda i, j, k: (k, j))] + [pl.BlockSpec(memory_space=pl.ANY)] * len(extra),
        out_specs=pl.BlockSpec((tmm, tn), lambda i, j, k: (i, j)),
        out_shape=jax.ShapeDtypeStruct((m, n), BF16),
        scratch_shapes=[pltpu.VMEM((tmm, tn), F32)],
        compiler_params=_seq(3),
    )(a, b, *extra)


def _inproj_fwd(xhat, gam, bet, w_main, w_lr, wgu, b_gate, widths, name):
    t, d = xhat.shape
    tm = _row_tile(t)
    kw = wgu.shape[1]
    offs = [0]
    for w in widths:
        offs.append(offs[-1] + w)

    def body(x_ref, g_ref, b_ref, wm_ref, wl_ref, wgu_ref, bg_ref, *outs):
        piece_refs, (zg_ref, la_ref, hb_ref) = outs[:len(widths)], outs[len(widths):]
        hb = (x_ref[...] * g_ref[...] + b_ref[...]).astype(BF16)
        hb_ref[...] = hb
        for p, ref in enumerate(piece_refs):
            ref[...] = _mm_nt(hb, wm_ref[offs[p]:offs[p + 1], :])
        zg = _mm_nt(hb, wl_ref[...])
        zg_ref[...] = zg
        logit = _mm(zg, wgu_ref[...]) + bg_ref[...]
        la_ref[...] = (jnp.minimum(logit, 0.0) - jnp.log(1.0 + jnp.exp(-jnp.abs(logit)))) * (1.0 / GLA_GATE_TEMP)

    row = lambda i: (i, 0)
    full = lambda a: pl.BlockSpec(a.shape, lambda i: (0,) * a.ndim)
    out_w = list(widths) + [LANE, kw]
    return _tc_call(
        body, name=name, grid=(t // tm,),
        in_specs=[pl.BlockSpec((tm, d), row), full(gam), full(bet), full(w_main), full(w_lr), full(wgu), full(b_gate)],
        out_specs=[pl.BlockSpec((tm, w), row) for w in out_w] + [pl.BlockSpec((tm, d), row)],
        out_shape=[jax.ShapeDtypeStruct((t, w), F32) for w in out_w] + [jax.ShapeDtypeStruct((t, d), BF16)],
        compiler_params=_seq(1),
    )(xhat, gam, bet, w_main, w_lr, wgu, b_gate)


def _inproj_bwd(dh_part, pieces, dzg, w_main, w_lr, name):
    t, d = dh_part.shape
    tm = _row_tile(t)
    widths = [p.shape[1] for p in pieces]
    offs = [0]
    for w in widths:
        offs.append(offs[-1] + w)

    def body(*refs):
        dhp_ref = refs[0]
        p_refs = refs[1:1 + len(widths)]
        dzg_ref, wm_ref, wl_ref, dh_ref, dz_ref = refs[1 + len(widths):]
        acc = dhp_ref[...] + _mm(dzg_ref[...], wl_ref[...])
        for p, ref in enumerate(p_refs):
            v = ref[...]
            dz_ref[:, offs[p]:offs[p + 1]] = v
            acc += _mm(v, wm_ref[offs[p]:offs[p + 1], :])
        dh_ref[...] = acc

    row = lambda i: (i, 0)
    full = lambda a: pl.BlockSpec(a.shape, lambda i: (0,) * a.ndim)
    return _tc_call(
        body, name=name, grid=(t // tm,),
        in_specs=[pl.BlockSpec((tm, d), row)] + [pl.BlockSpec((tm, w), row) for w in widths]
                 + [pl.BlockSpec((tm, LANE), row), full(w_main), full(w_lr)],
        out_specs=[pl.BlockSpec((tm, d), row), pl.BlockSpec((tm, offs[-1]), row)],
        out_shape=[jax.ShapeDtypeStruct((t, d), F32), jax.ShapeDtypeStruct((t, offs[-1]), BF16)],
        compiler_params=_seq(1),
    )(dh_part, *pieces, dzg, w_main, w_lr)


def _pool_cnt(tile, tm, w):
    t = tile * tm + lax.broadcasted_iota(jnp.int32, (tm, 1), 0)
    return jnp.minimum(t + 1, w).astype(F32)


def _pool_fwd(u, wp, scale, name):
    t, pw = u.shape
    tm = _row_tile(t)
    gd = wp.shape[1]

    def body(u_ref, wp_ref, sc_ref, y_ref, p_ref, ext):
        i = pl.program_id(0)

        @pl.when(i == 0)
        def _():
            ext[0:POOL_HALO, :] = jnp.zeros((POOL_HALO, pw), F32)

        ext[POOL_HALO:POOL_HALO + tm, :] = u_ref[...]
        for gi, w in enumerate(POOL_WINDOWS):
            cols = slice(gi * gd, (gi + 1) * gd)
            s = ext[pl.ds(POOL_HALO, tm), cols]
            tot = s
            for back in range(1, w):
                tot = tot + ext[pl.ds(POOL_HALO - back, tm), cols]
            p = (tot / _pool_cnt(i, tm, w) - s).astype(BF16)
            p_ref[:, cols] = p
            y_ref[:, cols] = (jnp.dot(p, wp_ref[gi], preferred_element_type=F32) * sc_ref[:, cols]).astype(BF16)
        ext[0:POOL_HALO, :] = ext[tm:tm + POOL_HALO, :]

    row = lambda i: (i, 0)
    return _tc_call(
        body, name=name, grid=(t // tm,),
        in_specs=[pl.BlockSpec((tm, pw), row), pl.BlockSpec(wp.shape, lambda i: (0, 0, 0)),
                  pl.BlockSpec((1, pw), lambda i: (0, 0))],
        out_specs=[pl.BlockSpec((tm, pw), row), pl.BlockSpec((tm, pw), row)],
        out_shape=[jax.ShapeDtypeStruct((t, pw), BF16), jax.ShapeDtypeStruct((t, pw), BF16)],
        scratch_shapes=[pltpu.VMEM((tm + POOL_HALO, pw), F32)],
        compiler_params=_seq(1),
    )(u, wp, scale)


def _pool_bwd(dy, pb, wp, scale, name):
    t, pw = dy.shape
    tm = _row_tile(t)
    nt = t // tm
    gd = wp.shape[1]

    def body(dy_ref, p_ref, wp_ref, sc_ref, du_ref, dwp_ref, dsc_ref, ext):
        i = pl.program_id(0)
        tile = nt - 1 - i

        @pl.when(i == 0)
        def _():
            ext[tm:tm + POOL_HALO, :] = jnp.zeros((POOL_HALO, pw), F32)
            dwp_ref[...] = jnp.zeros_like(dwp_ref)
            dsc_ref[...] = jnp.zeros_like(dsc_ref)

        dps = []
        for gi, w in enumerate(POOL_WINDOWS):
            cols = slice(gi * gd, (gi + 1) * gd)
            dyv = dy_ref[:, cols]
            p = p_ref[:, cols]
            dpre = (dyv * sc_ref[:, cols]).astype(BF16)
            dsc_ref[:, cols] += jnp.sum(dyv * jnp.dot(p, wp_ref[gi], preferred_element_type=F32), axis=0, keepdims=True)
            dwp_ref[gi] += _mm_tn(p, dpre)
            dp = _mm_nt(dpre, wp_ref[gi])
            dps.append(dp)
            ext[0:tm, cols] = dp / _pool_cnt(tile, tm, w)
        for gi, w in enumerate(POOL_WINDOWS):
            cols = slice(gi * gd, (gi + 1) * gd)
            tot = ext[pl.ds(0, tm), cols]
            for fwd in range(1, w):
                tot = tot + ext[pl.ds(fwd, tm), cols]
            du_ref[:, cols] = (tot - dps[gi]).astype(BF16)
        ext[tm:tm + POOL_HALO, :] = ext[0:POOL_HALO, :]

    row = lambda i: (nt - 1 - i, 0)
    return _tc_call(
        body, name=name, grid=(nt,),
        in_specs=[pl.BlockSpec((tm, pw), row), pl.BlockSpec((tm, pw), row),
                  pl.BlockSpec(wp.shape, lambda i: (0, 0, 0)), pl.BlockSpec((1, pw), lambda i: (0, 0))],
        out_specs=[pl.BlockSpec((tm, pw), row), pl.BlockSpec(wp.shape, lambda i: (0, 0, 0)),
                   pl.BlockSpec((1, pw), lambda i: (0, 0))],
        out_shape=[jax.ShapeDtypeStruct((t, pw), BF16), jax.ShapeDtypeStruct(wp.shape, F32),
                   jax.ShapeDtypeStruct((1, pw), F32)],
        scratch_shapes=[pltpu.VMEM((tm + POOL_HALO, pw), F32)],
        compiler_params=_seq(1),
    )(dy, pb, wp, scale)


def _gla_masks(kw, vw):
    dk, dv = kw // N_HEADS, vw // N_HEADS
    lane_k = lax.broadcasted_iota(jnp.int32, (1, kw), 1)
    lane_v = lax.broadcasted_iota(jnp.int32, (1, vw), 1)
    hk = [((lane_k >= h * dk) & (lane_k < (h + 1) * dk)).astype(F32) for h in range(N_HEADS)]
    hv = [((lane_v >= h * dv) & (lane_v < (h + 1) * dv)).astype(F32) for h in range(N_HEADS)]
    r = lax.broadcasted_iota(jnp.int32, (CHUNK, CHUNK), 0)
    c = lax.broadcasted_iota(jnp.int32, (CHUNK, CHUNK), 1)
    tril = r >= c
    rs = lax.broadcasted_iota(jnp.int32, (N_HEADS * CHUNK, CHUNK), 0) & (CHUNK - 1)
    stril = rs >= lax.broadcasted_iota(jnp.int32, (N_HEADS * CHUNK, CHUNK), 1)
    return hk, hv, tril, stril


def _block_diag(x, hk, dv):
    return jnp.concatenate([x[h * dv:(h + 1) * dv, :] * hk[h] for h in range(N_HEADS)], axis=0)


def _gla_fwd(q, k, v, loga, r, gnorm, name):
    t, kw = q.shape
    vw = v.shape[1]
    dk, dv = kw // N_HEADS, vw // N_HEADS
    tm = _row_tile(t)
    nc = tm // CHUNK
    qscale = dk ** -0.5

    def body(q_ref, k_ref, v_ref, la_ref, r_ref, gn_ref, o_ref, y_ref, sall_ref, st):
        @pl.when(pl.program_id(0) == 0)
        def _():
            st[...] = jnp.zeros_like(st)

        hk, hv, tril, stril = _gla_masks(kw, vw)
        trif = tril.astype(F32)

        def chunk(c, carry):
            rows = pl.ds(pl.multiple_of(c * CHUNK, CHUNK), CHUNK)
            la = la_ref[rows, :]
            b = _mm_f32(trif, la)
            bl = jnp.sum(la, axis=0, keepdims=True)
            qb = q_ref[rows, :] * (qscale * jnp.exp(b))
            kk = k_ref[rows, :]
            kb = kk * jnp.exp(-b)
            kl = kk * jnp.exp(bl - b)
            vv = v_ref[rows, :]
            s_t = st[...]
            compact = s_t[0:dv, :]
            for h in range(1, N_HEADS):
                compact = compact + s_t[h * dv:(h + 1) * dv, :]
            sall_ref[c] = compact
            qx = jnp.concatenate([qb * hk[h] for h in range(N_HEADS)], axis=0)
            a = jnp.where(stril, _mm_nt(qx, kb), 0.0).astype(BF16)
            o_inter = _mm_nt(qb, s_t)
            for h in range(N_HEADS):
                vs = slice(h * dv, (h + 1) * dv)
                o_ref[rows, vs] = o_inter[:, vs] + _mm(a[h * CHUNK:(h + 1) * CHUNK, :], vv[:, vs])
            st[...] = s_t * jnp.exp(bl) + _block_diag(_mm_tn(vv, kl), hk, dv)
            return carry

        lax.fori_loop(0, nc, chunk, 0, unroll=CHUNK_UNROLL)
        for h in range(N_HEADS):
            vs = slice(h * dv, (h + 1) * dv)
            oh = o_ref[:, vs]
            on = oh * lax.rsqrt(jnp.mean(oh * oh, axis=-1, keepdims=True) + RMS_EPS)
            _, sl = _silu_parts(r_ref[:, vs])
            y_ref[:, vs] = (on * gn_ref[:, vs] * sl).astype(BF16)

    row = lambda i: (i, 0)
    return _tc_call(
        body, name=name, grid=(t // tm,),
        in_specs=[pl.BlockSpec((tm, kw), row), pl.BlockSpec((tm, kw), row), pl.BlockSpec((tm, vw), row),
                  pl.BlockSpec((tm, kw), row), pl.BlockSpec((tm, vw), row), pl.BlockSpec((1, vw), lambda i: (0, 0))],
        out_specs=[pl.BlockSpec((tm, vw), row), pl.BlockSpec((tm, vw), row),
                   pl.BlockSpec((nc, dv, kw), lambda i: (i, 0, 0))],
        out_shape=[jax.ShapeDtypeStruct((t, vw), F32), jax.ShapeDtypeStruct((t, vw), BF16),
                   jax.ShapeDtypeStruct((t // CHUNK, dv, kw), F32)],
        scratch_shapes=[pltpu.VMEM((vw, kw), F32)],
        compiler_params=_seq(1),
    )(q, k, v, loga, r, gnorm)


def _gla_bwd(dy, o, r, gnorm, q, k, v, loga, zg, sall, wgu, name):
    t, kw = q.shape
    vw = v.shape[1]
    dk, dv = kw // N_HEADS, vw // N_HEADS
    tm = _row_tile(t)
    nt = t // tm
    nc = tm // CHUNK
    qscale = dk ** -0.5

    def body(dy_ref, o_ref, r_ref, gn_ref, q_ref, k_ref, v_ref, la_ref, zg_ref, sall_ref, wgu_ref,
             dq_ref, dk_ref, dv_ref, dr_ref, dzg_ref, dwgu_ref, dbg_ref, dgn_ref, dst, do_s):
        @pl.when(pl.program_id(0) == 0)
        def _():
            dst[...] = jnp.zeros_like(dst)
            dwgu_ref[...] = jnp.zeros_like(dwgu_ref)
            dbg_ref[...] = jnp.zeros_like(dbg_ref)
            dgn_ref[...] = jnp.zeros_like(dgn_ref)

        for h in range(N_HEADS):
            vs = slice(h * dv, (h + 1) * dv)
            oh = o_ref[:, vs]
            rinv = lax.rsqrt(jnp.mean(oh * oh, axis=-1, keepdims=True) + RMS_EPS)
            on = oh * rinv
            rr = r_ref[:, vs]
            sg, sl = _silu_parts(rr)
            dyv = dy_ref[:, vs]
            gn = gn_ref[:, vs]
            dgn_ref[:, vs] += jnp.sum(dyv * on * sl, axis=0, keepdims=True)
            dr_ref[:, vs] = (dyv * on * gn * (sg * (1.0 + rr * (1.0 - sg)))).astype(BF16)
            don = dyv * gn * sl
            do_s[:, vs] = rinv * (don - on * jnp.mean(don * on, axis=-1, keepdims=True))

        hk, hv, tril, stril = _gla_masks(kw, vw)
        trif = tril.astype(F32)
        triuf = (lax.broadcasted_iota(jnp.int32, (CHUNK, CHUNK), 0)
                 <= lax.broadcasted_iota(jnp.int32, (CHUNK, CHUNK), 1)).astype(F32)
        last_row = lax.broadcasted_iota(jnp.int32, (CHUNK, 1), 0) == CHUNK - 1

        def chunk(idx, carry):
            c = nc - 1 - idx
            rows = pl.ds(pl.multiple_of(c * CHUNK, CHUNK), CHUNK)
            la = la_ref[rows, :]
            b = _mm_f32(trif, la)
            bl = jnp.sum(la, axis=0, keepdims=True)
            eb = jnp.exp(b)
            enb = jnp.exp(-b)
            ebl = jnp.exp(bl - b)
            el = jnp.exp(bl)
            qb = q_ref[rows, :] * (qscale * eb)
            kk = k_ref[rows, :]
            kb = kk * enb
            kl = kk * ebl
            vv = v_ref[rows, :]
            do = do_s[rows, :]
            compact = sall_ref[c]
            s_t = jnp.concatenate([compact * hk[h] for h in range(N_HEADS)], axis=0)
            ds_t = dst[...]
            qx = jnp.concatenate([qb * hk[h] for h in range(N_HEADS)], axis=0)
            dox = jnp.concatenate([do * hv[h] for h in range(N_HEADS)], axis=0)
            a = jnp.where(stril, _mm_nt(qx, kb), 0.0).astype(BF16)
            da = jnp.where(stril, _mm_nt(dox, vv), 0.0).astype(BF16)
            dv_ref[rows, :] = (_mm_tn(a, dox) + _mm_nt(kl, ds_t)).astype(BF16)
            dak = _mm(da, kb)
            dqb = _mm(do, s_t)
            for h in range(N_HEADS):
                dqb = dqb + dak[h * CHUNK:(h + 1) * CHUNK, :] * hk[h]
            dkb = _mm_tn(da, qx)
            dkl = _mm(vv, ds_t)
            dbl = jnp.sum(dkl * kl, axis=0, keepdims=True) + el * jnp.sum(ds_t * s_t, axis=0, keepdims=True)
            dst[...] = ds_t * el + _block_diag(_mm_tn(do, qb), hk, dv)
            dq_ref[rows, :] = (dqb * (qscale * eb)).astype(BF16)
            dk_ref[rows, :] = (dkb * enb + dkl * ebl).astype(BF16)
            db = dqb * qb - dkb * kb - dkl * kl + jnp.where(last_row, dbl, 0.0)
            dla = _mm_f32(triuf, db)
            dlogit = dla * (1.0 / GLA_GATE_TEMP) * (1.0 - jnp.exp(GLA_GATE_TEMP * la))
            dzg_ref[rows, :] = _mm_nt(dlogit, wgu_ref[...]).astype(BF16)
            dwgu_ref[...] += _mm_tn(zg_ref[rows, :], dlogit)
            dbg_ref[...] += jnp.sum(dlogit, axis=0, keepdims=True)
            return carry

        lax.fori_loop(0, nc, chunk, 0, unroll=CHUNK_UNROLL)

    row = lambda i: (nt - 1 - i, 0)
    const = lambda i: (0, 0)
    return _tc_call(
        body, name=name, grid=(nt,),
        in_specs=[pl.BlockSpec((tm, vw), row), pl.BlockSpec((tm, vw), row), pl.BlockSpec((tm, vw), row),
                  pl.BlockSpec((1, vw), const), pl.BlockSpec((tm, kw), row), pl.BlockSpec((tm, kw), row),
                  pl.BlockSpec((tm, vw), row), pl.BlockSpec((tm, kw), row), pl.BlockSpec((tm, LANE), row),
                  pl.BlockSpec((nc, dv, kw), lambda i: (nt - 1 - i, 0, 0)), pl.BlockSpec((LANE, kw), const)],
        out_specs=[pl.BlockSpec((tm, kw), row), pl.BlockSpec((tm, kw), row), pl.BlockSpec((tm, vw), row),
                   pl.BlockSpec((tm, vw), row), pl.BlockSpec((tm, LANE), row), pl.BlockSpec((LANE, kw), const),
                   pl.BlockSpec((1, kw), const), pl.BlockSpec((1, vw), const)],
        out_shape=[jax.ShapeDtypeStruct((t, kw), BF16), jax.ShapeDtypeStruct((t, kw), BF16),
                   jax.ShapeDtypeStruct((t, vw), BF16), jax.ShapeDtypeStruct((t, vw), BF16),
                   jax.ShapeDtypeStruct((t, LANE), BF16), jax.ShapeDtypeStruct((LANE, kw), F32),
                   jax.ShapeDtypeStruct((1, kw), F32), jax.ShapeDtypeStruct((1, vw), F32)],
        scratch_shapes=[pltpu.VMEM((vw, kw), F32), pltpu.VMEM((tm, vw), F32)],
        compiler_params=_seq(1),
    )(dy, o, r, gnorm, q, k, v, loga, zg, sall, wgu)


def _outproj_fwd(yp, yg, w_out, xhat, gam, bet, alpha, name):
    t, d = xhat.shape
    pw = yp.shape[1]
    tm = _row_tile(t)

    def body(yp_ref, yg_ref, w_ref, x_ref, g_ref, b_ref, xhat_ref, rstd_ref):
        h = x_ref[...] * g_ref[...] + b_ref[...]
        y = (jnp.dot(yp_ref[...], w_ref[0:pw, :], preferred_element_type=F32)
             + jnp.dot(yg_ref[...], w_ref[pw:, :], preferred_element_type=F32))
        xh, rs = _ln_stats(alpha * h + y)
        xhat_ref[...] = xh
        rstd_ref[...] = rs

    row = lambda i: (i, 0)
    vec = pl.BlockSpec((1, d), lambda i: (0, 0))
    return _tc_call(
        body, name=name, grid=(t // tm,),
        in_specs=[pl.BlockSpec((tm, pw), row), pl.BlockSpec((tm, yg.shape[1]), row),
                  pl.BlockSpec(w_out.shape, lambda i: (0, 0)), pl.BlockSpec((tm, d), row), vec, vec],
        out_specs=[pl.BlockSpec((tm, d), row), pl.BlockSpec((tm, 1), row)],
        out_shape=[jax.ShapeDtypeStruct((t, d), F32), jax.ShapeDtypeStruct((t, 1), F32)],
        compiler_params=_seq(1),
    )(yp, yg, w_out, xhat, gam, bet)


def _outproj_bwd(dh, xhat, rstd, ln_g, w_out, pw, alpha, name):
    t, d = dh.shape
    tm = _row_tile(t)
    gw = w_out.shape[0] - pw

    def body(dh_ref, xh_ref, rs_ref, g_ref, w_ref, dyb_ref, dyp_ref, dyg_ref, dres_ref, dgam_ref, dbet_ref):
        @pl.when(pl.program_id(0) == 0)
        def _():
            dgam_ref[...] = jnp.zeros_like(dgam_ref)
            dbet_ref[...] = jnp.zeros_like(dbet_ref)

        dy = dh_ref[...]
        xh = xh_ref[...]
        dr = _ln_bwd(dy, xh, rs_ref[...], g_ref[...])
        dgam_ref[...] += jnp.sum(dy * xh, axis=0, keepdims=True)
        dbet_ref[...] += jnp.sum(dy, axis=0, keepdims=True)
        drb = dr.astype(BF16)
        dyb_ref[...] = drb
        dres_ref[...] = alpha * dr
        dyp_ref[...] = _mm_nt(drb, w_ref[0:pw, :])
        dyg_ref[...] = _mm_nt(drb, w_ref[pw:, :])

    row = lambda i: (i, 0)
    vec = pl.BlockSpec((1, d), lambda i: (0, 0))
    return _tc_call(
        body, name=name, grid=(t // tm,),
        in_specs=[pl.BlockSpec((tm, d), row), pl.BlockSpec((tm, d), row), pl.BlockSpec((tm, 1), row), vec,
                  pl.BlockSpec(w_out.shape, lambda i: (0, 0))],
        out_specs=[pl.BlockSpec((tm, d), row), pl.BlockSpec((tm, pw), row), pl.BlockSpec((tm, gw), row),
                   pl.BlockSpec((tm, d), row), vec, vec],
        out_shape=[jax.ShapeDtypeStruct((t, d), BF16), jax.ShapeDtypeStruct((t, pw), F32),
                   jax.ShapeDtypeStruct((t, gw), F32), jax.ShapeDtypeStruct((t, d), F32),
                   jax.ShapeDtypeStruct((1, d), F32), jax.ShapeDtypeStruct((1, d), F32)],
        compiler_params=_seq(1),
    )(dh, xhat, rstd, ln_g, w_out)


def _loss_head(xhat, gam, bet, target, n_rows, name):
    t, d = xhat.shape
    tm = _row_tile(t)

    def body(x_ref, g_ref, b_ref, t_ref, dy_ref, loss_ref):
        i = pl.program_id(0)

        @pl.when(i == 0)
        def _():
            loss_ref[...] = jnp.zeros_like(loss_ref)

        rowi = i * tm + lax.broadcasted_iota(jnp.int32, (tm, 1), 0)
        live = (rowi >= N_META) & (rowi < N_META + n_rows)
        diff = jnp.where(live, x_ref[...] * g_ref[...] + b_ref[...] - t_ref[...], 0.0)
        dy_ref[...] = diff * (1.0 / d)
        loss_ref[...] += jnp.sum(diff * diff) * (0.5 / d)

    row = lambda i: (i, 0)
    vec = pl.BlockSpec((1, d), lambda i: (0, 0))
    return _tc_call(
        body, name=name, grid=(t // tm,),
        in_specs=[pl.BlockSpec((tm, d), row), vec, vec, pl.BlockSpec((tm, d), row)],
        out_specs=[pl.BlockSpec((tm, d), row), pl.BlockSpec((8, LANE), lambda i: (0, 0))],
        out_shape=[jax.ShapeDtypeStruct((t, d), F32), jax.ShapeDtypeStruct((8, LANE), F32)],
        compiler_params=_seq(1),
    )(xhat, gam, bet, target)


def _rows_block(r, c):
    best = r
    for cand in range(8, r, 8):
        if r % cand == 0 and cand * c * 4 <= (1 << 20):
            best = cand
    return best if best * c * 4 <= (4 << 20) else r


def _sum_slots(recvs, name, layers_side_by_side=False):
    nl = len(recvs)
    ns, r, c = recvs[0].shape
    tr = _rows_block(r, c)

    def body(*refs):
        o_ref = refs[nl]
        for l in range(nl):
            acc = refs[l][0].astype(F32)
            for s in range(1, ns):
                acc = acc + refs[l][s].astype(F32)
            if layers_side_by_side:
                o_ref[0, :, l * c:(l + 1) * c] = acc
            else:
                o_ref[l] = acc

    out = (1, r, nl * c) if layers_side_by_side else (nl, r, c)
    return _tc_call(
        body, name=name, grid=(r // tr,),
        in_specs=[pl.BlockSpec((ns, tr, c), lambda i: (0, i, 0))] * nl,
        out_specs=pl.BlockSpec((out[0], tr, out[2]), lambda i: (0, i, 0)),
        out_shape=jax.ShapeDtypeStruct(out, F32),
        compiler_params=_seq(1),
    )(*recvs)


def _adamw(w, terms, m, v, name):
    nl, r, c = w.shape
    tc = c
    while tc % (2 * LANE) == 0 and tc > 4 * LANE:
        tc //= 2
    tr = _rows_block(r, tc)
    nterm = len(terms)

    def body(*refs):
        w_ref = refs[0]
        t_refs = refs[1:1 + nterm]
        m_ref, v_ref, g_ref, d_ref, nm_ref, nv_ref = refs[1 + nterm:]
        g = t_refs[0][...]
        for tr_ in t_refs[1:]:
            g = g + tr_[...]
        nm = ADAM_B1 * m_ref[...] + (1.0 - ADAM_B1) * g
        nv = ADAM_B2 * v_ref[...] + (1.0 - ADAM_B2) * jnp.square(g)
        m_hat = nm / (1.0 - ADAM_B1 ** ADAM_STEP)
        v_hat = nv / (1.0 - ADAM_B2 ** ADAM_STEP)
        g_ref[...] = g
        d_ref[...] = -ADAM_LR * (m_hat / (jnp.sqrt(v_hat) + ADAM_EPS) + ADAM_WD * w_ref[...])
        nm_ref[...] = nm
        nv_ref[...] = nv

    spec = pl.BlockSpec((None, tr, tc), lambda l, i, j: (l, i, j))
    shp = jax.ShapeDtypeStruct((nl, r, c), F32)
    return _tc_call(
        body, name=name, grid=(nl, r // tr, c // tc),
        in_specs=[spec] * (3 + nterm), out_specs=[spec] * 4, out_shape=[shp] * 4,
        compiler_params=_seq(3),
    )(w, *terms, m, v)


XY_RELATIONS = ((1, 0, 0), (0, 1, 0), (1, 1, 0))
ALL_RELATIONS = tuple((fx, fy, fc) for fx in (0, 1) for fy in (0, 1) for fc in (0, 1) if fx or fy or fc)
HBM_SPEC = pl.BlockSpec(memory_space=pltpu.HBM)
SEM_SPEC = pl.BlockSpec(memory_space=pltpu.SEMAPHORE)
DATAFLOW = pltpu.SideEffectType.DATAFLOW_SIDE_EFFECTING


def _split_call(body, **kw):
    return pl.pallas_call(body, **kw)


def _flip(v, f):
    return 1 - v if f else v


def _any_spec(n):
    return [pl.BlockSpec(memory_space=pl.ANY)] * n


def _relations(kind):
    return ALL_RELATIONS if kind == "bcast" else XY_RELATIONS


def _copies(kind, arr, land, sems):
    x, y, c = lax.axis_index("x"), lax.axis_index("y"), lax.axis_index("c")
    out = []
    for (fx, fy, fc), (send_sem, recv_sem) in zip(_relations(kind), sems):
        px, py, pc = _flip(x, fx), _flip(y, fy), _flip(c, fc)
        if kind == "bcast":
            mine, theirs = 4 * x + 2 * y + c, 4 * px + 2 * py + pc
        else:
            mine, theirs = 2 * x + y, 2 * px + py
        src, to_mine, to_theirs = arr, land.at[mine], land.at[theirs]
        if kind == "scatter":
            src = arr.at[theirs]
        if kind == "gather_half":
            rows = _my_half(arr.shape[0], c)
            src, to_mine, to_theirs = arr.at[rows], land.at[mine, rows], land.at[theirs, rows]
        both = dict(src_ref=src, send_sem=send_sem, recv_sem=recv_sem, device_id=(px, py, pc), device_id_type=MESH)
        out.append((pltpu.make_async_remote_copy(dst_ref=to_mine, **both),
                    pltpu.make_async_remote_copy(dst_ref=to_theirs, **both)))
    return out


def _my_half(nrows, c):
    return pl.ds(c * (nrows // 2), nrows // 2)


def _share_halves(name, kinds, lands):
    ks = [k for k, kd in enumerate(kinds) if kd == "gather_half"]
    n = len(ks)

    def body(*refs):
        l_refs = refs[n:2 * n]
        send_sems, recv_sems = refs[2 * n:]
        x, y, c = lax.axis_index("x"), lax.axis_index("y"), lax.axis_index("c")
        copies = []
        for i in range(n):
            nrows = l_refs[i].shape[1]
            for r, (fx, fy, _) in enumerate(XY_RELATIONS):
                slot = 2 * _flip(x, fx) + _flip(y, fy)
                both = dict(src_ref=l_refs[i].at[slot, _my_half(nrows, c)], send_sem=send_sems.at[i, r],
                            recv_sem=recv_sems.at[i, r], device_id=(x, y, 1 - c), device_id_type=MESH)
                copies.append((pltpu.make_async_remote_copy(dst_ref=l_refs[i].at[slot, _my_half(nrows, c)], **both),
                               pltpu.make_async_remote_copy(dst_ref=l_refs[i].at[slot, _my_half(nrows, 1 - c)], **both)))
        for send, _ in copies:
            send.start()
        for _, arrival in copies:
            arrival.wait_recv()
        for send, _ in copies:
            send.wait_send()

    outs = _comm_call(
        body, name=name,
        in_specs=_any_spec(n), out_specs=_any_spec(n),
        out_shape=[jax.ShapeDtypeStruct(lands[k].shape, lands[k].dtype) for k in ks],
        input_output_aliases={i: i for i in range(n)},
        scratch_shapes=[pltpu.SemaphoreType.DMA((n, 3)), pltpu.SemaphoreType.DMA((n, 3))],
    )(*[lands[k] for k in ks])
    lands = list(lands)
    for k, o in zip(ks, outs):
        lands[k] = o
    return lands


def _sem_pairs(kinds, sems):
    out, at = [], 0
    for kind in kinds:
        nrel = len(_relations(kind))
        out.append([(sems[at + 2 * r], sems[at + 2 * r + 1]) for r in range(nrel)])
        at += 2 * nrel
    return out


def _exchange_start(name, kinds, arrs, lands):
    n = len(arrs)
    nsem = sum(2 * len(_relations(kd)) for kd in kinds)

    def body(*refs):
        a_refs, l_refs = refs[:n], refs[n:2 * n]
        pairs = _sem_pairs(kinds, refs[2 * n:2 * n + nsem])
        token = refs[-1]
        for k in range(n):
            for send, _ in _copies(kinds[k], a_refs[k], l_refs[k], pairs[k]):
                send.start()
        token[...] = jnp.zeros_like(token)

    thru = [pltpu.HBM(a.shape, a.dtype) for a in list(arrs) + list(lands)]
    outs = _split_call(
        body, name=name,
        out_shape=(*[pltpu.SemaphoreType.DMA(())] * nsem, *thru, jax.ShapeDtypeStruct((8, LANE), F32)),
        in_specs=[HBM_SPEC] * (2 * n),
        out_specs=(*[SEM_SPEC] * nsem, *[HBM_SPEC] * (2 * n), pl.BlockSpec(memory_space=pltpu.VMEM)),
        input_output_aliases={i: nsem + i for i in range(2 * n)},
        compiler_params=pltpu.CompilerParams(has_side_effects=DATAFLOW),
    )(*[pltpu.with_memory_space_constraint(a, pltpu.HBM) for a in list(arrs) + list(lands)])
    return dict(kinds=kinds, sems=outs[:nsem], arrs=outs[nsem:nsem + n], lands=outs[nsem + n:nsem + 2 * n],
                token=outs[-1])


def _exchange_wait(name, st, after):
    kinds = st["kinds"]
    n = len(kinds)
    nsem = len(st["sems"])

    def body(*refs):
        a_refs, l_refs = refs[:n], refs[n:2 * n]
        pairs = _sem_pairs(kinds, refs[2 * n:2 * n + nsem])
        for k in range(n):
            for _, arrival in _copies(kinds[k], a_refs[k], l_refs[k], pairs[k]):
                arrival.wait_send()
                arrival.wait_recv()
        refs[-1][...] = jnp.zeros_like(refs[-1])

    ins = list(st["arrs"]) + list(st["lands"])
    outs = _split_call(
        body, name=name,
        out_shape=[pltpu.HBM(a.shape, a.dtype) for a in ins] + [jax.ShapeDtypeStruct((8, LANE), F32)],
        in_specs=[HBM_SPEC] * (2 * n) + [SEM_SPEC] * nsem + [pl.BlockSpec(memory_space=pl.ANY)],
        out_specs=[HBM_SPEC] * (2 * n) + [pl.BlockSpec(memory_space=pltpu.VMEM)],
        input_output_aliases={i: i for i in range(2 * n)},
        compiler_params=pltpu.CompilerParams(has_side_effects=DATAFLOW),
    )(*ins, *st["sems"], after)
    return outs[n:2 * n], outs[-1]


def _landing(own, slot, nslot):
    return lax.dynamic_update_slice(lax.empty((nslot,) + own.shape, own.dtype), own[None], (slot,) + (0,) * own.ndim)


def _swap_sibling(parts, name):
    n = len(parts)

    def body(*refs):
        ins, outs = refs[:n], refs[n:2 * n]
        send_sems, recv_sems = refs[2 * n:]
        sib = (lax.axis_index("x"), lax.axis_index("y"), 1 - lax.axis_index("c"))
        cps = [pltpu.make_async_remote_copy(src_ref=ins[k], dst_ref=outs[k], send_sem=send_sems.at[k],
                                            recv_sem=recv_sems.at[k], device_id=sib, device_id_type=MESH)
               for k in range(n)]
        for cp in cps:
            cp.start()
        for cp in cps:
            cp.wait_recv()
        for cp in cps:
            cp.wait_send()

    return _comm_call(
        body, name=name,
        in_specs=_any_spec(n), out_specs=_any_spec(n),
        out_shape=[jax.ShapeDtypeStruct(p.shape, p.dtype) for p in parts],
        scratch_shapes=[pltpu.SemaphoreType.DMA((n,)), pltpu.SemaphoreType.DMA((n,))],
    )(*parts)


def _col_shards(a, n=N_SHARD):
    r, c = a.shape
    return a.reshape(r, n, c // n).transpose(1, 0, 2)


def _from_col_shards(a):
    n, r, cs = a.shape
    return a.transpose(1, 0, 2).reshape(r, n * cs)


def kernel(x, meta_tokens, ffn1_w_gate, ffn1_w_up, ffn1_w_down, ln1_g, ln1_b, w_in, w_gate_up, b_gate, w_pool, pool_scale, gla_norm_g, w_out, ln2_g, ln2_b, ffn2_w_gate, ffn2_w_up, ffn2_w_down, ln3_g, ln3_b, loss_target, m_meta_tokens, m_ffn1_w_gate, m_ffn1_w_up, m_ffn1_w_down, m_ln1_g, m_ln1_b, m_w_in, m_w_gate_up, m_b_gate, m_w_pool, m_pool_scale, m_gla_norm_g, m_w_out, m_ln2_g, m_ln2_b, m_ffn2_w_gate, m_ffn2_w_up, m_ffn2_w_down, m_ln3_g, m_ln3_b, v_meta_tokens, v_ffn1_w_gate, v_ffn1_w_up, v_ffn1_w_down, v_ln1_g, v_ln1_b, v_w_in, v_w_gate_up, v_b_gate, v_w_pool, v_pool_scale, v_gla_norm_g, v_w_out, v_ln2_g, v_ln2_b, v_ffn2_w_gate, v_ffn2_w_up, v_ffn2_w_down, v_ln3_g, v_ln3_b):
    w = dict(meta_tokens=meta_tokens, ffn1_w_gate=ffn1_w_gate, ffn1_w_up=ffn1_w_up, ffn1_w_down=ffn1_w_down,
             ln1_g=ln1_g, ln1_b=ln1_b, w_in=w_in, w_gate_up=w_gate_up, b_gate=b_gate, w_pool=w_pool,
             pool_scale=pool_scale, gla_norm_g=gla_norm_g, w_out=w_out, ln2_g=ln2_g, ln2_b=ln2_b,
             ffn2_w_gate=ffn2_w_gate, ffn2_w_up=ffn2_w_up, ffn2_w_down=ffn2_w_down, ln3_g=ln3_g, ln3_b=ln3_b)
    mom1 = dict(meta_tokens=m_meta_tokens, ffn1_w_gate=m_ffn1_w_gate, ffn1_w_up=m_ffn1_w_up,
                ffn1_w_down=m_ffn1_w_down, ln1_g=m_ln1_g, ln1_b=m_ln1_b, w_in=m_w_in, w_gate_up=m_w_gate_up,
                b_gate=m_b_gate, w_pool=m_w_pool, pool_scale=m_pool_scale, gla_norm_g=m_gla_norm_g, w_out=m_w_out,
                ln2_g=m_ln2_g, ln2_b=m_ln2_b, ffn2_w_gate=m_ffn2_w_gate, ffn2_w_up=m_ffn2_w_up,
                ffn2_w_down=m_ffn2_w_down, ln3_g=m_ln3_g, ln3_b=m_ln3_b)
    mom2 = dict(meta_tokens=v_meta_tokens, ffn1_w_gate=v_ffn1_w_gate, ffn1_w_up=v_ffn1_w_up,
                ffn1_w_down=v_ffn1_w_down, ln1_g=v_ln1_g, ln1_b=v_ln1_b, w_in=v_w_in, w_gate_up=v_w_gate_up,
                b_gate=v_b_gate, w_pool=v_w_pool, pool_scale=v_pool_scale, gla_norm_g=v_gla_norm_g, w_out=v_w_out,
                ln2_g=v_ln2_g, ln2_b=v_ln2_b, ffn2_w_gate=v_ffn2_w_gate, ffn2_w_up=v_ffn2_w_up,
                ffn2_w_down=v_ffn2_w_down, ln3_g=v_ln3_g, ln3_b=v_ln3_b)

    xs = x[0]
    s_len, d = xs.shape
    nl = ln1_g.shape[0]
    alpha = (2.0 * nl) ** 0.25
    t_real = N_META + s_len
    t_pad = -(-t_real // LANE) * LANE
    pw = pool_scale.shape[1]
    kw = b_gate.shape[1]
    vw = gla_norm_g.shape[1]
    rank = w_gate_up.shape[1]
    widths = (pw, kw, kw, vw, vw)
    n_main = sum(widths)
    dff_s = ffn1_w_gate.shape[2]
    dff_c = N_SHARD * dff_s // FFN_CHUNKS

    me_xy = 2 * lax.axis_index("x") + lax.axis_index("y")
    me_all = 2 * me_xy + lax.axis_index("c")
    ffn1_names = ("ffn1_w_gate", "ffn1_w_up", "ffn1_w_down")
    mix_names = ("w_out", "w_gate_up", "w_in")
    ffn2_names = ("ffn2_w_gate", "ffn2_w_up", "ffn2_w_down")

    gate_up = ("ffn1_w_gate", "ffn1_w_up", "ffn2_w_gate", "ffn2_w_up")

    def stored(n, a):
        if n in gate_up:
            return jnp.swapaxes(a, 1, 2)
        return jnp.transpose(a, (2, 0, 1)) if n == "w_in" else a

    def as_given(n, a):
        if n in gate_up:
            return jnp.swapaxes(a, 1, 2)
        if n == "w_in":
            return jnp.transpose(a.reshape(-1, nl, d), (1, 2, 0))
        return a.reshape(w[n].shape)

    stages = [[("meta_tokens", None)], [(n, 0) for n in ffn1_names], [(n, 0) for n in mix_names + ffn2_names]]
    stages += [[(n, l) for n in BIG] for l in range(1, nl)]
    gathers, wa = {}, {}

    halved = ffn1_names + ffn2_names + ("w_out",)

    def start_gather(si, dep=None):
        own = []
        for n, l in stages[si]:
            a = meta_tokens if l is None else (stored(n, w[n])[:, l] if n == "w_in" else stored(n, w[n])[l])
            a = a if dep is None else a + dep
            own.append(a if l is None else a.astype(BF16))
        gathers[si] = _exchange_start(f"gather_start_{si}", ["gather_half" if n in halved else "gather"
                                                             for n, _ in stages[si]], own,
                                      [_landing(a, me_xy, N_SHARD) for a in own])
        return gathers[si]["token"]

    def arrive(si, after):
        lands, token = _exchange_wait(f"gather_wait_{si}", gathers[si], after)
        if any(kd == "gather_half" for kd in gathers[si]["kinds"]):
            lands = _share_halves(f"gather_share_{si}", gathers[si]["kinds"], lands)
        for item, a in zip(stages[si], lands):
            wa[item] = a.reshape(FFN_CHUNKS, -1, d) if item[0] in ffn1_names + ffn2_names else a
        return token

    def mixer_weights(l):
        wi = wa["w_in", l].reshape(-1, d)
        return dict(w_main=wi[:n_main], w_lr=jnp.pad(wi[n_main:], ((0, LANE - rank), (0, 0))),
                    wgu=jnp.pad(_from_col_shards(wa["w_gate_up", l]), ((0, LANE - rank), (0, 0))),
                    wout=wa["w_out", l].reshape(-1, d))

    wp16 = w_pool.astype(BF16)
    ones = jnp.ones((1, d), F32)
    zeros = jnp.zeros((1, d), F32)
    target = jnp.concatenate([jnp.zeros((N_META, d), F32), loss_target[0], jnp.zeros((t_pad - t_real, d), F32)], axis=0)

    started = start_gather(0)
    for si in range(1, len(stages)):
        started = start_gather(si, started[0:1, 0:1])
    arrive(0, started)
    meta_full = _from_col_shards(wa["meta_tokens", None])
    h0 = jnp.concatenate([meta_full, xs, jnp.zeros((t_pad - t_real, d), F32)], axis=0)
    arrive(1, h0[:8, :LANE] + target[:8, :LANE])

    saved, mw = [], []
    cur, cur_g, cur_b = h0, ones, zeros
    for l in range(nl):
        s = {}
        xh1, rs1, hb0, g1, u1 = _ffn_fwd(cur, cur_g, cur_b, wa["ffn1_w_gate", l], wa["ffn1_w_up", l],
                                         wa["ffn1_w_down", l], alpha, f"ffn1_fwd_{l}")
        if l == 0:
            arrive(2, xh1)
        mw.append(mixer_weights(l))
        up, q, k, v, r, zg, la, hb1 = _inproj_fwd(xh1, ln1_g[l:l + 1], ln1_b[l:l + 1], mw[l]["w_main"], mw[l]["w_lr"],
                                                  mw[l]["wgu"], b_gate[l:l + 1], widths, f"inproj_fwd_{l}")
        yp, pb = _pool_fwd(up, wp16[l], pool_scale[l:l + 1], f"pool_fwd_{l}")
        o, yg, sall = _gla_fwd(q, k, v, la, r, gla_norm_g[l:l + 1], f"gla_fwd_{l}")
        xh2, rs2 = _outproj_fwd(yp, yg, mw[l]["wout"], xh1, ln1_g[l:l + 1], ln1_b[l:l + 1], alpha, f"outproj_fwd_{l}")
        if l + 1 < nl:
            arrive(l + 3, xh2)
        xh3, rs3, hb2, g2, u2 = _ffn_fwd(xh2, ln2_g[l:l + 1], ln2_b[l:l + 1], wa["ffn2_w_gate", l], wa["ffn2_w_up", l],
                                         wa["ffn2_w_down", l], alpha, f"ffn2_fwd_{l}")
        s.update(xh1=xh1, rs1=rs1, hb0=hb0, g1=g1, u1=u1, q=q, k=k, v=v, r=r, zg=zg, la=la, hb1=hb1, yp=yp, pb=pb,
                 o=o, yg=yg, sall=sall, xh2=xh2, rs2=rs2, xh3=xh3, rs3=rs3, hb2=hb2, g2=g2, u2=u2)
        saved.append(s)
        cur, cur_g, cur_b = xh3, ln3_g[l:l + 1], ln3_b[l:l + 1]

    dh, loss_acc = _loss_head(cur, cur_g, cur_b, target, s_len, "loss_head")
    loss = lax.psum(loss_acc[0, 0], ("x", "y", "c"))

    small_grads = {n: [None] * nl for n in SMALL}
    scatters = []

    def depart(name, items, grads, kinds=None):
        lands = [_landing(g if kd == "bcast" else lax.dynamic_index_in_dim(g, me_xy, 0, keepdims=False),
                          me_all if kd == "bcast" else me_xy, N_DEV if kd == "bcast" else N_SHARD)
                 for g, kd in zip(grads, kinds or ["scatter"] * len(grads))]
        st = _exchange_start(name, kinds or ["scatter"] * len(grads), grads, lands)
        scatters.append((name, items, st))
        return st["token"]

    def pack(parts):
        flat = jnp.concatenate([parts[n].reshape(-1) for n in SMALL])
        return flat.reshape(-1, LANE)

    def ffn_wgrads(l, names, hb, dgb, dub, act, dfb):
        tag = names[0][:4]
        return [_wgrad(dgb, hb, dff_c, d, f"{tag}_dwg_{l}").reshape(N_SHARD, dff_s, d),
                _wgrad(dub, hb, dff_c, d, f"{tag}_dwu_{l}").reshape(N_SHARD, dff_s, d),
                _wgrad(act, dfb, dff_c, d, f"{tag}_dwd_{l}").reshape(N_SHARD, dff_s, d)]

    late = []
    for l in reversed(range(nl)):
        s = saved[l]
        dh, dfb, dgb, dub, act, dgam, dbet = _ffn_bwd(dh, s["xh3"], s["rs3"], ln3_g[l:l + 1], s["g2"], s["u2"],
                                                      wa["ffn2_w_gate", l], wa["ffn2_w_up", l], wa["ffn2_w_down", l],
                                                      alpha, f"ffn2_bwd_{l}")
        small_grads["ln3_g"][l], small_grads["ln3_b"][l] = dgam, dbet
        gone = depart(f"scatter_start_ffn2_{l}", [(n, l) for n in ffn2_names],
                      ffn_wgrads(l, ffn2_names, s["hb2"], dgb, dub, act, dfb))

        dyb, dyp, dyg, dres, dgam, dbet = _outproj_bwd(dh, s["xh2"], s["rs2"], ln2_g[l:l + 1] + gone[0:1, 0:1],
                                                       mw[l]["wout"], pw, alpha, f"outproj_bwd_{l}")
        small_grads["ln2_g"][l], small_grads["ln2_b"][l] = dgam, dbet
        dwo = jnp.concatenate([_wgrad(s["yp"], dyb, pw, d, f"dwout_pool_{l}"),
                               _wgrad(s["yg"], dyb, vw, d, f"dwout_gla_{l}")], axis=0)
        dq, dk, dv, dr, dzg, dwgu, dbg, dgn = _gla_bwd(dyg, s["o"], s["r"], gla_norm_g[l:l + 1], s["q"], s["k"],
                                                       s["v"], s["la"], s["zg"], s["sall"], mw[l]["wgu"],
                                                       f"gla_bwd_{l}")
        dup, dwp, dsc = _pool_bwd(dyp, s["pb"], wp16[l], pool_scale[l:l + 1], f"pool_bwd_{l}")
        small_grads["b_gate"][l], small_grads["gla_norm_g"][l] = dbg, dgn
        small_grads["w_pool"][l], small_grads["pool_scale"][l] = dwp, dsc
        dh, dz = _inproj_bwd(dres, [dup, dq, dk, dv, dr], dzg, mw[l]["w_main"], mw[l]["w_lr"], f"inproj_bwd_{l}")
        dwi = jnp.concatenate([_wgrad(dz, s["hb1"], 4 * LANE, d, f"dwin_main_{l}"),
                               _wgrad(dzg, s["hb1"], LANE, d, f"dwin_lr_{l}")[:rank]], axis=0)
        gone = depart(f"scatter_start_mix_{l}", [(n, l) for n in mix_names],
                      [dwo.reshape(N_SHARD, -1, d), _col_shards(dwgu[:rank].astype(BF16)),
                       dwi.reshape(N_SHARD, -1, d)])

        dh, dfb, dgb, dub, act, dgam, dbet = _ffn_bwd(dh, s["xh1"], s["rs1"], ln1_g[l:l + 1] + gone[0:1, 0:1],
                                                      s["g1"], s["u1"], wa["ffn1_w_gate", l], wa["ffn1_w_up", l],
                                                      wa["ffn1_w_down", l], alpha, f"ffn1_bwd_{l}")
        small_grads["ln1_g"][l], small_grads["ln1_b"][l] = dgam, dbet
        if l:
            gone = depart(f"scatter_start_ffn1_{l}", [(n, l) for n in ffn1_names],
                          ffn_wgrads(l, ffn1_names, s["hb0"], dgb, dub, act, dfb))
            ln3_g = ln3_g.at[l - 1:l].add(gone[0:1, 0:1])
            continue
        grad_x = dh[N_META:t_real][None]
        small_vec = pack({n: jnp.stack(small_grads[n]) for n in SMALL})
        gone = depart("scatter_start_rest", [("meta_tokens", 0), ("small", 0)],
                      [_col_shards(dh[:N_META].astype(BF16)), small_vec], ["scatter", "bcast"])
        for n, a, b in zip(ffn1_names, (dgb, dub, act), (s["hb0"], s["hb0"], dfb)):
            g = _wgrad(a, b, dff_c, d, f"{n}_grad_{l}", after=gone).reshape(N_SHARD, dff_s, d)
            gone = depart(f"scatter_start_{n}", [(n, l)], [g])
            late.append(scatters.pop())

    recv, results, firsts = {}, {}, []

    def collect(group, after):
        for name, items, st in group:
            for item, a in zip(items, _exchange_wait(name.replace("start", "wait"), st, after)[0]):
                recv[item] = a

    def reduce_and_update(names, tag):
        partial = [_sum_slots([recv[n, l] for l in range(1 if n == "meta_tokens" else nl)], f"sum_{n}", n == "w_in")
                   for n in names]
        for n, mine, theirs in zip(names, partial, _swap_sibling(partial, f"swap_sibling_{tag}")):
            fit = lambda a: stored(n, a).reshape(mine.shape)
            outs = _adamw(fit(w[n]), [mine, theirs], fit(mom1[n]), fit(mom2[n]), f"adamw_{n}")
            results[n] = [as_given(n, o) for o in outs]
            firsts.append(outs[1][0, 0, 0])

    collect(scatters, gone)
    early = [n for n in ("meta_tokens",) + BIG if n not in ffn1_names]
    reduce_and_update(early, "early")
    small_terms = [recv["small", 0][i][None] for i in range(N_DEV)]
    souts = _adamw(pack(w)[None], small_terms, pack(mom1)[None], pack(mom2)[None], "adamw_small")
    off = 0
    for n in SMALL:
        size = w[n].size
        results[n] = [o.reshape(-1)[off:off + size].reshape(w[n].shape) for o in souts]
        off += size
    collect(late, souts[0][0, :8] + functools.reduce(jnp.add, firsts))
    reduce_and_update(ffn1_names, "late")

    out = [loss, grad_x]
    for part in range(4):
        out += [results[n][part] for n in WEIGHTS]
    return tuple(out)
```

```python
import functools

import jax
import jax.numpy as jnp
from jax import lax
from jax.experimental import pallas as pl
from jax.experimental.pallas import tpu as pltpu

F32 = jnp.float32
BF16 = jnp.bfloat16
MESH = pl.DeviceIdType.MESH

N_META = 16
POOL_WINDOWS = (2, 4, 8, 16)
POOL_HALO = 16
N_HEADS = 4
GLA_GATE_TEMP = 16.0
CHUNK = 128
CHUNK_UNROLL = 5
LN_EPS = 1e-5
RMS_EPS = 1e-6
ADAM_LR = 0.001
ADAM_B1 = 0.9
ADAM_B2 = 0.999
ADAM_EPS = 1e-08
ADAM_WD = 0.01
ADAM_STEP = 10
LANE = 128
BF16_ROWS = 16
ROW_TILE = 640
FFN_ROW_TILE = 640
FFN_CHUNKS = 4
ROW_GROUPS = 2
FFN_SPLIT = 2
WGRAD_K_MAX = 2176
N_SHARD = 4
N_DEV = 8

BIG = ("ffn1_w_gate", "ffn1_w_up", "ffn1_w_down", "w_in", "w_gate_up", "w_out",
       "ffn2_w_gate", "ffn2_w_up", "ffn2_w_down")
SMALL = ("ln1_g", "ln1_b", "b_gate", "w_pool", "pool_scale", "gla_norm_g", "ln2_g", "ln2_b", "ln3_g", "ln3_b")
WEIGHTS = ("meta_tokens", "ffn1_w_gate", "ffn1_w_up", "ffn1_w_down", "ln1_g", "ln1_b", "w_in", "w_gate_up",
           "b_gate", "w_pool", "pool_scale", "gla_norm_g", "w_out", "ln2_g", "ln2_b", "ffn2_w_gate",
           "ffn2_w_up", "ffn2_w_down", "ln3_g", "ln3_b")


def _tc_call(body, **kw):
    return pl.pallas_call(body, **kw)


def _comm_call(body, **kw):
    return pl.pallas_call(body, **kw)


def _seq(n):
    return pltpu.CompilerParams(dimension_semantics=("arbitrary",) * n)


def _mm(a, b):
    return jnp.dot(a.astype(BF16), b.astype(BF16), preferred_element_type=F32)


def _mm_nt(a, b):
    return lax.dot_general(a.astype(BF16), b.astype(BF16), (((1,), (1,)), ((), ())), preferred_element_type=F32)


def _mm_tn(a, b):
    return lax.dot_general(a.astype(BF16), b.astype(BF16), (((0,), (0,)), ((), ())), preferred_element_type=F32)


def _mm_f32(a, b):
    return jnp.dot(a, b, precision=lax.Precision.HIGHEST, preferred_element_type=F32)


def _row_tile(t, most=None):
    tm = min(most or ROW_TILE, t)
    while t % tm:
        tm -= LANE
    return tm


def _silu_parts(g):
    sg = jax.nn.sigmoid(g)
    return sg, g * sg


def _ln_stats(r):
    mu = jnp.mean(r, axis=-1, keepdims=True)
    rc = r - mu
    var = jnp.mean(rc * rc, axis=-1, keepdims=True)
    rs = lax.rsqrt(var + LN_EPS)
    return rc * rs, rs


def _ln_bwd(dy, xh, rs, gam):
    dyg = dy * gam
    c1 = jnp.mean(dyg, axis=-1, keepdims=True)
    c2 = jnp.mean(dyg * xh, axis=-1, keepdims=True)
    return rs * (dyg - c1 - xh * c2)


def _ffn_fwd(xin, gam_in, bet_in, wg, wu, wd, alpha, name):
    t, d = xin.shape
    nj, tf, _ = wg.shape
    tm = _row_tile(t, FFN_ROW_TILE)

    def body(x_ref, gi_ref, bi_ref, wg_ref, wu_ref, wd_ref, xhat_ref, rstd_ref, hb_ref, go_ref, uo_ref, acc, hbs):
        j = pl.program_id(1)

        @pl.when(j == 0)
        def _():
            hb = (x_ref[...] * gi_ref[...] + bi_ref[...]).astype(BF16)
            hbs[...] = hb
            hb_ref[...] = hb
            acc[...] = jnp.zeros_like(acc)

        hb = hbs[...]
        g = _mm_nt(hb, wg_ref[...])
        u = _mm_nt(hb, wu_ref[...])
        _, sl = _silu_parts(g)
        go_ref[...] = g.astype(BF16)
        uo_ref[...] = u.astype(BF16)
        acc[...] += jnp.dot((sl * u).astype(BF16), wd_ref[...], preferred_element_type=F32)

        @pl.when(j == nj - 1)
        def _():
            h = x_ref[...] * gi_ref[...] + bi_ref[...]
            xhat, rs = _ln_stats(alpha * h + 0.5 * acc[...])
            xhat_ref[...] = xhat
            rstd_ref[...] = rs

    row = lambda i, j: (i, 0)
    vec = pl.BlockSpec((1, d), lambda i, j: (0, 0))
    return _tc_call(
        body, name=name, grid=(t // tm, nj),
        in_specs=[pl.BlockSpec((tm, d), row), vec, vec] + [pl.BlockSpec((None, tf, d), lambda i, j: (j, 0, 0))] * 3,
        out_specs=[pl.BlockSpec((tm, d), row), pl.BlockSpec((tm, 1), row), pl.BlockSpec((tm, d), row),
                   pl.BlockSpec((None, tm, tf), lambda i, j: (j, i, 0)),
                   pl.BlockSpec((None, tm, tf), lambda i, j: (j, i, 0))],
        out_shape=[jax.ShapeDtypeStruct((t, d), F32), jax.ShapeDtypeStruct((t, 1), F32),
                   jax.ShapeDtypeStruct((t, d), BF16), jax.ShapeDtypeStruct((nj, t, tf), BF16),
                   jax.ShapeDtypeStruct((nj, t, tf), BF16)],
        scratch_shapes=[pltpu.VMEM((tm, d), F32), pltpu.VMEM((tm, d), BF16)],
        compiler_params=_seq(2),
    )(xin, gam_in, bet_in, wg, wu, wd)


def _ffn_bwd(dh, xhat, rstd, ln_g, gb, ub, wg, wu, wd, alpha, name):
    t, d = dh.shape
    nj, tf, _ = wg.shape
    tm = _row_tile(t, FFN_ROW_TILE)

    def body(dh_ref, xh_ref, rs_ref, g_ref, gb_ref, ub_ref, wg_ref, wu_ref, wd_ref,
             dhin_ref, df_ref, dg_ref, du_ref, act_ref, dgam_ref, dbet_ref, dr_s, df_s, acc):
        i = pl.program_id(0)
        j = pl.program_id(1)

        @pl.when(j == 0)
        def _():
            dy = dh_ref[...]
            xh = xh_ref[...]
            dr = _ln_bwd(dy, xh, rs_ref[...], g_ref[...])
            dr_s[...] = dr
            dfb = (0.5 * dr).astype(BF16)
            df_s[...] = dfb
            df_ref[...] = dfb
            acc[...] = jnp.zeros_like(acc)

            @pl.when(i == 0)
            def _():
                dgam_ref[...] = jnp.zeros_like(dgam_ref)
                dbet_ref[...] = jnp.zeros_like(dbet_ref)

            dgam_ref[...] += jnp.sum(dy * xh, axis=0, keepdims=True)
            dbet_ref[...] += jnp.sum(dy, axis=0, keepdims=True)

        for part in range(FFN_SPLIT):
            rows = pl.ds(part * (tm // FFN_SPLIT), tm // FFN_SPLIT)
            dact = _mm_nt(df_s[rows, :], wd_ref[...])
            g = gb_ref[rows, :].astype(F32)
            u = ub_ref[rows, :].astype(F32)
            sg, sl = _silu_parts(g)
            dg = (dact * u * (sg * (1.0 + g * (1.0 - sg)))).astype(BF16)
            du = (dact * sl).astype(BF16)
            dg_ref[rows, :] = dg
            du_ref[rows, :] = du
            act_ref[rows, :] = (sl * u).astype(BF16)
            acc[rows, :] += _mm(dg, wg_ref[...]) + _mm(du, wu_ref[...])

        @pl.when(j == nj - 1)
        def _():
            dhin_ref[...] = alpha * dr_s[...] + acc[...]

    row = lambda i, j: (i, 0)
    col = pl.BlockSpec((None, tm, tf), lambda i, j: (j, i, 0))
    vec = pl.BlockSpec((1, d), lambda i, j: (0, 0))
    ff = jax.ShapeDtypeStruct((nj, t, tf), BF16)
    return _tc_call(
        body, name=name, grid=(t // tm, nj),
        in_specs=[pl.BlockSpec((tm, d), row), pl.BlockSpec((tm, d), row), pl.BlockSpec((tm, 1), row), vec,
                  col, col] + [pl.BlockSpec((None, tf, d), lambda i, j: (j, 0, 0))] * 3,
        out_specs=[pl.BlockSpec((tm, d), row), pl.BlockSpec((tm, d), row),
                   col, col, col, vec, vec],
        out_shape=[jax.ShapeDtypeStruct((t, d), F32), jax.ShapeDtypeStruct((t, d), BF16), ff, ff, ff,
                   jax.ShapeDtypeStruct((1, d), F32), jax.ShapeDtypeStruct((1, d), F32)],
        scratch_shapes=[pltpu.VMEM((tm, d), F32), pltpu.VMEM((tm, d), BF16), pltpu.VMEM((tm, d), F32)],
        compiler_params=_seq(2),
    )(dh, xhat, rstd, ln_g, gb, ub, wg, wu, wd)


def _wgrad(a, b, tmm, tn, name, after=None):
    t = a.shape[-2]
    m = a.shape[-1] * (a.shape[0] if a.ndim == 3 else 1)
    n = b.shape[-1] * (b.shape[0] if b.ndim == 3 else 1)
    tk = max(k for k in range(BF16_ROWS, WGRAD_K_MAX + 1, BF16_ROWS) if t % k == 0)
    nk = t // tk
    extra = [] if after is None else [after]

    def body(a_ref, b_ref, *rest):
        o_ref, acc = rest[len(extra):]
        k = pl.program_id(2)

        @pl.when(k == 0)
        def _():
            acc[...] = jnp.zeros_like(acc)

        acc[...] += _mm_tn(a_ref[...], b_ref[...])

        @pl.when(k == nk - 1)
        def _():
            o_ref[...] = acc[...].astype(o_ref.dtype)

    return _tc_call(
        body, name=name, grid=(m // tmm, n // tn, nk),
        in_specs=[pl.BlockSpec((None, tk, tmm), lambda i, j, k: (i, k, 0)) if a.ndim == 3
                  else pl.BlockSpec((tk, tmm), lambda i, j, k: (k, i)),
                  pl.BlockSpec((None, tk, tn), lambda i, j, k: (j, k, 0)) if b.ndim == 3
                  else pl.BlockSpec((tk, tn), lambda i, j, k: (k, j))] + [pl.BlockSpec(memory_space=pl.ANY)] * len(extra),
        out_specs=pl.BlockSpec((tmm, tn), lambda i, j, k: (i, j)),
        out_shape=jax.ShapeDtypeStruct((m, n), BF16),
        scratch_shapes=[pltpu.VMEM((tmm, tn), F32)],
        compiler_params=_seq(3),
    )(a, b, *extra)


def _inproj_fwd(xhat, gam, bet, w_main, w_lr, wgu, b_gate, widths, name):
    t, d = xhat.shape
    tm = _row_tile(t)
    kw = wgu.shape[1]
    offs = [0]
    for w in widths:
        offs.append(offs[-1] + w)

    def body(x_ref, g_ref, b_ref, wm_ref, wl_ref, wgu_ref, bg_ref, *outs):
        piece_refs, (zg_ref, la_ref, hb_ref) = outs[:len(widths)], outs[len(widths):]
        hb = (x_ref[...] * g_ref[...] + b_ref[...]).astype(BF16)
        hb_ref[...] = hb
        for p, ref in enumerate(piece_refs):
            ref[...] = _mm_nt(hb, wm_ref[offs[p]:offs[p + 1], :])
        zg = _mm_nt(hb, wl_ref[...])
        zg_ref[...] = zg
        logit = _mm(zg, wgu_ref[...]) + bg_ref[...]
        la_ref[...] = (jnp.minimum(logit, 0.0) - jnp.log(1.0 + jnp.exp(-jnp.abs(logit)))) * (1.0 / GLA_GATE_TEMP)

    row = lambda i: (i, 0)
    full = lambda a: pl.BlockSpec(a.shape, lambda i: (0,) * a.ndim)
    out_w = list(widths) + [LANE, kw]
    return _tc_call(
        body, name=name, grid=(t // tm,),
        in_specs=[pl.BlockSpec((tm, d), row), full(gam), full(bet), full(w_main), full(w_lr), full(wgu), full(b_gate)],
        out_specs=[pl.BlockSpec((tm, w), row) for w in out_w] + [pl.BlockSpec((tm, d), row)],
        out_shape=[jax.ShapeDtypeStruct((t, w), F32) for w in out_w] + [jax.ShapeDtypeStruct((t, d), BF16)],
        compiler_params=_seq(1),
    )(xhat, gam, bet, w_main, w_lr, wgu, b_gate)


def _inproj_bwd(dh_part, pieces, dzg, w_main, w_lr, name):
    t, d = dh_part.shape
    tm = _row_tile(t)
    widths = [p.shape[1] for p in pieces]
    offs = [0]
    for w in widths:
        offs.append(offs[-1] + w)

    def body(*refs):
        dhp_ref = refs[0]
        p_refs = refs[1:1 + len(widths)]
        dzg_ref, wm_ref, wl_ref, dh_ref, dz_ref = refs[1 + len(widths):]
        acc = dhp_ref[...] + _mm(dzg_ref[...], wl_ref[...])
        for p, ref in enumerate(p_refs):
            v = ref[...]
            dz_ref[:, offs[p]:offs[p + 1]] = v
            acc += _mm(v, wm_ref[offs[p]:offs[p + 1], :])
        dh_ref[...] = acc

    row = lambda i: (i, 0)
    full = lambda a: pl.BlockSpec(a.shape, lambda i: (0,) * a.ndim)
    return _tc_call(
        body, name=name, grid=(t // tm,),
        in_specs=[pl.BlockSpec((tm, d), row)] + [pl.BlockSpec((tm, w), row) for w in widths]
                 + [pl.BlockSpec((tm, LANE), row), full(w_main), full(w_lr)],
        out_specs=[pl.BlockSpec((tm, d), row), pl.BlockSpec((tm, offs[-1]), row)],
        out_shape=[jax.ShapeDtypeStruct((t, d), F32), jax.ShapeDtypeStruct((t, offs[-1]), BF16)],
        compiler_params=_seq(1),
    )(dh_part, *pieces, dzg, w_main, w_lr)


def _pool_cnt(tile, tm, w):
    t = tile * tm + lax.broadcasted_iota(jnp.int32, (tm, 1), 0)
    return jnp.minimum(t + 1, w).astype(F32)


def _pool_fwd(u, wp, scale, name):
    t, pw = u.shape
    tm = _row_tile(t)
    gd = wp.shape[1]

    def body(u_ref, wp_ref, sc_ref, y_ref, p_ref, ext):
        i = pl.program_id(0)

        @pl.when(i == 0)
        def _():
            ext[0:POOL_HALO, :] = jnp.zeros((POOL_HALO, pw), F32)

        ext[POOL_HALO:POOL_HALO + tm, :] = u_ref[...]
        for gi, w in enumerate(POOL_WINDOWS):
            cols = slice(gi * gd, (gi + 1) * gd)
            s = ext[pl.ds(POOL_HALO, tm), cols]
            tot = s
            for back in range(1, w):
                tot = tot + ext[pl.ds(POOL_HALO - back, tm), cols]
            p = (tot / _pool_cnt(i, tm, w) - s).astype(BF16)
            p_ref[:, cols] = p
            y_ref[:, cols] = (jnp.dot(p, wp_ref[gi], preferred_element_type=F32) * sc_ref[:, cols]).astype(BF16)
        ext[0:POOL_HALO, :] = ext[tm:tm + POOL_HALO, :]

    row = lambda i: (i, 0)
    return _tc_call(
        body, name=name, grid=(t // tm,),
        in_specs=[pl.BlockSpec((tm, pw), row), pl.BlockSpec(wp.shape, lambda i: (0, 0, 0)),
                  pl.BlockSpec((1, pw), lambda i: (0, 0))],
        out_specs=[pl.BlockSpec((tm, pw), row), pl.BlockSpec((tm, pw), row)],
        out_shape=[jax.ShapeDtypeStruct((t, pw), BF16), jax.ShapeDtypeStruct((t, pw), BF16)],
        scratch_shapes=[pltpu.VMEM((tm + POOL_HALO, pw), F32)],
        compiler_params=_seq(1),
    )(u, wp, scale)


def _pool_bwd(dy, pb, wp, scale, name):
    t, pw = dy.shape
    tm = _row_tile(t)
    nt = t // tm
    gd = wp.shape[1]

    def body(dy_ref, p_ref, wp_ref, sc_ref, du_ref, dwp_ref, dsc_ref, ext):
        i = pl.program_id(0)
        tile = nt - 1 - i

        @pl.when(i == 0)
        def _():
            ext[tm:tm + POOL_HALO, :] = jnp.zeros((POOL_HALO, pw), F32)
            dwp_ref[...] = jnp.zeros_like(dwp_ref)
            dsc_ref[...] = jnp.zeros_like(dsc_ref)

        dps = []
        for gi, w in enumerate(POOL_WINDOWS):
            cols = slice(gi * gd, (gi + 1) * gd)
            dyv = dy_ref[:, cols]
            p = p_ref[:, cols]
            dpre = (dyv * sc_ref[:, cols]).astype(BF16)
            dsc_ref[:, cols] += jnp.sum(dyv * jnp.dot(p, wp_ref[gi], preferred_element_type=F32), axis=0, keepdims=True)
            dwp_ref[gi] += _mm_tn(p, dpre)
            dp = _mm_nt(dpre, wp_ref[gi])
            dps.append(dp)
            ext[0:tm, cols] = dp / _pool_cnt(tile, tm, w)
        for gi, w in enumerate(POOL_WINDOWS):
            cols = slice(gi * gd, (gi + 1) * gd)
            tot = ext[pl.ds(0, tm), cols]
            for fwd in range(1, w):
                tot = tot + ext[pl.ds(fwd, tm), cols]
            du_ref[:, cols] = (tot - dps[gi]).astype(BF16)
        ext[tm:tm + POOL_HALO, :] = ext[0:POOL_HALO, :]

    row = lambda i: (nt - 1 - i, 0)
    return _tc_call(
        body, name=name, grid=(nt,),
        in_specs=[pl.BlockSpec((tm, pw), row), pl.BlockSpec((tm, pw), row),
                  pl.BlockSpec(wp.shape, lambda i: (0, 0, 0)), pl.BlockSpec((1, pw), lambda i: (0, 0))],
        out_specs=[pl.BlockSpec((tm, pw), row), pl.BlockSpec(wp.shape, lambda i: (0, 0, 0)),
                   pl.BlockSpec((1, pw), lambda i: (0, 0))],
        out_shape=[jax.ShapeDtypeStruct((t, pw), BF16), jax.ShapeDtypeStruct(wp.shape, F32),
                   jax.ShapeDtypeStruct((1, pw), F32)],
        scratch_shapes=[pltpu.VMEM((tm + POOL_HALO, pw), F32)],
        compiler_params=_seq(1),
    )(dy, pb, wp, scale)


def _gla_masks(kw, vw):
    dk, dv = kw // N_HEADS, vw // N_HEADS
    lane_k = lax.broadcasted_iota(jnp.int32, (1, kw), 1)
    lane_v = lax.broadcasted_iota(jnp.int32, (1, vw), 1)
    hk = [((lane_k >= h * dk) & (lane_k < (h + 1) * dk)).astype(F32) for h in range(N_HEADS)]
    hv = [((lane_v >= h * dv) & (lane_v < (h + 1) * dv)).astype(F32) for h in range(N_HEADS)]
    r = lax.broadcasted_iota(jnp.int32, (CHUNK, CHUNK), 0)
    c = lax.broadcasted_iota(jnp.int32, (CHUNK, CHUNK), 1)
    tril = r >= c
    rs = lax.broadcasted_iota(jnp.int32, (N_HEADS * CHUNK, CHUNK), 0) & (CHUNK - 1)
    stril = rs >= lax.broadcasted_iota(jnp.int32, (N_HEADS * CHUNK, CHUNK), 1)
    return hk, hv, tril, stril


def _block_diag(x, hk, dv):
    return jnp.concatenate([x[h * dv:(h + 1) * dv, :] * hk[h] for h in range(N_HEADS)], axis=0)


def _gla_fwd(q, k, v, loga, r, gnorm, name):
    t, kw = q.shape
    vw = v.shape[1]
    dk, dv = kw // N_HEADS, vw // N_HEADS
    tm = _row_tile(t)
    nc = tm // CHUNK
    qscale = dk ** -0.5

    def body(q_ref, k_ref, v_ref, la_ref, r_ref, gn_ref, o_ref, y_ref, sall_ref, st):
        @pl.when(pl.program_id(0) == 0)
        def _():
            st[...] = jnp.zeros_like(st)

        hk, hv, tril, stril = _gla_masks(kw, vw)
        trif = tril.astype(F32)

        def chunk(c, carry):
            rows = pl.ds(pl.multiple_of(c * CHUNK, CHUNK), CHUNK)
            la = la_ref[rows, :]
            b = _mm_f32(trif, la)
            bl = jnp.sum(la, axis=0, keepdims=True)
            qb = q_ref[rows, :] * (qscale * jnp.exp(b))
            kk = k_ref[rows, :]
            kb = kk * jnp.exp(-b)
            kl = kk * jnp.exp(bl - b)
            vv = v_ref[rows, :]
            s_t = st[...]
            compact = s_t[0:dv, :]
            for h in range(1, N_HEADS):
                compact = compact + s_t[h * dv:(h + 1) * dv, :]
            sall_ref[c] = compact
            qx = jnp.concatenate([qb * hk[h] for h in range(N_HEADS)], axis=0)
            a = jnp.where(stril, _mm_nt(qx, kb), 0.0).astype(BF16)
            o_inter = _mm_nt(qb, s_t)
            for h in range(N_HEADS):
                vs = slice(h * dv, (h + 1) * dv)
                o_ref[rows, vs] = o_inter[:, vs] + _mm(a[h * CHUNK:(h + 1) * CHUNK, :], vv[:, vs])
            st[...] = s_t * jnp.exp(bl) + _block_diag(_mm_tn(vv, kl), hk, dv)
            return carry

        lax.fori_loop(0, nc, chunk, 0, unroll=CHUNK_UNROLL)
        for h in range(N_HEADS):
            vs = slice(h * dv, (h + 1) * dv)
            oh = o_ref[:, vs]
            on = oh * lax.rsqrt(jnp.mean(oh * oh, axis=-1, keepdims=True) + RMS_EPS)
            _, sl = _silu_parts(r_ref[:, vs])
            y_ref[:, vs] = (on * gn_ref[:, vs] * sl).astype(BF16)

    row = lambda i: (i, 0)
    return _tc_call(
        body, name=name, grid=(t // tm,),
        in_specs=[pl.BlockSpec((tm, kw), row), pl.BlockSpec((tm, kw), row), pl.BlockSpec((tm, vw), row),
                  pl.BlockSpec((tm, kw), row), pl.BlockSpec((tm, vw), row), pl.BlockSpec((1, vw), lambda i: (0, 0))],
        out_specs=[pl.BlockSpec((tm, vw), row), pl.BlockSpec((tm, vw), row),
                   pl.BlockSpec((nc, dv, kw), lambda i: (i, 0, 0))],
        out_shape=[jax.ShapeDtypeStruct((t, vw), F32), jax.ShapeDtypeStruct((t, vw), BF16),
                   jax.ShapeDtypeStruct((t // CHUNK, dv, kw), F32)],
        scratch_shapes=[pltpu.VMEM((vw, kw), F32)],
        compiler_params=_seq(1),
    )(q, k, v, loga, r, gnorm)


def _gla_bwd(dy, o, r, gnorm, q, k, v, loga, zg, sall, wgu, name):
    t, kw = q.shape
    vw = v.shape[1]
    dk, dv = kw // N_HEADS, vw // N_HEADS
    tm = _row_tile(t)
    nt = t // tm
    nc = tm // CHUNK
    qscale = dk ** -0.5

    def body(dy_ref, o_ref, r_ref, gn_ref, q_ref, k_ref, v_ref, la_ref, zg_ref, sall_ref, wgu_ref,
             dq_ref, dk_ref, dv_ref, dr_ref, dzg_ref, dwgu_ref, dbg_ref, dgn_ref, dst, do_s):
        @pl.when(pl.program_id(0) == 0)
        def _():
            dst[...] = jnp.zeros_like(dst)
            dwgu_ref[...] = jnp.zeros_like(dwgu_ref)
            dbg_ref[...] = jnp.zeros_like(dbg_ref)
            dgn_ref[...] = jnp.zeros_like(dgn_ref)

        for h in range(N_HEADS):
            vs = slice(h * dv, (h + 1) * dv)
            oh = o_ref[:, vs]
            rinv = lax.rsqrt(jnp.mean(oh * oh, axis=-1, keepdims=True) + RMS_EPS)
            on = oh * rinv
            rr = r_ref[:, vs]
            sg, sl = _silu_parts(rr)
            dyv = dy_ref[:, vs]
            gn = gn_ref[:, vs]
            dgn_ref[:, vs] += jnp.sum(dyv * on * sl, axis=0, keepdims=True)
            dr_ref[:, vs] = (dyv * on * gn * (sg * (1.0 + rr * (1.0 - sg)))).astype(BF16)
            don = dyv * gn * sl
            do_s[:, vs] = rinv * (don - on * jnp.mean(don * on, axis=-1, keepdims=True))

        hk, hv, tril, stril = _gla_masks(kw, vw)
        trif = tril.astype(F32)
        triuf = (lax.broadcasted_iota(jnp.int32, (CHUNK, CHUNK), 0)
                 <= lax.broadcasted_iota(jnp.int32, (CHUNK, CHUNK), 1)).astype(F32)
        last_row = lax.broadcasted_iota(jnp.int32, (CHUNK, 1), 0) == CHUNK - 1

        def chunk(idx, carry):
            c = nc - 1 - idx
            rows = pl.ds(pl.multiple_of(c * CHUNK, CHUNK), CHUNK)
            la = la_ref[rows, :]
            b = _mm_f32(trif, la)
            bl = jnp.sum(la, axis=0, keepdims=True)
            eb = jnp.exp(b)
            enb = jnp.exp(-b)
            ebl = jnp.exp(bl - b)
            el = jnp.exp(bl)
            qb = q_ref[rows, :] * (qscale * eb)
            kk = k_ref[rows, :]
            kb = kk * enb
            kl = kk * ebl
            vv = v_ref[rows, :]
            do = do_s[rows, :]
            compact = sall_ref[c]
            s_t = jnp.concatenate([compact * hk[h] for h in range(N_HEADS)], axis=0)
            ds_t = dst[...]
            qx = jnp.concatenate([qb * hk[h] for h in range(N_HEADS)], axis=0)
            dox = jnp.concatenate([do * hv[h] for h in range(N_HEADS)], axis=0)
            a = jnp.where(stril, _mm_nt(qx, kb), 0.0).astype(BF16)
            da = jnp.where(stril, _mm_nt(dox, vv), 0.0).astype(BF16)
            dv_ref[rows, :] = (_mm_tn(a, dox) + _mm_nt(kl, ds_t)).astype(BF16)
            dak = _mm(da, kb)
            dqb = _mm(do, s_t)
            for h in range(N_HEADS):
                dqb = dqb + dak[h * CHUNK:(h + 1) * CHUNK, :] * hk[h]
            dkb = _mm_tn(da, qx)
            dkl = _mm(vv, ds_t)
            dbl = jnp.sum(dkl * kl, axis=0, keepdims=True) + el * jnp.sum(ds_t * s_t, axis=0, keepdims=True)
            dst[...] = ds_t * el + _block_diag(_mm_tn(do, qb), hk, dv)
            dq_ref[rows, :] = (dqb * (qscale * eb)).astype(BF16)
            dk_ref[rows, :] = (dkb * enb + dkl * ebl).astype(BF16)
            db = dqb * qb - dkb * kb - dkl * kl + jnp.where(last_row, dbl, 0.0)
            dla = _mm_f32(triuf, db)
            dlogit = dla * (1.0 / GLA_GATE_TEMP) * (1.0 - jnp.exp(GLA_GATE_TEMP * la))
            dzg_ref[rows, :] = _mm_nt(dlogit, wgu_ref[...]).astype(BF16)
            dwgu_ref[...] += _mm_tn(zg_ref[rows, :], dlogit)
            dbg_ref[...] += jnp.sum(dlogit, axis=0, keepdims=True)
            return carry

        lax.fori_loop(0, nc, chunk, 0, unroll=CHUNK_UNROLL)

    row = lambda i: (nt - 1 - i, 0)
    const = lambda i: (0, 0)
    return _tc_call(
        body, name=name, grid=(nt,),
        in_specs=[pl.BlockSpec((tm, vw), row), pl.BlockSpec((tm, vw), row), pl.BlockSpec((tm, vw), row),
                  pl.BlockSpec((1, vw), const), pl.BlockSpec((tm, kw), row), pl.BlockSpec((tm, kw), row),
                  pl.BlockSpec((tm, vw), row), pl.BlockSpec((tm, kw), row), pl.BlockSpec((tm, LANE), row),
                  pl.BlockSpec((nc, dv, kw), lambda i: (nt - 1 - i, 0, 0)), pl.BlockSpec((LANE, kw), const)],
        out_specs=[pl.BlockSpec((tm, kw), row), pl.BlockSpec((tm, kw), row), pl.BlockSpec((tm, vw), row),
                   pl.BlockSpec((tm, vw), row), pl.BlockSpec((tm, LANE), row), pl.BlockSpec((LANE, kw), const),
                   pl.BlockSpec((1, kw), const), pl.BlockSpec((1, vw), const)],
        out_shape=[jax.ShapeDtypeStruct((t, kw), BF16), jax.ShapeDtypeStruct((t, kw), BF16),
                   jax.ShapeDtypeStruct((t, vw), BF16), jax.ShapeDtypeStruct((t, vw), BF16),
                   jax.ShapeDtypeStruct((t, LANE), BF16), jax.ShapeDtypeStruct((LANE, kw), F32),
                   jax.ShapeDtypeStruct((1, kw), F32), jax.ShapeDtypeStruct((1, vw), F32)],
        scratch_shapes=[pltpu.VMEM((vw, kw), F32), pltpu.VMEM((tm, vw), F32)],
        compiler_params=_seq(1),
    )(dy, o, r, gnorm, q, k, v, loga, zg, sall, wgu)


def _outproj_fwd(yp, yg, w_out, xhat, gam, bet, alpha, name):
    t, d = xhat.shape
    pw = yp.shape[1]
    tm = _row_tile(t)

    def body(yp_ref, yg_ref, w_ref, x_ref, g_ref, b_ref, xhat_ref, rstd_ref):
        for part in range(ROW_GROUPS):
            rows = pl.ds(part * (tm // ROW_GROUPS), tm // ROW_GROUPS)
            h = x_ref[rows, :] * g_ref[...] + b_ref[...]
            y = (jnp.dot(yp_ref[rows, :], w_ref[0:pw, :], preferred_element_type=F32)
                 + jnp.dot(yg_ref[rows, :], w_ref[pw:, :], preferred_element_type=F32))
            xh, rs = _ln_stats(alpha * h + y)
            xhat_ref[rows, :] = xh
            rstd_ref[rows, :] = rs

    row = lambda i: (i, 0)
    vec = pl.BlockSpec((1, d), lambda i: (0, 0))
    return _tc_call(
        body, name=name, grid=(t // tm,),
        in_specs=[pl.BlockSpec((tm, pw), row), pl.BlockSpec((tm, yg.shape[1]), row),
                  pl.BlockSpec(w_out.shape, lambda i: (0, 0)), pl.BlockSpec((tm, d), row), vec, vec],
        out_specs=[pl.BlockSpec((tm, d), row), pl.BlockSpec((tm, 1), row)],
        out_shape=[jax.ShapeDtypeStruct((t, d), F32), jax.ShapeDtypeStruct((t, 1), F32)],
        compiler_params=_seq(1),
    )(yp, yg, w_out, xhat, gam, bet)


def _outproj_bwd(dh, xhat, rstd, ln_g, w_out, pw, alpha, name):
    t, d = dh.shape
    tm = _row_tile(t)
    gw = w_out.shape[0] - pw

    def body(dh_ref, xh_ref, rs_ref, g_ref, w_ref, dyb_ref, dyp_ref, dyg_ref, dres_ref, dgam_ref, dbet_ref):
        @pl.when(pl.program_id(0) == 0)
        def _():
            dgam_ref[...] = jnp.zeros_like(dgam_ref)
            dbet_ref[...] = jnp.zeros_like(dbet_ref)

        for part in range(ROW_GROUPS):
            rows = pl.ds(part * (tm // ROW_GROUPS), tm // ROW_GROUPS)
            dy = dh_ref[rows, :]
            xh = xh_ref[rows, :]
            dr = _ln_bwd(dy, xh, rs_ref[rows, :], g_ref[...])
            dgam_ref[...] += jnp.sum(dy * xh, axis=0, keepdims=True)
            dbet_ref[...] += jnp.sum(dy, axis=0, keepdims=True)
            drb = dr.astype(BF16)
            dyb_ref[rows, :] = drb
            dres_ref[rows, :] = alpha * dr
            dyp_ref[rows, :] = _mm_nt(drb, w_ref[0:pw, :])
            dyg_ref[rows, :] = _mm_nt(drb, w_ref[pw:, :])

    row = lambda i: (i, 0)
    vec = pl.BlockSpec((1, d), lambda i: (0, 0))
    return _tc_call(
        body, name=name, grid=(t // tm,),
        in_specs=[pl.BlockSpec((tm, d), row), pl.BlockSpec((tm, d), row), pl.BlockSpec((tm, 1), row), vec,
                  pl.BlockSpec(w_out.shape, lambda i: (0, 0))],
        out_specs=[pl.BlockSpec((tm, d), row), pl.BlockSpec((tm, pw), row), pl.BlockSpec((tm, gw), row),
                   pl.BlockSpec((tm, d), row), vec, vec],
        out_shape=[jax.ShapeDtypeStruct((t, d), BF16), jax.ShapeDtypeStruct((t, pw), F32),
                   jax.ShapeDtypeStruct((t, gw), F32), jax.ShapeDtypeStruct((t, d), F32),
                   jax.ShapeDtypeStruct((1, d), F32), jax.ShapeDtypeStruct((1, d), F32)],
        compiler_params=_seq(1),
    )(dh, xhat, rstd, ln_g, w_out)


def _loss_head(xhat, gam, bet, target, n_rows, name):
    t, d = xhat.shape
    tm = _row_tile(t)

    def body(x_ref, g_ref, b_ref, t_ref, dy_ref, loss_ref):
        i = pl.program_id(0)

        @pl.when(i == 0)
        def _():
            loss_ref[...] = jnp.zeros_like(loss_ref)

        rowi = i * tm + lax.broadcasted_iota(jnp.int32, (tm, 1), 0)
        live = (rowi >= N_META) & (rowi < N_META + n_rows)
        diff = jnp.where(live, x_ref[...] * g_ref[...] + b_ref[...] - t_ref[...], 0.0)
        dy_ref[...] = diff * (1.0 / d)
        loss_ref[...] += jnp.sum(diff * diff) * (0.5 / d)

    row = lambda i: (i, 0)
    vec = pl.BlockSpec((1, d), lambda i: (0, 0))
    return _tc_call(
        body, name=name, grid=(t // tm,),
        in_specs=[pl.BlockSpec((tm, d), row), vec, vec, pl.BlockSpec((tm, d), row)],
        out_specs=[pl.BlockSpec((tm, d), row), pl.BlockSpec((8, LANE), lambda i: (0, 0))],
        out_shape=[jax.ShapeDtypeStruct((t, d), F32), jax.ShapeDtypeStruct((8, LANE), F32)],
        compiler_params=_seq(1),
    )(xhat, gam, bet, target)


def _rows_block(r, c):
    best = r
    for cand in range(8, r, 8):
        if r % cand == 0 and cand * c * 4 <= (1 << 20):
            best = cand
    return best if best * c * 4 <= (4 << 20) else r


def _sum_slots(recvs, name, layers_side_by_side=False):
    nl = len(recvs)
    ns, r, c = recvs[0].shape
    tr = _rows_block(r, c)

    def body(*refs):
        o_ref = refs[nl]
        for l in range(nl):
            acc = refs[l][0].astype(F32)
            for s in range(1, ns):
                acc = acc + refs[l][s].astype(F32)
            if layers_side_by_side:
                o_ref[0, :, l * c:(l + 1) * c] = acc
            else:
                o_ref[l] = acc

    out = (1, r, nl * c) if layers_side_by_side else (nl, r, c)
    return _tc_call(
        body, name=name, grid=(r // tr,),
        in_specs=[pl.BlockSpec((ns, tr, c), lambda i: (0, i, 0))] * nl,
        out_specs=pl.BlockSpec((out[0], tr, out[2]), lambda i: (0, i, 0)),
        out_shape=jax.ShapeDtypeStruct(out, F32),
        compiler_params=_seq(1),
    )(*recvs)


def _adamw(w, terms, m, v, name):
    nl, r, c = w.shape
    tc = c
    while tc % (2 * LANE) == 0 and tc > 4 * LANE:
        tc //= 2
    tr = _rows_block(r, tc)
    nterm = len(terms)

    def body(*refs):
        w_ref = refs[0]
        t_refs = refs[1:1 + nterm]
        m_ref, v_ref, g_ref, d_ref, nm_ref, nv_ref = refs[1 + nterm:]
        g = t_refs[0][...]
        for tr_ in t_refs[1:]:
            g = g + tr_[...]
        nm = ADAM_B1 * m_ref[...] + (1.0 - ADAM_B1) * g
        nv = ADAM_B2 * v_ref[...] + (1.0 - ADAM_B2) * jnp.square(g)
        m_hat = nm / (1.0 - ADAM_B1 ** ADAM_STEP)
        v_hat = nv / (1.0 - ADAM_B2 ** ADAM_STEP)
        g_ref[...] = g
        d_ref[...] = -ADAM_LR * (m_hat / (jnp.sqrt(v_hat) + ADAM_EPS) + ADAM_WD * w_ref[...])
        nm_ref[...] = nm
        nv_ref[...] = nv

    spec = pl.BlockSpec((None, tr, tc), lambda l, i, j: (l, i, j))
    shp = jax.ShapeDtypeStruct((nl, r, c), F32)
    return _tc_call(
        body, name=name, grid=(nl, r // tr, c // tc),
        in_specs=[spec] * (3 + nterm), out_specs=[spec] * 4, out_shape=[shp] * 4,
        compiler_params=_seq(3),
    )(w, *terms, m, v)


XY_RELATIONS = ((1, 0, 0), (0, 1, 0), (1, 1, 0))
ALL_RELATIONS = tuple((fx, fy, fc) for fx in (0, 1) for fy in (0, 1) for fc in (0, 1) if fx or fy or fc)
HBM_SPEC = pl.BlockSpec(memory_space=pltpu.HBM)
SEM_SPEC = pl.BlockSpec(memory_space=pltpu.SEMAPHORE)
DATAFLOW = pltpu.SideEffectType.DATAFLOW_SIDE_EFFECTING


def _split_call(body, **kw):
    return pl.pallas_call(body, **kw)


def _flip(v, f):
    return 1 - v if f else v


def _any_spec(n):
    return [pl.BlockSpec(memory_space=pl.ANY)] * n


def _relations(kind):
    return ALL_RELATIONS if kind == "bcast" else XY_RELATIONS


def _copies(kind, arr, land, sems):
    x, y, c = lax.axis_index("x"), lax.axis_index("y"), lax.axis_index("c")
    out = []
    for (fx, fy, fc), (send_sem, recv_sem) in zip(_relations(kind), sems):
        px, py, pc = _flip(x, fx), _flip(y, fy), _flip(c, fc)
        if kind == "bcast":
            mine, theirs = 4 * x + 2 * y + c, 4 * px + 2 * py + pc
        else:
            mine, theirs = 2 * x + y, 2 * px + py
        src, to_mine, to_theirs = arr, land.at[mine], land.at[theirs]
        if kind == "scatter":
            src = arr.at[theirs]
        if kind == "gather_half":
            rows = _my_half(arr.shape[0], c)
            src, to_mine, to_theirs = arr.at[rows], land.at[mine, rows], land.at[theirs, rows]
        both = dict(src_ref=src, send_sem=send_sem, recv_sem=recv_sem, device_id=(px, py, pc), device_id_type=MESH)
        out.append((pltpu.make_async_remote_copy(dst_ref=to_mine, **both),
                    pltpu.make_async_remote_copy(dst_ref=to_theirs, **both)))
    return out


def _my_half(nrows, c):
    return pl.ds(c * (nrows // 2), nrows // 2)


def _share_halves(name, kinds, lands):
    ks = [k for k, kd in enumerate(kinds) if kd == "gather_half"]
    n = len(ks)

    def body(*refs):
        l_refs = refs[n:2 * n]
        send_sems, recv_sems = refs[2 * n:]
        x, y, c = lax.axis_index("x"), lax.axis_index("y"), lax.axis_index("c")
        copies = []
        for i in range(n):
            nrows = l_refs[i].shape[1]
            for r, (fx, fy, _) in enumerate(XY_RELATIONS):
                slot = 2 * _flip(x, fx) + _flip(y, fy)
                both = dict(src_ref=l_refs[i].at[slot, _my_half(nrows, c)], send_sem=send_sems.at[i, r],
                            recv_sem=recv_sems.at[i, r], device_id=(x, y, 1 - c), device_id_type=MESH)
                copies.append((pltpu.make_async_remote_copy(dst_ref=l_refs[i].at[slot, _my_half(nrows, c)], **both),
                               pltpu.make_async_remote_copy(dst_ref=l_refs[i].at[slot, _my_half(nrows, 1 - c)], **both)))
        for send, _ in copies:
            send.start()
        for _, arrival in copies:
            arrival.wait_recv()
        for send, _ in copies:
            send.wait_send()

    outs = _comm_call(
        body, name=name,
        in_specs=_any_spec(n), out_specs=_any_spec(n),
        out_shape=[jax.ShapeDtypeStruct(lands[k].shape, lands[k].dtype) for k in ks],
        input_output_aliases={i: i for i in range(n)},
        scratch_shapes=[pltpu.SemaphoreType.DMA((n, 3)), pltpu.SemaphoreType.DMA((n, 3))],
    )(*[lands[k] for k in ks])
    lands = list(lands)
    for k, o in zip(ks, outs):
        lands[k] = o
    return lands


def _sem_pairs(kinds, sems):
    out, at = [], 0
    for kind in kinds:
        nrel = len(_relations(kind))
        out.append([(sems[at + 2 * r], sems[at + 2 * r + 1]) for r in range(nrel)])
        at += 2 * nrel
    return out


def _exchange_start(name, kinds, arrs, lands):
    n = len(arrs)
    nsem = sum(2 * len(_relations(kd)) for kd in kinds)

    def body(*refs):
        a_refs, l_refs = refs[:n], refs[n:2 * n]
        pairs = _sem_pairs(kinds, refs[2 * n:2 * n + nsem])
        token = refs[-1]
        for k in range(n):
            for send, _ in _copies(kinds[k], a_refs[k], l_refs[k], pairs[k]):
                send.start()
        token[...] = jnp.zeros_like(token)

    thru = [pltpu.HBM(a.shape, a.dtype) for a in list(arrs) + list(lands)]
    outs = _split_call(
        body, name=name,
        out_shape=(*[pltpu.SemaphoreType.DMA(())] * nsem, *thru, jax.ShapeDtypeStruct((8, LANE), F32)),
        in_specs=[HBM_SPEC] * (2 * n),
        out_specs=(*[SEM_SPEC] * nsem, *[HBM_SPEC] * (2 * n), pl.BlockSpec(memory_space=pltpu.VMEM)),
        input_output_aliases={i: nsem + i for i in range(2 * n)},
        compiler_params=pltpu.CompilerParams(has_side_effects=DATAFLOW),
    )(*[pltpu.with_memory_space_constraint(a, pltpu.HBM) for a in list(arrs) + list(lands)])
    return dict(kinds=kinds, sems=outs[:nsem], arrs=outs[nsem:nsem + n], lands=outs[nsem + n:nsem + 2 * n],
                token=outs[-1])


def _exchange_wait(name, st, after):
    kinds = st["kinds"]
    n = len(kinds)
    nsem = len(st["sems"])

    def body(*refs):
        a_refs, l_refs = refs[:n], refs[n:2 * n]
        pairs = _sem_pairs(kinds, refs[2 * n:2 * n + nsem])
        for k in range(n):
            for _, arrival in _copies(kinds[k], a_refs[k], l_refs[k], pairs[k]):
                arrival.wait_send()
                arrival.wait_recv()
        refs[-1][...] = jnp.zeros_like(refs[-1])

    ins = list(st["arrs"]) + list(st["lands"])
    outs = _split_call(
        body, name=name,
        out_shape=[pltpu.HBM(a.shape, a.dtype) for a in ins] + [jax.ShapeDtypeStruct((8, LANE), F32)],
        in_specs=[HBM_SPEC] * (2 * n) + [SEM_SPEC] * nsem + [pl.BlockSpec(memory_space=pl.ANY)],
        out_specs=[HBM_SPEC] * (2 * n) + [pl.BlockSpec(memory_space=pltpu.VMEM)],
        input_output_aliases={i: i for i in range(2 * n)},
        compiler_params=pltpu.CompilerParams(has_side_effects=DATAFLOW),
    )(*ins, *st["sems"], after)
    return outs[n:2 * n], outs[-1]


def _landing(own, slot, nslot):
    return lax.dynamic_update_slice(lax.empty((nslot,) + own.shape, own.dtype), own[None], (slot,) + (0,) * own.ndim)


def _swap_sibling(parts, name):
    n = len(parts)

    def body(*refs):
        ins, outs = refs[:n], refs[n:2 * n]
        send_sems, recv_sems = refs[2 * n:]
        sib = (lax.axis_index("x"), lax.axis_index("y"), 1 - lax.axis_index("c"))
        cps = [pltpu.make_async_remote_copy(src_ref=ins[k], dst_ref=outs[k], send_sem=send_sems.at[k],
                                            recv_sem=recv_sems.at[k], device_id=sib, device_id_type=MESH)
               for k in range(n)]
        for cp in cps:
            cp.start()
        for cp in cps:
            cp.wait_recv()
        for cp in cps:
            cp.wait_send()

    return _comm_call(
        body, name=name,
        in_specs=_any_spec(n), out_specs=_any_spec(n),
        out_shape=[jax.ShapeDtypeStruct(p.shape, p.dtype) for p in parts],
        scratch_shapes=[pltpu.SemaphoreType.DMA((n,)), pltpu.SemaphoreType.DMA((n,))],
    )(*parts)


def _col_shards(a, n=N_SHARD):
    r, c = a.shape
    return a.reshape(r, n, c // n).transpose(1, 0, 2)


def _from_col_shards(a):
    n, r, cs = a.shape
    return a.transpose(1, 0, 2).reshape(r, n * cs)


def kernel(x, meta_tokens, ffn1_w_gate, ffn1_w_up, ffn1_w_down, ln1_g, ln1_b, w_in, w_gate_up, b_gate, w_pool, pool_scale, gla_norm_g, w_out, ln2_g, ln2_b, ffn2_w_gate, ffn2_w_up, ffn2_w_down, ln3_g, ln3_b, loss_target, m_meta_tokens, m_ffn1_w_gate, m_ffn1_w_up, m_ffn1_w_down, m_ln1_g, m_ln1_b, m_w_in, m_w_gate_up, m_b_gate, m_w_pool, m_pool_scale, m_gla_norm_g, m_w_out, m_ln2_g, m_ln2_b, m_ffn2_w_gate, m_ffn2_w_up, m_ffn2_w_down, m_ln3_g, m_ln3_b, v_meta_tokens, v_ffn1_w_gate, v_ffn1_w_up, v_ffn1_w_down, v_ln1_g, v_ln1_b, v_w_in, v_w_gate_up, v_b_gate, v_w_pool, v_pool_scale, v_gla_norm_g, v_w_out, v_ln2_g, v_ln2_b, v_ffn2_w_gate, v_ffn2_w_up, v_ffn2_w_down, v_ln3_g, v_ln3_b):
    w = dict(meta_tokens=meta_tokens, ffn1_w_gate=ffn1_w_gate, ffn1_w_up=ffn1_w_up, ffn1_w_down=ffn1_w_down,
             ln1_g=ln1_g, ln1_b=ln1_b, w_in=w_in, w_gate_up=w_gate_up, b_gate=b_gate, w_pool=w_pool,
             pool_scale=pool_scale, gla_norm_g=gla_norm_g, w_out=w_out, ln2_g=ln2_g, ln2_b=ln2_b,
             ffn2_w_gate=ffn2_w_gate, ffn2_w_up=ffn2_w_up, ffn2_w_down=ffn2_w_down, ln3_g=ln3_g, ln3_b=ln3_b)
    mom1 = dict(meta_tokens=m_meta_tokens, ffn1_w_gate=m_ffn1_w_gate, ffn1_w_up=m_ffn1_w_up,
                ffn1_w_down=m_ffn1_w_down, ln1_g=m_ln1_g, ln1_b=m_ln1_b, w_in=m_w_in, w_gate_up=m_w_gate_up,
                b_gate=m_b_gate, w_pool=m_w_pool, pool_scale=m_pool_scale, gla_norm_g=m_gla_norm_g, w_out=m_w_out,
                ln2_g=m_ln2_g, ln2_b=m_ln2_b, ffn2_w_gate=m_ffn2_w_gate, ffn2_w_up=m_ffn2_w_up,
                ffn2_w_down=m_ffn2_w_down, ln3_g=m_ln3_g, ln3_b=m_ln3_b)
    mom2 = dict(meta_tokens=v_meta_tokens, ffn1_w_gate=v_ffn1_w_gate, ffn1_w_up=v_ffn1_w_up,
                ffn1_w_down=v_ffn1_w_down, ln1_g=v_ln1_g, ln1_b=v_ln1_b, w_in=v_w_in, w_gate_up=v_w_gate_up,
                b_gate=v_b_gate, w_pool=v_w_pool, pool_scale=v_pool_scale, gla_norm_g=v_gla_norm_g, w_out=v_w_out,
                ln2_g=v_ln2_g, ln2_b=v_ln2_b, ffn2_w_gate=v_ffn2_w_gate, ffn2_w_up=v_ffn2_w_up,
                ffn2_w_down=v_ffn2_w_down, ln3_g=v_ln3_g, ln3_b=v_ln3_b)

    xs = x[0]
    s_len, d = xs.shape
    nl = ln1_g.shape[0]
    alpha = (2.0 * nl) ** 0.25
    t_real = N_META + s_len
    t_pad = -(-t_real // LANE) * LANE
    pw = pool_scale.shape[1]
    kw = b_gate.shape[1]
    vw = gla_norm_g.shape[1]
    rank = w_gate_up.shape[1]
    widths = (pw, kw, kw, vw, vw)
    n_main = sum(widths)
    dff_s = ffn1_w_gate.shape[2]
    dff_c = N_SHARD * dff_s // FFN_CHUNKS

    me_xy = 2 * lax.axis_index("x") + lax.axis_index("y")
    me_all = 2 * me_xy + lax.axis_index("c")
    ffn1_names = ("ffn1_w_gate", "ffn1_w_up", "ffn1_w_down")
    mix_names = ("w_out", "w_gate_up", "w_in")
    ffn2_names = ("ffn2_w_gate", "ffn2_w_up", "ffn2_w_down")

    gate_up = ("ffn1_w_gate", "ffn1_w_up", "ffn2_w_gate", "ffn2_w_up")

    def stored(n, a):
        if n in gate_up:
            return jnp.swapaxes(a, 1, 2)
        return jnp.transpose(a, (2, 0, 1)) if n == "w_in" else a

    def as_given(n, a):
        if n in gate_up:
            return jnp.swapaxes(a, 1, 2)
        if n == "w_in":
            return jnp.transpose(a.reshape(-1, nl, d), (1, 2, 0))
        return a.reshape(w[n].shape)

    stages = [[("meta_tokens", None)], [(n, 0) for n in ffn1_names], [(n, 0) for n in mix_names + ffn2_names]]
    stages += [[(n, l) for n in BIG] for l in range(1, nl)]
    gathers, wa = {}, {}

    halved = ffn1_names + ffn2_names + ("w_out",)

    def start_gather(si, dep=None):
        own = []
        for n, l in stages[si]:
            a = meta_tokens if l is None else (stored(n, w[n])[:, l] if n == "w_in" else stored(n, w[n])[l])
            a = a if dep is None else a + dep
            own.append(a if l is None else a.astype(BF16))
        gathers[si] = _exchange_start(f"gather_start_{si}", ["gather_half" if n in halved else "gather"
                                                             for n, _ in stages[si]], own,
                                      [_landing(a, me_xy, N_SHARD) for a in own])
        return gathers[si]["token"]

    def arrive(si, after):
        lands, token = _exchange_wait(f"gather_wait_{si}", gathers[si], after)
        if any(kd == "gather_half" for kd in gathers[si]["kinds"]):
            lands = _share_halves(f"gather_share_{si}", gathers[si]["kinds"], lands)
        for item, a in zip(stages[si], lands):
            wa[item] = a.reshape(FFN_CHUNKS, -1, d) if item[0] in ffn1_names + ffn2_names else a
        return token

    def mixer_weights(l):
        wi = wa["w_in", l].reshape(-1, d)
        return dict(w_main=wi[:n_main], w_lr=jnp.pad(wi[n_main:], ((0, LANE - rank), (0, 0))),
                    wgu=jnp.pad(_from_col_shards(wa["w_gate_up", l]), ((0, LANE - rank), (0, 0))),
                    wout=wa["w_out", l].reshape(-1, d))

    wp16 = w_pool.astype(BF16)
    ones = jnp.ones((1, d), F32)
    zeros = jnp.zeros((1, d), F32)
    target = jnp.concatenate([jnp.zeros((N_META, d), F32), loss_target[0], jnp.zeros((t_pad - t_real, d), F32)], axis=0)

    started = start_gather(0)
    for si in range(1, len(stages)):
        started = start_gather(si, started[0:1, 0:1])
    arrive(0, started)
    meta_full = _from_col_shards(wa["meta_tokens", None])
    h0 = jnp.concatenate([meta_full, xs, jnp.zeros((t_pad - t_real, d), F32)], axis=0)
    arrive(1, h0[:8, :LANE] + target[:8, :LANE])

    saved, mw = [], []
    cur, cur_g, cur_b = h0, ones, zeros
    for l in range(nl):
        s = {}
        xh1, rs1, hb0, g1, u1 = _ffn_fwd(cur, cur_g, cur_b, wa["ffn1_w_gate", l], wa["ffn1_w_up", l],
                                         wa["ffn1_w_down", l], alpha, f"ffn1_fwd_{l}")
        if l == 0:
            arrive(2, xh1)
        mw.append(mixer_weights(l))
        up, q, k, v, r, zg, la, hb1 = _inproj_fwd(xh1, ln1_g[l:l + 1], ln1_b[l:l + 1], mw[l]["w_main"], mw[l]["w_lr"],
                                                  mw[l]["wgu"], b_gate[l:l + 1], widths, f"inproj_fwd_{l}")
        yp, pb = _pool_fwd(up, wp16[l], pool_scale[l:l + 1], f"pool_fwd_{l}")
        o, yg, sall = _gla_fwd(q, k, v, la, r, gla_norm_g[l:l + 1], f"gla_fwd_{l}")
        xh2, rs2 = _outproj_fwd(yp, yg, mw[l]["wout"], xh1, ln1_g[l:l + 1], ln1_b[l:l + 1], alpha, f"outproj_fwd_{l}")
        if l + 1 < nl:
            arrive(l + 3, xh2)
        xh3, rs3, hb2, g2, u2 = _ffn_fwd(xh2, ln2_g[l:l + 1], ln2_b[l:l + 1], wa["ffn2_w_gate", l], wa["ffn2_w_up", l],
                                         wa["ffn2_w_down", l], alpha, f"ffn2_fwd_{l}")
        s.update(xh1=xh1, rs1=rs1, hb0=hb0, g1=g1, u1=u1, q=q, k=k, v=v, r=r, zg=zg, la=la, hb1=hb1, yp=yp, pb=pb,
                 o=o, yg=yg, sall=sall, xh2=xh2, rs2=rs2, xh3=xh3, rs3=rs3, hb2=hb2, g2=g2, u2=u2)
        saved.append(s)
        cur, cur_g, cur_b = xh3, ln3_g[l:l + 1], ln3_b[l:l + 1]

    dh, loss_acc = _loss_head(cur, cur_g, cur_b, target, s_len, "loss_head")
    loss = lax.psum(loss_acc[0, 0], ("x", "y", "c"))

    small_grads = {n: [None] * nl for n in SMALL}
    scatters = []

    def depart(name, items, grads, kinds=None):
        lands = [_landing(g if kd == "bcast" else lax.dynamic_index_in_dim(g, me_xy, 0, keepdims=False),
                          me_all if kd == "bcast" else me_xy, N_DEV if kd == "bcast" else N_SHARD)
                 for g, kd in zip(grads, kinds or ["scatter"] * len(grads))]
        st = _exchange_start(name, kinds or ["scatter"] * len(grads), grads, lands)
        scatters.append((name, items, st))
        return st["token"]

    def pack(parts):
        flat = jnp.concatenate([parts[n].reshape(-1) for n in SMALL])
        return flat.reshape(-1, LANE)

    def ffn_wgrads(l, names, hb, dgb, dub, act, dfb):
        tag = names[0][:4]
        return [_wgrad(dgb, hb, dff_c, d, f"{tag}_dwg_{l}").reshape(N_SHARD, dff_s, d),
                _wgrad(dub, hb, dff_c, d, f"{tag}_dwu_{l}").reshape(N_SHARD, dff_s, d),
                _wgrad(act, dfb, dff_c, d, f"{tag}_dwd_{l}").reshape(N_SHARD, dff_s, d)]

    late = []
    for l in reversed(range(nl)):
        s = saved[l]
        dh, dfb, dgb, dub, act, dgam, dbet = _ffn_bwd(dh, s["xh3"], s["rs3"], ln3_g[l:l + 1], s["g2"], s["u2"],
                                                      wa["ffn2_w_gate", l], wa["ffn2_w_up", l], wa["ffn2_w_down", l],
                                                      alpha, f"ffn2_bwd_{l}")
        small_grads["ln3_g"][l], small_grads["ln3_b"][l] = dgam, dbet
        gone = depart(f"scatter_start_ffn2_{l}", [(n, l) for n in ffn2_names],
                      ffn_wgrads(l, ffn2_names, s["hb2"], dgb, dub, act, dfb))

        dyb, dyp, dyg, dres, dgam, dbet = _outproj_bwd(dh, s["xh2"], s["rs2"], ln2_g[l:l + 1] + gone[0:1, 0:1],
                                                       mw[l]["wout"], pw, alpha, f"outproj_bwd_{l}")
        small_grads["ln2_g"][l], small_grads["ln2_b"][l] = dgam, dbet
        dwo = jnp.concatenate([_wgrad(s["yp"], dyb, pw, d, f"dwout_pool_{l}"),
                               _wgrad(s["yg"], dyb, vw, d, f"dwout_gla_{l}")], axis=0)
        dq, dk, dv, dr, dzg, dwgu, dbg, dgn = _gla_bwd(dyg, s["o"], s["r"], gla_norm_g[l:l + 1], s["q"], s["k"],
                                                       s["v"], s["la"], s["zg"], s["sall"], mw[l]["wgu"],
                                                       f"gla_bwd_{l}")
        dup, dwp, dsc = _pool_bwd(dyp, s["pb"], wp16[l], pool_scale[l:l + 1], f"pool_bwd_{l}")
        small_grads["b_gate"][l], small_grads["gla_norm_g"][l] = dbg, dgn
        small_grads["w_pool"][l], small_grads["pool_scale"][l] = dwp, dsc
        dh, dz = _inproj_bwd(dres, [dup, dq, dk, dv, dr], dzg, mw[l]["w_main"], mw[l]["w_lr"], f"inproj_bwd_{l}")
        dwi = jnp.concatenate([_wgrad(dz, s["hb1"], 4 * LANE, d, f"dwin_main_{l}"),
                               _wgrad(dzg, s["hb1"], LANE, d, f"dwin_lr_{l}")[:rank]], axis=0)
        gone = depart(f"scatter_start_mix_{l}", [(n, l) for n in mix_names],
                      [dwo.reshape(N_SHARD, -1, d), _col_shards(dwgu[:rank].astype(BF16)),
                       dwi.reshape(N_SHARD, -1, d)])

        dh, dfb, dgb, dub, act, dgam, dbet = _ffn_bwd(dh, s["xh1"], s["rs1"], ln1_g[l:l + 1] + gone[0:1, 0:1],
                                                      s["g1"], s["u1"], wa["ffn1_w_gate", l], wa["ffn1_w_up", l],
                                                      wa["ffn1_w_down", l], alpha, f"ffn1_bwd_{l}")
        small_grads["ln1_g"][l], small_grads["ln1_b"][l] = dgam, dbet
        if l:
            gone = depart(f"scatter_start_ffn1_{l}", [(n, l) for n in ffn1_names],
                          ffn_wgrads(l, ffn1_names, s["hb0"], dgb, dub, act, dfb))
            ln3_g = ln3_g.at[l - 1:l].add(gone[0:1, 0:1])
            continue
        grad_x = dh[N_META:t_real][None]
        small_vec = pack({n: jnp.stack(small_grads[n]) for n in SMALL})
        gone = depart("scatter_start_rest", [("meta_tokens", 0), ("small", 0)],
                      [_col_shards(dh[:N_META].astype(BF16)), small_vec], ["scatter", "bcast"])
        for n, a, b in zip(ffn1_names, (dgb, dub, act), (s["hb0"], s["hb0"], dfb)):
            g = _wgrad(a, b, dff_c, d, f"{n}_grad_{l}", after=gone).reshape(N_SHARD, dff_s, d)
            gone = depart(f"scatter_start_{n}", [(n, l)], [g])
            late.append(scatters.pop())

    recv, results, firsts = {}, {}, []

    def collect(group, after):
        for name, items, st in group:
            for item, a in zip(items, _exchange_wait(name.replace("start", "wait"), st, after)[0]):
                recv[item] = a

    def reduce_and_update(names, tag):
        partial = [_sum_slots([recv[n, l] for l in range(1 if n == "meta_tokens" else nl)], f"sum_{n}", n == "w_in")
                   for n in names]
        for n, mine, theirs in zip(names, partial, _swap_sibling(partial, f"swap_sibling_{tag}")):
            fit = lambda a: stored(n, a).reshape(mine.shape)
            outs = _adamw(fit(w[n]), [mine, theirs], fit(mom1[n]), fit(mom2[n]), f"adamw_{n}")
            results[n] = [as_given(n, o) for o in outs]
            firsts.append(outs[1][0, 0, 0])

    collect(scatters, gone)
    early = [n for n in ("meta_tokens",) + BIG if n not in ffn1_names]
    reduce_and_update(early, "early")
    small_terms = [recv["small", 0][i][None] for i in range(N_DEV)]
    souts = _adamw(pack(w)[None], small_terms, pack(mom1)[None], pack(mom2)[None], "adamw_small")
    off = 0
    for n in SMALL:
        size = w[n].size
        results[n] = [o.reshape(-1)[off:off + size].reshape(w[n].shape) for o in souts]
        off += size
    collect(late, souts[0][0, :8] + functools.reduce(jnp.add, firsts))
    reduce_and_update(ffn1_names, "late")

    out = [loss, grad_x]
    for part in range(4):
        out += [results[n][part] for n in WEIGHTS]
    return tuple(out)
```

```python
import functools

import jax
import jax.numpy as jnp
from jax import lax
from jax.experimental import pallas as pl
from jax.experimental.pallas import tpu as pltpu

F32 = jnp.float32
BF16 = jnp.bfloat16
MESH = pl.DeviceIdType.MESH

N_META = 16
POOL_WINDOWS = (2, 4, 8, 16)
POOL_HALO = 16
N_HEADS = 4
GLA_GATE_TEMP = 16.0
CHUNK = 128
CHUNK_UNROLL = 5
LN_EPS = 1e-5
RMS_EPS = 1e-6
ADAM_LR = 0.001
ADAM_B1 = 0.9
ADAM_B2 = 0.999
ADAM_EPS = 1e-08
ADAM_WD = 0.01
ADAM_STEP = 10
LANE = 128
BF16_ROWS = 16
ROW_TILE = 640
FFN_ROW_TILE = 640
FFN_CHUNKS = 4
ROW_GROUPS = 2
FFN_SPLIT = 2
WGRAD_K_MAX = 2176
N_SHARD = 4
N_DEV = 8

BIG = ("ffn1_w_gate", "ffn1_w_up", "ffn1_w_down", "w_in", "w_gate_up", "w_out",
       "ffn2_w_gate", "ffn2_w_up", "ffn2_w_down")
SMALL = ("ln1_g", "ln1_b", "b_gate", "w_pool", "pool_scale", "gla_norm_g", "ln2_g", "ln2_b", "ln3_g", "ln3_b")
WEIGHTS = ("meta_tokens", "ffn1_w_gate", "ffn1_w_up", "ffn1_w_down", "ln1_g", "ln1_b", "w_in", "w_gate_up",
           "b_gate", "w_pool", "pool_scale", "gla_norm_g", "w_out", "ln2_g", "ln2_b", "ffn2_w_gate",
           "ffn2_w_up", "ffn2_w_down", "ln3_g", "ln3_b")


def _tc_call(body, **kw):
    return pl.pallas_call(body, **kw)


def _comm_call(body, **kw):
    return pl.pallas_call(body, **kw)


def _seq(n):
    return pltpu.CompilerParams(dimension_semantics=("arbitrary",) * n)


def _mm(a, b):
    return jnp.dot(a.astype(BF16), b.astype(BF16), preferred_element_type=F32)


def _mm_nt(a, b):
    return lax.dot_general(a.astype(BF16), b.astype(BF16), (((1,), (1,)), ((), ())), preferred_element_type=F32)


def _mm_tn(a, b):
    return lax.dot_general(a.astype(BF16), b.astype(BF16), (((0,), (0,)), ((), ())), preferred_element_type=F32)


def _mm_f32(a, b):
    return jnp.dot(a, b, precision=lax.Precision.HIGHEST, preferred_element_type=F32)


def _row_tile(t, most=None):
    tm = min(most or ROW_TILE, t)
    while t % tm:
        tm -= LANE
    return tm


def _silu_parts(g):
    sg = jax.nn.sigmoid(g)
    return sg, g * sg


def _ln_stats(r):
    mu = jnp.mean(r, axis=-1, keepdims=True)
    rc = r - mu
    var = jnp.mean(rc * rc, axis=-1, keepdims=True)
    rs = lax.rsqrt(var + LN_EPS)
    return rc * rs, rs


def _ln_bwd(dy, xh, rs, gam):
    dyg = dy * gam
    c1 = jnp.mean(dyg, axis=-1, keepdims=True)
    c2 = jnp.mean(dyg * xh, axis=-1, keepdims=True)
    return rs * (dyg - c1 - xh * c2)


def _ffn_fwd(xin, gam_in, bet_in, wg, wu, wd, alpha, name):
    t, d = xin.shape
    nj, tf, _ = wg.shape
    tm = _row_tile(t, FFN_ROW_TILE)

    def body(x_ref, gi_ref, bi_ref, wg_ref, wu_ref, wd_ref, xhat_ref, rstd_ref, hb_ref, go_ref, uo_ref, acc, hbs):
        j = pl.program_id(1)

        @pl.when(j == 0)
        def _():
            h = x_ref[...] * gi_ref[...] + bi_ref[...]
            hb = h.astype(BF16)
            hbs[...] = hb
            hb_ref[...] = hb
            acc[...] = (2.0 * alpha) * h

        hb = hbs[...]
        g = _mm_nt(hb, wg_ref[...])
        u = _mm_nt(hb, wu_ref[...])
        _, sl = _silu_parts(g)
        go_ref[...] = g.astype(BF16)
        uo_ref[...] = u.astype(BF16)
        acc[...] += jnp.dot((sl * u).astype(BF16), wd_ref[...], preferred_element_type=F32)

        @pl.when(j == nj - 1)
        def _():
            xhat, rs = _ln_stats(0.5 * acc[...])
            xhat_ref[...] = xhat
            rstd_ref[...] = rs

    row = lambda i, j: (i, 0)
    vec = pl.BlockSpec((1, d), lambda i, j: (0, 0))
    return _tc_call(
        body, name=name, grid=(t // tm, nj),
        in_specs=[pl.BlockSpec((tm, d), row), vec, vec] + [pl.BlockSpec((None, tf, d), lambda i, j: (j, 0, 0))] * 3,
        out_specs=[pl.BlockSpec((tm, d), row), pl.BlockSpec((tm, 1), row), pl.BlockSpec((tm, d), row),
                   pl.BlockSpec((None, tm, tf), lambda i, j: (j, i, 0)),
                   pl.BlockSpec((None, tm, tf), lambda i, j: (j, i, 0))],
        out_shape=[jax.ShapeDtypeStruct((t, d), F32), jax.ShapeDtypeStruct((t, 1), F32),
                   jax.ShapeDtypeStruct((t, d), BF16), jax.ShapeDtypeStruct((nj, t, tf), BF16),
                   jax.ShapeDtypeStruct((nj, t, tf), BF16)],
        scratch_shapes=[pltpu.VMEM((tm, d), F32), pltpu.VMEM((tm, d), BF16)],
        compiler_params=_seq(2),
    )(xin, gam_in, bet_in, wg, wu, wd)


def _ffn_bwd(dh, xhat, rstd, ln_g, gb, ub, wg, wu, wd, alpha, name):
    t, d = dh.shape
    nj, tf, _ = wg.shape
    tm = _row_tile(t, FFN_ROW_TILE)

    def body(dh_ref, xh_ref, rs_ref, g_ref, gb_ref, ub_ref, wg_ref, wu_ref, wd_ref,
             dhin_ref, df_ref, dg_ref, du_ref, dgam_ref, dbet_ref, df_s):
        i = pl.program_id(0)
        j = pl.program_id(1)

        @pl.when(j == 0)
        def _():
            dy = dh_ref[...]
            xh = xh_ref[...]
            dr = _ln_bwd(dy, xh, rs_ref[...], g_ref[...])
            dhin_ref[...] = alpha * dr
            dfb = (0.5 * dr).astype(BF16)
            df_s[...] = dfb
            df_ref[...] = dfb

            @pl.when(i == 0)
            def _():
                dgam_ref[...] = jnp.zeros_like(dgam_ref)
                dbet_ref[...] = jnp.zeros_like(dbet_ref)

            dgam_ref[...] += jnp.sum(dy * xh, axis=0, keepdims=True)
            dbet_ref[...] += jnp.sum(dy, axis=0, keepdims=True)

        for part in range(FFN_SPLIT):
            rows = pl.ds(part * (tm // FFN_SPLIT), tm // FFN_SPLIT)
            dact = _mm_nt(df_s[rows, :], wd_ref[...])
            g = gb_ref[rows, :].astype(F32)
            u = ub_ref[rows, :].astype(F32)
            sg, sl = _silu_parts(g)
            dg = (dact * u * (sg * (1.0 + g * (1.0 - sg)))).astype(BF16)
            du = (dact * sl).astype(BF16)
            dg_ref[rows, :] = dg
            du_ref[rows, :] = du
            dhin_ref[rows, :] += _mm(dg, wg_ref[...]) + _mm(du, wu_ref[...])

    row = lambda i, j: (i, 0)
    col = pl.BlockSpec((None, tm, tf), lambda i, j: (j, i, 0))
    vec = pl.BlockSpec((1, d), lambda i, j: (0, 0))
    ff = jax.ShapeDtypeStruct((nj, t, tf), BF16)
    return _tc_call(
        body, name=name, grid=(t // tm, nj),
        in_specs=[pl.BlockSpec((tm, d), row), pl.BlockSpec((tm, d), row), pl.BlockSpec((tm, 1), row), vec,
                  col, col] + [pl.BlockSpec((None, tf, d), lambda i, j: (j, 0, 0))] * 3,
        out_specs=[pl.BlockSpec((tm, d), row), pl.BlockSpec((tm, d), row), col, col, vec, vec],
        out_shape=[jax.ShapeDtypeStruct((t, d), F32), jax.ShapeDtypeStruct((t, d), BF16), ff, ff,
                   jax.ShapeDtypeStruct((1, d), F32), jax.ShapeDtypeStruct((1, d), F32)],
        scratch_shapes=[pltpu.VMEM((tm, d), BF16)],
        compiler_params=_seq(2),
    )(dh, xhat, rstd, ln_g, gb, ub, wg, wu, wd)


def _wgrad(a, b, tmm, tn, name, after=None, gate=None):
    t = a.shape[-2]
    m = a.shape[-1] * (a.shape[0] if a.ndim == 3 else 1)
    n = b.shape[-1] * (b.shape[0] if b.ndim == 3 else 1)
    tk = max(k for k in range(BF16_ROWS, WGRAD_K_MAX + 1, BF16_ROWS) if t % k == 0)
    nk = t // tk
    lefts = [a] if gate is None else [a, gate]
    extra = [] if after is None else [after]

    def body(*refs):
        a_refs, b_ref = refs[:len(lefts)], refs[len(lefts)]
        o_ref, acc = refs[len(lefts) + 1 + len(extra):]
        k = pl.program_id(2)

        @pl.when(k == 0)
        def _():
            acc[...] = jnp.zeros_like(acc)

        left = a_refs[0][...]
        if gate is not None:
            left = (_silu_parts(left.astype(F32))[1] * a_refs[1][...].astype(F32)).astype(BF16)
        acc[...] += _mm_tn(left, b_ref[...])

        @pl.when(k == nk - 1)
        def _():
            o_ref[...] = acc[...].astype(o_ref.dtype)

    a_spec = (pl.BlockSpec((None, tk, tmm), lambda i, j, k: (i, k, 0)) if a.ndim == 3
              else pl.BlockSpec((tk, tmm), lambda i, j, k: (k, i)))
    return _tc_call(
        body, name=name, grid=(m // tmm, n // tn, nk),
        in_specs=[a_spec] * len(lefts)
                 + [pl.BlockSpec((None, tk, tn), lambda i, j, k: (j, k, 0)) if b.ndim == 3
                    else pl.BlockSpec((tk, tn), lambda i, j, k: (k, j))] + [pl.BlockSpec(memory_space=pl.ANY)] * len(extra),
        out_specs=pl.BlockSpec((tmm, tn), lambda i, j, k: (i, j)),
        out_shape=jax.ShapeDtypeStruct((m, n), BF16),
        scratch_shapes=[pltpu.VMEM((tmm, tn), F32)],
        compiler_params=_seq(3),
    )(*lefts, b, *extra)


def _inproj_fwd(xhat, gam, bet, w_main, w_lr, wgu, b_gate, widths, name):
    t, d = xhat.shape
    tm = _row_tile(t)
    kw = wgu.shape[1]
    offs = [0]
    for w in widths:
        offs.append(offs[-1] + w)

    def body(x_ref, g_ref, b_ref, wm_ref, wl_ref, wgu_ref, bg_ref, *outs):
        piece_refs, (zg_ref, la_ref, hb_ref) = outs[:len(widths)], outs[len(widths):]
        hb = (x_ref[...] * g_ref[...] + b_ref[...]).astype(BF16)
        hb_ref[...] = hb
        for p, ref in enumerate(piece_refs):
            ref[...] = _mm_nt(hb, wm_ref[offs[p]:offs[p + 1], :])
        zg = _mm_nt(hb, wl_ref[...])
        zg_ref[...] = zg
        logit = _mm(zg, wgu_ref[...]) + bg_ref[...]
        la_ref[...] = (jnp.minimum(logit, 0.0) - jnp.log(1.0 + jnp.exp(-jnp.abs(logit)))) * (1.0 / GLA_GATE_TEMP)

    row = lambda i: (i, 0)
    full = lambda a: pl.BlockSpec(a.shape, lambda i: (0,) * a.ndim)
    out_w = list(widths) + [LANE, kw]
    return _tc_call(
        body, name=name, grid=(t // tm,),
        in_specs=[pl.BlockSpec((tm, d), row), full(gam), full(bet), full(w_main), full(w_lr), full(wgu), full(b_gate)],
        out_specs=[pl.BlockSpec((tm, w), row) for w in out_w] + [pl.BlockSpec((tm, d), row)],
        out_shape=[jax.ShapeDtypeStruct((t, w), F32) for w in out_w] + [jax.ShapeDtypeStruct((t, d), BF16)],
        compiler_params=_seq(1),
    )(xhat, gam, bet, w_main, w_lr, wgu, b_gate)


def _inproj_bwd(dh_part, pieces, dzg, w_main, w_lr, name):
    t, d = dh_part.shape
    tm = _row_tile(t)
    widths = [p.shape[1] for p in pieces]
    offs = [0]
    for w in widths:
        offs.append(offs[-1] + w)

    def body(*refs):
        dhp_ref = refs[0]
        p_refs = refs[1:1 + len(widths)]
        dzg_ref, wm_ref, wl_ref, dh_ref, dz_ref = refs[1 + len(widths):]
        acc = dhp_ref[...] + _mm(dzg_ref[...], wl_ref[...])
        for p, ref in enumerate(p_refs):
            v = ref[...]
            dz_ref[:, offs[p]:offs[p + 1]] = v
            acc += _mm(v, wm_ref[offs[p]:offs[p + 1], :])
        dh_ref[...] = acc

    row = lambda i: (i, 0)
    full = lambda a: pl.BlockSpec(a.shape, lambda i: (0,) * a.ndim)
    return _tc_call(
        body, name=name, grid=(t // tm,),
        in_specs=[pl.BlockSpec((tm, d), row)] + [pl.BlockSpec((tm, w), row) for w in widths]
                 + [pl.BlockSpec((tm, LANE), row), full(w_main), full(w_lr)],
        out_specs=[pl.BlockSpec((tm, d), row), pl.BlockSpec((tm, offs[-1]), row)],
        out_shape=[jax.ShapeDtypeStruct((t, d), F32), jax.ShapeDtypeStruct((t, offs[-1]), BF16)],
        compiler_params=_seq(1),
    )(dh_part, *pieces, dzg, w_main, w_lr)


def _pool_cnt(tile, tm, w):
    t = tile * tm + lax.broadcasted_iota(jnp.int32, (tm, 1), 0)
    return jnp.minimum(t + 1, w).astype(F32)


def _pool_fwd(u, wp, scale, name):
    t, pw = u.shape
    tm = _row_tile(t)
    gd = wp.shape[1]

    def body(u_ref, wp_ref, sc_ref, y_ref, p_ref, ext):
        i = pl.program_id(0)

        @pl.when(i == 0)
        def _():
            ext[0:POOL_HALO, :] = jnp.zeros((POOL_HALO, pw), F32)

        ext[POOL_HALO:POOL_HALO + tm, :] = u_ref[...]
        for gi, w in enumerate(POOL_WINDOWS):
            cols = slice(gi * gd, (gi + 1) * gd)
            s = ext[pl.ds(POOL_HALO, tm), cols]
            tot = s
            for back in range(1, w):
                tot = tot + ext[pl.ds(POOL_HALO - back, tm), cols]
            p = (tot / _pool_cnt(i, tm, w) - s).astype(BF16)
            p_ref[:, cols] = p
            y_ref[:, cols] = (jnp.dot(p, wp_ref[gi], preferred_element_type=F32) * sc_ref[:, cols]).astype(BF16)
        ext[0:POOL_HALO, :] = ext[tm:tm + POOL_HALO, :]

    row = lambda i: (i, 0)
    return _tc_call(
        body, name=name, grid=(t // tm,),
        in_specs=[pl.BlockSpec((tm, pw), row), pl.BlockSpec(wp.shape, lambda i: (0, 0, 0)),
                  pl.BlockSpec((1, pw), lambda i: (0, 0))],
        out_specs=[pl.BlockSpec((tm, pw), row), pl.BlockSpec((tm, pw), row)],
        out_shape=[jax.ShapeDtypeStruct((t, pw), BF16), jax.ShapeDtypeStruct((t, pw), BF16)],
        scratch_shapes=[pltpu.VMEM((tm + POOL_HALO, pw), F32)],
        compiler_params=_seq(1),
    )(u, wp, scale)


def _pool_bwd(dy, pb, wp, scale, name):
    t, pw = dy.shape
    tm = _row_tile(t)
    nt = t // tm
    gd = wp.shape[1]

    def body(dy_ref, p_ref, wp_ref, sc_ref, du_ref, dwp_ref, dsc_ref, ext):
        i = pl.program_id(0)
        tile = nt - 1 - i

        @pl.when(i == 0)
        def _():
            ext[tm:tm + POOL_HALO, :] = jnp.zeros((POOL_HALO, pw), F32)
            dwp_ref[...] = jnp.zeros_like(dwp_ref)
            dsc_ref[...] = jnp.zeros_like(dsc_ref)

        dps = []
        for gi, w in enumerate(POOL_WINDOWS):
            cols = slice(gi * gd, (gi + 1) * gd)
            dyv = dy_ref[:, cols]
            p = p_ref[:, cols]
            dpre = (dyv * sc_ref[:, cols]).astype(BF16)
            dsc_ref[:, cols] += jnp.sum(dyv * jnp.dot(p, wp_ref[gi], preferred_element_type=F32), axis=0, keepdims=True)
            dwp_ref[gi] += _mm_tn(p, dpre)
            dp = _mm_nt(dpre, wp_ref[gi])
            dps.append(dp)
            ext[0:tm, cols] = dp / _pool_cnt(tile, tm, w)
        for gi, w in enumerate(POOL_WINDOWS):
            cols = slice(gi * gd, (gi + 1) * gd)
            tot = ext[pl.ds(0, tm), cols]
            for fwd in range(1, w):
                tot = tot + ext[pl.ds(fwd, tm), cols]
            du_ref[:, cols] = (tot - dps[gi]).astype(BF16)
        ext[tm:tm + POOL_HALO, :] = ext[0:POOL_HALO, :]

    row = lambda i: (nt - 1 - i, 0)
    return _tc_call(
        body, name=name, grid=(nt,),
        in_specs=[pl.BlockSpec((tm, pw), row), pl.BlockSpec((tm, pw), row),
                  pl.BlockSpec(wp.shape, lambda i: (0, 0, 0)), pl.BlockSpec((1, pw), lambda i: (0, 0))],
        out_specs=[pl.BlockSpec((tm, pw), row), pl.BlockSpec(wp.shape, lambda i: (0, 0, 0)),
                   pl.BlockSpec((1, pw), lambda i: (0, 0))],
        out_shape=[jax.ShapeDtypeStruct((t, pw), BF16), jax.ShapeDtypeStruct(wp.shape, F32),
                   jax.ShapeDtypeStruct((1, pw), F32)],
        scratch_shapes=[pltpu.VMEM((tm + POOL_HALO, pw), F32)],
        compiler_params=_seq(1),
    )(dy, pb, wp, scale)


def _gla_masks(kw, vw):
    dk, dv = kw // N_HEADS, vw // N_HEADS
    lane_k = lax.broadcasted_iota(jnp.int32, (1, kw), 1)
    lane_v = lax.broadcasted_iota(jnp.int32, (1, vw), 1)
    hk = [((lane_k >= h * dk) & (lane_k < (h + 1) * dk)).astype(F32) for h in range(N_HEADS)]
    hv = [((lane_v >= h * dv) & (lane_v < (h + 1) * dv)).astype(F32) for h in range(N_HEADS)]
    r = lax.broadcasted_iota(jnp.int32, (CHUNK, CHUNK), 0)
    c = lax.broadcasted_iota(jnp.int32, (CHUNK, CHUNK), 1)
    tril = r >= c
    rs = lax.broadcasted_iota(jnp.int32, (N_HEADS * CHUNK, CHUNK), 0) & (CHUNK - 1)
    stril = rs >= lax.broadcasted_iota(jnp.int32, (N_HEADS * CHUNK, CHUNK), 1)
    return hk, hv, tril, stril


def _block_diag(x, hk, dv):
    return jnp.concatenate([x[h * dv:(h + 1) * dv, :] * hk[h] for h in range(N_HEADS)], axis=0)


def _gla_fwd(q, k, v, loga, r, gnorm, name):
    t, kw = q.shape
    vw = v.shape[1]
    dk, dv = kw // N_HEADS, vw // N_HEADS
    tm = _row_tile(t)
    nc = tm // CHUNK
    qscale = dk ** -0.5

    def body(q_ref, k_ref, v_ref, la_ref, r_ref, gn_ref, o_ref, y_ref, sall_ref, st):
        @pl.when(pl.program_id(0) == 0)
        def _():
            st[...] = jnp.zeros_like(st)

        hk, hv, tril, stril = _gla_masks(kw, vw)
        trif = tril.astype(F32)

        def chunk(c, carry):
            rows = pl.ds(pl.multiple_of(c * CHUNK, CHUNK), CHUNK)
            la = la_ref[rows, :]
            b = _mm_f32(trif, la)
            bl = jnp.sum(la, axis=0, keepdims=True)
            qb = q_ref[rows, :] * (qscale * jnp.exp(b))
            kk = k_ref[rows, :]
            kb = kk * jnp.exp(-b)
            kl = kk * jnp.exp(bl - b)
            vv = v_ref[rows, :]
            s_t = st[...]
            compact = s_t[0:dv, :]
            for h in range(1, N_HEADS):
                compact = compact + s_t[h * dv:(h + 1) * dv, :]
            sall_ref[c] = compact
            qx = jnp.concatenate([qb * hk[h] for h in range(N_HEADS)], axis=0)
            a = jnp.where(stril, _mm_nt(qx, kb), 0.0).astype(BF16)
            o_inter = _mm_nt(qb, s_t)
            for h in range(N_HEADS):
                vs = slice(h * dv, (h + 1) * dv)
                o_ref[rows, vs] = o_inter[:, vs] + _mm(a[h * CHUNK:(h + 1) * CHUNK, :], vv[:, vs])
            st[...] = s_t * jnp.exp(bl) + _block_diag(_mm_tn(vv, kl), hk, dv)
            return carry

        lax.fori_loop(0, nc, chunk, 0, unroll=CHUNK_UNROLL)
        for h in range(N_HEADS):
            vs = slice(h * dv, (h + 1) * dv)
            oh = o_ref[:, vs]
            on = oh * lax.rsqrt(jnp.mean(oh * oh, axis=-1, keepdims=True) + RMS_EPS)
            _, sl = _silu_parts(r_ref[:, vs])
            y_ref[:, vs] = (on * gn_ref[:, vs] * sl).astype(BF16)

    row = lambda i: (i, 0)
    return _tc_call(
        body, name=name, grid=(t // tm,),
        in_specs=[pl.BlockSpec((tm, kw), row), pl.BlockSpec((tm, kw), row), pl.BlockSpec((tm, vw), row),
                  pl.BlockSpec((tm, kw), row), pl.BlockSpec((tm, vw), row), pl.BlockSpec((1, vw), lambda i: (0, 0))],
        out_specs=[pl.BlockSpec((tm, vw), row), pl.BlockSpec((tm, vw), row),
                   pl.BlockSpec((nc, dv, kw), lambda i: (i, 0, 0))],
        out_shape=[jax.ShapeDtypeStruct((t, vw), F32), jax.ShapeDtypeStruct((t, vw), BF16),
                   jax.ShapeDtypeStruct((t // CHUNK, dv, kw), F32)],
        scratch_shapes=[pltpu.VMEM((vw, kw), F32)],
        compiler_params=_seq(1),
    )(q, k, v, loga, r, gnorm)


def _gla_bwd(dy, o, r, gnorm, q, k, v, loga, zg, sall, wgu, name):
    t, kw = q.shape
    vw = v.shape[1]
    dk, dv = kw // N_HEADS, vw // N_HEADS
    tm = _row_tile(t)
    nt = t // tm
    nc = tm // CHUNK
    qscale = dk ** -0.5

    def body(dy_ref, o_ref, r_ref, gn_ref, q_ref, k_ref, v_ref, la_ref, zg_ref, sall_ref, wgu_ref,
             dq_ref, dk_ref, dv_ref, dr_ref, dzg_ref, dwgu_ref, dbg_ref, dgn_ref, dst, do_s):
        @pl.when(pl.program_id(0) == 0)
        def _():
            dst[...] = jnp.zeros_like(dst)
            dwgu_ref[...] = jnp.zeros_like(dwgu_ref)
            dbg_ref[...] = jnp.zeros_like(dbg_ref)
            dgn_ref[...] = jnp.zeros_like(dgn_ref)

        for h in range(N_HEADS):
            vs = slice(h * dv, (h + 1) * dv)
            oh = o_ref[:, vs]
            rinv = lax.rsqrt(jnp.mean(oh * oh, axis=-1, keepdims=True) + RMS_EPS)
            on = oh * rinv
            rr = r_ref[:, vs]
            sg, sl = _silu_parts(rr)
            dyv = dy_ref[:, vs]
            gn = gn_ref[:, vs]
            dgn_ref[:, vs] += jnp.sum(dyv * on * sl, axis=0, keepdims=True)
            dr_ref[:, vs] = (dyv * on * gn * (sg * (1.0 + rr * (1.0 - sg)))).astype(BF16)
            don = dyv * gn * sl
            do_s[:, vs] = rinv * (don - on * jnp.mean(don * on, axis=-1, keepdims=True))

        hk, hv, tril, stril = _gla_masks(kw, vw)
        trif = tril.astype(F32)
        triuf = (lax.broadcasted_iota(jnp.int32, (CHUNK, CHUNK), 0)
                 <= lax.broadcasted_iota(jnp.int32, (CHUNK, CHUNK), 1)).astype(F32)
        last_row = lax.broadcasted_iota(jnp.int32, (CHUNK, 1), 0) == CHUNK - 1

        def chunk(idx, carry):
            c = nc - 1 - idx
            rows = pl.ds(pl.multiple_of(c * CHUNK, CHUNK), CHUNK)
            la = la_ref[rows, :]
            b = _mm_f32(trif, la)
            bl = jnp.sum(la, axis=0, keepdims=True)
            eb = jnp.exp(b)
            enb = jnp.exp(-b)
            ebl = jnp.exp(bl - b)
            el = jnp.exp(bl)
            qb = q_ref[rows, :] * (qscale * eb)
            kk = k_ref[rows, :]
            kb = kk * enb
            kl = kk * ebl
            vv = v_ref[rows, :]
            do = do_s[rows, :]
            compact = sall_ref[c]
            s_t = jnp.concatenate([compact * hk[h] for h in range(N_HEADS)], axis=0)
            ds_t = dst[...]
            qx = jnp.concatenate([qb * hk[h] for h in range(N_HEADS)], axis=0)
            dox = jnp.concatenate([do * hv[h] for h in range(N_HEADS)], axis=0)
            a = jnp.where(stril, _mm_nt(qx, kb), 0.0).astype(BF16)
            da = jnp.where(stril, _mm_nt(dox, vv), 0.0).astype(BF16)
            dv_ref[rows, :] = (_mm_tn(a, dox) + _mm_nt(kl, ds_t)).astype(BF16)
            dak = _mm(da, kb)
            dqb = _mm(do, s_t)
            for h in range(N_HEADS):
                dqb = dqb + dak[h * CHUNK:(h + 1) * CHUNK, :] * hk[h]
            dkb = _mm_tn(da, qx)
            dkl = _mm(vv, ds_t)
            dbl = jnp.sum(dkl * kl, axis=0, keepdims=True) + el * jnp.sum(ds_t * s_t, axis=0, keepdims=True)
            dst[...] = ds_t * el + _block_diag(_mm_tn(do, qb), hk, dv)
            dq_ref[rows, :] = (dqb * (qscale * eb)).astype(BF16)
            dk_ref[rows, :] = (dkb * enb + dkl * ebl).astype(BF16)
            db = dqb * qb - dkb * kb - dkl * kl + jnp.where(last_row, dbl, 0.0)
            dla = _mm_f32(triuf, db)
            dlogit = dla * (1.0 / GLA_GATE_TEMP) * (1.0 - jnp.exp(GLA_GATE_TEMP * la))
            dzg_ref[rows, :] = _mm_nt(dlogit, wgu_ref[...]).astype(BF16)
            dwgu_ref[...] += _mm_tn(zg_ref[rows, :], dlogit)
            dbg_ref[...] += jnp.sum(dlogit, axis=0, keepdims=True)
            return carry

        lax.fori_loop(0, nc, chunk, 0, unroll=CHUNK_UNROLL)

    row = lambda i: (nt - 1 - i, 0)
    const = lambda i: (0, 0)
    return _tc_call(
        body, name=name, grid=(nt,),
        in_specs=[pl.BlockSpec((tm, vw), row), pl.BlockSpec((tm, vw), row), pl.BlockSpec((tm, vw), row),
                  pl.BlockSpec((1, vw), const), pl.BlockSpec((tm, kw), row), pl.BlockSpec((tm, kw), row),
                  pl.BlockSpec((tm, vw), row), pl.BlockSpec((tm, kw), row), pl.BlockSpec((tm, LANE), row),
                  pl.BlockSpec((nc, dv, kw), lambda i: (nt - 1 - i, 0, 0)), pl.BlockSpec((LANE, kw), const)],
        out_specs=[pl.BlockSpec((tm, kw), row), pl.BlockSpec((tm, kw), row), pl.BlockSpec((tm, vw), row),
                   pl.BlockSpec((tm, vw), row), pl.BlockSpec((tm, LANE), row), pl.BlockSpec((LANE, kw), const),
                   pl.BlockSpec((1, kw), const), pl.BlockSpec((1, vw), const)],
        out_shape=[jax.ShapeDtypeStruct((t, kw), BF16), jax.ShapeDtypeStruct((t, kw), BF16),
                   jax.ShapeDtypeStruct((t, vw), BF16), jax.ShapeDtypeStruct((t, vw), BF16),
                   jax.ShapeDtypeStruct((t, LANE), BF16), jax.ShapeDtypeStruct((LANE, kw), F32),
                   jax.ShapeDtypeStruct((1, kw), F32), jax.ShapeDtypeStruct((1, vw), F32)],
        scratch_shapes=[pltpu.VMEM((vw, kw), F32), pltpu.VMEM((tm, vw), F32)],
        compiler_params=_seq(1),
    )(dy, o, r, gnorm, q, k, v, loga, zg, sall, wgu)


def _outproj_fwd(yp, yg, w_out, xhat, gam, bet, alpha, name):
    t, d = xhat.shape
    pw = yp.shape[1]
    tm = _row_tile(t)

    def body(yp_ref, yg_ref, w_ref, x_ref, g_ref, b_ref, xhat_ref, rstd_ref):
        for part in range(ROW_GROUPS):
            rows = pl.ds(part * (tm // ROW_GROUPS), tm // ROW_GROUPS)
            h = x_ref[rows, :] * g_ref[...] + b_ref[...]
            y = (jnp.dot(yp_ref[rows, :], w_ref[0:pw, :], preferred_element_type=F32)
                 + jnp.dot(yg_ref[rows, :], w_ref[pw:, :], preferred_element_type=F32))
            xh, rs = _ln_stats(alpha * h + y)
            xhat_ref[rows, :] = xh
            rstd_ref[rows, :] = rs

    row = lambda i: (i, 0)
    vec = pl.BlockSpec((1, d), lambda i: (0, 0))
    return _tc_call(
        body, name=name, grid=(t // tm,),
        in_specs=[pl.BlockSpec((tm, pw), row), pl.BlockSpec((tm, yg.shape[1]), row),
                  pl.BlockSpec(w_out.shape, lambda i: (0, 0)), pl.BlockSpec((tm, d), row), vec, vec],
        out_specs=[pl.BlockSpec((tm, d), row), pl.BlockSpec((tm, 1), row)],
        out_shape=[jax.ShapeDtypeStruct((t, d), F32), jax.ShapeDtypeStruct((t, 1), F32)],
        compiler_params=_seq(1),
    )(yp, yg, w_out, xhat, gam, bet)


def _outproj_bwd(dh, xhat, rstd, ln_g, w_out, pw, alpha, name):
    t, d = dh.shape
    tm = _row_tile(t)
    gw = w_out.shape[0] - pw

    def body(dh_ref, xh_ref, rs_ref, g_ref, w_ref, dyb_ref, dyp_ref, dyg_ref, dres_ref, dgam_ref, dbet_ref):
        @pl.when(pl.program_id(0) == 0)
        def _():
            dgam_ref[...] = jnp.zeros_like(dgam_ref)
            dbet_ref[...] = jnp.zeros_like(dbet_ref)

        for part in range(ROW_GROUPS):
            rows = pl.ds(part * (tm // ROW_GROUPS), tm // ROW_GROUPS)
            dy = dh_ref[rows, :]
            xh = xh_ref[rows, :]
            dr = _ln_bwd(dy, xh, rs_ref[rows, :], g_ref[...])
            dgam_ref[...] += jnp.sum(dy * xh, axis=0, keepdims=True)
            dbet_ref[...] += jnp.sum(dy, axis=0, keepdims=True)
            drb = dr.astype(BF16)
            dyb_ref[rows, :] = drb
            dres_ref[rows, :] = alpha * dr
            dyp_ref[rows, :] = _mm_nt(drb, w_ref[0:pw, :])
            dyg_ref[rows, :] = _mm_nt(drb, w_ref[pw:, :])

    row = lambda i: (i, 0)
    vec = pl.BlockSpec((1, d), lambda i: (0, 0))
    return _tc_call(
        body, name=name, grid=(t // tm,),
        in_specs=[pl.BlockSpec((tm, d), row), pl.BlockSpec((tm, d), row), pl.BlockSpec((tm, 1), row), vec,
                  pl.BlockSpec(w_out.shape, lambda i: (0, 0))],
        out_specs=[pl.BlockSpec((tm, d), row), pl.BlockSpec((tm, pw), row), pl.BlockSpec((tm, gw), row),
                   pl.BlockSpec((tm, d), row), vec, vec],
        out_shape=[jax.ShapeDtypeStruct((t, d), BF16), jax.ShapeDtypeStruct((t, pw), F32),
                   jax.ShapeDtypeStruct((t, gw), F32), jax.ShapeDtypeStruct((t, d), F32),
                   jax.ShapeDtypeStruct((1, d), F32), jax.ShapeDtypeStruct((1, d), F32)],
        compiler_params=_seq(1),
    )(dh, xhat, rstd, ln_g, w_out)


def _loss_head(xhat, gam, bet, target, n_rows, name):
    t, d = xhat.shape
    tm = _row_tile(t)

    def body(x_ref, g_ref, b_ref, t_ref, dy_ref, loss_ref):
        i = pl.program_id(0)

        @pl.when(i == 0)
        def _():
            loss_ref[...] = jnp.zeros_like(loss_ref)

        rowi = i * tm + lax.broadcasted_iota(jnp.int32, (tm, 1), 0)
        live = (rowi >= N_META) & (rowi < N_META + n_rows)
        diff = jnp.where(live, x_ref[...] * g_ref[...] + b_ref[...] - t_ref[...], 0.0)
        dy_ref[...] = diff * (1.0 / d)
        loss_ref[...] += jnp.sum(diff * diff) * (0.5 / d)

    row = lambda i: (i, 0)
    vec = pl.BlockSpec((1, d), lambda i: (0, 0))
    return _tc_call(
        body, name=name, grid=(t // tm,),
        in_specs=[pl.BlockSpec((tm, d), row), vec, vec, pl.BlockSpec((tm, d), row)],
        out_specs=[pl.BlockSpec((tm, d), row), pl.BlockSpec((8, LANE), lambda i: (0, 0))],
        out_shape=[jax.ShapeDtypeStruct((t, d), F32), jax.ShapeDtypeStruct((8, LANE), F32)],
        compiler_params=_seq(1),
    )(xhat, gam, bet, target)


def _rows_block(r, c):
    best = r
    for cand in range(BF16_ROWS, r, BF16_ROWS):
        if r % cand == 0 and cand * c * 4 <= (1 << 20):
            best = cand
    return best if best * c * 4 <= (4 << 20) else r


def _sum_slots(recvs, name, layers_side_by_side=False):
    nl = len(recvs)
    ns, r, c = recvs[0].shape
    tr = _rows_block(r, c)

    def body(*refs):
        o_ref = refs[nl]
        for l in range(nl):
            acc = refs[l][0].astype(F32)
            for s in range(1, ns):
                acc = acc + refs[l][s].astype(F32)
            if layers_side_by_side:
                o_ref[0, :, l * c:(l + 1) * c] = acc.astype(o_ref.dtype)
            else:
                o_ref[l] = acc.astype(o_ref.dtype)

    out = (1, r, nl * c) if layers_side_by_side else (nl, r, c)
    return _tc_call(
        body, name=name, grid=(r // tr,),
        in_specs=[pl.BlockSpec((ns, tr, c), lambda i: (0, i, 0))] * nl,
        out_specs=pl.BlockSpec((out[0], tr, out[2]), lambda i: (0, i, 0)),
        out_shape=jax.ShapeDtypeStruct(out, recvs[0].dtype),
        compiler_params=_seq(1),
    )(*recvs)


def _adamw(w, terms, m, v, name):
    nl, r, c = w.shape
    tc = c
    while tc % (2 * LANE) == 0 and tc > 4 * LANE:
        tc //= 2
    tr = _rows_block(r, tc)
    nterm = len(terms)

    def body(*refs):
        w_ref = refs[0]
        t_refs = refs[1:1 + nterm]
        m_ref, v_ref, g_ref, d_ref, nm_ref, nv_ref = refs[1 + nterm:]
        g = t_refs[0][...].astype(F32)
        for tr_ in t_refs[1:]:
            g = g + tr_[...].astype(F32)
        nm = ADAM_B1 * m_ref[...] + (1.0 - ADAM_B1) * g
        nv = ADAM_B2 * v_ref[...] + (1.0 - ADAM_B2) * jnp.square(g)
        m_hat = nm / (1.0 - ADAM_B1 ** ADAM_STEP)
        v_hat = nv / (1.0 - ADAM_B2 ** ADAM_STEP)
        g_ref[...] = g
        d_ref[...] = -ADAM_LR * (m_hat / (jnp.sqrt(v_hat) + ADAM_EPS) + ADAM_WD * w_ref[...])
        nm_ref[...] = nm
        nv_ref[...] = nv

    spec = pl.BlockSpec((None, tr, tc), lambda l, i, j: (l, i, j))
    shp = jax.ShapeDtypeStruct((nl, r, c), F32)
    return _tc_call(
        body, name=name, grid=(nl, r // tr, c // tc),
        in_specs=[spec] * (3 + nterm), out_specs=[spec] * 4, out_shape=[shp] * 4,
        compiler_params=_seq(3),
    )(w, *terms, m, v)


XY_RELATIONS = ((1, 0, 0), (0, 1, 0), (1, 1, 0))
ALL_RELATIONS = tuple((fx, fy, fc) for fx in (0, 1) for fy in (0, 1) for fc in (0, 1) if fx or fy or fc)
HBM_SPEC = pl.BlockSpec(memory_space=pltpu.HBM)
SEM_SPEC = pl.BlockSpec(memory_space=pltpu.SEMAPHORE)
DATAFLOW = pltpu.SideEffectType.DATAFLOW_SIDE_EFFECTING


def _split_call(body, **kw):
    return pl.pallas_call(body, **kw)


def _flip(v, f):
    return 1 - v if f else v


def _any_spec(n):
    return [pl.BlockSpec(memory_space=pl.ANY)] * n


def _relations(kind):
    return ALL_RELATIONS if kind == "bcast" else XY_RELATIONS


def _copies(kind, arr, land, sems):
    x, y, c = lax.axis_index("x"), lax.axis_index("y"), lax.axis_index("c")
    out = []
    for (fx, fy, fc), (send_sem, recv_sem) in zip(_relations(kind), sems):
        px, py, pc = _flip(x, fx), _flip(y, fy), _flip(c, fc)
        if kind == "bcast":
            mine, theirs = 4 * x + 2 * y + c, 4 * px + 2 * py + pc
        else:
            mine, theirs = 2 * x + y, 2 * px + py
        src, to_mine, to_theirs = arr, land.at[mine], land.at[theirs]
        if kind == "scatter":
            src = arr.at[theirs]
        if kind == "gather_half":
            rows = _my_half(arr.shape[0], c)
            src, to_mine, to_theirs = arr.at[rows], land.at[mine, rows], land.at[theirs, rows]
        both = dict(src_ref=src, send_sem=send_sem, recv_sem=recv_sem, device_id=(px, py, pc), device_id_type=MESH)
        out.append((pltpu.make_async_remote_copy(dst_ref=to_mine, **both),
                    pltpu.make_async_remote_copy(dst_ref=to_theirs, **both)))
    return out


def _my_half(nrows, c):
    return pl.ds(c * (nrows // 2), nrows // 2)


def _share_halves(name, kinds, lands):
    ks = [k for k, kd in enumerate(kinds) if kd == "gather_half"]
    n = len(ks)

    def body(*refs):
        l_refs = refs[n:2 * n]
        send_sems, recv_sems = refs[2 * n:]
        x, y, c = lax.axis_index("x"), lax.axis_index("y"), lax.axis_index("c")
        copies = []
        for i in range(n):
            nrows = l_refs[i].shape[1]
            for r, (fx, fy, _) in enumerate(XY_RELATIONS):
                slot = 2 * _flip(x, fx) + _flip(y, fy)
                both = dict(src_ref=l_refs[i].at[slot, _my_half(nrows, c)], send_sem=send_sems.at[i, r],
                            recv_sem=recv_sems.at[i, r], device_id=(x, y, 1 - c), device_id_type=MESH)
                copies.append((pltpu.make_async_remote_copy(dst_ref=l_refs[i].at[slot, _my_half(nrows, c)], **both),
                               pltpu.make_async_remote_copy(dst_ref=l_refs[i].at[slot, _my_half(nrows, 1 - c)], **both)))
        for send, _ in copies:
            send.start()
        for _, arrival in copies:
            arrival.wait_recv()
        for send, _ in copies:
            send.wait_send()

    outs = _comm_call(
        body, name=name,
        in_specs=_any_spec(n), out_specs=_any_spec(n),
        out_shape=[jax.ShapeDtypeStruct(lands[k].shape, lands[k].dtype) for k in ks],
        input_output_aliases={i: i for i in range(n)},
        scratch_shapes=[pltpu.SemaphoreType.DMA((n, 3)), pltpu.SemaphoreType.DMA((n, 3))],
    )(*[lands[k] for k in ks])
    lands = list(lands)
    for k, o in zip(ks, outs):
        lands[k] = o
    return lands


def _sem_pairs(kinds, sems):
    out, at = [], 0
    for kind in kinds:
        nrel = len(_relations(kind))
        out.append([(sems[at + 2 * r], sems[at + 2 * r + 1]) for r in range(nrel)])
        at += 2 * nrel
    return out


def _exchange_start(name, kinds, arrs, lands):
    n = len(arrs)
    nsem = sum(2 * len(_relations(kd)) for kd in kinds)

    def body(*refs):
        a_refs, l_refs = refs[:n], refs[n:2 * n]
        pairs = _sem_pairs(kinds, refs[2 * n:2 * n + nsem])
        token = refs[-1]
        for k in range(n):
            for send, _ in _copies(kinds[k], a_refs[k], l_refs[k], pairs[k]):
                send.start()
        token[...] = jnp.zeros_like(token)

    thru = [pltpu.HBM(a.shape, a.dtype) for a in list(arrs) + list(lands)]
    outs = _split_call(
        body, name=name,
        out_shape=(*[pltpu.SemaphoreType.DMA(())] * nsem, *thru, jax.ShapeDtypeStruct((8, LANE), F32)),
        in_specs=[HBM_SPEC] * (2 * n),
        out_specs=(*[SEM_SPEC] * nsem, *[HBM_SPEC] * (2 * n), pl.BlockSpec(memory_space=pltpu.VMEM)),
        input_output_aliases={i: nsem + i for i in range(2 * n)},
        compiler_params=pltpu.CompilerParams(has_side_effects=DATAFLOW),
    )(*[pltpu.with_memory_space_constraint(a, pltpu.HBM) for a in list(arrs) + list(lands)])
    return dict(kinds=kinds, sems=outs[:nsem], arrs=outs[nsem:nsem + n], lands=outs[nsem + n:nsem + 2 * n],
                token=outs[-1])


def _exchange_wait(name, st, after):
    kinds = st["kinds"]
    n = len(kinds)
    nsem = len(st["sems"])

    def body(*refs):
        a_refs, l_refs = refs[:n], refs[n:2 * n]
        pairs = _sem_pairs(kinds, refs[2 * n:2 * n + nsem])
        for k in range(n):
            for _, arrival in _copies(kinds[k], a_refs[k], l_refs[k], pairs[k]):
                arrival.wait_send()
                arrival.wait_recv()
        refs[-1][...] = jnp.zeros_like(refs[-1])

    ins = list(st["arrs"]) + list(st["lands"])
    outs = _split_call(
        body, name=name,
        out_shape=[pltpu.HBM(a.shape, a.dtype) for a in ins] + [jax.ShapeDtypeStruct((8, LANE), F32)],
        in_specs=[HBM_SPEC] * (2 * n) + [SEM_SPEC] * nsem + [pl.BlockSpec(memory_space=pl.ANY)],
        out_specs=[HBM_SPEC] * (2 * n) + [pl.BlockSpec(memory_space=pltpu.VMEM)],
        input_output_aliases={i: i for i in range(2 * n)},
        compiler_params=pltpu.CompilerParams(has_side_effects=DATAFLOW),
    )(*ins, *st["sems"], after)
    return outs[n:2 * n], outs[-1]


def _landing(own, slot, nslot):
    return lax.dynamic_update_slice(lax.empty((nslot,) + own.shape, own.dtype), own[None], (slot,) + (0,) * own.ndim)


def _swap_sibling(parts, name):
    n = len(parts)

    def body(*refs):
        ins, outs = refs[:n], refs[n:2 * n]
        send_sems, recv_sems = refs[2 * n:]
        sib = (lax.axis_index("x"), lax.axis_index("y"), 1 - lax.axis_index("c"))
        cps = [pltpu.make_async_remote_copy(src_ref=ins[k], dst_ref=outs[k], send_sem=send_sems.at[k],
                                            recv_sem=recv_sems.at[k], device_id=sib, device_id_type=MESH)
               for k in range(n)]
        for cp in cps:
            cp.start()
        for cp in cps:
            cp.wait_recv()
        for cp in cps:
            cp.wait_send()

    return _comm_call(
        body, name=name,
        in_specs=_any_spec(n), out_specs=_any_spec(n),
        out_shape=[jax.ShapeDtypeStruct(p.shape, p.dtype) for p in parts],
        scratch_shapes=[pltpu.SemaphoreType.DMA((n,)), pltpu.SemaphoreType.DMA((n,))],
    )(*parts)


def _col_shards(a, n=N_SHARD):
    r, c = a.shape
    return a.reshape(r, n, c // n).transpose(1, 0, 2)


def _from_col_shards(a):
    n, r, cs = a.shape
    return a.transpose(1, 0, 2).reshape(r, n * cs)


def kernel(x, meta_tokens, ffn1_w_gate, ffn1_w_up, ffn1_w_down, ln1_g, ln1_b, w_in, w_gate_up, b_gate, w_pool, pool_scale, gla_norm_g, w_out, ln2_g, ln2_b, ffn2_w_gate, ffn2_w_up, ffn2_w_down, ln3_g, ln3_b, loss_target, m_meta_tokens, m_ffn1_w_gate, m_ffn1_w_up, m_ffn1_w_down, m_ln1_g, m_ln1_b, m_w_in, m_w_gate_up, m_b_gate, m_w_pool, m_pool_scale, m_gla_norm_g, m_w_out, m_ln2_g, m_ln2_b, m_ffn2_w_gate, m_ffn2_w_up, m_ffn2_w_down, m_ln3_g, m_ln3_b, v_meta_tokens, v_ffn1_w_gate, v_ffn1_w_up, v_ffn1_w_down, v_ln1_g, v_ln1_b, v_w_in, v_w_gate_up, v_b_gate, v_w_pool, v_pool_scale, v_gla_norm_g, v_w_out, v_ln2_g, v_ln2_b, v_ffn2_w_gate, v_ffn2_w_up, v_ffn2_w_down, v_ln3_g, v_ln3_b):
    w = dict(meta_tokens=meta_tokens, ffn1_w_gate=ffn1_w_gate, ffn1_w_up=ffn1_w_up, ffn1_w_down=ffn1_w_down,
             ln1_g=ln1_g, ln1_b=ln1_b, w_in=w_in, w_gate_up=w_gate_up, b_gate=b_gate, w_pool=w_pool,
             pool_scale=pool_scale, gla_norm_g=gla_norm_g, w_out=w_out, ln2_g=ln2_g, ln2_b=ln2_b,
             ffn2_w_gate=ffn2_w_gate, ffn2_w_up=ffn2_w_up, ffn2_w_down=ffn2_w_down, ln3_g=ln3_g, ln3_b=ln3_b)
    mom1 = dict(meta_tokens=m_meta_tokens, ffn1_w_gate=m_ffn1_w_gate, ffn1_w_up=m_ffn1_w_up,
                ffn1_w_down=m_ffn1_w_down, ln1_g=m_ln1_g, ln1_b=m_ln1_b, w_in=m_w_in, w_gate_up=m_w_gate_up,
                b_gate=m_b_gate, w_pool=m_w_pool, pool_scale=m_pool_scale, gla_norm_g=m_gla_norm_g, w_out=m_w_out,
                ln2_g=m_ln2_g, ln2_b=m_ln2_b, ffn2_w_gate=m_ffn2_w_gate, ffn2_w_up=m_ffn2_w_up,
                ffn2_w_down=m_ffn2_w_down, ln3_g=m_ln3_g, ln3_b=m_ln3_b)
    mom2 = dict(meta_tokens=v_meta_tokens, ffn1_w_gate=v_ffn1_w_gate, ffn1_w_up=v_ffn1_w_up,
                ffn1_w_down=v_ffn1_w_down, ln1_g=v_ln1_g, ln1_b=v_ln1_b, w_in=v_w_in, w_gate_up=v_w_gate_up,
                b_gate=v_b_gate, w_pool=v_w_pool, pool_scale=v_pool_scale, gla_norm_g=v_gla_norm_g, w_out=v_w_out,
                ln2_g=v_ln2_g, ln2_b=v_ln2_b, ffn2_w_gate=v_ffn2_w_gate, ffn2_w_up=v_ffn2_w_up,
                ffn2_w_down=v_ffn2_w_down, ln3_g=v_ln3_g, ln3_b=v_ln3_b)

    xs = x[0]
    s_len, d = xs.shape
    nl = ln1_g.shape[0]
    alpha = (2.0 * nl) ** 0.25
    t_real = N_META + s_len
    t_pad = -(-t_real // LANE) * LANE
    pw = pool_scale.shape[1]
    kw = b_gate.shape[1]
    vw = gla_norm_g.shape[1]
    rank = w_gate_up.shape[1]
    widths = (pw, kw, kw, vw, vw)
    n_main = sum(widths)
    dff_s = ffn1_w_gate.shape[2]
    dff_c = N_SHARD * dff_s // FFN_CHUNKS

    me_xy = 2 * lax.axis_index("x") + lax.axis_index("y")
    me_all = 2 * me_xy + lax.axis_index("c")
    ffn1_names = ("ffn1_w_gate", "ffn1_w_up", "ffn1_w_down")
    mix_names = ("w_out", "w_gate_up", "w_in")
    ffn2_names = ("ffn2_w_gate", "ffn2_w_up", "ffn2_w_down")

    gate_up = ("ffn1_w_gate", "ffn1_w_up", "ffn2_w_gate", "ffn2_w_up")

    def stored(n, a):
        if n in gate_up:
            return jnp.swapaxes(a, 1, 2)
        return jnp.transpose(a, (2, 0, 1)) if n == "w_in" else a

    def as_given(n, a):
        if n in gate_up:
            return jnp.swapaxes(a, 1, 2)
        if n == "w_in":
            return jnp.transpose(a.reshape(-1, nl, d), (1, 2, 0))
        return a.reshape(w[n].shape)

    stages = [[("meta_tokens", None)], [(n, 0) for n in ffn1_names], [(n, 0) for n in mix_names + ffn2_names]]
    stages += [[(n, l) for n in BIG] for l in range(1, nl)]
    gathers, wa = {}, {}

    halved = ffn1_names + ffn2_names + ("w_out",)

    def start_gather(si, dep=None):
        own = []
        for n, l in stages[si]:
            a = meta_tokens if l is None else (stored(n, w[n])[:, l] if n == "w_in" else stored(n, w[n])[l])
            a = a if dep is None else a + dep
            own.append(a if l is None else a.astype(BF16))
        gathers[si] = _exchange_start(f"gather_start_{si}", ["gather_half" if n in halved else "gather"
                                                             for n, _ in stages[si]], own,
                                      [_landing(a, me_xy, N_SHARD) for a in own])
        return gathers[si]["token"]

    def arrive(si, after):
        lands, token = _exchange_wait(f"gather_wait_{si}", gathers[si], after)
        if any(kd == "gather_half" for kd in gathers[si]["kinds"]):
            lands = _share_halves(f"gather_share_{si}", gathers[si]["kinds"], lands)
        for item, a in zip(stages[si], lands):
            wa[item] = a.reshape(FFN_CHUNKS, -1, d) if item[0] in ffn1_names + ffn2_names else a
        return token

    def mixer_weights(l):
        wi = wa["w_in", l].reshape(-1, d)
        return dict(w_main=wi[:n_main], w_lr=jnp.pad(wi[n_main:], ((0, LANE - rank), (0, 0))),
                    wgu=jnp.pad(_from_col_shards(wa["w_gate_up", l]), ((0, LANE - rank), (0, 0))),
                    wout=wa["w_out", l].reshape(-1, d))

    wp16 = w_pool.astype(BF16)
    ones = jnp.ones((1, d), F32)
    zeros = jnp.zeros((1, d), F32)
    target = jnp.concatenate([jnp.zeros((N_META, d), F32), loss_target[0], jnp.zeros((t_pad - t_real, d), F32)], axis=0)

    started = start_gather(0)
    for si in range(1, len(stages)):
        started = start_gather(si, started[0:1, 0:1])
    arrive(0, started)
    meta_full = _from_col_shards(wa["meta_tokens", None])
    h0 = jnp.concatenate([meta_full, xs, jnp.zeros((t_pad - t_real, d), F32)], axis=0)
    arrive(1, h0[:8, :LANE] + target[:8, :LANE])

    saved, mw = [], []
    cur, cur_g, cur_b = h0, ones, zeros
    for l in range(nl):
        s = {}
        xh1, rs1, hb0, g1, u1 = _ffn_fwd(cur, cur_g, cur_b, wa["ffn1_w_gate", l], wa["ffn1_w_up", l],
                                         wa["ffn1_w_down", l], alpha, f"ffn1_fwd_{l}")
        if l == 0:
            arrive(2, xh1)
        mw.append(mixer_weights(l))
        up, q, k, v, r, zg, la, hb1 = _inproj_fwd(xh1, ln1_g[l:l + 1], ln1_b[l:l + 1], mw[l]["w_main"], mw[l]["w_lr"],
                                                  mw[l]["wgu"], b_gate[l:l + 1], widths, f"inproj_fwd_{l}")
        yp, pb = _pool_fwd(up, wp16[l], pool_scale[l:l + 1], f"pool_fwd_{l}")
        o, yg, sall = _gla_fwd(q, k, v, la, r, gla_norm_g[l:l + 1], f"gla_fwd_{l}")
        xh2, rs2 = _outproj_fwd(yp, yg, mw[l]["wout"], xh1, ln1_g[l:l + 1], ln1_b[l:l + 1], alpha, f"outproj_fwd_{l}")
        if l + 1 < nl:
            arrive(l + 3, xh2)
        xh3, rs3, hb2, g2, u2 = _ffn_fwd(xh2, ln2_g[l:l + 1], ln2_b[l:l + 1], wa["ffn2_w_gate", l], wa["ffn2_w_up", l],
                                         wa["ffn2_w_down", l], alpha, f"ffn2_fwd_{l}")
        s.update(xh1=xh1, rs1=rs1, hb0=hb0, g1=g1, u1=u1, q=q, k=k, v=v, r=r, zg=zg, la=la, hb1=hb1, yp=yp, pb=pb,
                 o=o, yg=yg, sall=sall, xh2=xh2, rs2=rs2, xh3=xh3, rs3=rs3, hb2=hb2, g2=g2, u2=u2)
        saved.append(s)
        cur, cur_g, cur_b = xh3, ln3_g[l:l + 1], ln3_b[l:l + 1]

    dh, loss_acc = _loss_head(cur, cur_g, cur_b, target, s_len, "loss_head")
    loss = lax.psum(loss_acc[0, 0], ("x", "y", "c"))

    small_grads = {n: [None] * nl for n in SMALL}
    scatters = []

    def depart(name, items, grads, kinds=None):
        lands = [_landing(g if kd == "bcast" else lax.dynamic_index_in_dim(g, me_xy, 0, keepdims=False),
                          me_all if kd == "bcast" else me_xy, N_DEV if kd == "bcast" else N_SHARD)
                 for g, kd in zip(grads, kinds or ["scatter"] * len(grads))]
        st = _exchange_start(name, kinds or ["scatter"] * len(grads), grads, lands)
        scatters.append((name, items, st))
        return st["token"]

    def pack(parts):
        flat = jnp.concatenate([parts[n].reshape(-1) for n in SMALL])
        return flat.reshape(-1, LANE)

    def ffn_wgrad(n, l, hb, dgb, dub, gb, ub, dfb, after=None):
        if n.endswith("down"):
            dw = _wgrad(gb, dfb, dff_c, d, f"{n}_grad_{l}", after, gate=ub)
        else:
            dw = _wgrad(dgb if n.endswith("gate") else dub, hb, dff_c, d, f"{n}_grad_{l}", after)
        return dw.reshape(N_SHARD, dff_s, d)

    late = []
    for l in reversed(range(nl)):
        s = saved[l]
        dh, dfb, dgb, dub, dgam, dbet = _ffn_bwd(dh, s["xh3"], s["rs3"], ln3_g[l:l + 1], s["g2"], s["u2"],
                                                 wa["ffn2_w_gate", l], wa["ffn2_w_up", l], wa["ffn2_w_down", l],
                                                 alpha, f"ffn2_bwd_{l}")
        small_grads["ln3_g"][l], small_grads["ln3_b"][l] = dgam, dbet
        gone = depart(f"scatter_start_ffn2_{l}", [(n, l) for n in ffn2_names],
                      [ffn_wgrad(n, l, s["hb2"], dgb, dub, s["g2"], s["u2"], dfb) for n in ffn2_names])

        dyb, dyp, dyg, dres, dgam, dbet = _outproj_bwd(dh, s["xh2"], s["rs2"], ln2_g[l:l + 1] + gone[0:1, 0:1],
                                                       mw[l]["wout"], pw, alpha, f"outproj_bwd_{l}")
        small_grads["ln2_g"][l], small_grads["ln2_b"][l] = dgam, dbet
        dwo = jnp.concatenate([_wgrad(s["yp"], dyb, pw, d, f"dwout_pool_{l}"),
                               _wgrad(s["yg"], dyb, vw, d, f"dwout_gla_{l}")], axis=0)
        dq, dk, dv, dr, dzg, dwgu, dbg, dgn = _gla_bwd(dyg, s["o"], s["r"], gla_norm_g[l:l + 1], s["q"], s["k"],
                                                       s["v"], s["la"], s["zg"], s["sall"], mw[l]["wgu"],
                                                       f"gla_bwd_{l}")
        dup, dwp, dsc = _pool_bwd(dyp, s["pb"], wp16[l], pool_scale[l:l + 1], f"pool_bwd_{l}")
        small_grads["b_gate"][l], small_grads["gla_norm_g"][l] = dbg, dgn
        small_grads["w_pool"][l], small_grads["pool_scale"][l] = dwp, dsc
        dh, dz = _inproj_bwd(dres, [dup, dq, dk, dv, dr], dzg, mw[l]["w_main"], mw[l]["w_lr"], f"inproj_bwd_{l}")
        dwi = jnp.concatenate([_wgrad(dz, s["hb1"], 4 * LANE, d, f"dwin_main_{l}"),
                               _wgrad(dzg, s["hb1"], LANE, d, f"dwin_lr_{l}")[:rank]], axis=0)
        gone = depart(f"scatter_start_mix_{l}", [(n, l) for n in mix_names],
                      [dwo.reshape(N_SHARD, -1, d), _col_shards(dwgu[:rank]), dwi.reshape(N_SHARD, -1, d)])

        dh, dfb, dgb, dub, dgam, dbet = _ffn_bwd(dh, s["xh1"], s["rs1"], ln1_g[l:l + 1] + gone[0:1, 0:1],
                                                 s["g1"], s["u1"], wa["ffn1_w_gate", l], wa["ffn1_w_up", l],
                                                 wa["ffn1_w_down", l], alpha, f"ffn1_bwd_{l}")
        small_grads["ln1_g"][l], small_grads["ln1_b"][l] = dgam, dbet
        if l:
            gone = depart(f"scatter_start_ffn1_{l}", [(n, l) for n in ffn1_names],
                          [ffn_wgrad(n, l, s["hb0"], dgb, dub, s["g1"], s["u1"], dfb) for n in ffn1_names])
            ln3_g = ln3_g.at[l - 1:l].add(gone[0:1, 0:1])
            continue
        grad_x = dh[N_META:t_real][None]
        small_vec = pack({n: jnp.stack(small_grads[n]) for n in SMALL})
        gone = depart("scatter_start_rest", [("meta_tokens", 0), ("small", 0)],
                      [_col_shards(dh[:N_META]), small_vec], ["scatter", "bcast"])
        for n in ffn1_names:
            g = ffn_wgrad(n, l, s["hb0"], dgb, dub, s["g1"], s["u1"], dfb, after=gone)
            gone = depart(f"scatter_start_{n}", [(n, l)], [g])
            late.append(scatters.pop())

    recv, results, firsts = {}, {}, []

    def collect(group, after):
        for name, items, st in group:
            for item, a in zip(items, _exchange_wait(name.replace("start", "wait"), st, after)[0]):
                recv[item] = a

    def reduce_and_update(names, tag):
        partial = [_sum_slots([recv[n, l] for l in range(1 if n == "meta_tokens" else nl)], f"sum_{n}", n == "w_in")
                   for n in names]
        for n, mine, theirs in zip(names, partial, _swap_sibling(partial, f"swap_sibling_{tag}")):
            fit = lambda a: stored(n, a).reshape(mine.shape)
            outs = _adamw(fit(w[n]), [mine, theirs], fit(mom1[n]), fit(mom2[n]), f"adamw_{n}")
            results[n] = [as_given(n, o) for o in outs]
            firsts.append(outs[1][0, 0, 0])

    collect(scatters, gone)
    early = [n for n in ("meta_tokens",) + BIG if n not in ffn1_names]
    reduce_and_update(early, "early")
    small_terms = [recv["small", 0][i][None] for i in range(N_DEV)]
    souts = _adamw(pack(w)[None], small_terms, pack(mom1)[None], pack(mom2)[None], "adamw_small")
    off = 0
    for n in SMALL:
        size = w[n].size
        results[n] = [o.reshape(-1)[off:off + size].reshape(w[n].shape) for o in souts]
        off += size
    collect(late, souts[0][0, :8] + functools.reduce(jnp.add, firsts))
    reduce_and_update(ffn1_names, "late")

    out = [loss, grad_x]
    for part in range(4):
        out += [results[n][part] for n in WEIGHTS]
    return tuple(out)
```

```python
import functools

import jax
import jax.numpy as jnp
from jax import lax
from jax.experimental import pallas as pl
from jax.experimental.pallas import tpu as pltpu

F32 = jnp.float32
BF16 = jnp.bfloat16
MESH = pl.DeviceIdType.MESH

N_META = 16
POOL_WINDOWS = (2, 4, 8, 16)
POOL_HALO = 16
N_HEADS = 4
GLA_GATE_TEMP = 16.0
CHUNK = 128
CHUNK_UNROLL = 5
LN_EPS = 1e-5
RMS_EPS = 1e-6
ADAM_LR = 0.001
ADAM_B1 = 0.9
ADAM_B2 = 0.999
ADAM_EPS = 1e-08
ADAM_WD = 0.01
ADAM_STEP = 10
LANE = 128
BF16_ROWS = 16
ROW_TILE = 640
FFN_ROW_TILE = 640
FFN_CHUNKS = 4
ROW_GROUPS = 2
FFN_SPLIT = 2
WGRAD_K_MAX = 2176
N_SHARD = 4
N_DEV = 8

BIG = ("ffn1_w_gate", "ffn1_w_up", "ffn1_w_down", "w_in", "w_gate_up", "w_out",
       "ffn2_w_gate", "ffn2_w_up", "ffn2_w_down")
SMALL = ("ln1_g", "ln1_b", "b_gate", "w_pool", "pool_scale", "gla_norm_g", "ln2_g", "ln2_b", "ln3_g", "ln3_b")
WEIGHTS = ("meta_tokens", "ffn1_w_gate", "ffn1_w_up", "ffn1_w_down", "ln1_g", "ln1_b", "w_in", "w_gate_up",
           "b_gate", "w_pool", "pool_scale", "gla_norm_g", "w_out", "ln2_g", "ln2_b", "ffn2_w_gate",
           "ffn2_w_up", "ffn2_w_down", "ln3_g", "ln3_b")


def _tc_call(body, **kw):
    return pl.pallas_call(body, **kw)


def _comm_call(body, **kw):
    return pl.pallas_call(body, **kw)


def _seq(n):
    return pltpu.CompilerParams(dimension_semantics=("arbitrary",) * n)


def _mm(a, b):
    return jnp.dot(a.astype(BF16), b.astype(BF16), preferred_element_type=F32)


def _mm_nt(a, b):
    return lax.dot_general(a.astype(BF16), b.astype(BF16), (((1,), (1,)), ((), ())), preferred_element_type=F32)


def _mm_tn(a, b):
    return lax.dot_general(a.astype(BF16), b.astype(BF16), (((0,), (0,)), ((), ())), preferred_element_type=F32)


def _mm_f32(a, b):
    return jnp.dot(a, b, precision=lax.Precision.HIGHEST, preferred_element_type=F32)


def _row_tile(t, most=None):
    tm = min(most or ROW_TILE, t)
    while t % tm:
        tm -= LANE
    return tm


def _silu_parts(g):
    sg = jax.nn.sigmoid(g)
    return sg, g * sg


def _ln_stats(r):
    mu = jnp.mean(r, axis=-1, keepdims=True)
    rc = r - mu
    var = jnp.mean(rc * rc, axis=-1, keepdims=True)
    rs = lax.rsqrt(var + LN_EPS)
    return rc * rs, rs


def _ln_bwd(dy, xh, rs, gam):
    dyg = dy * gam
    c1 = jnp.mean(dyg, axis=-1, keepdims=True)
    c2 = jnp.mean(dyg * xh, axis=-1, keepdims=True)
    return rs * (dyg - c1 - xh * c2)


def _ffn_fwd(xin, gam_in, bet_in, wg, wu, wd, alpha, name):
    t, d = xin.shape
    nj, tf, _ = wg.shape
    tm = _row_tile(t, FFN_ROW_TILE)

    def body(x_ref, gi_ref, bi_ref, wg_ref, wu_ref, wd_ref, xhat_ref, rstd_ref, hb_ref, go_ref, uo_ref, acc, hbs):
        j = pl.program_id(1)

        @pl.when(j == 0)
        def _():
            h = x_ref[...] * gi_ref[...] + bi_ref[...]
            hb = h.astype(BF16)
            hbs[...] = hb
            hb_ref[...] = hb
            acc[...] = (2.0 * alpha) * h

        hb = hbs[...]
        g = _mm_nt(hb, wg_ref[...])
        u = _mm_nt(hb, wu_ref[...])
        _, sl = _silu_parts(g)
        go_ref[...] = g.astype(BF16)
        uo_ref[...] = u.astype(BF16)
        acc[...] += jnp.dot((sl * u).astype(BF16), wd_ref[...], preferred_element_type=F32)

        @pl.when(j == nj - 1)
        def _():
            xhat, rs = _ln_stats(0.5 * acc[...])
            xhat_ref[...] = xhat
            rstd_ref[...] = rs

    row = lambda i, j: (i, 0)
    vec = pl.BlockSpec((1, d), lambda i, j: (0, 0))
    return _tc_call(
        body, name=name, grid=(t // tm, nj),
        in_specs=[pl.BlockSpec((tm, d), row), vec, vec] + [pl.BlockSpec((None, tf, d), lambda i, j: (j, 0, 0))] * 3,
        out_specs=[pl.BlockSpec((tm, d), row), pl.BlockSpec((tm, 1), row), pl.BlockSpec((tm, d), row),
                   pl.BlockSpec((None, tm, tf), lambda i, j: (j, i, 0)),
                   pl.BlockSpec((None, tm, tf), lambda i, j: (j, i, 0))],
        out_shape=[jax.ShapeDtypeStruct((t, d), F32), jax.ShapeDtypeStruct((t, 1), F32),
                   jax.ShapeDtypeStruct((t, d), BF16), jax.ShapeDtypeStruct((nj, t, tf), BF16),
                   jax.ShapeDtypeStruct((nj, t, tf), BF16)],
        scratch_shapes=[pltpu.VMEM((tm, d), F32), pltpu.VMEM((tm, d), BF16)],
        compiler_params=_seq(2),
    )(xin, gam_in, bet_in, wg, wu, wd)


def _ffn_bwd(dh, xhat, rstd, ln_g, gb, ub, wg, wu, wd, alpha, name):
    t, d = dh.shape
    nj, tf, _ = wg.shape
    tm = _row_tile(t, FFN_ROW_TILE)

    def body(dh_ref, xh_ref, rs_ref, g_ref, gb_ref, ub_ref, wg_ref, wu_ref, wd_ref,
             dhin_ref, df_ref, dg_ref, du_ref, act_ref, dgam_ref, dbet_ref, df_s):
        i = pl.program_id(0)
        j = pl.program_id(1)

        @pl.when(j == 0)
        def _():
            dy = dh_ref[...]
            xh = xh_ref[...]
            dr = _ln_bwd(dy, xh, rs_ref[...], g_ref[...])
            dhin_ref[...] = alpha * dr
            dfb = (0.5 * dr).astype(BF16)
            df_s[...] = dfb
            df_ref[...] = dfb

            @pl.when(i == 0)
            def _():
                dgam_ref[...] = jnp.zeros_like(dgam_ref)
                dbet_ref[...] = jnp.zeros_like(dbet_ref)

            dgam_ref[...] += jnp.sum(dy * xh, axis=0, keepdims=True)
            dbet_ref[...] += jnp.sum(dy, axis=0, keepdims=True)

        for part in range(FFN_SPLIT):
            rows = pl.ds(part * (tm // FFN_SPLIT), tm // FFN_SPLIT)
            dact = _mm_nt(df_s[rows, :], wd_ref[...])
            g = gb_ref[rows, :].astype(F32)
            u = ub_ref[rows, :].astype(F32)
            sg, sl = _silu_parts(g)
            dg = (dact * u * (sg * (1.0 + g * (1.0 - sg)))).astype(BF16)
            du = (dact * sl).astype(BF16)
            dg_ref[rows, :] = dg
            du_ref[rows, :] = du
            act_ref[rows, :] = (sl * u).astype(BF16)
            dhin_ref[rows, :] += _mm(dg, wg_ref[...]) + _mm(du, wu_ref[...])

    row = lambda i, j: (i, 0)
    col = pl.BlockSpec((None, tm, tf), lambda i, j: (j, i, 0))
    vec = pl.BlockSpec((1, d), lambda i, j: (0, 0))
    ff = jax.ShapeDtypeStruct((nj, t, tf), BF16)
    return _tc_call(
        body, name=name, grid=(t // tm, nj),
        in_specs=[pl.BlockSpec((tm, d), row), pl.BlockSpec((tm, d), row), pl.BlockSpec((tm, 1), row), vec,
                  col, col] + [pl.BlockSpec((None, tf, d), lambda i, j: (j, 0, 0))] * 3,
        out_specs=[pl.BlockSpec((tm, d), row), pl.BlockSpec((tm, d), row), col, col, col, vec, vec],
        out_shape=[jax.ShapeDtypeStruct((t, d), F32), jax.ShapeDtypeStruct((t, d), BF16), ff, ff, ff,
                   jax.ShapeDtypeStruct((1, d), F32), jax.ShapeDtypeStruct((1, d), F32)],
        scratch_shapes=[pltpu.VMEM((tm, d), BF16)],
        compiler_params=_seq(2),
    )(dh, xhat, rstd, ln_g, gb, ub, wg, wu, wd)


def _wgrad(a, b, tmm, tn, name, after=None):
    t = a.shape[-2]
    m = a.shape[-1] * (a.shape[0] if a.ndim == 3 else 1)
    n = b.shape[-1] * (b.shape[0] if b.ndim == 3 else 1)
    tk = max(k for k in range(BF16_ROWS, WGRAD_K_MAX + 1, BF16_ROWS) if t % k == 0)
    nk = t // tk
    extra = [] if after is None else [after]

    def body(a_ref, b_ref, *rest):
        o_ref, acc = rest[len(extra):]
        k = pl.program_id(2)

        @pl.when(k == 0)
        def _():
            acc[...] = jnp.zeros_like(acc)

        acc[...] += _mm_tn(a_ref[...], b_ref[...])

        @pl.when(k == nk - 1)
        def _():
            o_ref[...] = acc[...].astype(o_ref.dtype)

    a_spec = (pl.BlockSpec((None, tk, tmm), lambda i, j, k: (i, k, 0)) if a.ndim == 3
              else pl.BlockSpec((tk, tmm), lambda i, j, k: (k, i)))
    return _tc_call(
        body, name=name, grid=(m // tmm, n // tn, nk),
        in_specs=[a_spec, pl.BlockSpec((None, tk, tn), lambda i, j, k: (j, k, 0)) if b.ndim == 3
                  else pl.BlockSpec((tk, tn), lambda i, j, k: (k, j))] + [pl.BlockSpec(memory_space=pl.ANY)] * len(extra),
        out_specs=pl.BlockSpec((tmm, tn), lambda i, j, k: (i, j)),
        out_shape=jax.ShapeDtypeStruct((m, n), BF16),
        scratch_shapes=[pltpu.VMEM((tmm, tn), F32)],
        compiler_params=_seq(3),
    )(a, b, *extra)


def _inproj_fwd(xhat, gam, bet, w_main, w_lr, wgu, b_gate, widths, name):
    t, d = xhat.shape
    tm = _row_tile(t)
    kw = wgu.shape[1]
    offs = [0]
    for w in widths:
        offs.append(offs[-1] + w)

    def body(x_ref, g_ref, b_ref, wm_ref, wl_ref, wgu_ref, bg_ref, *outs):
        piece_refs, (zg_ref, la_ref, hb_ref) = outs[:len(widths)], outs[len(widths):]
        hb = (x_ref[...] * g_ref[...] + b_ref[...]).astype(BF16)
        hb_ref[...] = hb
        for p, ref in enumerate(piece_refs):
            ref[...] = _mm_nt(hb, wm_ref[offs[p]:offs[p + 1], :])
        zg = _mm_nt(hb, wl_ref[...])
        zg_ref[...] = zg
        logit = _mm(zg, wgu_ref[...]) + bg_ref[...]
        la_ref[...] = (jnp.minimum(logit, 0.0) - jnp.log(1.0 + jnp.exp(-jnp.abs(logit)))) * (1.0 / GLA_GATE_TEMP)

    row = lambda i: (i, 0)
    full = lambda a: pl.BlockSpec(a.shape, lambda i: (0,) * a.ndim)
    out_w = list(widths) + [LANE, kw]
    return _tc_call(
        body, name=name, grid=(t // tm,),
        in_specs=[pl.BlockSpec((tm, d), row), full(gam), full(bet), full(w_main), full(w_lr), full(wgu), full(b_gate)],
        out_specs=[pl.BlockSpec((tm, w), row) for w in out_w] + [pl.BlockSpec((tm, d), row)],
        out_shape=[jax.ShapeDtypeStruct((t, w), F32) for w in out_w] + [jax.ShapeDtypeStruct((t, d), BF16)],
        compiler_params=_seq(1),
    )(xhat, gam, bet, w_main, w_lr, wgu, b_gate)


def _inproj_bwd(dh_part, pieces, dzg, w_main, w_lr, name):
    t, d = dh_part.shape
    tm = _row_tile(t)
    widths = [p.shape[1] for p in pieces]
    offs = [0]
    for w in widths:
        offs.append(offs[-1] + w)

    def body(*refs):
        dhp_ref = refs[0]
        p_refs = refs[1:1 + len(widths)]
        dzg_ref, wm_ref, wl_ref, dh_ref, dz_ref = refs[1 + len(widths):]
        acc = dhp_ref[...] + _mm(dzg_ref[...], wl_ref[...])
        for p, ref in enumerate(p_refs):
            v = ref[...]
            dz_ref[:, offs[p]:offs[p + 1]] = v
            acc += _mm(v, wm_ref[offs[p]:offs[p + 1], :])
        dh_ref[...] = acc

    row = lambda i: (i, 0)
    full = lambda a: pl.BlockSpec(a.shape, lambda i: (0,) * a.ndim)
    return _tc_call(
        body, name=name, grid=(t // tm,),
        in_specs=[pl.BlockSpec((tm, d), row)] + [pl.BlockSpec((tm, w), row) for w in widths]
                 + [pl.BlockSpec((tm, LANE), row), full(w_main), full(w_lr)],
        out_specs=[pl.BlockSpec((tm, d), row), pl.BlockSpec((tm, offs[-1]), row)],
        out_shape=[jax.ShapeDtypeStruct((t, d), F32), jax.ShapeDtypeStruct((t, offs[-1]), BF16)],
        compiler_params=_seq(1),
    )(dh_part, *pieces, dzg, w_main, w_lr)


def _pool_cnt(tile, tm, w):
    t = tile * tm + lax.broadcasted_iota(jnp.int32, (tm, 1), 0)
    return jnp.minimum(t + 1, w).astype(F32)


def _pool_fwd(u, wp, scale, name):
    t, pw = u.shape
    tm = _row_tile(t)
    gd = wp.shape[1]

    def body(u_ref, wp_ref, sc_ref, y_ref, p_ref, ext):
        i = pl.program_id(0)

        @pl.when(i == 0)
        def _():
            ext[0:POOL_HALO, :] = jnp.zeros((POOL_HALO, pw), F32)

        ext[POOL_HALO:POOL_HALO + tm, :] = u_ref[...]
        for gi, w in enumerate(POOL_WINDOWS):
            cols = slice(gi * gd, (gi + 1) * gd)
            s = ext[pl.ds(POOL_HALO, tm), cols]
            tot = s
            for back in range(1, w):
                tot = tot + ext[pl.ds(POOL_HALO - back, tm), cols]
            p = (tot / _pool_cnt(i, tm, w) - s).astype(BF16)
            p_ref[:, cols] = p
            y_ref[:, cols] = (jnp.dot(p, wp_ref[gi], preferred_element_type=F32) * sc_ref[:, cols]).astype(BF16)
        ext[0:POOL_HALO, :] = ext[tm:tm + POOL_HALO, :]

    row = lambda i: (i, 0)
    return _tc_call(
        body, name=name, grid=(t // tm,),
        in_specs=[pl.BlockSpec((tm, pw), row), pl.BlockSpec(wp.shape, lambda i: (0, 0, 0)),
                  pl.BlockSpec((1, pw), lambda i: (0, 0))],
        out_specs=[pl.BlockSpec((tm, pw), row), pl.BlockSpec((tm, pw), row)],
        out_shape=[jax.ShapeDtypeStruct((t, pw), BF16), jax.ShapeDtypeStruct((t, pw), BF16)],
        scratch_shapes=[pltpu.VMEM((tm + POOL_HALO, pw), F32)],
        compiler_params=_seq(1),
    )(u, wp, scale)


def _pool_bwd(dy, pb, wp, scale, name):
    t, pw = dy.shape
    tm = _row_tile(t)
    nt = t // tm
    gd = wp.shape[1]

    def body(dy_ref, p_ref, wp_ref, sc_ref, du_ref, dwp_ref, dsc_ref, ext):
        i = pl.program_id(0)
        tile = nt - 1 - i

        @pl.when(i == 0)
        def _():
            ext[tm:tm + POOL_HALO, :] = jnp.zeros((POOL_HALO, pw), F32)
            dwp_ref[...] = jnp.zeros_like(dwp_ref)
            dsc_ref[...] = jnp.zeros_like(dsc_ref)

        dps = []
        for gi, w in enumerate(POOL_WINDOWS):
            cols = slice(gi * gd, (gi + 1) * gd)
            dyv = dy_ref[:, cols]
            p = p_ref[:, cols]
            dpre = (dyv * sc_ref[:, cols]).astype(BF16)
            dsc_ref[:, cols] += jnp.sum(dyv * jnp.dot(p, wp_ref[gi], preferred_element_type=F32), axis=0, keepdims=True)
            dwp_ref[gi] += _mm_tn(p, dpre)
            dp = _mm_nt(dpre, wp_ref[gi])
            dps.append(dp)
            ext[0:tm, cols] = dp / _pool_cnt(tile, tm, w)
        for gi, w in enumerate(POOL_WINDOWS):
            cols = slice(gi * gd, (gi + 1) * gd)
            tot = ext[pl.ds(0, tm), cols]
            for fwd in range(1, w):
                tot = tot + ext[pl.ds(fwd, tm), cols]
            du_ref[:, cols] = (tot - dps[gi]).astype(BF16)
        ext[tm:tm + POOL_HALO, :] = ext[0:POOL_HALO, :]

    row = lambda i: (nt - 1 - i, 0)
    return _tc_call(
        body, name=name, grid=(nt,),
        in_specs=[pl.BlockSpec((tm, pw), row), pl.BlockSpec((tm, pw), row),
                  pl.BlockSpec(wp.shape, lambda i: (0, 0, 0)), pl.BlockSpec((1, pw), lambda i: (0, 0))],
        out_specs=[pl.BlockSpec((tm, pw), row), pl.BlockSpec(wp.shape, lambda i: (0, 0, 0)),
                   pl.BlockSpec((1, pw), lambda i: (0, 0))],
        out_shape=[jax.ShapeDtypeStruct((t, pw), BF16), jax.ShapeDtypeStruct(wp.shape, F32),
                   jax.ShapeDtypeStruct((1, pw), F32)],
        scratch_shapes=[pltpu.VMEM((tm + POOL_HALO, pw), F32)],
        compiler_params=_seq(1),
    )(dy, pb, wp, scale)


def _gla_masks(kw, vw):
    dk, dv = kw // N_HEADS, vw // N_HEADS
    lane_k = lax.broadcasted_iota(jnp.int32, (1, kw), 1)
    lane_v = lax.broadcasted_iota(jnp.int32, (1, vw), 1)
    hk = [((lane_k >= h * dk) & (lane_k < (h + 1) * dk)).astype(F32) for h in range(N_HEADS)]
    hv = [((lane_v >= h * dv) & (lane_v < (h + 1) * dv)).astype(F32) for h in range(N_HEADS)]
    r = lax.broadcasted_iota(jnp.int32, (CHUNK, CHUNK), 0)
    c = lax.broadcasted_iota(jnp.int32, (CHUNK, CHUNK), 1)
    tril = r >= c
    rs = lax.broadcasted_iota(jnp.int32, (N_HEADS * CHUNK, CHUNK), 0) & (CHUNK - 1)
    stril = rs >= lax.broadcasted_iota(jnp.int32, (N_HEADS * CHUNK, CHUNK), 1)
    return hk, hv, tril, stril


def _block_diag(x, hk, dv):
    return jnp.concatenate([x[h * dv:(h + 1) * dv, :] * hk[h] for h in range(N_HEADS)], axis=0)


def _gla_fwd(q, k, v, loga, r, gnorm, name):
    t, kw = q.shape
    vw = v.shape[1]
    dk, dv = kw // N_HEADS, vw // N_HEADS
    tm = _row_tile(t)
    nc = tm // CHUNK
    qscale = dk ** -0.5

    def body(q_ref, k_ref, v_ref, la_ref, r_ref, gn_ref, o_ref, y_ref, sall_ref, st):
        @pl.when(pl.program_id(0) == 0)
        def _():
            st[...] = jnp.zeros_like(st)

        hk, hv, tril, stril = _gla_masks(kw, vw)
        trif = tril.astype(F32)

        def chunk(c, carry):
            rows = pl.ds(pl.multiple_of(c * CHUNK, CHUNK), CHUNK)
            la = la_ref[rows, :]
            b = _mm_f32(trif, la)
            bl = jnp.sum(la, axis=0, keepdims=True)
            qb = q_ref[rows, :] * (qscale * jnp.exp(b))
            kk = k_ref[rows, :]
            kb = kk * jnp.exp(-b)
            kl = kk * jnp.exp(bl - b)
            vv = v_ref[rows, :]
            s_t = st[...]
            compact = s_t[0:dv, :]
            for h in range(1, N_HEADS):
                compact = compact + s_t[h * dv:(h + 1) * dv, :]
            sall_ref[c] = compact
            qx = jnp.concatenate([qb * hk[h] for h in range(N_HEADS)], axis=0)
            a = jnp.where(stril, _mm_nt(qx, kb), 0.0).astype(BF16)
            o_inter = _mm_nt(qb, s_t)
            for h in range(N_HEADS):
                vs = slice(h * dv, (h + 1) * dv)
                o_ref[rows, vs] = o_inter[:, vs] + _mm(a[h * CHUNK:(h + 1) * CHUNK, :], vv[:, vs])
            st[...] = s_t * jnp.exp(bl) + _block_diag(_mm_tn(vv, kl), hk, dv)
            return carry

        lax.fori_loop(0, nc, chunk, 0, unroll=CHUNK_UNROLL)
        for h in range(N_HEADS):
            vs = slice(h * dv, (h + 1) * dv)
            oh = o_ref[:, vs]
            on = oh * lax.rsqrt(jnp.mean(oh * oh, axis=-1, keepdims=True) + RMS_EPS)
            _, sl = _silu_parts(r_ref[:, vs])
            y_ref[:, vs] = (on * gn_ref[:, vs] * sl).astype(BF16)

    row = lambda i: (i, 0)
    return _tc_call(
        body, name=name, grid=(t // tm,),
        in_specs=[pl.BlockSpec((tm, kw), row), pl.BlockSpec((tm, kw), row), pl.BlockSpec((tm, vw), row),
                  pl.BlockSpec((tm, kw), row), pl.BlockSpec((tm, vw), row), pl.BlockSpec((1, vw), lambda i: (0, 0))],
        out_specs=[pl.BlockSpec((tm, vw), row), pl.BlockSpec((tm, vw), row),
                   pl.BlockSpec((nc, dv, kw), lambda i: (i, 0, 0))],
        out_shape=[jax.ShapeDtypeStruct((t, vw), F32), jax.ShapeDtypeStruct((t, vw), BF16),
                   jax.ShapeDtypeStruct((t // CHUNK, dv, kw), F32)],
        scratch_shapes=[pltpu.VMEM((vw, kw), F32)],
        compiler_params=_seq(1),
    )(q, k, v, loga, r, gnorm)


def _gla_bwd(dy, o, r, gnorm, q, k, v, loga, zg, sall, wgu, name):
    t, kw = q.shape
    vw = v.shape[1]
    dk, dv = kw // N_HEADS, vw // N_HEADS
    tm = _row_tile(t)
    nt = t // tm
    nc = tm // CHUNK
    qscale = dk ** -0.5

    def body(dy_ref, o_ref, r_ref, gn_ref, q_ref, k_ref, v_ref, la_ref, zg_ref, sall_ref, wgu_ref,
             dq_ref, dk_ref, dv_ref, dr_ref, dzg_ref, dwgu_ref, dbg_ref, dgn_ref, dst, do_s):
        @pl.when(pl.program_id(0) == 0)
        def _():
            dst[...] = jnp.zeros_like(dst)
            dwgu_ref[...] = jnp.zeros_like(dwgu_ref)
            dbg_ref[...] = jnp.zeros_like(dbg_ref)
            dgn_ref[...] = jnp.zeros_like(dgn_ref)

        for h in range(N_HEADS):
            vs = slice(h * dv, (h + 1) * dv)
            oh = o_ref[:, vs]
            rinv = lax.rsqrt(jnp.mean(oh * oh, axis=-1, keepdims=True) + RMS_EPS)
            on = oh * rinv
            rr = r_ref[:, vs]
            sg, sl = _silu_parts(rr)
            dyv = dy_ref[:, vs]
            gn = gn_ref[:, vs]
            dgn_ref[:, vs] += jnp.sum(dyv * on * sl, axis=0, keepdims=True)
            dr_ref[:, vs] = (dyv * on * gn * (sg * (1.0 + rr * (1.0 - sg)))).astype(BF16)
            don = dyv * gn * sl
            do_s[:, vs] = rinv * (don - on * jnp.mean(don * on, axis=-1, keepdims=True))

        hk, hv, tril, stril = _gla_masks(kw, vw)
        trif = tril.astype(F32)
        triuf = (lax.broadcasted_iota(jnp.int32, (CHUNK, CHUNK), 0)
                 <= lax.broadcasted_iota(jnp.int32, (CHUNK, CHUNK), 1)).astype(F32)
        last_row = lax.broadcasted_iota(jnp.int32, (CHUNK, 1), 0) == CHUNK - 1

        def chunk(idx, carry):
            c = nc - 1 - idx
            rows = pl.ds(pl.multiple_of(c * CHUNK, CHUNK), CHUNK)
            la = la_ref[rows, :]
            b = _mm_f32(trif, la)
            bl = jnp.sum(la, axis=0, keepdims=True)
            eb = jnp.exp(b)
            enb = jnp.exp(-b)
            ebl = jnp.exp(bl - b)
            el = jnp.exp(bl)
            qb = q_ref[rows, :] * (qscale * eb)
            kk = k_ref[rows, :]
            kb = kk * enb
            kl = kk * ebl
            vv = v_ref[rows, :]
            do = do_s[rows, :]
            compact = sall_ref[c]
            s_t = jnp.concatenate([compact * hk[h] for h in range(N_HEADS)], axis=0)
            ds_t = dst[...]
            qx = jnp.concatenate([qb * hk[h] for h in range(N_HEADS)], axis=0)
            dox = jnp.concatenate([do * hv[h] for h in range(N_HEADS)], axis=0)
            a = jnp.where(stril, _mm_nt(qx, kb), 0.0).astype(BF16)
            da = jnp.where(stril, _mm_nt(dox, vv), 0.0).astype(BF16)
            dv_ref[rows, :] = (_mm_tn(a, dox) + _mm_nt(kl, ds_t)).astype(BF16)
            dak = _mm(da, kb)
            dqb = _mm(do, s_t)
            for h in range(N_HEADS):
                dqb = dqb + dak[h * CHUNK:(h + 1) * CHUNK, :] * hk[h]
            dkb = _mm_tn(da, qx)
            dkl = _mm(vv, ds_t)
            dbl = jnp.sum(dkl * kl, axis=0, keepdims=True) + el * jnp.sum(ds_t * s_t, axis=0, keepdims=True)
            dst[...] = ds_t * el + _block_diag(_mm_tn(do, qb), hk, dv)
            dq_ref[rows, :] = (dqb * (qscale * eb)).astype(BF16)
            dk_ref[rows, :] = (dkb * enb + dkl * ebl).astype(BF16)
            db = dqb * qb - dkb * kb - dkl * kl + jnp.where(last_row, dbl, 0.0)
            dla = _mm_f32(triuf, db)
            dlogit = dla * (1.0 / GLA_GATE_TEMP) * (1.0 - jnp.exp(GLA_GATE_TEMP * la))
            dzg_ref[rows, :] = _mm_nt(dlogit, wgu_ref[...]).astype(BF16)
            dwgu_ref[...] += _mm_tn(zg_ref[rows, :], dlogit)
            dbg_ref[...] += jnp.sum(dlogit, axis=0, keepdims=True)
            return carry

        lax.fori_loop(0, nc, chunk, 0, unroll=CHUNK_UNROLL)

    row = lambda i: (nt - 1 - i, 0)
    const = lambda i: (0, 0)
    return _tc_call(
        body, name=name, grid=(nt,),
        in_specs=[pl.BlockSpec((tm, vw), row), pl.BlockSpec((tm, vw), row), pl.BlockSpec((tm, vw), row),
                  pl.BlockSpec((1, vw), const), pl.BlockSpec((tm, kw), row), pl.BlockSpec((tm, kw), row),
                  pl.BlockSpec((tm, vw), row), pl.BlockSpec((tm, kw), row), pl.BlockSpec((tm, LANE), row),
                  pl.BlockSpec((nc, dv, kw), lambda i: (nt - 1 - i, 0, 0)), pl.BlockSpec((LANE, kw), const)],
        out_specs=[pl.BlockSpec((tm, kw), row), pl.BlockSpec((tm, kw), row), pl.BlockSpec((tm, vw), row),
                   pl.BlockSpec((tm, vw), row), pl.BlockSpec((tm, LANE), row), pl.BlockSpec((LANE, kw), const),
                   pl.BlockSpec((1, kw), const), pl.BlockSpec((1, vw), const)],
        out_shape=[jax.ShapeDtypeStruct((t, kw), BF16), jax.ShapeDtypeStruct((t, kw), BF16),
                   jax.ShapeDtypeStruct((t, vw), BF16), jax.ShapeDtypeStruct((t, vw), BF16),
                   jax.ShapeDtypeStruct((t, LANE), BF16), jax.ShapeDtypeStruct((LANE, kw), F32),
                   jax.ShapeDtypeStruct((1, kw), F32), jax.ShapeDtypeStruct((1, vw), F32)],
        scratch_shapes=[pltpu.VMEM((vw, kw), F32), pltpu.VMEM((tm, vw), F32)],
        compiler_params=_seq(1),
    )(dy, o, r, gnorm, q, k, v, loga, zg, sall, wgu)


def _outproj_fwd(yp, yg, w_out, xhat, gam, bet, alpha, name):
    t, d = xhat.shape
    pw = yp.shape[1]
    tm = _row_tile(t)

    def body(yp_ref, yg_ref, w_ref, x_ref, g_ref, b_ref, xhat_ref, rstd_ref):
        for part in range(ROW_GROUPS):
            rows = pl.ds(part * (tm // ROW_GROUPS), tm // ROW_GROUPS)
            h = x_ref[rows, :] * g_ref[...] + b_ref[...]
            y = (jnp.dot(yp_ref[rows, :], w_ref[0:pw, :], preferred_element_type=F32)
                 + jnp.dot(yg_ref[rows, :], w_ref[pw:, :], preferred_element_type=F32))
            xh, rs = _ln_stats(alpha * h + y)
            xhat_ref[rows, :] = xh
            rstd_ref[rows, :] = rs

    row = lambda i: (i, 0)
    vec = pl.BlockSpec((1, d), lambda i: (0, 0))
    return _tc_call(
        body, name=name, grid=(t // tm,),
        in_specs=[pl.BlockSpec((tm, pw), row), pl.BlockSpec((tm, yg.shape[1]), row),
                  pl.BlockSpec(w_out.shape, lambda i: (0, 0)), pl.BlockSpec((tm, d), row), vec, vec],
        out_specs=[pl.BlockSpec((tm, d), row), pl.BlockSpec((tm, 1), row)],
        out_shape=[jax.ShapeDtypeStruct((t, d), F32), jax.ShapeDtypeStruct((t, 1), F32)],
        compiler_params=_seq(1),
    )(yp, yg, w_out, xhat, gam, bet)


def _outproj_bwd(dh, xhat, rstd, ln_g, w_out, pw, alpha, name):
    t, d = dh.shape
    tm = _row_tile(t)
    gw = w_out.shape[0] - pw

    def body(dh_ref, xh_ref, rs_ref, g_ref, w_ref, dyb_ref, dyp_ref, dyg_ref, dres_ref, dgam_ref, dbet_ref):
        @pl.when(pl.program_id(0) == 0)
        def _():
            dgam_ref[...] = jnp.zeros_like(dgam_ref)
            dbet_ref[...] = jnp.zeros_like(dbet_ref)

        for part in range(ROW_GROUPS):
            rows = pl.ds(part * (tm // ROW_GROUPS), tm // ROW_GROUPS)
            dy = dh_ref[rows, :]
            xh = xh_ref[rows, :]
            dr = _ln_bwd(dy, xh, rs_ref[rows, :], g_ref[...])
            dgam_ref[...] += jnp.sum(dy * xh, axis=0, keepdims=True)
            dbet_ref[...] += jnp.sum(dy, axis=0, keepdims=True)
            drb = dr.astype(BF16)
            dyb_ref[rows, :] = drb
            dres_ref[rows, :] = alpha * dr
            dyp_ref[rows, :] = _mm_nt(drb, w_ref[0:pw, :])
            dyg_ref[rows, :] = _mm_nt(drb, w_ref[pw:, :])

    row = lambda i: (i, 0)
    vec = pl.BlockSpec((1, d), lambda i: (0, 0))
    return _tc_call(
        body, name=name, grid=(t // tm,),
        in_specs=[pl.BlockSpec((tm, d), row), pl.BlockSpec((tm, d), row), pl.BlockSpec((tm, 1), row), vec,
                  pl.BlockSpec(w_out.shape, lambda i: (0, 0))],
        out_specs=[pl.BlockSpec((tm, d), row), pl.BlockSpec((tm, pw), row), pl.BlockSpec((tm, gw), row),
                   pl.BlockSpec((tm, d), row), vec, vec],
        out_shape=[jax.ShapeDtypeStruct((t, d), BF16), jax.ShapeDtypeStruct((t, pw), F32),
                   jax.ShapeDtypeStruct((t, gw), F32), jax.ShapeDtypeStruct((t, d), F32),
                   jax.ShapeDtypeStruct((1, d), F32), jax.ShapeDtypeStruct((1, d), F32)],
        compiler_params=_seq(1),
    )(dh, xhat, rstd, ln_g, w_out)


def _loss_head(xhat, gam, bet, target, n_rows, name):
    t, d = xhat.shape
    tm = _row_tile(t)

    def body(x_ref, g_ref, b_ref, t_ref, dy_ref, loss_ref):
        i = pl.program_id(0)

        @pl.when(i == 0)
        def _():
            loss_ref[...] = jnp.zeros_like(loss_ref)

        rowi = i * tm + lax.broadcasted_iota(jnp.int32, (tm, 1), 0)
        live = (rowi >= N_META) & (rowi < N_META + n_rows)
        diff = jnp.where(live, x_ref[...] * g_ref[...] + b_ref[...] - t_ref[...], 0.0)
        dy_ref[...] = diff * (1.0 / d)
        loss_ref[...] += jnp.sum(diff * diff) * (0.5 / d)

    row = lambda i: (i, 0)
    vec = pl.BlockSpec((1, d), lambda i: (0, 0))
    return _tc_call(
        body, name=name, grid=(t // tm,),
        in_specs=[pl.BlockSpec((tm, d), row), vec, vec, pl.BlockSpec((tm, d), row)],
        out_specs=[pl.BlockSpec((tm, d), row), pl.BlockSpec((8, LANE), lambda i: (0, 0))],
        out_shape=[jax.ShapeDtypeStruct((t, d), F32), jax.ShapeDtypeStruct((8, LANE), F32)],
        compiler_params=_seq(1),
    )(xhat, gam, bet, target)


def _rows_block(r, c):
    best = r
    for cand in range(BF16_ROWS, r, BF16_ROWS):
        if r % cand == 0 and cand * c * 4 <= (1 << 20):
            best = cand
    return best if best * c * 4 <= (4 << 20) else r


def _sum_slots(me, mine, recvs, name, layers_side_by_side=False):
    nl = len(recvs)
    ns, r, c = recvs[0].shape
    tr = _rows_block(r, c)

    def body(me_ref, *refs):
        o_ref = refs[nl * ns]
        for l in range(nl):
            acc = refs[l * ns][...].astype(F32)
            for s in range(1, ns):
                acc = acc + refs[l * ns + s][...].astype(F32)
            if layers_side_by_side:
                o_ref[0, :, l * c:(l + 1) * c] = acc.astype(o_ref.dtype)
            else:
                o_ref[l] = acc.astype(o_ref.dtype)

    def slot(s):
        return pl.BlockSpec((None, tr, c), lambda i, me_ref: ((me_ref[0] + s) % ns, i, 0))

    out = (1, r, nl * c) if layers_side_by_side else (nl, r, c)
    operands = []
    for l in range(nl):
        operands += [mine[l]] + [recvs[l]] * (ns - 1)
    return _tc_call(
        body, name=name,
        grid_spec=pltpu.PrefetchScalarGridSpec(
            num_scalar_prefetch=1, grid=(r // tr,),
            in_specs=[slot(s) for s in range(ns)] * nl,
            out_specs=pl.BlockSpec((out[0], tr, out[2]), lambda i, me_ref: (0, i, 0))),
        out_shape=jax.ShapeDtypeStruct(out, recvs[0].dtype),
        compiler_params=_seq(1),
    )(me, *operands)


def _adamw(w, terms, m, v, name):
    nl, r, c = w.shape
    tc = c
    while tc % (2 * LANE) == 0 and tc > 4 * LANE:
        tc //= 2
    tr = _rows_block(r, tc)
    nterm = len(terms)

    def body(*refs):
        w_ref = refs[0]
        t_refs = refs[1:1 + nterm]
        m_ref, v_ref, g_ref, d_ref, nm_ref, nv_ref = refs[1 + nterm:]
        g = t_refs[0][...].astype(F32)
        for tr_ in t_refs[1:]:
            g = g + tr_[...].astype(F32)
        nm = ADAM_B1 * m_ref[...] + (1.0 - ADAM_B1) * g
        nv = ADAM_B2 * v_ref[...] + (1.0 - ADAM_B2) * jnp.square(g)
        m_hat = nm / (1.0 - ADAM_B1 ** ADAM_STEP)
        v_hat = nv / (1.0 - ADAM_B2 ** ADAM_STEP)
        g_ref[...] = g
        d_ref[...] = -ADAM_LR * (m_hat / (jnp.sqrt(v_hat) + ADAM_EPS) + ADAM_WD * w_ref[...])
        nm_ref[...] = nm
        nv_ref[...] = nv

    spec = pl.BlockSpec((None, tr, tc), lambda l, i, j: (l, i, j))
    shp = jax.ShapeDtypeStruct((nl, r, c), F32)
    return _tc_call(
        body, name=name, grid=(nl, r // tr, c // tc),
        in_specs=[spec] * (3 + nterm), out_specs=[spec] * 4, out_shape=[shp] * 4,
        compiler_params=_seq(3),
    )(w, *terms, m, v)


XY_RELATIONS = ((1, 0, 0), (0, 1, 0), (1, 1, 0))
ALL_RELATIONS = tuple((fx, fy, fc) for fx in (0, 1) for fy in (0, 1) for fc in (0, 1) if fx or fy or fc)
HBM_SPEC = pl.BlockSpec(memory_space=pltpu.HBM)
SEM_SPEC = pl.BlockSpec(memory_space=pltpu.SEMAPHORE)
DATAFLOW = pltpu.SideEffectType.DATAFLOW_SIDE_EFFECTING


def _split_call(body, **kw):
    return pl.pallas_call(body, **kw)


def _flip(v, f):
    return 1 - v if f else v


def _any_spec(n):
    return [pl.BlockSpec(memory_space=pl.ANY)] * n


def _relations(kind):
    return ALL_RELATIONS if kind == "bcast" else XY_RELATIONS


def _copies(kind, arr, land, sems):
    x, y, c = lax.axis_index("x"), lax.axis_index("y"), lax.axis_index("c")
    out = []
    for (fx, fy, fc), (send_sem, recv_sem) in zip(_relations(kind), sems):
        px, py, pc = _flip(x, fx), _flip(y, fy), _flip(c, fc)
        if kind == "bcast":
            mine, theirs = 4 * x + 2 * y + c, 4 * px + 2 * py + pc
        else:
            mine, theirs = 2 * x + y, 2 * px + py
        src, to_mine, to_theirs = arr, land.at[mine], land.at[theirs]
        if kind == "scatter":
            src = arr.at[theirs]
        if kind == "gather_half":
            rows = _my_half(arr.shape[0], c)
            src, to_mine, to_theirs = arr.at[rows], land.at[mine, rows], land.at[theirs, rows]
        both = dict(src_ref=src, send_sem=send_sem, recv_sem=recv_sem, device_id=(px, py, pc), device_id_type=MESH)
        out.append((pltpu.make_async_remote_copy(dst_ref=to_mine, **both),
                    pltpu.make_async_remote_copy(dst_ref=to_theirs, **both)))
    return out


def _my_half(nrows, c):
    return pl.ds(c * (nrows // 2), nrows // 2)


def _share_halves(name, kinds, lands):
    ks = [k for k, kd in enumerate(kinds) if kd == "gather_half"]
    n = len(ks)

    def body(*refs):
        l_refs = refs[n:2 * n]
        send_sems, recv_sems = refs[2 * n:]
        x, y, c = lax.axis_index("x"), lax.axis_index("y"), lax.axis_index("c")
        copies = []
        for i in range(n):
            nrows = l_refs[i].shape[1]
            for r, (fx, fy, _) in enumerate(XY_RELATIONS):
                slot = 2 * _flip(x, fx) + _flip(y, fy)
                both = dict(src_ref=l_refs[i].at[slot, _my_half(nrows, c)], send_sem=send_sems.at[i, r],
                            recv_sem=recv_sems.at[i, r], device_id=(x, y, 1 - c), device_id_type=MESH)
                copies.append((pltpu.make_async_remote_copy(dst_ref=l_refs[i].at[slot, _my_half(nrows, c)], **both),
                               pltpu.make_async_remote_copy(dst_ref=l_refs[i].at[slot, _my_half(nrows, 1 - c)], **both)))
        for send, _ in copies:
            send.start()
        for _, arrival in copies:
            arrival.wait_recv()
        for send, _ in copies:
            send.wait_send()

    outs = _comm_call(
        body, name=name,
        in_specs=_any_spec(n), out_specs=_any_spec(n),
        out_shape=[jax.ShapeDtypeStruct(lands[k].shape, lands[k].dtype) for k in ks],
        input_output_aliases={i: i for i in range(n)},
        scratch_shapes=[pltpu.SemaphoreType.DMA((n, 3)), pltpu.SemaphoreType.DMA((n, 3))],
    )(*[lands[k] for k in ks])
    lands = list(lands)
    for k, o in zip(ks, outs):
        lands[k] = o
    return lands


def _sem_pairs(kinds, sems):
    out, at = [], 0
    for kind in kinds:
        nrel = len(_relations(kind))
        out.append([(sems[at + 2 * r], sems[at + 2 * r + 1]) for r in range(nrel)])
        at += 2 * nrel
    return out


def _exchange_start(name, kinds, arrs, lands):
    n = len(arrs)
    nsem = sum(2 * len(_relations(kd)) for kd in kinds)

    def body(*refs):
        a_refs, l_refs = refs[:n], refs[n:2 * n]
        pairs = _sem_pairs(kinds, refs[2 * n:2 * n + nsem])
        token = refs[-1]
        for k in range(n):
            for send, _ in _copies(kinds[k], a_refs[k], l_refs[k], pairs[k]):
                send.start()
        token[...] = jnp.zeros_like(token)

    thru = [pltpu.HBM(a.shape, a.dtype) for a in list(arrs) + list(lands)]
    outs = _split_call(
        body, name=name,
        out_shape=(*[pltpu.SemaphoreType.DMA(())] * nsem, *thru, jax.ShapeDtypeStruct((8, LANE), F32)),
        in_specs=[HBM_SPEC] * (2 * n),
        out_specs=(*[SEM_SPEC] * nsem, *[HBM_SPEC] * (2 * n), pl.BlockSpec(memory_space=pltpu.VMEM)),
        input_output_aliases={i: nsem + i for i in range(2 * n)},
        compiler_params=pltpu.CompilerParams(has_side_effects=DATAFLOW),
    )(*[pltpu.with_memory_space_constraint(a, pltpu.HBM) for a in list(arrs) + list(lands)])
    return dict(kinds=kinds, sems=outs[:nsem], arrs=outs[nsem:nsem + n], lands=outs[nsem + n:nsem + 2 * n],
                token=outs[-1])


def _exchange_wait(name, st, after):
    kinds = st["kinds"]
    n = len(kinds)
    nsem = len(st["sems"])

    def body(*refs):
        a_refs, l_refs = refs[:n], refs[n:2 * n]
        pairs = _sem_pairs(kinds, refs[2 * n:2 * n + nsem])
        for k in range(n):
            for _, arrival in _copies(kinds[k], a_refs[k], l_refs[k], pairs[k]):
                arrival.wait_send()
                arrival.wait_recv()
        refs[-1][...] = jnp.zeros_like(refs[-1])

    ins = list(st["arrs"]) + list(st["lands"])
    outs = _split_call(
        body, name=name,
        out_shape=[pltpu.HBM(a.shape, a.dtype) for a in ins] + [jax.ShapeDtypeStruct((8, LANE), F32)],
        in_specs=[HBM_SPEC] * (2 * n) + [SEM_SPEC] * nsem + [pl.BlockSpec(memory_space=pl.ANY)],
        out_specs=[HBM_SPEC] * (2 * n) + [pl.BlockSpec(memory_space=pltpu.VMEM)],
        input_output_aliases={i: i for i in range(2 * n)},
        compiler_params=pltpu.CompilerParams(has_side_effects=DATAFLOW),
    )(*ins, *st["sems"], after)
    return outs[:n], outs[n:2 * n], outs[-1]


def _landing(own, slot, nslot):
    return lax.dynamic_update_slice(lax.empty((nslot,) + own.shape, own.dtype), own[None], (slot,) + (0,) * own.ndim)


def _swap_sibling(parts, name):
    n = len(parts)

    def body(*refs):
        ins, outs = refs[:n], refs[n:2 * n]
        send_sems, recv_sems = refs[2 * n:]
        sib = (lax.axis_index("x"), lax.axis_index("y"), 1 - lax.axis_index("c"))
        cps = [pltpu.make_async_remote_copy(src_ref=ins[k], dst_ref=outs[k], send_sem=send_sems.at[k],
                                            recv_sem=recv_sems.at[k], device_id=sib, device_id_type=MESH)
               for k in range(n)]
        for cp in cps:
            cp.start()
        for cp in cps:
            cp.wait_recv()
        for cp in cps:
            cp.wait_send()

    return _comm_call(
        body, name=name,
        in_specs=_any_spec(n), out_specs=_any_spec(n),
        out_shape=[jax.ShapeDtypeStruct(p.shape, p.dtype) for p in parts],
        scratch_shapes=[pltpu.SemaphoreType.DMA((n,)), pltpu.SemaphoreType.DMA((n,))],
    )(*parts)


def _col_shards(a, n=N_SHARD):
    r, c = a.shape
    return a.reshape(r, n, c // n).transpose(1, 0, 2)


def _from_col_shards(a):
    n, r, cs = a.shape
    return a.transpose(1, 0, 2).reshape(r, n * cs)


def kernel(x, meta_tokens, ffn1_w_gate, ffn1_w_up, ffn1_w_down, ln1_g, ln1_b, w_in, w_gate_up, b_gate, w_pool, pool_scale, gla_norm_g, w_out, ln2_g, ln2_b, ffn2_w_gate, ffn2_w_up, ffn2_w_down, ln3_g, ln3_b, loss_target, m_meta_tokens, m_ffn1_w_gate, m_ffn1_w_up, m_ffn1_w_down, m_ln1_g, m_ln1_b, m_w_in, m_w_gate_up, m_b_gate, m_w_pool, m_pool_scale, m_gla_norm_g, m_w_out, m_ln2_g, m_ln2_b, m_ffn2_w_gate, m_ffn2_w_up, m_ffn2_w_down, m_ln3_g, m_ln3_b, v_meta_tokens, v_ffn1_w_gate, v_ffn1_w_up, v_ffn1_w_down, v_ln1_g, v_ln1_b, v_w_in, v_w_gate_up, v_b_gate, v_w_pool, v_pool_scale, v_gla_norm_g, v_w_out, v_ln2_g, v_ln2_b, v_ffn2_w_gate, v_ffn2_w_up, v_ffn2_w_down, v_ln3_g, v_ln3_b):
    w = dict(meta_tokens=meta_tokens, ffn1_w_gate=ffn1_w_gate, ffn1_w_up=ffn1_w_up, ffn1_w_down=ffn1_w_down,
             ln1_g=ln1_g, ln1_b=ln1_b, w_in=w_in, w_gate_up=w_gate_up, b_gate=b_gate, w_pool=w_pool,
             pool_scale=pool_scale, gla_norm_g=gla_norm_g, w_out=w_out, ln2_g=ln2_g, ln2_b=ln2_b,
             ffn2_w_gate=ffn2_w_gate, ffn2_w_up=ffn2_w_up, ffn2_w_down=ffn2_w_down, ln3_g=ln3_g, ln3_b=ln3_b)
    mom1 = dict(meta_tokens=m_meta_tokens, ffn1_w_gate=m_ffn1_w_gate, ffn1_w_up=m_ffn1_w_up,
                ffn1_w_down=m_ffn1_w_down, ln1_g=m_ln1_g, ln1_b=m_ln1_b, w_in=m_w_in, w_gate_up=m_w_gate_up,
                b_gate=m_b_gate, w_pool=m_w_pool, pool_scale=m_pool_scale, gla_norm_g=m_gla_norm_g, w_out=m_w_out,
                ln2_g=m_ln2_g, ln2_b=m_ln2_b, ffn2_w_gate=m_ffn2_w_gate, ffn2_w_up=m_ffn2_w_up,
                ffn2_w_down=m_ffn2_w_down, ln3_g=m_ln3_g, ln3_b=m_ln3_b)
    mom2 = dict(meta_tokens=v_meta_tokens, ffn1_w_gate=v_ffn1_w_gate, ffn1_w_up=v_ffn1_w_up,
                ffn1_w_down=v_ffn1_w_down, ln1_g=v_ln1_g, ln1_b=v_ln1_b, w_in=v_w_in, w_gate_up=v_w_gate_up,
                b_gate=v_b_gate, w_pool=v_w_pool, pool_scale=v_pool_scale, gla_norm_g=v_gla_norm_g, w_out=v_w_out,
                ln2_g=v_ln2_g, ln2_b=v_ln2_b, ffn2_w_gate=v_ffn2_w_gate, ffn2_w_up=v_ffn2_w_up,
                ffn2_w_down=v_ffn2_w_down, ln3_g=v_ln3_g, ln3_b=v_ln3_b)

    xs = x[0]
    s_len, d = xs.shape
    nl = ln1_g.shape[0]
    alpha = (2.0 * nl) ** 0.25
    t_real = N_META + s_len
    t_pad = -(-t_real // LANE) * LANE
    pw = pool_scale.shape[1]
    kw = b_gate.shape[1]
    vw = gla_norm_g.shape[1]
    rank = w_gate_up.shape[1]
    widths = (pw, kw, kw, vw, vw)
    n_main = sum(widths)
    dff_s = ffn1_w_gate.shape[2]
    dff_c = N_SHARD * dff_s // FFN_CHUNKS

    me_xy = 2 * lax.axis_index("x") + lax.axis_index("y")
    me_all = 2 * me_xy + lax.axis_index("c")
    ffn1_names = ("ffn1_w_gate", "ffn1_w_up", "ffn1_w_down")
    mix_names = ("w_out", "w_gate_up", "w_in")
    ffn2_names = ("ffn2_w_gate", "ffn2_w_up", "ffn2_w_down")

    gate_up = ("ffn1_w_gate", "ffn1_w_up", "ffn2_w_gate", "ffn2_w_up")

    def stored(n, a):
        if n in gate_up:
            return jnp.swapaxes(a, 1, 2)
        return jnp.transpose(a, (2, 0, 1)) if n == "w_in" else a

    def as_given(n, a):
        if n in gate_up:
            return jnp.swapaxes(a, 1, 2)
        if n == "w_in":
            return jnp.transpose(a.reshape(-1, nl, d), (1, 2, 0))
        return a.reshape(w[n].shape)

    stages = [[("meta_tokens", None)], [(n, 0) for n in ffn1_names], [(n, 0) for n in mix_names + ffn2_names]]
    stages += [[(n, l) for n in BIG] for l in range(1, nl)]
    gathers, wa = {}, {}

    halved = ffn1_names + ffn2_names + ("w_out",)

    def start_gather(si, dep=None):
        own = []
        for n, l in stages[si]:
            a = meta_tokens if l is None else (stored(n, w[n])[:, l] if n == "w_in" else stored(n, w[n])[l])
            a = a if dep is None else a + dep
            own.append(a if l is None else a.astype(BF16))
        gathers[si] = _exchange_start(f"gather_start_{si}", ["gather_half" if n in halved else "gather"
                                                             for n, _ in stages[si]], own,
                                      [_landing(a, me_xy, N_SHARD) for a in own])
        return gathers[si]["token"]

    def arrive(si, after):
        _, lands, token = _exchange_wait(f"gather_wait_{si}", gathers[si], after)
        if any(kd == "gather_half" for kd in gathers[si]["kinds"]):
            lands = _share_halves(f"gather_share_{si}", gathers[si]["kinds"], lands)
        for item, a in zip(stages[si], lands):
            wa[item] = a.reshape(FFN_CHUNKS, -1, d) if item[0] in ffn1_names + ffn2_names else a
        return token

    def mixer_weights(l):
        wi = wa["w_in", l].reshape(-1, d)
        return dict(w_main=wi[:n_main], w_lr=jnp.pad(wi[n_main:], ((0, LANE - rank), (0, 0))),
                    wgu=jnp.pad(_from_col_shards(wa["w_gate_up", l]), ((0, LANE - rank), (0, 0))),
                    wout=wa["w_out", l].reshape(-1, d))

    wp16 = w_pool.astype(BF16)
    ones = jnp.ones((1, d), F32)
    zeros = jnp.zeros((1, d), F32)
    target = jnp.concatenate([jnp.zeros((N_META, d), F32), loss_target[0], jnp.zeros((t_pad - t_real, d), F32)], axis=0)

    started = start_gather(0)
    for si in range(1, len(stages)):
        started = start_gather(si, started[0:1, 0:1])
    arrive(0, started)
    meta_full = _from_col_shards(wa["meta_tokens", None])
    h0 = jnp.concatenate([meta_full, xs, jnp.zeros((t_pad - t_real, d), F32)], axis=0)
    arrive(1, h0[:8, :LANE] + target[:8, :LANE])

    saved, mw = [], []
    cur, cur_g, cur_b = h0, ones, zeros
    for l in range(nl):
        s = {}
        xh1, rs1, hb0, g1, u1 = _ffn_fwd(cur, cur_g, cur_b, wa["ffn1_w_gate", l], wa["ffn1_w_up", l],
                                         wa["ffn1_w_down", l], alpha, f"ffn1_fwd_{l}")
        if l == 0:
            arrive(2, xh1)
        mw.append(mixer_weights(l))
        up, q, k, v, r, zg, la, hb1 = _inproj_fwd(xh1, ln1_g[l:l + 1], ln1_b[l:l + 1], mw[l]["w_main"], mw[l]["w_lr"],
                                                  mw[l]["wgu"], b_gate[l:l + 1], widths, f"inproj_fwd_{l}")
        yp, pb = _pool_fwd(up, wp16[l], pool_scale[l:l + 1], f"pool_fwd_{l}")
        o, yg, sall = _gla_fwd(q, k, v, la, r, gla_norm_g[l:l + 1], f"gla_fwd_{l}")
        xh2, rs2 = _outproj_fwd(yp, yg, mw[l]["wout"], xh1, ln1_g[l:l + 1], ln1_b[l:l + 1], alpha, f"outproj_fwd_{l}")
        if l + 1 < nl:
            arrive(l + 3, xh2)
        xh3, rs3, hb2, g2, u2 = _ffn_fwd(xh2, ln2_g[l:l + 1], ln2_b[l:l + 1], wa["ffn2_w_gate", l], wa["ffn2_w_up", l],
                                         wa["ffn2_w_down", l], alpha, f"ffn2_fwd_{l}")
        s.update(xh1=xh1, rs1=rs1, hb0=hb0, g1=g1, u1=u1, q=q, k=k, v=v, r=r, zg=zg, la=la, hb1=hb1, yp=yp, pb=pb,
                 o=o, yg=yg, sall=sall, xh2=xh2, rs2=rs2, xh3=xh3, rs3=rs3, hb2=hb2, g2=g2, u2=u2)
        saved.append(s)
        cur, cur_g, cur_b = xh3, ln3_g[l:l + 1], ln3_b[l:l + 1]

    dh, loss_acc = _loss_head(cur, cur_g, cur_b, target, s_len, "loss_head")
    loss = lax.psum(loss_acc[0, 0], ("x", "y", "c"))

    small_grads = {n: [None] * nl for n in SMALL}
    scatters = []

    def depart(name, items, grads, kinds=None):
        lands = [_landing(g, me_all, N_DEV) if kd == "bcast" else lax.empty(g.shape, g.dtype)
                 for g, kd in zip(grads, kinds or ["scatter"] * len(grads))]
        st = _exchange_start(name, kinds or ["scatter"] * len(grads), grads, lands)
        scatters.append((name, items, st))
        return st["token"]

    def pack(parts):
        flat = jnp.concatenate([parts[n].reshape(-1) for n in SMALL])
        return flat.reshape(-1, LANE)

    def ffn_wgrad(n, l, hb, dgb, dub, act, dfb, after=None):
        if n.endswith("down"):
            dw = _wgrad(act, dfb, dff_c, d, f"{n}_grad_{l}", after)
        else:
            dw = _wgrad(dgb if n.endswith("gate") else dub, hb, dff_c, d, f"{n}_grad_{l}", after)
        return dw.reshape(N_SHARD, dff_s, d)

    late = []
    for l in reversed(range(nl)):
        s = saved[l]
        dh, dfb, dgb, dub, act, dgam, dbet = _ffn_bwd(dh, s["xh3"], s["rs3"], ln3_g[l:l + 1], s["g2"], s["u2"],
                                                      wa["ffn2_w_gate", l], wa["ffn2_w_up", l], wa["ffn2_w_down", l],
                                                      alpha, f"ffn2_bwd_{l}")
        small_grads["ln3_g"][l], small_grads["ln3_b"][l] = dgam, dbet
        gone = depart(f"scatter_start_ffn2_{l}", [(n, l) for n in ffn2_names],
                      [ffn_wgrad(n, l, s["hb2"], dgb, dub, act, dfb) for n in ffn2_names])

        dyb, dyp, dyg, dres, dgam, dbet = _outproj_bwd(dh, s["xh2"], s["rs2"], ln2_g[l:l + 1] + gone[0:1, 0:1],
                                                       mw[l]["wout"], pw, alpha, f"outproj_bwd_{l}")
        small_grads["ln2_g"][l], small_grads["ln2_b"][l] = dgam, dbet
        dwo = jnp.concatenate([_wgrad(s["yp"], dyb, pw, d, f"dwout_pool_{l}"),
                               _wgrad(s["yg"], dyb, vw, d, f"dwout_gla_{l}")], axis=0)
        dq, dk, dv, dr, dzg, dwgu, dbg, dgn = _gla_bwd(dyg, s["o"], s["r"], gla_norm_g[l:l + 1], s["q"], s["k"],
                                                       s["v"], s["la"], s["zg"], s["sall"], mw[l]["wgu"],
                                                       f"gla_bwd_{l}")
        dup, dwp, dsc = _pool_bwd(dyp, s["pb"], wp16[l], pool_scale[l:l + 1], f"pool_bwd_{l}")
        small_grads["b_gate"][l], small_grads["gla_norm_g"][l] = dbg, dgn
        small_grads["w_pool"][l], small_grads["pool_scale"][l] = dwp, dsc
        dh, dz = _inproj_bwd(dres, [dup, dq, dk, dv, dr], dzg, mw[l]["w_main"], mw[l]["w_lr"], f"inproj_bwd_{l}")
        dwi = jnp.concatenate([_wgrad(dz, s["hb1"], 4 * LANE, d, f"dwin_main_{l}"),
                               _wgrad(dzg, s["hb1"], LANE, d, f"dwin_lr_{l}")[:rank]], axis=0)
        gone = depart(f"scatter_start_mix_{l}", [(n, l) for n in mix_names],
                      [dwo.reshape(N_SHARD, -1, d), _col_shards(dwgu[:rank]), dwi.reshape(N_SHARD, -1, d)])

        dh, dfb, dgb, dub, act, dgam, dbet = _ffn_bwd(dh, s["xh1"], s["rs1"], ln1_g[l:l + 1] + gone[0:1, 0:1],
                                                      s["g1"], s["u1"], wa["ffn1_w_gate", l], wa["ffn1_w_up", l],
                                                      wa["ffn1_w_down", l], alpha, f"ffn1_bwd_{l}")
        small_grads["ln1_g"][l], small_grads["ln1_b"][l] = dgam, dbet
        if l:
            gone = depart(f"scatter_start_ffn1_{l}", [(n, l) for n in ffn1_names],
                          [ffn_wgrad(n, l, s["hb0"], dgb, dub, act, dfb) for n in ffn1_names])
            ln3_g = ln3_g.at[l - 1:l].add(gone[0:1, 0:1])
            continue
        grad_x = dh[N_META:t_real][None]
        small_vec = pack({n: jnp.stack(small_grads[n]) for n in SMALL})
        gone = depart("scatter_start_rest", [("meta_tokens", 0), ("small", 0)],
                      [_col_shards(dh[:N_META]), small_vec], ["scatter", "bcast"])
        for n in ffn1_names:
            g = ffn_wgrad(n, l, s["hb0"], dgb, dub, act, dfb, after=gone)
            gone = depart(f"scatter_start_{n}", [(n, l)], [g])
            late.append(scatters.pop())

    sent, recv, results, firsts = {}, {}, {}, []
    my_slot = me_xy.reshape(1).astype(jnp.int32)

    def collect(group, after):
        for name, items, st in group:
            arrs, lands, _ = _exchange_wait(name.replace("start", "wait"), st, after)
            for item, a, b in zip(items, arrs, lands):
                sent[item], recv[item] = a, b

    def reduce_and_update(names, tag):
        partial = []
        for n in names:
            layers = [(n, l) for l in range(1 if n == "meta_tokens" else nl)]
            partial.append(_sum_slots(my_slot, [sent[it] for it in layers], [recv[it] for it in layers],
                                      f"sum_{n}", n == "w_in"))
        for n, mine, theirs in zip(names, partial, _swap_sibling(partial, f"swap_sibling_{tag}")):
            fit = lambda a: stored(n, a).reshape(mine.shape)
            outs = _adamw(fit(w[n]), [mine, theirs], fit(mom1[n]), fit(mom2[n]), f"adamw_{n}")
            results[n] = [as_given(n, o) for o in outs]
            firsts.append(outs[1][0, 0, 0])

    collect(scatters, gone)
    early = [n for n in ("meta_tokens",) + BIG if n not in ffn1_names]
    reduce_and_update(early, "early")
    small_terms = [recv["small", 0][i][None] for i in range(N_DEV)]
    souts = _adamw(pack(w)[None], small_terms, pack(mom1)[None], pack(mom2)[None], "adamw_small")
    off = 0
    for n in SMALL:
        size = w[n].size
        results[n] = [o.reshape(-1)[off:off + size].reshape(w[n].shape) for o in souts]
        off += size
    collect(late, souts[0][0, :8] + functools.reduce(jnp.add, firsts))
    reduce_and_update(ffn1_names, "late")

    out = [loss, grad_x]
    for part in range(4):
        out += [results[n][part] for n in WEIGHTS]
    return tuple(out)
```

```python
import functools

import jax
import jax.numpy as jnp
from jax import lax
from jax.experimental import pallas as pl
from jax.experimental.pallas import tpu as pltpu

F32 = jnp.float32
BF16 = jnp.bfloat16
MESH = pl.DeviceIdType.MESH

N_META = 16
POOL_WINDOWS = (2, 4, 8, 16)
POOL_HALO = 16
N_HEADS = 4
GLA_GATE_TEMP = 16.0
CHUNK = 128
CHUNK_UNROLL = 5
LN_EPS = 1e-5
RMS_EPS = 1e-6
ADAM_LR = 0.001
ADAM_B1 = 0.9
ADAM_B2 = 0.999
ADAM_EPS = 1e-08
ADAM_WD = 0.01
ADAM_STEP = 10
LANE = 128
BF16_ROWS = 16
ROW_TILE = 640
FFN_ROW_TILE = 640
FFN_CHUNKS = 4
ROW_GROUPS = 2
FFN_SPLIT = 2
WGRAD_K_MAX = 2176
N_SHARD = 4
N_DEV = 8

BIG = ("ffn1_w_gate", "ffn1_w_up", "ffn1_w_down", "w_in", "w_gate_up", "w_out",
       "ffn2_w_gate", "ffn2_w_up", "ffn2_w_down")
SMALL = ("ln1_g", "ln1_b", "b_gate", "w_pool", "pool_scale", "gla_norm_g", "ln2_g", "ln2_b", "ln3_g", "ln3_b")
WEIGHTS = ("meta_tokens", "ffn1_w_gate", "ffn1_w_up", "ffn1_w_down", "ln1_g", "ln1_b", "w_in", "w_gate_up",
           "b_gate", "w_pool", "pool_scale", "gla_norm_g", "w_out", "ln2_g", "ln2_b", "ffn2_w_gate",
           "ffn2_w_up", "ffn2_w_down", "ln3_g", "ln3_b")


def _tc_call(body, **kw):
    return pl.pallas_call(body, **kw)


def _comm_call(body, **kw):
    return pl.pallas_call(body, **kw)


def _seq(n):
    return pltpu.CompilerParams(dimension_semantics=("arbitrary",) * n)


def _mm(a, b):
    return jnp.dot(a.astype(BF16), b.astype(BF16), preferred_element_type=F32)


def _mm_nt(a, b):
    return lax.dot_general(a.astype(BF16), b.astype(BF16), (((1,), (1,)), ((), ())), preferred_element_type=F32)


def _mm_tn(a, b):
    return lax.dot_general(a.astype(BF16), b.astype(BF16), (((0,), (0,)), ((), ())), preferred_element_type=F32)


def _mm_01(a, b):
    hi = b.astype(BF16)
    lo = (b - hi.astype(F32)).astype(BF16)
    a = a.astype(BF16)
    return jnp.dot(a, hi, preferred_element_type=F32) + jnp.dot(a, lo, preferred_element_type=F32)


def _row_tile(t, most=None):
    tm = min(most or ROW_TILE, t)
    while t % tm:
        tm -= LANE
    return tm


def _silu_parts(g):
    sg = jax.nn.sigmoid(g)
    return sg, g * sg


def _ln_stats(r):
    mu = jnp.mean(r, axis=-1, keepdims=True)
    rc = r - mu
    var = jnp.mean(rc * rc, axis=-1, keepdims=True)
    rs = lax.rsqrt(var + LN_EPS)
    return rc * rs, rs


def _ln_bwd(dy, xh, rs, gam):
    dyg = dy * gam
    c1 = jnp.mean(dyg, axis=-1, keepdims=True)
    c2 = jnp.mean(dyg * xh, axis=-1, keepdims=True)
    return rs * (dyg - c1 - xh * c2)


def _ffn_fwd(xin, gam_in, bet_in, wg, wu, wd, alpha, name):
    t, d = xin.shape
    nj, tf, _ = wg.shape
    tm = _row_tile(t, FFN_ROW_TILE)

    def body(x_ref, gi_ref, bi_ref, wg_ref, wu_ref, wd_ref, xhat_ref, rstd_ref, hb_ref, go_ref, uo_ref, acc, hbs):
        j = pl.program_id(1)

        @pl.when(j == 0)
        def _():
            h = x_ref[...] * gi_ref[...] + bi_ref[...]
            hb = h.astype(BF16)
            hbs[...] = hb
            hb_ref[...] = hb
            acc[...] = (2.0 * alpha) * h

        hb = hbs[...]
        g = _mm_nt(hb, wg_ref[...])
        u = _mm_nt(hb, wu_ref[...])
        _, sl = _silu_parts(g)
        go_ref[...] = g.astype(BF16)
        uo_ref[...] = u.astype(BF16)
        acc[...] += jnp.dot((sl * u).astype(BF16), wd_ref[...], preferred_element_type=F32)

        @pl.when(j == nj - 1)
        def _():
            xhat, rs = _ln_stats(0.5 * acc[...])
            xhat_ref[...] = xhat
            rstd_ref[...] = rs

    row = lambda i, j: (i, 0)
    vec = pl.BlockSpec((1, d), lambda i, j: (0, 0))
    return _tc_call(
        body, name=name, grid=(t // tm, nj),
        in_specs=[pl.BlockSpec((tm, d), row), vec, vec] + [pl.BlockSpec((None, tf, d), lambda i, j: (j, 0, 0))] * 3,
        out_specs=[pl.BlockSpec((tm, d), row), pl.BlockSpec((tm, 1), row), pl.BlockSpec((tm, d), row),
                   pl.BlockSpec((None, tm, tf), lambda i, j: (j, i, 0)),
                   pl.BlockSpec((None, tm, tf), lambda i, j: (j, i, 0))],
        out_shape=[jax.ShapeDtypeStruct((t, d), F32), jax.ShapeDtypeStruct((t, 1), F32),
                   jax.ShapeDtypeStruct((t, d), BF16), jax.ShapeDtypeStruct((nj, t, tf), BF16),
                   jax.ShapeDtypeStruct((nj, t, tf), BF16)],
        scratch_shapes=[pltpu.VMEM((tm, d), F32), pltpu.VMEM((tm, d), BF16)],
        compiler_params=_seq(2),
    )(xin, gam_in, bet_in, wg, wu, wd)


def _ffn_bwd(dh, xhat, rstd, ln_g, gb, ub, wg, wu, wd, alpha, name):
    t, d = dh.shape
    nj, tf, _ = wg.shape
    tm = _row_tile(t, FFN_ROW_TILE)

    def body(dh_ref, xh_ref, rs_ref, g_ref, gb_ref, ub_ref, wg_ref, wu_ref, wd_ref,
             dhin_ref, df_ref, dg_ref, du_ref, act_ref, dgam_ref, dbet_ref, df_s):
        i = pl.program_id(0)
        j = pl.program_id(1)

        @pl.when(j == 0)
        def _():
            dy = dh_ref[...]
            xh = xh_ref[...]
            dr = _ln_bwd(dy, xh, rs_ref[...], g_ref[...])
            dhin_ref[...] = alpha * dr
            dfb = (0.5 * dr).astype(BF16)
            df_s[...] = dfb
            df_ref[...] = dfb

            @pl.when(i == 0)
            def _():
                dgam_ref[...] = jnp.zeros_like(dgam_ref)
                dbet_ref[...] = jnp.zeros_like(dbet_ref)

            dgam_ref[...] += jnp.sum(dy * xh, axis=0, keepdims=True)
            dbet_ref[...] += jnp.sum(dy, axis=0, keepdims=True)

        for part in range(FFN_SPLIT):
            rows = pl.ds(part * (tm // FFN_SPLIT), tm // FFN_SPLIT)
            dact = _mm_nt(df_s[rows, :], wd_ref[...])
            g = gb_ref[rows, :].astype(F32)
            u = ub_ref[rows, :].astype(F32)
            sg, sl = _silu_parts(g)
            dg = (dact * u * (sg * (1.0 + g * (1.0 - sg)))).astype(BF16)
            du = (dact * sl).astype(BF16)
            dg_ref[rows, :] = dg
            du_ref[rows, :] = du
            act_ref[rows, :] = (sl * u).astype(BF16)
            dhin_ref[rows, :] += _mm(dg, wg_ref[...]) + _mm(du, wu_ref[...])

    row = lambda i, j: (i, 0)
    col = pl.BlockSpec((None, tm, tf), lambda i, j: (j, i, 0))
    vec = pl.BlockSpec((1, d), lambda i, j: (0, 0))
    ff = jax.ShapeDtypeStruct((nj, t, tf), BF16)
    return _tc_call(
        body, name=name, grid=(t // tm, nj),
        in_specs=[pl.BlockSpec((tm, d), row), pl.BlockSpec((tm, d), row), pl.BlockSpec((tm, 1), row), vec,
                  col, col] + [pl.BlockSpec((None, tf, d), lambda i, j: (j, 0, 0))] * 3,
        out_specs=[pl.BlockSpec((tm, d), row), pl.BlockSpec((tm, d), row), col, col, col, vec, vec],
        out_shape=[jax.ShapeDtypeStruct((t, d), F32), jax.ShapeDtypeStruct((t, d), BF16), ff, ff, ff,
                   jax.ShapeDtypeStruct((1, d), F32), jax.ShapeDtypeStruct((1, d), F32)],
        scratch_shapes=[pltpu.VMEM((tm, d), BF16)],
        compiler_params=_seq(2),
    )(dh, xhat, rstd, ln_g, gb, ub, wg, wu, wd)


def _wgrad(a, b, tmm, tn, name, after=None):
    t = a.shape[-2]
    m = a.shape[-1] * (a.shape[0] if a.ndim == 3 else 1)
    n = b.shape[-1] * (b.shape[0] if b.ndim == 3 else 1)
    tk = max(k for k in range(BF16_ROWS, WGRAD_K_MAX + 1, BF16_ROWS) if t % k == 0)
    nk = t // tk
    extra = [] if after is None else [after]

    def body(a_ref, b_ref, *rest):
        o_ref, acc = rest[len(extra):]
        k = pl.program_id(2)

        @pl.when(k == 0)
        def _():
            acc[...] = jnp.zeros_like(acc)

        acc[...] += _mm_tn(a_ref[...], b_ref[...])

        @pl.when(k == nk - 1)
        def _():
            o_ref[...] = acc[...].astype(o_ref.dtype)

    a_spec = (pl.BlockSpec((None, tk, tmm), lambda i, j, k: (i, k, 0)) if a.ndim == 3
              else pl.BlockSpec((tk, tmm), lambda i, j, k: (k, i)))
    return _tc_call(
        body, name=name, grid=(m // tmm, n // tn, nk),
        in_specs=[a_spec, pl.BlockSpec((None, tk, tn), lambda i, j, k: (j, k, 0)) if b.ndim == 3
                  else pl.BlockSpec((tk, tn), lambda i, j, k: (k, j))] + [pl.BlockSpec(memory_space=pl.ANY)] * len(extra),
        out_specs=pl.BlockSpec((tmm, tn), lambda i, j, k: (i, j)),
        out_shape=jax.ShapeDtypeStruct((m, n), BF16),
        scratch_shapes=[pltpu.VMEM((tmm, tn), F32)],
        compiler_params=_seq(3),
    )(a, b, *extra)


def _inproj_fwd(xhat, gam, bet, w_main, w_lr, wgu, b_gate, widths, name):
    t, d = xhat.shape
    tm = _row_tile(t)
    kw = wgu.shape[1]
    offs = [0]
    for w in widths:
        offs.append(offs[-1] + w)

    def body(x_ref, g_ref, b_ref, wm_ref, wl_ref, wgu_ref, bg_ref, *outs):
        piece_refs, (zg_ref, la_ref, hb_ref) = outs[:len(widths)], outs[len(widths):]
        hb = (x_ref[...] * g_ref[...] + b_ref[...]).astype(BF16)
        hb_ref[...] = hb
        for p, ref in enumerate(piece_refs):
            ref[...] = _mm_nt(hb, wm_ref[offs[p]:offs[p + 1], :])
        zg = _mm_nt(hb, wl_ref[...])
        zg_ref[...] = zg
        logit = _mm(zg, wgu_ref[...]) + bg_ref[...]
        la_ref[...] = (jnp.minimum(logit, 0.0) - jnp.log(1.0 + jnp.exp(-jnp.abs(logit)))) * (1.0 / GLA_GATE_TEMP)

    row = lambda i: (i, 0)
    full = lambda a: pl.BlockSpec(a.shape, lambda i: (0,) * a.ndim)
    out_w = list(widths) + [LANE, kw]
    return _tc_call(
        body, name=name, grid=(t // tm,),
        in_specs=[pl.BlockSpec((tm, d), row), full(gam), full(bet), full(w_main), full(w_lr), full(wgu), full(b_gate)],
        out_specs=[pl.BlockSpec((tm, w), row) for w in out_w] + [pl.BlockSpec((tm, d), row)],
        out_shape=[jax.ShapeDtypeStruct((t, w), F32) for w in out_w] + [jax.ShapeDtypeStruct((t, d), BF16)],
        compiler_params=_seq(1),
    )(xhat, gam, bet, w_main, w_lr, wgu, b_gate)


def _inproj_bwd(dh_part, pieces, dzg, w_main, w_lr, name):
    t, d = dh_part.shape
    tm = _row_tile(t)
    widths = [p.shape[1] for p in pieces]
    offs = [0]
    for w in widths:
        offs.append(offs[-1] + w)

    def body(*refs):
        dhp_ref = refs[0]
        p_refs = refs[1:1 + len(widths)]
        dzg_ref, wm_ref, wl_ref, dh_ref, dz_ref = refs[1 + len(widths):]
        acc = dhp_ref[...] + _mm(dzg_ref[...], wl_ref[...])
        for p, ref in enumerate(p_refs):
            v = ref[...]
            dz_ref[:, offs[p]:offs[p + 1]] = v
            acc += _mm(v, wm_ref[offs[p]:offs[p + 1], :])
        dh_ref[...] = acc

    row = lambda i: (i, 0)
    full = lambda a: pl.BlockSpec(a.shape, lambda i: (0,) * a.ndim)
    return _tc_call(
        body, name=name, grid=(t // tm,),
        in_specs=[pl.BlockSpec((tm, d), row)] + [pl.BlockSpec((tm, w), row) for w in widths]
                 + [pl.BlockSpec((tm, LANE), row), full(w_main), full(w_lr)],
        out_specs=[pl.BlockSpec((tm, d), row), pl.BlockSpec((tm, offs[-1]), row)],
        out_shape=[jax.ShapeDtypeStruct((t, d), F32), jax.ShapeDtypeStruct((t, offs[-1]), BF16)],
        compiler_params=_seq(1),
    )(dh_part, *pieces, dzg, w_main, w_lr)


def _pool_cnt(tile, tm, w):
    t = tile * tm + lax.broadcasted_iota(jnp.int32, (tm, 1), 0)
    return jnp.minimum(t + 1, w).astype(F32)


def _pool_fwd(u, wp, scale, name):
    t, pw = u.shape
    tm = _row_tile(t)
    gd = wp.shape[1]

    def body(u_ref, wp_ref, sc_ref, y_ref, p_ref, ext):
        i = pl.program_id(0)

        @pl.when(i == 0)
        def _():
            ext[0:POOL_HALO, :] = jnp.zeros((POOL_HALO, pw), F32)

        ext[POOL_HALO:POOL_HALO + tm, :] = u_ref[...]
        for gi, w in enumerate(POOL_WINDOWS):
            cols = slice(gi * gd, (gi + 1) * gd)
            s = ext[pl.ds(POOL_HALO, tm), cols]
            tot = s
            for back in range(1, w):
                tot = tot + ext[pl.ds(POOL_HALO - back, tm), cols]
            p = (tot / _pool_cnt(i, tm, w) - s).astype(BF16)
            p_ref[:, cols] = p
            y_ref[:, cols] = (jnp.dot(p, wp_ref[gi], preferred_element_type=F32) * sc_ref[:, cols]).astype(BF16)
        ext[0:POOL_HALO, :] = ext[tm:tm + POOL_HALO, :]

    row = lambda i: (i, 0)
    return _tc_call(
        body, name=name, grid=(t // tm,),
        in_specs=[pl.BlockSpec((tm, pw), row), pl.BlockSpec(wp.shape, lambda i: (0, 0, 0)),
                  pl.BlockSpec((1, pw), lambda i: (0, 0))],
        out_specs=[pl.BlockSpec((tm, pw), row), pl.BlockSpec((tm, pw), row)],
        out_shape=[jax.ShapeDtypeStruct((t, pw), BF16), jax.ShapeDtypeStruct((t, pw), BF16)],
        scratch_shapes=[pltpu.VMEM((tm + POOL_HALO, pw), F32)],
        compiler_params=_seq(1),
    )(u, wp, scale)


def _pool_bwd(dy, pb, wp, scale, name):
    t, pw = dy.shape
    tm = _row_tile(t)
    nt = t // tm
    gd = wp.shape[1]

    def body(dy_ref, p_ref, wp_ref, sc_ref, du_ref, dwp_ref, dsc_ref, ext):
        i = pl.program_id(0)
        tile = nt - 1 - i

        @pl.when(i == 0)
        def _():
            ext[tm:tm + POOL_HALO, :] = jnp.zeros((POOL_HALO, pw), F32)
            dwp_ref[...] = jnp.zeros_like(dwp_ref)
            dsc_ref[...] = jnp.zeros_like(dsc_ref)

        dps = []
        for gi, w in enumerate(POOL_WINDOWS):
            cols = slice(gi * gd, (gi + 1) * gd)
            dyv = dy_ref[:, cols]
            p = p_ref[:, cols]
            dpre = (dyv * sc_ref[:, cols]).astype(BF16)
            dsc_ref[:, cols] += jnp.sum(dyv * jnp.dot(p, wp_ref[gi], preferred_element_type=F32), axis=0, keepdims=True)
            dwp_ref[gi] += _mm_tn(p, dpre)
            dp = _mm_nt(dpre, wp_ref[gi])
            dps.append(dp)
            ext[0:tm, cols] = dp / _pool_cnt(tile, tm, w)
        for gi, w in enumerate(POOL_WINDOWS):
            cols = slice(gi * gd, (gi + 1) * gd)
            tot = ext[pl.ds(0, tm), cols]
            for fwd in range(1, w):
                tot = tot + ext[pl.ds(fwd, tm), cols]
            du_ref[:, cols] = (tot - dps[gi]).astype(BF16)
        ext[tm:tm + POOL_HALO, :] = ext[0:POOL_HALO, :]

    row = lambda i: (nt - 1 - i, 0)
    return _tc_call(
        body, name=name, grid=(nt,),
        in_specs=[pl.BlockSpec((tm, pw), row), pl.BlockSpec((tm, pw), row),
                  pl.BlockSpec(wp.shape, lambda i: (0, 0, 0)), pl.BlockSpec((1, pw), lambda i: (0, 0))],
        out_specs=[pl.BlockSpec((tm, pw), row), pl.BlockSpec(wp.shape, lambda i: (0, 0, 0)),
                   pl.BlockSpec((1, pw), lambda i: (0, 0))],
        out_shape=[jax.ShapeDtypeStruct((t, pw), BF16), jax.ShapeDtypeStruct(wp.shape, F32),
                   jax.ShapeDtypeStruct((1, pw), F32)],
        scratch_shapes=[pltpu.VMEM((tm + POOL_HALO, pw), F32)],
        compiler_params=_seq(1),
    )(dy, pb, wp, scale)


def _gla_masks(kw, vw):
    dk, dv = kw // N_HEADS, vw // N_HEADS
    lane_k = lax.broadcasted_iota(jnp.int32, (1, kw), 1)
    lane_v = lax.broadcasted_iota(jnp.int32, (1, vw), 1)
    hk = [((lane_k >= h * dk) & (lane_k < (h + 1) * dk)).astype(F32) for h in range(N_HEADS)]
    hv = [((lane_v >= h * dv) & (lane_v < (h + 1) * dv)).astype(F32) for h in range(N_HEADS)]
    r = lax.broadcasted_iota(jnp.int32, (CHUNK, CHUNK), 0)
    c = lax.broadcasted_iota(jnp.int32, (CHUNK, CHUNK), 1)
    tril = r >= c
    rs = lax.broadcasted_iota(jnp.int32, (N_HEADS * CHUNK, CHUNK), 0) & (CHUNK - 1)
    stril = rs >= lax.broadcasted_iota(jnp.int32, (N_HEADS * CHUNK, CHUNK), 1)
    return hk, hv, tril, stril


def _block_diag(x, hk, dv):
    return jnp.concatenate([x[h * dv:(h + 1) * dv, :] * hk[h] for h in range(N_HEADS)], axis=0)


def _gla_fwd(q, k, v, loga, r, gnorm, name):
    t, kw = q.shape
    vw = v.shape[1]
    dk, dv = kw // N_HEADS, vw // N_HEADS
    tm = _row_tile(t)
    nc = tm // CHUNK
    qscale = dk ** -0.5

    def body(q_ref, k_ref, v_ref, la_ref, r_ref, gn_ref, o_ref, y_ref, sall_ref, st):
        @pl.when(pl.program_id(0) == 0)
        def _():
            st[...] = jnp.zeros_like(st)

        hk, hv, tril, stril = _gla_masks(kw, vw)
        trif = tril.astype(F32)

        def chunk(c, carry):
            rows = pl.ds(pl.multiple_of(c * CHUNK, CHUNK), CHUNK)
            la = la_ref[rows, :]
            b = _mm_01(trif, la)
            bl = jnp.sum(la, axis=0, keepdims=True)
            qb = q_ref[rows, :] * (qscale * jnp.exp(b))
            kk = k_ref[rows, :]
            kb = kk * jnp.exp(-b)
            kl = kk * jnp.exp(bl - b)
            vv = v_ref[rows, :]
            s_t = st[...]
            compact = s_t[0:dv, :]
            for h in range(1, N_HEADS):
                compact = compact + s_t[h * dv:(h + 1) * dv, :]
            sall_ref[c] = compact
            qx = jnp.concatenate([qb.astype(BF16) * hk[h].astype(BF16) for h in range(N_HEADS)], axis=0)
            a = jnp.where(stril, _mm_nt(qx, kb), 0.0).astype(BF16)
            o_inter = _mm_nt(qb, s_t)
            for h in range(N_HEADS):
                vs = slice(h * dv, (h + 1) * dv)
                o_ref[rows, vs] = o_inter[:, vs] + _mm(a[h * CHUNK:(h + 1) * CHUNK, :], vv[:, vs])
            st[...] = s_t * jnp.exp(bl) + _block_diag(_mm_tn(vv, kl), hk, dv)
            return carry

        lax.fori_loop(0, nc, chunk, 0, unroll=CHUNK_UNROLL)
        for h in range(N_HEADS):
            vs = slice(h * dv, (h + 1) * dv)
            oh = o_ref[:, vs]
            on = oh * lax.rsqrt(jnp.mean(oh * oh, axis=-1, keepdims=True) + RMS_EPS)
            _, sl = _silu_parts(r_ref[:, vs])
            y_ref[:, vs] = (on * gn_ref[:, vs] * sl).astype(BF16)

    row = lambda i: (i, 0)
    return _tc_call(
        body, name=name, grid=(t // tm,),
        in_specs=[pl.BlockSpec((tm, kw), row), pl.BlockSpec((tm, kw), row), pl.BlockSpec((tm, vw), row),
                  pl.BlockSpec((tm, kw), row), pl.BlockSpec((tm, vw), row), pl.BlockSpec((1, vw), lambda i: (0, 0))],
        out_specs=[pl.BlockSpec((tm, vw), row), pl.BlockSpec((tm, vw), row),
                   pl.BlockSpec((nc, dv, kw), lambda i: (i, 0, 0))],
        out_shape=[jax.ShapeDtypeStruct((t, vw), F32), jax.ShapeDtypeStruct((t, vw), BF16),
                   jax.ShapeDtypeStruct((t // CHUNK, dv, kw), F32)],
        scratch_shapes=[pltpu.VMEM((vw, kw), F32)],
        compiler_params=_seq(1),
    )(q, k, v, loga, r, gnorm)


def _gla_bwd(dy, o, r, gnorm, q, k, v, loga, zg, sall, wgu, name):
    t, kw = q.shape
    vw = v.shape[1]
    dk, dv = kw // N_HEADS, vw // N_HEADS
    tm = _row_tile(t)
    nt = t // tm
    nc = tm // CHUNK
    qscale = dk ** -0.5

    def body(dy_ref, o_ref, r_ref, gn_ref, q_ref, k_ref, v_ref, la_ref, zg_ref, sall_ref, wgu_ref,
             dq_ref, dk_ref, dv_ref, dr_ref, dzg_ref, dwgu_ref, dbg_ref, dgn_ref, dst, do_s):
        @pl.when(pl.program_id(0) == 0)
        def _():
            dst[...] = jnp.zeros_like(dst)
            dwgu_ref[...] = jnp.zeros_like(dwgu_ref)
            dbg_ref[...] = jnp.zeros_like(dbg_ref)
            dgn_ref[...] = jnp.zeros_like(dgn_ref)

        for h in range(N_HEADS):
            vs = slice(h * dv, (h + 1) * dv)
            oh = o_ref[:, vs]
            rinv = lax.rsqrt(jnp.mean(oh * oh, axis=-1, keepdims=True) + RMS_EPS)
            on = oh * rinv
            rr = r_ref[:, vs]
            sg, sl = _silu_parts(rr)
            dyv = dy_ref[:, vs]
            gn = gn_ref[:, vs]
            dgn_ref[:, vs] += jnp.sum(dyv * on * sl, axis=0, keepdims=True)
            dr_ref[:, vs] = (dyv * on * gn * (sg * (1.0 + rr * (1.0 - sg)))).astype(BF16)
            don = dyv * gn * sl
            do_s[:, vs] = rinv * (don - on * jnp.mean(don * on, axis=-1, keepdims=True))

        hk, hv, tril, stril = _gla_masks(kw, vw)
        trif = tril.astype(F32)
        triuf = (lax.broadcasted_iota(jnp.int32, (CHUNK, CHUNK), 0)
                 <= lax.broadcasted_iota(jnp.int32, (CHUNK, CHUNK), 1)).astype(F32)
        last_row = lax.broadcasted_iota(jnp.int32, (CHUNK, 1), 0) == CHUNK - 1

        def chunk(idx, carry):
            c = nc - 1 - idx
            rows = pl.ds(pl.multiple_of(c * CHUNK, CHUNK), CHUNK)
            la = la_ref[rows, :]
            b = _mm_01(trif, la)
            bl = jnp.sum(la, axis=0, keepdims=True)
            eb = jnp.exp(b)
            enb = jnp.exp(-b)
            ebl = jnp.exp(bl - b)
            el = jnp.exp(bl)
            qb = q_ref[rows, :] * (qscale * eb)
            kk = k_ref[rows, :]
            kb = kk * enb
            kl = kk * ebl
            vv = v_ref[rows, :]
            do = do_s[rows, :]
            compact = sall_ref[c]
            s_t = jnp.concatenate([compact * hk[h] for h in range(N_HEADS)], axis=0)
            ds_t = dst[...]
            qx = jnp.concatenate([qb.astype(BF16) * hk[h].astype(BF16) for h in range(N_HEADS)], axis=0)
            dox = jnp.concatenate([do.astype(BF16) * hv[h].astype(BF16) for h in range(N_HEADS)], axis=0)
            a = jnp.where(stril, _mm_nt(qx, kb), 0.0).astype(BF16)
            da = jnp.where(stril, _mm_nt(dox, vv), 0.0).astype(BF16)
            dv_ref[rows, :] = (_mm_tn(a, dox) + _mm_nt(kl, ds_t)).astype(BF16)
            dak = _mm(da, kb)
            dqb = _mm(do, s_t)
            for h in range(N_HEADS):
                dqb = dqb + dak[h * CHUNK:(h + 1) * CHUNK, :] * hk[h]
            dkb = _mm_tn(da, qx)
            dkl = _mm(vv, ds_t)
            dbl = jnp.sum(dkl * kl, axis=0, keepdims=True) + el * jnp.sum(ds_t * s_t, axis=0, keepdims=True)
            dst[...] = ds_t * el + _block_diag(_mm_tn(do, qb), hk, dv)
            dq_ref[rows, :] = (dqb * (qscale * eb)).astype(BF16)
            dk_ref[rows, :] = (dkb * enb + dkl * ebl).astype(BF16)
            db = dqb * qb - dkb * kb - dkl * kl + jnp.where(last_row, dbl, 0.0)
            dla = _mm_01(triuf, db)
            dlogit = dla * (1.0 / GLA_GATE_TEMP) * (1.0 - jnp.exp(GLA_GATE_TEMP * la))
            dzg_ref[rows, :] = _mm_nt(dlogit, wgu_ref[...]).astype(BF16)
            dwgu_ref[...] += _mm_tn(zg_ref[rows, :], dlogit)
            dbg_ref[...] += jnp.sum(dlogit, axis=0, keepdims=True)
            return carry

        lax.fori_loop(0, nc, chunk, 0, unroll=CHUNK_UNROLL)

    row = lambda i: (nt - 1 - i, 0)
    const = lambda i: (0, 0)
    return _tc_call(
        body, name=name, grid=(nt,),
        in_specs=[pl.BlockSpec((tm, vw), row), pl.BlockSpec((tm, vw), row), pl.BlockSpec((tm, vw), row),
                  pl.BlockSpec((1, vw), const), pl.BlockSpec((tm, kw), row), pl.BlockSpec((tm, kw), row),
                  pl.BlockSpec((tm, vw), row), pl.BlockSpec((tm, kw), row), pl.BlockSpec((tm, LANE), row),
                  pl.BlockSpec((nc, dv, kw), lambda i: (nt - 1 - i, 0, 0)), pl.BlockSpec((LANE, kw), const)],
        out_specs=[pl.BlockSpec((tm, kw), row), pl.BlockSpec((tm, kw), row), pl.BlockSpec((tm, vw), row),
                   pl.BlockSpec((tm, vw), row), pl.BlockSpec((tm, LANE), row), pl.BlockSpec((LANE, kw), const),
                   pl.BlockSpec((1, kw), const), pl.BlockSpec((1, vw), const)],
        out_shape=[jax.ShapeDtypeStruct((t, kw), BF16), jax.ShapeDtypeStruct((t, kw), BF16),
                   jax.ShapeDtypeStruct((t, vw), BF16), jax.ShapeDtypeStruct((t, vw), BF16),
                   jax.ShapeDtypeStruct((t, LANE), BF16), jax.ShapeDtypeStruct((LANE, kw), F32),
                   jax.ShapeDtypeStruct((1, kw), F32), jax.ShapeDtypeStruct((1, vw), F32)],
        scratch_shapes=[pltpu.VMEM((vw, kw), F32), pltpu.VMEM((tm, vw), F32)],
        compiler_params=_seq(1),
    )(dy, o, r, gnorm, q, k, v, loga, zg, sall, wgu)


def _outproj_fwd(yp, yg, w_out, xhat, gam, bet, alpha, name):
    t, d = xhat.shape
    pw = yp.shape[1]
    tm = _row_tile(t)

    def body(yp_ref, yg_ref, w_ref, x_ref, g_ref, b_ref, xhat_ref, rstd_ref):
        for part in range(ROW_GROUPS):
            rows = pl.ds(part * (tm // ROW_GROUPS), tm // ROW_GROUPS)
            h = x_ref[rows, :] * g_ref[...] + b_ref[...]
            y = (jnp.dot(yp_ref[rows, :], w_ref[0:pw, :], preferred_element_type=F32)
                 + jnp.dot(yg_ref[rows, :], w_ref[pw:, :], preferred_element_type=F32))
            xh, rs = _ln_stats(alpha * h + y)
            xhat_ref[rows, :] = xh
            rstd_ref[rows, :] = rs

    row = lambda i: (i, 0)
    vec = pl.BlockSpec((1, d), lambda i: (0, 0))
    return _tc_call(
        body, name=name, grid=(t // tm,),
        in_specs=[pl.BlockSpec((tm, pw), row), pl.BlockSpec((tm, yg.shape[1]), row),
                  pl.BlockSpec(w_out.shape, lambda i: (0, 0)), pl.BlockSpec((tm, d), row), vec, vec],
        out_specs=[pl.BlockSpec((tm, d), row), pl.BlockSpec((tm, 1), row)],
        out_shape=[jax.ShapeDtypeStruct((t, d), F32), jax.ShapeDtypeStruct((t, 1), F32)],
        compiler_params=_seq(1),
    )(yp, yg, w_out, xhat, gam, bet)


def _outproj_bwd(dh, xhat, rstd, ln_g, w_out, pw, alpha, name):
    t, d = dh.shape
    tm = _row_tile(t)
    gw = w_out.shape[0] - pw

    def body(dh_ref, xh_ref, rs_ref, g_ref, w_ref, dyb_ref, dyp_ref, dyg_ref, dres_ref, dgam_ref, dbet_ref):
        @pl.when(pl.program_id(0) == 0)
        def _():
            dgam_ref[...] = jnp.zeros_like(dgam_ref)
            dbet_ref[...] = jnp.zeros_like(dbet_ref)

        for part in range(ROW_GROUPS):
            rows = pl.ds(part * (tm // ROW_GROUPS), tm // ROW_GROUPS)
            dy = dh_ref[rows, :]
            xh = xh_ref[rows, :]
            dr = _ln_bwd(dy, xh, rs_ref[rows, :], g_ref[...])
            dgam_ref[...] += jnp.sum(dy * xh, axis=0, keepdims=True)
            dbet_ref[...] += jnp.sum(dy, axis=0, keepdims=True)
            drb = dr.astype(BF16)
            dyb_ref[rows, :] = drb
            dres_ref[rows, :] = alpha * dr
            dyp_ref[rows, :] = _mm_nt(drb, w_ref[0:pw, :])
            dyg_ref[rows, :] = _mm_nt(drb, w_ref[pw:, :])

    row = lambda i: (i, 0)
    vec = pl.BlockSpec((1, d), lambda i: (0, 0))
    return _tc_call(
        body, name=name, grid=(t // tm,),
        in_specs=[pl.BlockSpec((tm, d), row), pl.BlockSpec((tm, d), row), pl.BlockSpec((tm, 1), row), vec,
                  pl.BlockSpec(w_out.shape, lambda i: (0, 0))],
        out_specs=[pl.BlockSpec((tm, d), row), pl.BlockSpec((tm, pw), row), pl.BlockSpec((tm, gw), row),
                   pl.BlockSpec((tm, d), row), vec, vec],
        out_shape=[jax.ShapeDtypeStruct((t, d), BF16), jax.ShapeDtypeStruct((t, pw), F32),
                   jax.ShapeDtypeStruct((t, gw), F32), jax.ShapeDtypeStruct((t, d), F32),
                   jax.ShapeDtypeStruct((1, d), F32), jax.ShapeDtypeStruct((1, d), F32)],
        compiler_params=_seq(1),
    )(dh, xhat, rstd, ln_g, w_out)


def _loss_head(xhat, gam, bet, target, n_rows, name):
    t, d = xhat.shape
    tm = _row_tile(t)

    def body(x_ref, g_ref, b_ref, t_ref, dy_ref, loss_ref):
        i = pl.program_id(0)

        @pl.when(i == 0)
        def _():
            loss_ref[...] = jnp.zeros_like(loss_ref)

        rowi = i * tm + lax.broadcasted_iota(jnp.int32, (tm, 1), 0)
        live = (rowi >= N_META) & (rowi < N_META + n_rows)
        diff = jnp.where(live, x_ref[...] * g_ref[...] + b_ref[...] - t_ref[...], 0.0)
        dy_ref[...] = diff * (1.0 / d)
        loss_ref[...] += jnp.sum(diff * diff) * (0.5 / d)

    row = lambda i: (i, 0)
    vec = pl.BlockSpec((1, d), lambda i: (0, 0))
    return _tc_call(
        body, name=name, grid=(t // tm,),
        in_specs=[pl.BlockSpec((tm, d), row), vec, vec, pl.BlockSpec((tm, d), row)],
        out_specs=[pl.BlockSpec((tm, d), row), pl.BlockSpec((8, LANE), lambda i: (0, 0))],
        out_shape=[jax.ShapeDtypeStruct((t, d), F32), jax.ShapeDtypeStruct((8, LANE), F32)],
        compiler_params=_seq(1),
    )(xhat, gam, bet, target)


def _rows_block(r, c):
    best = r
    for cand in range(BF16_ROWS, r, BF16_ROWS):
        if r % cand == 0 and cand * c * 4 <= (1 << 20):
            best = cand
    return best if best * c * 4 <= (4 << 20) else r


def _sum_slots(me, mine, recvs, name, layers_side_by_side=False):
    nl = len(recvs)
    ns, r, c = recvs[0].shape
    tr = _rows_block(r, c)

    def body(me_ref, *refs):
        o_ref = refs[nl * ns]
        for l in range(nl):
            acc = refs[l * ns][...].astype(F32)
            for s in range(1, ns):
                acc = acc + refs[l * ns + s][...].astype(F32)
            if layers_side_by_side:
                o_ref[0, :, l * c:(l + 1) * c] = acc.astype(o_ref.dtype)
            else:
                o_ref[l] = acc.astype(o_ref.dtype)

    def slot(s):
        return pl.BlockSpec((None, tr, c), lambda i, me_ref: ((me_ref[0] + s) % ns, i, 0))

    out = (1, r, nl * c) if layers_side_by_side else (nl, r, c)
    operands = []
    for l in range(nl):
        operands += [mine[l]] + [recvs[l]] * (ns - 1)
    return _tc_call(
        body, name=name,
        grid_spec=pltpu.PrefetchScalarGridSpec(
            num_scalar_prefetch=1, grid=(r // tr,),
            in_specs=[slot(s) for s in range(ns)] * nl,
            out_specs=pl.BlockSpec((out[0], tr, out[2]), lambda i, me_ref: (0, i, 0))),
        out_shape=jax.ShapeDtypeStruct(out, recvs[0].dtype),
        compiler_params=_seq(1),
    )(me, *operands)


def _adamw(w, terms, m, v, name):
    nl, r, c = w.shape
    tc = c
    while tc % (2 * LANE) == 0 and tc > 4 * LANE:
        tc //= 2
    tr = _rows_block(r, tc)
    nterm = len(terms)

    def body(*refs):
        w_ref = refs[0]
        t_refs = refs[1:1 + nterm]
        m_ref, v_ref, g_ref, d_ref, nm_ref, nv_ref = refs[1 + nterm:]
        g = t_refs[0][...].astype(F32)
        for tr_ in t_refs[1:]:
            g = g + tr_[...].astype(F32)
        nm = ADAM_B1 * m_ref[...] + (1.0 - ADAM_B1) * g
        nv = ADAM_B2 * v_ref[...] + (1.0 - ADAM_B2) * jnp.square(g)
        m_hat = nm / (1.0 - ADAM_B1 ** ADAM_STEP)
        v_hat = nv / (1.0 - ADAM_B2 ** ADAM_STEP)
        g_ref[...] = g
        d_ref[...] = -ADAM_LR * (m_hat / (jnp.sqrt(v_hat) + ADAM_EPS) + ADAM_WD * w_ref[...])
        nm_ref[...] = nm
        nv_ref[...] = nv

    spec = pl.BlockSpec((None, tr, tc), lambda l, i, j: (l, i, j))
    shp = jax.ShapeDtypeStruct((nl, r, c), F32)
    return _tc_call(
        body, name=name, grid=(nl, r // tr, c // tc),
        in_specs=[spec] * (3 + nterm), out_specs=[spec] * 4, out_shape=[shp] * 4,
        compiler_params=_seq(3),
    )(w, *terms, m, v)


XY_RELATIONS = ((1, 0, 0), (0, 1, 0), (1, 1, 0))
ALL_RELATIONS = tuple((fx, fy, fc) for fx in (0, 1) for fy in (0, 1) for fc in (0, 1) if fx or fy or fc)
HBM_SPEC = pl.BlockSpec(memory_space=pltpu.HBM)
SEM_SPEC = pl.BlockSpec(memory_space=pltpu.SEMAPHORE)
DATAFLOW = pltpu.SideEffectType.DATAFLOW_SIDE_EFFECTING


def _split_call(body, **kw):
    return pl.pallas_call(body, **kw)


def _flip(v, f):
    return 1 - v if f else v


def _any_spec(n):
    return [pl.BlockSpec(memory_space=pl.ANY)] * n


def _relations(kind):
    return ALL_RELATIONS if kind == "bcast" else XY_RELATIONS


def _copies(kind, arr, land, sems):
    x, y, c = lax.axis_index("x"), lax.axis_index("y"), lax.axis_index("c")
    out = []
    for (fx, fy, fc), (send_sem, recv_sem) in zip(_relations(kind), sems):
        px, py, pc = _flip(x, fx), _flip(y, fy), _flip(c, fc)
        if kind == "bcast":
            mine, theirs = 4 * x + 2 * y + c, 4 * px + 2 * py + pc
        else:
            mine, theirs = 2 * x + y, 2 * px + py
        src, to_mine, to_theirs = arr, land.at[mine], land.at[theirs]
        if kind == "scatter":
            src = arr.at[theirs]
        if kind == "gather_half":
            rows = _my_half(arr.shape[0], c)
            src, to_mine, to_theirs = arr.at[rows], land.at[mine, rows], land.at[theirs, rows]
        both = dict(src_ref=src, send_sem=send_sem, recv_sem=recv_sem, device_id=(px, py, pc), device_id_type=MESH)
        out.append((pltpu.make_async_remote_copy(dst_ref=to_mine, **both),
                    pltpu.make_async_remote_copy(dst_ref=to_theirs, **both)))
    return out


def _my_half(nrows, c):
    return pl.ds(c * (nrows // 2), nrows // 2)


def _share_halves(name, kinds, lands):
    ks = [k for k, kd in enumerate(kinds) if kd == "gather_half"]
    n = len(ks)

    def body(*refs):
        l_refs = refs[n:2 * n]
        send_sems, recv_sems = refs[2 * n:]
        x, y, c = lax.axis_index("x"), lax.axis_index("y"), lax.axis_index("c")
        copies = []
        for i in range(n):
            nrows = l_refs[i].shape[1]
            for r, (fx, fy, _) in enumerate(XY_RELATIONS):
                slot = 2 * _flip(x, fx) + _flip(y, fy)
                both = dict(src_ref=l_refs[i].at[slot, _my_half(nrows, c)], send_sem=send_sems.at[i, r],
                            recv_sem=recv_sems.at[i, r], device_id=(x, y, 1 - c), device_id_type=MESH)
                copies.append((pltpu.make_async_remote_copy(dst_ref=l_refs[i].at[slot, _my_half(nrows, c)], **both),
                               pltpu.make_async_remote_copy(dst_ref=l_refs[i].at[slot, _my_half(nrows, 1 - c)], **both)))
        for send, _ in copies:
            send.start()
        for _, arrival in copies:
            arrival.wait_recv()
        for send, _ in copies:
            send.wait_send()

    outs = _comm_call(
        body, name=name,
        in_specs=_any_spec(n), out_specs=_any_spec(n),
        out_shape=[jax.ShapeDtypeStruct(lands[k].shape, lands[k].dtype) for k in ks],
        input_output_aliases={i: i for i in range(n)},
        scratch_shapes=[pltpu.SemaphoreType.DMA((n, 3)), pltpu.SemaphoreType.DMA((n, 3))],
    )(*[lands[k] for k in ks])
    lands = list(lands)
    for k, o in zip(ks, outs):
        lands[k] = o
    return lands


def _sem_pairs(kinds, sems):
    out, at = [], 0
    for kind in kinds:
        nrel = len(_relations(kind))
        out.append([(sems[at + 2 * r], sems[at + 2 * r + 1]) for r in range(nrel)])
        at += 2 * nrel
    return out


def _exchange_start(name, kinds, arrs, lands):
    n = len(arrs)
    nsem = sum(2 * len(_relations(kd)) for kd in kinds)

    def body(*refs):
        a_refs, l_refs = refs[:n], refs[n:2 * n]
        pairs = _sem_pairs(kinds, refs[2 * n:2 * n + nsem])
        token = refs[-1]
        for k in range(n):
            for send, _ in _copies(kinds[k], a_refs[k], l_refs[k], pairs[k]):
                send.start()
        token[...] = jnp.zeros_like(token)

    thru = [pltpu.HBM(a.shape, a.dtype) for a in list(arrs) + list(lands)]
    outs = _split_call(
        body, name=name,
        out_shape=(*[pltpu.SemaphoreType.DMA(())] * nsem, *thru, jax.ShapeDtypeStruct((8, LANE), F32)),
        in_specs=[HBM_SPEC] * (2 * n),
        out_specs=(*[SEM_SPEC] * nsem, *[HBM_SPEC] * (2 * n), pl.BlockSpec(memory_space=pltpu.VMEM)),
        input_output_aliases={i: nsem + i for i in range(2 * n)},
        compiler_params=pltpu.CompilerParams(has_side_effects=DATAFLOW),
    )(*[pltpu.with_memory_space_constraint(a, pltpu.HBM) for a in list(arrs) + list(lands)])
    return dict(kinds=kinds, sems=outs[:nsem], arrs=outs[nsem:nsem + n], lands=outs[nsem + n:nsem + 2 * n],
                token=outs[-1])


def _exchange_wait(name, st, after):
    kinds = st["kinds"]
    n = len(kinds)
    nsem = len(st["sems"])

    def body(*refs):
        a_refs, l_refs = refs[:n], refs[n:2 * n]
        pairs = _sem_pairs(kinds, refs[2 * n:2 * n + nsem])
        for k in range(n):
            for _, arrival in _copies(kinds[k], a_refs[k], l_refs[k], pairs[k]):
                arrival.wait_send()
                arrival.wait_recv()
        refs[-1][...] = jnp.zeros_like(refs[-1])

    ins = list(st["arrs"]) + list(st["lands"])
    outs = _split_call(
        body, name=name,
        out_shape=[pltpu.HBM(a.shape, a.dtype) for a in ins] + [jax.ShapeDtypeStruct((8, LANE), F32)],
        in_specs=[HBM_SPEC] * (2 * n) + [SEM_SPEC] * nsem + [pl.BlockSpec(memory_space=pl.ANY)],
        out_specs=[HBM_SPEC] * (2 * n) + [pl.BlockSpec(memory_space=pltpu.VMEM)],
        input_output_aliases={i: i for i in range(2 * n)},
        compiler_params=pltpu.CompilerParams(has_side_effects=DATAFLOW),
    )(*ins, *st["sems"], after)
    return outs[:n], outs[n:2 * n], outs[-1]


def _landing(own, slot, nslot):
    return lax.dynamic_update_slice(lax.empty((nslot,) + own.shape, own.dtype), own[None], (slot,) + (0,) * own.ndim)


def _swap_sibling(parts, name):
    n = len(parts)

    def body(*refs):
        ins, outs = refs[:n], refs[n:2 * n]
        send_sems, recv_sems = refs[2 * n:]
        sib = (lax.axis_index("x"), lax.axis_index("y"), 1 - lax.axis_index("c"))
        cps = [pltpu.make_async_remote_copy(src_ref=ins[k], dst_ref=outs[k], send_sem=send_sems.at[k],
                                            recv_sem=recv_sems.at[k], device_id=sib, device_id_type=MESH)
               for k in range(n)]
        for cp in cps:
            cp.start()
        for cp in cps:
            cp.wait_recv()
        for cp in cps:
            cp.wait_send()

    return _comm_call(
        body, name=name,
        in_specs=_any_spec(n), out_specs=_any_spec(n),
        out_shape=[jax.ShapeDtypeStruct(p.shape, p.dtype) for p in parts],
        scratch_shapes=[pltpu.SemaphoreType.DMA((n,)), pltpu.SemaphoreType.DMA((n,))],
    )(*parts)


def _col_shards(a, n=N_SHARD):
    r, c = a.shape
    return a.reshape(r, n, c // n).transpose(1, 0, 2)


def _from_col_shards(a):
    n, r, cs = a.shape
    return a.transpose(1, 0, 2).reshape(r, n * cs)


def kernel(x, meta_tokens, ffn1_w_gate, ffn1_w_up, ffn1_w_down, ln1_g, ln1_b, w_in, w_gate_up, b_gate, w_pool, pool_scale, gla_norm_g, w_out, ln2_g, ln2_b, ffn2_w_gate, ffn2_w_up, ffn2_w_down, ln3_g, ln3_b, loss_target, m_meta_tokens, m_ffn1_w_gate, m_ffn1_w_up, m_ffn1_w_down, m_ln1_g, m_ln1_b, m_w_in, m_w_gate_up, m_b_gate, m_w_pool, m_pool_scale, m_gla_norm_g, m_w_out, m_ln2_g, m_ln2_b, m_ffn2_w_gate, m_ffn2_w_up, m_ffn2_w_down, m_ln3_g, m_ln3_b, v_meta_tokens, v_ffn1_w_gate, v_ffn1_w_up, v_ffn1_w_down, v_ln1_g, v_ln1_b, v_w_in, v_w_gate_up, v_b_gate, v_w_pool, v_pool_scale, v_gla_norm_g, v_w_out, v_ln2_g, v_ln2_b, v_ffn2_w_gate, v_ffn2_w_up, v_ffn2_w_down, v_ln3_g, v_ln3_b):
    w = dict(meta_tokens=meta_tokens, ffn1_w_gate=ffn1_w_gate, ffn1_w_up=ffn1_w_up, ffn1_w_down=ffn1_w_down,
             ln1_g=ln1_g, ln1_b=ln1_b, w_in=w_in, w_gate_up=w_gate_up, b_gate=b_gate, w_pool=w_pool,
             pool_scale=pool_scale, gla_norm_g=gla_norm_g, w_out=w_out, ln2_g=ln2_g, ln2_b=ln2_b,
             ffn2_w_gate=ffn2_w_gate, ffn2_w_up=ffn2_w_up, ffn2_w_down=ffn2_w_down, ln3_g=ln3_g, ln3_b=ln3_b)
    mom1 = dict(meta_tokens=m_meta_tokens, ffn1_w_gate=m_ffn1_w_gate, ffn1_w_up=m_ffn1_w_up,
                ffn1_w_down=m_ffn1_w_down, ln1_g=m_ln1_g, ln1_b=m_ln1_b, w_in=m_w_in, w_gate_up=m_w_gate_up,
                b_gate=m_b_gate, w_pool=m_w_pool, pool_scale=m_pool_scale, gla_norm_g=m_gla_norm_g, w_out=m_w_out,
                ln2_g=m_ln2_g, ln2_b=m_ln2_b, ffn2_w_gate=m_ffn2_w_gate, ffn2_w_up=m_ffn2_w_up,
                ffn2_w_down=m_ffn2_w_down, ln3_g=m_ln3_g, ln3_b=m_ln3_b)
    mom2 = dict(meta_tokens=v_meta_tokens, ffn1_w_gate=v_ffn1_w_gate, ffn1_w_up=v_ffn1_w_up,
                ffn1_w_down=v_ffn1_w_down, ln1_g=v_ln1_g, ln1_b=v_ln1_b, w_in=v_w_in, w_gate_up=v_w_gate_up,
                b_gate=v_b_gate, w_pool=v_w_pool, pool_scale=v_pool_scale, gla_norm_g=v_gla_norm_g, w_out=v_w_out,
                ln2_g=v_ln2_g, ln2_b=v_ln2_b, ffn2_w_gate=v_ffn2_w_gate, ffn2_w_up=v_ffn2_w_up,
                ffn2_w_down=v_ffn2_w_down, ln3_g=v_ln3_g, ln3_b=v_ln3_b)

    xs = x[0]
    s_len, d = xs.shape
    nl = ln1_g.shape[0]
    alpha = (2.0 * nl) ** 0.25
    t_real = N_META + s_len
    t_pad = -(-t_real // LANE) * LANE
    pw = pool_scale.shape[1]
    kw = b_gate.shape[1]
    vw = gla_norm_g.shape[1]
    rank = w_gate_up.shape[1]
    widths = (pw, kw, kw, vw, vw)
    n_main = sum(widths)
    dff_s = ffn1_w_gate.shape[2]
    dff_c = N_SHARD * dff_s // FFN_CHUNKS

    me_xy = 2 * lax.axis_index("x") + lax.axis_index("y")
    me_all = 2 * me_xy + lax.axis_index("c")
    ffn1_names = ("ffn1_w_gate", "ffn1_w_up", "ffn1_w_down")
    mix_names = ("w_out", "w_gate_up", "w_in")
    ffn2_names = ("ffn2_w_gate", "ffn2_w_up", "ffn2_w_down")

    gate_up = ("ffn1_w_gate", "ffn1_w_up", "ffn2_w_gate", "ffn2_w_up")

    def stored(n, a):
        if n in gate_up:
            return jnp.swapaxes(a, 1, 2)
        return jnp.transpose(a, (2, 0, 1)) if n == "w_in" else a

    def as_given(n, a):
        if n in gate_up:
            return jnp.swapaxes(a, 1, 2)
        if n == "w_in":
            return jnp.transpose(a.reshape(-1, nl, d), (1, 2, 0))
        return a.reshape(w[n].shape)

    stages = [[("meta_tokens", None)], [(n, 0) for n in ffn1_names], [(n, 0) for n in mix_names + ffn2_names]]
    stages += [[(n, l) for n in BIG] for l in range(1, nl)]
    gathers, wa = {}, {}

    halved = ffn1_names + ffn2_names + ("w_out",)

    def start_gather(si, dep=None):
        own = []
        for n, l in stages[si]:
            a = meta_tokens if l is None else (stored(n, w[n])[:, l] if n == "w_in" else stored(n, w[n])[l])
            a = a if dep is None else a + dep
            own.append(a if l is None else a.astype(BF16))
        gathers[si] = _exchange_start(f"gather_start_{si}", ["gather_half" if n in halved else "gather"
                                                             for n, _ in stages[si]], own,
                                      [_landing(a, me_xy, N_SHARD) for a in own])
        return gathers[si]["token"]

    def arrive(si, after):
        _, lands, token = _exchange_wait(f"gather_wait_{si}", gathers[si], after)
        if any(kd == "gather_half" for kd in gathers[si]["kinds"]):
            lands = _share_halves(f"gather_share_{si}", gathers[si]["kinds"], lands)
        for item, a in zip(stages[si], lands):
            wa[item] = a.reshape(FFN_CHUNKS, -1, d) if item[0] in ffn1_names + ffn2_names else a
        return token

    def mixer_weights(l):
        wi = wa["w_in", l].reshape(-1, d)
        return dict(w_main=wi[:n_main], w_lr=jnp.pad(wi[n_main:], ((0, LANE - rank), (0, 0))),
                    wgu=jnp.pad(_from_col_shards(wa["w_gate_up", l]), ((0, LANE - rank), (0, 0))),
                    wout=wa["w_out", l].reshape(-1, d))

    wp16 = w_pool.astype(BF16)
    ones = jnp.ones((1, d), F32)
    zeros = jnp.zeros((1, d), F32)
    target = jnp.concatenate([jnp.zeros((N_META, d), F32), loss_target[0], jnp.zeros((t_pad - t_real, d), F32)], axis=0)

    started = start_gather(0)
    for si in range(1, len(stages)):
        started = start_gather(si, started[0:1, 0:1])
    arrive(0, started)
    meta_full = _from_col_shards(wa["meta_tokens", None])
    h0 = jnp.concatenate([meta_full, xs, jnp.zeros((t_pad - t_real, d), F32)], axis=0)
    arrive(1, h0[:8, :LANE] + target[:8, :LANE])

    saved, mw = [], []
    cur, cur_g, cur_b = h0, ones, zeros
    for l in range(nl):
        s = {}
        xh1, rs1, hb0, g1, u1 = _ffn_fwd(cur, cur_g, cur_b, wa["ffn1_w_gate", l], wa["ffn1_w_up", l],
                                         wa["ffn1_w_down", l], alpha, f"ffn1_fwd_{l}")
        if l == 0:
            arrive(2, xh1)
        mw.append(mixer_weights(l))
        up, q, k, v, r, zg, la, hb1 = _inproj_fwd(xh1, ln1_g[l:l + 1], ln1_b[l:l + 1], mw[l]["w_main"], mw[l]["w_lr"],
                                                  mw[l]["wgu"], b_gate[l:l + 1], widths, f"inproj_fwd_{l}")
        yp, pb = _pool_fwd(up, wp16[l], pool_scale[l:l + 1], f"pool_fwd_{l}")
        o, yg, sall = _gla_fwd(q, k, v, la, r, gla_norm_g[l:l + 1], f"gla_fwd_{l}")
        xh2, rs2 = _outproj_fwd(yp, yg, mw[l]["wout"], xh1, ln1_g[l:l + 1], ln1_b[l:l + 1], alpha, f"outproj_fwd_{l}")
        if l + 1 < nl:
            arrive(l + 3, xh2)
        xh3, rs3, hb2, g2, u2 = _ffn_fwd(xh2, ln2_g[l:l + 1], ln2_b[l:l + 1], wa["ffn2_w_gate", l], wa["ffn2_w_up", l],
                                         wa["ffn2_w_down", l], alpha, f"ffn2_fwd_{l}")
        s.update(xh1=xh1, rs1=rs1, hb0=hb0, g1=g1, u1=u1, q=q, k=k, v=v, r=r, zg=zg, la=la, hb1=hb1, yp=yp, pb=pb,
                 o=o, yg=yg, sall=sall, xh2=xh2, rs2=rs2, xh3=xh3, rs3=rs3, hb2=hb2, g2=g2, u2=u2)
        saved.append(s)
        cur, cur_g, cur_b = xh3, ln3_g[l:l + 1], ln3_b[l:l + 1]

    dh, loss_acc = _loss_head(cur, cur_g, cur_b, target, s_len, "loss_head")
    loss = lax.psum(loss_acc[0, 0], ("x", "y", "c"))

    small_grads = {n: [None] * nl for n in SMALL}
    scatters = []

    def depart(name, items, grads, kinds=None):
        lands = [_landing(g, me_all, N_DEV) if kd == "bcast" else lax.empty(g.shape, g.dtype)
                 for g, kd in zip(grads, kinds or ["scatter"] * len(grads))]
        st = _exchange_start(name, kinds or ["scatter"] * len(grads), grads, lands)
        scatters.append((name, items, st))
        return st["token"]

    def pack(parts):
        flat = jnp.concatenate([parts[n].reshape(-1) for n in SMALL])
        return flat.reshape(-1, LANE)

    def ffn_wgrad(n, l, hb, dgb, dub, act, dfb, after=None):
        if n.endswith("down"):
            dw = _wgrad(act, dfb, dff_c, d, f"{n}_grad_{l}", after)
        else:
            dw = _wgrad(dgb if n.endswith("gate") else dub, hb, dff_c, d, f"{n}_grad_{l}", after)
        return dw.reshape(N_SHARD, dff_s, d)

    late = []
    for l in reversed(range(nl)):
        s = saved[l]
        dh, dfb, dgb, dub, act, dgam, dbet = _ffn_bwd(dh, s["xh3"], s["rs3"], ln3_g[l:l + 1], s["g2"], s["u2"],
                                                      wa["ffn2_w_gate", l], wa["ffn2_w_up", l], wa["ffn2_w_down", l],
                                                      alpha, f"ffn2_bwd_{l}")
        small_grads["ln3_g"][l], small_grads["ln3_b"][l] = dgam, dbet
        gone = depart(f"scatter_start_ffn2_{l}", [(n, l) for n in ffn2_names],
                      [ffn_wgrad(n, l, s["hb2"], dgb, dub, act, dfb) for n in ffn2_names])

        dyb, dyp, dyg, dres, dgam, dbet = _outproj_bwd(dh, s["xh2"], s["rs2"], ln2_g[l:l + 1] + gone[0:1, 0:1],
                                                       mw[l]["wout"], pw, alpha, f"outproj_bwd_{l}")
        small_grads["ln2_g"][l], small_grads["ln2_b"][l] = dgam, dbet
        dwo = jnp.concatenate([_wgrad(s["yp"], dyb, pw, d, f"dwout_pool_{l}"),
                               _wgrad(s["yg"], dyb, vw, d, f"dwout_gla_{l}")], axis=0)
        dq, dk, dv, dr, dzg, dwgu, dbg, dgn = _gla_bwd(dyg, s["o"], s["r"], gla_norm_g[l:l + 1], s["q"], s["k"],
                                                       s["v"], s["la"], s["zg"], s["sall"], mw[l]["wgu"],
                                                       f"gla_bwd_{l}")
        dup, dwp, dsc = _pool_bwd(dyp, s["pb"], wp16[l], pool_scale[l:l + 1], f"pool_bwd_{l}")
        small_grads["b_gate"][l], small_grads["gla_norm_g"][l] = dbg, dgn
        small_grads["w_pool"][l], small_grads["pool_scale"][l] = dwp, dsc
        dh, dz = _inproj_bwd(dres, [dup, dq, dk, dv, dr], dzg, mw[l]["w_main"], mw[l]["w_lr"], f"inproj_bwd_{l}")
        dwi = jnp.concatenate([_wgrad(dz, s["hb1"], 4 * LANE, d, f"dwin_main_{l}"),
                               _wgrad(dzg, s["hb1"], LANE, d, f"dwin_lr_{l}")[:rank]], axis=0)
        gone = depart(f"scatter_start_mix_{l}", [(n, l) for n in mix_names],
                      [dwo.reshape(N_SHARD, -1, d), _col_shards(dwgu[:rank]), dwi.reshape(N_SHARD, -1, d)])

        dh, dfb, dgb, dub, act, dgam, dbet = _ffn_bwd(dh, s["xh1"], s["rs1"], ln1_g[l:l + 1] + gone[0:1, 0:1],
                                                      s["g1"], s["u1"], wa["ffn1_w_gate", l], wa["ffn1_w_up", l],
                                                      wa["ffn1_w_down", l], alpha, f"ffn1_bwd_{l}")
        small_grads["ln1_g"][l], small_grads["ln1_b"][l] = dgam, dbet
        if l:
            gone = depart(f"scatter_start_ffn1_{l}", [(n, l) for n in ffn1_names],
                          [ffn_wgrad(n, l, s["hb0"], dgb, dub, act, dfb) for n in ffn1_names])
            ln3_g = ln3_g.at[l - 1:l].add(gone[0:1, 0:1])
            continue
        grad_x = dh[N_META:t_real][None]
        small_vec = pack({n: jnp.stack(small_grads[n]) for n in SMALL})
        gone = depart("scatter_start_rest", [("meta_tokens", 0), ("small", 0)],
                      [_col_shards(dh[:N_META]), small_vec], ["scatter", "bcast"])
        for n in ffn1_names:
            g = ffn_wgrad(n, l, s["hb0"], dgb, dub, act, dfb, after=gone)
            gone = depart(f"scatter_start_{n}", [(n, l)], [g])
            late.append(scatters.pop())

    sent, recv, results, firsts = {}, {}, {}, []
    my_slot = me_xy.reshape(1).astype(jnp.int32)

    def collect(group, after):
        for name, items, st in group:
            arrs, lands, _ = _exchange_wait(name.replace("start", "wait"), st, after)
            for item, a, b in zip(items, arrs, lands):
                sent[item], recv[item] = a, b

    def reduce_and_update(names, tag):
        partial = []
        for n in names:
            layers = [(n, l) for l in range(1 if n == "meta_tokens" else nl)]
            partial.append(_sum_slots(my_slot, [sent[it] for it in layers], [recv[it] for it in layers],
                                      f"sum_{n}", n == "w_in"))
        for n, mine, theirs in zip(names, partial, _swap_sibling(partial, f"swap_sibling_{tag}")):
            fit = lambda a: stored(n, a).reshape(mine.shape)
            outs = _adamw(fit(w[n]), [mine, theirs], fit(mom1[n]), fit(mom2[n]), f"adamw_{n}")
            results[n] = [as_given(n, o) for o in outs]
            firsts.append(outs[1][0, 0, 0])

    collect(scatters, gone)
    early = [n for n in ("meta_tokens",) + BIG if n not in ffn1_names]
    reduce_and_update(early, "early")
    small_terms = [recv["small", 0][i][None] for i in range(N_DEV)]
    souts = _adamw(pack(w)[None], small_terms, pack(mom1)[None], pack(mom2)[None], "adamw_small")
    off = 0
    for n in SMALL:
        size = w[n].size
        results[n] = [o.reshape(-1)[off:off + size].reshape(w[n].shape) for o in souts]
        off += size
    collect(late, souts[0][0, :8] + functools.reduce(jnp.add, firsts))
    reduce_and_update(ffn1_names, "late")

    out = [loss, grad_x]
    for part in range(4):
        out += [results[n][part] for n in WEIGHTS]
    return tuple(out)
```

```python
import functools

import jax
import jax.numpy as jnp
from jax import lax
from jax.experimental import pallas as pl
from jax.experimental.pallas import tpu as pltpu

F32 = jnp.float32
BF16 = jnp.bfloat16
MESH = pl.DeviceIdType.MESH

N_META = 16
POOL_WINDOWS = (2, 4, 8, 16)
POOL_HALO = 16
N_HEADS = 4
GLA_GATE_TEMP = 16.0
CHUNK = 128
CHUNK_UNROLL = 5
LN_EPS = 1e-5
RMS_EPS = 1e-6
ADAM_LR = 0.001
ADAM_B1 = 0.9
ADAM_B2 = 0.999
ADAM_EPS = 1e-08
ADAM_WD = 0.01
ADAM_STEP = 10
LANE = 128
BF16_ROWS = 16
ROW_TILE = 640
FFN_ROW_TILE = 640
FFN_CHUNKS = 4
FFN_SPLIT = 2
ROW_GROUPS = 2
WGRAD_K_MAX = 2176
N_SHARD = 4
N_DEV = 8

BIG = ("ffn1_w_gate", "ffn1_w_up", "ffn1_w_down", "w_in", "w_gate_up", "w_out",
       "ffn2_w_gate", "ffn2_w_up", "ffn2_w_down")
SMALL = ("ln1_g", "ln1_b", "b_gate", "w_pool", "pool_scale", "gla_norm_g", "ln2_g", "ln2_b", "ln3_g", "ln3_b")
WEIGHTS = ("meta_tokens", "ffn1_w_gate", "ffn1_w_up", "ffn1_w_down", "ln1_g", "ln1_b", "w_in", "w_gate_up",
           "b_gate", "w_pool", "pool_scale", "gla_norm_g", "w_out", "ln2_g", "ln2_b", "ffn2_w_gate",
           "ffn2_w_up", "ffn2_w_down", "ln3_g", "ln3_b")


def _tc_call(body, **kw):
    return pl.pallas_call(body, **kw)


def _comm_call(body, **kw):
    return pl.pallas_call(body, **kw)


def _seq(n):
    return pltpu.CompilerParams(dimension_semantics=("arbitrary",) * n)


def _mm(a, b):
    return jnp.dot(a.astype(BF16), b.astype(BF16), preferred_element_type=F32)


def _mm_nt(a, b):
    return lax.dot_general(a.astype(BF16), b.astype(BF16), (((1,), (1,)), ((), ())), preferred_element_type=F32)


def _mm_tn(a, b):
    return lax.dot_general(a.astype(BF16), b.astype(BF16), (((0,), (0,)), ((), ())), preferred_element_type=F32)


def _mm_01(a, b):
    hi = b.astype(BF16)
    lo = (b - hi.astype(F32)).astype(BF16)
    a = a.astype(BF16)
    return jnp.dot(a, hi, preferred_element_type=F32) + jnp.dot(a, lo, preferred_element_type=F32)


def _row_tile(t, most=None):
    tm = min(most or ROW_TILE, t)
    while t % tm:
        tm -= LANE
    return tm


def _silu_parts(g):
    sg = jax.nn.sigmoid(g)
    return sg, g * sg


def _ln_stats(r):
    mu = jnp.mean(r, axis=-1, keepdims=True)
    rc = r - mu
    var = jnp.mean(rc * rc, axis=-1, keepdims=True)
    rs = lax.rsqrt(var + LN_EPS)
    return rc * rs, rs


def _ln_bwd(dy, xh, rs, gam):
    dyg = dy * gam
    c1 = jnp.mean(dyg, axis=-1, keepdims=True)
    c2 = jnp.mean(dyg * xh, axis=-1, keepdims=True)
    return rs * (dyg - c1 - xh * c2)


def _ffn_fwd(xin, gam_in, bet_in, wg, wu, wd, alpha, name):
    t, d = xin.shape
    nj, tf, _ = wg.shape
    tm = _row_tile(t, FFN_ROW_TILE)

    def body(x_ref, gi_ref, bi_ref, wg_ref, wu_ref, wd_ref, xhat_ref, rstd_ref, hb_ref, go_ref, uo_ref, acc, hbs):
        j = pl.program_id(1)

        @pl.when(j == 0)
        def _():
            h = x_ref[...] * gi_ref[...] + bi_ref[...]
            hb = h.astype(BF16)
            hbs[...] = hb
            hb_ref[...] = hb
            acc[...] = (2.0 * alpha) * h

        hb = hbs[...]
        g = _mm_nt(hb, wg_ref[...])
        u = _mm_nt(hb, wu_ref[...])
        _, sl = _silu_parts(g)
        go_ref[...] = g.astype(BF16)
        uo_ref[...] = u.astype(BF16)
        acc[...] += jnp.dot((sl * u).astype(BF16), wd_ref[...], preferred_element_type=F32)

        @pl.when(j == nj - 1)
        def _():
            xhat, rs = _ln_stats(0.5 * acc[...])
            xhat_ref[...] = xhat
            rstd_ref[...] = rs

    row = lambda i, j: (i, 0)
    vec = pl.BlockSpec((1, d), lambda i, j: (0, 0))
    return _tc_call(
        body, name=name, grid=(t // tm, nj),
        in_specs=[pl.BlockSpec((tm, d), row), vec, vec] + [pl.BlockSpec((None, tf, d), lambda i, j: (j, 0, 0))] * 3,
        out_specs=[pl.BlockSpec((tm, d), row), pl.BlockSpec((tm, 1), row), pl.BlockSpec((tm, d), row),
                   pl.BlockSpec((None, tm, tf), lambda i, j: (j, i, 0)),
                   pl.BlockSpec((None, tm, tf), lambda i, j: (j, i, 0))],
        out_shape=[jax.ShapeDtypeStruct((t, d), F32), jax.ShapeDtypeStruct((t, 1), F32),
                   jax.ShapeDtypeStruct((t, d), BF16), jax.ShapeDtypeStruct((nj, t, tf), BF16),
                   jax.ShapeDtypeStruct((nj, t, tf), BF16)],
        scratch_shapes=[pltpu.VMEM((tm, d), F32), pltpu.VMEM((tm, d), BF16)],
        compiler_params=_seq(2),
    )(xin, gam_in, bet_in, wg, wu, wd)


def _ffn_bwd(dh, xhat, rstd, ln_g, gb, ub, wg, wu, wd, alpha, name):
    t, d = dh.shape
    nj, tf, _ = wg.shape
    tm = _row_tile(t, FFN_ROW_TILE)
    nt = t // tm
    share = tm // nj

    def body(dh_ref, xh_ref, rs_ref, g_ref, gb_ref, ub_ref, wg_ref, wu_ref, wd_ref,
             dhin_ref, df_ref, dg_ref, du_ref, act_ref, dgam_ref, dbet_ref, df_s, dres_next, df_next):
        i = pl.program_id(0)
        j = pl.program_id(1)

        @pl.when((i == 0) & (j == 0))
        def _():
            dgam_ref[...] = jnp.zeros_like(dgam_ref)
            dbet_ref[...] = jnp.zeros_like(dbet_ref)

        def look_ahead():
            rows = pl.ds(pl.multiple_of(j * share, BF16_ROWS), share)
            dy = dh_ref[rows, :]
            xh = xh_ref[rows, :]
            dr = _ln_bwd(dy, xh, rs_ref[rows, :], g_ref[...])
            dres_next[rows, :] = alpha * dr
            df_next[rows, :] = (0.5 * dr).astype(BF16)
            live = jnp.where(i < nt, 1.0, 0.0)
            dgam_ref[...] += live * jnp.sum(dy * xh, axis=0, keepdims=True)
            dbet_ref[...] += live * jnp.sum(dy, axis=0, keepdims=True)

        @pl.when(i == 0)
        def _():
            look_ahead()

        @pl.when(i > 0)
        def _():
            @pl.when(j == 0)
            def _():
                dhin_ref[...] = dres_next[...]
                dfb = df_next[...]
                df_s[...] = dfb
                df_ref[...] = dfb

            look_ahead()
            for part in range(FFN_SPLIT):
                rows = pl.ds(part * (tm // FFN_SPLIT), tm // FFN_SPLIT)
                dact = _mm_nt(df_s[rows, :], wd_ref[...])
                g = gb_ref[rows, :].astype(F32)
                u = ub_ref[rows, :].astype(F32)
                sg, sl = _silu_parts(g)
                dg = (dact * u * (sg * (1.0 + g * (1.0 - sg)))).astype(BF16)
                du = (dact * sl).astype(BF16)
                dg_ref[rows, :] = dg
                du_ref[rows, :] = du
                act_ref[rows, :] = (sl * u).astype(BF16)
                dhin_ref[rows, :] += _mm(dg, wg_ref[...]) + _mm(du, wu_ref[...])

    ahead = lambda i, j: (jnp.minimum(i, nt - 1), 0)
    row = lambda i, j: (jnp.maximum(i - 1, 0), 0)
    chunk = lambda i, j: (jnp.where(i > 0, j, 0), 0, 0)
    col = pl.BlockSpec((None, tm, tf), lambda i, j: (jnp.where(i > 0, j, 0), jnp.maximum(i - 1, 0), 0))
    vec = pl.BlockSpec((1, d), lambda i, j: (0, 0))
    ff = jax.ShapeDtypeStruct((nj, t, tf), BF16)
    return _tc_call(
        body, name=name, grid=(nt + 1, nj),
        in_specs=[pl.BlockSpec((tm, d), ahead), pl.BlockSpec((tm, d), ahead), pl.BlockSpec((tm, 1), ahead), vec,
                  col, col] + [pl.BlockSpec((None, tf, d), chunk)] * 3,
        out_specs=[pl.BlockSpec((tm, d), row), pl.BlockSpec((tm, d), row), col, col, col, vec, vec],
        out_shape=[jax.ShapeDtypeStruct((t, d), F32), jax.ShapeDtypeStruct((t, d), BF16), ff, ff, ff,
                   jax.ShapeDtypeStruct((1, d), F32), jax.ShapeDtypeStruct((1, d), F32)],
        scratch_shapes=[pltpu.VMEM((tm, d), BF16), pltpu.VMEM((tm, d), F32), pltpu.VMEM((tm, d), BF16)],
        compiler_params=_seq(2),
    )(dh, xhat, rstd, ln_g, gb, ub, wg, wu, wd)


def _wgrad(a, b, tmm, tn, name, after=None):
    t = a.shape[-2]
    m = a.shape[-1] * (a.shape[0] if a.ndim == 3 else 1)
    n = b.shape[-1] * (b.shape[0] if b.ndim == 3 else 1)
    tk = max(k for k in range(BF16_ROWS, WGRAD_K_MAX + 1, BF16_ROWS) if t % k == 0)
    nk = t // tk
    extra = [] if after is None else [after]

    def body(a_ref, b_ref, *rest):
        o_ref, acc = rest[len(extra):]
        k = pl.program_id(2)

        @pl.when(k == 0)
        def _():
            acc[...] = jnp.zeros_like(acc)

        acc[...] += _mm_tn(a_ref[...], b_ref[...])

        @pl.when(k == nk - 1)
        def _():
            o_ref[...] = acc[...].astype(o_ref.dtype)

    a_spec = (pl.BlockSpec((None, tk, tmm), lambda i, j, k: (i, k, 0)) if a.ndim == 3
              else pl.BlockSpec((tk, tmm), lambda i, j, k: (k, i)))
    return _tc_call(
        body, name=name, grid=(m // tmm, n // tn, nk),
        in_specs=[a_spec, pl.BlockSpec((None, tk, tn), lambda i, j, k: (j, k, 0)) if b.ndim == 3
                  else pl.BlockSpec((tk, tn), lambda i, j, k: (k, j))] + [pl.BlockSpec(memory_space=pl.ANY)] * len(extra),
        out_specs=pl.BlockSpec((tmm, tn), lambda i, j, k: (i, j)),
        out_shape=jax.ShapeDtypeStruct((m, n), BF16),
        scratch_shapes=[pltpu.VMEM((tmm, tn), F32)],
        compiler_params=_seq(3),
    )(a, b, *extra)


def _inproj_fwd(xhat, gam, bet, w_main, w_lr, wgu, b_gate, widths, name):
    t, d = xhat.shape
    tm = _row_tile(t)
    kw = wgu.shape[1]
    offs = [0]
    for w in widths:
        offs.append(offs[-1] + w)

    def body(x_ref, g_ref, b_ref, wm_ref, wl_ref, wgu_ref, bg_ref, *outs):
        piece_refs, (zg_ref, la_ref, hb_ref) = outs[:len(widths)], outs[len(widths):]
        hb = (x_ref[...] * g_ref[...] + b_ref[...]).astype(BF16)
        hb_ref[...] = hb
        for p, ref in enumerate(piece_refs):
            ref[...] = _mm_nt(hb, wm_ref[offs[p]:offs[p + 1], :])
        zg = _mm_nt(hb, wl_ref[...])
        zg_ref[...] = zg
        logit = _mm(zg, wgu_ref[...]) + bg_ref[...]
        la_ref[...] = (jnp.minimum(logit, 0.0) - jnp.log(1.0 + jnp.exp(-jnp.abs(logit)))) * (1.0 / GLA_GATE_TEMP)

    row = lambda i: (i, 0)
    full = lambda a: pl.BlockSpec(a.shape, lambda i: (0,) * a.ndim)
    out_w = list(widths) + [LANE, kw]
    return _tc_call(
        body, name=name, grid=(t // tm,),
        in_specs=[pl.BlockSpec((tm, d), row), full(gam), full(bet), full(w_main), full(w_lr), full(wgu), full(b_gate)],
        out_specs=[pl.BlockSpec((tm, w), row) for w in out_w] + [pl.BlockSpec((tm, d), row)],
        out_shape=[jax.ShapeDtypeStruct((t, w), F32) for w in out_w] + [jax.ShapeDtypeStruct((t, d), BF16)],
        compiler_params=_seq(1),
    )(xhat, gam, bet, w_main, w_lr, wgu, b_gate)


def _inproj_bwd(dh_part, pieces, dzg, w_main, w_lr, name):
    t, d = dh_part.shape
    tm = _row_tile(t)
    widths = [p.shape[1] for p in pieces]
    offs = [0]
    for w in widths:
        offs.append(offs[-1] + w)

    def body(*refs):
        dhp_ref = refs[0]
        p_refs = refs[1:1 + len(widths)]
        dzg_ref, wm_ref, wl_ref, dh_ref, dz_ref = refs[1 + len(widths):]
        acc = dhp_ref[...] + _mm(dzg_ref[...], wl_ref[...])
        for p, ref in enumerate(p_refs):
            v = ref[...]
            dz_ref[:, offs[p]:offs[p + 1]] = v
            acc += _mm(v, wm_ref[offs[p]:offs[p + 1], :])
        dh_ref[...] = acc

    row = lambda i: (i, 0)
    full = lambda a: pl.BlockSpec(a.shape, lambda i: (0,) * a.ndim)
    return _tc_call(
        body, name=name, grid=(t // tm,),
        in_specs=[pl.BlockSpec((tm, d), row)] + [pl.BlockSpec((tm, w), row) for w in widths]
                 + [pl.BlockSpec((tm, LANE), row), full(w_main), full(w_lr)],
        out_specs=[pl.BlockSpec((tm, d), row), pl.BlockSpec((tm, offs[-1]), row)],
        out_shape=[jax.ShapeDtypeStruct((t, d), F32), jax.ShapeDtypeStruct((t, offs[-1]), BF16)],
        compiler_params=_seq(1),
    )(dh_part, *pieces, dzg, w_main, w_lr)


def _pool_cnt(tile, tm, w):
    t = tile * tm + lax.broadcasted_iota(jnp.int32, (tm, 1), 0)
    return jnp.minimum(t + 1, w).astype(F32)


def _pool_fwd(u, wp, scale, name):
    t, pw = u.shape
    tm = _row_tile(t)
    gd = wp.shape[1]

    def body(u_ref, wp_ref, sc_ref, y_ref, p_ref, ext):
        i = pl.program_id(0)

        @pl.when(i == 0)
        def _():
            ext[0:POOL_HALO, :] = jnp.zeros((POOL_HALO, pw), F32)

        ext[POOL_HALO:POOL_HALO + tm, :] = u_ref[...]
        for gi, w in enumerate(POOL_WINDOWS):
            cols = slice(gi * gd, (gi + 1) * gd)
            s = ext[pl.ds(POOL_HALO, tm), cols]
            tot = s
            for back in range(1, w):
                tot = tot + ext[pl.ds(POOL_HALO - back, tm), cols]
            p = (tot / _pool_cnt(i, tm, w) - s).astype(BF16)
            p_ref[:, cols] = p
            y_ref[:, cols] = (jnp.dot(p, wp_ref[gi], preferred_element_type=F32) * sc_ref[:, cols]).astype(BF16)
        ext[0:POOL_HALO, :] = ext[tm:tm + POOL_HALO, :]

    row = lambda i: (i, 0)
    return _tc_call(
        body, name=name, grid=(t // tm,),
        in_specs=[pl.BlockSpec((tm, pw), row), pl.BlockSpec(wp.shape, lambda i: (0, 0, 0)),
                  pl.BlockSpec((1, pw), lambda i: (0, 0))],
        out_specs=[pl.BlockSpec((tm, pw), row), pl.BlockSpec((tm, pw), row)],
        out_shape=[jax.ShapeDtypeStruct((t, pw), BF16), jax.ShapeDtypeStruct((t, pw), BF16)],
        scratch_shapes=[pltpu.VMEM((tm + POOL_HALO, pw), F32)],
        compiler_params=_seq(1),
    )(u, wp, scale)


def _pool_bwd(dy, pb, wp, scale, name):
    t, pw = dy.shape
    tm = _row_tile(t)
    nt = t // tm
    gd = wp.shape[1]

    def body(dy_ref, p_ref, wp_ref, sc_ref, du_ref, dwp_ref, dsc_ref, ext):
        i = pl.program_id(0)
        tile = nt - 1 - i

        @pl.when(i == 0)
        def _():
            ext[tm:tm + POOL_HALO, :] = jnp.zeros((POOL_HALO, pw), F32)
            dwp_ref[...] = jnp.zeros_like(dwp_ref)
            dsc_ref[...] = jnp.zeros_like(dsc_ref)

        dps = []
        for gi, w in enumerate(POOL_WINDOWS):
            cols = slice(gi * gd, (gi + 1) * gd)
            dyv = dy_ref[:, cols]
            p = p_ref[:, cols]
            dpre = (dyv * sc_ref[:, cols]).astype(BF16)
            dsc_ref[:, cols] += jnp.sum(dyv * jnp.dot(p, wp_ref[gi], preferred_element_type=F32), axis=0, keepdims=True)
            dwp_ref[gi] += _mm_tn(p, dpre)
            dp = _mm_nt(dpre, wp_ref[gi])
            dps.append(dp)
            ext[0:tm, cols] = dp / _pool_cnt(tile, tm, w)
        for gi, w in enumerate(POOL_WINDOWS):
            cols = slice(gi * gd, (gi + 1) * gd)
            tot = ext[pl.ds(0, tm), cols]
            for fwd in range(1, w):
                tot = tot + ext[pl.ds(fwd, tm), cols]
            du_ref[:, cols] = (tot - dps[gi]).astype(BF16)
        ext[tm:tm + POOL_HALO, :] = ext[0:POOL_HALO, :]

    row = lambda i: (nt - 1 - i, 0)
    return _tc_call(
        body, name=name, grid=(nt,),
        in_specs=[pl.BlockSpec((tm, pw), row), pl.BlockSpec((tm, pw), row),
                  pl.BlockSpec(wp.shape, lambda i: (0, 0, 0)), pl.BlockSpec((1, pw), lambda i: (0, 0))],
        out_specs=[pl.BlockSpec((tm, pw), row), pl.BlockSpec(wp.shape, lambda i: (0, 0, 0)),
                   pl.BlockSpec((1, pw), lambda i: (0, 0))],
        out_shape=[jax.ShapeDtypeStruct((t, pw), BF16), jax.ShapeDtypeStruct(wp.shape, F32),
                   jax.ShapeDtypeStruct((1, pw), F32)],
        scratch_shapes=[pltpu.VMEM((tm + POOL_HALO, pw), F32)],
        compiler_params=_seq(1),
    )(dy, pb, wp, scale)


def _gla_masks(kw, vw):
    dk, dv = kw // N_HEADS, vw // N_HEADS
    lane_k = lax.broadcasted_iota(jnp.int32, (1, kw), 1)
    lane_v = lax.broadcasted_iota(jnp.int32, (1, vw), 1)
    hk = [((lane_k >= h * dk) & (lane_k < (h + 1) * dk)).astype(F32) for h in range(N_HEADS)]
    hv = [((lane_v >= h * dv) & (lane_v < (h + 1) * dv)).astype(F32) for h in range(N_HEADS)]
    r = lax.broadcasted_iota(jnp.int32, (CHUNK, CHUNK), 0)
    c = lax.broadcasted_iota(jnp.int32, (CHUNK, CHUNK), 1)
    tril = r >= c
    rs = lax.broadcasted_iota(jnp.int32, (N_HEADS * CHUNK, CHUNK), 0) & (CHUNK - 1)
    stril = rs >= lax.broadcasted_iota(jnp.int32, (N_HEADS * CHUNK, CHUNK), 1)
    return hk, hv, tril, stril


def _block_diag(x, hk, dv):
    return jnp.concatenate([x[h * dv:(h + 1) * dv, :] * hk[h] for h in range(N_HEADS)], axis=0)


def _gla_fwd(q, k, v, loga, r, gnorm, name):
    t, kw = q.shape
    vw = v.shape[1]
    dk, dv = kw // N_HEADS, vw // N_HEADS
    tm = _row_tile(t)
    nc = tm // CHUNK
    qscale = dk ** -0.5

    def body(q_ref, k_ref, v_ref, la_ref, r_ref, gn_ref, o_ref, y_ref, sall_ref, st):
        @pl.when(pl.program_id(0) == 0)
        def _():
            st[...] = jnp.zeros_like(st)

        hk, hv, tril, stril = _gla_masks(kw, vw)
        trif = tril.astype(F32)

        def chunk(c, carry):
            rows = pl.ds(pl.multiple_of(c * CHUNK, CHUNK), CHUNK)
            la = la_ref[rows, :]
            b = _mm_01(trif, la)
            bl = jnp.sum(la, axis=0, keepdims=True)
            qb = q_ref[rows, :] * (qscale * jnp.exp(b))
            kk = k_ref[rows, :]
            kb = kk * jnp.exp(-b)
            kl = kk * jnp.exp(bl - b)
            vv = v_ref[rows, :]
            s_t = st[...]
            compact = s_t[0:dv, :]
            for h in range(1, N_HEADS):
                compact = compact + s_t[h * dv:(h + 1) * dv, :]
            sall_ref[c] = compact
            qx = jnp.concatenate([qb.astype(BF16) * hk[h].astype(BF16) for h in range(N_HEADS)], axis=0)
            a = jnp.where(stril, _mm_nt(qx, kb), 0.0).astype(BF16)
            o_inter = _mm_nt(qb, s_t)
            for h in range(N_HEADS):
                vs = slice(h * dv, (h + 1) * dv)
                o_ref[rows, vs] = o_inter[:, vs] + _mm(a[h * CHUNK:(h + 1) * CHUNK, :], vv[:, vs])
            st[...] = s_t * jnp.exp(bl) + _block_diag(_mm_tn(vv, kl), hk, dv)
            return carry

        lax.fori_loop(0, nc, chunk, 0, unroll=CHUNK_UNROLL)
        for h in range(N_HEADS):
            vs = slice(h * dv, (h + 1) * dv)
            oh = o_ref[:, vs]
            on = oh * lax.rsqrt(jnp.mean(oh * oh, axis=-1, keepdims=True) + RMS_EPS)
            _, sl = _silu_parts(r_ref[:, vs])
            y_ref[:, vs] = (on * gn_ref[:, vs] * sl).astype(BF16)

    row = lambda i: (i, 0)
    return _tc_call(
        body, name=name, grid=(t // tm,),
        in_specs=[pl.BlockSpec((tm, kw), row), pl.BlockSpec((tm, kw), row), pl.BlockSpec((tm, vw), row),
                  pl.BlockSpec((tm, kw), row), pl.BlockSpec((tm, vw), row), pl.BlockSpec((1, vw), lambda i: (0, 0))],
        out_specs=[pl.BlockSpec((tm, vw), row), pl.BlockSpec((tm, vw), row),
                   pl.BlockSpec((nc, dv, kw), lambda i: (i, 0, 0))],
        out_shape=[jax.ShapeDtypeStruct((t, vw), F32), jax.ShapeDtypeStruct((t, vw), BF16),
                   jax.ShapeDtypeStruct((t // CHUNK, dv, kw), F32)],
        scratch_shapes=[pltpu.VMEM((vw, kw), F32)],
        compiler_params=_seq(1),
    )(q, k, v, loga, r, gnorm)


def _gla_bwd(dy, o, r, gnorm, q, k, v, loga, zg, sall, wgu, name):
    t, kw = q.shape
    vw = v.shape[1]
    dk, dv = kw // N_HEADS, vw // N_HEADS
    tm = _row_tile(t)
    nt = t // tm
    nc = tm // CHUNK
    qscale = dk ** -0.5

    def body(dy_ref, o_ref, r_ref, gn_ref, q_ref, k_ref, v_ref, la_ref, zg_ref, sall_ref, wgu_ref,
             dq_ref, dk_ref, dv_ref, dr_ref, dzg_ref, dwgu_ref, dbg_ref, dgn_ref, dst, do_s):
        @pl.when(pl.program_id(0) == 0)
        def _():
            dst[...] = jnp.zeros_like(dst)
            dwgu_ref[...] = jnp.zeros_like(dwgu_ref)
            dbg_ref[...] = jnp.zeros_like(dbg_ref)
            dgn_ref[...] = jnp.zeros_like(dgn_ref)

        for h in range(N_HEADS):
            vs = slice(h * dv, (h + 1) * dv)
            oh = o_ref[:, vs]
            rinv = lax.rsqrt(jnp.mean(oh * oh, axis=-1, keepdims=True) + RMS_EPS)
            on = oh * rinv
            rr = r_ref[:, vs]
            sg, sl = _silu_parts(rr)
            dyv = dy_ref[:, vs]
            gn = gn_ref[:, vs]
            dgn_ref[:, vs] += jnp.sum(dyv * on * sl, axis=0, keepdims=True)
            dr_ref[:, vs] = (dyv * on * gn * (sg * (1.0 + rr * (1.0 - sg)))).astype(BF16)
            don = dyv * gn * sl
            do_s[:, vs] = rinv * (don - on * jnp.mean(don * on, axis=-1, keepdims=True))

        hk, hv, tril, stril = _gla_masks(kw, vw)
        trif = tril.astype(F32)
        triuf = (lax.broadcasted_iota(jnp.int32, (CHUNK, CHUNK), 0)
                 <= lax.broadcasted_iota(jnp.int32, (CHUNK, CHUNK), 1)).astype(F32)
        last_row = lax.broadcasted_iota(jnp.int32, (CHUNK, 1), 0) == CHUNK - 1

        def chunk(idx, carry):
            c = nc - 1 - idx
            rows = pl.ds(pl.multiple_of(c * CHUNK, CHUNK), CHUNK)
            la = la_ref[rows, :]
            b = _mm_01(trif, la)
            bl = jnp.sum(la, axis=0, keepdims=True)
            eb = jnp.exp(b)
            enb = jnp.exp(-b)
            ebl = jnp.exp(bl - b)
            el = jnp.exp(bl)
            qb = q_ref[rows, :] * (qscale * eb)
            kk = k_ref[rows, :]
            kb = kk * enb
            kl = kk * ebl
            vv = v_ref[rows, :]
            do = do_s[rows, :]
            compact = sall_ref[c]
            s_t = jnp.concatenate([compact * hk[h] for h in range(N_HEADS)], axis=0)
            ds_t = dst[...]
            qx = jnp.concatenate([qb.astype(BF16) * hk[h].astype(BF16) for h in range(N_HEADS)], axis=0)
            dox = jnp.concatenate([do.astype(BF16) * hv[h].astype(BF16) for h in range(N_HEADS)], axis=0)
            a = jnp.where(stril, _mm_nt(qx, kb), 0.0).astype(BF16)
            da = jnp.where(stril, _mm_nt(dox, vv), 0.0).astype(BF16)
            dv_ref[rows, :] = (_mm_tn(a, dox) + _mm_nt(kl, ds_t)).astype(BF16)
            dak = _mm(da, kb)
            dqb = _mm(do, s_t)
            for h in range(N_HEADS):
                dqb = dqb + dak[h * CHUNK:(h + 1) * CHUNK, :] * hk[h]
            dkb = _mm_tn(da, qx)
            dkl = _mm(vv, ds_t)
            dbl = jnp.sum(dkl * kl, axis=0, keepdims=True) + el * jnp.sum(ds_t * s_t, axis=0, keepdims=True)
            dst[...] = ds_t * el + _block_diag(_mm_tn(do, qb), hk, dv)
            dq_ref[rows, :] = (dqb * (qscale * eb)).astype(BF16)
            dk_ref[rows, :] = (dkb * enb + dkl * ebl).astype(BF16)
            db = dqb * qb - dkb * kb - dkl * kl + jnp.where(last_row, dbl, 0.0)
            dla = _mm_01(triuf, db)
            dlogit = dla * (1.0 / GLA_GATE_TEMP) * (1.0 - jnp.exp(GLA_GATE_TEMP * la))
            dzg_ref[rows, :] = _mm_nt(dlogit, wgu_ref[...]).astype(BF16)
            dwgu_ref[...] += _mm_tn(zg_ref[rows, :], dlogit)
            dbg_ref[...] += jnp.sum(dlogit, axis=0, keepdims=True)
            return carry

        lax.fori_loop(0, nc, chunk, 0, unroll=CHUNK_UNROLL)

    row = lambda i: (nt - 1 - i, 0)
    const = lambda i: (0, 0)
    return _tc_call(
        body, name=name, grid=(nt,),
        in_specs=[pl.BlockSpec((tm, vw), row), pl.BlockSpec((tm, vw), row), pl.BlockSpec((tm, vw), row),
                  pl.BlockSpec((1, vw), const), pl.BlockSpec((tm, kw), row), pl.BlockSpec((tm, kw), row),
                  pl.BlockSpec((tm, vw), row), pl.BlockSpec((tm, kw), row), pl.BlockSpec((tm, LANE), row),
                  pl.BlockSpec((nc, dv, kw), lambda i: (nt - 1 - i, 0, 0)), pl.BlockSpec((LANE, kw), const)],
        out_specs=[pl.BlockSpec((tm, kw), row), pl.BlockSpec((tm, kw), row), pl.BlockSpec((tm, vw), row),
                   pl.BlockSpec((tm, vw), row), pl.BlockSpec((tm, LANE), row), pl.BlockSpec((LANE, kw), const),
                   pl.BlockSpec((1, kw), const), pl.BlockSpec((1, vw), const)],
        out_shape=[jax.ShapeDtypeStruct((t, kw), BF16), jax.ShapeDtypeStruct((t, kw), BF16),
                   jax.ShapeDtypeStruct((t, vw), BF16), jax.ShapeDtypeStruct((t, vw), BF16),
                   jax.ShapeDtypeStruct((t, LANE), BF16), jax.ShapeDtypeStruct((LANE, kw), F32),
                   jax.ShapeDtypeStruct((1, kw), F32), jax.ShapeDtypeStruct((1, vw), F32)],
        scratch_shapes=[pltpu.VMEM((vw, kw), F32), pltpu.VMEM((tm, vw), F32)],
        compiler_params=_seq(1),
    )(dy, o, r, gnorm, q, k, v, loga, zg, sall, wgu)


def _outproj_fwd(yp, yg, w_out, xhat, gam, bet, alpha, name):
    t, d = xhat.shape
    pw = yp.shape[1]
    tm = _row_tile(t)

    def body(yp_ref, yg_ref, w_ref, x_ref, g_ref, b_ref, xhat_ref, rstd_ref):
        for part in range(ROW_GROUPS):
            rows = pl.ds(part * (tm // ROW_GROUPS), tm // ROW_GROUPS)
            h = x_ref[rows, :] * g_ref[...] + b_ref[...]
            y = (jnp.dot(yp_ref[rows, :], w_ref[0:pw, :], preferred_element_type=F32)
                 + jnp.dot(yg_ref[rows, :], w_ref[pw:, :], preferred_element_type=F32))
            xh, rs = _ln_stats(alpha * h + y)
            xhat_ref[rows, :] = xh
            rstd_ref[rows, :] = rs

    row = lambda i: (i, 0)
    vec = pl.BlockSpec((1, d), lambda i: (0, 0))
    return _tc_call(
        body, name=name, grid=(t // tm,),
        in_specs=[pl.BlockSpec((tm, pw), row), pl.BlockSpec((tm, yg.shape[1]), row),
                  pl.BlockSpec(w_out.shape, lambda i: (0, 0)), pl.BlockSpec((tm, d), row), vec, vec],
        out_specs=[pl.BlockSpec((tm, d), row), pl.BlockSpec((tm, 1), row)],
        out_shape=[jax.ShapeDtypeStruct((t, d), F32), jax.ShapeDtypeStruct((t, 1), F32)],
        compiler_params=_seq(1),
    )(yp, yg, w_out, xhat, gam, bet)


def _outproj_bwd(dh, xhat, rstd, ln_g, w_out, pw, alpha, name):
    t, d = dh.shape
    tm = _row_tile(t)
    gw = w_out.shape[0] - pw

    def body(dh_ref, xh_ref, rs_ref, g_ref, w_ref, dyb_ref, dyp_ref, dyg_ref, dres_ref, dgam_ref, dbet_ref):
        @pl.when(pl.program_id(0) == 0)
        def _():
            dgam_ref[...] = jnp.zeros_like(dgam_ref)
            dbet_ref[...] = jnp.zeros_like(dbet_ref)

        for part in range(ROW_GROUPS):
            rows = pl.ds(part * (tm // ROW_GROUPS), tm // ROW_GROUPS)
            dy = dh_ref[rows, :]
            xh = xh_ref[rows, :]
            dr = _ln_bwd(dy, xh, rs_ref[rows, :], g_ref[...])
            dgam_ref[...] += jnp.sum(dy * xh, axis=0, keepdims=True)
            dbet_ref[...] += jnp.sum(dy, axis=0, keepdims=True)
            drb = dr.astype(BF16)
            dyb_ref[rows, :] = drb
            dres_ref[rows, :] = alpha * dr
            dyp_ref[rows, :] = _mm_nt(drb, w_ref[0:pw, :])
            dyg_ref[rows, :] = _mm_nt(drb, w_ref[pw:, :])

    row = lambda i: (i, 0)
    vec = pl.BlockSpec((1, d), lambda i: (0, 0))
    return _tc_call(
        body, name=name, grid=(t // tm,),
        in_specs=[pl.BlockSpec((tm, d), row), pl.BlockSpec((tm, d), row), pl.BlockSpec((tm, 1), row), vec,
                  pl.BlockSpec(w_out.shape, lambda i: (0, 0))],
        out_specs=[pl.BlockSpec((tm, d), row), pl.BlockSpec((tm, pw), row), pl.BlockSpec((tm, gw), row),
                   pl.BlockSpec((tm, d), row), vec, vec],
        out_shape=[jax.ShapeDtypeStruct((t, d), BF16), jax.ShapeDtypeStruct((t, pw), F32),
                   jax.ShapeDtypeStruct((t, gw), F32), jax.ShapeDtypeStruct((t, d), F32),
                   jax.ShapeDtypeStruct((1, d), F32), jax.ShapeDtypeStruct((1, d), F32)],
        compiler_params=_seq(1),
    )(dh, xhat, rstd, ln_g, w_out)


def _loss_head(xhat, gam, bet, target, n_rows, name):
    t, d = xhat.shape
    tm = _row_tile(t)

    def body(x_ref, g_ref, b_ref, t_ref, dy_ref, loss_ref):
        i = pl.program_id(0)

        @pl.when(i == 0)
        def _():
            loss_ref[...] = jnp.zeros_like(loss_ref)

        rowi = i * tm + lax.broadcasted_iota(jnp.int32, (tm, 1), 0)
        live = (rowi >= N_META) & (rowi < N_META + n_rows)
        diff = jnp.where(live, x_ref[...] * g_ref[...] + b_ref[...] - t_ref[...], 0.0)
        dy_ref[...] = diff * (1.0 / d)
        loss_ref[...] += jnp.sum(diff * diff) * (0.5 / d)

    row = lambda i: (i, 0)
    vec = pl.BlockSpec((1, d), lambda i: (0, 0))
    return _tc_call(
        body, name=name, grid=(t // tm,),
        in_specs=[pl.BlockSpec((tm, d), row), vec, vec, pl.BlockSpec((tm, d), row)],
        out_specs=[pl.BlockSpec((tm, d), row), pl.BlockSpec((8, LANE), lambda i: (0, 0))],
        out_shape=[jax.ShapeDtypeStruct((t, d), F32), jax.ShapeDtypeStruct((8, LANE), F32)],
        compiler_params=_seq(1),
    )(xhat, gam, bet, target)


def _rows_block(r, c):
    best = r
    for cand in range(BF16_ROWS, r, BF16_ROWS):
        if r % cand == 0 and cand * c * 4 <= (1 << 20):
            best = cand
    return best if best * c * 4 <= (4 << 20) else r


def _sum_slots(me, mine, recvs, name, layers_side_by_side=False):
    nl = len(recvs)
    ns, r, c = recvs[0].shape
    tr = _rows_block(r, c)

    def body(me_ref, *refs):
        o_ref = refs[nl * ns]
        for l in range(nl):
            acc = refs[l * ns][...].astype(F32)
            for s in range(1, ns):
                acc = acc + refs[l * ns + s][...].astype(F32)
            if layers_side_by_side:
                o_ref[0, :, l * c:(l + 1) * c] = acc.astype(o_ref.dtype)
            else:
                o_ref[l] = acc.astype(o_ref.dtype)

    def slot(s):
        return pl.BlockSpec((None, tr, c), lambda i, me_ref: ((me_ref[0] + s) % ns, i, 0))

    out = (1, r, nl * c) if layers_side_by_side else (nl, r, c)
    operands = []
    for l in range(nl):
        operands += [mine[l]] + [recvs[l]] * (ns - 1)
    return _tc_call(
        body, name=name,
        grid_spec=pltpu.PrefetchScalarGridSpec(
            num_scalar_prefetch=1, grid=(r // tr,),
            in_specs=[slot(s) for s in range(ns)] * nl,
            out_specs=pl.BlockSpec((out[0], tr, out[2]), lambda i, me_ref: (0, i, 0))),
        out_shape=jax.ShapeDtypeStruct(out, recvs[0].dtype),
        compiler_params=_seq(1),
    )(me, *operands)


def _adamw(w, terms, m, v, name):
    nl, r, c = w.shape
    tc = c
    while tc % (2 * LANE) == 0 and tc > 4 * LANE:
        tc //= 2
    tr = _rows_block(r, tc)
    nterm = len(terms)

    def body(*refs):
        w_ref = refs[0]
        t_refs = refs[1:1 + nterm]
        m_ref, v_ref, g_ref, d_ref, nm_ref, nv_ref = refs[1 + nterm:]
        g = t_refs[0][...].astype(F32)
        for tr_ in t_refs[1:]:
            g = g + tr_[...].astype(F32)
        nm = ADAM_B1 * m_ref[...] + (1.0 - ADAM_B1) * g
        nv = ADAM_B2 * v_ref[...] + (1.0 - ADAM_B2) * jnp.square(g)
        m_hat = nm / (1.0 - ADAM_B1 ** ADAM_STEP)
        v_hat = nv / (1.0 - ADAM_B2 ** ADAM_STEP)
        g_ref[...] = g
        d_ref[...] = -ADAM_LR * (m_hat / (jnp.sqrt(v_hat) + ADAM_EPS) + ADAM_WD * w_ref[...])
        nm_ref[...] = nm
        nv_ref[...] = nv

    spec = pl.BlockSpec((None, tr, tc), lambda l, i, j: (l, i, j))
    shp = jax.ShapeDtypeStruct((nl, r, c), F32)
    return _tc_call(
        body, name=name, grid=(nl, r // tr, c // tc),
        in_specs=[spec] * (3 + nterm), out_specs=[spec] * 4, out_shape=[shp] * 4,
        compiler_params=_seq(3),
    )(w, *terms, m, v)


XY_RELATIONS = ((1, 0, 0), (0, 1, 0), (1, 1, 0))
ALL_RELATIONS = tuple((fx, fy, fc) for fx in (0, 1) for fy in (0, 1) for fc in (0, 1) if fx or fy or fc)
HBM_SPEC = pl.BlockSpec(memory_space=pltpu.HBM)
SEM_SPEC = pl.BlockSpec(memory_space=pltpu.SEMAPHORE)
DATAFLOW = pltpu.SideEffectType.DATAFLOW_SIDE_EFFECTING


def _split_call(body, **kw):
    return pl.pallas_call(body, **kw)


def _flip(v, f):
    return 1 - v if f else v


def _any_spec(n):
    return [pl.BlockSpec(memory_space=pl.ANY)] * n


def _relations(kind):
    return ALL_RELATIONS if kind == "bcast" else XY_RELATIONS


def _copies(kind, arr, land, sems):
    x, y, c = lax.axis_index("x"), lax.axis_index("y"), lax.axis_index("c")
    out = []
    for (fx, fy, fc), (send_sem, recv_sem) in zip(_relations(kind), sems):
        px, py, pc = _flip(x, fx), _flip(y, fy), _flip(c, fc)
        if kind == "bcast":
            mine, theirs = 4 * x + 2 * y + c, 4 * px + 2 * py + pc
        else:
            mine, theirs = 2 * x + y, 2 * px + py
        src, to_mine, to_theirs = arr, land.at[mine], land.at[theirs]
        if kind == "scatter":
            src = arr.at[theirs]
        if kind == "gather_half":
            rows = _my_half(arr.shape[0], c)
            src, to_mine, to_theirs = arr.at[rows], land.at[mine, rows], land.at[theirs, rows]
        both = dict(src_ref=src, send_sem=send_sem, recv_sem=recv_sem, device_id=(px, py, pc), device_id_type=MESH)
        out.append((pltpu.make_async_remote_copy(dst_ref=to_mine, **both),
                    pltpu.make_async_remote_copy(dst_ref=to_theirs, **both)))
    return out


def _my_half(nrows, c):
    return pl.ds(c * (nrows // 2), nrows // 2)


def _share_halves(name, kinds, lands):
    ks = [k for k, kd in enumerate(kinds) if kd == "gather_half"]
    n = len(ks)

    def body(*refs):
        l_refs = refs[n:2 * n]
        send_sems, recv_sems = refs[2 * n:]
        x, y, c = lax.axis_index("x"), lax.axis_index("y"), lax.axis_index("c")
        copies = []
        for i in range(n):
            nrows = l_refs[i].shape[1]
            for r, (fx, fy, _) in enumerate(XY_RELATIONS):
                slot = 2 * _flip(x, fx) + _flip(y, fy)
                both = dict(src_ref=l_refs[i].at[slot, _my_half(nrows, c)], send_sem=send_sems.at[i, r],
                            recv_sem=recv_sems.at[i, r], device_id=(x, y, 1 - c), device_id_type=MESH)
                copies.append((pltpu.make_async_remote_copy(dst_ref=l_refs[i].at[slot, _my_half(nrows, c)], **both),
                               pltpu.make_async_remote_copy(dst_ref=l_refs[i].at[slot, _my_half(nrows, 1 - c)], **both)))
        for send, _ in copies:
            send.start()
        for _, arrival in copies:
            arrival.wait_recv()
        for send, _ in copies:
            send.wait_send()

    outs = _comm_call(
        body, name=name,
        in_specs=_any_spec(n), out_specs=_any_spec(n),
        out_shape=[jax.ShapeDtypeStruct(lands[k].shape, lands[k].dtype) for k in ks],
        input_output_aliases={i: i for i in range(n)},
        scratch_shapes=[pltpu.SemaphoreType.DMA((n, 3)), pltpu.SemaphoreType.DMA((n, 3))],
    )(*[lands[k] for k in ks])
    lands = list(lands)
    for k, o in zip(ks, outs):
        lands[k] = o
    return lands


def _sem_pairs(kinds, sems):
    out, at = [], 0
    for kind in kinds:
        nrel = len(_relations(kind))
        out.append([(sems[at + 2 * r], sems[at + 2 * r + 1]) for r in range(nrel)])
        at += 2 * nrel
    return out


def _exchange_start(name, kinds, arrs, lands):
    n = len(arrs)
    nsem = sum(2 * len(_relations(kd)) for kd in kinds)

    def body(*refs):
        a_refs, l_refs = refs[:n], refs[n:2 * n]
        pairs = _sem_pairs(kinds, refs[2 * n:2 * n + nsem])
        token = refs[-1]
        for k in range(n):
            for send, _ in _copies(kinds[k], a_refs[k], l_refs[k], pairs[k]):
                send.start()
        token[...] = jnp.zeros_like(token)

    thru = [pltpu.HBM(a.shape, a.dtype) for a in list(arrs) + list(lands)]
    outs = _split_call(
        body, name=name,
        out_shape=(*[pltpu.SemaphoreType.DMA(())] * nsem, *thru, jax.ShapeDtypeStruct((8, LANE), F32)),
        in_specs=[HBM_SPEC] * (2 * n),
        out_specs=(*[SEM_SPEC] * nsem, *[HBM_SPEC] * (2 * n), pl.BlockSpec(memory_space=pltpu.VMEM)),
        input_output_aliases={i: nsem + i for i in range(2 * n)},
        compiler_params=pltpu.CompilerParams(has_side_effects=DATAFLOW),
    )(*[pltpu.with_memory_space_constraint(a, pltpu.HBM) for a in list(arrs) + list(lands)])
    return dict(kinds=kinds, sems=outs[:nsem], arrs=outs[nsem:nsem + n], lands=outs[nsem + n:nsem + 2 * n],
                token=outs[-1])


def _exchange_wait(name, st, after):
    kinds = st["kinds"]
    n = len(kinds)
    nsem = len(st["sems"])

    def body(*refs):
        a_refs, l_refs = refs[:n], refs[n:2 * n]
        pairs = _sem_pairs(kinds, refs[2 * n:2 * n + nsem])
        for k in range(n):
            for _, arrival in _copies(kinds[k], a_refs[k], l_refs[k], pairs[k]):
                arrival.wait_send()
                arrival.wait_recv()
        refs[-1][...] = jnp.zeros_like(refs[-1])

    ins = list(st["arrs"]) + list(st["lands"])
    outs = _split_call(
        body, name=name,
        out_shape=[pltpu.HBM(a.shape, a.dtype) for a in ins] + [jax.ShapeDtypeStruct((8, LANE), F32)],
        in_specs=[HBM_SPEC] * (2 * n) + [SEM_SPEC] * nsem + [pl.BlockSpec(memory_space=pl.ANY)],
        out_specs=[HBM_SPEC] * (2 * n) + [pl.BlockSpec(memory_space=pltpu.VMEM)],
        input_output_aliases={i: i for i in range(2 * n)},
        compiler_params=pltpu.CompilerParams(has_side_effects=DATAFLOW),
    )(*ins, *st["sems"], after)
    return outs[:n], outs[n:2 * n], outs[-1]


def _landing(own, slot, nslot):
    return lax.dynamic_update_slice(lax.empty((nslot,) + own.shape, own.dtype), own[None], (slot,) + (0,) * own.ndim)


def _swap_sibling(parts, name):
    n = len(parts)

    def body(*refs):
        ins, outs = refs[:n], refs[n:2 * n]
        send_sems, recv_sems = refs[2 * n:]
        sib = (lax.axis_index("x"), lax.axis_index("y"), 1 - lax.axis_index("c"))
        cps = [pltpu.make_async_remote_copy(src_ref=ins[k], dst_ref=outs[k], send_sem=send_sems.at[k],
                                            recv_sem=recv_sems.at[k], device_id=sib, device_id_type=MESH)
               for k in range(n)]
        for cp in cps:
            cp.start()
        for cp in cps:
            cp.wait_recv()
        for cp in cps:
            cp.wait_send()

    return _comm_call(
        body, name=name,
        in_specs=_any_spec(n), out_specs=_any_spec(n),
        out_shape=[jax.ShapeDtypeStruct(p.shape, p.dtype) for p in parts],
        scratch_shapes=[pltpu.SemaphoreType.DMA((n,)), pltpu.SemaphoreType.DMA((n,))],
    )(*parts)


def _col_shards(a, n=N_SHARD):
    r, c = a.shape
    return a.reshape(r, n, c // n).transpose(1, 0, 2)


def _from_col_shards(a):
    n, r, cs = a.shape
    return a.transpose(1, 0, 2).reshape(r, n * cs)


def kernel(x, meta_tokens, ffn1_w_gate, ffn1_w_up, ffn1_w_down, ln1_g, ln1_b, w_in, w_gate_up, b_gate, w_pool, pool_scale, gla_norm_g, w_out, ln2_g, ln2_b, ffn2_w_gate, ffn2_w_up, ffn2_w_down, ln3_g, ln3_b, loss_target, m_meta_tokens, m_ffn1_w_gate, m_ffn1_w_up, m_ffn1_w_down, m_ln1_g, m_ln1_b, m_w_in, m_w_gate_up, m_b_gate, m_w_pool, m_pool_scale, m_gla_norm_g, m_w_out, m_ln2_g, m_ln2_b, m_ffn2_w_gate, m_ffn2_w_up, m_ffn2_w_down, m_ln3_g, m_ln3_b, v_meta_tokens, v_ffn1_w_gate, v_ffn1_w_up, v_ffn1_w_down, v_ln1_g, v_ln1_b, v_w_in, v_w_gate_up, v_b_gate, v_w_pool, v_pool_scale, v_gla_norm_g, v_w_out, v_ln2_g, v_ln2_b, v_ffn2_w_gate, v_ffn2_w_up, v_ffn2_w_down, v_ln3_g, v_ln3_b):
    w = dict(meta_tokens=meta_tokens, ffn1_w_gate=ffn1_w_gate, ffn1_w_up=ffn1_w_up, ffn1_w_down=ffn1_w_down,
             ln1_g=ln1_g, ln1_b=ln1_b, w_in=w_in, w_gate_up=w_gate_up, b_gate=b_gate, w_pool=w_pool,
             pool_scale=pool_scale, gla_norm_g=gla_norm_g, w_out=w_out, ln2_g=ln2_g, ln2_b=ln2_b,
             ffn2_w_gate=ffn2_w_gate, ffn2_w_up=ffn2_w_up, ffn2_w_down=ffn2_w_down, ln3_g=ln3_g, ln3_b=ln3_b)
    mom1 = dict(meta_tokens=m_meta_tokens, ffn1_w_gate=m_ffn1_w_gate, ffn1_w_up=m_ffn1_w_up,
                ffn1_w_down=m_ffn1_w_down, ln1_g=m_ln1_g, ln1_b=m_ln1_b, w_in=m_w_in, w_gate_up=m_w_gate_up,
                b_gate=m_b_gate, w_pool=m_w_pool, pool_scale=m_pool_scale, gla_norm_g=m_gla_norm_g, w_out=m_w_out,
                ln2_g=m_ln2_g, ln2_b=m_ln2_b, ffn2_w_gate=m_ffn2_w_gate, ffn2_w_up=m_ffn2_w_up,
                ffn2_w_down=m_ffn2_w_down, ln3_g=m_ln3_g, ln3_b=m_ln3_b)
    mom2 = dict(meta_tokens=v_meta_tokens, ffn1_w_gate=v_ffn1_w_gate, ffn1_w_up=v_ffn1_w_up,
                ffn1_w_down=v_ffn1_w_down, ln1_g=v_ln1_g, ln1_b=v_ln1_b, w_in=v_w_in, w_gate_up=v_w_gate_up,
                b_gate=v_b_gate, w_pool=v_w_pool, pool_scale=v_pool_scale, gla_norm_g=v_gla_norm_g, w_out=v_w_out,
                ln2_g=v_ln2_g, ln2_b=v_ln2_b, ffn2_w_gate=v_ffn2_w_gate, ffn2_w_up=v_ffn2_w_up,
                ffn2_w_down=v_ffn2_w_down, ln3_g=v_ln3_g, ln3_b=v_ln3_b)

    xs = x[0]
    s_len, d = xs.shape
    nl = ln1_g.shape[0]
    alpha = (2.0 * nl) ** 0.25
    t_real = N_META + s_len
    t_pad = -(-t_real // LANE) * LANE
    pw = pool_scale.shape[1]
    kw = b_gate.shape[1]
    vw = gla_norm_g.shape[1]
    rank = w_gate_up.shape[1]
    widths = (pw, kw, kw, vw, vw)
    n_main = sum(widths)
    dff_s = ffn1_w_gate.shape[2]
    dff_c = N_SHARD * dff_s // FFN_CHUNKS

    me_xy = 2 * lax.axis_index("x") + lax.axis_index("y")
    me_all = 2 * me_xy + lax.axis_index("c")
    ffn1_names = ("ffn1_w_gate", "ffn1_w_up", "ffn1_w_down")
    mix_names = ("w_out", "w_gate_up", "w_in")
    ffn2_names = ("ffn2_w_gate", "ffn2_w_up", "ffn2_w_down")

    gate_up = ("ffn1_w_gate", "ffn1_w_up", "ffn2_w_gate", "ffn2_w_up")

    def stored(n, a):
        if n in gate_up:
            return jnp.swapaxes(a, 1, 2)
        return jnp.transpose(a, (2, 0, 1)) if n == "w_in" else a

    def as_given(n, a):
        if n in gate_up:
            return jnp.swapaxes(a, 1, 2)
        if n == "w_in":
            return jnp.transpose(a.reshape(-1, nl, d), (1, 2, 0))
        return a.reshape(w[n].shape)

    stages = [[("meta_tokens", None)], [(n, 0) for n in ffn1_names], [(n, 0) for n in mix_names + ffn2_names]]
    stages += [[(n, l) for n in BIG] for l in range(1, nl)]
    gathers, wa = {}, {}

    halved = ffn1_names + ffn2_names + ("w_out",)

    def start_gather(si, dep=None):
        own = []
        for n, l in stages[si]:
            a = meta_tokens if l is None else (stored(n, w[n])[:, l] if n == "w_in" else stored(n, w[n])[l])
            a = a if dep is None else a + dep
            own.append(a if l is None else a.astype(BF16))
        gathers[si] = _exchange_start(f"gather_start_{si}", ["gather_half" if n in halved else "gather"
                                                             for n, _ in stages[si]], own,
                                      [_landing(a, me_xy, N_SHARD) for a in own])
        return gathers[si]["token"]

    def arrive(si, after):
        _, lands, token = _exchange_wait(f"gather_wait_{si}", gathers[si], after)
        if any(kd == "gather_half" for kd in gathers[si]["kinds"]):
            lands = _share_halves(f"gather_share_{si}", gathers[si]["kinds"], lands)
        for item, a in zip(stages[si], lands):
            wa[item] = a.reshape(FFN_CHUNKS, -1, d) if item[0] in ffn1_names + ffn2_names else a
        return token

    def mixer_weights(l):
        wi = wa["w_in", l].reshape(-1, d)
        return dict(w_main=wi[:n_main], w_lr=jnp.pad(wi[n_main:], ((0, LANE - rank), (0, 0))),
                    wgu=jnp.pad(_from_col_shards(wa["w_gate_up", l]), ((0, LANE - rank), (0, 0))),
                    wout=wa["w_out", l].reshape(-1, d))

    wp16 = w_pool.astype(BF16)
    ones = jnp.ones((1, d), F32)
    zeros = jnp.zeros((1, d), F32)
    target = jnp.concatenate([jnp.zeros((N_META, d), F32), loss_target[0], jnp.zeros((t_pad - t_real, d), F32)], axis=0)

    started = start_gather(0)
    for si in range(1, len(stages)):
        started = start_gather(si, started[0:1, 0:1])
    arrive(0, started)
    meta_full = _from_col_shards(wa["meta_tokens", None])
    h0 = jnp.concatenate([meta_full, xs, jnp.zeros((t_pad - t_real, d), F32)], axis=0)
    arrive(1, h0[:8, :LANE] + target[:8, :LANE])

    saved, mw = [], []
    cur, cur_g, cur_b = h0, ones, zeros
    for l in range(nl):
        s = {}
        xh1, rs1, hb0, g1, u1 = _ffn_fwd(cur, cur_g, cur_b, wa["ffn1_w_gate", l], wa["ffn1_w_up", l],
                                         wa["ffn1_w_down", l], alpha, f"ffn1_fwd_{l}")
        if l == 0:
            arrive(2, xh1)
        mw.append(mixer_weights(l))
        up, q, k, v, r, zg, la, hb1 = _inproj_fwd(xh1, ln1_g[l:l + 1], ln1_b[l:l + 1], mw[l]["w_main"], mw[l]["w_lr"],
                                                  mw[l]["wgu"], b_gate[l:l + 1], widths, f"inproj_fwd_{l}")
        yp, pb = _pool_fwd(up, wp16[l], pool_scale[l:l + 1], f"pool_fwd_{l}")
        o, yg, sall = _gla_fwd(q, k, v, la, r, gla_norm_g[l:l + 1], f"gla_fwd_{l}")
        xh2, rs2 = _outproj_fwd(yp, yg, mw[l]["wout"], xh1, ln1_g[l:l + 1], ln1_b[l:l + 1], alpha, f"outproj_fwd_{l}")
        if l + 1 < nl:
            arrive(l + 3, xh2)
        xh3, rs3, hb2, g2, u2 = _ffn_fwd(xh2, ln2_g[l:l + 1], ln2_b[l:l + 1], wa["ffn2_w_gate", l], wa["ffn2_w_up", l],
                                         wa["ffn2_w_down", l], alpha, f"ffn2_fwd_{l}")
        s.update(xh1=xh1, rs1=rs1, hb0=hb0, g1=g1, u1=u1, q=q, k=k, v=v, r=r, zg=zg, la=la, hb1=hb1, yp=yp, pb=pb,
                 o=o, yg=yg, sall=sall, xh2=xh2, rs2=rs2, xh3=xh3, rs3=rs3, hb2=hb2, g2=g2, u2=u2)
        saved.append(s)
        cur, cur_g, cur_b = xh3, ln3_g[l:l + 1], ln3_b[l:l + 1]

    dh, loss_acc = _loss_head(cur, cur_g, cur_b, target, s_len, "loss_head")
    loss = lax.psum(loss_acc[0, 0], ("x", "y", "c"))

    small_grads = {n: [None] * nl for n in SMALL}
    scatters = []

    def depart(name, items, grads, kinds=None):
        lands = [_landing(g, me_all, N_DEV) if kd == "bcast" else lax.empty(g.shape, g.dtype)
                 for g, kd in zip(grads, kinds or ["scatter"] * len(grads))]
        st = _exchange_start(name, kinds or ["scatter"] * len(grads), grads, lands)
        scatters.append((name, items, st))
        return st["token"]

    def pack(parts):
        flat = jnp.concatenate([parts[n].reshape(-1) for n in SMALL])
        return flat.reshape(-1, LANE)

    def ffn_wgrad(n, l, hb, dgb, dub, act, dfb, after=None):
        if n.endswith("down"):
            dw = _wgrad(act, dfb, dff_c, d, f"{n}_grad_{l}", after)
        else:
            dw = _wgrad(dgb if n.endswith("gate") else dub, hb, dff_c, d, f"{n}_grad_{l}", after)
        return dw.reshape(N_SHARD, dff_s, d)

    late = []
    for l in reversed(range(nl)):
        s = saved[l]
        dh, dfb, dgb, dub, act, dgam, dbet = _ffn_bwd(dh, s["xh3"], s["rs3"], ln3_g[l:l + 1], s["g2"], s["u2"],
                                                      wa["ffn2_w_gate", l], wa["ffn2_w_up", l], wa["ffn2_w_down", l],
                                                      alpha, f"ffn2_bwd_{l}")
        small_grads["ln3_g"][l], small_grads["ln3_b"][l] = dgam, dbet
        gone = depart(f"scatter_start_ffn2_{l}", [(n, l) for n in ffn2_names],
                      [ffn_wgrad(n, l, s["hb2"], dgb, dub, act, dfb) for n in ffn2_names])

        dyb, dyp, dyg, dres, dgam, dbet = _outproj_bwd(dh, s["xh2"], s["rs2"], ln2_g[l:l + 1] + gone[0:1, 0:1],
                                                       mw[l]["wout"], pw, alpha, f"outproj_bwd_{l}")
        small_grads["ln2_g"][l], small_grads["ln2_b"][l] = dgam, dbet
        dwo = jnp.concatenate([_wgrad(s["yp"], dyb, pw, d, f"dwout_pool_{l}"),
                               _wgrad(s["yg"], dyb, vw, d, f"dwout_gla_{l}")], axis=0)
        dq, dk, dv, dr, dzg, dwgu, dbg, dgn = _gla_bwd(dyg, s["o"], s["r"], gla_norm_g[l:l + 1], s["q"], s["k"],
                                                       s["v"], s["la"], s["zg"], s["sall"], mw[l]["wgu"],
                                                       f"gla_bwd_{l}")
        dup, dwp, dsc = _pool_bwd(dyp, s["pb"], wp16[l], pool_scale[l:l + 1], f"pool_bwd_{l}")
        small_grads["b_gate"][l], small_grads["gla_norm_g"][l] = dbg, dgn
        small_grads["w_pool"][l], small_grads["pool_scale"][l] = dwp, dsc
        dh, dz = _inproj_bwd(dres, [dup, dq, dk, dv, dr], dzg, mw[l]["w_main"], mw[l]["w_lr"], f"inproj_bwd_{l}")
        dwi = jnp.concatenate([_wgrad(dz, s["hb1"], 4 * LANE, d, f"dwin_main_{l}"),
                               _wgrad(dzg, s["hb1"], LANE, d, f"dwin_lr_{l}")[:rank]], axis=0)
        gone = depart(f"scatter_start_mix_{l}", [(n, l) for n in mix_names],
                      [dwo.reshape(N_SHARD, -1, d), _col_shards(dwgu[:rank]), dwi.reshape(N_SHARD, -1, d)])

        dh, dfb, dgb, dub, act, dgam, dbet = _ffn_bwd(dh, s["xh1"], s["rs1"], ln1_g[l:l + 1] + gone[0:1, 0:1],
                                                      s["g1"], s["u1"], wa["ffn1_w_gate", l], wa["ffn1_w_up", l],
                                                      wa["ffn1_w_down", l], alpha, f"ffn1_bwd_{l}")
        small_grads["ln1_g"][l], small_grads["ln1_b"][l] = dgam, dbet
        if l:
            gone = depart(f"scatter_start_ffn1_{l}", [(n, l) for n in ffn1_names],
                          [ffn_wgrad(n, l, s["hb0"], dgb, dub, act, dfb) for n in ffn1_names])
            ln3_g = ln3_g.at[l - 1:l].add(gone[0:1, 0:1])
            continue
        grad_x = dh[N_META:t_real][None]
        small_vec = pack({n: jnp.stack(small_grads[n]) for n in SMALL})
        gone = depart("scatter_start_rest", [("meta_tokens", 0), ("small", 0)],
                      [_col_shards(dh[:N_META]), small_vec], ["scatter", "bcast"])
        for n in ffn1_names:
            g = ffn_wgrad(n, l, s["hb0"], dgb, dub, act, dfb, after=gone)
            gone = depart(f"scatter_start_{n}", [(n, l)], [g])
            late.append(scatters.pop())

    sent, recv, results, firsts = {}, {}, {}, []
    my_slot = me_xy.reshape(1).astype(jnp.int32)

    def collect(group, after):
        for name, items, st in group:
            arrs, lands, _ = _exchange_wait(name.replace("start", "wait"), st, after)
            for item, a, b in zip(items, arrs, lands):
                sent[item], recv[item] = a, b

    def reduce_and_update(names, tag):
        partial = []
        for n in names:
            layers = [(n, l) for l in range(1 if n == "meta_tokens" else nl)]
            partial.append(_sum_slots(my_slot, [sent[it] for it in layers], [recv[it] for it in layers],
                                      f"sum_{n}", n == "w_in"))
        for n, mine, theirs in zip(names, partial, _swap_sibling(partial, f"swap_sibling_{tag}")):
            fit = lambda a: stored(n, a).reshape(mine.shape)
            outs = _adamw(fit(w[n]), [mine, theirs], fit(mom1[n]), fit(mom2[n]), f"adamw_{n}")
            results[n] = [as_given(n, o) for o in outs]
            firsts.append(outs[1][0, 0, 0])

    collect(scatters, gone)
    early = [n for n in ("meta_tokens",) + BIG if n not in ffn1_names]
    reduce_and_update(early, "early")
    small_terms = [recv["small", 0][i][None] for i in range(N_DEV)]
    souts = _adamw(pack(w)[None], small_terms, pack(mom1)[None], pack(mom2)[None], "adamw_small")
    off = 0
    for n in SMALL:
        size = w[n].size
        results[n] = [o.reshape(-1)[off:off + size].reshape(w[n].shape) for o in souts]
        off += size
    collect(late, souts[0][0, :8] + functools.reduce(jnp.add, firsts))
    reduce_and_update(ffn1_names, "late")

    out = [loss, grad_x]
    for part in range(4):
        out += [results[n][part] for n in WEIGHTS]
    return tuple(out)
```

```python
import functools

import jax
import jax.numpy as jnp
from jax import lax
from jax.experimental import pallas as pl
from jax.experimental.pallas import tpu as pltpu

F32 = jnp.float32
BF16 = jnp.bfloat16
MESH = pl.DeviceIdType.MESH

N_META = 16
POOL_WINDOWS = (2, 4, 8, 16)
POOL_HALO = 16
N_HEADS = 4
GLA_GATE_TEMP = 16.0
CHUNK = 128
CHUNK_UNROLL = 5
LN_EPS = 1e-5
RMS_EPS = 1e-6
ADAM_LR = 0.001
ADAM_B1 = 0.9
ADAM_B2 = 0.999
ADAM_EPS = 1e-08
ADAM_WD = 0.01
ADAM_STEP = 10
LANE = 128
BF16_ROWS = 16
ROW_TILE = 640
FFN_ROW_TILE = 640
FFN_CHUNKS = 4
FFN_SPLIT = 2
ROW_GROUPS = 2
WGRAD_K_MAX = 2176
N_SHARD = 4
N_DEV = 8

BIG = ("ffn1_w_gate", "ffn1_w_up", "ffn1_w_down", "w_in", "w_gate_up", "w_out",
       "ffn2_w_gate", "ffn2_w_up", "ffn2_w_down")
SMALL = ("ln1_g", "ln1_b", "b_gate", "w_pool", "pool_scale", "gla_norm_g", "ln2_g", "ln2_b", "ln3_g", "ln3_b")
WEIGHTS = ("meta_tokens", "ffn1_w_gate", "ffn1_w_up", "ffn1_w_down", "ln1_g", "ln1_b", "w_in", "w_gate_up",
           "b_gate", "w_pool", "pool_scale", "gla_norm_g", "w_out", "ln2_g", "ln2_b", "ffn2_w_gate",
           "ffn2_w_up", "ffn2_w_down", "ln3_g", "ln3_b")


def _tc_call(body, **kw):
    return pl.pallas_call(body, **kw)


def _comm_call(body, **kw):
    return pl.pallas_call(body, **kw)


def _seq(n):
    return pltpu.CompilerParams(dimension_semantics=("arbitrary",) * n)


def _mm(a, b):
    return jnp.dot(a.astype(BF16), b.astype(BF16), preferred_element_type=F32)


def _mm_nt(a, b):
    return lax.dot_general(a.astype(BF16), b.astype(BF16), (((1,), (1,)), ((), ())), preferred_element_type=F32)


def _mm_tn(a, b):
    return lax.dot_general(a.astype(BF16), b.astype(BF16), (((0,), (0,)), ((), ())), preferred_element_type=F32)


def _mm_01(a, b):
    hi = b.astype(BF16)
    lo = (b - hi.astype(F32)).astype(BF16)
    a = a.astype(BF16)
    return jnp.dot(a, hi, preferred_element_type=F32) + jnp.dot(a, lo, preferred_element_type=F32)


def _row_tile(t, most=None):
    tm = min(most or ROW_TILE, t)
    while t % tm:
        tm -= LANE
    return tm


def _silu_parts(g):
    sg = jax.nn.sigmoid(g)
    return sg, g * sg


def _ln_stats(r):
    mu = jnp.mean(r, axis=-1, keepdims=True)
    rc = r - mu
    var = jnp.mean(rc * rc, axis=-1, keepdims=True)
    rs = lax.rsqrt(var + LN_EPS)
    return rc * rs, rs


def _ln_bwd(dy, xh, rs, gam):
    dyg = dy * gam
    c1 = jnp.mean(dyg, axis=-1, keepdims=True)
    c2 = jnp.mean(dyg * xh, axis=-1, keepdims=True)
    return rs * (dyg - c1 - xh * c2)


def _ffn_fwd(xin, gam_in, bet_in, wg, wu, wd, alpha, name):
    t, d = xin.shape
    nj, tf, _ = wg.shape
    tm = _row_tile(t, FFN_ROW_TILE)
    nt = t // tm
    share = tm // nj

    def body(x_ref, gi_ref, bi_ref, wg_ref, wu_ref, wd_ref, xhat_ref, rstd_ref, hb_ref, go_ref, uo_ref, acc, hbs):
        i = pl.program_id(0)
        j = pl.program_id(1)
        cur = i % 2

        def norm_previous():
            rows = pl.ds(pl.multiple_of(j * share, BF16_ROWS), share)
            xhat, rs = _ln_stats(0.5 * acc[1 - cur, rows, :])
            xhat_ref[rows, :] = xhat
            rstd_ref[rows, :] = rs

        @pl.when(i < nt)
        def _():
            @pl.when(j == 0)
            def _():
                h = x_ref[...] * gi_ref[...] + bi_ref[...]
                hb = h.astype(BF16)
                hbs[...] = hb
                hb_ref[...] = hb
                acc[cur] = (2.0 * alpha) * h

                @pl.when(i == 0)
                def _():
                    acc[1] = jnp.zeros((tm, d), F32)

            norm_previous()
            hb = hbs[...]
            g = _mm_nt(hb, wg_ref[...])
            u = _mm_nt(hb, wu_ref[...])
            _, sl = _silu_parts(g)
            go_ref[...] = g.astype(BF16)
            uo_ref[...] = u.astype(BF16)
            acc[cur] += jnp.dot((sl * u).astype(BF16), wd_ref[...], preferred_element_type=F32)

        @pl.when(i == nt)
        def _():
            norm_previous()

    here = lambda i, j: (jnp.minimum(i, nt - 1), 0)
    before = lambda i, j: (jnp.maximum(i - 1, 0), 0)
    chunk = lambda i, j: (jnp.where(i < nt, j, nj - 1), 0, 0)
    col = pl.BlockSpec((None, tm, tf), lambda i, j: (jnp.where(i < nt, j, nj - 1), jnp.minimum(i, nt - 1), 0))
    vec = pl.BlockSpec((1, d), lambda i, j: (0, 0))
    return _tc_call(
        body, name=name, grid=(nt + 1, nj),
        in_specs=[pl.BlockSpec((tm, d), here), vec, vec] + [pl.BlockSpec((None, tf, d), chunk)] * 3,
        out_specs=[pl.BlockSpec((tm, d), before), pl.BlockSpec((tm, 1), before), pl.BlockSpec((tm, d), here), col, col],
        out_shape=[jax.ShapeDtypeStruct((t, d), F32), jax.ShapeDtypeStruct((t, 1), F32),
                   jax.ShapeDtypeStruct((t, d), BF16), jax.ShapeDtypeStruct((nj, t, tf), BF16),
                   jax.ShapeDtypeStruct((nj, t, tf), BF16)],
        scratch_shapes=[pltpu.VMEM((2, tm, d), F32), pltpu.VMEM((tm, d), BF16)],
        compiler_params=_seq(2),
    )(xin, gam_in, bet_in, wg, wu, wd)


def _ffn_bwd(dh, xhat, rstd, ln_g, gb, ub, wg, wu, wd, alpha, name):
    t, d = dh.shape
    nj, tf, _ = wg.shape
    tm = _row_tile(t, FFN_ROW_TILE)
    nt = t // tm
    share = tm // nj

    def body(dh_ref, xh_ref, rs_ref, g_ref, gb_ref, ub_ref, wg_ref, wu_ref, wd_ref,
             dhin_ref, df_ref, dg_ref, du_ref, act_ref, dgam_ref, dbet_ref, df_s, dres_next, df_next):
        i = pl.program_id(0)
        j = pl.program_id(1)

        @pl.when((i == 0) & (j == 0))
        def _():
            dgam_ref[...] = jnp.zeros_like(dgam_ref)
            dbet_ref[...] = jnp.zeros_like(dbet_ref)

        def look_ahead():
            rows = pl.ds(pl.multiple_of(j * share, BF16_ROWS), share)
            dy = dh_ref[rows, :]
            xh = xh_ref[rows, :]
            dr = _ln_bwd(dy, xh, rs_ref[rows, :], g_ref[...])
            dres_next[rows, :] = alpha * dr
            df_next[rows, :] = (0.5 * dr).astype(BF16)
            live = jnp.where(i < nt, 1.0, 0.0)
            dgam_ref[...] += live * jnp.sum(dy * xh, axis=0, keepdims=True)
            dbet_ref[...] += live * jnp.sum(dy, axis=0, keepdims=True)

        @pl.when(i == 0)
        def _():
            look_ahead()

        @pl.when(i > 0)
        def _():
            @pl.when(j == 0)
            def _():
                dhin_ref[...] = dres_next[...]
                dfb = df_next[...]
                df_s[...] = dfb
                df_ref[...] = dfb

            look_ahead()
            for part in range(FFN_SPLIT):
                rows = pl.ds(part * (tm // FFN_SPLIT), tm // FFN_SPLIT)
                dact = _mm_nt(df_s[rows, :], wd_ref[...])
                g = gb_ref[rows, :].astype(F32)
                u = ub_ref[rows, :].astype(F32)
                sg, sl = _silu_parts(g)
                dg = (dact * u * (sg * (1.0 + g * (1.0 - sg)))).astype(BF16)
                du = (dact * sl).astype(BF16)
                dg_ref[rows, :] = dg
                du_ref[rows, :] = du
                act_ref[rows, :] = (sl * u).astype(BF16)
                dhin_ref[rows, :] += _mm(dg, wg_ref[...]) + _mm(du, wu_ref[...])

    ahead = lambda i, j: (jnp.minimum(i, nt - 1), 0)
    row = lambda i, j: (jnp.maximum(i - 1, 0), 0)
    chunk = lambda i, j: (jnp.where(i > 0, j, 0), 0, 0)
    col = pl.BlockSpec((None, tm, tf), lambda i, j: (jnp.where(i > 0, j, 0), jnp.maximum(i - 1, 0), 0))
    vec = pl.BlockSpec((1, d), lambda i, j: (0, 0))
    ff = jax.ShapeDtypeStruct((nj, t, tf), BF16)
    return _tc_call(
        body, name=name, grid=(nt + 1, nj),
        in_specs=[pl.BlockSpec((tm, d), ahead), pl.BlockSpec((tm, d), ahead), pl.BlockSpec((tm, 1), ahead), vec,
                  col, col] + [pl.BlockSpec((None, tf, d), chunk)] * 3,
        out_specs=[pl.BlockSpec((tm, d), row), pl.BlockSpec((tm, d), row), col, col, col, vec, vec],
        out_shape=[jax.ShapeDtypeStruct((t, d), F32), jax.ShapeDtypeStruct((t, d), BF16), ff, ff, ff,
                   jax.ShapeDtypeStruct((1, d), F32), jax.ShapeDtypeStruct((1, d), F32)],
        scratch_shapes=[pltpu.VMEM((tm, d), BF16), pltpu.VMEM((tm, d), F32), pltpu.VMEM((tm, d), BF16)],
        compiler_params=_seq(2),
    )(dh, xhat, rstd, ln_g, gb, ub, wg, wu, wd)


def _wgrad(a, b, tmm, tn, name, after=None):
    t = a.shape[-2]
    m = a.shape[-1] * (a.shape[0] if a.ndim == 3 else 1)
    n = b.shape[-1] * (b.shape[0] if b.ndim == 3 else 1)
    tk = max(k for k in range(BF16_ROWS, WGRAD_K_MAX + 1, BF16_ROWS) if t % k == 0)
    nk = t // tk
    extra = [] if after is None else [after]

    def body(a_ref, b_ref, *rest):
        o_ref, acc = rest[len(extra):]
        k = pl.program_id(2)

        @pl.when(k == 0)
        def _():
            acc[...] = jnp.zeros_like(acc)

        acc[...] += _mm_tn(a_ref[...], b_ref[...])

        @pl.when(k == nk - 1)
        def _():
            o_ref[...] = acc[...].astype(o_ref.dtype)

    a_spec = (pl.BlockSpec((None, tk, tmm), lambda i, j, k: (i, k, 0)) if a.ndim == 3
              else pl.BlockSpec((tk, tmm), lambda i, j, k: (k, i)))
    return _tc_call(
        body, name=name, grid=(m // tmm, n // tn, nk),
        in_specs=[a_spec, pl.BlockSpec((None, tk, tn), lambda i, j, k: (j, k, 0)) if b.ndim == 3
                  else pl.BlockSpec((tk, tn), lambda i, j, k: (k, j))] + [pl.BlockSpec(memory_space=pl.ANY)] * len(extra),
        out_specs=pl.BlockSpec((tmm, tn), lambda i, j, k: (i, j)),
        out_shape=jax.ShapeDtypeStruct((m, n), BF16),
        scratch_shapes=[pltpu.VMEM((tmm, tn), F32)],
        compiler_params=_seq(3),
    )(a, b, *extra)


def _inproj_fwd(xhat, gam, bet, w_main, w_lr, wgu, b_gate, widths, name):
    t, d = xhat.shape
    tm = _row_tile(t)
    kw = wgu.shape[1]
    offs = [0]
    for w in widths:
        offs.append(offs[-1] + w)

    def body(x_ref, g_ref, b_ref, wm_ref, wl_ref, wgu_ref, bg_ref, *outs):
        piece_refs, (zg_ref, la_ref, hb_ref) = outs[:len(widths)], outs[len(widths):]
        hb = (x_ref[...] * g_ref[...] + b_ref[...]).astype(BF16)
        hb_ref[...] = hb
        for p, ref in enumerate(piece_refs):
            ref[...] = _mm_nt(hb, wm_ref[offs[p]:offs[p + 1], :])
        zg = _mm_nt(hb, wl_ref[...])
        zg_ref[...] = zg
        logit = _mm(zg, wgu_ref[...]) + bg_ref[...]
        la_ref[...] = (jnp.minimum(logit, 0.0) - jnp.log(1.0 + jnp.exp(-jnp.abs(logit)))) * (1.0 / GLA_GATE_TEMP)

    row = lambda i: (i, 0)
    full = lambda a: pl.BlockSpec(a.shape, lambda i: (0,) * a.ndim)
    out_w = list(widths) + [LANE, kw]
    return _tc_call(
        body, name=name, grid=(t // tm,),
        in_specs=[pl.BlockSpec((tm, d), row), full(gam), full(bet), full(w_main), full(w_lr), full(wgu), full(b_gate)],
        out_specs=[pl.BlockSpec((tm, w), row) for w in out_w] + [pl.BlockSpec((tm, d), row)],
        out_shape=[jax.ShapeDtypeStruct((t, w), F32) for w in out_w] + [jax.ShapeDtypeStruct((t, d), BF16)],
        compiler_params=_seq(1),
    )(xhat, gam, bet, w_main, w_lr, wgu, b_gate)


def _inproj_bwd(dh_part, pieces, dzg, w_main, w_lr, name):
    t, d = dh_part.shape
    tm = _row_tile(t)
    widths = [p.shape[1] for p in pieces]
    offs = [0]
    for w in widths:
        offs.append(offs[-1] + w)

    def body(*refs):
        dhp_ref = refs[0]
        p_refs = refs[1:1 + len(widths)]
        dzg_ref, wm_ref, wl_ref, dh_ref, dz_ref = refs[1 + len(widths):]
        acc = dhp_ref[...] + _mm(dzg_ref[...], wl_ref[...])
        for p, ref in enumerate(p_refs):
            v = ref[...]
            dz_ref[:, offs[p]:offs[p + 1]] = v
            acc += _mm(v, wm_ref[offs[p]:offs[p + 1], :])
        dh_ref[...] = acc

    row = lambda i: (i, 0)
    full = lambda a: pl.BlockSpec(a.shape, lambda i: (0,) * a.ndim)
    return _tc_call(
        body, name=name, grid=(t // tm,),
        in_specs=[pl.BlockSpec((tm, d), row)] + [pl.BlockSpec((tm, w), row) for w in widths]
                 + [pl.BlockSpec((tm, LANE), row), full(w_main), full(w_lr)],
        out_specs=[pl.BlockSpec((tm, d), row), pl.BlockSpec((tm, offs[-1]), row)],
        out_shape=[jax.ShapeDtypeStruct((t, d), F32), jax.ShapeDtypeStruct((t, offs[-1]), BF16)],
        compiler_params=_seq(1),
    )(dh_part, *pieces, dzg, w_main, w_lr)


def _pool_cnt(tile, tm, w):
    t = tile * tm + lax.broadcasted_iota(jnp.int32, (tm, 1), 0)
    return jnp.minimum(t + 1, w).astype(F32)


def _pool_fwd(u, wp, scale, name):
    t, pw = u.shape
    tm = _row_tile(t)
    gd = wp.shape[1]

    def body(u_ref, wp_ref, sc_ref, y_ref, p_ref, ext):
        i = pl.program_id(0)

        @pl.when(i == 0)
        def _():
            ext[0:POOL_HALO, :] = jnp.zeros((POOL_HALO, pw), F32)

        ext[POOL_HALO:POOL_HALO + tm, :] = u_ref[...]
        for gi, w in enumerate(POOL_WINDOWS):
            cols = slice(gi * gd, (gi + 1) * gd)
            s = ext[pl.ds(POOL_HALO, tm), cols]
            tot = s
            for back in range(1, w):
                tot = tot + ext[pl.ds(POOL_HALO - back, tm), cols]
            p = (tot / _pool_cnt(i, tm, w) - s).astype(BF16)
            p_ref[:, cols] = p
            y_ref[:, cols] = (jnp.dot(p, wp_ref[gi], preferred_element_type=F32) * sc_ref[:, cols]).astype(BF16)
        ext[0:POOL_HALO, :] = ext[tm:tm + POOL_HALO, :]

    row = lambda i: (i, 0)
    return _tc_call(
        body, name=name, grid=(t // tm,),
        in_specs=[pl.BlockSpec((tm, pw), row), pl.BlockSpec(wp.shape, lambda i: (0, 0, 0)),
                  pl.BlockSpec((1, pw), lambda i: (0, 0))],
        out_specs=[pl.BlockSpec((tm, pw), row), pl.BlockSpec((tm, pw), row)],
        out_shape=[jax.ShapeDtypeStruct((t, pw), BF16), jax.ShapeDtypeStruct((t, pw), BF16)],
        scratch_shapes=[pltpu.VMEM((tm + POOL_HALO, pw), F32)],
        compiler_params=_seq(1),
    )(u, wp, scale)


def _pool_bwd(dy, pb, wp, scale, name):
    t, pw = dy.shape
    tm = _row_tile(t)
    nt = t // tm
    gd = wp.shape[1]

    def body(dy_ref, p_ref, wp_ref, sc_ref, du_ref, dwp_ref, dsc_ref, ext):
        i = pl.program_id(0)
        tile = nt - 1 - i

        @pl.when(i == 0)
        def _():
            ext[tm:tm + POOL_HALO, :] = jnp.zeros((POOL_HALO, pw), F32)
            dwp_ref[...] = jnp.zeros_like(dwp_ref)
            dsc_ref[...] = jnp.zeros_like(dsc_ref)

        dps = []
        for gi, w in enumerate(POOL_WINDOWS):
            cols = slice(gi * gd, (gi + 1) * gd)
            dyv = dy_ref[:, cols]
            p = p_ref[:, cols]
            dpre = (dyv * sc_ref[:, cols]).astype(BF16)
            dsc_ref[:, cols] += jnp.sum(dyv * jnp.dot(p, wp_ref[gi], preferred_element_type=F32), axis=0, keepdims=True)
            dwp_ref[gi] += _mm_tn(p, dpre)
            dp = _mm_nt(dpre, wp_ref[gi])
            dps.append(dp)
            ext[0:tm, cols] = dp / _pool_cnt(tile, tm, w)
        for gi, w in enumerate(POOL_WINDOWS):
            cols = slice(gi * gd, (gi + 1) * gd)
            tot = ext[pl.ds(0, tm), cols]
            for fwd in range(1, w):
                tot = tot + ext[pl.ds(fwd, tm), cols]
            du_ref[:, cols] = (tot - dps[gi]).astype(BF16)
        ext[tm:tm + POOL_HALO, :] = ext[0:POOL_HALO, :]

    row = lambda i: (nt - 1 - i, 0)
    return _tc_call(
        body, name=name, grid=(nt,),
        in_specs=[pl.BlockSpec((tm, pw), row), pl.BlockSpec((tm, pw), row),
                  pl.BlockSpec(wp.shape, lambda i: (0, 0, 0)), pl.BlockSpec((1, pw), lambda i: (0, 0))],
        out_specs=[pl.BlockSpec((tm, pw), row), pl.BlockSpec(wp.shape, lambda i: (0, 0, 0)),
                   pl.BlockSpec((1, pw), lambda i: (0, 0))],
        out_shape=[jax.ShapeDtypeStruct((t, pw), BF16), jax.ShapeDtypeStruct(wp.shape, F32),
                   jax.ShapeDtypeStruct((1, pw), F32)],
        scratch_shapes=[pltpu.VMEM((tm + POOL_HALO, pw), F32)],
        compiler_params=_seq(1),
    )(dy, pb, wp, scale)


def _gla_masks(kw, vw):
    dk, dv = kw // N_HEADS, vw // N_HEADS
    lane_k = lax.broadcasted_iota(jnp.int32, (1, kw), 1)
    lane_v = lax.broadcasted_iota(jnp.int32, (1, vw), 1)
    hk = [((lane_k >= h * dk) & (lane_k < (h + 1) * dk)).astype(F32) for h in range(N_HEADS)]
    hv = [((lane_v >= h * dv) & (lane_v < (h + 1) * dv)).astype(F32) for h in range(N_HEADS)]
    r = lax.broadcasted_iota(jnp.int32, (CHUNK, CHUNK), 0)
    c = lax.broadcasted_iota(jnp.int32, (CHUNK, CHUNK), 1)
    tril = r >= c
    rs = lax.broadcasted_iota(jnp.int32, (N_HEADS * CHUNK, CHUNK), 0) & (CHUNK - 1)
    stril = rs >= lax.broadcasted_iota(jnp.int32, (N_HEADS * CHUNK, CHUNK), 1)
    return hk, hv, tril, stril


def _block_diag(x, hk, dv):
    return jnp.concatenate([x[h * dv:(h + 1) * dv, :] * hk[h] for h in range(N_HEADS)], axis=0)


def _gla_fwd(q, k, v, loga, r, gnorm, name):
    t, kw = q.shape
    vw = v.shape[1]
    dk, dv = kw // N_HEADS, vw // N_HEADS
    tm = _row_tile(t)
    nc = tm // CHUNK
    qscale = dk ** -0.5

    def body(q_ref, k_ref, v_ref, la_ref, r_ref, gn_ref, o_ref, y_ref, sall_ref, st):
        @pl.when(pl.program_id(0) == 0)
        def _():
            st[...] = jnp.zeros_like(st)

        hk, hv, tril, stril = _gla_masks(kw, vw)
        trif = tril.astype(F32)

        def chunk(c, carry):
            rows = pl.ds(pl.multiple_of(c * CHUNK, CHUNK), CHUNK)
            la = la_ref[rows, :]
            b = _mm_01(trif, la)
            bl = jnp.sum(la, axis=0, keepdims=True)
            qb = q_ref[rows, :] * (qscale * jnp.exp(b))
            kk = k_ref[rows, :]
            kb = kk * jnp.exp(-b)
            kl = kk * jnp.exp(bl - b)
            vv = v_ref[rows, :]
            s_t = st[...]
            compact = s_t[0:dv, :]
            for h in range(1, N_HEADS):
                compact = compact + s_t[h * dv:(h + 1) * dv, :]
            sall_ref[c] = compact
            qx = jnp.concatenate([qb.astype(BF16) * hk[h].astype(BF16) for h in range(N_HEADS)], axis=0)
            a = jnp.where(stril, _mm_nt(qx, kb), 0.0).astype(BF16)
            o_inter = _mm_nt(qb, s_t)
            for h in range(N_HEADS):
                vs = slice(h * dv, (h + 1) * dv)
                o_ref[rows, vs] = o_inter[:, vs] + _mm(a[h * CHUNK:(h + 1) * CHUNK, :], vv[:, vs])
            st[...] = s_t * jnp.exp(bl) + _block_diag(_mm_tn(vv, kl), hk, dv)
            return carry

        lax.fori_loop(0, nc, chunk, 0, unroll=CHUNK_UNROLL)
        for h in range(N_HEADS):
            vs = slice(h * dv, (h + 1) * dv)
            oh = o_ref[:, vs]
            on = oh * lax.rsqrt(jnp.mean(oh * oh, axis=-1, keepdims=True) + RMS_EPS)
            _, sl = _silu_parts(r_ref[:, vs])
            y_ref[:, vs] = (on * gn_ref[:, vs] * sl).astype(BF16)

    row = lambda i: (i, 0)
    return _tc_call(
        body, name=name, grid=(t // tm,),
        in_specs=[pl.BlockSpec((tm, kw), row), pl.BlockSpec((tm, kw), row), pl.BlockSpec((tm, vw), row),
                  pl.BlockSpec((tm, kw), row), pl.BlockSpec((tm, vw), row), pl.BlockSpec((1, vw), lambda i: (0, 0))],
        out_specs=[pl.BlockSpec((tm, vw), row), pl.BlockSpec((tm, vw), row),
                   pl.BlockSpec((nc, dv, kw), lambda i: (i, 0, 0))],
        out_shape=[jax.ShapeDtypeStruct((t, vw), F32), jax.ShapeDtypeStruct((t, vw), BF16),
                   jax.ShapeDtypeStruct((t // CHUNK, dv, kw), F32)],
        scratch_shapes=[pltpu.VMEM((vw, kw), F32)],
        compiler_params=_seq(1),
    )(q, k, v, loga, r, gnorm)


def _gla_bwd(dy, o, r, gnorm, q, k, v, loga, zg, sall, wgu, name):
    t, kw = q.shape
    vw = v.shape[1]
    dk, dv = kw // N_HEADS, vw // N_HEADS
    tm = _row_tile(t)
    nt = t // tm
    nc = tm // CHUNK
    qscale = dk ** -0.5

    def body(dy_ref, o_ref, r_ref, gn_ref, q_ref, k_ref, v_ref, la_ref, zg_ref, sall_ref, wgu_ref,
             dq_ref, dk_ref, dv_ref, dr_ref, dzg_ref, dwgu_ref, dbg_ref, dgn_ref, dst, do_s):
        @pl.when(pl.program_id(0) == 0)
        def _():
            dst[...] = jnp.zeros_like(dst)
            dwgu_ref[...] = jnp.zeros_like(dwgu_ref)
            dbg_ref[...] = jnp.zeros_like(dbg_ref)
            dgn_ref[...] = jnp.zeros_like(dgn_ref)

        for h in range(N_HEADS):
            vs = slice(h * dv, (h + 1) * dv)
            oh = o_ref[:, vs]
            rinv = lax.rsqrt(jnp.mean(oh * oh, axis=-1, keepdims=True) + RMS_EPS)
            on = oh * rinv
            rr = r_ref[:, vs]
            sg, sl = _silu_parts(rr)
            dyv = dy_ref[:, vs]
            gn = gn_ref[:, vs]
            dgn_ref[:, vs] += jnp.sum(dyv * on * sl, axis=0, keepdims=True)
            dr_ref[:, vs] = (dyv * on * gn * (sg * (1.0 + rr * (1.0 - sg)))).astype(BF16)
            don = dyv * gn * sl
            do_s[:, vs] = rinv * (don - on * jnp.mean(don * on, axis=-1, keepdims=True))

        hk, hv, tril, stril = _gla_masks(kw, vw)
        trif = tril.astype(F32)
        triuf = (lax.broadcasted_iota(jnp.int32, (CHUNK, CHUNK), 0)
                 <= lax.broadcasted_iota(jnp.int32, (CHUNK, CHUNK), 1)).astype(F32)
        last_row = lax.broadcasted_iota(jnp.int32, (CHUNK, 1), 0) == CHUNK - 1

        def chunk(idx, carry):
            c = nc - 1 - idx
            rows = pl.ds(pl.multiple_of(c * CHUNK, CHUNK), CHUNK)
            la = la_ref[rows, :]
            b = _mm_01(trif, la)
            bl = jnp.sum(la, axis=0, keepdims=True)
            eb = jnp.exp(b)
            enb = jnp.exp(-b)
            ebl = jnp.exp(bl - b)
            el = jnp.exp(bl)
            qb = q_ref[rows, :] * (qscale * eb)
            kk = k_ref[rows, :]
            kb = kk * enb
            kl = kk * ebl
            vv = v_ref[rows, :]
            do = do_s[rows, :]
            compact = sall_ref[c]
            s_t = jnp.concatenate([compact * hk[h] for h in range(N_HEADS)], axis=0)
            ds_t = dst[...]
            qx = jnp.concatenate([qb.astype(BF16) * hk[h].astype(BF16) for h in range(N_HEADS)], axis=0)
            dox = jnp.concatenate([do.astype(BF16) * hv[h].astype(BF16) for h in range(N_HEADS)], axis=0)
            a = jnp.where(stril, _mm_nt(qx, kb), 0.0).astype(BF16)
            da = jnp.where(stril, _mm_nt(dox, vv), 0.0).astype(BF16)
            dv_ref[rows, :] = (_mm_tn(a, dox) + _mm_nt(kl, ds_t)).astype(BF16)
            dak = _mm(da, kb)
            dqb = _mm(do, s_t)
            for h in range(N_HEADS):
                dqb = dqb + dak[h * CHUNK:(h + 1) * CHUNK, :] * hk[h]
            dkb = _mm_tn(da, qx)
            dkl = _mm(vv, ds_t)
            dbl = jnp.sum(dkl * kl, axis=0, keepdims=True) + el * jnp.sum(ds_t * s_t, axis=0, keepdims=True)
            dst[...] = ds_t * el + _block_diag(_mm_tn(do, qb), hk, dv)
            dq_ref[rows, :] = (dqb * (qscale * eb)).astype(BF16)
            dk_ref[rows, :] = (dkb * enb + dkl * ebl).astype(BF16)
            db = dqb * qb - dkb * kb - dkl * kl + jnp.where(last_row, dbl, 0.0)
            dla = _mm_01(triuf, db)
            dlogit = dla * (1.0 / GLA_GATE_TEMP) * (1.0 - jnp.exp(GLA_GATE_TEMP * la))
            dzg_ref[rows, :] = _mm_nt(dlogit, wgu_ref[...]).astype(BF16)
            dwgu_ref[...] += _mm_tn(zg_ref[rows, :], dlogit)
            dbg_ref[...] += jnp.sum(dlogit, axis=0, keepdims=True)
            return carry

        lax.fori_loop(0, nc, chunk, 0, unroll=CHUNK_UNROLL)

    row = lambda i: (nt - 1 - i, 0)
    const = lambda i: (0, 0)
    return _tc_call(
        body, name=name, grid=(nt,),
        in_specs=[pl.BlockSpec((tm, vw), row), pl.BlockSpec((tm, vw), row), pl.BlockSpec((tm, vw), row),
                  pl.BlockSpec((1, vw), const), pl.BlockSpec((tm, kw), row), pl.BlockSpec((tm, kw), row),
                  pl.BlockSpec((tm, vw), row), pl.BlockSpec((tm, kw), row), pl.BlockSpec((tm, LANE), row),
                  pl.BlockSpec((nc, dv, kw), lambda i: (nt - 1 - i, 0, 0)), pl.BlockSpec((LANE, kw), const)],
        out_specs=[pl.BlockSpec((tm, kw), row), pl.BlockSpec((tm, kw), row), pl.BlockSpec((tm, vw), row),
                   pl.BlockSpec((tm, vw), row), pl.BlockSpec((tm, LANE), row), pl.BlockSpec((LANE, kw), const),
                   pl.BlockSpec((1, kw), const), pl.BlockSpec((1, vw), const)],
        out_shape=[jax.ShapeDtypeStruct((t, kw), BF16), jax.ShapeDtypeStruct((t, kw), BF16),
                   jax.ShapeDtypeStruct((t, vw), BF16), jax.ShapeDtypeStruct((t, vw), BF16),
                   jax.ShapeDtypeStruct((t, LANE), BF16), jax.ShapeDtypeStruct((LANE, kw), F32),
                   jax.ShapeDtypeStruct((1, kw), F32), jax.ShapeDtypeStruct((1, vw), F32)],
        scratch_shapes=[pltpu.VMEM((vw, kw), F32), pltpu.VMEM((tm, vw), F32)],
        compiler_params=_seq(1),
    )(dy, o, r, gnorm, q, k, v, loga, zg, sall, wgu)


def _outproj_fwd(yp, yg, w_out, xhat, gam, bet, alpha, name):
    t, d = xhat.shape
    pw = yp.shape[1]
    tm = _row_tile(t)

    def body(yp_ref, yg_ref, w_ref, x_ref, g_ref, b_ref, xhat_ref, rstd_ref):
        for part in range(ROW_GROUPS):
            rows = pl.ds(part * (tm // ROW_GROUPS), tm // ROW_GROUPS)
            h = x_ref[rows, :] * g_ref[...] + b_ref[...]
            y = (jnp.dot(yp_ref[rows, :], w_ref[0:pw, :], preferred_element_type=F32)
                 + jnp.dot(yg_ref[rows, :], w_ref[pw:, :], preferred_element_type=F32))
            xh, rs = _ln_stats(alpha * h + y)
            xhat_ref[rows, :] = xh
            rstd_ref[rows, :] = rs

    row = lambda i: (i, 0)
    vec = pl.BlockSpec((1, d), lambda i: (0, 0))
    return _tc_call(
        body, name=name, grid=(t // tm,),
        in_specs=[pl.BlockSpec((tm, pw), row), pl.BlockSpec((tm, yg.shape[1]), row),
                  pl.BlockSpec(w_out.shape, lambda i: (0, 0)), pl.BlockSpec((tm, d), row), vec, vec],
        out_specs=[pl.BlockSpec((tm, d), row), pl.BlockSpec((tm, 1), row)],
        out_shape=[jax.ShapeDtypeStruct((t, d), F32), jax.ShapeDtypeStruct((t, 1), F32)],
        compiler_params=_seq(1),
    )(yp, yg, w_out, xhat, gam, bet)


def _outproj_bwd(dh, xhat, rstd, ln_g, w_out, pw, alpha, name):
    t, d = dh.shape
    tm = _row_tile(t)
    gw = w_out.shape[0] - pw

    def body(dh_ref, xh_ref, rs_ref, g_ref, w_ref, dyb_ref, dyp_ref, dyg_ref, dres_ref, dgam_ref, dbet_ref):
        @pl.when(pl.program_id(0) == 0)
        def _():
            dgam_ref[...] = jnp.zeros_like(dgam_ref)
            dbet_ref[...] = jnp.zeros_like(dbet_ref)

        for part in range(ROW_GROUPS):
            rows = pl.ds(part * (tm // ROW_GROUPS), tm // ROW_GROUPS)
            dy = dh_ref[rows, :]
            xh = xh_ref[rows, :]
            dr = _ln_bwd(dy, xh, rs_ref[rows, :], g_ref[...])
            dgam_ref[...] += jnp.sum(dy * xh, axis=0, keepdims=True)
            dbet_ref[...] += jnp.sum(dy, axis=0, keepdims=True)
            drb = dr.astype(BF16)
            dyb_ref[rows, :] = drb
            dres_ref[rows, :] = alpha * dr
            dyp_ref[rows, :] = _mm_nt(drb, w_ref[0:pw, :])
            dyg_ref[rows, :] = _mm_nt(drb, w_ref[pw:, :])

    row = lambda i: (i, 0)
    vec = pl.BlockSpec((1, d), lambda i: (0, 0))
    return _tc_call(
        body, name=name, grid=(t // tm,),
        in_specs=[pl.BlockSpec((tm, d), row), pl.BlockSpec((tm, d), row), pl.BlockSpec((tm, 1), row), vec,
                  pl.BlockSpec(w_out.shape, lambda i: (0, 0))],
        out_specs=[pl.BlockSpec((tm, d), row), pl.BlockSpec((tm, pw), row), pl.BlockSpec((tm, gw), row),
                   pl.BlockSpec((tm, d), row), vec, vec],
        out_shape=[jax.ShapeDtypeStruct((t, d), BF16), jax.ShapeDtypeStruct((t, pw), F32),
                   jax.ShapeDtypeStruct((t, gw), F32), jax.ShapeDtypeStruct((t, d), F32),
                   jax.ShapeDtypeStruct((1, d), F32), jax.ShapeDtypeStruct((1, d), F32)],
        compiler_params=_seq(1),
    )(dh, xhat, rstd, ln_g, w_out)


def _loss_head(xhat, gam, bet, target, n_rows, name):
    t, d = xhat.shape
    tm = _row_tile(t)

    def body(x_ref, g_ref, b_ref, t_ref, dy_ref, loss_ref):
        i = pl.program_id(0)

        @pl.when(i == 0)
        def _():
            loss_ref[...] = jnp.zeros_like(loss_ref)

        rowi = i * tm + lax.broadcasted_iota(jnp.int32, (tm, 1), 0)
        live = (rowi >= N_META) & (rowi < N_META + n_rows)
        diff = jnp.where(live, x_ref[...] * g_ref[...] + b_ref[...] - t_ref[...], 0.0)
        dy_ref[...] = diff * (1.0 / d)
        loss_ref[...] += jnp.sum(diff * diff) * (0.5 / d)

    row = lambda i: (i, 0)
    vec = pl.BlockSpec((1, d), lambda i: (0, 0))
    return _tc_call(
        body, name=name, grid=(t // tm,),
        in_specs=[pl.BlockSpec((tm, d), row), vec, vec, pl.BlockSpec((tm, d), row)],
        out_specs=[pl.BlockSpec((tm, d), row), pl.BlockSpec((8, LANE), lambda i: (0, 0))],
        out_shape=[jax.ShapeDtypeStruct((t, d), F32), jax.ShapeDtypeStruct((8, LANE), F32)],
        compiler_params=_seq(1),
    )(xhat, gam, bet, target)


def _rows_block(r, c):
    best = r
    for cand in range(BF16_ROWS, r, BF16_ROWS):
        if r % cand == 0 and cand * c * 4 <= (1 << 20):
            best = cand
    return best if best * c * 4 <= (4 << 20) else r


def _sum_slots(me, mine, recvs, name, layers_side_by_side=False):
    nl = len(recvs)
    ns, r, c = recvs[0].shape
    tr = _rows_block(r, c)

    def body(me_ref, *refs):
        o_ref = refs[nl * ns]
        for l in range(nl):
            acc = refs[l * ns][...].astype(F32)
            for s in range(1, ns):
                acc = acc + refs[l * ns + s][...].astype(F32)
            if layers_side_by_side:
                o_ref[0, :, l * c:(l + 1) * c] = acc.astype(o_ref.dtype)
            else:
                o_ref[l] = acc.astype(o_ref.dtype)

    def slot(s):
        return pl.BlockSpec((None, tr, c), lambda i, me_ref: ((me_ref[0] + s) % ns, i, 0))

    out = (1, r, nl * c) if layers_side_by_side else (nl, r, c)
    operands = []
    for l in range(nl):
        operands += [mine[l]] + [recvs[l]] * (ns - 1)
    return _tc_call(
        body, name=name,
        grid_spec=pltpu.PrefetchScalarGridSpec(
            num_scalar_prefetch=1, grid=(r // tr,),
            in_specs=[slot(s) for s in range(ns)] * nl,
            out_specs=pl.BlockSpec((out[0], tr, out[2]), lambda i, me_ref: (0, i, 0))),
        out_shape=jax.ShapeDtypeStruct(out, recvs[0].dtype),
        compiler_params=_seq(1),
    )(me, *operands)


def _adamw(w, terms, m, v, name):
    nl, r, c = w.shape
    tc = c
    while tc % (2 * LANE) == 0 and tc > 4 * LANE:
        tc //= 2
    tr = _rows_block(r, tc)
    nterm = len(terms)

    def body(*refs):
        w_ref = refs[0]
        t_refs = refs[1:1 + nterm]
        m_ref, v_ref, g_ref, d_ref, nm_ref, nv_ref = refs[1 + nterm:]
        g = t_refs[0][...].astype(F32)
        for tr_ in t_refs[1:]:
            g = g + tr_[...].astype(F32)
        nm = ADAM_B1 * m_ref[...] + (1.0 - ADAM_B1) * g
        nv = ADAM_B2 * v_ref[...] + (1.0 - ADAM_B2) * jnp.square(g)
        m_hat = nm / (1.0 - ADAM_B1 ** ADAM_STEP)
        v_hat = nv / (1.0 - ADAM_B2 ** ADAM_STEP)
        g_ref[...] = g
        d_ref[...] = -ADAM_LR * (m_hat / (jnp.sqrt(v_hat) + ADAM_EPS) + ADAM_WD * w_ref[...])
        nm_ref[...] = nm
        nv_ref[...] = nv

    spec = pl.BlockSpec((None, tr, tc), lambda l, i, j: (l, i, j))
    shp = jax.ShapeDtypeStruct((nl, r, c), F32)
    return _tc_call(
        body, name=name, grid=(nl, r // tr, c // tc),
        in_specs=[spec] * (3 + nterm), out_specs=[spec] * 4, out_shape=[shp] * 4,
        compiler_params=_seq(3),
    )(w, *terms, m, v)


XY_RELATIONS = ((1, 0, 0), (0, 1, 0), (1, 1, 0))
ALL_RELATIONS = tuple((fx, fy, fc) for fx in (0, 1) for fy in (0, 1) for fc in (0, 1) if fx or fy or fc)
HBM_SPEC = pl.BlockSpec(memory_space=pltpu.HBM)
SEM_SPEC = pl.BlockSpec(memory_space=pltpu.SEMAPHORE)
DATAFLOW = pltpu.SideEffectType.DATAFLOW_SIDE_EFFECTING


def _split_call(body, **kw):
    return pl.pallas_call(body, **kw)


def _flip(v, f):
    return 1 - v if f else v


def _any_spec(n):
    return [pl.BlockSpec(memory_space=pl.ANY)] * n


def _relations(kind):
    return ALL_RELATIONS if kind == "bcast" else XY_RELATIONS


def _copies(kind, arr, land, sems):
    x, y, c = lax.axis_index("x"), lax.axis_index("y"), lax.axis_index("c")
    out = []
    for (fx, fy, fc), (send_sem, recv_sem) in zip(_relations(kind), sems):
        px, py, pc = _flip(x, fx), _flip(y, fy), _flip(c, fc)
        if kind == "bcast":
            mine, theirs = 4 * x + 2 * y + c, 4 * px + 2 * py + pc
        else:
            mine, theirs = 2 * x + y, 2 * px + py
        src, to_mine, to_theirs = arr, land.at[mine], land.at[theirs]
        if kind == "scatter":
            src = arr.at[theirs]
        if kind == "gather_half":
            rows = _my_half(arr.shape[0], c)
            src, to_mine, to_theirs = arr.at[rows], land.at[mine, rows], land.at[theirs, rows]
        both = dict(src_ref=src, send_sem=send_sem, recv_sem=recv_sem, device_id=(px, py, pc), device_id_type=MESH)
        out.append((pltpu.make_async_remote_copy(dst_ref=to_mine, **both),
                    pltpu.make_async_remote_copy(dst_ref=to_theirs, **both)))
    return out


def _my_half(nrows, c):
    return pl.ds(c * (nrows // 2), nrows // 2)


def _share_halves(name, kinds, lands):
    ks = [k for k, kd in enumerate(kinds) if kd == "gather_half"]
    n = len(ks)

    def body(*refs):
        l_refs = refs[n:2 * n]
        send_sems, recv_sems = refs[2 * n:]
        x, y, c = lax.axis_index("x"), lax.axis_index("y"), lax.axis_index("c")
        copies = []
        for i in range(n):
            nrows = l_refs[i].shape[1]
            for r, (fx, fy, _) in enumerate(XY_RELATIONS):
                slot = 2 * _flip(x, fx) + _flip(y, fy)
                both = dict(src_ref=l_refs[i].at[slot, _my_half(nrows, c)], send_sem=send_sems.at[i, r],
                            recv_sem=recv_sems.at[i, r], device_id=(x, y, 1 - c), device_id_type=MESH)
                copies.append((pltpu.make_async_remote_copy(dst_ref=l_refs[i].at[slot, _my_half(nrows, c)], **both),
                               pltpu.make_async_remote_copy(dst_ref=l_refs[i].at[slot, _my_half(nrows, 1 - c)], **both)))
        for send, _ in copies:
            send.start()
        for _, arrival in copies:
            arrival.wait_recv()
        for send, _ in copies:
            send.wait_send()

    outs = _comm_call(
        body, name=name,
        in_specs=_any_spec(n), out_specs=_any_spec(n),
        out_shape=[jax.ShapeDtypeStruct(lands[k].shape, lands[k].dtype) for k in ks],
        input_output_aliases={i: i for i in range(n)},
        scratch_shapes=[pltpu.SemaphoreType.DMA((n, 3)), pltpu.SemaphoreType.DMA((n, 3))],
    )(*[lands[k] for k in ks])
    lands = list(lands)
    for k, o in zip(ks, outs):
        lands[k] = o
    return lands


def _sem_pairs(kinds, sems):
    out, at = [], 0
    for kind in kinds:
        nrel = len(_relations(kind))
        out.append([(sems[at + 2 * r], sems[at + 2 * r + 1]) for r in range(nrel)])
        at += 2 * nrel
    return out


def _exchange_start(name, kinds, arrs, lands):
    n = len(arrs)
    nsem = sum(2 * len(_relations(kd)) for kd in kinds)

    def body(*refs):
        a_refs, l_refs = refs[:n], refs[n:2 * n]
        pairs = _sem_pairs(kinds, refs[2 * n:2 * n + nsem])
        token = refs[-1]
        for k in range(n):
            for send, _ in _copies(kinds[k], a_refs[k], l_refs[k], pairs[k]):
                send.start()
        token[...] = jnp.zeros_like(token)

    thru = [pltpu.HBM(a.shape, a.dtype) for a in list(arrs) + list(lands)]
    outs = _split_call(
        body, name=name,
        out_shape=(*[pltpu.SemaphoreType.DMA(())] * nsem, *thru, jax.ShapeDtypeStruct((8, LANE), F32)),
        in_specs=[HBM_SPEC] * (2 * n),
        out_specs=(*[SEM_SPEC] * nsem, *[HBM_SPEC] * (2 * n), pl.BlockSpec(memory_space=pltpu.VMEM)),
        input_output_aliases={i: nsem + i for i in range(2 * n)},
        compiler_params=pltpu.CompilerParams(has_side_effects=DATAFLOW),
    )(*[pltpu.with_memory_space_constraint(a, pltpu.HBM) for a in list(arrs) + list(lands)])
    return dict(kinds=kinds, sems=outs[:nsem], arrs=outs[nsem:nsem + n], lands=outs[nsem + n:nsem + 2 * n],
                token=outs[-1])


def _exchange_wait(name, st, after):
    kinds = st["kinds"]
    n = len(kinds)
    nsem = len(st["sems"])

    def body(*refs):
        a_refs, l_refs = refs[:n], refs[n:2 * n]
        pairs = _sem_pairs(kinds, refs[2 * n:2 * n + nsem])
        for k in range(n):
            for _, arrival in _copies(kinds[k], a_refs[k], l_refs[k], pairs[k]):
                arrival.wait_send()
                arrival.wait_recv()
        refs[-1][...] = jnp.zeros_like(refs[-1])

    ins = list(st["arrs"]) + list(st["lands"])
    outs = _split_call(
        body, name=name,
        out_shape=[pltpu.HBM(a.shape, a.dtype) for a in ins] + [jax.ShapeDtypeStruct((8, LANE), F32)],
        in_specs=[HBM_SPEC] * (2 * n) + [SEM_SPEC] * nsem + [pl.BlockSpec(memory_space=pl.ANY)],
        out_specs=[HBM_SPEC] * (2 * n) + [pl.BlockSpec(memory_space=pltpu.VMEM)],
        input_output_aliases={i: i for i in range(2 * n)},
        compiler_params=pltpu.CompilerParams(has_side_effects=DATAFLOW),
    )(*ins, *st["sems"], after)
    return outs[:n], outs[n:2 * n], outs[-1]


def _landing(own, slot, nslot):
    return lax.dynamic_update_slice(lax.empty((nslot,) + own.shape, own.dtype), own[None], (slot,) + (0,) * own.ndim)


def _swap_sibling(parts, name):
    n = len(parts)

    def body(*refs):
        ins, outs = refs[:n], refs[n:2 * n]
        send_sems, recv_sems = refs[2 * n:]
        sib = (lax.axis_index("x"), lax.axis_index("y"), 1 - lax.axis_index("c"))
        cps = [pltpu.make_async_remote_copy(src_ref=ins[k], dst_ref=outs[k], send_sem=send_sems.at[k],
                                            recv_sem=recv_sems.at[k], device_id=sib, device_id_type=MESH)
               for k in range(n)]
        for cp in cps:
            cp.start()
        for cp in cps:
            cp.wait_recv()
        for cp in cps:
            cp.wait_send()

    return _comm_call(
        body, name=name,
        in_specs=_any_spec(n), out_specs=_any_spec(n),
        out_shape=[jax.ShapeDtypeStruct(p.shape, p.dtype) for p in parts],
        scratch_shapes=[pltpu.SemaphoreType.DMA((n,)), pltpu.SemaphoreType.DMA((n,))],
    )(*parts)


def _col_shards(a, n=N_SHARD):
    r, c = a.shape
    return a.reshape(r, n, c // n).transpose(1, 0, 2)


def _from_col_shards(a):
    n, r, cs = a.shape
    return a.transpose(1, 0, 2).reshape(r, n * cs)


def kernel(x, meta_tokens, ffn1_w_gate, ffn1_w_up, ffn1_w_down, ln1_g, ln1_b, w_in, w_gate_up, b_gate, w_pool, pool_scale, gla_norm_g, w_out, ln2_g, ln2_b, ffn2_w_gate, ffn2_w_up, ffn2_w_down, ln3_g, ln3_b, loss_target, m_meta_tokens, m_ffn1_w_gate, m_ffn1_w_up, m_ffn1_w_down, m_ln1_g, m_ln1_b, m_w_in, m_w_gate_up, m_b_gate, m_w_pool, m_pool_scale, m_gla_norm_g, m_w_out, m_ln2_g, m_ln2_b, m_ffn2_w_gate, m_ffn2_w_up, m_ffn2_w_down, m_ln3_g, m_ln3_b, v_meta_tokens, v_ffn1_w_gate, v_ffn1_w_up, v_ffn1_w_down, v_ln1_g, v_ln1_b, v_w_in, v_w_gate_up, v_b_gate, v_w_pool, v_pool_scale, v_gla_norm_g, v_w_out, v_ln2_g, v_ln2_b, v_ffn2_w_gate, v_ffn2_w_up, v_ffn2_w_down, v_ln3_g, v_ln3_b):
    w = dict(meta_tokens=meta_tokens, ffn1_w_gate=ffn1_w_gate, ffn1_w_up=ffn1_w_up, ffn1_w_down=ffn1_w_down,
             ln1_g=ln1_g, ln1_b=ln1_b, w_in=w_in, w_gate_up=w_gate_up, b_gate=b_gate, w_pool=w_pool,
             pool_scale=pool_scale, gla_norm_g=gla_norm_g, w_out=w_out, ln2_g=ln2_g, ln2_b=ln2_b,
             ffn2_w_gate=ffn2_w_gate, ffn2_w_up=ffn2_w_up, ffn2_w_down=ffn2_w_down, ln3_g=ln3_g, ln3_b=ln3_b)
    mom1 = dict(meta_tokens=m_meta_tokens, ffn1_w_gate=m_ffn1_w_gate, ffn1_w_up=m_ffn1_w_up,
                ffn1_w_down=m_ffn1_w_down, ln1_g=m_ln1_g, ln1_b=m_ln1_b, w_in=m_w_in, w_gate_up=m_w_gate_up,
                b_gate=m_b_gate, w_pool=m_w_pool, pool_scale=m_pool_scale, gla_norm_g=m_gla_norm_g, w_out=m_w_out,
                ln2_g=m_ln2_g, ln2_b=m_ln2_b, ffn2_w_gate=m_ffn2_w_gate, ffn2_w_up=m_ffn2_w_up,
                ffn2_w_down=m_ffn2_w_down, ln3_g=m_ln3_g, ln3_b=m_ln3_b)
    mom2 = dict(meta_tokens=v_meta_tokens, ffn1_w_gate=v_ffn1_w_gate, ffn1_w_up=v_ffn1_w_up,
                ffn1_w_down=v_ffn1_w_down, ln1_g=v_ln1_g, ln1_b=v_ln1_b, w_in=v_w_in, w_gate_up=v_w_gate_up,
                b_gate=v_b_gate, w_pool=v_w_pool, pool_scale=v_pool_scale, gla_norm_g=v_gla_norm_g, w_out=v_w_out,
                ln2_g=v_ln2_g, ln2_b=v_ln2_b, ffn2_w_gate=v_ffn2_w_gate, ffn2_w_up=v_ffn2_w_up,
                ffn2_w_down=v_ffn2_w_down, ln3_g=v_ln3_g, ln3_b=v_ln3_b)

    xs = x[0]
    s_len, d = xs.shape
    nl = ln1_g.shape[0]
    alpha = (2.0 * nl) ** 0.25
    t_real = N_META + s_len
    t_pad = -(-t_real // LANE) * LANE
    pw = pool_scale.shape[1]
    kw = b_gate.shape[1]
    vw = gla_norm_g.shape[1]
    rank = w_gate_up.shape[1]
    widths = (pw, kw, kw, vw, vw)
    n_main = sum(widths)
    dff_s = ffn1_w_gate.shape[2]
    dff_c = N_SHARD * dff_s // FFN_CHUNKS

    me_xy = 2 * lax.axis_index("x") + lax.axis_index("y")
    me_all = 2 * me_xy + lax.axis_index("c")
    ffn1_names = ("ffn1_w_gate", "ffn1_w_up", "ffn1_w_down")
    mix_names = ("w_out", "w_gate_up", "w_in")
    ffn2_names = ("ffn2_w_gate", "ffn2_w_up", "ffn2_w_down")

    gate_up = ("ffn1_w_gate", "ffn1_w_up", "ffn2_w_gate", "ffn2_w_up")

    def stored(n, a):
        if n in gate_up:
            return jnp.swapaxes(a, 1, 2)
        return jnp.transpose(a, (2, 0, 1)) if n == "w_in" else a

    def as_given(n, a):
        if n in gate_up:
            return jnp.swapaxes(a, 1, 2)
        if n == "w_in":
            return jnp.transpose(a.reshape(-1, nl, d), (1, 2, 0))
        return a.reshape(w[n].shape)

    stages = [[("meta_tokens", None)], [(n, 0) for n in ffn1_names], [(n, 0) for n in mix_names + ffn2_names]]
    stages += [[(n, l) for n in BIG] for l in range(1, nl)]
    gathers, wa = {}, {}

    halved = ffn1_names + ffn2_names + ("w_out",)

    def start_gather(si, dep=None):
        own = []
        for n, l in stages[si]:
            a = meta_tokens if l is None else (stored(n, w[n])[:, l] if n == "w_in" else stored(n, w[n])[l])
            a = a if dep is None else a + dep
            own.append(a if l is None else a.astype(BF16))
        gathers[si] = _exchange_start(f"gather_start_{si}", ["gather_half" if n in halved else "gather"
                                                             for n, _ in stages[si]], own,
                                      [_landing(a, me_xy, N_SHARD) for a in own])
        return gathers[si]["token"]

    def arrive(si, after):
        _, lands, token = _exchange_wait(f"gather_wait_{si}", gathers[si], after)
        if any(kd == "gather_half" for kd in gathers[si]["kinds"]):
            lands = _share_halves(f"gather_share_{si}", gathers[si]["kinds"], lands)
        for item, a in zip(stages[si], lands):
            wa[item] = a.reshape(FFN_CHUNKS, -1, d) if item[0] in ffn1_names + ffn2_names else a
        return token

    def mixer_weights(l):
        wi = wa["w_in", l].reshape(-1, d)
        return dict(w_main=wi[:n_main], w_lr=jnp.pad(wi[n_main:], ((0, LANE - rank), (0, 0))),
                    wgu=jnp.pad(_from_col_shards(wa["w_gate_up", l]), ((0, LANE - rank), (0, 0))),
                    wout=wa["w_out", l].reshape(-1, d))

    wp16 = w_pool.astype(BF16)
    ones = jnp.ones((1, d), F32)
    zeros = jnp.zeros((1, d), F32)
    target = jnp.concatenate([jnp.zeros((N_META, d), F32), loss_target[0], jnp.zeros((t_pad - t_real, d), F32)], axis=0)

    started = start_gather(0)
    for si in range(1, len(stages)):
        started = start_gather(si, started[0:1, 0:1])
    arrive(0, started)
    meta_full = _from_col_shards(wa["meta_tokens", None])
    h0 = jnp.concatenate([meta_full, xs, jnp.zeros((t_pad - t_real, d), F32)], axis=0)
    arrive(1, h0[:8, :LANE] + target[:8, :LANE])

    saved, mw = [], []
    cur, cur_g, cur_b = h0, ones, zeros
    for l in range(nl):
        s = {}
        xh1, rs1, hb0, g1, u1 = _ffn_fwd(cur, cur_g, cur_b, wa["ffn1_w_gate", l], wa["ffn1_w_up", l],
                                         wa["ffn1_w_down", l], alpha, f"ffn1_fwd_{l}")
        if l == 0:
            arrive(2, xh1)
        mw.append(mixer_weights(l))
        up, q, k, v, r, zg, la, hb1 = _inproj_fwd(xh1, ln1_g[l:l + 1], ln1_b[l:l + 1], mw[l]["w_main"], mw[l]["w_lr"],
                                                  mw[l]["wgu"], b_gate[l:l + 1], widths, f"inproj_fwd_{l}")
        yp, pb = _pool_fwd(up, wp16[l], pool_scale[l:l + 1], f"pool_fwd_{l}")
        o, yg, sall = _gla_fwd(q, k, v, la, r, gla_norm_g[l:l + 1], f"gla_fwd_{l}")
        xh2, rs2 = _outproj_fwd(yp, yg, mw[l]["wout"], xh1, ln1_g[l:l + 1], ln1_b[l:l + 1], alpha, f"outproj_fwd_{l}")
        if l + 1 < nl:
            arrive(l + 3, xh2)
        xh3, rs3, hb2, g2, u2 = _ffn_fwd(xh2, ln2_g[l:l + 1], ln2_b[l:l + 1], wa["ffn2_w_gate", l], wa["ffn2_w_up", l],
                                         wa["ffn2_w_down", l], alpha, f"ffn2_fwd_{l}")
        s.update(xh1=xh1, rs1=rs1, hb0=hb0, g1=g1, u1=u1, q=q, k=k, v=v, r=r, zg=zg, la=la, hb1=hb1, yp=yp, pb=pb,
                 o=o, yg=yg, sall=sall, xh2=xh2, rs2=rs2, xh3=xh3, rs3=rs3, hb2=hb2, g2=g2, u2=u2)
        saved.append(s)
        cur, cur_g, cur_b = xh3, ln3_g[l:l + 1], ln3_b[l:l + 1]

    dh, loss_acc = _loss_head(cur, cur_g, cur_b, target, s_len, "loss_head")
    loss = lax.psum(loss_acc[0, 0], ("x", "y", "c"))

    small_grads = {n: [None] * nl for n in SMALL}
    scatters = []

    def depart(name, items, grads, kinds=None):
        lands = [_landing(g, me_all, N_DEV) if kd == "bcast" else lax.empty(g.shape, g.dtype)
                 for g, kd in zip(grads, kinds or ["scatter"] * len(grads))]
        st = _exchange_start(name, kinds or ["scatter"] * len(grads), grads, lands)
        scatters.append((name, items, st))
        return st["token"]

    def pack(parts):
        flat = jnp.concatenate([parts[n].reshape(-1) for n in SMALL])
        return flat.reshape(-1, LANE)

    def ffn_wgrad(n, l, hb, dgb, dub, act, dfb, after=None):
        if n.endswith("down"):
            dw = _wgrad(act, dfb, dff_c, d, f"{n}_grad_{l}", after)
        else:
            dw = _wgrad(dgb if n.endswith("gate") else dub, hb, dff_c, d, f"{n}_grad_{l}", after)
        return dw.reshape(N_SHARD, dff_s, d)

    late = []
    for l in reversed(range(nl)):
        s = saved[l]
        dh, dfb, dgb, dub, act, dgam, dbet = _ffn_bwd(dh, s["xh3"], s["rs3"], ln3_g[l:l + 1], s["g2"], s["u2"],
                                                      wa["ffn2_w_gate", l], wa["ffn2_w_up", l], wa["ffn2_w_down", l],
                                                      alpha, f"ffn2_bwd_{l}")
        small_grads["ln3_g"][l], small_grads["ln3_b"][l] = dgam, dbet
        gone = depart(f"scatter_start_ffn2_{l}", [(n, l) for n in ffn2_names],
                      [ffn_wgrad(n, l, s["hb2"], dgb, dub, act, dfb) for n in ffn2_names])

        dyb, dyp, dyg, dres, dgam, dbet = _outproj_bwd(dh, s["xh2"], s["rs2"], ln2_g[l:l + 1] + gone[0:1, 0:1],
                                                       mw[l]["wout"], pw, alpha, f"outproj_bwd_{l}")
        small_grads["ln2_g"][l], small_grads["ln2_b"][l] = dgam, dbet
        dwo = jnp.concatenate([_wgrad(s["yp"], dyb, pw, d, f"dwout_pool_{l}"),
                               _wgrad(s["yg"], dyb, vw, d, f"dwout_gla_{l}")], axis=0)
        dq, dk, dv, dr, dzg, dwgu, dbg, dgn = _gla_bwd(dyg, s["o"], s["r"], gla_norm_g[l:l + 1], s["q"], s["k"],
                                                       s["v"], s["la"], s["zg"], s["sall"], mw[l]["wgu"],
                                                       f"gla_bwd_{l}")
        dup, dwp, dsc = _pool_bwd(dyp, s["pb"], wp16[l], pool_scale[l:l + 1], f"pool_bwd_{l}")
        small_grads["b_gate"][l], small_grads["gla_norm_g"][l] = dbg, dgn
        small_grads["w_pool"][l], small_grads["pool_scale"][l] = dwp, dsc
        dh, dz = _inproj_bwd(dres, [dup, dq, dk, dv, dr], dzg, mw[l]["w_main"], mw[l]["w_lr"], f"inproj_bwd_{l}")
        dwi = jnp.concatenate([_wgrad(dz, s["hb1"], 4 * LANE, d, f"dwin_main_{l}"),
                               _wgrad(dzg, s["hb1"], LANE, d, f"dwin_lr_{l}")[:rank]], axis=0)
        gone = depart(f"scatter_start_mix_{l}", [(n, l) for n in mix_names],
                      [dwo.reshape(N_SHARD, -1, d), _col_shards(dwgu[:rank]), dwi.reshape(N_SHARD, -1, d)])

        dh, dfb, dgb, dub, act, dgam, dbet = _ffn_bwd(dh, s["xh1"], s["rs1"], ln1_g[l:l + 1] + gone[0:1, 0:1],
                                                      s["g1"], s["u1"], wa["ffn1_w_gate", l], wa["ffn1_w_up", l],
                                                      wa["ffn1_w_down", l], alpha, f"ffn1_bwd_{l}")
        small_grads["ln1_g"][l], small_grads["ln1_b"][l] = dgam, dbet
        if l:
            gone = depart(f"scatter_start_ffn1_{l}", [(n, l) for n in ffn1_names],
                          [ffn_wgrad(n, l, s["hb0"], dgb, dub, act, dfb) for n in ffn1_names])
            ln3_g = ln3_g.at[l - 1:l].add(gone[0:1, 0:1])
            continue
        grad_x = dh[N_META:t_real][None]
        small_vec = pack({n: jnp.stack(small_grads[n]) for n in SMALL})
        gone = depart("scatter_start_rest", [("meta_tokens", 0), ("small", 0)],
                      [_col_shards(dh[:N_META]), small_vec], ["scatter", "bcast"])
        for n in ffn1_names:
            g = ffn_wgrad(n, l, s["hb0"], dgb, dub, act, dfb, after=gone)
            gone = depart(f"scatter_start_{n}", [(n, l)], [g])
            late.append(scatters.pop())

    sent, recv, results, firsts = {}, {}, {}, []
    my_slot = me_xy.reshape(1).astype(jnp.int32)

    def collect(group, after):
        for name, items, st in group:
            arrs, lands, _ = _exchange_wait(name.replace("start", "wait"), st, after)
            for item, a, b in zip(items, arrs, lands):
                sent[item], recv[item] = a, b

    def reduce_and_update(names, tag):
        partial = []
        for n in names:
            layers = [(n, l) for l in range(1 if n == "meta_tokens" else nl)]
            partial.append(_sum_slots(my_slot, [sent[it] for it in layers], [recv[it] for it in layers],
                                      f"sum_{n}", n == "w_in"))
        for n, mine, theirs in zip(names, partial, _swap_sibling(partial, f"swap_sibling_{tag}")):
            fit = lambda a: stored(n, a).reshape(mine.shape)
            outs = _adamw(fit(w[n]), [mine, theirs], fit(mom1[n]), fit(mom2[n]), f"adamw_{n}")
            results[n] = [as_given(n, o) for o in outs]
            firsts.append(outs[1][0, 0, 0])

    collect(scatters, gone)
    early = [n for n in ("meta_tokens",) + BIG if n not in ffn1_names]
    reduce_and_update(early, "early")
    small_terms = [recv["small", 0][i][None] for i in range(N_DEV)]
    souts = _adamw(pack(w)[None], small_terms, pack(mom1)[None], pack(mom2)[None], "adamw_small")
    off = 0
    for n in SMALL:
        size = w[n].size
        results[n] = [o.reshape(-1)[off:off + size].reshape(w[n].shape) for o in souts]
        off += size
    collect(late, souts[0][0, :8] + functools.reduce(jnp.add, firsts))
    reduce_and_update(ffn1_names, "late")

    out = [loss, grad_x]
    for part in range(4):
        out += [results[n][part] for n in WEIGHTS]
    return tuple(out)
```

```python
import functools

import jax
import jax.numpy as jnp
from jax import lax
from jax.experimental import pallas as pl
from jax.experimental.pallas import tpu as pltpu

F32 = jnp.float32
BF16 = jnp.bfloat16
MESH = pl.DeviceIdType.MESH

N_META = 16
POOL_WINDOWS = (2, 4, 8, 16)
POOL_HALO = 16
N_HEADS = 4
GLA_GATE_TEMP = 16.0
CHUNK = 128
CHUNK_UNROLL = 5
LN_EPS = 1e-5
RMS_EPS = 1e-6
ADAM_LR = 0.001
ADAM_B1 = 0.9
ADAM_B2 = 0.999
ADAM_EPS = 1e-08
ADAM_WD = 0.01
ADAM_STEP = 10
LANE = 128
BF16_ROWS = 16
ROW_TILE = 640
FFN_ROW_TILE = 640
FFN_CHUNKS = 4
FFN_SPLIT = 2
ROW_GROUPS = 2
WGRAD_K_MAX = 4224
N_SHARD = 4
N_DEV = 8

BIG = ("ffn1_w_gate", "ffn1_w_up", "ffn1_w_down", "w_in", "w_gate_up", "w_out",
       "ffn2_w_gate", "ffn2_w_up", "ffn2_w_down")
SMALL = ("ln1_g", "ln1_b", "b_gate", "w_pool", "pool_scale", "gla_norm_g", "ln2_g", "ln2_b", "ln3_g", "ln3_b")
WEIGHTS = ("meta_tokens", "ffn1_w_gate", "ffn1_w_up", "ffn1_w_down", "ln1_g", "ln1_b", "w_in", "w_gate_up",
           "b_gate", "w_pool", "pool_scale", "gla_norm_g", "w_out", "ln2_g", "ln2_b", "ffn2_w_gate",
           "ffn2_w_up", "ffn2_w_down", "ln3_g", "ln3_b")


def _tc_call(body, **kw):
    return pl.pallas_call(body, **kw)


def _comm_call(body, **kw):
    return pl.pallas_call(body, **kw)


def _seq(n):
    return pltpu.CompilerParams(dimension_semantics=("arbitrary",) * n)


def _mm(a, b):
    return jnp.dot(a.astype(BF16), b.astype(BF16), preferred_element_type=F32)


def _mm_nt(a, b):
    return lax.dot_general(a.astype(BF16), b.astype(BF16), (((1,), (1,)), ((), ())), preferred_element_type=F32)


def _mm_tn(a, b):
    return lax.dot_general(a.astype(BF16), b.astype(BF16), (((0,), (0,)), ((), ())), preferred_element_type=F32)


def _mm_01(a, b):
    hi = b.astype(BF16)
    lo = (b - hi.astype(F32)).astype(BF16)
    a = a.astype(BF16)
    return jnp.dot(a, hi, preferred_element_type=F32) + jnp.dot(a, lo, preferred_element_type=F32)


def _row_tile(t, most=None):
    tm = min(most or ROW_TILE, t)
    while t % tm:
        tm -= LANE
    return tm


def _silu_parts(g):
    sg = jax.nn.sigmoid(g)
    return sg, g * sg


def _ln_stats(r):
    mu = jnp.mean(r, axis=-1, keepdims=True)
    rc = r - mu
    var = jnp.mean(rc * rc, axis=-1, keepdims=True)
    rs = lax.rsqrt(var + LN_EPS)
    return rc * rs, rs


def _ln_bwd(dy, xh, rs, gam):
    dyg = dy * gam
    c1 = jnp.mean(dyg, axis=-1, keepdims=True)
    c2 = jnp.mean(dyg * xh, axis=-1, keepdims=True)
    return rs * (dyg - c1 - xh * c2)


def _ffn_fwd(xin, gam_in, bet_in, wg, wu, wd, alpha, name):
    t, d = xin.shape
    nj, tf, _ = wg.shape
    tm = _row_tile(t, FFN_ROW_TILE)
    nt = t // tm
    share = tm // nj

    def body(x_ref, gi_ref, bi_ref, wg_ref, wu_ref, wd_ref, xhat_ref, rstd_ref, hb_ref, go_ref, uo_ref, acc, hbs):
        i = pl.program_id(0)
        j = pl.program_id(1)
        cur = i % 2

        def norm_previous():
            rows = pl.ds(pl.multiple_of(j * share, BF16_ROWS), share)
            xhat, rs = _ln_stats(0.5 * acc[1 - cur, rows, :])
            xhat_ref[rows, :] = xhat
            rstd_ref[rows, :] = rs

        @pl.when(i < nt)
        def _():
            @pl.when(j == 0)
            def _():
                h = x_ref[...] * gi_ref[...] + bi_ref[...]
                hb = h.astype(BF16)
                hbs[...] = hb
                hb_ref[...] = hb
                acc[cur] = (2.0 * alpha) * h

                @pl.when(i == 0)
                def _():
                    acc[1] = jnp.zeros((tm, d), F32)

            norm_previous()
            hb = hbs[...]
            g = _mm_nt(hb, wg_ref[...])
            u = _mm_nt(hb, wu_ref[...])
            _, sl = _silu_parts(g)
            go_ref[...] = g.astype(BF16)
            uo_ref[...] = u.astype(BF16)
            acc[cur] += jnp.dot((sl * u).astype(BF16), wd_ref[...], preferred_element_type=F32)

        @pl.when(i == nt)
        def _():
            norm_previous()

    here = lambda i, j: (jnp.minimum(i, nt - 1), 0)
    before = lambda i, j: (jnp.maximum(i - 1, 0), 0)
    chunk = lambda i, j: (jnp.where(i < nt, j, nj - 1), 0, 0)
    col = pl.BlockSpec((None, tm, tf), lambda i, j: (jnp.where(i < nt, j, nj - 1), jnp.minimum(i, nt - 1), 0))
    vec = pl.BlockSpec((1, d), lambda i, j: (0, 0))
    return _tc_call(
        body, name=name, grid=(nt + 1, nj),
        in_specs=[pl.BlockSpec((tm, d), here), vec, vec] + [pl.BlockSpec((None, tf, d), chunk)] * 3,
        out_specs=[pl.BlockSpec((tm, d), before), pl.BlockSpec((tm, 1), before), pl.BlockSpec((tm, d), here), col, col],
        out_shape=[jax.ShapeDtypeStruct((t, d), F32), jax.ShapeDtypeStruct((t, 1), F32),
                   jax.ShapeDtypeStruct((t, d), BF16), jax.ShapeDtypeStruct((nj, t, tf), BF16),
                   jax.ShapeDtypeStruct((nj, t, tf), BF16)],
        scratch_shapes=[pltpu.VMEM((2, tm, d), F32), pltpu.VMEM((tm, d), BF16)],
        compiler_params=_seq(2),
    )(xin, gam_in, bet_in, wg, wu, wd)


def _ffn_bwd(dh, xhat, rstd, ln_g, gb, ub, wg, wu, wd, alpha, name):
    t, d = dh.shape
    nj, tf, _ = wg.shape
    tm = _row_tile(t, FFN_ROW_TILE)
    nt = t // tm
    share = tm // nj

    def body(dh_ref, xh_ref, rs_ref, g_ref, gb_ref, ub_ref, wg_ref, wu_ref, wd_ref,
             dhin_ref, df_ref, dg_ref, du_ref, act_ref, dgam_ref, dbet_ref, df_s, dres_next, df_next):
        i = pl.program_id(0)
        j = pl.program_id(1)

        @pl.when((i == 0) & (j == 0))
        def _():
            dgam_ref[...] = jnp.zeros_like(dgam_ref)
            dbet_ref[...] = jnp.zeros_like(dbet_ref)

        def look_ahead():
            rows = pl.ds(pl.multiple_of(j * share, BF16_ROWS), share)
            dy = dh_ref[rows, :]
            xh = xh_ref[rows, :]
            dr = _ln_bwd(dy, xh, rs_ref[rows, :], g_ref[...])
            dres_next[rows, :] = alpha * dr
            df_next[rows, :] = (0.5 * dr).astype(BF16)
            live = jnp.where(i < nt, 1.0, 0.0)
            dgam_ref[...] += live * jnp.sum(dy * xh, axis=0, keepdims=True)
            dbet_ref[...] += live * jnp.sum(dy, axis=0, keepdims=True)

        @pl.when(i == 0)
        def _():
            look_ahead()

        @pl.when(i > 0)
        def _():
            @pl.when(j == 0)
            def _():
                dhin_ref[...] = dres_next[...]
                dfb = df_next[...]
                df_s[...] = dfb
                df_ref[...] = dfb

            look_ahead()
            for part in range(FFN_SPLIT):
                rows = pl.ds(part * (tm // FFN_SPLIT), tm // FFN_SPLIT)
                dact = _mm_nt(df_s[rows, :], wd_ref[...])
                g = gb_ref[rows, :].astype(F32)
                u = ub_ref[rows, :].astype(F32)
                sg, sl = _silu_parts(g)
                dg = (dact * u * (sg * (1.0 + g * (1.0 - sg)))).astype(BF16)
                du = (dact * sl).astype(BF16)
                dg_ref[rows, :] = dg
                du_ref[rows, :] = du
                act_ref[rows, :] = (sl * u).astype(BF16)
                dhin_ref[rows, :] += _mm(dg, wg_ref[...]) + _mm(du, wu_ref[...])

    ahead = lambda i, j: (jnp.minimum(i, nt - 1), 0)
    row = lambda i, j: (jnp.maximum(i - 1, 0), 0)
    chunk = lambda i, j: (jnp.where(i > 0, j, 0), 0, 0)
    col = pl.BlockSpec((None, tm, tf), lambda i, j: (jnp.where(i > 0, j, 0), jnp.maximum(i - 1, 0), 0))
    vec = pl.BlockSpec((1, d), lambda i, j: (0, 0))
    ff = jax.ShapeDtypeStruct((nj, t, tf), BF16)
    return _tc_call(
        body, name=name, grid=(nt + 1, nj),
        in_specs=[pl.BlockSpec((tm, d), ahead), pl.BlockSpec((tm, d), ahead), pl.BlockSpec((tm, 1), ahead), vec,
                  col, col] + [pl.BlockSpec((None, tf, d), chunk)] * 3,
        out_specs=[pl.BlockSpec((tm, d), row), pl.BlockSpec((tm, d), row), col, col, col, vec, vec],
        out_shape=[jax.ShapeDtypeStruct((t, d), F32), jax.ShapeDtypeStruct((t, d), BF16), ff, ff, ff,
                   jax.ShapeDtypeStruct((1, d), F32), jax.ShapeDtypeStruct((1, d), F32)],
        scratch_shapes=[pltpu.VMEM((tm, d), BF16), pltpu.VMEM((tm, d), F32), pltpu.VMEM((tm, d), BF16)],
        compiler_params=_seq(2),
    )(dh, xhat, rstd, ln_g, gb, ub, wg, wu, wd)


def _wgrad(a, b, tmm, tn, name, after=None):
    t = a.shape[-2]
    m = a.shape[-1] * (a.shape[0] if a.ndim == 3 else 1)
    n = b.shape[-1] * (b.shape[0] if b.ndim == 3 else 1)
    tk = max(k for k in range(BF16_ROWS, WGRAD_K_MAX + 1, BF16_ROWS) if t % k == 0)
    nk = t // tk
    extra = [] if after is None else [after]

    def body(a_ref, b_ref, *rest):
        o_ref, acc = rest[len(extra):]
        k = pl.program_id(2)

        @pl.when(k == 0)
        def _():
            acc[...] = jnp.zeros_like(acc)

        acc[...] += _mm_tn(a_ref[...], b_ref[...])

        @pl.when(k == nk - 1)
        def _():
            o_ref[...] = acc[...].astype(o_ref.dtype)

    a_spec = (pl.BlockSpec((None, tk, tmm), lambda i, j, k: (i, k, 0)) if a.ndim == 3
              else pl.BlockSpec((tk, tmm), lambda i, j, k: (k, i)))
    return _tc_call(
        body, name=name, grid=(m // tmm, n // tn, nk),
        in_specs=[a_spec, pl.BlockSpec((None, tk, tn), lambda i, j, k: (j, k, 0)) if b.ndim == 3
                  else pl.BlockSpec((tk, tn), lambda i, j, k: (k, j))] + [pl.BlockSpec(memory_space=pl.ANY)] * len(extra),
        out_specs=pl.BlockSpec((tmm, tn), lambda i, j, k: (i, j)),
        out_shape=jax.ShapeDtypeStruct((m, n), BF16),
        scratch_shapes=[pltpu.VMEM((tmm, tn), F32)],
        compiler_params=_seq(3),
    )(a, b, *extra)


def _inproj_fwd(xhat, gam, bet, w_main, w_lr, wgu, b_gate, widths, name):
    t, d = xhat.shape
    tm = _row_tile(t)
    kw = wgu.shape[1]
    offs = [0]
    for w in widths:
        offs.append(offs[-1] + w)

    def body(x_ref, g_ref, b_ref, wm_ref, wl_ref, wgu_ref, bg_ref, *outs):
        piece_refs, (zg_ref, la_ref, hb_ref) = outs[:len(widths)], outs[len(widths):]
        hb = (x_ref[...] * g_ref[...] + b_ref[...]).astype(BF16)
        hb_ref[...] = hb
        for p, ref in enumerate(piece_refs):
            ref[...] = _mm_nt(hb, wm_ref[offs[p]:offs[p + 1], :])
        zg = _mm_nt(hb, wl_ref[...])
        zg_ref[...] = zg
        logit = _mm(zg, wgu_ref[...]) + bg_ref[...]
        la_ref[...] = (jnp.minimum(logit, 0.0) - jnp.log(1.0 + jnp.exp(-jnp.abs(logit)))) * (1.0 / GLA_GATE_TEMP)

    row = lambda i: (i, 0)
    full = lambda a: pl.BlockSpec(a.shape, lambda i: (0,) * a.ndim)
    out_w = list(widths) + [LANE, kw]
    return _tc_call(
        body, name=name, grid=(t // tm,),
        in_specs=[pl.BlockSpec((tm, d), row), full(gam), full(bet), full(w_main), full(w_lr), full(wgu), full(b_gate)],
        out_specs=[pl.BlockSpec((tm, w), row) for w in out_w] + [pl.BlockSpec((tm, d), row)],
        out_shape=[jax.ShapeDtypeStruct((t, w), F32) for w in out_w] + [jax.ShapeDtypeStruct((t, d), BF16)],
        compiler_params=_seq(1),
    )(xhat, gam, bet, w_main, w_lr, wgu, b_gate)


def _inproj_bwd(dh_part, pieces, dzg, w_main, w_lr, name):
    t, d = dh_part.shape
    tm = _row_tile(t)
    widths = [p.shape[1] for p in pieces]
    offs = [0]
    for w in widths:
        offs.append(offs[-1] + w)

    def body(*refs):
        dhp_ref = refs[0]
        p_refs = refs[1:1 + len(widths)]
        dzg_ref, wm_ref, wl_ref, dh_ref, dz_ref = refs[1 + len(widths):]
        acc = dhp_ref[...] + _mm(dzg_ref[...], wl_ref[...])
        for p, ref in enumerate(p_refs):
            v = ref[...]
            dz_ref[:, offs[p]:offs[p + 1]] = v
            acc += _mm(v, wm_ref[offs[p]:offs[p + 1], :])
        dh_ref[...] = acc

    row = lambda i: (i, 0)
    full = lambda a: pl.BlockSpec(a.shape, lambda i: (0,) * a.ndim)
    return _tc_call(
        body, name=name, grid=(t // tm,),
        in_specs=[pl.BlockSpec((tm, d), row)] + [pl.BlockSpec((tm, w), row) for w in widths]
                 + [pl.BlockSpec((tm, LANE), row), full(w_main), full(w_lr)],
        out_specs=[pl.BlockSpec((tm, d), row), pl.BlockSpec((tm, offs[-1]), row)],
        out_shape=[jax.ShapeDtypeStruct((t, d), F32), jax.ShapeDtypeStruct((t, offs[-1]), BF16)],
        compiler_params=_seq(1),
    )(dh_part, *pieces, dzg, w_main, w_lr)


def _pool_cnt(tile, tm, w):
    t = tile * tm + lax.broadcasted_iota(jnp.int32, (tm, 1), 0)
    return jnp.minimum(t + 1, w).astype(F32)


def _pool_fwd(u, wp, scale, name):
    t, pw = u.shape
    tm = _row_tile(t)
    gd = wp.shape[1]

    def body(u_ref, wp_ref, sc_ref, y_ref, p_ref, ext):
        i = pl.program_id(0)

        @pl.when(i == 0)
        def _():
            ext[0:POOL_HALO, :] = jnp.zeros((POOL_HALO, pw), F32)

        ext[POOL_HALO:POOL_HALO + tm, :] = u_ref[...]
        for gi, w in enumerate(POOL_WINDOWS):
            cols = slice(gi * gd, (gi + 1) * gd)
            s = ext[pl.ds(POOL_HALO, tm), cols]
            tot = s
            for back in range(1, w):
                tot = tot + ext[pl.ds(POOL_HALO - back, tm), cols]
            p = (tot / _pool_cnt(i, tm, w) - s).astype(BF16)
            p_ref[:, cols] = p
            y_ref[:, cols] = (jnp.dot(p, wp_ref[gi], preferred_element_type=F32) * sc_ref[:, cols]).astype(BF16)
        ext[0:POOL_HALO, :] = ext[tm:tm + POOL_HALO, :]

    row = lambda i: (i, 0)
    return _tc_call(
        body, name=name, grid=(t // tm,),
        in_specs=[pl.BlockSpec((tm, pw), row), pl.BlockSpec(wp.shape, lambda i: (0, 0, 0)),
                  pl.BlockSpec((1, pw), lambda i: (0, 0))],
        out_specs=[pl.BlockSpec((tm, pw), row), pl.BlockSpec((tm, pw), row)],
        out_shape=[jax.ShapeDtypeStruct((t, pw), BF16), jax.ShapeDtypeStruct((t, pw), BF16)],
        scratch_shapes=[pltpu.VMEM((tm + POOL_HALO, pw), F32)],
        compiler_params=_seq(1),
    )(u, wp, scale)


def _pool_bwd(dy, pb, wp, scale, name):
    t, pw = dy.shape
    tm = _row_tile(t)
    nt = t // tm
    gd = wp.shape[1]

    def body(dy_ref, p_ref, wp_ref, sc_ref, du_ref, dwp_ref, dsc_ref, ext):
        i = pl.program_id(0)
        tile = nt - 1 - i

        @pl.when(i == 0)
        def _():
            ext[tm:tm + POOL_HALO, :] = jnp.zeros((POOL_HALO, pw), F32)
            dwp_ref[...] = jnp.zeros_like(dwp_ref)
            dsc_ref[...] = jnp.zeros_like(dsc_ref)

        dps = []
        for gi, w in enumerate(POOL_WINDOWS):
            cols = slice(gi * gd, (gi + 1) * gd)
            dyv = dy_ref[:, cols]
            p = p_ref[:, cols]
            dpre = (dyv * sc_ref[:, cols]).astype(BF16)
            dsc_ref[:, cols] += jnp.sum(dyv * jnp.dot(p, wp_ref[gi], preferred_element_type=F32), axis=0, keepdims=True)
            dwp_ref[gi] += _mm_tn(p, dpre)
            dp = _mm_nt(dpre, wp_ref[gi])
            dps.append(dp)
            ext[0:tm, cols] = dp / _pool_cnt(tile, tm, w)
        for gi, w in enumerate(POOL_WINDOWS):
            cols = slice(gi * gd, (gi + 1) * gd)
            tot = ext[pl.ds(0, tm), cols]
            for fwd in range(1, w):
                tot = tot + ext[pl.ds(fwd, tm), cols]
            du_ref[:, cols] = (tot - dps[gi]).astype(BF16)
        ext[tm:tm + POOL_HALO, :] = ext[0:POOL_HALO, :]

    row = lambda i: (nt - 1 - i, 0)
    return _tc_call(
        body, name=name, grid=(nt,),
        in_specs=[pl.BlockSpec((tm, pw), row), pl.BlockSpec((tm, pw), row),
                  pl.BlockSpec(wp.shape, lambda i: (0, 0, 0)), pl.BlockSpec((1, pw), lambda i: (0, 0))],
        out_specs=[pl.BlockSpec((tm, pw), row), pl.BlockSpec(wp.shape, lambda i: (0, 0, 0)),
                   pl.BlockSpec((1, pw), lambda i: (0, 0))],
        out_shape=[jax.ShapeDtypeStruct((t, pw), BF16), jax.ShapeDtypeStruct(wp.shape, F32),
                   jax.ShapeDtypeStruct((1, pw), F32)],
        scratch_shapes=[pltpu.VMEM((tm + POOL_HALO, pw), F32)],
        compiler_params=_seq(1),
    )(dy, pb, wp, scale)


def _gla_masks(kw, vw):
    dk, dv = kw // N_HEADS, vw // N_HEADS
    lane_k = lax.broadcasted_iota(jnp.int32, (1, kw), 1)
    lane_v = lax.broadcasted_iota(jnp.int32, (1, vw), 1)
    hk = [((lane_k >= h * dk) & (lane_k < (h + 1) * dk)).astype(F32) for h in range(N_HEADS)]
    hv = [((lane_v >= h * dv) & (lane_v < (h + 1) * dv)).astype(F32) for h in range(N_HEADS)]
    r = lax.broadcasted_iota(jnp.int32, (CHUNK, CHUNK), 0)
    c = lax.broadcasted_iota(jnp.int32, (CHUNK, CHUNK), 1)
    tril = r >= c
    rs = lax.broadcasted_iota(jnp.int32, (N_HEADS * CHUNK, CHUNK), 0) & (CHUNK - 1)
    stril = rs >= lax.broadcasted_iota(jnp.int32, (N_HEADS * CHUNK, CHUNK), 1)
    return hk, hv, tril, stril


def _block_diag(x, hk, dv):
    return jnp.concatenate([x[h * dv:(h + 1) * dv, :] * hk[h] for h in range(N_HEADS)], axis=0)


def _gla_fwd(q, k, v, loga, r, gnorm, name):
    t, kw = q.shape
    vw = v.shape[1]
    dk, dv = kw // N_HEADS, vw // N_HEADS
    tm = _row_tile(t)
    nc = tm // CHUNK
    qscale = dk ** -0.5

    def body(q_ref, k_ref, v_ref, la_ref, r_ref, gn_ref, o_ref, y_ref, sall_ref, st):
        @pl.when(pl.program_id(0) == 0)
        def _():
            st[...] = jnp.zeros_like(st)

        hk, hv, tril, stril = _gla_masks(kw, vw)
        trif = tril.astype(F32)

        def chunk(c, carry):
            rows = pl.ds(pl.multiple_of(c * CHUNK, CHUNK), CHUNK)
            la = la_ref[rows, :]
            b = _mm_01(trif, la)
            bl = jnp.sum(la, axis=0, keepdims=True)
            qb = q_ref[rows, :] * (qscale * jnp.exp(b))
            kk = k_ref[rows, :]
            kb = kk * jnp.exp(-b)
            kl = kk * jnp.exp(bl - b)
            vv = v_ref[rows, :]
            s_t = st[...]
            compact = s_t[0:dv, :]
            for h in range(1, N_HEADS):
                compact = compact + s_t[h * dv:(h + 1) * dv, :]
            sall_ref[c] = compact
            qx = jnp.concatenate([qb.astype(BF16) * hk[h].astype(BF16) for h in range(N_HEADS)], axis=0)
            a = jnp.where(stril, _mm_nt(qx, kb), 0.0).astype(BF16)
            o_inter = _mm_nt(qb, s_t)
            for h in range(N_HEADS):
                vs = slice(h * dv, (h + 1) * dv)
                o_ref[rows, vs] = o_inter[:, vs] + _mm(a[h * CHUNK:(h + 1) * CHUNK, :], vv[:, vs])
            st[...] = s_t * jnp.exp(bl) + _block_diag(_mm_tn(vv, kl), hk, dv)
            return carry

        lax.fori_loop(0, nc, chunk, 0, unroll=CHUNK_UNROLL)
        for h in range(N_HEADS):
            vs = slice(h * dv, (h + 1) * dv)
            oh = o_ref[:, vs]
            on = oh * lax.rsqrt(jnp.mean(oh * oh, axis=-1, keepdims=True) + RMS_EPS)
            _, sl = _silu_parts(r_ref[:, vs])
            y_ref[:, vs] = (on * gn_ref[:, vs] * sl).astype(BF16)

    row = lambda i: (i, 0)
    return _tc_call(
        body, name=name, grid=(t // tm,),
        in_specs=[pl.BlockSpec((tm, kw), row), pl.BlockSpec((tm, kw), row), pl.BlockSpec((tm, vw), row),
                  pl.BlockSpec((tm, kw), row), pl.BlockSpec((tm, vw), row), pl.BlockSpec((1, vw), lambda i: (0, 0))],
        out_specs=[pl.BlockSpec((tm, vw), row), pl.BlockSpec((tm, vw), row),
                   pl.BlockSpec((nc, dv, kw), lambda i: (i, 0, 0))],
        out_shape=[jax.ShapeDtypeStruct((t, vw), F32), jax.ShapeDtypeStruct((t, vw), BF16),
                   jax.ShapeDtypeStruct((t // CHUNK, dv, kw), F32)],
        scratch_shapes=[pltpu.VMEM((vw, kw), F32)],
        compiler_params=_seq(1),
    )(q, k, v, loga, r, gnorm)


def _gla_bwd(dy, o, r, gnorm, q, k, v, loga, zg, sall, wgu, name):
    t, kw = q.shape
    vw = v.shape[1]
    dk, dv = kw // N_HEADS, vw // N_HEADS
    tm = _row_tile(t)
    nt = t // tm
    nc = tm // CHUNK
    qscale = dk ** -0.5

    def body(dy_ref, o_ref, r_ref, gn_ref, q_ref, k_ref, v_ref, la_ref, zg_ref, sall_ref, wgu_ref,
             dq_ref, dk_ref, dv_ref, dr_ref, dzg_ref, dwgu_ref, dbg_ref, dgn_ref, dst, do_s):
        @pl.when(pl.program_id(0) == 0)
        def _():
            dst[...] = jnp.zeros_like(dst)
            dwgu_ref[...] = jnp.zeros_like(dwgu_ref)
            dbg_ref[...] = jnp.zeros_like(dbg_ref)
            dgn_ref[...] = jnp.zeros_like(dgn_ref)

        for h in range(N_HEADS):
            vs = slice(h * dv, (h + 1) * dv)
            oh = o_ref[:, vs]
            rinv = lax.rsqrt(jnp.mean(oh * oh, axis=-1, keepdims=True) + RMS_EPS)
            on = oh * rinv
            rr = r_ref[:, vs]
            sg, sl = _silu_parts(rr)
            dyv = dy_ref[:, vs]
            gn = gn_ref[:, vs]
            dgn_ref[:, vs] += jnp.sum(dyv * on * sl, axis=0, keepdims=True)
            dr_ref[:, vs] = (dyv * on * gn * (sg * (1.0 + rr * (1.0 - sg)))).astype(BF16)
            don = dyv * gn * sl
            do_s[:, vs] = rinv * (don - on * jnp.mean(don * on, axis=-1, keepdims=True))

        hk, hv, tril, stril = _gla_masks(kw, vw)
        trif = tril.astype(F32)
        triuf = (lax.broadcasted_iota(jnp.int32, (CHUNK, CHUNK), 0)
                 <= lax.broadcasted_iota(jnp.int32, (CHUNK, CHUNK), 1)).astype(F32)
        last_row = lax.broadcasted_iota(jnp.int32, (CHUNK, 1), 0) == CHUNK - 1

        def chunk(idx, carry):
            c = nc - 1 - idx
            rows = pl.ds(pl.multiple_of(c * CHUNK, CHUNK), CHUNK)
            la = la_ref[rows, :]
            b = _mm_01(trif, la)
            bl = jnp.sum(la, axis=0, keepdims=True)
            eb = jnp.exp(b)
            enb = jnp.exp(-b)
            ebl = jnp.exp(bl - b)
            el = jnp.exp(bl)
            qb = q_ref[rows, :] * (qscale * eb)
            kk = k_ref[rows, :]
            kb = kk * enb
            kl = kk * ebl
            vv = v_ref[rows, :]
            do = do_s[rows, :]
            compact = sall_ref[c]
            s_t = jnp.concatenate([compact * hk[h] for h in range(N_HEADS)], axis=0)
            ds_t = dst[...]
            qx = jnp.concatenate([qb.astype(BF16) * hk[h].astype(BF16) for h in range(N_HEADS)], axis=0)
            dox = jnp.concatenate([do.astype(BF16) * hv[h].astype(BF16) for h in range(N_HEADS)], axis=0)
            a = jnp.where(stril, _mm_nt(qx, kb), 0.0).astype(BF16)
            da = jnp.where(stril, _mm_nt(dox, vv), 0.0).astype(BF16)
            dv_ref[rows, :] = (_mm_tn(a, dox) + _mm_nt(kl, ds_t)).astype(BF16)
            dak = _mm(da, kb)
            dqb = _mm(do, s_t)
            for h in range(N_HEADS):
                dqb = dqb + dak[h * CHUNK:(h + 1) * CHUNK, :] * hk[h]
            dkb = _mm_tn(da, qx)
            dkl = _mm(vv, ds_t)
            dbl = jnp.sum(dkl * kl, axis=0, keepdims=True) + el * jnp.sum(ds_t * s_t, axis=0, keepdims=True)
            dst[...] = ds_t * el + _block_diag(_mm_tn(do, qb), hk, dv)
            dq_ref[rows, :] = (dqb * (qscale * eb)).astype(BF16)
            dk_ref[rows, :] = (dkb * enb + dkl * ebl).astype(BF16)
            db = dqb * qb - dkb * kb - dkl * kl + jnp.where(last_row, dbl, 0.0)
            dla = _mm_01(triuf, db)
            dlogit = dla * (1.0 / GLA_GATE_TEMP) * (1.0 - jnp.exp(GLA_GATE_TEMP * la))
            dzg_ref[rows, :] = _mm_nt(dlogit, wgu_ref[...]).astype(BF16)
            dwgu_ref[...] += _mm_tn(zg_ref[rows, :], dlogit)
            dbg_ref[...] += jnp.sum(dlogit, axis=0, keepdims=True)
            return carry

        lax.fori_loop(0, nc, chunk, 0, unroll=CHUNK_UNROLL)

    row = lambda i: (nt - 1 - i, 0)
    const = lambda i: (0, 0)
    return _tc_call(
        body, name=name, grid=(nt,),
        in_specs=[pl.BlockSpec((tm, vw), row), pl.BlockSpec((tm, vw), row), pl.BlockSpec((tm, vw), row),
                  pl.BlockSpec((1, vw), const), pl.BlockSpec((tm, kw), row), pl.BlockSpec((tm, kw), row),
                  pl.BlockSpec((tm, vw), row), pl.BlockSpec((tm, kw), row), pl.BlockSpec((tm, LANE), row),
                  pl.BlockSpec((nc, dv, kw), lambda i: (nt - 1 - i, 0, 0)), pl.BlockSpec((LANE, kw), const)],
        out_specs=[pl.BlockSpec((tm, kw), row), pl.BlockSpec((tm, kw), row), pl.BlockSpec((tm, vw), row),
                   pl.BlockSpec((tm, vw), row), pl.BlockSpec((tm, LANE), row), pl.BlockSpec((LANE, kw), const),
                   pl.BlockSpec((1, kw), const), pl.BlockSpec((1, vw), const)],
        out_shape=[jax.ShapeDtypeStruct((t, kw), BF16), jax.ShapeDtypeStruct((t, kw), BF16),
                   jax.ShapeDtypeStruct((t, vw), BF16), jax.ShapeDtypeStruct((t, vw), BF16),
                   jax.ShapeDtypeStruct((t, LANE), BF16), jax.ShapeDtypeStruct((LANE, kw), F32),
                   jax.ShapeDtypeStruct((1, kw), F32), jax.ShapeDtypeStruct((1, vw), F32)],
        scratch_shapes=[pltpu.VMEM((vw, kw), F32), pltpu.VMEM((tm, vw), F32)],
        compiler_params=_seq(1),
    )(dy, o, r, gnorm, q, k, v, loga, zg, sall, wgu)


def _outproj_fwd(yp, yg, w_out, xhat, gam, bet, alpha, name):
    t, d = xhat.shape
    pw = yp.shape[1]
    tm = _row_tile(t)

    def body(yp_ref, yg_ref, w_ref, x_ref, g_ref, b_ref, xhat_ref, rstd_ref):
        for part in range(ROW_GROUPS):
            rows = pl.ds(part * (tm // ROW_GROUPS), tm // ROW_GROUPS)
            h = x_ref[rows, :] * g_ref[...] + b_ref[...]
            y = (jnp.dot(yp_ref[rows, :], w_ref[0:pw, :], preferred_element_type=F32)
                 + jnp.dot(yg_ref[rows, :], w_ref[pw:, :], preferred_element_type=F32))
            xh, rs = _ln_stats(alpha * h + y)
            xhat_ref[rows, :] = xh
            rstd_ref[rows, :] = rs

    row = lambda i: (i, 0)
    vec = pl.BlockSpec((1, d), lambda i: (0, 0))
    return _tc_call(
        body, name=name, grid=(t // tm,),
        in_specs=[pl.BlockSpec((tm, pw), row), pl.BlockSpec((tm, yg.shape[1]), row),
                  pl.BlockSpec(w_out.shape, lambda i: (0, 0)), pl.BlockSpec((tm, d), row), vec, vec],
        out_specs=[pl.BlockSpec((tm, d), row), pl.BlockSpec((tm, 1), row)],
        out_shape=[jax.ShapeDtypeStruct((t, d), F32), jax.ShapeDtypeStruct((t, 1), F32)],
        compiler_params=_seq(1),
    )(yp, yg, w_out, xhat, gam, bet)


def _outproj_bwd(dh, xhat, rstd, ln_g, w_out, pw, alpha, name):
    t, d = dh.shape
    tm = _row_tile(t)
    gw = w_out.shape[0] - pw

    def body(dh_ref, xh_ref, rs_ref, g_ref, w_ref, dyb_ref, dyp_ref, dyg_ref, dres_ref, dgam_ref, dbet_ref):
        @pl.when(pl.program_id(0) == 0)
        def _():
            dgam_ref[...] = jnp.zeros_like(dgam_ref)
            dbet_ref[...] = jnp.zeros_like(dbet_ref)

        for part in range(ROW_GROUPS):
            rows = pl.ds(part * (tm // ROW_GROUPS), tm // ROW_GROUPS)
            dy = dh_ref[rows, :]
            xh = xh_ref[rows, :]
            dr = _ln_bwd(dy, xh, rs_ref[rows, :], g_ref[...])
            dgam_ref[...] += jnp.sum(dy * xh, axis=0, keepdims=True)
            dbet_ref[...] += jnp.sum(dy, axis=0, keepdims=True)
            drb = dr.astype(BF16)
            dyb_ref[rows, :] = drb
            dres_ref[rows, :] = alpha * dr
            dyp_ref[rows, :] = _mm_nt(drb, w_ref[0:pw, :])
            dyg_ref[rows, :] = _mm_nt(drb, w_ref[pw:, :])

    row = lambda i: (i, 0)
    vec = pl.BlockSpec((1, d), lambda i: (0, 0))
    return _tc_call(
        body, name=name, grid=(t // tm,),
        in_specs=[pl.BlockSpec((tm, d), row), pl.BlockSpec((tm, d), row), pl.BlockSpec((tm, 1), row), vec,
                  pl.BlockSpec(w_out.shape, lambda i: (0, 0))],
        out_specs=[pl.BlockSpec((tm, d), row), pl.BlockSpec((tm, pw), row), pl.BlockSpec((tm, gw), row),
                   pl.BlockSpec((tm, d), row), vec, vec],
        out_shape=[jax.ShapeDtypeStruct((t, d), BF16), jax.ShapeDtypeStruct((t, pw), F32),
                   jax.ShapeDtypeStruct((t, gw), F32), jax.ShapeDtypeStruct((t, d), F32),
                   jax.ShapeDtypeStruct((1, d), F32), jax.ShapeDtypeStruct((1, d), F32)],
        compiler_params=_seq(1),
    )(dh, xhat, rstd, ln_g, w_out)


def _loss_head(xhat, gam, bet, target, n_rows, name):
    t, d = xhat.shape
    tm = _row_tile(t)

    def body(x_ref, g_ref, b_ref, t_ref, dy_ref, loss_ref):
        i = pl.program_id(0)

        @pl.when(i == 0)
        def _():
            loss_ref[...] = jnp.zeros_like(loss_ref)

        rowi = i * tm + lax.broadcasted_iota(jnp.int32, (tm, 1), 0)
        live = (rowi >= N_META) & (rowi < N_META + n_rows)
        diff = jnp.where(live, x_ref[...] * g_ref[...] + b_ref[...] - t_ref[...], 0.0)
        dy_ref[...] = diff * (1.0 / d)
        loss_ref[...] += jnp.sum(diff * diff) * (0.5 / d)

    row = lambda i: (i, 0)
    vec = pl.BlockSpec((1, d), lambda i: (0, 0))
    return _tc_call(
        body, name=name, grid=(t // tm,),
        in_specs=[pl.BlockSpec((tm, d), row), vec, vec, pl.BlockSpec((tm, d), row)],
        out_specs=[pl.BlockSpec((tm, d), row), pl.BlockSpec((8, LANE), lambda i: (0, 0))],
        out_shape=[jax.ShapeDtypeStruct((t, d), F32), jax.ShapeDtypeStruct((8, LANE), F32)],
        compiler_params=_seq(1),
    )(xhat, gam, bet, target)


def _rows_block(r, c):
    best = r
    for cand in range(BF16_ROWS, r, BF16_ROWS):
        if r % cand == 0 and cand * c * 4 <= (1 << 20):
            best = cand
    return best if best * c * 4 <= (4 << 20) else r


def _sum_slots(me, mine, recvs, name, layers_side_by_side=False):
    nl = len(recvs)
    ns, r, c = recvs[0].shape
    tr = _rows_block(r, c)

    def body(me_ref, *refs):
        o_ref = refs[nl * ns]
        for l in range(nl):
            acc = refs[l * ns][...].astype(F32)
            for s in range(1, ns):
                acc = acc + refs[l * ns + s][...].astype(F32)
            if layers_side_by_side:
                o_ref[0, :, l * c:(l + 1) * c] = acc.astype(o_ref.dtype)
            else:
                o_ref[l] = acc.astype(o_ref.dtype)

    def slot(s):
        return pl.BlockSpec((None, tr, c), lambda i, me_ref: ((me_ref[0] + s) % ns, i, 0))

    out = (1, r, nl * c) if layers_side_by_side else (nl, r, c)
    operands = []
    for l in range(nl):
        operands += [mine[l]] + [recvs[l]] * (ns - 1)
    return _tc_call(
        body, name=name,
        grid_spec=pltpu.PrefetchScalarGridSpec(
            num_scalar_prefetch=1, grid=(r // tr,),
            in_specs=[slot(s) for s in range(ns)] * nl,
            out_specs=pl.BlockSpec((out[0], tr, out[2]), lambda i, me_ref: (0, i, 0))),
        out_shape=jax.ShapeDtypeStruct(out, recvs[0].dtype),
        compiler_params=_seq(1),
    )(me, *operands)


def _adamw(w, terms, m, v, name):
    nl, r, c = w.shape
    tc = c
    while tc % (2 * LANE) == 0 and tc > 4 * LANE:
        tc //= 2
    tr = _rows_block(r, tc)
    nterm = len(terms)

    def body(*refs):
        w_ref = refs[0]
        t_refs = refs[1:1 + nterm]
        m_ref, v_ref, g_ref, d_ref, nm_ref, nv_ref = refs[1 + nterm:]
        g = t_refs[0][...].astype(F32)
        for tr_ in t_refs[1:]:
            g = g + tr_[...].astype(F32)
        nm = ADAM_B1 * m_ref[...] + (1.0 - ADAM_B1) * g
        nv = ADAM_B2 * v_ref[...] + (1.0 - ADAM_B2) * jnp.square(g)
        m_hat = nm / (1.0 - ADAM_B1 ** ADAM_STEP)
        v_hat = nv / (1.0 - ADAM_B2 ** ADAM_STEP)
        g_ref[...] = g
        d_ref[...] = -ADAM_LR * (m_hat / (jnp.sqrt(v_hat) + ADAM_EPS) + ADAM_WD * w_ref[...])
        nm_ref[...] = nm
        nv_ref[...] = nv

    spec = pl.BlockSpec((None, tr, tc), lambda l, i, j: (l, i, j))
    shp = jax.ShapeDtypeStruct((nl, r, c), F32)
    return _tc_call(
        body, name=name, grid=(nl, r // tr, c // tc),
        in_specs=[spec] * (3 + nterm), out_specs=[spec] * 4, out_shape=[shp] * 4,
        compiler_params=_seq(3),
    )(w, *terms, m, v)


XY_RELATIONS = ((1, 0, 0), (0, 1, 0), (1, 1, 0))
ALL_RELATIONS = tuple((fx, fy, fc) for fx in (0, 1) for fy in (0, 1) for fc in (0, 1) if fx or fy or fc)
HBM_SPEC = pl.BlockSpec(memory_space=pltpu.HBM)
SEM_SPEC = pl.BlockSpec(memory_space=pltpu.SEMAPHORE)
DATAFLOW = pltpu.SideEffectType.DATAFLOW_SIDE_EFFECTING


def _split_call(body, **kw):
    return pl.pallas_call(body, **kw)


def _flip(v, f):
    return 1 - v if f else v


def _any_spec(n):
    return [pl.BlockSpec(memory_space=pl.ANY)] * n


def _relations(kind):
    return ALL_RELATIONS if kind == "bcast" else XY_RELATIONS


def _copies(kind, arr, land, sems):
    x, y, c = lax.axis_index("x"), lax.axis_index("y"), lax.axis_index("c")
    out = []
    for (fx, fy, fc), (send_sem, recv_sem) in zip(_relations(kind), sems):
        px, py, pc = _flip(x, fx), _flip(y, fy), _flip(c, fc)
        if kind == "bcast":
            mine, theirs = 4 * x + 2 * y + c, 4 * px + 2 * py + pc
        else:
            mine, theirs = 2 * x + y, 2 * px + py
        src, to_mine, to_theirs = arr, land.at[mine], land.at[theirs]
        if kind == "scatter":
            src = arr.at[theirs]
        if kind == "gather_half":
            rows = _my_half(arr.shape[0], c)
            src, to_mine, to_theirs = arr.at[rows], land.at[mine, rows], land.at[theirs, rows]
        both = dict(src_ref=src, send_sem=send_sem, recv_sem=recv_sem, device_id=(px, py, pc), device_id_type=MESH)
        out.append((pltpu.make_async_remote_copy(dst_ref=to_mine, **both),
                    pltpu.make_async_remote_copy(dst_ref=to_theirs, **both)))
    return out


def _my_half(nrows, c):
    return pl.ds(c * (nrows // 2), nrows // 2)


def _share_halves(name, kinds, lands):
    ks = [k for k, kd in enumerate(kinds) if kd == "gather_half"]
    n = len(ks)

    def body(*refs):
        l_refs = refs[n:2 * n]
        send_sems, recv_sems = refs[2 * n:]
        x, y, c = lax.axis_index("x"), lax.axis_index("y"), lax.axis_index("c")
        copies = []
        for i in range(n):
            nrows = l_refs[i].shape[1]
            for r, (fx, fy, _) in enumerate(XY_RELATIONS):
                slot = 2 * _flip(x, fx) + _flip(y, fy)
                both = dict(src_ref=l_refs[i].at[slot, _my_half(nrows, c)], send_sem=send_sems.at[i, r],
                            recv_sem=recv_sems.at[i, r], device_id=(x, y, 1 - c), device_id_type=MESH)
                copies.append((pltpu.make_async_remote_copy(dst_ref=l_refs[i].at[slot, _my_half(nrows, c)], **both),
                               pltpu.make_async_remote_copy(dst_ref=l_refs[i].at[slot, _my_half(nrows, 1 - c)], **both)))
        for send, _ in copies:
            send.start()
        for _, arrival in copies:
            arrival.wait_recv()
        for send, _ in copies:
            send.wait_send()

    outs = _comm_call(
        body, name=name,
        in_specs=_any_spec(n), out_specs=_any_spec(n),
        out_shape=[jax.ShapeDtypeStruct(lands[k].shape, lands[k].dtype) for k in ks],
        input_output_aliases={i: i for i in range(n)},
        scratch_shapes=[pltpu.SemaphoreType.DMA((n, 3)), pltpu.SemaphoreType.DMA((n, 3))],
    )(*[lands[k] for k in ks])
    lands = list(lands)
    for k, o in zip(ks, outs):
        lands[k] = o
    return lands


def _sem_pairs(kinds, sems):
    out, at = [], 0
    for kind in kinds:
        nrel = len(_relations(kind))
        out.append([(sems[at + 2 * r], sems[at + 2 * r + 1]) for r in range(nrel)])
        at += 2 * nrel
    return out


def _exchange_start(name, kinds, arrs, lands):
    n = len(arrs)
    nsem = sum(2 * len(_relations(kd)) for kd in kinds)

    def body(*refs):
        a_refs, l_refs = refs[:n], refs[n:2 * n]
        pairs = _sem_pairs(kinds, refs[2 * n:2 * n + nsem])
        token = refs[-1]
        for k in range(n):
            for send, _ in _copies(kinds[k], a_refs[k], l_refs[k], pairs[k]):
                send.start()
        token[...] = jnp.zeros_like(token)

    thru = [pltpu.HBM(a.shape, a.dtype) for a in list(arrs) + list(lands)]
    outs = _split_call(
        body, name=name,
        out_shape=(*[pltpu.SemaphoreType.DMA(())] * nsem, *thru, jax.ShapeDtypeStruct((8, LANE), F32)),
        in_specs=[HBM_SPEC] * (2 * n),
        out_specs=(*[SEM_SPEC] * nsem, *[HBM_SPEC] * (2 * n), pl.BlockSpec(memory_space=pltpu.VMEM)),
        input_output_aliases={i: nsem + i for i in range(2 * n)},
        compiler_params=pltpu.CompilerParams(has_side_effects=DATAFLOW),
    )(*[pltpu.with_memory_space_constraint(a, pltpu.HBM) for a in list(arrs) + list(lands)])
    return dict(kinds=kinds, sems=outs[:nsem], arrs=outs[nsem:nsem + n], lands=outs[nsem + n:nsem + 2 * n],
                token=outs[-1])


def _exchange_wait(name, st, after):
    kinds = st["kinds"]
    n = len(kinds)
    nsem = len(st["sems"])

    def body(*refs):
        a_refs, l_refs = refs[:n], refs[n:2 * n]
        pairs = _sem_pairs(kinds, refs[2 * n:2 * n + nsem])
        for k in range(n):
            for _, arrival in _copies(kinds[k], a_refs[k], l_refs[k], pairs[k]):
                arrival.wait_send()
                arrival.wait_recv()
        refs[-1][...] = jnp.zeros_like(refs[-1])

    ins = list(st["arrs"]) + list(st["lands"])
    outs = _split_call(
        body, name=name,
        out_shape=[pltpu.HBM(a.shape, a.dtype) for a in ins] + [jax.ShapeDtypeStruct((8, LANE), F32)],
        in_specs=[HBM_SPEC] * (2 * n) + [SEM_SPEC] * nsem + [pl.BlockSpec(memory_space=pl.ANY)],
        out_specs=[HBM_SPEC] * (2 * n) + [pl.BlockSpec(memory_space=pltpu.VMEM)],
        input_output_aliases={i: i for i in range(2 * n)},
        compiler_params=pltpu.CompilerParams(has_side_effects=DATAFLOW),
    )(*ins, *st["sems"], after)
    return outs[:n], outs[n:2 * n], outs[-1]


def _landing(own, slot, nslot):
    return lax.dynamic_update_slice(lax.empty((nslot,) + own.shape, own.dtype), own[None], (slot,) + (0,) * own.ndim)


def _swap_sibling(parts, name):
    n = len(parts)

    def body(*refs):
        ins, outs = refs[:n], refs[n:2 * n]
        send_sems, recv_sems = refs[2 * n:]
        sib = (lax.axis_index("x"), lax.axis_index("y"), 1 - lax.axis_index("c"))
        cps = [pltpu.make_async_remote_copy(src_ref=ins[k], dst_ref=outs[k], send_sem=send_sems.at[k],
                                            recv_sem=recv_sems.at[k], device_id=sib, device_id_type=MESH)
               for k in range(n)]
        for cp in cps:
            cp.start()
        for cp in cps:
            cp.wait_recv()
        for cp in cps:
            cp.wait_send()

    return _comm_call(
        body, name=name,
        in_specs=_any_spec(n), out_specs=_any_spec(n),
        out_shape=[jax.ShapeDtypeStruct(p.shape, p.dtype) for p in parts],
        scratch_shapes=[pltpu.SemaphoreType.DMA((n,)), pltpu.SemaphoreType.DMA((n,))],
    )(*parts)


def _col_shards(a, n=N_SHARD):
    r, c = a.shape
    return a.reshape(r, n, c // n).transpose(1, 0, 2)


def _from_col_shards(a):
    n, r, cs = a.shape
    return a.transpose(1, 0, 2).reshape(r, n * cs)


def kernel(x, meta_tokens, ffn1_w_gate, ffn1_w_up, ffn1_w_down, ln1_g, ln1_b, w_in, w_gate_up, b_gate, w_pool, pool_scale, gla_norm_g, w_out, ln2_g, ln2_b, ffn2_w_gate, ffn2_w_up, ffn2_w_down, ln3_g, ln3_b, loss_target, m_meta_tokens, m_ffn1_w_gate, m_ffn1_w_up, m_ffn1_w_down, m_ln1_g, m_ln1_b, m_w_in, m_w_gate_up, m_b_gate, m_w_pool, m_pool_scale, m_gla_norm_g, m_w_out, m_ln2_g, m_ln2_b, m_ffn2_w_gate, m_ffn2_w_up, m_ffn2_w_down, m_ln3_g, m_ln3_b, v_meta_tokens, v_ffn1_w_gate, v_ffn1_w_up, v_ffn1_w_down, v_ln1_g, v_ln1_b, v_w_in, v_w_gate_up, v_b_gate, v_w_pool, v_pool_scale, v_gla_norm_g, v_w_out, v_ln2_g, v_ln2_b, v_ffn2_w_gate, v_ffn2_w_up, v_ffn2_w_down, v_ln3_g, v_ln3_b):
    w = dict(meta_tokens=meta_tokens, ffn1_w_gate=ffn1_w_gate, ffn1_w_up=ffn1_w_up, ffn1_w_down=ffn1_w_down,
             ln1_g=ln1_g, ln1_b=ln1_b, w_in=w_in, w_gate_up=w_gate_up, b_gate=b_gate, w_pool=w_pool,
             pool_scale=pool_scale, gla_norm_g=gla_norm_g, w_out=w_out, ln2_g=ln2_g, ln2_b=ln2_b,
             ffn2_w_gate=ffn2_w_gate, ffn2_w_up=ffn2_w_up, ffn2_w_down=ffn2_w_down, ln3_g=ln3_g, ln3_b=ln3_b)
    mom1 = dict(meta_tokens=m_meta_tokens, ffn1_w_gate=m_ffn1_w_gate, ffn1_w_up=m_ffn1_w_up,
                ffn1_w_down=m_ffn1_w_down, ln1_g=m_ln1_g, ln1_b=m_ln1_b, w_in=m_w_in, w_gate_up=m_w_gate_up,
                b_gate=m_b_gate, w_pool=m_w_pool, pool_scale=m_pool_scale, gla_norm_g=m_gla_norm_g, w_out=m_w_out,
                ln2_g=m_ln2_g, ln2_b=m_ln2_b, ffn2_w_gate=m_ffn2_w_gate, ffn2_w_up=m_ffn2_w_up,
                ffn2_w_down=m_ffn2_w_down, ln3_g=m_ln3_g, ln3_b=m_ln3_b)
    mom2 = dict(meta_tokens=v_meta_tokens, ffn1_w_gate=v_ffn1_w_gate, ffn1_w_up=v_ffn1_w_up,
                ffn1_w_down=v_ffn1_w_down, ln1_g=v_ln1_g, ln1_b=v_ln1_b, w_in=v_w_in, w_gate_up=v_w_gate_up,
                b_gate=v_b_gate, w_pool=v_w_pool, pool_scale=v_pool_scale, gla_norm_g=v_gla_norm_g, w_out=v_w_out,
                ln2_g=v_ln2_g, ln2_b=v_ln2_b, ffn2_w_gate=v_ffn2_w_gate, ffn2_w_up=v_ffn2_w_up,
                ffn2_w_down=v_ffn2_w_down, ln3_g=v_ln3_g, ln3_b=v_ln3_b)

    xs = x[0]
    s_len, d = xs.shape
    nl = ln1_g.shape[0]
    alpha = (2.0 * nl) ** 0.25
    t_real = N_META + s_len
    t_pad = -(-t_real // LANE) * LANE
    pw = pool_scale.shape[1]
    kw = b_gate.shape[1]
    vw = gla_norm_g.shape[1]
    rank = w_gate_up.shape[1]
    widths = (pw, kw, kw, vw, vw)
    n_main = sum(widths)
    dff_s = ffn1_w_gate.shape[2]
    dff_c = N_SHARD * dff_s // FFN_CHUNKS

    me_xy = 2 * lax.axis_index("x") + lax.axis_index("y")
    me_all = 2 * me_xy + lax.axis_index("c")
    ffn1_names = ("ffn1_w_gate", "ffn1_w_up", "ffn1_w_down")
    mix_names = ("w_out", "w_gate_up", "w_in")
    ffn2_names = ("ffn2_w_gate", "ffn2_w_up", "ffn2_w_down")

    gate_up = ("ffn1_w_gate", "ffn1_w_up", "ffn2_w_gate", "ffn2_w_up")

    def stored(n, a):
        if n in gate_up:
            return jnp.swapaxes(a, 1, 2)
        return jnp.transpose(a, (2, 0, 1)) if n == "w_in" else a

    def as_given(n, a):
        if n in gate_up:
            return jnp.swapaxes(a, 1, 2)
        if n == "w_in":
            return jnp.transpose(a.reshape(-1, nl, d), (1, 2, 0))
        return a.reshape(w[n].shape)

    stages = [[("meta_tokens", None)], [(n, 0) for n in ffn1_names], [(n, 0) for n in mix_names + ffn2_names]]
    stages += [[(n, l) for n in BIG] for l in range(1, nl)]
    gathers, wa = {}, {}

    halved = ffn1_names + ffn2_names + ("w_out",)

    def start_gather(si, dep=None):
        own = []
        for n, l in stages[si]:
            a = meta_tokens if l is None else (stored(n, w[n])[:, l] if n == "w_in" else stored(n, w[n])[l])
            a = a if dep is None else a + dep
            own.append(a if l is None else a.astype(BF16))
        gathers[si] = _exchange_start(f"gather_start_{si}", ["gather_half" if n in halved else "gather"
                                                             for n, _ in stages[si]], own,
                                      [_landing(a, me_xy, N_SHARD) for a in own])
        return gathers[si]["token"]

    def arrive(si, after):
        _, lands, token = _exchange_wait(f"gather_wait_{si}", gathers[si], after)
        if any(kd == "gather_half" for kd in gathers[si]["kinds"]):
            lands = _share_halves(f"gather_share_{si}", gathers[si]["kinds"], lands)
        for item, a in zip(stages[si], lands):
            wa[item] = a.reshape(FFN_CHUNKS, -1, d) if item[0] in ffn1_names + ffn2_names else a
        return token

    def mixer_weights(l):
        wi = wa["w_in", l].reshape(-1, d)
        return dict(w_main=wi[:n_main], w_lr=jnp.pad(wi[n_main:], ((0, LANE - rank), (0, 0))),
                    wgu=jnp.pad(_from_col_shards(wa["w_gate_up", l]), ((0, LANE - rank), (0, 0))),
                    wout=wa["w_out", l].reshape(-1, d))

    wp16 = w_pool.astype(BF16)
    ones = jnp.ones((1, d), F32)
    zeros = jnp.zeros((1, d), F32)
    target = jnp.concatenate([jnp.zeros((N_META, d), F32), loss_target[0], jnp.zeros((t_pad - t_real, d), F32)], axis=0)

    started = start_gather(0)
    for si in range(1, len(stages)):
        started = start_gather(si, started[0:1, 0:1])
    arrive(0, started)
    meta_full = _from_col_shards(wa["meta_tokens", None])
    h0 = jnp.concatenate([meta_full, xs, jnp.zeros((t_pad - t_real, d), F32)], axis=0)
    arrive(1, h0[:8, :LANE] + target[:8, :LANE])

    saved, mw = [], []
    cur, cur_g, cur_b = h0, ones, zeros
    for l in range(nl):
        s = {}
        xh1, rs1, hb0, g1, u1 = _ffn_fwd(cur, cur_g, cur_b, wa["ffn1_w_gate", l], wa["ffn1_w_up", l],
                                         wa["ffn1_w_down", l], alpha, f"ffn1_fwd_{l}")
        if l == 0:
            arrive(2, xh1)
        mw.append(mixer_weights(l))
        up, q, k, v, r, zg, la, hb1 = _inproj_fwd(xh1, ln1_g[l:l + 1], ln1_b[l:l + 1], mw[l]["w_main"], mw[l]["w_lr"],
                                                  mw[l]["wgu"], b_gate[l:l + 1], widths, f"inproj_fwd_{l}")
        yp, pb = _pool_fwd(up, wp16[l], pool_scale[l:l + 1], f"pool_fwd_{l}")
        o, yg, sall = _gla_fwd(q, k, v, la, r, gla_norm_g[l:l + 1], f"gla_fwd_{l}")
        xh2, rs2 = _outproj_fwd(yp, yg, mw[l]["wout"], xh1, ln1_g[l:l + 1], ln1_b[l:l + 1], alpha, f"outproj_fwd_{l}")
        if l + 1 < nl:
            arrive(l + 3, xh2)
        xh3, rs3, hb2, g2, u2 = _ffn_fwd(xh2, ln2_g[l:l + 1], ln2_b[l:l + 1], wa["ffn2_w_gate", l], wa["ffn2_w_up", l],
                                         wa["ffn2_w_down", l], alpha, f"ffn2_fwd_{l}")
        s.update(xh1=xh1, rs1=rs1, hb0=hb0, g1=g1, u1=u1, q=q, k=k, v=v, r=r, zg=zg, la=la, hb1=hb1, yp=yp, pb=pb,
                 o=o, yg=yg, sall=sall, xh2=xh2, rs2=rs2, xh3=xh3, rs3=rs3, hb2=hb2, g2=g2, u2=u2)
        saved.append(s)
        cur, cur_g, cur_b = xh3, ln3_g[l:l + 1], ln3_b[l:l + 1]

    dh, loss_acc = _loss_head(cur, cur_g, cur_b, target, s_len, "loss_head")
    loss = lax.psum(loss_acc[0, 0], ("x", "y", "c"))

    small_grads = {n: [None] * nl for n in SMALL}
    scatters = []

    def depart(name, items, grads, kinds=None):
        lands = [_landing(g, me_all, N_DEV) if kd == "bcast" else lax.empty(g.shape, g.dtype)
                 for g, kd in zip(grads, kinds or ["scatter"] * len(grads))]
        st = _exchange_start(name, kinds or ["scatter"] * len(grads), grads, lands)
        scatters.append((name, items, st))
        return st["token"]

    def pack(parts):
        flat = jnp.concatenate([parts[n].reshape(-1) for n in SMALL])
        return flat.reshape(-1, LANE)

    def ffn_wgrad(n, l, hb, dgb, dub, act, dfb, after=None):
        if n.endswith("down"):
            dw = _wgrad(act, dfb, dff_c, d, f"{n}_grad_{l}", after)
        else:
            dw = _wgrad(dgb if n.endswith("gate") else dub, hb, dff_c, d, f"{n}_grad_{l}", after)
        return dw.reshape(N_SHARD, dff_s, d)

    late = []
    for l in reversed(range(nl)):
        s = saved[l]
        dh, dfb, dgb, dub, act, dgam, dbet = _ffn_bwd(dh, s["xh3"], s["rs3"], ln3_g[l:l + 1], s["g2"], s["u2"],
                                                      wa["ffn2_w_gate", l], wa["ffn2_w_up", l], wa["ffn2_w_down", l],
                                                      alpha, f"ffn2_bwd_{l}")
        small_grads["ln3_g"][l], small_grads["ln3_b"][l] = dgam, dbet
        gone = depart(f"scatter_start_ffn2_{l}", [(n, l) for n in ffn2_names],
                      [ffn_wgrad(n, l, s["hb2"], dgb, dub, act, dfb) for n in ffn2_names])

        dyb, dyp, dyg, dres, dgam, dbet = _outproj_bwd(dh, s["xh2"], s["rs2"], ln2_g[l:l + 1] + gone[0:1, 0:1],
                                                       mw[l]["wout"], pw, alpha, f"outproj_bwd_{l}")
        small_grads["ln2_g"][l], small_grads["ln2_b"][l] = dgam, dbet
        dwo = jnp.concatenate([_wgrad(s["yp"], dyb, pw, d, f"dwout_pool_{l}"),
                               _wgrad(s["yg"], dyb, vw, d, f"dwout_gla_{l}")], axis=0)
        dq, dk, dv, dr, dzg, dwgu, dbg, dgn = _gla_bwd(dyg, s["o"], s["r"], gla_norm_g[l:l + 1], s["q"], s["k"],
                                                       s["v"], s["la"], s["zg"], s["sall"], mw[l]["wgu"],
                                                       f"gla_bwd_{l}")
        dup, dwp, dsc = _pool_bwd(dyp, s["pb"], wp16[l], pool_scale[l:l + 1], f"pool_bwd_{l}")
        small_grads["b_gate"][l], small_grads["gla_norm_g"][l] = dbg, dgn
        small_grads["w_pool"][l], small_grads["pool_scale"][l] = dwp, dsc
        dh, dz = _inproj_bwd(dres, [dup, dq, dk, dv, dr], dzg, mw[l]["w_main"], mw[l]["w_lr"], f"inproj_bwd_{l}")
        dwi = jnp.concatenate([_wgrad(dz, s["hb1"], 4 * LANE, d, f"dwin_main_{l}"),
                               _wgrad(dzg, s["hb1"], LANE, d, f"dwin_lr_{l}")[:rank]], axis=0)
        gone = depart(f"scatter_start_mix_{l}", [(n, l) for n in mix_names],
                      [dwo.reshape(N_SHARD, -1, d), _col_shards(dwgu[:rank]), dwi.reshape(N_SHARD, -1, d)])

        dh, dfb, dgb, dub, act, dgam, dbet = _ffn_bwd(dh, s["xh1"], s["rs1"], ln1_g[l:l + 1] + gone[0:1, 0:1],
                                                      s["g1"], s["u1"], wa["ffn1_w_gate", l], wa["ffn1_w_up", l],
                                                      wa["ffn1_w_down", l], alpha, f"ffn1_bwd_{l}")
        small_grads["ln1_g"][l], small_grads["ln1_b"][l] = dgam, dbet
        if l:
            gone = depart(f"scatter_start_ffn1_{l}", [(n, l) for n in ffn1_names],
                          [ffn_wgrad(n, l, s["hb0"], dgb, dub, act, dfb) for n in ffn1_names])
            ln3_g = ln3_g.at[l - 1:l].add(gone[0:1, 0:1])
            continue
        grad_x = dh[N_META:t_real][None]
        small_vec = pack({n: jnp.stack(small_grads[n]) for n in SMALL})
        gone = depart("scatter_start_rest", [("meta_tokens", 0), ("small", 0)],
                      [_col_shards(dh[:N_META]), small_vec], ["scatter", "bcast"])
        for n in ffn1_names:
            g = ffn_wgrad(n, l, s["hb0"], dgb, dub, act, dfb, after=gone)
            gone = depart(f"scatter_start_{n}", [(n, l)], [g])
            late.append(scatters.pop())

    sent, recv, results, firsts = {}, {}, {}, []
    my_slot = me_xy.reshape(1).astype(jnp.int32)

    def collect(group, after):
        for name, items, st in group:
            arrs, lands, _ = _exchange_wait(name.replace("start", "wait"), st, after)
            for item, a, b in zip(items, arrs, lands):
                sent[item], recv[item] = a, b

    def reduce_and_update(names, tag):
        partial = []
        for n in names:
            layers = [(n, l) for l in range(1 if n == "meta_tokens" else nl)]
            partial.append(_sum_slots(my_slot, [sent[it] for it in layers], [recv[it] for it in layers],
                                      f"sum_{n}", n == "w_in"))
        for n, mine, theirs in zip(names, partial, _swap_sibling(partial, f"swap_sibling_{tag}")):
            fit = lambda a: stored(n, a).reshape(mine.shape)
            outs = _adamw(fit(w[n]), [mine, theirs], fit(mom1[n]), fit(mom2[n]), f"adamw_{n}")
            results[n] = [as_given(n, o) for o in outs]
            firsts.append(outs[1][0, 0, 0])

    collect(scatters, gone)
    early = [n for n in ("meta_tokens",) + BIG if n not in ffn1_names]
    reduce_and_update(early, "early")
    small_terms = [recv["small", 0][i][None] for i in range(N_DEV)]
    souts = _adamw(pack(w)[None], small_terms, pack(mom1)[None], pack(mom2)[None], "adamw_small")
    off = 0
    for n in SMALL:
        size = w[n].size
        results[n] = [o.reshape(-1)[off:off + size].reshape(w[n].shape) for o in souts]
        off += size
    collect(late, souts[0][0, :8] + functools.reduce(jnp.add, firsts))
    reduce_and_update(ffn1_names, "late")

    out = [loss, grad_x]
    for part in range(4):
        out += [results[n][part] for n in WEIGHTS]
    return tuple(out)
```

```python
import functools

import jax
import jax.numpy as jnp
from jax import lax
from jax.experimental import pallas as pl
from jax.experimental.pallas import tpu as pltpu

F32 = jnp.float32
BF16 = jnp.bfloat16
MESH = pl.DeviceIdType.MESH

N_META = 16
POOL_WINDOWS = (2, 4, 8, 16)
POOL_HALO = 16
N_HEADS = 4
GLA_GATE_TEMP = 16.0
CHUNK = 128
CHUNK_UNROLL = 5
LN_EPS = 1e-5
RMS_EPS = 1e-6
ADAM_LR = 0.001
ADAM_B1 = 0.9
ADAM_B2 = 0.999
ADAM_EPS = 1e-08
ADAM_WD = 0.01
ADAM_STEP = 10
LANE = 128
BF16_ROWS = 16
ROW_TILE = 640
FFN_ROW_TILE = 640
FFN_CHUNKS = 4
FFN_SPLIT = 2
ROW_GROUPS = 2
WGRAD_K_MAX = 4224
N_SHARD = 4
N_DEV = 8

BIG = ("ffn1_w_gate", "ffn1_w_up", "ffn1_w_down", "w_in", "w_gate_up", "w_out",
       "ffn2_w_gate", "ffn2_w_up", "ffn2_w_down")
SMALL = ("ln1_g", "ln1_b", "b_gate", "w_pool", "pool_scale", "gla_norm_g", "ln2_g", "ln2_b", "ln3_g", "ln3_b")
WEIGHTS = ("meta_tokens", "ffn1_w_gate", "ffn1_w_up", "ffn1_w_down", "ln1_g", "ln1_b", "w_in", "w_gate_up",
           "b_gate", "w_pool", "pool_scale", "gla_norm_g", "w_out", "ln2_g", "ln2_b", "ffn2_w_gate",
           "ffn2_w_up", "ffn2_w_down", "ln3_g", "ln3_b")


def _tc_call(body, **kw):
    return pl.pallas_call(body, **kw)


def _comm_call(body, **kw):
    return pl.pallas_call(body, **kw)


def _seq(n):
    return pltpu.CompilerParams(dimension_semantics=("arbitrary",) * n)


def _mm(a, b):
    return jnp.dot(a.astype(BF16), b.astype(BF16), preferred_element_type=F32)


def _mm_nt(a, b):
    return lax.dot_general(a.astype(BF16), b.astype(BF16), (((1,), (1,)), ((), ())), preferred_element_type=F32)


def _mm_tn(a, b):
    return lax.dot_general(a.astype(BF16), b.astype(BF16), (((0,), (0,)), ((), ())), preferred_element_type=F32)


def _mm_01(a, b):
    hi = b.astype(BF16)
    lo = (b - hi.astype(F32)).astype(BF16)
    a = a.astype(BF16)
    return jnp.dot(a, hi, preferred_element_type=F32) + jnp.dot(a, lo, preferred_element_type=F32)


def _row_tile(t, most=None):
    tm = min(most or ROW_TILE, t)
    while t % tm:
        tm -= LANE
    return tm


def _silu_parts(g):
    sg = jax.nn.sigmoid(g)
    return sg, g * sg


def _ln_stats(r):
    mu = jnp.mean(r, axis=-1, keepdims=True)
    rc = r - mu
    var = jnp.mean(rc * rc, axis=-1, keepdims=True)
    rs = lax.rsqrt(var + LN_EPS)
    return rc * rs, rs


def _ln_bwd(dy, xh, rs, gam):
    dyg = dy * gam
    c1 = jnp.mean(dyg, axis=-1, keepdims=True)
    c2 = jnp.mean(dyg * xh, axis=-1, keepdims=True)
    return rs * (dyg - c1 - xh * c2)


def _ffn_fwd(xin, gam_in, bet_in, wg, wu, wd, alpha, name, loss_head=None):
    t, d = xin.shape
    nj, tf, _ = wg.shape
    tm = _row_tile(t, FFN_ROW_TILE)
    nt = t // tm
    share = tm // nj
    head = [] if loss_head is None else list(loss_head[:3])

    def body(x_ref, gi_ref, bi_ref, wg_ref, wu_ref, wd_ref, *rest):
        head_refs, rest = rest[:len(head)], rest[len(head):]
        xhat_ref, rstd_ref, hb_ref, go_ref, uo_ref = rest[:5]
        acc, hbs = rest[-2:]
        i = pl.program_id(0)
        j = pl.program_id(1)
        cur = i % 2

        if head:
            @pl.when((i == 0) & (j == 0))
            def _():
                rest[6][...] = jnp.zeros_like(rest[6])

        def norm_previous():
            rows = pl.ds(pl.multiple_of(j * share, BF16_ROWS), share)
            xhat, rs = _ln_stats(0.5 * acc[1 - cur, rows, :])
            xhat_ref[rows, :] = xhat
            rstd_ref[rows, :] = rs
            if head:
                g_ref, b_ref, t_ref = head_refs
                dy_ref, loss_ref = rest[5], rest[6]
                rowi = (i - 1) * tm + j * share + lax.broadcasted_iota(jnp.int32, (share, 1), 0)
                live = (rowi >= N_META) & (rowi < N_META + loss_head[3])
                diff = jnp.where(live, xhat * g_ref[...] + b_ref[...] - t_ref[rows, :], 0.0)
                dy_ref[rows, :] = diff * (1.0 / d)
                loss_ref[...] += jnp.sum(diff * diff) * (0.5 / d)

        @pl.when(i < nt)
        def _():
            @pl.when(j == 0)
            def _():
                h = x_ref[...] * gi_ref[...] + bi_ref[...]
                hb = h.astype(BF16)
                hbs[...] = hb
                hb_ref[...] = hb
                acc[cur] = (2.0 * alpha) * h

                @pl.when(i == 0)
                def _():
                    acc[1] = jnp.zeros((tm, d), F32)

            norm_previous()
            hb = hbs[...]
            g = _mm_nt(hb, wg_ref[...])
            u = _mm_nt(hb, wu_ref[...])
            _, sl = _silu_parts(g)
            go_ref[...] = g.astype(BF16)
            uo_ref[...] = u.astype(BF16)
            acc[cur] += jnp.dot((sl * u).astype(BF16), wd_ref[...], preferred_element_type=F32)

        @pl.when(i == nt)
        def _():
            norm_previous()

    here = lambda i, j: (jnp.minimum(i, nt - 1), 0)
    before = lambda i, j: (jnp.maximum(i - 1, 0), 0)
    chunk = lambda i, j: (jnp.where(i < nt, j, nj - 1), 0, 0)
    col = pl.BlockSpec((None, tm, tf), lambda i, j: (jnp.where(i < nt, j, nj - 1), jnp.minimum(i, nt - 1), 0))
    vec = pl.BlockSpec((1, d), lambda i, j: (0, 0))
    head_in = [vec, vec, pl.BlockSpec((tm, d), before)] if head else []
    head_out = [pl.BlockSpec((tm, d), before), pl.BlockSpec((8, LANE), lambda i, j: (0, 0))] if head else []
    head_shape = [jax.ShapeDtypeStruct((t, d), F32), jax.ShapeDtypeStruct((8, LANE), F32)] if head else []
    return _tc_call(
        body, name=name, grid=(nt + 1, nj),
        in_specs=[pl.BlockSpec((tm, d), here), vec, vec] + [pl.BlockSpec((None, tf, d), chunk)] * 3 + head_in,
        out_specs=[pl.BlockSpec((tm, d), before), pl.BlockSpec((tm, 1), before), pl.BlockSpec((tm, d), here), col, col]
                  + head_out,
        out_shape=[jax.ShapeDtypeStruct((t, d), F32), jax.ShapeDtypeStruct((t, 1), F32),
                   jax.ShapeDtypeStruct((t, d), BF16), jax.ShapeDtypeStruct((nj, t, tf), BF16),
                   jax.ShapeDtypeStruct((nj, t, tf), BF16)] + head_shape,
        scratch_shapes=[pltpu.VMEM((2, tm, d), F32), pltpu.VMEM((tm, d), BF16)],
        compiler_params=_seq(2),
    )(xin, gam_in, bet_in, wg, wu, wd, *head)


def _ffn_bwd(dh, xhat, rstd, ln_g, gb, ub, wg, wu, wd, alpha, name):
    t, d = dh.shape
    nj, tf, _ = wg.shape
    tm = _row_tile(t, FFN_ROW_TILE)
    nt = t // tm
    share = tm // nj

    def body(dh_ref, xh_ref, rs_ref, g_ref, gb_ref, ub_ref, wg_ref, wu_ref, wd_ref,
             dhin_ref, df_ref, dg_ref, du_ref, act_ref, dgam_ref, dbet_ref, df_s, dres_next, df_next):
        i = pl.program_id(0)
        j = pl.program_id(1)

        @pl.when((i == 0) & (j == 0))
        def _():
            dgam_ref[...] = jnp.zeros_like(dgam_ref)
            dbet_ref[...] = jnp.zeros_like(dbet_ref)

        def look_ahead():
            rows = pl.ds(pl.multiple_of(j * share, BF16_ROWS), share)
            dy = dh_ref[rows, :]
            xh = xh_ref[rows, :]
            dr = _ln_bwd(dy, xh, rs_ref[rows, :], g_ref[...])
            dres_next[rows, :] = alpha * dr
            df_next[rows, :] = (0.5 * dr).astype(BF16)
            live = jnp.where(i < nt, 1.0, 0.0)
            dgam_ref[...] += live * jnp.sum(dy * xh, axis=0, keepdims=True)
            dbet_ref[...] += live * jnp.sum(dy, axis=0, keepdims=True)

        @pl.when(i == 0)
        def _():
            look_ahead()

        @pl.when(i > 0)
        def _():
            @pl.when(j == 0)
            def _():
                dhin_ref[...] = dres_next[...]
                dfb = df_next[...]
                df_s[...] = dfb
                df_ref[...] = dfb

            look_ahead()
            for part in range(FFN_SPLIT):
                rows = pl.ds(part * (tm // FFN_SPLIT), tm // FFN_SPLIT)
                dact = _mm_nt(df_s[rows, :], wd_ref[...])
                g = gb_ref[rows, :].astype(F32)
                u = ub_ref[rows, :].astype(F32)
                sg, sl = _silu_parts(g)
                dg = (dact * u * (sg * (1.0 + g * (1.0 - sg)))).astype(BF16)
                du = (dact * sl).astype(BF16)
                dg_ref[rows, :] = dg
                du_ref[rows, :] = du
                act_ref[rows, :] = (sl * u).astype(BF16)
                dhin_ref[rows, :] += _mm(dg, wg_ref[...]) + _mm(du, wu_ref[...])

    ahead = lambda i, j: (jnp.minimum(i, nt - 1), 0)
    row = lambda i, j: (jnp.maximum(i - 1, 0), 0)
    chunk = lambda i, j: (jnp.where(i > 0, j, 0), 0, 0)
    col = pl.BlockSpec((None, tm, tf), lambda i, j: (jnp.where(i > 0, j, 0), jnp.maximum(i - 1, 0), 0))
    vec = pl.BlockSpec((1, d), lambda i, j: (0, 0))
    ff = jax.ShapeDtypeStruct((nj, t, tf), BF16)
    return _tc_call(
        body, name=name, grid=(nt + 1, nj),
        in_specs=[pl.BlockSpec((tm, d), ahead), pl.BlockSpec((tm, d), ahead), pl.BlockSpec((tm, 1), ahead), vec,
                  col, col] + [pl.BlockSpec((None, tf, d), chunk)] * 3,
        out_specs=[pl.BlockSpec((tm, d), row), pl.BlockSpec((tm, d), row), col, col, col, vec, vec],
        out_shape=[jax.ShapeDtypeStruct((t, d), F32), jax.ShapeDtypeStruct((t, d), BF16), ff, ff, ff,
                   jax.ShapeDtypeStruct((1, d), F32), jax.ShapeDtypeStruct((1, d), F32)],
        scratch_shapes=[pltpu.VMEM((tm, d), BF16), pltpu.VMEM((tm, d), F32), pltpu.VMEM((tm, d), BF16)],
        compiler_params=_seq(2),
    )(dh, xhat, rstd, ln_g, gb, ub, wg, wu, wd)


def _wgrad(a, b, tmm, tn, name, after=None):
    t = a.shape[-2]
    m = a.shape[-1] * (a.shape[0] if a.ndim == 3 else 1)
    n = b.shape[-1] * (b.shape[0] if b.ndim == 3 else 1)
    tk = max(k for k in range(BF16_ROWS, WGRAD_K_MAX + 1, BF16_ROWS) if t % k == 0)
    nk = t // tk
    extra = [] if after is None else [after]

    def body(a_ref, b_ref, *rest):
        o_ref, acc = rest[len(extra):]
        k = pl.program_id(2)

        @pl.when(k == 0)
        def _():
            acc[...] = jnp.zeros_like(acc)

        acc[...] += _mm_tn(a_ref[...], b_ref[...])

        @pl.when(k == nk - 1)
        def _():
            o_ref[...] = acc[...].astype(o_ref.dtype)

    a_spec = (pl.BlockSpec((None, tk, tmm), lambda i, j, k: (i, k, 0)) if a.ndim == 3
              else pl.BlockSpec((tk, tmm), lambda i, j, k: (k, i)))
    return _tc_call(
        body, name=name, grid=(m // tmm, n // tn, nk),
        in_specs=[a_spec, pl.BlockSpec((None, tk, tn), lambda i, j, k: (j, k, 0)) if b.ndim == 3
                  else pl.BlockSpec((tk, tn), lambda i, j, k: (k, j))] + [pl.BlockSpec(memory_space=pl.ANY)] * len(extra),
        out_specs=pl.BlockSpec((tmm, tn), lambda i, j, k: (i, j)),
        out_shape=jax.ShapeDtypeStruct((m, n), BF16),
        scratch_shapes=[pltpu.VMEM((tmm, tn), F32)],
        compiler_params=_seq(3),
    )(a, b, *extra)


def _inproj_fwd(xhat, gam, bet, w_main, w_lr, wgu, b_gate, widths, name):
    t, d = xhat.shape
    tm = _row_tile(t)
    kw = wgu.shape[1]
    offs = [0]
    for w in widths:
        offs.append(offs[-1] + w)

    def body(x_ref, g_ref, b_ref, wm_ref, wl_ref, wgu_ref, bg_ref, *outs):
        piece_refs, (zg_ref, la_ref, hb_ref) = outs[:len(widths)], outs[len(widths):]
        hb = (x_ref[...] * g_ref[...] + b_ref[...]).astype(BF16)
        hb_ref[...] = hb
        for p, ref in enumerate(piece_refs):
            ref[...] = _mm_nt(hb, wm_ref[offs[p]:offs[p + 1], :])
        zg = _mm_nt(hb, wl_ref[...])
        zg_ref[...] = zg
        logit = _mm(zg, wgu_ref[...]) + bg_ref[...]
        la_ref[...] = (jnp.minimum(logit, 0.0) - jnp.log(1.0 + jnp.exp(-jnp.abs(logit)))) * (1.0 / GLA_GATE_TEMP)

    row = lambda i: (i, 0)
    full = lambda a: pl.BlockSpec(a.shape, lambda i: (0,) * a.ndim)
    out_w = list(widths) + [LANE, kw]
    return _tc_call(
        body, name=name, grid=(t // tm,),
        in_specs=[pl.BlockSpec((tm, d), row), full(gam), full(bet), full(w_main), full(w_lr), full(wgu), full(b_gate)],
        out_specs=[pl.BlockSpec((tm, w), row) for w in out_w] + [pl.BlockSpec((tm, d), row)],
        out_shape=[jax.ShapeDtypeStruct((t, w), F32) for w in out_w] + [jax.ShapeDtypeStruct((t, d), BF16)],
        compiler_params=_seq(1),
    )(xhat, gam, bet, w_main, w_lr, wgu, b_gate)


def _inproj_bwd(dh_part, pieces, dzg, w_main, w_lr, name):
    t, d = dh_part.shape
    tm = _row_tile(t)
    widths = [p.shape[1] for p in pieces]
    offs = [0]
    for w in widths:
        offs.append(offs[-1] + w)

    def body(*refs):
        dhp_ref = refs[0]
        p_refs = refs[1:1 + len(widths)]
        dzg_ref, wm_ref, wl_ref, dh_ref, dz_ref = refs[1 + len(widths):]
        acc = dhp_ref[...] + _mm(dzg_ref[...], wl_ref[...])
        for p, ref in enumerate(p_refs):
            v = ref[...]
            dz_ref[:, offs[p]:offs[p + 1]] = v
            acc += _mm(v, wm_ref[offs[p]:offs[p + 1], :])
        dh_ref[...] = acc

    row = lambda i: (i, 0)
    full = lambda a: pl.BlockSpec(a.shape, lambda i: (0,) * a.ndim)
    return _tc_call(
        body, name=name, grid=(t // tm,),
        in_specs=[pl.BlockSpec((tm, d), row)] + [pl.BlockSpec((tm, w), row) for w in widths]
                 + [pl.BlockSpec((tm, LANE), row), full(w_main), full(w_lr)],
        out_specs=[pl.BlockSpec((tm, d), row), pl.BlockSpec((tm, offs[-1]), row)],
        out_shape=[jax.ShapeDtypeStruct((t, d), F32), jax.ShapeDtypeStruct((t, offs[-1]), BF16)],
        compiler_params=_seq(1),
    )(dh_part, *pieces, dzg, w_main, w_lr)


def _pool_cnt(tile, tm, w):
    t = tile * tm + lax.broadcasted_iota(jnp.int32, (tm, 1), 0)
    return jnp.minimum(t + 1, w).astype(F32)


def _pool_fwd(u, wp, scale, name):
    t, pw = u.shape
    tm = _row_tile(t)
    gd = wp.shape[1]

    def body(u_ref, wp_ref, sc_ref, y_ref, p_ref, ext):
        i = pl.program_id(0)

        @pl.when(i == 0)
        def _():
            ext[0:POOL_HALO, :] = jnp.zeros((POOL_HALO, pw), F32)

        ext[POOL_HALO:POOL_HALO + tm, :] = u_ref[...]
        for gi, w in enumerate(POOL_WINDOWS):
            cols = slice(gi * gd, (gi + 1) * gd)
            s = ext[pl.ds(POOL_HALO, tm), cols]
            tot = s
            for back in range(1, w):
                tot = tot + ext[pl.ds(POOL_HALO - back, tm), cols]
            p = (tot / _pool_cnt(i, tm, w) - s).astype(BF16)
            p_ref[:, cols] = p
            y_ref[:, cols] = (jnp.dot(p, wp_ref[gi], preferred_element_type=F32) * sc_ref[:, cols]).astype(BF16)
        ext[0:POOL_HALO, :] = ext[tm:tm + POOL_HALO, :]

    row = lambda i: (i, 0)
    return _tc_call(
        body, name=name, grid=(t // tm,),
        in_specs=[pl.BlockSpec((tm, pw), row), pl.BlockSpec(wp.shape, lambda i: (0, 0, 0)),
                  pl.BlockSpec((1, pw), lambda i: (0, 0))],
        out_specs=[pl.BlockSpec((tm, pw), row), pl.BlockSpec((tm, pw), row)],
        out_shape=[jax.ShapeDtypeStruct((t, pw), BF16), jax.ShapeDtypeStruct((t, pw), BF16)],
        scratch_shapes=[pltpu.VMEM((tm + POOL_HALO, pw), F32)],
        compiler_params=_seq(1),
    )(u, wp, scale)


def _pool_bwd(dy, pb, wp, scale, name):
    t, pw = dy.shape
    tm = _row_tile(t)
    nt = t // tm
    gd = wp.shape[1]

    def body(dy_ref, p_ref, wp_ref, sc_ref, du_ref, dwp_ref, dsc_ref, ext):
        i = pl.program_id(0)
        tile = nt - 1 - i

        @pl.when(i == 0)
        def _():
            ext[tm:tm + POOL_HALO, :] = jnp.zeros((POOL_HALO, pw), F32)
            dwp_ref[...] = jnp.zeros_like(dwp_ref)
            dsc_ref[...] = jnp.zeros_like(dsc_ref)

        dps = []
        for gi, w in enumerate(POOL_WINDOWS):
            cols = slice(gi * gd, (gi + 1) * gd)
            dyv = dy_ref[:, cols]
            p = p_ref[:, cols]
            dpre = (dyv * sc_ref[:, cols]).astype(BF16)
            dsc_ref[:, cols] += jnp.sum(dyv * jnp.dot(p, wp_ref[gi], preferred_element_type=F32), axis=0, keepdims=True)
            dwp_ref[gi] += _mm_tn(p, dpre)
            dp = _mm_nt(dpre, wp_ref[gi])
            dps.append(dp)
            ext[0:tm, cols] = dp / _pool_cnt(tile, tm, w)
        for gi, w in enumerate(POOL_WINDOWS):
            cols = slice(gi * gd, (gi + 1) * gd)
            tot = ext[pl.ds(0, tm), cols]
            for fwd in range(1, w):
                tot = tot + ext[pl.ds(fwd, tm), cols]
            du_ref[:, cols] = (tot - dps[gi]).astype(BF16)
        ext[tm:tm + POOL_HALO, :] = ext[0:POOL_HALO, :]

    row = lambda i: (nt - 1 - i, 0)
    return _tc_call(
        body, name=name, grid=(nt,),
        in_specs=[pl.BlockSpec((tm, pw), row), pl.BlockSpec((tm, pw), row),
                  pl.BlockSpec(wp.shape, lambda i: (0, 0, 0)), pl.BlockSpec((1, pw), lambda i: (0, 0))],
        out_specs=[pl.BlockSpec((tm, pw), row), pl.BlockSpec(wp.shape, lambda i: (0, 0, 0)),
                   pl.BlockSpec((1, pw), lambda i: (0, 0))],
        out_shape=[jax.ShapeDtypeStruct((t, pw), BF16), jax.ShapeDtypeStruct(wp.shape, F32),
                   jax.ShapeDtypeStruct((1, pw), F32)],
        scratch_shapes=[pltpu.VMEM((tm + POOL_HALO, pw), F32)],
        compiler_params=_seq(1),
    )(dy, pb, wp, scale)


def _gla_masks(kw, vw):
    dk, dv = kw // N_HEADS, vw // N_HEADS
    lane_k = lax.broadcasted_iota(jnp.int32, (1, kw), 1)
    lane_v = lax.broadcasted_iota(jnp.int32, (1, vw), 1)
    hk = [((lane_k >= h * dk) & (lane_k < (h + 1) * dk)).astype(F32) for h in range(N_HEADS)]
    hv = [((lane_v >= h * dv) & (lane_v < (h + 1) * dv)).astype(F32) for h in range(N_HEADS)]
    r = lax.broadcasted_iota(jnp.int32, (CHUNK, CHUNK), 0)
    c = lax.broadcasted_iota(jnp.int32, (CHUNK, CHUNK), 1)
    tril = r >= c
    rs = lax.broadcasted_iota(jnp.int32, (N_HEADS * CHUNK, CHUNK), 0) & (CHUNK - 1)
    stril = rs >= lax.broadcasted_iota(jnp.int32, (N_HEADS * CHUNK, CHUNK), 1)
    return hk, hv, tril, stril


def _block_diag(x, hk, dv):
    return jnp.concatenate([x[h * dv:(h + 1) * dv, :] * hk[h] for h in range(N_HEADS)], axis=0)


def _gla_fwd(q, k, v, loga, r, gnorm, name):
    t, kw = q.shape
    vw = v.shape[1]
    dk, dv = kw // N_HEADS, vw // N_HEADS
    tm = _row_tile(t)
    nc = tm // CHUNK
    qscale = dk ** -0.5

    def body(q_ref, k_ref, v_ref, la_ref, r_ref, gn_ref, o_ref, y_ref, sall_ref, st):
        @pl.when(pl.program_id(0) == 0)
        def _():
            st[...] = jnp.zeros_like(st)

        hk, hv, tril, stril = _gla_masks(kw, vw)
        trif = tril.astype(F32)

        def chunk(c, carry):
            rows = pl.ds(pl.multiple_of(c * CHUNK, CHUNK), CHUNK)
            la = la_ref[rows, :]
            b = _mm_01(trif, la)
            bl = jnp.sum(la, axis=0, keepdims=True)
            qb = q_ref[rows, :] * (qscale * jnp.exp(b))
            kk = k_ref[rows, :]
            kb = kk * jnp.exp(-b)
            kl = kk * jnp.exp(bl - b)
            vv = v_ref[rows, :]
            s_t = st[...]
            compact = s_t[0:dv, :]
            for h in range(1, N_HEADS):
                compact = compact + s_t[h * dv:(h + 1) * dv, :]
            sall_ref[c] = compact
            qx = jnp.concatenate([qb.astype(BF16) * hk[h].astype(BF16) for h in range(N_HEADS)], axis=0)
            a = jnp.where(stril, _mm_nt(qx, kb), 0.0).astype(BF16)
            o_inter = _mm_nt(qb, s_t)
            for h in range(N_HEADS):
                vs = slice(h * dv, (h + 1) * dv)
                o_ref[rows, vs] = o_inter[:, vs] + _mm(a[h * CHUNK:(h + 1) * CHUNK, :], vv[:, vs])
            st[...] = s_t * jnp.exp(bl) + _block_diag(_mm_tn(vv, kl), hk, dv)
            return carry

        lax.fori_loop(0, nc, chunk, 0, unroll=CHUNK_UNROLL)
        for h in range(N_HEADS):
            vs = slice(h * dv, (h + 1) * dv)
            oh = o_ref[:, vs]
            on = oh * lax.rsqrt(jnp.mean(oh * oh, axis=-1, keepdims=True) + RMS_EPS)
            _, sl = _silu_parts(r_ref[:, vs])
            y_ref[:, vs] = (on * gn_ref[:, vs] * sl).astype(BF16)

    row = lambda i: (i, 0)
    return _tc_call(
        body, name=name, grid=(t // tm,),
        in_specs=[pl.BlockSpec((tm, kw), row), pl.BlockSpec((tm, kw), row), pl.BlockSpec((tm, vw), row),
                  pl.BlockSpec((tm, kw), row), pl.BlockSpec((tm, vw), row), pl.BlockSpec((1, vw), lambda i: (0, 0))],
        out_specs=[pl.BlockSpec((tm, vw), row), pl.BlockSpec((tm, vw), row),
                   pl.BlockSpec((nc, dv, kw), lambda i: (i, 0, 0))],
        out_shape=[jax.ShapeDtypeStruct((t, vw), F32), jax.ShapeDtypeStruct((t, vw), BF16),
                   jax.ShapeDtypeStruct((t // CHUNK, dv, kw), F32)],
        scratch_shapes=[pltpu.VMEM((vw, kw), F32)],
        compiler_params=_seq(1),
    )(q, k, v, loga, r, gnorm)


def _gla_bwd(dy, o, r, gnorm, q, k, v, loga, zg, sall, wgu, name):
    t, kw = q.shape
    vw = v.shape[1]
    dk, dv = kw // N_HEADS, vw // N_HEADS
    tm = _row_tile(t)
    nt = t // tm
    nc = tm // CHUNK
    qscale = dk ** -0.5

    def body(dy_ref, o_ref, r_ref, gn_ref, q_ref, k_ref, v_ref, la_ref, zg_ref, sall_ref, wgu_ref,
             dq_ref, dk_ref, dv_ref, dr_ref, dzg_ref, dwgu_ref, dbg_ref, dgn_ref, dst, do_s):
        @pl.when(pl.program_id(0) == 0)
        def _():
            dst[...] = jnp.zeros_like(dst)
            dwgu_ref[...] = jnp.zeros_like(dwgu_ref)
            dbg_ref[...] = jnp.zeros_like(dbg_ref)
            dgn_ref[...] = jnp.zeros_like(dgn_ref)

        for h in range(N_HEADS):
            vs = slice(h * dv, (h + 1) * dv)
            oh = o_ref[:, vs]
            rinv = lax.rsqrt(jnp.mean(oh * oh, axis=-1, keepdims=True) + RMS_EPS)
            on = oh * rinv
            rr = r_ref[:, vs]
            sg, sl = _silu_parts(rr)
            dyv = dy_ref[:, vs]
            gn = gn_ref[:, vs]
            dgn_ref[:, vs] += jnp.sum(dyv * on * sl, axis=0, keepdims=True)
            dr_ref[:, vs] = (dyv * on * gn * (sg * (1.0 + rr * (1.0 - sg)))).astype(BF16)
            don = dyv * gn * sl
            do_s[:, vs] = rinv * (don - on * jnp.mean(don * on, axis=-1, keepdims=True))

        hk, hv, tril, stril = _gla_masks(kw, vw)
        trif = tril.astype(F32)
        triuf = (lax.broadcasted_iota(jnp.int32, (CHUNK, CHUNK), 0)
                 <= lax.broadcasted_iota(jnp.int32, (CHUNK, CHUNK), 1)).astype(F32)
        last_row = lax.broadcasted_iota(jnp.int32, (CHUNK, 1), 0) == CHUNK - 1

        def chunk(idx, carry):
            c = nc - 1 - idx
            rows = pl.ds(pl.multiple_of(c * CHUNK, CHUNK), CHUNK)
            la = la_ref[rows, :]
            b = _mm_01(trif, la)
            bl = jnp.sum(la, axis=0, keepdims=True)
            eb = jnp.exp(b)
            enb = jnp.exp(-b)
            ebl = jnp.exp(bl - b)
            el = jnp.exp(bl)
            qb = q_ref[rows, :] * (qscale * eb)
            kk = k_ref[rows, :]
            kb = kk * enb
            kl = kk * ebl
            vv = v_ref[rows, :]
            do = do_s[rows, :]
            compact = sall_ref[c]
            s_t = jnp.concatenate([compact * hk[h] for h in range(N_HEADS)], axis=0)
            ds_t = dst[...]
            qx = jnp.concatenate([qb.astype(BF16) * hk[h].astype(BF16) for h in range(N_HEADS)], axis=0)
            dox = jnp.concatenate([do.astype(BF16) * hv[h].astype(BF16) for h in range(N_HEADS)], axis=0)
            a = jnp.where(stril, _mm_nt(qx, kb), 0.0).astype(BF16)
            da = jnp.where(stril, _mm_nt(dox, vv), 0.0).astype(BF16)
            dv_ref[rows, :] = (_mm_tn(a, dox) + _mm_nt(kl, ds_t)).astype(BF16)
            dak = _mm(da, kb)
            dqb = _mm(do, s_t)
            for h in range(N_HEADS):
                dqb = dqb + dak[h * CHUNK:(h + 1) * CHUNK, :] * hk[h]
            dkb = _mm_tn(da, qx)
            dkl = _mm(vv, ds_t)
            dbl = jnp.sum(dkl * kl, axis=0, keepdims=True) + el * jnp.sum(ds_t * s_t, axis=0, keepdims=True)
            dst[...] = ds_t * el + _block_diag(_mm_tn(do, qb), hk, dv)
            dq_ref[rows, :] = (dqb * (qscale * eb)).astype(BF16)
            dk_ref[rows, :] = (dkb * enb + dkl * ebl).astype(BF16)
            db = dqb * qb - dkb * kb - dkl * kl + jnp.where(last_row, dbl, 0.0)
            dla = _mm_01(triuf, db)
            dlogit = dla * (1.0 / GLA_GATE_TEMP) * (1.0 - jnp.exp(GLA_GATE_TEMP * la))
            dzg_ref[rows, :] = _mm_nt(dlogit, wgu_ref[...]).astype(BF16)
            dwgu_ref[...] += _mm_tn(zg_ref[rows, :], dlogit)
            dbg_ref[...] += jnp.sum(dlogit, axis=0, keepdims=True)
            return carry

        lax.fori_loop(0, nc, chunk, 0, unroll=CHUNK_UNROLL)

    row = lambda i: (nt - 1 - i, 0)
    const = lambda i: (0, 0)
    return _tc_call(
        body, name=name, grid=(nt,),
        in_specs=[pl.BlockSpec((tm, vw), row), pl.BlockSpec((tm, vw), row), pl.BlockSpec((tm, vw), row),
                  pl.BlockSpec((1, vw), const), pl.BlockSpec((tm, kw), row), pl.BlockSpec((tm, kw), row),
                  pl.BlockSpec((tm, vw), row), pl.BlockSpec((tm, kw), row), pl.BlockSpec((tm, LANE), row),
                  pl.BlockSpec((nc, dv, kw), lambda i: (nt - 1 - i, 0, 0)), pl.BlockSpec((LANE, kw), const)],
        out_specs=[pl.BlockSpec((tm, kw), row), pl.BlockSpec((tm, kw), row), pl.BlockSpec((tm, vw), row),
                   pl.BlockSpec((tm, vw), row), pl.BlockSpec((tm, LANE), row), pl.BlockSpec((LANE, kw), const),
                   pl.BlockSpec((1, kw), const), pl.BlockSpec((1, vw), const)],
        out_shape=[jax.ShapeDtypeStruct((t, kw), BF16), jax.ShapeDtypeStruct((t, kw), BF16),
                   jax.ShapeDtypeStruct((t, vw), BF16), jax.ShapeDtypeStruct((t, vw), BF16),
                   jax.ShapeDtypeStruct((t, LANE), BF16), jax.ShapeDtypeStruct((LANE, kw), F32),
                   jax.ShapeDtypeStruct((1, kw), F32), jax.ShapeDtypeStruct((1, vw), F32)],
        scratch_shapes=[pltpu.VMEM((vw, kw), F32), pltpu.VMEM((tm, vw), F32)],
        compiler_params=_seq(1),
    )(dy, o, r, gnorm, q, k, v, loga, zg, sall, wgu)


def _outproj_fwd(yp, yg, w_out, xhat, gam, bet, alpha, name):
    t, d = xhat.shape
    pw = yp.shape[1]
    tm = _row_tile(t)

    def body(yp_ref, yg_ref, w_ref, x_ref, g_ref, b_ref, xhat_ref, rstd_ref):
        for part in range(ROW_GROUPS):
            rows = pl.ds(part * (tm // ROW_GROUPS), tm // ROW_GROUPS)
            h = x_ref[rows, :] * g_ref[...] + b_ref[...]
            y = (jnp.dot(yp_ref[rows, :], w_ref[0:pw, :], preferred_element_type=F32)
                 + jnp.dot(yg_ref[rows, :], w_ref[pw:, :], preferred_element_type=F32))
            xh, rs = _ln_stats(alpha * h + y)
            xhat_ref[rows, :] = xh
            rstd_ref[rows, :] = rs

    row = lambda i: (i, 0)
    vec = pl.BlockSpec((1, d), lambda i: (0, 0))
    return _tc_call(
        body, name=name, grid=(t // tm,),
        in_specs=[pl.BlockSpec((tm, pw), row), pl.BlockSpec((tm, yg.shape[1]), row),
                  pl.BlockSpec(w_out.shape, lambda i: (0, 0)), pl.BlockSpec((tm, d), row), vec, vec],
        out_specs=[pl.BlockSpec((tm, d), row), pl.BlockSpec((tm, 1), row)],
        out_shape=[jax.ShapeDtypeStruct((t, d), F32), jax.ShapeDtypeStruct((t, 1), F32)],
        compiler_params=_seq(1),
    )(yp, yg, w_out, xhat, gam, bet)


def _outproj_bwd(dh, xhat, rstd, ln_g, w_out, pw, alpha, name):
    t, d = dh.shape
    tm = _row_tile(t)
    gw = w_out.shape[0] - pw

    def body(dh_ref, xh_ref, rs_ref, g_ref, w_ref, dyb_ref, dyp_ref, dyg_ref, dres_ref, dgam_ref, dbet_ref):
        @pl.when(pl.program_id(0) == 0)
        def _():
            dgam_ref[...] = jnp.zeros_like(dgam_ref)
            dbet_ref[...] = jnp.zeros_like(dbet_ref)

        for part in range(ROW_GROUPS):
            rows = pl.ds(part * (tm // ROW_GROUPS), tm // ROW_GROUPS)
            dy = dh_ref[rows, :]
            xh = xh_ref[rows, :]
            dr = _ln_bwd(dy, xh, rs_ref[rows, :], g_ref[...])
            dgam_ref[...] += jnp.sum(dy * xh, axis=0, keepdims=True)
            dbet_ref[...] += jnp.sum(dy, axis=0, keepdims=True)
            drb = dr.astype(BF16)
            dyb_ref[rows, :] = drb
            dres_ref[rows, :] = alpha * dr
            dyp_ref[rows, :] = _mm_nt(drb, w_ref[0:pw, :])
            dyg_ref[rows, :] = _mm_nt(drb, w_ref[pw:, :])

    row = lambda i: (i, 0)
    vec = pl.BlockSpec((1, d), lambda i: (0, 0))
    return _tc_call(
        body, name=name, grid=(t // tm,),
        in_specs=[pl.BlockSpec((tm, d), row), pl.BlockSpec((tm, d), row), pl.BlockSpec((tm, 1), row), vec,
                  pl.BlockSpec(w_out.shape, lambda i: (0, 0))],
        out_specs=[pl.BlockSpec((tm, d), row), pl.BlockSpec((tm, pw), row), pl.BlockSpec((tm, gw), row),
                   pl.BlockSpec((tm, d), row), vec, vec],
        out_shape=[jax.ShapeDtypeStruct((t, d), BF16), jax.ShapeDtypeStruct((t, pw), F32),
                   jax.ShapeDtypeStruct((t, gw), F32), jax.ShapeDtypeStruct((t, d), F32),
                   jax.ShapeDtypeStruct((1, d), F32), jax.ShapeDtypeStruct((1, d), F32)],
        compiler_params=_seq(1),
    )(dh, xhat, rstd, ln_g, w_out)


def _rows_block(r, c):
    best = r
    for cand in range(BF16_ROWS, r, BF16_ROWS):
        if r % cand == 0 and cand * c * 4 <= (1 << 20):
            best = cand
    return best if best * c * 4 <= (4 << 20) else r


def _sum_slots(me, mine, recvs, name, layers_side_by_side=False):
    nl = len(recvs)
    ns, r, c = recvs[0].shape
    tr = _rows_block(r, c)

    def body(me_ref, *refs):
        o_ref = refs[nl * ns]
        for l in range(nl):
            acc = refs[l * ns][...].astype(F32)
            for s in range(1, ns):
                acc = acc + refs[l * ns + s][...].astype(F32)
            if layers_side_by_side:
                o_ref[0, :, l * c:(l + 1) * c] = acc.astype(o_ref.dtype)
            else:
                o_ref[l] = acc.astype(o_ref.dtype)

    def slot(s):
        return pl.BlockSpec((None, tr, c), lambda i, me_ref: ((me_ref[0] + s) % ns, i, 0))

    out = (1, r, nl * c) if layers_side_by_side else (nl, r, c)
    operands = []
    for l in range(nl):
        operands += [mine[l]] + [recvs[l]] * (ns - 1)
    return _tc_call(
        body, name=name,
        grid_spec=pltpu.PrefetchScalarGridSpec(
            num_scalar_prefetch=1, grid=(r // tr,),
            in_specs=[slot(s) for s in range(ns)] * nl,
            out_specs=pl.BlockSpec((out[0], tr, out[2]), lambda i, me_ref: (0, i, 0))),
        out_shape=jax.ShapeDtypeStruct(out, recvs[0].dtype),
        compiler_params=_seq(1),
    )(me, *operands)


def _adamw(w, terms, m, v, name):
    nl, r, c = w.shape
    tc = c
    while tc % (2 * LANE) == 0 and tc > 4 * LANE:
        tc //= 2
    tr = _rows_block(r, tc)
    nterm = len(terms)

    def body(*refs):
        w_ref = refs[0]
        t_refs = refs[1:1 + nterm]
        m_ref, v_ref, g_ref, d_ref, nm_ref, nv_ref = refs[1 + nterm:]
        g = t_refs[0][...].astype(F32)
        for tr_ in t_refs[1:]:
            g = g + tr_[...].astype(F32)
        nm = ADAM_B1 * m_ref[...] + (1.0 - ADAM_B1) * g
        nv = ADAM_B2 * v_ref[...] + (1.0 - ADAM_B2) * jnp.square(g)
        m_hat = nm / (1.0 - ADAM_B1 ** ADAM_STEP)
        v_hat = nv / (1.0 - ADAM_B2 ** ADAM_STEP)
        g_ref[...] = g
        d_ref[...] = -ADAM_LR * (m_hat / (jnp.sqrt(v_hat) + ADAM_EPS) + ADAM_WD * w_ref[...])
        nm_ref[...] = nm
        nv_ref[...] = nv

    spec = pl.BlockSpec((None, tr, tc), lambda l, i, j: (l, i, j))
    shp = jax.ShapeDtypeStruct((nl, r, c), F32)
    return _tc_call(
        body, name=name, grid=(nl, r // tr, c // tc),
        in_specs=[spec] * (3 + nterm), out_specs=[spec] * 4, out_shape=[shp] * 4,
        compiler_params=_seq(3),
    )(w, *terms, m, v)


XY_RELATIONS = ((1, 0, 0), (0, 1, 0), (1, 1, 0))
ALL_RELATIONS = tuple((fx, fy, fc) for fx in (0, 1) for fy in (0, 1) for fc in (0, 1) if fx or fy or fc)
HBM_SPEC = pl.BlockSpec(memory_space=pltpu.HBM)
SEM_SPEC = pl.BlockSpec(memory_space=pltpu.SEMAPHORE)
DATAFLOW = pltpu.SideEffectType.DATAFLOW_SIDE_EFFECTING


def _split_call(body, **kw):
    return pl.pallas_call(body, **kw)


def _flip(v, f):
    return 1 - v if f else v


def _any_spec(n):
    return [pl.BlockSpec(memory_space=pl.ANY)] * n


def _relations(kind):
    return ALL_RELATIONS if kind == "bcast" else XY_RELATIONS


def _copies(kind, arr, land, sems):
    x, y, c = lax.axis_index("x"), lax.axis_index("y"), lax.axis_index("c")
    out = []
    for (fx, fy, fc), (send_sem, recv_sem) in zip(_relations(kind), sems):
        px, py, pc = _flip(x, fx), _flip(y, fy), _flip(c, fc)
        if kind == "bcast":
            mine, theirs = 4 * x + 2 * y + c, 4 * px + 2 * py + pc
        else:
            mine, theirs = 2 * x + y, 2 * px + py
        src, to_mine, to_theirs = arr, land.at[mine], land.at[theirs]
        if kind == "scatter":
            src = arr.at[theirs]
        if kind == "gather_half":
            rows = _my_half(arr.shape[0], c)
            src, to_mine, to_theirs = arr.at[rows], land.at[mine, rows], land.at[theirs, rows]
        both = dict(src_ref=src, send_sem=send_sem, recv_sem=recv_sem, device_id=(px, py, pc), device_id_type=MESH)
        out.append((pltpu.make_async_remote_copy(dst_ref=to_mine, **both),
                    pltpu.make_async_remote_copy(dst_ref=to_theirs, **both)))
    return out


def _my_half(nrows, c):
    return pl.ds(c * (nrows // 2), nrows // 2)


def _share_halves(name, kinds, lands):
    ks = [k for k, kd in enumerate(kinds) if kd == "gather_half"]
    n = len(ks)

    def body(*refs):
        l_refs = refs[n:2 * n]
        send_sems, recv_sems = refs[2 * n:]
        x, y, c = lax.axis_index("x"), lax.axis_index("y"), lax.axis_index("c")
        copies = []
        for i in range(n):
            nrows = l_refs[i].shape[1]
            for r, (fx, fy, _) in enumerate(XY_RELATIONS):
                slot = 2 * _flip(x, fx) + _flip(y, fy)
                both = dict(src_ref=l_refs[i].at[slot, _my_half(nrows, c)], send_sem=send_sems.at[i, r],
                            recv_sem=recv_sems.at[i, r], device_id=(x, y, 1 - c), device_id_type=MESH)
                copies.append((pltpu.make_async_remote_copy(dst_ref=l_refs[i].at[slot, _my_half(nrows, c)], **both),
                               pltpu.make_async_remote_copy(dst_ref=l_refs[i].at[slot, _my_half(nrows, 1 - c)], **both)))
        for send, _ in copies:
            send.start()
        for _, arrival in copies:
            arrival.wait_recv()
        for send, _ in copies:
            send.wait_send()

    outs = _comm_call(
        body, name=name,
        in_specs=_any_spec(n), out_specs=_any_spec(n),
        out_shape=[jax.ShapeDtypeStruct(lands[k].shape, lands[k].dtype) for k in ks],
        input_output_aliases={i: i for i in range(n)},
        scratch_shapes=[pltpu.SemaphoreType.DMA((n, 3)), pltpu.SemaphoreType.DMA((n, 3))],
    )(*[lands[k] for k in ks])
    lands = list(lands)
    for k, o in zip(ks, outs):
        lands[k] = o
    return lands


def _sem_pairs(kinds, sems):
    out, at = [], 0
    for kind in kinds:
        nrel = len(_relations(kind))
        out.append([(sems[at + 2 * r], sems[at + 2 * r + 1]) for r in range(nrel)])
        at += 2 * nrel
    return out


def _exchange_start(name, kinds, arrs, lands):
    n = len(arrs)
    nsem = sum(2 * len(_relations(kd)) for kd in kinds)

    def body(*refs):
        a_refs, l_refs = refs[:n], refs[n:2 * n]
        pairs = _sem_pairs(kinds, refs[2 * n:2 * n + nsem])
        token = refs[-1]
        for k in range(n):
            for send, _ in _copies(kinds[k], a_refs[k], l_refs[k], pairs[k]):
                send.start()
        token[...] = jnp.zeros_like(token)

    thru = [pltpu.HBM(a.shape, a.dtype) for a in list(arrs) + list(lands)]
    outs = _split_call(
        body, name=name,
        out_shape=(*[pltpu.SemaphoreType.DMA(())] * nsem, *thru, jax.ShapeDtypeStruct((8, LANE), F32)),
        in_specs=[HBM_SPEC] * (2 * n),
        out_specs=(*[SEM_SPEC] * nsem, *[HBM_SPEC] * (2 * n), pl.BlockSpec(memory_space=pltpu.VMEM)),
        input_output_aliases={i: nsem + i for i in range(2 * n)},
        compiler_params=pltpu.CompilerParams(has_side_effects=DATAFLOW),
    )(*[pltpu.with_memory_space_constraint(a, pltpu.HBM) for a in list(arrs) + list(lands)])
    return dict(kinds=kinds, sems=outs[:nsem], arrs=outs[nsem:nsem + n], lands=outs[nsem + n:nsem + 2 * n],
                token=outs[-1])


def _exchange_wait(name, st, after):
    kinds = st["kinds"]
    n = len(kinds)
    nsem = len(st["sems"])

    def body(*refs):
        a_refs, l_refs = refs[:n], refs[n:2 * n]
        pairs = _sem_pairs(kinds, refs[2 * n:2 * n + nsem])
        for k in range(n):
            for _, arrival in _copies(kinds[k], a_refs[k], l_refs[k], pairs[k]):
                arrival.wait_send()
                arrival.wait_recv()
        refs[-1][...] = jnp.zeros_like(refs[-1])

    ins = list(st["arrs"]) + list(st["lands"])
    outs = _split_call(
        body, name=name,
        out_shape=[pltpu.HBM(a.shape, a.dtype) for a in ins] + [jax.ShapeDtypeStruct((8, LANE), F32)],
        in_specs=[HBM_SPEC] * (2 * n) + [SEM_SPEC] * nsem + [pl.BlockSpec(memory_space=pl.ANY)],
        out_specs=[HBM_SPEC] * (2 * n) + [pl.BlockSpec(memory_space=pltpu.VMEM)],
        input_output_aliases={i: i for i in range(2 * n)},
        compiler_params=pltpu.CompilerParams(has_side_effects=DATAFLOW),
    )(*ins, *st["sems"], after)
    return outs[:n], outs[n:2 * n], outs[-1]


def _landing(own, slot, nslot):
    return lax.dynamic_update_slice(lax.empty((nslot,) + own.shape, own.dtype), own[None], (slot,) + (0,) * own.ndim)


def _swap_sibling(parts, name):
    n = len(parts)

    def body(*refs):
        ins, outs = refs[:n], refs[n:2 * n]
        send_sems, recv_sems = refs[2 * n:]
        sib = (lax.axis_index("x"), lax.axis_index("y"), 1 - lax.axis_index("c"))
        cps = [pltpu.make_async_remote_copy(src_ref=ins[k], dst_ref=outs[k], send_sem=send_sems.at[k],
                                            recv_sem=recv_sems.at[k], device_id=sib, device_id_type=MESH)
               for k in range(n)]
        for cp in cps:
            cp.start()
        for cp in cps:
            cp.wait_recv()
        for cp in cps:
            cp.wait_send()

    return _comm_call(
        body, name=name,
        in_specs=_any_spec(n), out_specs=_any_spec(n),
        out_shape=[jax.ShapeDtypeStruct(p.shape, p.dtype) for p in parts],
        scratch_shapes=[pltpu.SemaphoreType.DMA((n,)), pltpu.SemaphoreType.DMA((n,))],
    )(*parts)


def _col_shards(a, n=N_SHARD):
    r, c = a.shape
    return a.reshape(r, n, c // n).transpose(1, 0, 2)


def _from_col_shards(a):
    n, r, cs = a.shape
    return a.transpose(1, 0, 2).reshape(r, n * cs)


def kernel(x, meta_tokens, ffn1_w_gate, ffn1_w_up, ffn1_w_down, ln1_g, ln1_b, w_in, w_gate_up, b_gate, w_pool, pool_scale, gla_norm_g, w_out, ln2_g, ln2_b, ffn2_w_gate, ffn2_w_up, ffn2_w_down, ln3_g, ln3_b, loss_target, m_meta_tokens, m_ffn1_w_gate, m_ffn1_w_up, m_ffn1_w_down, m_ln1_g, m_ln1_b, m_w_in, m_w_gate_up, m_b_gate, m_w_pool, m_pool_scale, m_gla_norm_g, m_w_out, m_ln2_g, m_ln2_b, m_ffn2_w_gate, m_ffn2_w_up, m_ffn2_w_down, m_ln3_g, m_ln3_b, v_meta_tokens, v_ffn1_w_gate, v_ffn1_w_up, v_ffn1_w_down, v_ln1_g, v_ln1_b, v_w_in, v_w_gate_up, v_b_gate, v_w_pool, v_pool_scale, v_gla_norm_g, v_w_out, v_ln2_g, v_ln2_b, v_ffn2_w_gate, v_ffn2_w_up, v_ffn2_w_down, v_ln3_g, v_ln3_b):
    w = dict(meta_tokens=meta_tokens, ffn1_w_gate=ffn1_w_gate, ffn1_w_up=ffn1_w_up, ffn1_w_down=ffn1_w_down,
             ln1_g=ln1_g, ln1_b=ln1_b, w_in=w_in, w_gate_up=w_gate_up, b_gate=b_gate, w_pool=w_pool,
             pool_scale=pool_scale, gla_norm_g=gla_norm_g, w_out=w_out, ln2_g=ln2_g, ln2_b=ln2_b,
             ffn2_w_gate=ffn2_w_gate, ffn2_w_up=ffn2_w_up, ffn2_w_down=ffn2_w_down, ln3_g=ln3_g, ln3_b=ln3_b)
    mom1 = dict(meta_tokens=m_meta_tokens, ffn1_w_gate=m_ffn1_w_gate, ffn1_w_up=m_ffn1_w_up,
                ffn1_w_down=m_ffn1_w_down, ln1_g=m_ln1_g, ln1_b=m_ln1_b, w_in=m_w_in, w_gate_up=m_w_gate_up,
                b_gate=m_b_gate, w_pool=m_w_pool, pool_scale=m_pool_scale, gla_norm_g=m_gla_norm_g, w_out=m_w_out,
                ln2_g=m_ln2_g, ln2_b=m_ln2_b, ffn2_w_gate=m_ffn2_w_gate, ffn2_w_up=m_ffn2_w_up,
                ffn2_w_down=m_ffn2_w_down, ln3_g=m_ln3_g, ln3_b=m_ln3_b)
    mom2 = dict(meta_tokens=v_meta_tokens, ffn1_w_gate=v_ffn1_w_gate, ffn1_w_up=v_ffn1_w_up,
                ffn1_w_down=v_ffn1_w_down, ln1_g=v_ln1_g, ln1_b=v_ln1_b, w_in=v_w_in, w_gate_up=v_w_gate_up,
                b_gate=v_b_gate, w_pool=v_w_pool, pool_scale=v_pool_scale, gla_norm_g=v_gla_norm_g, w_out=v_w_out,
                ln2_g=v_ln2_g, ln2_b=v_ln2_b, ffn2_w_gate=v_ffn2_w_gate, ffn2_w_up=v_ffn2_w_up,
                ffn2_w_down=v_ffn2_w_down, ln3_g=v_ln3_g, ln3_b=v_ln3_b)

    xs = x[0]
    s_len, d = xs.shape
    nl = ln1_g.shape[0]
    alpha = (2.0 * nl) ** 0.25
    t_real = N_META + s_len
    t_pad = -(-t_real // LANE) * LANE
    pw = pool_scale.shape[1]
    kw = b_gate.shape[1]
    vw = gla_norm_g.shape[1]
    rank = w_gate_up.shape[1]
    widths = (pw, kw, kw, vw, vw)
    n_main = sum(widths)
    dff_s = ffn1_w_gate.shape[2]
    dff_c = N_SHARD * dff_s // FFN_CHUNKS

    me_xy = 2 * lax.axis_index("x") + lax.axis_index("y")
    me_all = 2 * me_xy + lax.axis_index("c")
    ffn1_names = ("ffn1_w_gate", "ffn1_w_up", "ffn1_w_down")
    mix_names = ("w_out", "w_gate_up", "w_in")
    ffn2_names = ("ffn2_w_gate", "ffn2_w_up", "ffn2_w_down")

    gate_up = ("ffn1_w_gate", "ffn1_w_up", "ffn2_w_gate", "ffn2_w_up")

    def stored(n, a):
        if n in gate_up:
            return jnp.swapaxes(a, 1, 2)
        return jnp.transpose(a, (2, 0, 1)) if n == "w_in" else a

    def as_given(n, a):
        if n in gate_up:
            return jnp.swapaxes(a, 1, 2)
        if n == "w_in":
            return jnp.transpose(a.reshape(-1, nl, d), (1, 2, 0))
        return a.reshape(w[n].shape)

    stages = [[("meta_tokens", None)], [(n, 0) for n in ffn1_names], [(n, 0) for n in mix_names + ffn2_names]]
    stages += [[(n, l) for n in BIG] for l in range(1, nl)]
    gathers, wa = {}, {}

    halved = ffn1_names + ffn2_names + ("w_out",)

    def start_gather(si, dep=None):
        own = []
        for n, l in stages[si]:
            a = meta_tokens if l is None else (stored(n, w[n])[:, l] if n == "w_in" else stored(n, w[n])[l])
            a = a if dep is None else a + dep
            own.append(a if l is None else a.astype(BF16))
        gathers[si] = _exchange_start(f"gather_start_{si}", ["gather_half" if n in halved else "gather"
                                                             for n, _ in stages[si]], own,
                                      [_landing(a, me_xy, N_SHARD) for a in own])
        return gathers[si]["token"]

    def arrive(si, after):
        _, lands, token = _exchange_wait(f"gather_wait_{si}", gathers[si], after)
        if any(kd == "gather_half" for kd in gathers[si]["kinds"]):
            lands = _share_halves(f"gather_share_{si}", gathers[si]["kinds"], lands)
        for item, a in zip(stages[si], lands):
            wa[item] = a.reshape(FFN_CHUNKS, -1, d) if item[0] in ffn1_names + ffn2_names else a
        return token

    def mixer_weights(l):
        wi = wa["w_in", l].reshape(-1, d)
        return dict(w_main=wi[:n_main], w_lr=jnp.pad(wi[n_main:], ((0, LANE - rank), (0, 0))),
                    wgu=jnp.pad(_from_col_shards(wa["w_gate_up", l]), ((0, LANE - rank), (0, 0))),
                    wout=wa["w_out", l].reshape(-1, d))

    wp16 = w_pool.astype(BF16)
    ones = jnp.ones((1, d), F32)
    zeros = jnp.zeros((1, d), F32)
    target = jnp.concatenate([jnp.zeros((N_META, d), F32), loss_target[0], jnp.zeros((t_pad - t_real, d), F32)], axis=0)

    started = start_gather(0)
    for si in range(1, len(stages)):
        started = start_gather(si, started[0:1, 0:1])
    arrive(0, started)
    meta_full = _from_col_shards(wa["meta_tokens", None])
    h0 = jnp.concatenate([meta_full, xs, jnp.zeros((t_pad - t_real, d), F32)], axis=0)
    arrive(1, h0[:8, :LANE] + target[:8, :LANE])

    saved, mw = [], []
    cur, cur_g, cur_b = h0, ones, zeros
    for l in range(nl):
        s = {}
        xh1, rs1, hb0, g1, u1 = _ffn_fwd(cur, cur_g, cur_b, wa["ffn1_w_gate", l], wa["ffn1_w_up", l],
                                         wa["ffn1_w_down", l], alpha, f"ffn1_fwd_{l}")
        if l == 0:
            arrive(2, xh1)
        mw.append(mixer_weights(l))
        up, q, k, v, r, zg, la, hb1 = _inproj_fwd(xh1, ln1_g[l:l + 1], ln1_b[l:l + 1], mw[l]["w_main"], mw[l]["w_lr"],
                                                  mw[l]["wgu"], b_gate[l:l + 1], widths, f"inproj_fwd_{l}")
        yp, pb = _pool_fwd(up, wp16[l], pool_scale[l:l + 1], f"pool_fwd_{l}")
        o, yg, sall = _gla_fwd(q, k, v, la, r, gla_norm_g[l:l + 1], f"gla_fwd_{l}")
        xh2, rs2 = _outproj_fwd(yp, yg, mw[l]["wout"], xh1, ln1_g[l:l + 1], ln1_b[l:l + 1], alpha, f"outproj_fwd_{l}")
        if l + 1 < nl:
            arrive(l + 3, xh2)
        head = (ln3_g[l:l + 1], ln3_b[l:l + 1], target, s_len) if l == nl - 1 else None
        xh3, rs3, hb2, g2, u2, *at_head = _ffn_fwd(xh2, ln2_g[l:l + 1], ln2_b[l:l + 1], wa["ffn2_w_gate", l],
                                                   wa["ffn2_w_up", l], wa["ffn2_w_down", l], alpha, f"ffn2_fwd_{l}",
                                                   head)
        s.update(xh1=xh1, rs1=rs1, hb0=hb0, g1=g1, u1=u1, q=q, k=k, v=v, r=r, zg=zg, la=la, hb1=hb1, yp=yp, pb=pb,
                 o=o, yg=yg, sall=sall, xh2=xh2, rs2=rs2, xh3=xh3, rs3=rs3, hb2=hb2, g2=g2, u2=u2)
        saved.append(s)
        cur, cur_g, cur_b = xh3, ln3_g[l:l + 1], ln3_b[l:l + 1]

    dh, loss_acc = at_head
    loss = lax.psum(loss_acc[0, 0], ("x", "y", "c"))

    small_grads = {n: [None] * nl for n in SMALL}
    scatters = []

    def depart(name, items, grads, kinds=None):
        lands = [_landing(g, me_all, N_DEV) if kd == "bcast" else lax.empty(g.shape, g.dtype)
                 for g, kd in zip(grads, kinds or ["scatter"] * len(grads))]
        st = _exchange_start(name, kinds or ["scatter"] * len(grads), grads, lands)
        scatters.append((name, items, st))
        return st["token"]

    def pack(parts):
        flat = jnp.concatenate([parts[n].reshape(-1) for n in SMALL])
        return flat.reshape(-1, LANE)

    def ffn_wgrad(n, l, hb, dgb, dub, act, dfb, after=None):
        if n.endswith("down"):
            dw = _wgrad(act, dfb, dff_c, d, f"{n}_grad_{l}", after)
        else:
            dw = _wgrad(dgb if n.endswith("gate") else dub, hb, dff_c, d, f"{n}_grad_{l}", after)
        return dw.reshape(N_SHARD, dff_s, d)

    late = []
    for l in reversed(range(nl)):
        s = saved[l]
        dh, dfb, dgb, dub, act, dgam, dbet = _ffn_bwd(dh, s["xh3"], s["rs3"], ln3_g[l:l + 1], s["g2"], s["u2"],
                                                      wa["ffn2_w_gate", l], wa["ffn2_w_up", l], wa["ffn2_w_down", l],
                                                      alpha, f"ffn2_bwd_{l}")
        small_grads["ln3_g"][l], small_grads["ln3_b"][l] = dgam, dbet
        gone = depart(f"scatter_start_ffn2_{l}", [(n, l) for n in ffn2_names],
                      [ffn_wgrad(n, l, s["hb2"], dgb, dub, act, dfb) for n in ffn2_names])

        dyb, dyp, dyg, dres, dgam, dbet = _outproj_bwd(dh, s["xh2"], s["rs2"], ln2_g[l:l + 1] + gone[0:1, 0:1],
                                                       mw[l]["wout"], pw, alpha, f"outproj_bwd_{l}")
        small_grads["ln2_g"][l], small_grads["ln2_b"][l] = dgam, dbet
        dwo = jnp.concatenate([_wgrad(s["yp"], dyb, pw, d, f"dwout_pool_{l}"),
                               _wgrad(s["yg"], dyb, vw, d, f"dwout_gla_{l}")], axis=0)
        dq, dk, dv, dr, dzg, dwgu, dbg, dgn = _gla_bwd(dyg, s["o"], s["r"], gla_norm_g[l:l + 1], s["q"], s["k"],
                                                       s["v"], s["la"], s["zg"], s["sall"], mw[l]["wgu"],
                                                       f"gla_bwd_{l}")
        dup, dwp, dsc = _pool_bwd(dyp, s["pb"], wp16[l], pool_scale[l:l + 1], f"pool_bwd_{l}")
        small_grads["b_gate"][l], small_grads["gla_norm_g"][l] = dbg, dgn
        small_grads["w_pool"][l], small_grads["pool_scale"][l] = dwp, dsc
        dh, dz = _inproj_bwd(dres, [dup, dq, dk, dv, dr], dzg, mw[l]["w_main"], mw[l]["w_lr"], f"inproj_bwd_{l}")
        dwi = jnp.concatenate([_wgrad(dz, s["hb1"], 4 * LANE, d, f"dwin_main_{l}"),
                               _wgrad(dzg, s["hb1"], LANE, d, f"dwin_lr_{l}")[:rank]], axis=0)
        gone = depart(f"scatter_start_mix_{l}", [(n, l) for n in mix_names],
                      [dwo.reshape(N_SHARD, -1, d), _col_shards(dwgu[:rank]), dwi.reshape(N_SHARD, -1, d)])

        dh, dfb, dgb, dub, act, dgam, dbet = _ffn_bwd(dh, s["xh1"], s["rs1"], ln1_g[l:l + 1] + gone[0:1, 0:1],
                                                      s["g1"], s["u1"], wa["ffn1_w_gate", l], wa["ffn1_w_up", l],
                                                      wa["ffn1_w_down", l], alpha, f"ffn1_bwd_{l}")
        small_grads["ln1_g"][l], small_grads["ln1_b"][l] = dgam, dbet
        if l:
            gone = depart(f"scatter_start_ffn1_{l}", [(n, l) for n in ffn1_names],
                          [ffn_wgrad(n, l, s["hb0"], dgb, dub, act, dfb) for n in ffn1_names])
            ln3_g = ln3_g.at[l - 1:l].add(gone[0:1, 0:1])
            continue
        grad_x = dh[N_META:t_real][None]
        small_vec = pack({n: jnp.stack(small_grads[n]) for n in SMALL})
        gone = depart("scatter_start_rest", [("meta_tokens", 0), ("small", 0)],
                      [_col_shards(dh[:N_META]), small_vec], ["scatter", "bcast"])
        for n in ffn1_names:
            g = ffn_wgrad(n, l, s["hb0"], dgb, dub, act, dfb, after=gone)
            gone = depart(f"scatter_start_{n}", [(n, l)], [g])
            late.append(scatters.pop())

    sent, recv, results, firsts = {}, {}, {}, []
    my_slot = me_xy.reshape(1).astype(jnp.int32)

    def collect(group, after):
        for name, items, st in group:
            arrs, lands, _ = _exchange_wait(name.replace("start", "wait"), st, after)
            for item, a, b in zip(items, arrs, lands):
                sent[item], recv[item] = a, b

    def reduce_and_update(names, tag):
        partial = []
        for n in names:
            layers = [(n, l) for l in range(1 if n == "meta_tokens" else nl)]
            partial.append(_sum_slots(my_slot, [sent[it] for it in layers], [recv[it] for it in layers],
                                      f"sum_{n}", n == "w_in"))
        for n, mine, theirs in zip(names, partial, _swap_sibling(partial, f"swap_sibling_{tag}")):
            fit = lambda a: stored(n, a).reshape(mine.shape)
            outs = _adamw(fit(w[n]), [mine, theirs], fit(mom1[n]), fit(mom2[n]), f"adamw_{n}")
            results[n] = [as_given(n, o) for o in outs]
            firsts.append(outs[1][0, 0, 0])

    collect(scatters, gone)
    early = [n for n in ("meta_tokens",) + BIG if n not in ffn1_names]
    reduce_and_update(early, "early")
    small_terms = [recv["small", 0][i][None] for i in range(N_DEV)]
    souts = _adamw(pack(w)[None], small_terms, pack(mom1)[None], pack(mom2)[None], "adamw_small")
    off = 0
    for n in SMALL:
        size = w[n].size
        results[n] = [o.reshape(-1)[off:off + size].reshape(w[n].shape) for o in souts]
        off += size
    collect(late, souts[0][0, :8] + functools.reduce(jnp.add, firsts))
    reduce_and_update(ffn1_names, "late")

    out = [loss, grad_x]
    for part in range(4):
        out += [results[n][part] for n in WEIGHTS]
    return tuple(out)
```

```python
import functools

import jax
import jax.numpy as jnp
from jax import lax
from jax.experimental import pallas as pl
from jax.experimental.pallas import tpu as pltpu

F32 = jnp.float32
BF16 = jnp.bfloat16
MESH = pl.DeviceIdType.MESH

N_META = 16
POOL_WINDOWS = (2, 4, 8, 16)
POOL_HALO = 16
N_HEADS = 4
GLA_GATE_TEMP = 16.0
CHUNK = 128
CHUNK_UNROLL = 5
LN_EPS = 1e-5
RMS_EPS = 1e-6
ADAM_LR = 0.001
ADAM_B1 = 0.9
ADAM_B2 = 0.999
ADAM_EPS = 1e-08
ADAM_WD = 0.01
ADAM_STEP = 10
LANE = 128
BF16_ROWS = 16
ROW_TILE = 640
FFN_ROW_TILE = 640
FFN_CHUNKS = 4
FFN_SPLIT = 2
ROW_GROUPS = 2
WGRAD_K_MAX = 4224
N_SHARD = 4
N_DEV = 8

BIG = ("ffn1_w_gate", "ffn1_w_up", "ffn1_w_down", "w_in", "w_gate_up", "w_out",
       "ffn2_w_gate", "ffn2_w_up", "ffn2_w_down")
SMALL = ("ln1_g", "ln1_b", "b_gate", "w_pool", "pool_scale", "gla_norm_g", "ln2_g", "ln2_b", "ln3_g", "ln3_b")
WEIGHTS = ("meta_tokens", "ffn1_w_gate", "ffn1_w_up", "ffn1_w_down", "ln1_g", "ln1_b", "w_in", "w_gate_up",
           "b_gate", "w_pool", "pool_scale", "gla_norm_g", "w_out", "ln2_g", "ln2_b", "ffn2_w_gate",
           "ffn2_w_up", "ffn2_w_down", "ln3_g", "ln3_b")


def _tc_call(body, **kw):
    return pl.pallas_call(body, **kw)


def _comm_call(body, **kw):
    return pl.pallas_call(body, **kw)


def _seq(n):
    return pltpu.CompilerParams(dimension_semantics=("arbitrary",) * n)


def _mm(a, b):
    return jnp.dot(a.astype(BF16), b.astype(BF16), preferred_element_type=F32)


def _mm_nt(a, b):
    return lax.dot_general(a.astype(BF16), b.astype(BF16), (((1,), (1,)), ((), ())), preferred_element_type=F32)


def _mm_tn(a, b):
    return lax.dot_general(a.astype(BF16), b.astype(BF16), (((0,), (0,)), ((), ())), preferred_element_type=F32)


def _mm_01(a, b):
    hi = b.astype(BF16)
    lo = (b - hi.astype(F32)).astype(BF16)
    a = a.astype(BF16)
    return jnp.dot(a, hi, preferred_element_type=F32) + jnp.dot(a, lo, preferred_element_type=F32)


def _row_tile(t, most=None):
    tm = min(most or ROW_TILE, t)
    while t % tm:
        tm -= LANE
    return tm


def _silu_parts(g):
    sg = jax.nn.sigmoid(g)
    return sg, g * sg


def _ln_stats(r):
    mu = jnp.mean(r, axis=-1, keepdims=True)
    rc = r - mu
    var = jnp.mean(rc * rc, axis=-1, keepdims=True)
    rs = lax.rsqrt(var + LN_EPS)
    return rc * rs, rs


def _ln_bwd(dy, xh, rs, gam):
    dyg = dy * gam
    c1 = jnp.mean(dyg, axis=-1, keepdims=True)
    c2 = jnp.mean(dyg * xh, axis=-1, keepdims=True)
    return rs * (dyg - c1 - xh * c2)


def _ffn_fwd(xin, gam_in, bet_in, wg, wu, wd, alpha, name, loss_head=None):
    t, d = xin.shape
    nj, tf, _ = wg.shape
    tm = _row_tile(t, FFN_ROW_TILE)
    nt = t // tm
    share = tm // nj
    head = [] if loss_head is None else list(loss_head[:3])

    def body(x_ref, gi_ref, bi_ref, wg_ref, wu_ref, wd_ref, *rest):
        head_refs, rest = rest[:len(head)], rest[len(head):]
        xhat_ref, rstd_ref, hb_ref, go_ref, uo_ref = rest[:5]
        acc, hbs = rest[-2:]
        i = pl.program_id(0)
        j = pl.program_id(1)
        cur = i % 2

        if head:
            @pl.when((i == 0) & (j == 0))
            def _():
                rest[6][...] = jnp.zeros_like(rest[6])

        def norm_previous():
            rows = pl.ds(pl.multiple_of(j * share, BF16_ROWS), share)
            xhat, rs = _ln_stats(0.5 * acc[1 - cur, rows, :])
            xhat_ref[rows, :] = xhat
            rstd_ref[rows, :] = rs
            if head:
                g_ref, b_ref, t_ref = head_refs
                dy_ref, loss_ref = rest[5], rest[6]
                rowi = (i - 1) * tm + j * share + lax.broadcasted_iota(jnp.int32, (share, 1), 0)
                live = (rowi >= N_META) & (rowi < N_META + loss_head[3])
                diff = jnp.where(live, xhat * g_ref[...] + b_ref[...] - t_ref[rows, :], 0.0)
                dy_ref[rows, :] = diff * (1.0 / d)
                loss_ref[...] += jnp.sum(diff * diff) * (0.5 / d)

        @pl.when(i < nt)
        def _():
            @pl.when(j == 0)
            def _():
                h = x_ref[...] * gi_ref[...] + bi_ref[...]
                hb = h.astype(BF16)
                hbs[...] = hb
                hb_ref[...] = hb
                acc[cur] = (2.0 * alpha) * h

                @pl.when(i == 0)
                def _():
                    acc[1] = jnp.zeros((tm, d), F32)

            norm_previous()
            hb = hbs[...]
            g = _mm_nt(hb, wg_ref[...])
            u = _mm_nt(hb, wu_ref[...])
            _, sl = _silu_parts(g)
            go_ref[...] = g.astype(BF16)
            uo_ref[...] = u.astype(BF16)
            acc[cur] += jnp.dot((sl * u).astype(BF16), wd_ref[...], preferred_element_type=F32)

        @pl.when(i == nt)
        def _():
            norm_previous()

    here = lambda i, j: (jnp.minimum(i, nt - 1), 0)
    before = lambda i, j: (jnp.maximum(i - 1, 0), 0)
    chunk = lambda i, j: (jnp.where(i < nt, j, nj - 1), 0, 0)
    col = pl.BlockSpec((None, tm, tf), lambda i, j: (jnp.where(i < nt, j, nj - 1), jnp.minimum(i, nt - 1), 0))
    vec = pl.BlockSpec((1, d), lambda i, j: (0, 0))
    head_in = [vec, vec, pl.BlockSpec((tm, d), before)] if head else []
    head_out = [pl.BlockSpec((tm, d), before), pl.BlockSpec((8, LANE), lambda i, j: (0, 0))] if head else []
    head_shape = [jax.ShapeDtypeStruct((t, d), F32), jax.ShapeDtypeStruct((8, LANE), F32)] if head else []
    return _tc_call(
        body, name=name, grid=(nt + 1, nj),
        in_specs=[pl.BlockSpec((tm, d), here), vec, vec] + [pl.BlockSpec((None, tf, d), chunk)] * 3 + head_in,
        out_specs=[pl.BlockSpec((tm, d), before), pl.BlockSpec((tm, 1), before), pl.BlockSpec((tm, d), here), col, col]
                  + head_out,
        out_shape=[jax.ShapeDtypeStruct((t, d), F32), jax.ShapeDtypeStruct((t, 1), F32),
                   jax.ShapeDtypeStruct((t, d), BF16), jax.ShapeDtypeStruct((nj, t, tf), BF16),
                   jax.ShapeDtypeStruct((nj, t, tf), BF16)] + head_shape,
        scratch_shapes=[pltpu.VMEM((2, tm, d), F32), pltpu.VMEM((tm, d), BF16)],
        compiler_params=_seq(2),
    )(xin, gam_in, bet_in, wg, wu, wd, *head)


def _ffn_bwd(dh, xhat, rstd, ln_g, gb, ub, wg, wu, wd, alpha, name):
    t, d = dh.shape
    nj, tf, _ = wg.shape
    tm = _row_tile(t, FFN_ROW_TILE)
    nt = t // tm
    share = tm // nj

    def body(dh_ref, xh_ref, rs_ref, g_ref, gb_ref, ub_ref, wg_ref, wu_ref, wd_ref,
             dhin_ref, df_ref, dg_ref, du_ref, act_ref, dgam_ref, dbet_ref, df_s, dres_next, df_next):
        i = pl.program_id(0)
        j = pl.program_id(1)

        @pl.when((i == 0) & (j == 0))
        def _():
            dgam_ref[...] = jnp.zeros_like(dgam_ref)
            dbet_ref[...] = jnp.zeros_like(dbet_ref)

        def look_ahead():
            rows = pl.ds(pl.multiple_of(j * share, BF16_ROWS), share)
            dy = dh_ref[rows, :]
            xh = xh_ref[rows, :]
            dr = _ln_bwd(dy, xh, rs_ref[rows, :], g_ref[...])
            dres_next[rows, :] = alpha * dr
            df_next[rows, :] = (0.5 * dr).astype(BF16)
            live = jnp.where(i < nt, 1.0, 0.0)
            dgam_ref[...] += live * jnp.sum(dy * xh, axis=0, keepdims=True)
            dbet_ref[...] += live * jnp.sum(dy, axis=0, keepdims=True)

        @pl.when(i == 0)
        def _():
            look_ahead()

        @pl.when(i > 0)
        def _():
            @pl.when(j == 0)
            def _():
                dhin_ref[...] = dres_next[...]
                dfb = df_next[...]
                df_s[...] = dfb
                df_ref[...] = dfb

            look_ahead()
            for part in range(FFN_SPLIT):
                rows = pl.ds(part * (tm // FFN_SPLIT), tm // FFN_SPLIT)
                dact = _mm_nt(df_s[rows, :], wd_ref[...])
                g = gb_ref[rows, :].astype(F32)
                u = ub_ref[rows, :].astype(F32)
                sg, sl = _silu_parts(g)
                dg = (dact * u * (sg + sl * (1.0 - sg))).astype(BF16)
                du = (dact * sl).astype(BF16)
                dg_ref[rows, :] = dg
                du_ref[rows, :] = du
                act_ref[rows, :] = (sl * u).astype(BF16)
                dhin_ref[rows, :] += _mm(dg, wg_ref[...]) + _mm(du, wu_ref[...])

    ahead = lambda i, j: (jnp.minimum(i, nt - 1), 0)
    row = lambda i, j: (jnp.maximum(i - 1, 0), 0)
    chunk = lambda i, j: (jnp.where(i > 0, j, 0), 0, 0)
    col = pl.BlockSpec((None, tm, tf), lambda i, j: (jnp.where(i > 0, j, 0), jnp.maximum(i - 1, 0), 0))
    vec = pl.BlockSpec((1, d), lambda i, j: (0, 0))
    ff = jax.ShapeDtypeStruct((nj, t, tf), BF16)
    return _tc_call(
        body, name=name, grid=(nt + 1, nj),
        in_specs=[pl.BlockSpec((tm, d), ahead), pl.BlockSpec((tm, d), ahead), pl.BlockSpec((tm, 1), ahead), vec,
                  col, col] + [pl.BlockSpec((None, tf, d), chunk)] * 3,
        out_specs=[pl.BlockSpec((tm, d), row), pl.BlockSpec((tm, d), row), col, col, col, vec, vec],
        out_shape=[jax.ShapeDtypeStruct((t, d), F32), jax.ShapeDtypeStruct((t, d), BF16), ff, ff, ff,
                   jax.ShapeDtypeStruct((1, d), F32), jax.ShapeDtypeStruct((1, d), F32)],
        scratch_shapes=[pltpu.VMEM((tm, d), BF16), pltpu.VMEM((tm, d), F32), pltpu.VMEM((tm, d), BF16)],
        compiler_params=_seq(2),
    )(dh, xhat, rstd, ln_g, gb, ub, wg, wu, wd)


def _wgrad(a, b, tmm, tn, name, after=None):
    t = a.shape[-2]
    m = a.shape[-1] * (a.shape[0] if a.ndim == 3 else 1)
    n = b.shape[-1] * (b.shape[0] if b.ndim == 3 else 1)
    tk = max(k for k in range(BF16_ROWS, WGRAD_K_MAX + 1, BF16_ROWS) if t % k == 0)
    nk = t // tk
    extra = [] if after is None else [after]

    def body(a_ref, b_ref, *rest):
        o_ref, acc = rest[len(extra):]
        k = pl.program_id(2)

        @pl.when(k == 0)
        def _():
            acc[...] = jnp.zeros_like(acc)

        acc[...] += _mm_tn(a_ref[...], b_ref[...])

        @pl.when(k == nk - 1)
        def _():
            o_ref[...] = acc[...].astype(o_ref.dtype)

    a_spec = (pl.BlockSpec((None, tk, tmm), lambda i, j, k: (i, k, 0)) if a.ndim == 3
              else pl.BlockSpec((tk, tmm), lambda i, j, k: (k, i)))
    return _tc_call(
        body, name=name, grid=(m // tmm, n // tn, nk),
        in_specs=[a_spec, pl.BlockSpec((None, tk, tn), lambda i, j, k: (j, k, 0)) if b.ndim == 3
                  else pl.BlockSpec((tk, tn), lambda i, j, k: (k, j))] + [pl.BlockSpec(memory_space=pl.ANY)] * len(extra),
        out_specs=pl.BlockSpec((tmm, tn), lambda i, j, k: (i, j)),
        out_shape=jax.ShapeDtypeStruct((m, n), BF16),
        scratch_shapes=[pltpu.VMEM((tmm, tn), F32)],
        compiler_params=_seq(3),
    )(a, b, *extra)


def _inproj_fwd(xhat, gam, bet, w_main, w_lr, wgu, b_gate, widths, name):
    t, d = xhat.shape
    tm = _row_tile(t)
    kw = wgu.shape[1]
    offs = [0]
    for w in widths:
        offs.append(offs[-1] + w)

    def body(x_ref, g_ref, b_ref, wm_ref, wl_ref, wgu_ref, bg_ref, *outs):
        piece_refs, (zg_ref, la_ref, hb_ref) = outs[:len(widths)], outs[len(widths):]
        hb = (x_ref[...] * g_ref[...] + b_ref[...]).astype(BF16)
        hb_ref[...] = hb
        for p, ref in enumerate(piece_refs):
            ref[...] = _mm_nt(hb, wm_ref[offs[p]:offs[p + 1], :]).astype(ref.dtype)
        zg = _mm_nt(hb, wl_ref[...])
        zg_ref[...] = zg
        logit = _mm(zg, wgu_ref[...]) + bg_ref[...]
        la_ref[...] = (jnp.minimum(logit, 0.0) - jnp.log(1.0 + jnp.exp(-jnp.abs(logit)))) * (1.0 / GLA_GATE_TEMP)

    row = lambda i: (i, 0)
    full = lambda a: pl.BlockSpec(a.shape, lambda i: (0,) * a.ndim)
    out_w = list(widths) + [LANE, kw]
    out_t = [F32, F32, F32, BF16, F32, F32, F32]
    return _tc_call(
        body, name=name, grid=(t // tm,),
        in_specs=[pl.BlockSpec((tm, d), row), full(gam), full(bet), full(w_main), full(w_lr), full(wgu), full(b_gate)],
        out_specs=[pl.BlockSpec((tm, w), row) for w in out_w] + [pl.BlockSpec((tm, d), row)],
        out_shape=[jax.ShapeDtypeStruct((t, w), ty) for w, ty in zip(out_w, out_t)]
                  + [jax.ShapeDtypeStruct((t, d), BF16)],
        compiler_params=_seq(1),
    )(xhat, gam, bet, w_main, w_lr, wgu, b_gate)


def _inproj_bwd(dh_part, pieces, dzg, w_main, w_lr, name):
    t, d = dh_part.shape
    tm = _row_tile(t)
    widths = [p.shape[1] for p in pieces]
    offs = [0]
    for w in widths:
        offs.append(offs[-1] + w)

    def body(*refs):
        dhp_ref = refs[0]
        p_refs = refs[1:1 + len(widths)]
        dzg_ref, wm_ref, wl_ref, dh_ref, dz_ref = refs[1 + len(widths):]
        acc = dhp_ref[...] + _mm(dzg_ref[...], wl_ref[...])
        for p, ref in enumerate(p_refs):
            v = ref[...]
            dz_ref[:, offs[p]:offs[p + 1]] = v
            acc += _mm(v, wm_ref[offs[p]:offs[p + 1], :])
        dh_ref[...] = acc

    row = lambda i: (i, 0)
    full = lambda a: pl.BlockSpec(a.shape, lambda i: (0,) * a.ndim)
    return _tc_call(
        body, name=name, grid=(t // tm,),
        in_specs=[pl.BlockSpec((tm, d), row)] + [pl.BlockSpec((tm, w), row) for w in widths]
                 + [pl.BlockSpec((tm, LANE), row), full(w_main), full(w_lr)],
        out_specs=[pl.BlockSpec((tm, d), row), pl.BlockSpec((tm, offs[-1]), row)],
        out_shape=[jax.ShapeDtypeStruct((t, d), F32), jax.ShapeDtypeStruct((t, offs[-1]), BF16)],
        compiler_params=_seq(1),
    )(dh_part, *pieces, dzg, w_main, w_lr)


def _pool_cnt(tile, tm, w):
    t = tile * tm + lax.broadcasted_iota(jnp.int32, (tm, 1), 0)
    return jnp.minimum(t + 1, w).astype(F32)


def _pool_fwd(u, wp, scale, name):
    t, pw = u.shape
    tm = _row_tile(t)
    gd = wp.shape[1]

    def body(u_ref, wp_ref, sc_ref, y_ref, p_ref, ext):
        i = pl.program_id(0)

        @pl.when(i == 0)
        def _():
            ext[0:POOL_HALO, :] = jnp.zeros((POOL_HALO, pw), F32)

        ext[POOL_HALO:POOL_HALO + tm, :] = u_ref[...]
        for gi, w in enumerate(POOL_WINDOWS):
            cols = slice(gi * gd, (gi + 1) * gd)
            s = ext[pl.ds(POOL_HALO, tm), cols]
            tot = s
            for back in range(1, w):
                tot = tot + ext[pl.ds(POOL_HALO - back, tm), cols]
            p = (tot / _pool_cnt(i, tm, w) - s).astype(BF16)
            p_ref[:, cols] = p
            y_ref[:, cols] = (jnp.dot(p, wp_ref[gi], preferred_element_type=F32) * sc_ref[:, cols]).astype(BF16)
        ext[0:POOL_HALO, :] = ext[tm:tm + POOL_HALO, :]

    row = lambda i: (i, 0)
    return _tc_call(
        body, name=name, grid=(t // tm,),
        in_specs=[pl.BlockSpec((tm, pw), row), pl.BlockSpec(wp.shape, lambda i: (0, 0, 0)),
                  pl.BlockSpec((1, pw), lambda i: (0, 0))],
        out_specs=[pl.BlockSpec((tm, pw), row), pl.BlockSpec((tm, pw), row)],
        out_shape=[jax.ShapeDtypeStruct((t, pw), BF16), jax.ShapeDtypeStruct((t, pw), BF16)],
        scratch_shapes=[pltpu.VMEM((tm + POOL_HALO, pw), F32)],
        compiler_params=_seq(1),
    )(u, wp, scale)


def _pool_bwd(dy, pb, wp, scale, name):
    t, pw = dy.shape
    tm = _row_tile(t)
    nt = t // tm
    gd = wp.shape[1]

    def body(dy_ref, p_ref, wp_ref, sc_ref, du_ref, dwp_ref, dsc_ref, ext):
        i = pl.program_id(0)
        tile = nt - 1 - i

        @pl.when(i == 0)
        def _():
            ext[tm:tm + POOL_HALO, :] = jnp.zeros((POOL_HALO, pw), F32)
            dwp_ref[...] = jnp.zeros_like(dwp_ref)
            dsc_ref[...] = jnp.zeros_like(dsc_ref)

        dps = []
        for gi, w in enumerate(POOL_WINDOWS):
            cols = slice(gi * gd, (gi + 1) * gd)
            dyv = dy_ref[:, cols]
            p = p_ref[:, cols]
            dpre = (dyv * sc_ref[:, cols]).astype(BF16)
            dsc_ref[:, cols] += jnp.sum(dyv * jnp.dot(p, wp_ref[gi], preferred_element_type=F32), axis=0, keepdims=True)
            dwp_ref[gi] += _mm_tn(p, dpre)
            dp = _mm_nt(dpre, wp_ref[gi])
            dps.append(dp)
            ext[0:tm, cols] = dp / _pool_cnt(tile, tm, w)
        for gi, w in enumerate(POOL_WINDOWS):
            cols = slice(gi * gd, (gi + 1) * gd)
            tot = ext[pl.ds(0, tm), cols]
            for fwd in range(1, w):
                tot = tot + ext[pl.ds(fwd, tm), cols]
            du_ref[:, cols] = (tot - dps[gi]).astype(BF16)
        ext[tm:tm + POOL_HALO, :] = ext[0:POOL_HALO, :]

    row = lambda i: (nt - 1 - i, 0)
    return _tc_call(
        body, name=name, grid=(nt,),
        in_specs=[pl.BlockSpec((tm, pw), row), pl.BlockSpec((tm, pw), row),
                  pl.BlockSpec(wp.shape, lambda i: (0, 0, 0)), pl.BlockSpec((1, pw), lambda i: (0, 0))],
        out_specs=[pl.BlockSpec((tm, pw), row), pl.BlockSpec(wp.shape, lambda i: (0, 0, 0)),
                   pl.BlockSpec((1, pw), lambda i: (0, 0))],
        out_shape=[jax.ShapeDtypeStruct((t, pw), BF16), jax.ShapeDtypeStruct(wp.shape, F32),
                   jax.ShapeDtypeStruct((1, pw), F32)],
        scratch_shapes=[pltpu.VMEM((tm + POOL_HALO, pw), F32)],
        compiler_params=_seq(1),
    )(dy, pb, wp, scale)


def _gla_masks(kw, vw):
    dk, dv = kw // N_HEADS, vw // N_HEADS
    lane_k = lax.broadcasted_iota(jnp.int32, (1, kw), 1)
    lane_v = lax.broadcasted_iota(jnp.int32, (1, vw), 1)
    hk = [((lane_k >= h * dk) & (lane_k < (h + 1) * dk)).astype(F32) for h in range(N_HEADS)]
    hv = [((lane_v >= h * dv) & (lane_v < (h + 1) * dv)).astype(F32) for h in range(N_HEADS)]
    r = lax.broadcasted_iota(jnp.int32, (CHUNK, CHUNK), 0)
    c = lax.broadcasted_iota(jnp.int32, (CHUNK, CHUNK), 1)
    tril = r >= c
    rs = lax.broadcasted_iota(jnp.int32, (N_HEADS * CHUNK, CHUNK), 0) & (CHUNK - 1)
    stril = rs >= lax.broadcasted_iota(jnp.int32, (N_HEADS * CHUNK, CHUNK), 1)
    return hk, hv, tril, stril


def _block_diag(x, hk, dv):
    return jnp.concatenate([x[h * dv:(h + 1) * dv, :] * hk[h] for h in range(N_HEADS)], axis=0)


def _gla_fwd(q, k, v, loga, r, gnorm, name):
    t, kw = q.shape
    vw = v.shape[1]
    dk, dv = kw // N_HEADS, vw // N_HEADS
    tm = _row_tile(t)
    nc = tm // CHUNK
    qscale = dk ** -0.5

    def body(q_ref, k_ref, v_ref, la_ref, r_ref, gn_ref, o_ref, y_ref, sall_ref, st):
        @pl.when(pl.program_id(0) == 0)
        def _():
            st[...] = jnp.zeros_like(st)

        hk, hv, tril, stril = _gla_masks(kw, vw)
        trif = tril.astype(F32)

        def chunk(c, carry):
            rows = pl.ds(pl.multiple_of(c * CHUNK, CHUNK), CHUNK)
            la = la_ref[rows, :]
            b = _mm_01(trif, la)
            bl = jnp.sum(la, axis=0, keepdims=True)
            qb = q_ref[rows, :] * (qscale * jnp.exp(b))
            kk = k_ref[rows, :]
            kb = kk * jnp.exp(-b)
            kl = kk * jnp.exp(bl - b)
            vv = v_ref[rows, :]
            s_t = st[...]
            compact = s_t[0:dv, :]
            for h in range(1, N_HEADS):
                compact = compact + s_t[h * dv:(h + 1) * dv, :]
            sall_ref[c] = compact
            qx = jnp.concatenate([qb.astype(BF16) * hk[h].astype(BF16) for h in range(N_HEADS)], axis=0)
            a = jnp.where(stril, _mm_nt(qx, kb), 0.0).astype(BF16)
            o_inter = _mm_nt(qb, s_t)
            for h in range(N_HEADS):
                vs = slice(h * dv, (h + 1) * dv)
                o_ref[rows, vs] = o_inter[:, vs] + _mm(a[h * CHUNK:(h + 1) * CHUNK, :], vv[:, vs])
            st[...] = s_t * jnp.exp(bl) + _block_diag(_mm_tn(vv, kl), hk, dv)
            return carry

        lax.fori_loop(0, nc, chunk, 0, unroll=CHUNK_UNROLL)
        for h in range(N_HEADS):
            vs = slice(h * dv, (h + 1) * dv)
            oh = o_ref[:, vs]
            on = oh * lax.rsqrt(jnp.mean(oh * oh, axis=-1, keepdims=True) + RMS_EPS)
            _, sl = _silu_parts(r_ref[:, vs])
            y_ref[:, vs] = (on * gn_ref[:, vs] * sl).astype(BF16)

    row = lambda i: (i, 0)
    return _tc_call(
        body, name=name, grid=(t // tm,),
        in_specs=[pl.BlockSpec((tm, kw), row), pl.BlockSpec((tm, kw), row), pl.BlockSpec((tm, vw), row),
                  pl.BlockSpec((tm, kw), row), pl.BlockSpec((tm, vw), row), pl.BlockSpec((1, vw), lambda i: (0, 0))],
        out_specs=[pl.BlockSpec((tm, vw), row), pl.BlockSpec((tm, vw), row),
                   pl.BlockSpec((nc, dv, kw), lambda i: (i, 0, 0))],
        out_shape=[jax.ShapeDtypeStruct((t, vw), F32), jax.ShapeDtypeStruct((t, vw), BF16),
                   jax.ShapeDtypeStruct((t // CHUNK, dv, kw), F32)],
        scratch_shapes=[pltpu.VMEM((vw, kw), F32)],
        compiler_params=_seq(1),
    )(q, k, v, loga, r, gnorm)


def _gla_bwd(dy, o, r, gnorm, q, k, v, loga, zg, sall, wgu, name):
    t, kw = q.shape
    vw = v.shape[1]
    dk, dv = kw // N_HEADS, vw // N_HEADS
    tm = _row_tile(t)
    nt = t // tm
    nc = tm // CHUNK
    qscale = dk ** -0.5

    def body(dy_ref, o_ref, r_ref, gn_ref, q_ref, k_ref, v_ref, la_ref, zg_ref, sall_ref, wgu_ref,
             dq_ref, dk_ref, dv_ref, dr_ref, dzg_ref, dwgu_ref, dbg_ref, dgn_ref, dst, do_s):
        @pl.when(pl.program_id(0) == 0)
        def _():
            dst[...] = jnp.zeros_like(dst)
            dwgu_ref[...] = jnp.zeros_like(dwgu_ref)
            dbg_ref[...] = jnp.zeros_like(dbg_ref)
            dgn_ref[...] = jnp.zeros_like(dgn_ref)

        for h in range(N_HEADS):
            vs = slice(h * dv, (h + 1) * dv)
            oh = o_ref[:, vs]
            rinv = lax.rsqrt(jnp.mean(oh * oh, axis=-1, keepdims=True) + RMS_EPS)
            on = oh * rinv
            rr = r_ref[:, vs]
            sg, sl = _silu_parts(rr)
            dyv = dy_ref[:, vs]
            gn = gn_ref[:, vs]
            dgn_ref[:, vs] += jnp.sum(dyv * on * sl, axis=0, keepdims=True)
            dr_ref[:, vs] = (dyv * on * gn * (sg + sl * (1.0 - sg))).astype(BF16)
            don = dyv * gn * sl
            do_s[:, vs] = rinv * (don - on * jnp.mean(don * on, axis=-1, keepdims=True))

        hk, hv, tril, stril = _gla_masks(kw, vw)
        trif = tril.astype(F32)
        triuf = (lax.broadcasted_iota(jnp.int32, (CHUNK, CHUNK), 0)
                 <= lax.broadcasted_iota(jnp.int32, (CHUNK, CHUNK), 1)).astype(F32)
        last_row = lax.broadcasted_iota(jnp.int32, (CHUNK, 1), 0) == CHUNK - 1

        def chunk(idx, carry):
            c = nc - 1 - idx
            rows = pl.ds(pl.multiple_of(c * CHUNK, CHUNK), CHUNK)
            la = la_ref[rows, :]
            b = _mm_01(trif, la)
            bl = jnp.sum(la, axis=0, keepdims=True)
            eb = jnp.exp(b)
            enb = jnp.exp(-b)
            ebl = jnp.exp(bl - b)
            el = jnp.exp(bl)
            qb = q_ref[rows, :] * (qscale * eb)
            kk = k_ref[rows, :]
            kb = kk * enb
            kl = kk * ebl
            vv = v_ref[rows, :]
            do = do_s[rows, :]
            compact = sall_ref[c]
            s_t = jnp.concatenate([compact * hk[h] for h in range(N_HEADS)], axis=0)
            ds_t = dst[...]
            qx = jnp.concatenate([qb.astype(BF16) * hk[h].astype(BF16) for h in range(N_HEADS)], axis=0)
            dox = jnp.concatenate([do.astype(BF16) * hv[h].astype(BF16) for h in range(N_HEADS)], axis=0)
            a = jnp.where(stril, _mm_nt(qx, kb), 0.0).astype(BF16)
            da = jnp.where(stril, _mm_nt(dox, vv), 0.0).astype(BF16)
            dv_ref[rows, :] = (_mm_tn(a, dox) + _mm_nt(kl, ds_t)).astype(BF16)
            dak = _mm(da, kb)
            dqb = _mm(do, s_t)
            for h in range(N_HEADS):
                dqb = dqb + dak[h * CHUNK:(h + 1) * CHUNK, :] * hk[h]
            dkb = _mm_tn(da, qx)
            dkl = _mm(vv, ds_t)
            dbl = jnp.sum(dkl * kl, axis=0, keepdims=True) + el * jnp.sum(ds_t * s_t, axis=0, keepdims=True)
            dst[...] = ds_t * el + _block_diag(_mm_tn(do, qb), hk, dv)
            dq_ref[rows, :] = (dqb * (qscale * eb)).astype(BF16)
            dk_ref[rows, :] = (dkb * enb + dkl * ebl).astype(BF16)
            db = dqb * qb - dkb * kb - dkl * kl + jnp.where(last_row, dbl, 0.0)
            dla = _mm_01(triuf, db)
            dlogit = dla * (1.0 / GLA_GATE_TEMP) * (1.0 - jnp.exp(GLA_GATE_TEMP * la))
            dzg_ref[rows, :] = _mm_nt(dlogit, wgu_ref[...]).astype(BF16)
            dwgu_ref[...] += _mm_tn(zg_ref[rows, :], dlogit)
            dbg_ref[...] += jnp.sum(dlogit, axis=0, keepdims=True)
            return carry

        lax.fori_loop(0, nc, chunk, 0, unroll=CHUNK_UNROLL)

    row = lambda i: (nt - 1 - i, 0)
    const = lambda i: (0, 0)
    return _tc_call(
        body, name=name, grid=(nt,),
        in_specs=[pl.BlockSpec((tm, vw), row), pl.BlockSpec((tm, vw), row), pl.BlockSpec((tm, vw), row),
                  pl.BlockSpec((1, vw), const), pl.BlockSpec((tm, kw), row), pl.BlockSpec((tm, kw), row),
                  pl.BlockSpec((tm, vw), row), pl.BlockSpec((tm, kw), row), pl.BlockSpec((tm, LANE), row),
                  pl.BlockSpec((nc, dv, kw), lambda i: (nt - 1 - i, 0, 0)), pl.BlockSpec((LANE, kw), const)],
        out_specs=[pl.BlockSpec((tm, kw), row), pl.BlockSpec((tm, kw), row), pl.BlockSpec((tm, vw), row),
                   pl.BlockSpec((tm, vw), row), pl.BlockSpec((tm, LANE), row), pl.BlockSpec((LANE, kw), const),
                   pl.BlockSpec((1, kw), const), pl.BlockSpec((1, vw), const)],
        out_shape=[jax.ShapeDtypeStruct((t, kw), BF16), jax.ShapeDtypeStruct((t, kw), BF16),
                   jax.ShapeDtypeStruct((t, vw), BF16), jax.ShapeDtypeStruct((t, vw), BF16),
                   jax.ShapeDtypeStruct((t, LANE), BF16), jax.ShapeDtypeStruct((LANE, kw), F32),
                   jax.ShapeDtypeStruct((1, kw), F32), jax.ShapeDtypeStruct((1, vw), F32)],
        scratch_shapes=[pltpu.VMEM((vw, kw), F32), pltpu.VMEM((tm, vw), F32)],
        compiler_params=_seq(1),
    )(dy, o, r, gnorm, q, k, v, loga, zg, sall, wgu)


def _outproj_fwd(yp, yg, w_out, xhat, gam, bet, alpha, name):
    t, d = xhat.shape
    pw = yp.shape[1]
    tm = _row_tile(t)

    def body(yp_ref, yg_ref, w_ref, x_ref, g_ref, b_ref, xhat_ref, rstd_ref):
        for part in range(ROW_GROUPS):
            rows = pl.ds(part * (tm // ROW_GROUPS), tm // ROW_GROUPS)
            h = x_ref[rows, :] * g_ref[...] + b_ref[...]
            y = (jnp.dot(yp_ref[rows, :], w_ref[0:pw, :], preferred_element_type=F32)
                 + jnp.dot(yg_ref[rows, :], w_ref[pw:, :], preferred_element_type=F32))
            xh, rs = _ln_stats(alpha * h + y)
            xhat_ref[rows, :] = xh
            rstd_ref[rows, :] = rs

    row = lambda i: (i, 0)
    vec = pl.BlockSpec((1, d), lambda i: (0, 0))
    return _tc_call(
        body, name=name, grid=(t // tm,),
        in_specs=[pl.BlockSpec((tm, pw), row), pl.BlockSpec((tm, yg.shape[1]), row),
                  pl.BlockSpec(w_out.shape, lambda i: (0, 0)), pl.BlockSpec((tm, d), row), vec, vec],
        out_specs=[pl.BlockSpec((tm, d), row), pl.BlockSpec((tm, 1), row)],
        out_shape=[jax.ShapeDtypeStruct((t, d), F32), jax.ShapeDtypeStruct((t, 1), F32)],
        compiler_params=_seq(1),
    )(yp, yg, w_out, xhat, gam, bet)


def _outproj_bwd(dh, xhat, rstd, ln_g, w_out, pw, alpha, name):
    t, d = dh.shape
    tm = _row_tile(t)
    gw = w_out.shape[0] - pw

    def body(dh_ref, xh_ref, rs_ref, g_ref, w_ref, dyb_ref, dyp_ref, dyg_ref, dres_ref, dgam_ref, dbet_ref):
        @pl.when(pl.program_id(0) == 0)
        def _():
            dgam_ref[...] = jnp.zeros_like(dgam_ref)
            dbet_ref[...] = jnp.zeros_like(dbet_ref)

        for part in range(ROW_GROUPS):
            rows = pl.ds(part * (tm // ROW_GROUPS), tm // ROW_GROUPS)
            dy = dh_ref[rows, :]
            xh = xh_ref[rows, :]
            dr = _ln_bwd(dy, xh, rs_ref[rows, :], g_ref[...])
            dgam_ref[...] += jnp.sum(dy * xh, axis=0, keepdims=True)
            dbet_ref[...] += jnp.sum(dy, axis=0, keepdims=True)
            drb = dr.astype(BF16)
            dyb_ref[rows, :] = drb
            dres_ref[rows, :] = alpha * dr
            dyp_ref[rows, :] = _mm_nt(drb, w_ref[0:pw, :])
            dyg_ref[rows, :] = _mm_nt(drb, w_ref[pw:, :])

    row = lambda i: (i, 0)
    vec = pl.BlockSpec((1, d), lambda i: (0, 0))
    return _tc_call(
        body, name=name, grid=(t // tm,),
        in_specs=[pl.BlockSpec((tm, d), row), pl.BlockSpec((tm, d), row), pl.BlockSpec((tm, 1), row), vec,
                  pl.BlockSpec(w_out.shape, lambda i: (0, 0))],
        out_specs=[pl.BlockSpec((tm, d), row), pl.BlockSpec((tm, pw), row), pl.BlockSpec((tm, gw), row),
                   pl.BlockSpec((tm, d), row), vec, vec],
        out_shape=[jax.ShapeDtypeStruct((t, d), BF16), jax.ShapeDtypeStruct((t, pw), F32),
                   jax.ShapeDtypeStruct((t, gw), F32), jax.ShapeDtypeStruct((t, d), F32),
                   jax.ShapeDtypeStruct((1, d), F32), jax.ShapeDtypeStruct((1, d), F32)],
        compiler_params=_seq(1),
    )(dh, xhat, rstd, ln_g, w_out)


def _rows_block(r, c):
    best = r
    for cand in range(BF16_ROWS, r, BF16_ROWS):
        if r % cand == 0 and cand * c * 4 <= (1 << 20):
            best = cand
    return best if best * c * 4 <= (4 << 20) else r


def _sum_slots(me, mine, recvs, name, layers_side_by_side=False):
    nl = len(recvs)
    ns, r, c = recvs[0].shape
    tr = _rows_block(r, c)

    def body(me_ref, *refs):
        o_ref = refs[nl * ns]
        for l in range(nl):
            acc = refs[l * ns][...].astype(F32)
            for s in range(1, ns):
                acc = acc + refs[l * ns + s][...].astype(F32)
            if layers_side_by_side:
                o_ref[0, :, l * c:(l + 1) * c] = acc.astype(o_ref.dtype)
            else:
                o_ref[l] = acc.astype(o_ref.dtype)

    def slot(s):
        return pl.BlockSpec((None, tr, c), lambda i, me_ref: ((me_ref[0] + s) % ns, i, 0))

    out = (1, r, nl * c) if layers_side_by_side else (nl, r, c)
    operands = []
    for l in range(nl):
        operands += [mine[l]] + [recvs[l]] * (ns - 1)
    return _tc_call(
        body, name=name,
        grid_spec=pltpu.PrefetchScalarGridSpec(
            num_scalar_prefetch=1, grid=(r // tr,),
            in_specs=[slot(s) for s in range(ns)] * nl,
            out_specs=pl.BlockSpec((out[0], tr, out[2]), lambda i, me_ref: (0, i, 0))),
        out_shape=jax.ShapeDtypeStruct(out, recvs[0].dtype),
        compiler_params=_seq(1),
    )(me, *operands)


def _adamw(w, terms, m, v, name):
    nl, r, c = w.shape
    tc = c
    while tc % (2 * LANE) == 0 and tc > 4 * LANE:
        tc //= 2
    tr = _rows_block(r, tc)
    nterm = len(terms)

    def body(*refs):
        w_ref = refs[0]
        t_refs = refs[1:1 + nterm]
        m_ref, v_ref, g_ref, d_ref, nm_ref, nv_ref = refs[1 + nterm:]
        g = t_refs[0][...].astype(F32)
        for tr_ in t_refs[1:]:
            g = g + tr_[...].astype(F32)
        nm = ADAM_B1 * m_ref[...] + (1.0 - ADAM_B1) * g
        nv = ADAM_B2 * v_ref[...] + (1.0 - ADAM_B2) * jnp.square(g)
        m_hat = nm / (1.0 - ADAM_B1 ** ADAM_STEP)
        v_hat = nv / (1.0 - ADAM_B2 ** ADAM_STEP)
        g_ref[...] = g
        d_ref[...] = -ADAM_LR * (m_hat / (jnp.sqrt(v_hat) + ADAM_EPS) + ADAM_WD * w_ref[...])
        nm_ref[...] = nm
        nv_ref[...] = nv

    spec = pl.BlockSpec((None, tr, tc), lambda l, i, j: (l, i, j))
    shp = jax.ShapeDtypeStruct((nl, r, c), F32)
    return _tc_call(
        body, name=name, grid=(nl, r // tr, c // tc),
        in_specs=[spec] * (3 + nterm), out_specs=[spec] * 4, out_shape=[shp] * 4,
        compiler_params=_seq(3),
    )(w, *terms, m, v)


XY_RELATIONS = ((1, 0, 0), (0, 1, 0), (1, 1, 0))
ALL_RELATIONS = tuple((fx, fy, fc) for fx in (0, 1) for fy in (0, 1) for fc in (0, 1) if fx or fy or fc)
HBM_SPEC = pl.BlockSpec(memory_space=pltpu.HBM)
SEM_SPEC = pl.BlockSpec(memory_space=pltpu.SEMAPHORE)
DATAFLOW = pltpu.SideEffectType.DATAFLOW_SIDE_EFFECTING


def _split_call(body, **kw):
    return pl.pallas_call(body, **kw)


def _flip(v, f):
    return 1 - v if f else v


def _any_spec(n):
    return [pl.BlockSpec(memory_space=pl.ANY)] * n


def _relations(kind):
    return ALL_RELATIONS if kind == "bcast" else XY_RELATIONS


def _copies(kind, arr, land, sems):
    x, y, c = lax.axis_index("x"), lax.axis_index("y"), lax.axis_index("c")
    out = []
    for (fx, fy, fc), (send_sem, recv_sem) in zip(_relations(kind), sems):
        px, py, pc = _flip(x, fx), _flip(y, fy), _flip(c, fc)
        if kind == "bcast":
            mine, theirs = 4 * x + 2 * y + c, 4 * px + 2 * py + pc
        else:
            mine, theirs = 2 * x + y, 2 * px + py
        src, to_mine, to_theirs = arr, land.at[mine], land.at[theirs]
        if kind == "scatter":
            src = arr.at[theirs]
        if kind == "gather_half":
            rows = _my_half(arr.shape[0], c)
            src, to_mine, to_theirs = arr.at[rows], land.at[mine, rows], land.at[theirs, rows]
        both = dict(src_ref=src, send_sem=send_sem, recv_sem=recv_sem, device_id=(px, py, pc), device_id_type=MESH)
        out.append((pltpu.make_async_remote_copy(dst_ref=to_mine, **both),
                    pltpu.make_async_remote_copy(dst_ref=to_theirs, **both)))
    return out


def _my_half(nrows, c):
    return pl.ds(c * (nrows // 2), nrows // 2)


def _share_halves(name, kinds, lands):
    ks = [k for k, kd in enumerate(kinds) if kd == "gather_half"]
    n = len(ks)

    def body(*refs):
        l_refs = refs[n:2 * n]
        send_sems, recv_sems = refs[2 * n:]
        x, y, c = lax.axis_index("x"), lax.axis_index("y"), lax.axis_index("c")
        copies = []
        for i in range(n):
            nrows = l_refs[i].shape[1]
            for r, (fx, fy, _) in enumerate(XY_RELATIONS):
                slot = 2 * _flip(x, fx) + _flip(y, fy)
                both = dict(src_ref=l_refs[i].at[slot, _my_half(nrows, c)], send_sem=send_sems.at[i, r],
                            recv_sem=recv_sems.at[i, r], device_id=(x, y, 1 - c), device_id_type=MESH)
                copies.append((pltpu.make_async_remote_copy(dst_ref=l_refs[i].at[slot, _my_half(nrows, c)], **both),
                               pltpu.make_async_remote_copy(dst_ref=l_refs[i].at[slot, _my_half(nrows, 1 - c)], **both)))
        for send, _ in copies:
            send.start()
        for _, arrival in copies:
            arrival.wait_recv()
        for send, _ in copies:
            send.wait_send()

    outs = _comm_call(
        body, name=name,
        in_specs=_any_spec(n), out_specs=_any_spec(n),
        out_shape=[jax.ShapeDtypeStruct(lands[k].shape, lands[k].dtype) for k in ks],
        input_output_aliases={i: i for i in range(n)},
        scratch_shapes=[pltpu.SemaphoreType.DMA((n, 3)), pltpu.SemaphoreType.DMA((n, 3))],
    )(*[lands[k] for k in ks])
    lands = list(lands)
    for k, o in zip(ks, outs):
        lands[k] = o
    return lands


def _sem_pairs(kinds, sems):
    out, at = [], 0
    for kind in kinds:
        nrel = len(_relations(kind))
        out.append([(sems[at + 2 * r], sems[at + 2 * r + 1]) for r in range(nrel)])
        at += 2 * nrel
    return out


def _exchange_start(name, kinds, arrs, lands):
    n = len(arrs)
    nsem = sum(2 * len(_relations(kd)) for kd in kinds)

    def body(*refs):
        a_refs, l_refs = refs[:n], refs[n:2 * n]
        pairs = _sem_pairs(kinds, refs[2 * n:2 * n + nsem])
        token = refs[-1]
        for k in range(n):
            for send, _ in _copies(kinds[k], a_refs[k], l_refs[k], pairs[k]):
                send.start()
        token[...] = jnp.zeros_like(token)

    thru = [pltpu.HBM(a.shape, a.dtype) for a in list(arrs) + list(lands)]
    outs = _split_call(
        body, name=name,
        out_shape=(*[pltpu.SemaphoreType.DMA(())] * nsem, *thru, jax.ShapeDtypeStruct((8, LANE), F32)),
        in_specs=[HBM_SPEC] * (2 * n),
        out_specs=(*[SEM_SPEC] * nsem, *[HBM_SPEC] * (2 * n), pl.BlockSpec(memory_space=pltpu.VMEM)),
        input_output_aliases={i: nsem + i for i in range(2 * n)},
        compiler_params=pltpu.CompilerParams(has_side_effects=DATAFLOW),
    )(*[pltpu.with_memory_space_constraint(a, pltpu.HBM) for a in list(arrs) + list(lands)])
    return dict(kinds=kinds, sems=outs[:nsem], arrs=outs[nsem:nsem + n], lands=outs[nsem + n:nsem + 2 * n],
                token=outs[-1])


def _exchange_wait(name, st, after):
    kinds = st["kinds"]
    n = len(kinds)
    nsem = len(st["sems"])

    def body(*refs):
        a_refs, l_refs = refs[:n], refs[n:2 * n]
        pairs = _sem_pairs(kinds, refs[2 * n:2 * n + nsem])
        for k in range(n):
            for _, arrival in _copies(kinds[k], a_refs[k], l_refs[k], pairs[k]):
                arrival.wait_send()
                arrival.wait_recv()
        refs[-1][...] = jnp.zeros_like(refs[-1])

    ins = list(st["arrs"]) + list(st["lands"])
    outs = _split_call(
        body, name=name,
        out_shape=[pltpu.HBM(a.shape, a.dtype) for a in ins] + [jax.ShapeDtypeStruct((8, LANE), F32)],
        in_specs=[HBM_SPEC] * (2 * n) + [SEM_SPEC] * nsem + [pl.BlockSpec(memory_space=pl.ANY)],
        out_specs=[HBM_SPEC] * (2 * n) + [pl.BlockSpec(memory_space=pltpu.VMEM)],
        input_output_aliases={i: i for i in range(2 * n)},
        compiler_params=pltpu.CompilerParams(has_side_effects=DATAFLOW),
    )(*ins, *st["sems"], after)
    return outs[:n], outs[n:2 * n], outs[-1]


def _landing(own, slot, nslot):
    return lax.dynamic_update_slice(lax.empty((nslot,) + own.shape, own.dtype), own[None], (slot,) + (0,) * own.ndim)


def _swap_sibling(parts, name):
    n = len(parts)

    def body(*refs):
        ins, outs = refs[:n], refs[n:2 * n]
        send_sems, recv_sems = refs[2 * n:]
        sib = (lax.axis_index("x"), lax.axis_index("y"), 1 - lax.axis_index("c"))
        cps = [pltpu.make_async_remote_copy(src_ref=ins[k], dst_ref=outs[k], send_sem=send_sems.at[k],
                                            recv_sem=recv_sems.at[k], device_id=sib, device_id_type=MESH)
               for k in range(n)]
        for cp in cps:
            cp.start()
        for cp in cps:
            cp.wait_recv()
        for cp in cps:
            cp.wait_send()

    return _comm_call(
        body, name=name,
        in_specs=_any_spec(n), out_specs=_any_spec(n),
        out_shape=[jax.ShapeDtypeStruct(p.shape, p.dtype) for p in parts],
        scratch_shapes=[pltpu.SemaphoreType.DMA((n,)), pltpu.SemaphoreType.DMA((n,))],
    )(*parts)


def _col_shards(a, n=N_SHARD):
    r, c = a.shape
    return a.reshape(r, n, c // n).transpose(1, 0, 2)


def _from_col_shards(a):
    n, r, cs = a.shape
    return a.transpose(1, 0, 2).reshape(r, n * cs)


def kernel(x, meta_tokens, ffn1_w_gate, ffn1_w_up, ffn1_w_down, ln1_g, ln1_b, w_in, w_gate_up, b_gate, w_pool, pool_scale, gla_norm_g, w_out, ln2_g, ln2_b, ffn2_w_gate, ffn2_w_up, ffn2_w_down, ln3_g, ln3_b, loss_target, m_meta_tokens, m_ffn1_w_gate, m_ffn1_w_up, m_ffn1_w_down, m_ln1_g, m_ln1_b, m_w_in, m_w_gate_up, m_b_gate, m_w_pool, m_pool_scale, m_gla_norm_g, m_w_out, m_ln2_g, m_ln2_b, m_ffn2_w_gate, m_ffn2_w_up, m_ffn2_w_down, m_ln3_g, m_ln3_b, v_meta_tokens, v_ffn1_w_gate, v_ffn1_w_up, v_ffn1_w_down, v_ln1_g, v_ln1_b, v_w_in, v_w_gate_up, v_b_gate, v_w_pool, v_pool_scale, v_gla_norm_g, v_w_out, v_ln2_g, v_ln2_b, v_ffn2_w_gate, v_ffn2_w_up, v_ffn2_w_down, v_ln3_g, v_ln3_b):
    w = dict(meta_tokens=meta_tokens, ffn1_w_gate=ffn1_w_gate, ffn1_w_up=ffn1_w_up, ffn1_w_down=ffn1_w_down,
             ln1_g=ln1_g, ln1_b=ln1_b, w_in=w_in, w_gate_up=w_gate_up, b_gate=b_gate, w_pool=w_pool,
             pool_scale=pool_scale, gla_norm_g=gla_norm_g, w_out=w_out, ln2_g=ln2_g, ln2_b=ln2_b,
             ffn2_w_gate=ffn2_w_gate, ffn2_w_up=ffn2_w_up, ffn2_w_down=ffn2_w_down, ln3_g=ln3_g, ln3_b=ln3_b)
    mom1 = dict(meta_tokens=m_meta_tokens, ffn1_w_gate=m_ffn1_w_gate, ffn1_w_up=m_ffn1_w_up,
                ffn1_w_down=m_ffn1_w_down, ln1_g=m_ln1_g, ln1_b=m_ln1_b, w_in=m_w_in, w_gate_up=m_w_gate_up,
                b_gate=m_b_gate, w_pool=m_w_pool, pool_scale=m_pool_scale, gla_norm_g=m_gla_norm_g, w_out=m_w_out,
                ln2_g=m_ln2_g, ln2_b=m_ln2_b, ffn2_w_gate=m_ffn2_w_gate, ffn2_w_up=m_ffn2_w_up,
                ffn2_w_down=m_ffn2_w_down, ln3_g=m_ln3_g, ln3_b=m_ln3_b)
    mom2 = dict(meta_tokens=v_meta_tokens, ffn1_w_gate=v_ffn1_w_gate, ffn1_w_up=v_ffn1_w_up,
                ffn1_w_down=v_ffn1_w_down, ln1_g=v_ln1_g, ln1_b=v_ln1_b, w_in=v_w_in, w_gate_up=v_w_gate_up,
                b_gate=v_b_gate, w_pool=v_w_pool, pool_scale=v_pool_scale, gla_norm_g=v_gla_norm_g, w_out=v_w_out,
                ln2_g=v_ln2_g, ln2_b=v_ln2_b, ffn2_w_gate=v_ffn2_w_gate, ffn2_w_up=v_ffn2_w_up,
                ffn2_w_down=v_ffn2_w_down, ln3_g=v_ln3_g, ln3_b=v_ln3_b)

    xs = x[0]
    s_len, d = xs.shape
    nl = ln1_g.shape[0]
    alpha = (2.0 * nl) ** 0.25
    t_real = N_META + s_len
    t_pad = -(-t_real // LANE) * LANE
    pw = pool_scale.shape[1]
    kw = b_gate.shape[1]
    vw = gla_norm_g.shape[1]
    rank = w_gate_up.shape[1]
    widths = (pw, kw, kw, vw, vw)
    n_main = sum(widths)
    dff_s = ffn1_w_gate.shape[2]
    dff_c = N_SHARD * dff_s // FFN_CHUNKS

    me_xy = 2 * lax.axis_index("x") + lax.axis_index("y")
    me_all = 2 * me_xy + lax.axis_index("c")
    ffn1_names = ("ffn1_w_gate", "ffn1_w_up", "ffn1_w_down")
    mix_names = ("w_out", "w_gate_up", "w_in")
    ffn2_names = ("ffn2_w_gate", "ffn2_w_up", "ffn2_w_down")

    gate_up = ("ffn1_w_gate", "ffn1_w_up", "ffn2_w_gate", "ffn2_w_up")

    def stored(n, a):
        if n in gate_up:
            return jnp.swapaxes(a, 1, 2)
        return jnp.transpose(a, (2, 0, 1)) if n == "w_in" else a

    def as_given(n, a):
        if n in gate_up:
            return jnp.swapaxes(a, 1, 2)
        if n == "w_in":
            return jnp.transpose(a.reshape(-1, nl, d), (1, 2, 0))
        return a.reshape(w[n].shape)

    stages = [[("meta_tokens", None)], [(n, 0) for n in ffn1_names], [(n, 0) for n in mix_names + ffn2_names]]
    stages += [[(n, l) for n in BIG] for l in range(1, nl)]
    gathers, wa = {}, {}

    halved = ffn1_names + ffn2_names + ("w_out",)

    def start_gather(si, dep=None):
        own = []
        for n, l in stages[si]:
            a = meta_tokens if l is None else (stored(n, w[n])[:, l] if n == "w_in" else stored(n, w[n])[l])
            a = a if dep is None else a + dep
            own.append(a if l is None else a.astype(BF16))
        gathers[si] = _exchange_start(f"gather_start_{si}", ["gather_half" if n in halved else "gather"
                                                             for n, _ in stages[si]], own,
                                      [_landing(a, me_xy, N_SHARD) for a in own])
        return gathers[si]["token"]

    def arrive(si, after):
        _, lands, token = _exchange_wait(f"gather_wait_{si}", gathers[si], after)
        if any(kd == "gather_half" for kd in gathers[si]["kinds"]):
            lands = _share_halves(f"gather_share_{si}", gathers[si]["kinds"], lands)
        for item, a in zip(stages[si], lands):
            wa[item] = a.reshape(FFN_CHUNKS, -1, d) if item[0] in ffn1_names + ffn2_names else a
        return token

    def mixer_weights(l):
        wi = wa["w_in", l].reshape(-1, d)
        return dict(w_main=wi[:n_main], w_lr=jnp.pad(wi[n_main:], ((0, LANE - rank), (0, 0))),
                    wgu=jnp.pad(_from_col_shards(wa["w_gate_up", l]), ((0, LANE - rank), (0, 0))),
                    wout=wa["w_out", l].reshape(-1, d))

    wp16 = w_pool.astype(BF16)
    ones = jnp.ones((1, d), F32)
    zeros = jnp.zeros((1, d), F32)
    target = jnp.concatenate([jnp.zeros((N_META, d), F32), loss_target[0], jnp.zeros((t_pad - t_real, d), F32)], axis=0)

    started = start_gather(0)
    for si in range(1, len(stages)):
        started = start_gather(si, started[0:1, 0:1])
    arrive(0, started)
    meta_full = _from_col_shards(wa["meta_tokens", None])
    h0 = jnp.concatenate([meta_full, xs, jnp.zeros((t_pad - t_real, d), F32)], axis=0)
    arrive(1, h0[:8, :LANE] + target[:8, :LANE])

    saved, mw = [], []
    cur, cur_g, cur_b = h0, ones, zeros
    for l in range(nl):
        s = {}
        xh1, rs1, hb0, g1, u1 = _ffn_fwd(cur, cur_g, cur_b, wa["ffn1_w_gate", l], wa["ffn1_w_up", l],
                                         wa["ffn1_w_down", l], alpha, f"ffn1_fwd_{l}")
        if l == 0:
            arrive(2, xh1)
        mw.append(mixer_weights(l))
        up, q, k, v, r, zg, la, hb1 = _inproj_fwd(xh1, ln1_g[l:l + 1], ln1_b[l:l + 1], mw[l]["w_main"], mw[l]["w_lr"],
                                                  mw[l]["wgu"], b_gate[l:l + 1], widths, f"inproj_fwd_{l}")
        yp, pb = _pool_fwd(up, wp16[l], pool_scale[l:l + 1], f"pool_fwd_{l}")
        o, yg, sall = _gla_fwd(q, k, v, la, r, gla_norm_g[l:l + 1], f"gla_fwd_{l}")
        xh2, rs2 = _outproj_fwd(yp, yg, mw[l]["wout"], xh1, ln1_g[l:l + 1], ln1_b[l:l + 1], alpha, f"outproj_fwd_{l}")
        if l + 1 < nl:
            arrive(l + 3, xh2)
        head = (ln3_g[l:l + 1], ln3_b[l:l + 1], target, s_len) if l == nl - 1 else None
        xh3, rs3, hb2, g2, u2, *at_head = _ffn_fwd(xh2, ln2_g[l:l + 1], ln2_b[l:l + 1], wa["ffn2_w_gate", l],
                                                   wa["ffn2_w_up", l], wa["ffn2_w_down", l], alpha, f"ffn2_fwd_{l}",
                                                   head)
        s.update(xh1=xh1, rs1=rs1, hb0=hb0, g1=g1, u1=u1, q=q, k=k, v=v, r=r, zg=zg, la=la, hb1=hb1, yp=yp, pb=pb,
                 o=o, yg=yg, sall=sall, xh2=xh2, rs2=rs2, xh3=xh3, rs3=rs3, hb2=hb2, g2=g2, u2=u2)
        saved.append(s)
        cur, cur_g, cur_b = xh3, ln3_g[l:l + 1], ln3_b[l:l + 1]

    dh, loss_acc = at_head
    loss = lax.psum(loss_acc[0, 0], ("x", "y", "c"))

    small_grads = {n: [None] * nl for n in SMALL}
    scatters = []

    def depart(name, items, grads, kinds=None):
        lands = [_landing(g, me_all, N_DEV) if kd == "bcast" else lax.empty(g.shape, g.dtype)
                 for g, kd in zip(grads, kinds or ["scatter"] * len(grads))]
        st = _exchange_start(name, kinds or ["scatter"] * len(grads), grads, lands)
        scatters.append((name, items, st))
        return st["token"]

    def pack(parts):
        flat = jnp.concatenate([parts[n].reshape(-1) for n in SMALL])
        return flat.reshape(-1, LANE)

    def ffn_wgrad(n, l, hb, dgb, dub, act, dfb, after=None):
        if n.endswith("down"):
            dw = _wgrad(act, dfb, dff_c, d, f"{n}_grad_{l}", after)
        else:
            dw = _wgrad(dgb if n.endswith("gate") else dub, hb, dff_c, d, f"{n}_grad_{l}", after)
        return dw.reshape(N_SHARD, dff_s, d)

    late = []
    for l in reversed(range(nl)):
        s = saved[l]
        dh, dfb, dgb, dub, act, dgam, dbet = _ffn_bwd(dh, s["xh3"], s["rs3"], ln3_g[l:l + 1], s["g2"], s["u2"],
                                                      wa["ffn2_w_gate", l], wa["ffn2_w_up", l], wa["ffn2_w_down", l],
                                                      alpha, f"ffn2_bwd_{l}")
        small_grads["ln3_g"][l], small_grads["ln3_b"][l] = dgam, dbet
        gone = depart(f"scatter_start_ffn2_{l}", [(n, l) for n in ffn2_names],
                      [ffn_wgrad(n, l, s["hb2"], dgb, dub, act, dfb) for n in ffn2_names])

        dyb, dyp, dyg, dres, dgam, dbet = _outproj_bwd(dh, s["xh2"], s["rs2"], ln2_g[l:l + 1] + gone[0:1, 0:1],
                                                       mw[l]["wout"], pw, alpha, f"outproj_bwd_{l}")
        small_grads["ln2_g"][l], small_grads["ln2_b"][l] = dgam, dbet
        dwo = jnp.concatenate([_wgrad(s["yp"], dyb, pw, d, f"dwout_pool_{l}"),
                               _wgrad(s["yg"], dyb, vw, d, f"dwout_gla_{l}")], axis=0)
        dq, dk, dv, dr, dzg, dwgu, dbg, dgn = _gla_bwd(dyg, s["o"], s["r"], gla_norm_g[l:l + 1], s["q"], s["k"],
                                                       s["v"], s["la"], s["zg"], s["sall"], mw[l]["wgu"],
                                                       f"gla_bwd_{l}")
        dup, dwp, dsc = _pool_bwd(dyp, s["pb"], wp16[l], pool_scale[l:l + 1], f"pool_bwd_{l}")
        small_grads["b_gate"][l], small_grads["gla_norm_g"][l] = dbg, dgn
        small_grads["w_pool"][l], small_grads["pool_scale"][l] = dwp, dsc
        dh, dz = _inproj_bwd(dres, [dup, dq, dk, dv, dr], dzg, mw[l]["w_main"], mw[l]["w_lr"], f"inproj_bwd_{l}")
        dwi = jnp.concatenate([_wgrad(dz, s["hb1"], 4 * LANE, d, f"dwin_main_{l}"),
                               _wgrad(dzg, s["hb1"], LANE, d, f"dwin_lr_{l}")[:rank]], axis=0)
        gone = depart(f"scatter_start_mix_{l}", [(n, l) for n in mix_names],
                      [dwo.reshape(N_SHARD, -1, d), _col_shards(dwgu[:rank]), dwi.reshape(N_SHARD, -1, d)])

        dh, dfb, dgb, dub, act, dgam, dbet = _ffn_bwd(dh, s["xh1"], s["rs1"], ln1_g[l:l + 1] + gone[0:1, 0:1],
                                                      s["g1"], s["u1"], wa["ffn1_w_gate", l], wa["ffn1_w_up", l],
                                                      wa["ffn1_w_down", l], alpha, f"ffn1_bwd_{l}")
        small_grads["ln1_g"][l], small_grads["ln1_b"][l] = dgam, dbet
        if l:
            gone = depart(f"scatter_start_ffn1_{l}", [(n, l) for n in ffn1_names],
                          [ffn_wgrad(n, l, s["hb0"], dgb, dub, act, dfb) for n in ffn1_names])
            ln3_g = ln3_g.at[l - 1:l].add(gone[0:1, 0:1])
            continue
        grad_x = dh[N_META:t_real][None]
        small_vec = pack({n: jnp.stack(small_grads[n]) for n in SMALL})
        gone = depart("scatter_start_rest", [("meta_tokens", 0), ("small", 0)],
                      [_col_shards(dh[:N_META]), small_vec], ["scatter", "bcast"])
        for n in ffn1_names:
            g = ffn_wgrad(n, l, s["hb0"], dgb, dub, act, dfb, after=gone)
            gone = depart(f"scatter_start_{n}", [(n, l)], [g])
            late.append(scatters.pop())

    sent, recv, results, firsts = {}, {}, {}, []
    my_slot = me_xy.reshape(1).astype(jnp.int32)

    def collect(group, after):
        for name, items, st in group:
            arrs, lands, _ = _exchange_wait(name.replace("start", "wait"), st, after)
            for item, a, b in zip(items, arrs, lands):
                sent[item], recv[item] = a, b

    def reduce_and_update(names, tag):
        partial = []
        for n in names:
            layers = [(n, l) for l in range(1 if n == "meta_tokens" else nl)]
            partial.append(_sum_slots(my_slot, [sent[it] for it in layers], [recv[it] for it in layers],
                                      f"sum_{n}", n == "w_in"))
        for n, mine, theirs in zip(names, partial, _swap_sibling(partial, f"swap_sibling_{tag}")):
            fit = lambda a: stored(n, a).reshape(mine.shape)
            outs = _adamw(fit(w[n]), [mine, theirs], fit(mom1[n]), fit(mom2[n]), f"adamw_{n}")
            results[n] = [as_given(n, o) for o in outs]
            firsts.append(outs[1][0, 0, 0])

    collect(scatters, gone)
    early = [n for n in ("meta_tokens",) + BIG if n not in ffn1_names]
    reduce_and_update(early, "early")
    small_terms = [recv["small", 0][i][None] for i in range(N_DEV)]
    souts = _adamw(pack(w)[None], small_terms, pack(mom1)[None], pack(mom2)[None], "adamw_small")
    off = 0
    for n in SMALL:
        size = w[n].size
        results[n] = [o.reshape(-1)[off:off + size].reshape(w[n].shape) for o in souts]
        off += size
    collect(late, souts[0][0, :8] + functools.reduce(jnp.add, firsts))
    reduce_and_update(ffn1_names, "late")

    out = [loss, grad_x]
    for part in range(4):
        out += [results[n][part] for n in WEIGHTS]
    return tuple(out)
```

```python
import functools

import jax
import jax.numpy as jnp
from jax import lax
from jax.experimental import pallas as pl
from jax.experimental.pallas import tpu as pltpu

F32 = jnp.float32
BF16 = jnp.bfloat16
MESH = pl.DeviceIdType.MESH

N_META = 16
POOL_WINDOWS = (2, 4, 8, 16)
POOL_HALO = 16
N_HEADS = 4
GLA_GATE_TEMP = 16.0
CHUNK = 128
CHUNK_UNROLL = 5
LN_EPS = 1e-5
RMS_EPS = 1e-6
ADAM_LR = 0.001
ADAM_B1 = 0.9
ADAM_B2 = 0.999
ADAM_EPS = 1e-08
ADAM_WD = 0.01
ADAM_STEP = 10
LANE = 128
BF16_ROWS = 16
ROW_TILE = 640
FFN_ROW_TILE = 640
FFN_CHUNKS = 4
FFN_SPLIT = 2
ROW_GROUPS = 2
WGRAD_K_MAX = 4224
N_SHARD = 4
N_DEV = 8

BIG = ("ffn1_w_gate", "ffn1_w_up", "ffn1_w_down", "w_in", "w_gate_up", "w_out",
       "ffn2_w_gate", "ffn2_w_up", "ffn2_w_down")
SMALL = ("ln1_g", "ln1_b", "b_gate", "w_pool", "pool_scale", "gla_norm_g", "ln2_g", "ln2_b", "ln3_g", "ln3_b")
WEIGHTS = ("meta_tokens", "ffn1_w_gate", "ffn1_w_up", "ffn1_w_down", "ln1_g", "ln1_b", "w_in", "w_gate_up",
           "b_gate", "w_pool", "pool_scale", "gla_norm_g", "w_out", "ln2_g", "ln2_b", "ffn2_w_gate",
           "ffn2_w_up", "ffn2_w_down", "ln3_g", "ln3_b")


def _tc_call(body, **kw):
    return pl.pallas_call(body, **kw)


def _comm_call(body, **kw):
    return pl.pallas_call(body, **kw)


def _seq(n):
    return pltpu.CompilerParams(dimension_semantics=("arbitrary",) * n)


def _mm(a, b):
    return jnp.dot(a.astype(BF16), b.astype(BF16), preferred_element_type=F32)


def _mm_nt(a, b):
    return lax.dot_general(a.astype(BF16), b.astype(BF16), (((1,), (1,)), ((), ())), preferred_element_type=F32)


def _mm_tn(a, b):
    return lax.dot_general(a.astype(BF16), b.astype(BF16), (((0,), (0,)), ((), ())), preferred_element_type=F32)


def _mm_01(a, b):
    hi = b.astype(BF16)
    lo = (b - hi.astype(F32)).astype(BF16)
    a = a.astype(BF16)
    return jnp.dot(a, hi, preferred_element_type=F32) + jnp.dot(a, lo, preferred_element_type=F32)


def _row_tile(t, most=None):
    tm = min(most or ROW_TILE, t)
    while t % tm:
        tm -= LANE
    return tm


def _silu_parts(g):
    sg = jax.nn.sigmoid(g)
    return sg, g * sg


def _ln_stats(r):
    mu = jnp.mean(r, axis=-1, keepdims=True)
    rc = r - mu
    var = jnp.mean(rc * rc, axis=-1, keepdims=True)
    rs = lax.rsqrt(var + LN_EPS)
    return rc * rs, rs


def _ln_bwd(dy, xh, rs, gam):
    dyg = dy * gam
    c1 = jnp.mean(dyg, axis=-1, keepdims=True)
    c2 = jnp.mean(dyg * xh, axis=-1, keepdims=True)
    return rs * (dyg - c1 - xh * c2)


def _ffn_fwd(xin, gam_in, bet_in, wg, wu, wd, alpha, name, loss_head=None):
    t, d = xin.shape
    nj, tf, _ = wg.shape
    tm = _row_tile(t, FFN_ROW_TILE)
    nt = t // tm
    share = tm // nj
    head = [] if loss_head is None else list(loss_head[:3])

    def body(x_ref, gi_ref, bi_ref, wg_ref, wu_ref, wd_ref, *rest):
        head_refs, rest = rest[:len(head)], rest[len(head):]
        xhat_ref, rstd_ref, hb_ref, go_ref, uo_ref = rest[:5]
        acc, hbs = rest[-2:]
        i = pl.program_id(0)
        j = pl.program_id(1)
        cur = i % 2

        if head:
            @pl.when((i == 0) & (j == 0))
            def _():
                rest[6][...] = jnp.zeros_like(rest[6])

        def norm_previous():
            rows = pl.ds(pl.multiple_of(j * share, BF16_ROWS), share)
            xhat, rs = _ln_stats(0.5 * acc[1 - cur, rows, :])
            xhat_ref[rows, :] = xhat
            rstd_ref[rows, :] = rs
            if head:
                g_ref, b_ref, t_ref = head_refs
                dy_ref, loss_ref = rest[5], rest[6]
                rowi = (i - 1) * tm + j * share + lax.broadcasted_iota(jnp.int32, (share, 1), 0)
                live = (rowi >= N_META) & (rowi < N_META + loss_head[3])
                diff = jnp.where(live, xhat * g_ref[...] + b_ref[...] - t_ref[rows, :], 0.0)
                dy_ref[rows, :] = diff * (1.0 / d)
                loss_ref[...] += jnp.sum(diff * diff) * (0.5 / d)

        @pl.when(i < nt)
        def _():
            @pl.when(j == 0)
            def _():
                h = x_ref[...] * gi_ref[...] + bi_ref[...]
                hb = h.astype(BF16)
                hbs[...] = hb
                hb_ref[...] = hb
                acc[cur] = (2.0 * alpha) * h

                @pl.when(i == 0)
                def _():
                    acc[1] = jnp.zeros((tm, d), F32)

            norm_previous()
            hb = hbs[...]
            g = _mm_nt(hb, wg_ref[...])
            u = _mm_nt(hb, wu_ref[...])
            _, sl = _silu_parts(g)
            go_ref[...] = g.astype(BF16)
            uo_ref[...] = u.astype(BF16)
            acc[cur] += jnp.dot((sl * u).astype(BF16), wd_ref[...], preferred_element_type=F32)

        @pl.when(i == nt)
        def _():
            norm_previous()

    here = lambda i, j: (jnp.minimum(i, nt - 1), 0)
    before = lambda i, j: (jnp.maximum(i - 1, 0), 0)
    chunk = lambda i, j: (jnp.where(i < nt, j, nj - 1), 0, 0)
    col = pl.BlockSpec((None, tm, tf), lambda i, j: (jnp.where(i < nt, j, nj - 1), jnp.minimum(i, nt - 1), 0))
    vec = pl.BlockSpec((1, d), lambda i, j: (0, 0))
    head_in = [vec, vec, pl.BlockSpec((tm, d), before)] if head else []
    head_out = [pl.BlockSpec((tm, d), before), pl.BlockSpec((8, LANE), lambda i, j: (0, 0))] if head else []
    head_shape = [jax.ShapeDtypeStruct((t, d), F32), jax.ShapeDtypeStruct((8, LANE), F32)] if head else []
    return _tc_call(
        body, name=name, grid=(nt + 1, nj),
        in_specs=[pl.BlockSpec((tm, d), here), vec, vec] + [pl.BlockSpec((None, tf, d), chunk)] * 3 + head_in,
        out_specs=[pl.BlockSpec((tm, d), before), pl.BlockSpec((tm, 1), before), pl.BlockSpec((tm, d), here), col, col]
                  + head_out,
        out_shape=[jax.ShapeDtypeStruct((t, d), F32), jax.ShapeDtypeStruct((t, 1), F32),
                   jax.ShapeDtypeStruct((t, d), BF16), jax.ShapeDtypeStruct((nj, t, tf), BF16),
                   jax.ShapeDtypeStruct((nj, t, tf), BF16)] + head_shape,
        scratch_shapes=[pltpu.VMEM((2, tm, d), F32), pltpu.VMEM((tm, d), BF16)],
        compiler_params=_seq(2),
    )(xin, gam_in, bet_in, wg, wu, wd, *head)


def _ffn_bwd(dh, xhat, rstd, ln_g, gb, ub, wg, wu, wd, alpha, name):
    t, d = dh.shape
    nj, tf, _ = wg.shape
    tm = _row_tile(t, FFN_ROW_TILE)
    nt = t // tm
    share = tm // nj

    def body(dh_ref, xh_ref, rs_ref, g_ref, gb_ref, ub_ref, wg_ref, wu_ref, wd_ref,
             dhin_ref, df_ref, dg_ref, du_ref, act_ref, dgam_ref, dbet_ref, df_s, dres_next, df_next):
        i = pl.program_id(0)
        j = pl.program_id(1)

        @pl.when((i == 0) & (j == 0))
        def _():
            dgam_ref[...] = jnp.zeros_like(dgam_ref)
            dbet_ref[...] = jnp.zeros_like(dbet_ref)

        def look_ahead():
            rows = pl.ds(pl.multiple_of(j * share, BF16_ROWS), share)
            dy = dh_ref[rows, :]
            xh = xh_ref[rows, :]
            dr = _ln_bwd(dy, xh, rs_ref[rows, :], g_ref[...])
            dres_next[rows, :] = alpha * dr
            df_next[rows, :] = (0.5 * dr).astype(BF16)
            live = jnp.where(i < nt, 1.0, 0.0)
            dgam_ref[...] += live * jnp.sum(dy * xh, axis=0, keepdims=True)
            dbet_ref[...] += live * jnp.sum(dy, axis=0, keepdims=True)

        @pl.when(i == 0)
        def _():
            look_ahead()

        @pl.when(i > 0)
        def _():
            @pl.when(j == 0)
            def _():
                dhin_ref[...] = dres_next[...]
                dfb = df_next[...]
                df_s[...] = dfb
                df_ref[...] = dfb

            look_ahead()
            for part in range(FFN_SPLIT):
                rows = pl.ds(part * (tm // FFN_SPLIT), tm // FFN_SPLIT)
                dact = _mm_nt(df_s[rows, :], wd_ref[...])
                g = gb_ref[rows, :].astype(F32)
                u = ub_ref[rows, :].astype(F32)
                sg, sl = _silu_parts(g)
                dg = (dact * u * (sg + sl * (1.0 - sg))).astype(BF16)
                du = (dact * sl).astype(BF16)
                dg_ref[rows, :] = dg
                du_ref[rows, :] = du
                act_ref[rows, :] = (sl * u).astype(BF16)
                dhin_ref[rows, :] += _mm(dg, wg_ref[...]) + _mm(du, wu_ref[...])

    ahead = lambda i, j: (jnp.minimum(i, nt - 1), 0)
    row = lambda i, j: (jnp.maximum(i - 1, 0), 0)
    chunk = lambda i, j: (jnp.where(i > 0, j, 0), 0, 0)
    col = pl.BlockSpec((None, tm, tf), lambda i, j: (jnp.where(i > 0, j, 0), jnp.maximum(i - 1, 0), 0))
    vec = pl.BlockSpec((1, d), lambda i, j: (0, 0))
    ff = jax.ShapeDtypeStruct((nj, t, tf), BF16)
    return _tc_call(
        body, name=name, grid=(nt + 1, nj),
        in_specs=[pl.BlockSpec((tm, d), ahead), pl.BlockSpec((tm, d), ahead), pl.BlockSpec((tm, 1), ahead), vec,
                  col, col] + [pl.BlockSpec((None, tf, d), chunk)] * 3,
        out_specs=[pl.BlockSpec((tm, d), row), pl.BlockSpec((tm, d), row), col, col, col, vec, vec],
        out_shape=[jax.ShapeDtypeStruct((t, d), F32), jax.ShapeDtypeStruct((t, d), BF16), ff, ff, ff,
                   jax.ShapeDtypeStruct((1, d), F32), jax.ShapeDtypeStruct((1, d), F32)],
        scratch_shapes=[pltpu.VMEM((tm, d), BF16), pltpu.VMEM((tm, d), F32), pltpu.VMEM((tm, d), BF16)],
        compiler_params=_seq(2),
    )(dh, xhat, rstd, ln_g, gb, ub, wg, wu, wd)


def _wgrad(a, b, tmm, tn, name, after=None):
    t = a.shape[-2]
    m = a.shape[-1] * (a.shape[0] if a.ndim == 3 else 1)
    n = b.shape[-1] * (b.shape[0] if b.ndim == 3 else 1)
    tk = max(k for k in range(BF16_ROWS, WGRAD_K_MAX + 1, BF16_ROWS) if t % k == 0)
    nk = t // tk
    extra = [] if after is None else [after]

    def body(a_ref, b_ref, *rest):
        o_ref, acc = rest[len(extra):]
        k = pl.program_id(2)

        @pl.when(k == 0)
        def _():
            acc[...] = jnp.zeros_like(acc)

        acc[...] += _mm_tn(a_ref[...], b_ref[...])

        @pl.when(k == nk - 1)
        def _():
            o_ref[...] = acc[...].astype(o_ref.dtype)

    a_spec = (pl.BlockSpec((None, tk, tmm), lambda i, j, k: (i, k, 0)) if a.ndim == 3
              else pl.BlockSpec((tk, tmm), lambda i, j, k: (k, i)))
    return _tc_call(
        body, name=name, grid=(m // tmm, n // tn, nk),
        in_specs=[a_spec, pl.BlockSpec((None, tk, tn), lambda i, j, k: (j, k, 0)) if b.ndim == 3
                  else pl.BlockSpec((tk, tn), lambda i, j, k: (k, j))] + [pl.BlockSpec(memory_space=pl.ANY)] * len(extra),
        out_specs=pl.BlockSpec((tmm, tn), lambda i, j, k: (i, j)),
        out_shape=jax.ShapeDtypeStruct((m, n), BF16),
        scratch_shapes=[pltpu.VMEM((tmm, tn), F32)],
        compiler_params=_seq(3),
    )(a, b, *extra)


def _inproj_fwd(xhat, gam, bet, w_main, w_lr, wgu, b_gate, widths, name):
    t, d = xhat.shape
    tm = _row_tile(t)
    kw = wgu.shape[1]
    offs = [0]
    for w in widths:
        offs.append(offs[-1] + w)

    def body(x_ref, g_ref, b_ref, wm_ref, wl_ref, wgu_ref, bg_ref, *outs):
        piece_refs, (zg_ref, la_ref, hb_ref) = outs[:len(widths)], outs[len(widths):]
        hb = (x_ref[...] * g_ref[...] + b_ref[...]).astype(BF16)
        hb_ref[...] = hb
        for p, ref in enumerate(piece_refs):
            ref[...] = _mm_nt(hb, wm_ref[offs[p]:offs[p + 1], :]).astype(ref.dtype)
        zg = _mm_nt(hb, wl_ref[...])
        zg_ref[...] = zg
        logit = _mm(zg, wgu_ref[...]) + bg_ref[...]
        la_ref[...] = (jnp.minimum(logit, 0.0) - jnp.log(1.0 + jnp.exp(-jnp.abs(logit)))) * (1.0 / GLA_GATE_TEMP)

    row = lambda i: (i, 0)
    full = lambda a: pl.BlockSpec(a.shape, lambda i: (0,) * a.ndim)
    out_w = list(widths) + [LANE, kw]
    out_t = [F32, F32, F32, BF16, F32, F32, F32]
    return _tc_call(
        body, name=name, grid=(t // tm,),
        in_specs=[pl.BlockSpec((tm, d), row), full(gam), full(bet), full(w_main), full(w_lr), full(wgu), full(b_gate)],
        out_specs=[pl.BlockSpec((tm, w), row) for w in out_w] + [pl.BlockSpec((tm, d), row)],
        out_shape=[jax.ShapeDtypeStruct((t, w), ty) for w, ty in zip(out_w, out_t)]
                  + [jax.ShapeDtypeStruct((t, d), BF16)],
        compiler_params=_seq(1),
    )(xhat, gam, bet, w_main, w_lr, wgu, b_gate)


def _inproj_bwd(dh_part, pieces, dzg, w_main, w_lr, name):
    t, d = dh_part.shape
    tm = _row_tile(t)
    widths = [p.shape[1] for p in pieces]
    offs = [0]
    for w in widths:
        offs.append(offs[-1] + w)

    def body(*refs):
        dhp_ref = refs[0]
        p_refs = refs[1:1 + len(widths)]
        dzg_ref, wm_ref, wl_ref, dh_ref, dz_ref = refs[1 + len(widths):]
        acc = dhp_ref[...] + _mm(dzg_ref[...], wl_ref[...])
        for p, ref in enumerate(p_refs):
            v = ref[...]
            dz_ref[:, offs[p]:offs[p + 1]] = v
            acc += _mm(v, wm_ref[offs[p]:offs[p + 1], :])
        dh_ref[...] = acc

    row = lambda i: (i, 0)
    full = lambda a: pl.BlockSpec(a.shape, lambda i: (0,) * a.ndim)
    return _tc_call(
        body, name=name, grid=(t // tm,),
        in_specs=[pl.BlockSpec((tm, d), row)] + [pl.BlockSpec((tm, w), row) for w in widths]
                 + [pl.BlockSpec((tm, LANE), row), full(w_main), full(w_lr)],
        out_specs=[pl.BlockSpec((tm, d), row), pl.BlockSpec((tm, offs[-1]), row)],
        out_shape=[jax.ShapeDtypeStruct((t, d), F32), jax.ShapeDtypeStruct((t, offs[-1]), BF16)],
        compiler_params=_seq(1),
    )(dh_part, *pieces, dzg, w_main, w_lr)


def _pool_cnt(tile, tm, w):
    t = tile * tm + lax.broadcasted_iota(jnp.int32, (tm, 1), 0)
    return jnp.minimum(t + 1, w).astype(F32)


def _pool_fwd(u, wp, scale, name):
    t, pw = u.shape
    tm = _row_tile(t)
    gd = wp.shape[1]

    def body(u_ref, wp_ref, sc_ref, y_ref, p_ref, ext):
        i = pl.program_id(0)

        @pl.when(i == 0)
        def _():
            ext[0:POOL_HALO, :] = jnp.zeros((POOL_HALO, pw), F32)

        ext[POOL_HALO:POOL_HALO + tm, :] = u_ref[...]
        for gi, w in enumerate(POOL_WINDOWS):
            cols = slice(gi * gd, (gi + 1) * gd)
            s = ext[pl.ds(POOL_HALO, tm), cols]
            tot = s
            for back in range(1, w):
                tot = tot + ext[pl.ds(POOL_HALO - back, tm), cols]
            p = (tot / _pool_cnt(i, tm, w) - s).astype(BF16)
            p_ref[:, cols] = p
            y_ref[:, cols] = (jnp.dot(p, wp_ref[gi], preferred_element_type=F32) * sc_ref[:, cols]).astype(BF16)
        ext[0:POOL_HALO, :] = ext[tm:tm + POOL_HALO, :]

    row = lambda i: (i, 0)
    return _tc_call(
        body, name=name, grid=(t // tm,),
        in_specs=[pl.BlockSpec((tm, pw), row), pl.BlockSpec(wp.shape, lambda i: (0, 0, 0)),
                  pl.BlockSpec((1, pw), lambda i: (0, 0))],
        out_specs=[pl.BlockSpec((tm, pw), row), pl.BlockSpec((tm, pw), row)],
        out_shape=[jax.ShapeDtypeStruct((t, pw), BF16), jax.ShapeDtypeStruct((t, pw), BF16)],
        scratch_shapes=[pltpu.VMEM((tm + POOL_HALO, pw), F32)],
        compiler_params=_seq(1),
    )(u, wp, scale)


def _pool_bwd(dy, pb, wp, scale, name):
    t, pw = dy.shape
    tm = _row_tile(t)
    nt = t // tm
    gd = wp.shape[1]

    def body(dy_ref, p_ref, wp_ref, sc_ref, du_ref, dwp_ref, dsc_ref, ext):
        i = pl.program_id(0)
        tile = nt - 1 - i

        @pl.when(i == 0)
        def _():
            ext[tm:tm + POOL_HALO, :] = jnp.zeros((POOL_HALO, pw), F32)
            dwp_ref[...] = jnp.zeros_like(dwp_ref)
            dsc_ref[...] = jnp.zeros_like(dsc_ref)

        dps = []
        for gi, w in enumerate(POOL_WINDOWS):
            cols = slice(gi * gd, (gi + 1) * gd)
            dyv = dy_ref[:, cols]
            p = p_ref[:, cols]
            dpre = (dyv * sc_ref[:, cols]).astype(BF16)
            dsc_ref[:, cols] += jnp.sum(dyv * jnp.dot(p, wp_ref[gi], preferred_element_type=F32), axis=0, keepdims=True)
            dwp_ref[gi] += _mm_tn(p, dpre)
            dp = _mm_nt(dpre, wp_ref[gi])
            dps.append(dp)
            ext[0:tm, cols] = dp / _pool_cnt(tile, tm, w)
        for gi, w in enumerate(POOL_WINDOWS):
            cols = slice(gi * gd, (gi + 1) * gd)
            tot = ext[pl.ds(0, tm), cols]
            for fwd in range(1, w):
                tot = tot + ext[pl.ds(fwd, tm), cols]
            du_ref[:, cols] = (tot - dps[gi]).astype(BF16)
        ext[tm:tm + POOL_HALO, :] = ext[0:POOL_HALO, :]

    row = lambda i: (nt - 1 - i, 0)
    return _tc_call(
        body, name=name, grid=(nt,),
        in_specs=[pl.BlockSpec((tm, pw), row), pl.BlockSpec((tm, pw), row),
                  pl.BlockSpec(wp.shape, lambda i: (0, 0, 0)), pl.BlockSpec((1, pw), lambda i: (0, 0))],
        out_specs=[pl.BlockSpec((tm, pw), row), pl.BlockSpec(wp.shape, lambda i: (0, 0, 0)),
                   pl.BlockSpec((1, pw), lambda i: (0, 0))],
        out_shape=[jax.ShapeDtypeStruct((t, pw), BF16), jax.ShapeDtypeStruct(wp.shape, F32),
                   jax.ShapeDtypeStruct((1, pw), F32)],
        scratch_shapes=[pltpu.VMEM((tm + POOL_HALO, pw), F32)],
        compiler_params=_seq(1),
    )(dy, pb, wp, scale)


def _gla_masks(kw, vw):
    dk, dv = kw // N_HEADS, vw // N_HEADS
    lane_k = lax.broadcasted_iota(jnp.int32, (1, kw), 1)
    lane_v = lax.broadcasted_iota(jnp.int32, (1, vw), 1)
    hk = [((lane_k >= h * dk) & (lane_k < (h + 1) * dk)).astype(F32) for h in range(N_HEADS)]
    hv = [((lane_v >= h * dv) & (lane_v < (h + 1) * dv)).astype(F32) for h in range(N_HEADS)]
    r = lax.broadcasted_iota(jnp.int32, (CHUNK, CHUNK), 0)
    c = lax.broadcasted_iota(jnp.int32, (CHUNK, CHUNK), 1)
    tril = r >= c
    rs = lax.broadcasted_iota(jnp.int32, (N_HEADS * CHUNK, CHUNK), 0) & (CHUNK - 1)
    stril = rs >= lax.broadcasted_iota(jnp.int32, (N_HEADS * CHUNK, CHUNK), 1)
    return hk, hv, tril, stril


def _block_diag(x, hk, dv):
    return jnp.concatenate([x[h * dv:(h + 1) * dv, :] * hk[h] for h in range(N_HEADS)], axis=0)


def _gla_fwd(q, k, v, loga, r, gnorm, name):
    t, kw = q.shape
    vw = v.shape[1]
    dk, dv = kw // N_HEADS, vw // N_HEADS
    tm = _row_tile(t)
    nc = tm // CHUNK
    qscale = dk ** -0.5

    def body(q_ref, k_ref, v_ref, la_ref, r_ref, gn_ref, o_ref, y_ref, sall_ref, st):
        @pl.when(pl.program_id(0) == 0)
        def _():
            st[...] = jnp.zeros_like(st)

        hk, hv, tril, stril = _gla_masks(kw, vw)
        trif = tril.astype(F32)

        def chunk(c, carry):
            rows = pl.ds(pl.multiple_of(c * CHUNK, CHUNK), CHUNK)
            la = la_ref[rows, :]
            b = _mm_01(trif, la)
            bl = jnp.sum(la, axis=0, keepdims=True)
            qb = q_ref[rows, :] * (qscale * jnp.exp(b))
            kk = k_ref[rows, :]
            kb = kk * jnp.exp(-b)
            kl = kk * jnp.exp(bl - b)
            vv = v_ref[rows, :]
            s_t = st[...]
            compact = s_t[0:dv, :]
            for h in range(1, N_HEADS):
                compact = compact + s_t[h * dv:(h + 1) * dv, :]
            sall_ref[c] = compact
            qx = jnp.concatenate([qb.astype(BF16) * hk[h].astype(BF16) for h in range(N_HEADS)], axis=0)
            a = jnp.where(stril, _mm_nt(qx, kb), 0.0).astype(BF16)
            o_inter = _mm_nt(qb, s_t)
            for h in range(N_HEADS):
                vs = slice(h * dv, (h + 1) * dv)
                o_ref[rows, vs] = o_inter[:, vs] + _mm(a[h * CHUNK:(h + 1) * CHUNK, :], vv[:, vs])
            st[...] = s_t * jnp.exp(bl) + _block_diag(_mm_tn(vv, kl), hk, dv)
            return carry

        lax.fori_loop(0, nc, chunk, 0, unroll=CHUNK_UNROLL)
        for h in range(N_HEADS):
            vs = slice(h * dv, (h + 1) * dv)
            oh = o_ref[:, vs]
            on = oh * lax.rsqrt(jnp.mean(oh * oh, axis=-1, keepdims=True) + RMS_EPS)
            _, sl = _silu_parts(r_ref[:, vs])
            y_ref[:, vs] = (on * gn_ref[:, vs] * sl).astype(BF16)

    row = lambda i: (i, 0)
    return _tc_call(
        body, name=name, grid=(t // tm,),
        in_specs=[pl.BlockSpec((tm, kw), row), pl.BlockSpec((tm, kw), row), pl.BlockSpec((tm, vw), row),
                  pl.BlockSpec((tm, kw), row), pl.BlockSpec((tm, vw), row), pl.BlockSpec((1, vw), lambda i: (0, 0))],
        out_specs=[pl.BlockSpec((tm, vw), row), pl.BlockSpec((tm, vw), row),
                   pl.BlockSpec((nc, dv, kw), lambda i: (i, 0, 0))],
        out_shape=[jax.ShapeDtypeStruct((t, vw), F32), jax.ShapeDtypeStruct((t, vw), BF16),
                   jax.ShapeDtypeStruct((t // CHUNK, dv, kw), F32)],
        scratch_shapes=[pltpu.VMEM((vw, kw), F32)],
        compiler_params=_seq(1),
    )(q, k, v, loga, r, gnorm)


def _gla_bwd(dy, o, r, gnorm, q, k, v, loga, zg, sall, wgu, name):
    t, kw = q.shape
    vw = v.shape[1]
    dk, dv = kw // N_HEADS, vw // N_HEADS
    tm = _row_tile(t)
    nt = t // tm
    nc = tm // CHUNK
    qscale = dk ** -0.5

    def body(dy_ref, o_ref, r_ref, gn_ref, q_ref, k_ref, v_ref, la_ref, zg_ref, sall_ref, wgu_ref,
             dq_ref, dk_ref, dv_ref, dr_ref, dzg_ref, dwgu_ref, dbg_ref, dgn_ref, dst, do_s):
        @pl.when(pl.program_id(0) == 0)
        def _():
            dst[...] = jnp.zeros_like(dst)
            dwgu_ref[...] = jnp.zeros_like(dwgu_ref)
            dbg_ref[...] = jnp.zeros_like(dbg_ref)
            dgn_ref[...] = jnp.zeros_like(dgn_ref)

        for h in range(N_HEADS):
            vs = slice(h * dv, (h + 1) * dv)
            oh = o_ref[:, vs]
            rinv = lax.rsqrt(jnp.mean(oh * oh, axis=-1, keepdims=True) + RMS_EPS)
            on = oh * rinv
            rr = r_ref[:, vs]
            sg, sl = _silu_parts(rr)
            dyv = dy_ref[:, vs]
            gn = gn_ref[:, vs]
            dgn_ref[:, vs] += jnp.sum(dyv * on * sl, axis=0, keepdims=True)
            dr_ref[:, vs] = (dyv * on * gn * (sg + sl * (1.0 - sg))).astype(BF16)
            don = dyv * gn * sl
            do_s[:, vs] = rinv * (don - on * jnp.mean(don * on, axis=-1, keepdims=True))

        hk, hv, tril, stril = _gla_masks(kw, vw)
        trif = tril.astype(F32)
        triuf = (lax.broadcasted_iota(jnp.int32, (CHUNK, CHUNK), 0)
                 <= lax.broadcasted_iota(jnp.int32, (CHUNK, CHUNK), 1)).astype(F32)
        last_row = lax.broadcasted_iota(jnp.int32, (CHUNK, 1), 0) == CHUNK - 1

        def chunk(idx, carry):
            c = nc - 1 - idx
            rows = pl.ds(pl.multiple_of(c * CHUNK, CHUNK), CHUNK)
            la = la_ref[rows, :]
            b = _mm_01(trif, la)
            bl = jnp.sum(la, axis=0, keepdims=True)
            eb = jnp.exp(b)
            enb = jnp.exp(-b)
            ebl = jnp.exp(bl - b)
            el = jnp.exp(bl)
            qb = q_ref[rows, :] * (qscale * eb)
            kk = k_ref[rows, :]
            kb = kk * enb
            kl = kk * ebl
            vv = v_ref[rows, :]
            do = do_s[rows, :]
            compact = sall_ref[c]
            s_t = jnp.concatenate([compact * hk[h] for h in range(N_HEADS)], axis=0)
            ds_t = dst[...]
            qx = jnp.concatenate([qb.astype(BF16) * hk[h].astype(BF16) for h in range(N_HEADS)], axis=0)
            dox = jnp.concatenate([do.astype(BF16) * hv[h].astype(BF16) for h in range(N_HEADS)], axis=0)
            a = jnp.where(stril, _mm_nt(qx, kb), 0.0).astype(BF16)
            da = jnp.where(stril, _mm_nt(dox, vv), 0.0).astype(BF16)
            dv_ref[rows, :] = (_mm_tn(a, dox) + _mm_nt(kl, ds_t)).astype(BF16)
            dak = _mm(da, kb)
            dqb = _mm(do, s_t)
            for h in range(N_HEADS):
                dqb = dqb + dak[h * CHUNK:(h + 1) * CHUNK, :] * hk[h]
            dkb = _mm_tn(da, qx)
            dkl = _mm(vv, ds_t)
            dbl = jnp.sum(dkl * kl, axis=0, keepdims=True) + el * jnp.sum(ds_t * s_t, axis=0, keepdims=True)
            dst[...] = ds_t * el + _block_diag(_mm_tn(do, qb), hk, dv)
            dq_ref[rows, :] = (dqb * (qscale * eb)).astype(BF16)
            dk_ref[rows, :] = (dkb * enb + dkl * ebl).astype(BF16)
            db = dqb * qb - dkb * kb - dkl * kl + jnp.where(last_row, dbl, 0.0)
            dla = _mm_01(triuf, db)
            dlogit = dla * (1.0 / GLA_GATE_TEMP) * (1.0 - jnp.exp(GLA_GATE_TEMP * la))
            dzg_ref[rows, :] = _mm_nt(dlogit, wgu_ref[...]).astype(BF16)
            dwgu_ref[...] += _mm_tn(zg_ref[rows, :], dlogit)
            dbg_ref[...] += jnp.sum(dlogit, axis=0, keepdims=True)
            return carry

        lax.fori_loop(0, nc, chunk, 0, unroll=CHUNK_UNROLL)

    row = lambda i: (nt - 1 - i, 0)
    const = lambda i: (0, 0)
    return _tc_call(
        body, name=name, grid=(nt,),
        in_specs=[pl.BlockSpec((tm, vw), row), pl.BlockSpec((tm, vw), row), pl.BlockSpec((tm, vw), row),
                  pl.BlockSpec((1, vw), const), pl.BlockSpec((tm, kw), row), pl.BlockSpec((tm, kw), row),
                  pl.BlockSpec((tm, vw), row), pl.BlockSpec((tm, kw), row), pl.BlockSpec((tm, LANE), row),
                  pl.BlockSpec((nc, dv, kw), lambda i: (nt - 1 - i, 0, 0)), pl.BlockSpec((LANE, kw), const)],
        out_specs=[pl.BlockSpec((tm, kw), row), pl.BlockSpec((tm, kw), row), pl.BlockSpec((tm, vw), row),
                   pl.BlockSpec((tm, vw), row), pl.BlockSpec((tm, LANE), row), pl.BlockSpec((LANE, kw), const),
                   pl.BlockSpec((1, kw), const), pl.BlockSpec((1, vw), const)],
        out_shape=[jax.ShapeDtypeStruct((t, kw), BF16), jax.ShapeDtypeStruct((t, kw), BF16),
                   jax.ShapeDtypeStruct((t, vw), BF16), jax.ShapeDtypeStruct((t, vw), BF16),
                   jax.ShapeDtypeStruct((t, LANE), BF16), jax.ShapeDtypeStruct((LANE, kw), F32),
                   jax.ShapeDtypeStruct((1, kw), F32), jax.ShapeDtypeStruct((1, vw), F32)],
        scratch_shapes=[pltpu.VMEM((vw, kw), F32), pltpu.VMEM((tm, vw), F32)],
        compiler_params=_seq(1),
    )(dy, o, r, gnorm, q, k, v, loga, zg, sall, wgu)


def _outproj_fwd(yp, yg, w_out, xhat, gam, bet, alpha, name):
    t, d = xhat.shape
    pw = yp.shape[1]
    tm = _row_tile(t)

    def body(yp_ref, yg_ref, w_ref, x_ref, g_ref, b_ref, xhat_ref, rstd_ref):
        for part in range(ROW_GROUPS):
            rows = pl.ds(part * (tm // ROW_GROUPS), tm // ROW_GROUPS)
            h = x_ref[rows, :] * g_ref[...] + b_ref[...]
            y = (jnp.dot(yp_ref[rows, :], w_ref[0:pw, :], preferred_element_type=F32)
                 + jnp.dot(yg_ref[rows, :], w_ref[pw:, :], preferred_element_type=F32))
            xh, rs = _ln_stats(alpha * h + y)
            xhat_ref[rows, :] = xh
            rstd_ref[rows, :] = rs

    row = lambda i: (i, 0)
    vec = pl.BlockSpec((1, d), lambda i: (0, 0))
    return _tc_call(
        body, name=name, grid=(t // tm,),
        in_specs=[pl.BlockSpec((tm, pw), row), pl.BlockSpec((tm, yg.shape[1]), row),
                  pl.BlockSpec(w_out.shape, lambda i: (0, 0)), pl.BlockSpec((tm, d), row), vec, vec],
        out_specs=[pl.BlockSpec((tm, d), row), pl.BlockSpec((tm, 1), row)],
        out_shape=[jax.ShapeDtypeStruct((t, d), F32), jax.ShapeDtypeStruct((t, 1), F32)],
        compiler_params=_seq(1),
    )(yp, yg, w_out, xhat, gam, bet)


def _outproj_bwd(dh, xhat, rstd, ln_g, w_out, pw, alpha, name):
    t, d = dh.shape
    tm = _row_tile(t)
    gw = w_out.shape[0] - pw

    def body(dh_ref, xh_ref, rs_ref, g_ref, w_ref, dyb_ref, dyp_ref, dyg_ref, dres_ref, dgam_ref, dbet_ref):
        @pl.when(pl.program_id(0) == 0)
        def _():
            dgam_ref[...] = jnp.zeros_like(dgam_ref)
            dbet_ref[...] = jnp.zeros_like(dbet_ref)

        for part in range(ROW_GROUPS):
            rows = pl.ds(part * (tm // ROW_GROUPS), tm // ROW_GROUPS)
            dy = dh_ref[rows, :]
            xh = xh_ref[rows, :]
            dr = _ln_bwd(dy, xh, rs_ref[rows, :], g_ref[...])
            dgam_ref[...] += jnp.sum(dy * xh, axis=0, keepdims=True)
            dbet_ref[...] += jnp.sum(dy, axis=0, keepdims=True)
            drb = dr.astype(BF16)
            dyb_ref[rows, :] = drb
            dres_ref[rows, :] = alpha * dr
            dyp_ref[rows, :] = _mm_nt(drb, w_ref[0:pw, :])
            dyg_ref[rows, :] = _mm_nt(drb, w_ref[pw:, :])

    row = lambda i: (i, 0)
    vec = pl.BlockSpec((1, d), lambda i: (0, 0))
    return _tc_call(
        body, name=name, grid=(t // tm,),
        in_specs=[pl.BlockSpec((tm, d), row), pl.BlockSpec((tm, d), row), pl.BlockSpec((tm, 1), row), vec,
                  pl.BlockSpec(w_out.shape, lambda i: (0, 0))],
        out_specs=[pl.BlockSpec((tm, d), row), pl.BlockSpec((tm, pw), row), pl.BlockSpec((tm, gw), row),
                   pl.BlockSpec((tm, d), row), vec, vec],
        out_shape=[jax.ShapeDtypeStruct((t, d), BF16), jax.ShapeDtypeStruct((t, pw), F32),
                   jax.ShapeDtypeStruct((t, gw), F32), jax.ShapeDtypeStruct((t, d), F32),
                   jax.ShapeDtypeStruct((1, d), F32), jax.ShapeDtypeStruct((1, d), F32)],
        compiler_params=_seq(1),
    )(dh, xhat, rstd, ln_g, w_out)


def _rows_block(r, c):
    best = r
    for cand in range(BF16_ROWS, r, BF16_ROWS):
        if r % cand == 0 and cand * c * 4 <= (1 << 20):
            best = cand
    return best if best * c * 4 <= (4 << 20) else r


def _sum_slots(me, mine, recvs, name, layers_side_by_side=False):
    nl = len(recvs)
    ns, r, c = recvs[0].shape
    tr = _rows_block(r, c)

    def body(me_ref, *refs):
        o_ref = refs[nl * ns]
        for l in range(nl):
            acc = refs[l * ns][...].astype(F32)
            for s in range(1, ns):
                acc = acc + refs[l * ns + s][...].astype(F32)
            if layers_side_by_side:
                o_ref[0, :, l * c:(l + 1) * c] = acc.astype(o_ref.dtype)
            else:
                o_ref[l] = acc.astype(o_ref.dtype)

    def slot(s):
        return pl.BlockSpec((None, tr, c), lambda i, me_ref: ((me_ref[0] + s) % ns, i, 0))

    out = (1, r, nl * c) if layers_side_by_side else (nl, r, c)
    operands = []
    for l in range(nl):
        operands += [mine[l]] + [recvs[l]] * (ns - 1)
    return _tc_call(
        body, name=name,
        grid_spec=pltpu.PrefetchScalarGridSpec(
            num_scalar_prefetch=1, grid=(r // tr,),
            in_specs=[slot(s) for s in range(ns)] * nl,
            out_specs=pl.BlockSpec((out[0], tr, out[2]), lambda i, me_ref: (0, i, 0))),
        out_shape=jax.ShapeDtypeStruct(out, recvs[0].dtype),
        compiler_params=_seq(1),
    )(me, *operands)


def _adamw(w, terms, m, v, name):
    nl, r, c = w.shape
    tc = c
    while tc % (2 * LANE) == 0 and tc > 4 * LANE:
        tc //= 2
    tr = _rows_block(r, tc)
    nterm = len(terms)

    def body(*refs):
        w_ref = refs[0]
        t_refs = refs[1:1 + nterm]
        m_ref, v_ref, g_ref, d_ref, nm_ref, nv_ref = refs[1 + nterm:]
        g = t_refs[0][...].astype(F32)
        for tr_ in t_refs[1:]:
            g = g + tr_[...].astype(F32)
        nm = ADAM_B1 * m_ref[...] + (1.0 - ADAM_B1) * g
        nv = ADAM_B2 * v_ref[...] + (1.0 - ADAM_B2) * jnp.square(g)
        m_hat = nm / (1.0 - ADAM_B1 ** ADAM_STEP)
        v_hat = nv / (1.0 - ADAM_B2 ** ADAM_STEP)
        g_ref[...] = g
        d_ref[...] = -ADAM_LR * (m_hat / (jnp.sqrt(v_hat) + ADAM_EPS) + ADAM_WD * w_ref[...])
        nm_ref[...] = nm
        nv_ref[...] = nv

    spec = pl.BlockSpec((None, tr, tc), lambda l, i, j: (l, i, j))
    shp = jax.ShapeDtypeStruct((nl, r, c), F32)
    return _tc_call(
        body, name=name, grid=(nl, r // tr, c // tc),
        in_specs=[spec] * (3 + nterm), out_specs=[spec] * 4, out_shape=[shp] * 4,
        compiler_params=_seq(3),
    )(w, *terms, m, v)


XY_RELATIONS = ((1, 0, 0), (0, 1, 0), (1, 1, 0))
ALL_RELATIONS = tuple((fx, fy, fc) for fx in (0, 1) for fy in (0, 1) for fc in (0, 1) if fx or fy or fc)
SIBLING = ((0, 0, 1),)
HBM_SPEC = pl.BlockSpec(memory_space=pltpu.HBM)
SEM_SPEC = pl.BlockSpec(memory_space=pltpu.SEMAPHORE)
DATAFLOW = pltpu.SideEffectType.DATAFLOW_SIDE_EFFECTING


def _split_call(body, **kw):
    return pl.pallas_call(body, **kw)


def _flip(v, f):
    return 1 - v if f else v


def _any_spec(n):
    return [pl.BlockSpec(memory_space=pl.ANY)] * n


def _relations(kind):
    return {"bcast": ALL_RELATIONS, "swap": SIBLING}.get(kind, XY_RELATIONS)


def _copies(kind, arr, land, sems):
    x, y, c = lax.axis_index("x"), lax.axis_index("y"), lax.axis_index("c")
    out = []
    for (fx, fy, fc), (send_sem, recv_sem) in zip(_relations(kind), sems):
        px, py, pc = _flip(x, fx), _flip(y, fy), _flip(c, fc)
        if kind == "bcast":
            mine, theirs = 4 * x + 2 * y + c, 4 * px + 2 * py + pc
        else:
            mine, theirs = 2 * x + y, 2 * px + py
        src, to_mine, to_theirs = arr, land.at[mine], land.at[theirs]
        if kind == "swap":
            to_mine = to_theirs = land
        if kind == "scatter":
            src = arr.at[theirs]
        if kind == "gather_half":
            rows = _my_half(arr.shape[0], c)
            src, to_mine, to_theirs = arr.at[rows], land.at[mine, rows], land.at[theirs, rows]
        both = dict(src_ref=src, send_sem=send_sem, recv_sem=recv_sem, device_id=(px, py, pc), device_id_type=MESH)
        out.append((pltpu.make_async_remote_copy(dst_ref=to_mine, **both),
                    pltpu.make_async_remote_copy(dst_ref=to_theirs, **both)))
    return out


def _my_half(nrows, c):
    return pl.ds(c * (nrows // 2), nrows // 2)


def _share_halves(name, kinds, lands):
    ks = [k for k, kd in enumerate(kinds) if kd == "gather_half"]
    n = len(ks)

    def body(*refs):
        l_refs = refs[n:2 * n]
        send_sems, recv_sems = refs[2 * n:]
        x, y, c = lax.axis_index("x"), lax.axis_index("y"), lax.axis_index("c")
        copies = []
        for i in range(n):
            nrows = l_refs[i].shape[1]
            for r, (fx, fy, _) in enumerate(XY_RELATIONS):
                slot = 2 * _flip(x, fx) + _flip(y, fy)
                both = dict(src_ref=l_refs[i].at[slot, _my_half(nrows, c)], send_sem=send_sems.at[i, r],
                            recv_sem=recv_sems.at[i, r], device_id=(x, y, 1 - c), device_id_type=MESH)
                copies.append((pltpu.make_async_remote_copy(dst_ref=l_refs[i].at[slot, _my_half(nrows, c)], **both),
                               pltpu.make_async_remote_copy(dst_ref=l_refs[i].at[slot, _my_half(nrows, 1 - c)], **both)))
        for send, _ in copies:
            send.start()
        for _, arrival in copies:
            arrival.wait_recv()
        for send, _ in copies:
            send.wait_send()

    outs = _comm_call(
        body, name=name,
        in_specs=_any_spec(n), out_specs=_any_spec(n),
        out_shape=[jax.ShapeDtypeStruct(lands[k].shape, lands[k].dtype) for k in ks],
        input_output_aliases={i: i for i in range(n)},
        scratch_shapes=[pltpu.SemaphoreType.DMA((n, 3)), pltpu.SemaphoreType.DMA((n, 3))],
    )(*[lands[k] for k in ks])
    lands = list(lands)
    for k, o in zip(ks, outs):
        lands[k] = o
    return lands


def _sem_pairs(kinds, sems):
    out, at = [], 0
    for kind in kinds:
        nrel = len(_relations(kind))
        out.append([(sems[at + 2 * r], sems[at + 2 * r + 1]) for r in range(nrel)])
        at += 2 * nrel
    return out


def _part_of(st, ks):
    at = [0]
    for kind in st["kinds"]:
        at.append(at[-1] + 2 * len(_relations(kind)))
    return dict(kinds=[st["kinds"][k] for k in ks], arrs=[st["arrs"][k] for k in ks],
                lands=[st["lands"][k] for k in ks], sems=[s for k in ks for s in st["sems"][at[k]:at[k + 1]]])


def _exchange_start(name, kinds, arrs, lands):
    n = len(arrs)
    nsem = sum(2 * len(_relations(kd)) for kd in kinds)

    def body(*refs):
        a_refs, l_refs = refs[:n], refs[n:2 * n]
        pairs = _sem_pairs(kinds, refs[2 * n:2 * n + nsem])
        token = refs[-1]
        for k in range(n):
            for send, _ in _copies(kinds[k], a_refs[k], l_refs[k], pairs[k]):
                send.start()
        token[...] = jnp.zeros_like(token)

    thru = [pltpu.HBM(a.shape, a.dtype) for a in list(arrs) + list(lands)]
    outs = _split_call(
        body, name=name,
        out_shape=(*[pltpu.SemaphoreType.DMA(())] * nsem, *thru, jax.ShapeDtypeStruct((8, LANE), F32)),
        in_specs=[HBM_SPEC] * (2 * n),
        out_specs=(*[SEM_SPEC] * nsem, *[HBM_SPEC] * (2 * n), pl.BlockSpec(memory_space=pltpu.VMEM)),
        input_output_aliases={i: nsem + i for i in range(2 * n)},
        compiler_params=pltpu.CompilerParams(has_side_effects=DATAFLOW),
    )(*[pltpu.with_memory_space_constraint(a, pltpu.HBM) for a in list(arrs) + list(lands)])
    return dict(kinds=kinds, sems=outs[:nsem], arrs=outs[nsem:nsem + n], lands=outs[nsem + n:nsem + 2 * n],
                token=outs[-1])


def _exchange_wait(name, st, after):
    kinds = st["kinds"]
    n = len(kinds)
    nsem = len(st["sems"])

    def body(*refs):
        a_refs, l_refs = refs[:n], refs[n:2 * n]
        pairs = _sem_pairs(kinds, refs[2 * n:2 * n + nsem])
        for k in range(n):
            for _, arrival in _copies(kinds[k], a_refs[k], l_refs[k], pairs[k]):
                arrival.wait_send()
                arrival.wait_recv()
        refs[-1][...] = jnp.zeros_like(refs[-1])

    ins = list(st["arrs"]) + list(st["lands"])
    outs = _split_call(
        body, name=name,
        out_shape=[pltpu.HBM(a.shape, a.dtype) for a in ins] + [jax.ShapeDtypeStruct((8, LANE), F32)],
        in_specs=[HBM_SPEC] * (2 * n) + [SEM_SPEC] * nsem + [pl.BlockSpec(memory_space=pl.ANY)],
        out_specs=[HBM_SPEC] * (2 * n) + [pl.BlockSpec(memory_space=pltpu.VMEM)],
        input_output_aliases={i: i for i in range(2 * n)},
        compiler_params=pltpu.CompilerParams(has_side_effects=DATAFLOW),
    )(*ins, *st["sems"], after)
    return outs[:n], outs[n:2 * n], outs[-1]


def _landing(own, slot, nslot):
    return lax.dynamic_update_slice(lax.empty((nslot,) + own.shape, own.dtype), own[None], (slot,) + (0,) * own.ndim)


def _col_shards(a, n=N_SHARD):
    r, c = a.shape
    return a.reshape(r, n, c // n).transpose(1, 0, 2)


def _from_col_shards(a):
    n, r, cs = a.shape
    return a.transpose(1, 0, 2).reshape(r, n * cs)


def kernel(x, meta_tokens, ffn1_w_gate, ffn1_w_up, ffn1_w_down, ln1_g, ln1_b, w_in, w_gate_up, b_gate, w_pool, pool_scale, gla_norm_g, w_out, ln2_g, ln2_b, ffn2_w_gate, ffn2_w_up, ffn2_w_down, ln3_g, ln3_b, loss_target, m_meta_tokens, m_ffn1_w_gate, m_ffn1_w_up, m_ffn1_w_down, m_ln1_g, m_ln1_b, m_w_in, m_w_gate_up, m_b_gate, m_w_pool, m_pool_scale, m_gla_norm_g, m_w_out, m_ln2_g, m_ln2_b, m_ffn2_w_gate, m_ffn2_w_up, m_ffn2_w_down, m_ln3_g, m_ln3_b, v_meta_tokens, v_ffn1_w_gate, v_ffn1_w_up, v_ffn1_w_down, v_ln1_g, v_ln1_b, v_w_in, v_w_gate_up, v_b_gate, v_w_pool, v_pool_scale, v_gla_norm_g, v_w_out, v_ln2_g, v_ln2_b, v_ffn2_w_gate, v_ffn2_w_up, v_ffn2_w_down, v_ln3_g, v_ln3_b):
    w = dict(meta_tokens=meta_tokens, ffn1_w_gate=ffn1_w_gate, ffn1_w_up=ffn1_w_up, ffn1_w_down=ffn1_w_down,
             ln1_g=ln1_g, ln1_b=ln1_b, w_in=w_in, w_gate_up=w_gate_up, b_gate=b_gate, w_pool=w_pool,
             pool_scale=pool_scale, gla_norm_g=gla_norm_g, w_out=w_out, ln2_g=ln2_g, ln2_b=ln2_b,
             ffn2_w_gate=ffn2_w_gate, ffn2_w_up=ffn2_w_up, ffn2_w_down=ffn2_w_down, ln3_g=ln3_g, ln3_b=ln3_b)
    mom1 = dict(meta_tokens=m_meta_tokens, ffn1_w_gate=m_ffn1_w_gate, ffn1_w_up=m_ffn1_w_up,
                ffn1_w_down=m_ffn1_w_down, ln1_g=m_ln1_g, ln1_b=m_ln1_b, w_in=m_w_in, w_gate_up=m_w_gate_up,
                b_gate=m_b_gate, w_pool=m_w_pool, pool_scale=m_pool_scale, gla_norm_g=m_gla_norm_g, w_out=m_w_out,
                ln2_g=m_ln2_g, ln2_b=m_ln2_b, ffn2_w_gate=m_ffn2_w_gate, ffn2_w_up=m_ffn2_w_up,
                ffn2_w_down=m_ffn2_w_down, ln3_g=m_ln3_g, ln3_b=m_ln3_b)
    mom2 = dict(meta_tokens=v_meta_tokens, ffn1_w_gate=v_ffn1_w_gate, ffn1_w_up=v_ffn1_w_up,
                ffn1_w_down=v_ffn1_w_down, ln1_g=v_ln1_g, ln1_b=v_ln1_b, w_in=v_w_in, w_gate_up=v_w_gate_up,
                b_gate=v_b_gate, w_pool=v_w_pool, pool_scale=v_pool_scale, gla_norm_g=v_gla_norm_g, w_out=v_w_out,
                ln2_g=v_ln2_g, ln2_b=v_ln2_b, ffn2_w_gate=v_ffn2_w_gate, ffn2_w_up=v_ffn2_w_up,
                ffn2_w_down=v_ffn2_w_down, ln3_g=v_ln3_g, ln3_b=v_ln3_b)

    xs = x[0]
    s_len, d = xs.shape
    nl = ln1_g.shape[0]
    alpha = (2.0 * nl) ** 0.25
    t_real = N_META + s_len
    t_pad = -(-t_real // LANE) * LANE
    pw = pool_scale.shape[1]
    kw = b_gate.shape[1]
    vw = gla_norm_g.shape[1]
    rank = w_gate_up.shape[1]
    widths = (pw, kw, kw, vw, vw)
    n_main = sum(widths)
    dff_s = ffn1_w_gate.shape[2]
    dff_c = N_SHARD * dff_s // FFN_CHUNKS

    me_xy = 2 * lax.axis_index("x") + lax.axis_index("y")
    me_all = 2 * me_xy + lax.axis_index("c")
    ffn1_names = ("ffn1_w_gate", "ffn1_w_up", "ffn1_w_down")
    mix_names = ("w_out", "w_gate_up", "w_in")
    ffn2_names = ("ffn2_w_gate", "ffn2_w_up", "ffn2_w_down")

    gate_up = ("ffn1_w_gate", "ffn1_w_up", "ffn2_w_gate", "ffn2_w_up")

    def stored(n, a):
        if n in gate_up:
            return jnp.swapaxes(a, 1, 2)
        return jnp.transpose(a, (2, 0, 1)) if n == "w_in" else a

    def as_given(n, a):
        if n in gate_up:
            return jnp.swapaxes(a, 1, 2)
        if n == "w_in":
            return jnp.transpose(a.reshape(-1, nl, d), (1, 2, 0))
        return a.reshape(w[n].shape)

    stages = [[("meta_tokens", None)], [(n, 0) for n in ffn1_names], [(n, 0) for n in mix_names + ffn2_names]]
    stages += [[(n, l) for n in BIG] for l in range(1, nl)]
    gathers, wa = {}, {}

    halved = ffn1_names + ffn2_names + ("w_out",)

    def start_gathers():
        own = []
        for n, l in [item for items in stages for item in items]:
            a = meta_tokens if l is None else (stored(n, w[n])[:, l] if n == "w_in" else stored(n, w[n])[l])
            own.append(a if l is None else a.astype(BF16))
        kinds = ["gather_half" if n in halved else "gather" for items in stages for n, _ in items]
        return _exchange_start("gather_start", kinds, own, [_landing(a, me_xy, N_SHARD) for a in own])

    def arrive(si, after):
        first = sum(len(items) for items in stages[:si])
        part = _part_of(gathers, range(first, first + len(stages[si])))
        _, lands, _ = _exchange_wait(f"gather_wait_{si}", part, after)
        if any(kd == "gather_half" for kd in part["kinds"]):
            lands = _share_halves(f"gather_share_{si}", part["kinds"], lands)
        for item, a in zip(stages[si], lands):
            wa[item] = a.reshape(FFN_CHUNKS, -1, d) if item[0] in ffn1_names + ffn2_names else a

    def mixer_weights(l):
        wi = wa["w_in", l].reshape(-1, d)
        return dict(w_main=wi[:n_main], w_lr=jnp.pad(wi[n_main:], ((0, LANE - rank), (0, 0))),
                    wgu=jnp.pad(_from_col_shards(wa["w_gate_up", l]), ((0, LANE - rank), (0, 0))),
                    wout=wa["w_out", l].reshape(-1, d))

    wp16 = w_pool.astype(BF16)
    ones = jnp.ones((1, d), F32)
    zeros = jnp.zeros((1, d), F32)
    target = jnp.concatenate([jnp.zeros((N_META, d), F32), loss_target[0], jnp.zeros((t_pad - t_real, d), F32)], axis=0)

    gathers = start_gathers()
    arrive(0, gathers["token"])
    meta_full = _from_col_shards(wa["meta_tokens", None])
    h0 = jnp.concatenate([meta_full, xs, jnp.zeros((t_pad - t_real, d), F32)], axis=0)
    arrive(1, h0[:8, :LANE] + target[:8, :LANE])

    saved, mw = [], []
    cur, cur_g, cur_b = h0, ones, zeros
    for l in range(nl):
        s = {}
        xh1, rs1, hb0, g1, u1 = _ffn_fwd(cur, cur_g, cur_b, wa["ffn1_w_gate", l], wa["ffn1_w_up", l],
                                         wa["ffn1_w_down", l], alpha, f"ffn1_fwd_{l}")
        if l == 0:
            arrive(2, xh1)
        mw.append(mixer_weights(l))
        up, q, k, v, r, zg, la, hb1 = _inproj_fwd(xh1, ln1_g[l:l + 1], ln1_b[l:l + 1], mw[l]["w_main"], mw[l]["w_lr"],
                                                  mw[l]["wgu"], b_gate[l:l + 1], widths, f"inproj_fwd_{l}")
        yp, pb = _pool_fwd(up, wp16[l], pool_scale[l:l + 1], f"pool_fwd_{l}")
        o, yg, sall = _gla_fwd(q, k, v, la, r, gla_norm_g[l:l + 1], f"gla_fwd_{l}")
        xh2, rs2 = _outproj_fwd(yp, yg, mw[l]["wout"], xh1, ln1_g[l:l + 1], ln1_b[l:l + 1], alpha, f"outproj_fwd_{l}")
        if l + 1 < nl:
            arrive(l + 3, xh2)
        head = (ln3_g[l:l + 1], ln3_b[l:l + 1], target, s_len) if l == nl - 1 else None
        xh3, rs3, hb2, g2, u2, *at_head = _ffn_fwd(xh2, ln2_g[l:l + 1], ln2_b[l:l + 1], wa["ffn2_w_gate", l],
                                                   wa["ffn2_w_up", l], wa["ffn2_w_down", l], alpha, f"ffn2_fwd_{l}",
                                                   head)
        s.update(xh1=xh1, rs1=rs1, hb0=hb0, g1=g1, u1=u1, q=q, k=k, v=v, r=r, zg=zg, la=la, hb1=hb1, yp=yp, pb=pb,
                 o=o, yg=yg, sall=sall, xh2=xh2, rs2=rs2, xh3=xh3, rs3=rs3, hb2=hb2, g2=g2, u2=u2)
        saved.append(s)
        cur, cur_g, cur_b = xh3, ln3_g[l:l + 1], ln3_b[l:l + 1]

    dh, loss_acc = at_head
    loss = lax.psum(loss_acc[0, 0], ("x", "y", "c"))

    small_grads = {n: [None] * nl for n in SMALL}
    scatters = []

    def depart(name, items, grads, kinds=None):
        lands = [_landing(g, me_all, N_DEV) if kd == "bcast" else lax.empty(g.shape, g.dtype)
                 for g, kd in zip(grads, kinds or ["scatter"] * len(grads))]
        st = _exchange_start(name, kinds or ["scatter"] * len(grads), grads, lands)
        scatters.append((name, items, st))
        return st["token"]

    def pack(parts):
        flat = jnp.concatenate([parts[n].reshape(-1) for n in SMALL])
        return flat.reshape(-1, LANE)

    def ffn_wgrad(n, l, hb, dgb, dub, act, dfb, after=None):
        if n.endswith("down"):
            dw = _wgrad(act, dfb, dff_c, d, f"{n}_grad_{l}", after)
        else:
            dw = _wgrad(dgb if n.endswith("gate") else dub, hb, dff_c, d, f"{n}_grad_{l}", after)
        return dw.reshape(N_SHARD, dff_s, d)

    late = []
    for l in reversed(range(nl)):
        s = saved[l]
        dh, dfb, dgb, dub, act, dgam, dbet = _ffn_bwd(dh, s["xh3"], s["rs3"], ln3_g[l:l + 1], s["g2"], s["u2"],
                                                      wa["ffn2_w_gate", l], wa["ffn2_w_up", l], wa["ffn2_w_down", l],
                                                      alpha, f"ffn2_bwd_{l}")
        small_grads["ln3_g"][l], small_grads["ln3_b"][l] = dgam, dbet
        gone = depart(f"scatter_start_ffn2_{l}", [(n, l) for n in ffn2_names],
                      [ffn_wgrad(n, l, s["hb2"], dgb, dub, act, dfb) for n in ffn2_names])

        dyb, dyp, dyg, dres, dgam, dbet = _outproj_bwd(dh, s["xh2"], s["rs2"], ln2_g[l:l + 1] + gone[0:1, 0:1],
                                                       mw[l]["wout"], pw, alpha, f"outproj_bwd_{l}")
        small_grads["ln2_g"][l], small_grads["ln2_b"][l] = dgam, dbet
        dwo = jnp.concatenate([_wgrad(s["yp"], dyb, pw, d, f"dwout_pool_{l}"),
                               _wgrad(s["yg"], dyb, vw, d, f"dwout_gla_{l}")], axis=0)
        dq, dk, dv, dr, dzg, dwgu, dbg, dgn = _gla_bwd(dyg, s["o"], s["r"], gla_norm_g[l:l + 1], s["q"], s["k"],
                                                       s["v"], s["la"], s["zg"], s["sall"], mw[l]["wgu"],
                                                       f"gla_bwd_{l}")
        dup, dwp, dsc = _pool_bwd(dyp, s["pb"], wp16[l], pool_scale[l:l + 1], f"pool_bwd_{l}")
        small_grads["b_gate"][l], small_grads["gla_norm_g"][l] = dbg, dgn
        small_grads["w_pool"][l], small_grads["pool_scale"][l] = dwp, dsc
        dh, dz = _inproj_bwd(dres, [dup, dq, dk, dv, dr], dzg, mw[l]["w_main"], mw[l]["w_lr"], f"inproj_bwd_{l}")
        dwi = jnp.concatenate([_wgrad(dz, s["hb1"], 4 * LANE, d, f"dwin_main_{l}"),
                               _wgrad(dzg, s["hb1"], LANE, d, f"dwin_lr_{l}")[:rank]], axis=0)
        gone = depart(f"scatter_start_mix_{l}", [(n, l) for n in mix_names],
                      [dwo.reshape(N_SHARD, -1, d), _col_shards(dwgu[:rank]), dwi.reshape(N_SHARD, -1, d)])

        dh, dfb, dgb, dub, act, dgam, dbet = _ffn_bwd(dh, s["xh1"], s["rs1"], ln1_g[l:l + 1] + gone[0:1, 0:1],
                                                      s["g1"], s["u1"], wa["ffn1_w_gate", l], wa["ffn1_w_up", l],
                                                      wa["ffn1_w_down", l], alpha, f"ffn1_bwd_{l}")
        small_grads["ln1_g"][l], small_grads["ln1_b"][l] = dgam, dbet
        if l:
            gone = depart(f"scatter_start_ffn1_{l}", [(n, l) for n in ffn1_names],
                          [ffn_wgrad(n, l, s["hb0"], dgb, dub, act, dfb) for n in ffn1_names])
            ln3_g = ln3_g.at[l - 1:l].add(gone[0:1, 0:1])
            continue
        grad_x = dh[N_META:t_real][None]
        small_vec = pack({n: jnp.stack(small_grads[n]) for n in SMALL})
        gone = depart("scatter_start_rest", [("meta_tokens", 0), ("small", 0)],
                      [_col_shards(dh[:N_META]), small_vec], ["scatter", "bcast"])
        for n in ffn1_names:
            g = ffn_wgrad(n, l, s["hb0"], dgb, dub, act, dfb, after=gone)
            gone = depart(f"scatter_start_{n}", [(n, l)], [g])
            late.append(scatters.pop())

    sent, recv, results, firsts = {}, {}, {}, []
    my_slot = me_xy.reshape(1).astype(jnp.int32)

    def collect(group, after):
        for name, items, st in group:
            arrs, lands, _ = _exchange_wait(name.replace("start", "wait"), st, after)
            for item, a, b in zip(items, arrs, lands):
                sent[item], recv[item] = a, b

    def reduce_and_update(names, tag):
        partial = []
        for n in names:
            layers = [(n, l) for l in range(1 if n == "meta_tokens" else nl)]
            partial.append(_sum_slots(my_slot, [sent[it] for it in layers], [recv[it] for it in layers],
                                      f"sum_{n}", n == "w_in"))
        swap = _exchange_start(f"swap_start_{tag}", ["swap"] * len(names), partial,
                               [lax.empty(p.shape, p.dtype) for p in partial])
        after = swap["token"]
        for k, n in enumerate(names):
            (mine,), (theirs,), _ = _exchange_wait(f"swap_wait_{n}", _part_of(swap, [k]), after)
            fit = lambda a: stored(n, a).reshape(mine.shape)
            outs = _adamw(fit(w[n]), [mine, theirs], fit(mom1[n]), fit(mom2[n]), f"adamw_{n}")
            results[n] = [as_given(n, o) for o in outs]
            firsts.append(outs[1][0, 0, 0])
            after = outs[1]

    collect(scatters, gone)
    early = [n for n in ("meta_tokens",) + BIG if n not in ffn1_names]
    reduce_and_update(early, "early")
    small_terms = [recv["small", 0][i][None] for i in range(N_DEV)]
    souts = _adamw(pack(w)[None], small_terms, pack(mom1)[None], pack(mom2)[None], "adamw_small")
    off = 0
    for n in SMALL:
        size = w[n].size
        results[n] = [o.reshape(-1)[off:off + size].reshape(w[n].shape) for o in souts]
        off += size
    collect(late, souts[0][0, :8] + functools.reduce(jnp.add, firsts))
    reduce_and_update(ffn1_names, "late")

    out = [loss, grad_x]
    for part in range(4):
        out += [results[n][part] for n in WEIGHTS]
    return tuple(out)
```

```python
import functools

import jax
import jax.numpy as jnp
from jax import lax
from jax.experimental import pallas as pl
from jax.experimental.pallas import tpu as pltpu

F32 = jnp.float32
BF16 = jnp.bfloat16
MESH = pl.DeviceIdType.MESH

N_META = 16
POOL_WINDOWS = (2, 4, 8, 16)
POOL_HALO = 16
N_HEADS = 4
GLA_GATE_TEMP = 16.0
CHUNK = 128
HEAD_GROUP = 2
CHUNK_UNROLL = 5
LN_EPS = 1e-5
RMS_EPS = 1e-6
ADAM_LR = 0.001
ADAM_B1 = 0.9
ADAM_B2 = 0.999
ADAM_EPS = 1e-08
ADAM_WD = 0.01
ADAM_STEP = 10
LANE = 128
BF16_ROWS = 16
ROW_TILE = 640
FFN_ROW_TILE = 640
FFN_CHUNKS = 4
FFN_SPLIT = 2
ROW_GROUPS = 2
WGRAD_K_MAX = 4224
N_SHARD = 4
N_DEV = 8

BIG = ("ffn1_w_gate", "ffn1_w_up", "ffn1_w_down", "w_in", "w_gate_up", "w_out",
       "ffn2_w_gate", "ffn2_w_up", "ffn2_w_down")
SMALL = ("ln1_g", "ln1_b", "b_gate", "w_pool", "pool_scale", "gla_norm_g", "ln2_g", "ln2_b", "ln3_g", "ln3_b")
WEIGHTS = ("meta_tokens", "ffn1_w_gate", "ffn1_w_up", "ffn1_w_down", "ln1_g", "ln1_b", "w_in", "w_gate_up",
           "b_gate", "w_pool", "pool_scale", "gla_norm_g", "w_out", "ln2_g", "ln2_b", "ffn2_w_gate",
           "ffn2_w_up", "ffn2_w_down", "ln3_g", "ln3_b")


def _tc_call(body, **kw):
    return pl.pallas_call(body, **kw)


def _comm_call(body, **kw):
    return pl.pallas_call(body, **kw)


def _seq(n):
    return pltpu.CompilerParams(dimension_semantics=("arbitrary",) * n)


def _mm(a, b):
    return jnp.dot(a.astype(BF16), b.astype(BF16), preferred_element_type=F32)


def _mm_nt(a, b):
    return lax.dot_general(a.astype(BF16), b.astype(BF16), (((1,), (1,)), ((), ())), preferred_element_type=F32)


def _mm_tn(a, b):
    return lax.dot_general(a.astype(BF16), b.astype(BF16), (((0,), (0,)), ((), ())), preferred_element_type=F32)


def _mm_01(a, b):
    hi = b.astype(BF16)
    lo = (b - hi.astype(F32)).astype(BF16)
    a = a.astype(BF16)
    return jnp.dot(a, hi, preferred_element_type=F32) + jnp.dot(a, lo, preferred_element_type=F32)


def _row_tile(t, most=None):
    tm = min(most or ROW_TILE, t)
    while t % tm:
        tm -= LANE
    return tm


def _silu_parts(g):
    sg = jax.nn.sigmoid(g)
    return sg, g * sg


def _ln_stats(r):
    mu = jnp.mean(r, axis=-1, keepdims=True)
    rc = r - mu
    var = jnp.mean(rc * rc, axis=-1, keepdims=True)
    rs = lax.rsqrt(var + LN_EPS)
    return rc * rs, rs


def _ln_bwd(dy, xh, rs, gam):
    dyg = dy * gam
    c1 = jnp.mean(dyg, axis=-1, keepdims=True)
    c2 = jnp.mean(dyg * xh, axis=-1, keepdims=True)
    return rs * (dyg - c1 - xh * c2)


def _ffn_fwd(xin, gam_in, bet_in, wg, wu, wd, alpha, name, loss_head=None):
    t, d = xin.shape
    nj, tf, _ = wg.shape
    tm = _row_tile(t, FFN_ROW_TILE)
    nt = t // tm
    share = tm // nj
    head = [] if loss_head is None else list(loss_head[:3])

    def body(x_ref, gi_ref, bi_ref, wg_ref, wu_ref, wd_ref, *rest):
        head_refs, rest = rest[:len(head)], rest[len(head):]
        xhat_ref, rstd_ref, hb_ref, go_ref, uo_ref = rest[:5]
        acc, hbs = rest[-2:]
        i = pl.program_id(0)
        j = pl.program_id(1)
        cur = i % 2

        if head:
            @pl.when((i == 0) & (j == 0))
            def _():
                rest[6][...] = jnp.zeros_like(rest[6])

        def norm_previous():
            rows = pl.ds(pl.multiple_of(j * share, BF16_ROWS), share)
            xhat, rs = _ln_stats(0.5 * acc[1 - cur, rows, :])
            xhat_ref[rows, :] = xhat
            rstd_ref[rows, :] = rs
            if head:
                g_ref, b_ref, t_ref = head_refs
                dy_ref, loss_ref = rest[5], rest[6]
                rowi = (i - 1) * tm + j * share + lax.broadcasted_iota(jnp.int32, (share, 1), 0)
                live = (rowi >= N_META) & (rowi < N_META + loss_head[3])
                diff = jnp.where(live, xhat * g_ref[...] + b_ref[...] - t_ref[rows, :], 0.0)
                dy_ref[rows, :] = diff * (1.0 / d)
                loss_ref[...] += jnp.sum(diff * diff) * (0.5 / d)

        @pl.when(i < nt)
        def _():
            @pl.when(j == 0)
            def _():
                h = x_ref[...] * gi_ref[...] + bi_ref[...]
                hb = h.astype(BF16)
                hbs[...] = hb
                hb_ref[...] = hb
                acc[cur] = (2.0 * alpha) * h

                @pl.when(i == 0)
                def _():
                    acc[1] = jnp.zeros((tm, d), F32)

            norm_previous()
            hb = hbs[...]
            g = _mm_nt(hb, wg_ref[...])
            u = _mm_nt(hb, wu_ref[...])
            _, sl = _silu_parts(g)
            go_ref[...] = g.astype(BF16)
            uo_ref[...] = u.astype(BF16)
            acc[cur] += jnp.dot((sl * u).astype(BF16), wd_ref[...], preferred_element_type=F32)

        @pl.when(i == nt)
        def _():
            norm_previous()

    here = lambda i, j: (jnp.minimum(i, nt - 1), 0)
    before = lambda i, j: (jnp.maximum(i - 1, 0), 0)
    chunk = lambda i, j: (jnp.where(i < nt, j, nj - 1), 0, 0)
    col = pl.BlockSpec((None, tm, tf), lambda i, j: (jnp.where(i < nt, j, nj - 1), jnp.minimum(i, nt - 1), 0))
    vec = pl.BlockSpec((1, d), lambda i, j: (0, 0))
    head_in = [vec, vec, pl.BlockSpec((tm, d), before)] if head else []
    head_out = [pl.BlockSpec((tm, d), before), pl.BlockSpec((8, LANE), lambda i, j: (0, 0))] if head else []
    head_shape = [jax.ShapeDtypeStruct((t, d), F32), jax.ShapeDtypeStruct((8, LANE), F32)] if head else []
    return _tc_call(
        body, name=name, grid=(nt + 1, nj),
        in_specs=[pl.BlockSpec((tm, d), here), vec, vec] + [pl.BlockSpec((None, tf, d), chunk)] * 3 + head_in,
        out_specs=[pl.BlockSpec((tm, d), before), pl.BlockSpec((tm, 1), before), pl.BlockSpec((tm, d), here), col, col]
                  + head_out,
        out_shape=[jax.ShapeDtypeStruct((t, d), F32), jax.ShapeDtypeStruct((t, 1), F32),
                   jax.ShapeDtypeStruct((t, d), BF16), jax.ShapeDtypeStruct((nj, t, tf), BF16),
                   jax.ShapeDtypeStruct((nj, t, tf), BF16)] + head_shape,
        scratch_shapes=[pltpu.VMEM((2, tm, d), F32), pltpu.VMEM((tm, d), BF16)],
        compiler_params=_seq(2),
    )(xin, gam_in, bet_in, wg, wu, wd, *head)


def _ffn_bwd(dh, xhat, rstd, ln_g, gb, ub, wg, wu, wd, alpha, name):
    t, d = dh.shape
    nj, tf, _ = wg.shape
    tm = _row_tile(t, FFN_ROW_TILE)
    nt = t // tm
    share = tm // nj

    def body(dh_ref, xh_ref, rs_ref, g_ref, gb_ref, ub_ref, wg_ref, wu_ref, wd_ref,
             dhin_ref, df_ref, dg_ref, du_ref, act_ref, dgam_ref, dbet_ref, df_s, dres_next, df_next):
        i = pl.program_id(0)
        j = pl.program_id(1)

        @pl.when((i == 0) & (j == 0))
        def _():
            dgam_ref[...] = jnp.zeros_like(dgam_ref)
            dbet_ref[...] = jnp.zeros_like(dbet_ref)

        def look_ahead():
            rows = pl.ds(pl.multiple_of(j * share, BF16_ROWS), share)
            dy = dh_ref[rows, :]
            xh = xh_ref[rows, :]
            dr = _ln_bwd(dy, xh, rs_ref[rows, :], g_ref[...])
            dres_next[rows, :] = alpha * dr
            df_next[rows, :] = (0.5 * dr).astype(BF16)
            live = jnp.where(i < nt, 1.0, 0.0)
            dgam_ref[...] += live * jnp.sum(dy * xh, axis=0, keepdims=True)
            dbet_ref[...] += live * jnp.sum(dy, axis=0, keepdims=True)

        @pl.when(i == 0)
        def _():
            look_ahead()

        @pl.when(i > 0)
        def _():
            @pl.when(j == 0)
            def _():
                dhin_ref[...] = dres_next[...]
                dfb = df_next[...]
                df_s[...] = dfb
                df_ref[...] = dfb

            look_ahead()
            for part in range(FFN_SPLIT):
                rows = pl.ds(part * (tm // FFN_SPLIT), tm // FFN_SPLIT)
                dact = _mm_nt(df_s[rows, :], wd_ref[...])
                g = gb_ref[rows, :].astype(F32)
                u = ub_ref[rows, :].astype(F32)
                sg, sl = _silu_parts(g)
                dg = (dact * u * (sg + sl * (1.0 - sg))).astype(BF16)
                du = (dact * sl).astype(BF16)
                dg_ref[rows, :] = dg
                du_ref[rows, :] = du
                act_ref[rows, :] = (sl * u).astype(BF16)
                dhin_ref[rows, :] += _mm(dg, wg_ref[...]) + _mm(du, wu_ref[...])

    ahead = lambda i, j: (jnp.minimum(i, nt - 1), 0)
    row = lambda i, j: (jnp.maximum(i - 1, 0), 0)
    chunk = lambda i, j: (jnp.where(i > 0, j, 0), 0, 0)
    col = pl.BlockSpec((None, tm, tf), lambda i, j: (jnp.where(i > 0, j, 0), jnp.maximum(i - 1, 0), 0))
    vec = pl.BlockSpec((1, d), lambda i, j: (0, 0))
    ff = jax.ShapeDtypeStruct((nj, t, tf), BF16)
    return _tc_call(
        body, name=name, grid=(nt + 1, nj),
        in_specs=[pl.BlockSpec((tm, d), ahead), pl.BlockSpec((tm, d), ahead), pl.BlockSpec((tm, 1), ahead), vec,
                  col, col] + [pl.BlockSpec((None, tf, d), chunk)] * 3,
        out_specs=[pl.BlockSpec((tm, d), row), pl.BlockSpec((tm, d), row), col, col, col, vec, vec],
        out_shape=[jax.ShapeDtypeStruct((t, d), F32), jax.ShapeDtypeStruct((t, d), BF16), ff, ff, ff,
                   jax.ShapeDtypeStruct((1, d), F32), jax.ShapeDtypeStruct((1, d), F32)],
        scratch_shapes=[pltpu.VMEM((tm, d), BF16), pltpu.VMEM((tm, d), F32), pltpu.VMEM((tm, d), BF16)],
        compiler_params=_seq(2),
    )(dh, xhat, rstd, ln_g, gb, ub, wg, wu, wd)


def _wgrad(a, b, tmm, tn, name, after=None):
    t = a.shape[-2]
    m = a.shape[-1] * (a.shape[0] if a.ndim == 3 else 1)
    n = b.shape[-1] * (b.shape[0] if b.ndim == 3 else 1)
    tk = max(k for k in range(BF16_ROWS, WGRAD_K_MAX + 1, BF16_ROWS) if t % k == 0)
    nk = t // tk
    extra = [] if after is None else [after]

    def body(a_ref, b_ref, *rest):
        o_ref, acc = rest[len(extra):]
        k = pl.program_id(2)

        @pl.when(k == 0)
        def _():
            acc[...] = jnp.zeros_like(acc)

        acc[...] += _mm_tn(a_ref[...], b_ref[...])

        @pl.when(k == nk - 1)
        def _():
            o_ref[...] = acc[...].astype(o_ref.dtype)

    a_spec = (pl.BlockSpec((None, tk, tmm), lambda i, j, k: (i, k, 0)) if a.ndim == 3
              else pl.BlockSpec((tk, tmm), lambda i, j, k: (k, i)))
    return _tc_call(
        body, name=name, grid=(m // tmm, n // tn, nk),
        in_specs=[a_spec, pl.BlockSpec((None, tk, tn), lambda i, j, k: (j, k, 0)) if b.ndim == 3
                  else pl.BlockSpec((tk, tn), lambda i, j, k: (k, j))] + [pl.BlockSpec(memory_space=pl.ANY)] * len(extra),
        out_specs=pl.BlockSpec((tmm, tn), lambda i, j, k: (i, j)),
        out_shape=jax.ShapeDtypeStruct((m, n), BF16),
        scratch_shapes=[pltpu.VMEM((tmm, tn), F32)],
        compiler_params=_seq(3),
    )(a, b, *extra)


def _inproj_fwd(xhat, gam, bet, w_main, w_lr, wgu, b_gate, widths, name):
    t, d = xhat.shape
    tm = _row_tile(t)
    kw = wgu.shape[1]
    offs = [0]
    for w in widths:
        offs.append(offs[-1] + w)

    def body(x_ref, g_ref, b_ref, wm_ref, wl_ref, wgu_ref, bg_ref, *outs):
        piece_refs, (zg_ref, la_ref, hb_ref) = outs[:len(widths)], outs[len(widths):]
        hb = (x_ref[...] * g_ref[...] + b_ref[...]).astype(BF16)
        hb_ref[...] = hb
        for p, ref in enumerate(piece_refs):
            ref[...] = _mm_nt(hb, wm_ref[offs[p]:offs[p + 1], :]).astype(ref.dtype)
        zg = _mm_nt(hb, wl_ref[...])
        zg_ref[...] = zg
        logit = _mm(zg, wgu_ref[...]) + bg_ref[...]
        la_ref[...] = (jnp.minimum(logit, 0.0) - jnp.log(1.0 + jnp.exp(-jnp.abs(logit)))) * (1.0 / GLA_GATE_TEMP)

    row = lambda i: (i, 0)
    full = lambda a: pl.BlockSpec(a.shape, lambda i: (0,) * a.ndim)
    out_w = list(widths) + [LANE, kw]
    out_t = [F32, F32, F32, BF16, F32, F32, F32]
    return _tc_call(
        body, name=name, grid=(t // tm,),
        in_specs=[pl.BlockSpec((tm, d), row), full(gam), full(bet), full(w_main), full(w_lr), full(wgu), full(b_gate)],
        out_specs=[pl.BlockSpec((tm, w), row) for w in out_w] + [pl.BlockSpec((tm, d), row)],
        out_shape=[jax.ShapeDtypeStruct((t, w), ty) for w, ty in zip(out_w, out_t)]
                  + [jax.ShapeDtypeStruct((t, d), BF16)],
        compiler_params=_seq(1),
    )(xhat, gam, bet, w_main, w_lr, wgu, b_gate)


def _inproj_bwd(dh_part, pieces, dzg, w_main, w_lr, name):
    t, d = dh_part.shape
    tm = _row_tile(t)
    widths = [p.shape[1] for p in pieces]
    offs = [0]
    for w in widths:
        offs.append(offs[-1] + w)

    def body(*refs):
        dhp_ref = refs[0]
        p_refs = refs[1:1 + len(widths)]
        dzg_ref, wm_ref, wl_ref, dh_ref, dz_ref = refs[1 + len(widths):]
        acc = dhp_ref[...] + _mm(dzg_ref[...], wl_ref[...])
        for p, ref in enumerate(p_refs):
            v = ref[...]
            dz_ref[:, offs[p]:offs[p + 1]] = v
            acc += _mm(v, wm_ref[offs[p]:offs[p + 1], :])
        dh_ref[...] = acc

    row = lambda i: (i, 0)
    full = lambda a: pl.BlockSpec(a.shape, lambda i: (0,) * a.ndim)
    return _tc_call(
        body, name=name, grid=(t // tm,),
        in_specs=[pl.BlockSpec((tm, d), row)] + [pl.BlockSpec((tm, w), row) for w in widths]
                 + [pl.BlockSpec((tm, LANE), row), full(w_main), full(w_lr)],
        out_specs=[pl.BlockSpec((tm, d), row), pl.BlockSpec((tm, offs[-1]), row)],
        out_shape=[jax.ShapeDtypeStruct((t, d), F32), jax.ShapeDtypeStruct((t, offs[-1]), BF16)],
        compiler_params=_seq(1),
    )(dh_part, *pieces, dzg, w_main, w_lr)


def _pool_cnt(tile, tm, w):
    t = tile * tm + lax.broadcasted_iota(jnp.int32, (tm, 1), 0)
    return jnp.minimum(t + 1, w).astype(F32)


def _pool_fwd(u, wp, scale, name):
    t, pw = u.shape
    tm = _row_tile(t)
    gd = wp.shape[1]

    def body(u_ref, wp_ref, sc_ref, y_ref, p_ref, ext):
        i = pl.program_id(0)

        @pl.when(i == 0)
        def _():
            ext[0:POOL_HALO, :] = jnp.zeros((POOL_HALO, pw), F32)

        ext[POOL_HALO:POOL_HALO + tm, :] = u_ref[...]
        for gi, w in enumerate(POOL_WINDOWS):
            cols = slice(gi * gd, (gi + 1) * gd)
            s = ext[pl.ds(POOL_HALO, tm), cols]
            tot = s
            for back in range(1, w):
                tot = tot + ext[pl.ds(POOL_HALO - back, tm), cols]
            p = (tot / _pool_cnt(i, tm, w) - s).astype(BF16)
            p_ref[:, cols] = p
            y_ref[:, cols] = (jnp.dot(p, wp_ref[gi], preferred_element_type=F32) * sc_ref[:, cols]).astype(BF16)
        ext[0:POOL_HALO, :] = ext[tm:tm + POOL_HALO, :]

    row = lambda i: (i, 0)
    return _tc_call(
        body, name=name, grid=(t // tm,),
        in_specs=[pl.BlockSpec((tm, pw), row), pl.BlockSpec(wp.shape, lambda i: (0, 0, 0)),
                  pl.BlockSpec((1, pw), lambda i: (0, 0))],
        out_specs=[pl.BlockSpec((tm, pw), row), pl.BlockSpec((tm, pw), row)],
        out_shape=[jax.ShapeDtypeStruct((t, pw), BF16), jax.ShapeDtypeStruct((t, pw), BF16)],
        scratch_shapes=[pltpu.VMEM((tm + POOL_HALO, pw), F32)],
        compiler_params=_seq(1),
    )(u, wp, scale)


def _pool_bwd(dy, pb, wp, scale, name):
    t, pw = dy.shape
    tm = _row_tile(t)
    nt = t // tm
    gd = wp.shape[1]

    def body(dy_ref, p_ref, wp_ref, sc_ref, du_ref, dwp_ref, dsc_ref, ext):
        i = pl.program_id(0)
        tile = nt - 1 - i

        @pl.when(i == 0)
        def _():
            ext[tm:tm + POOL_HALO, :] = jnp.zeros((POOL_HALO, pw), F32)
            dwp_ref[...] = jnp.zeros_like(dwp_ref)
            dsc_ref[...] = jnp.zeros_like(dsc_ref)

        dps = []
        for gi, w in enumerate(POOL_WINDOWS):
            cols = slice(gi * gd, (gi + 1) * gd)
            dyv = dy_ref[:, cols]
            p = p_ref[:, cols]
            dpre = (dyv * sc_ref[:, cols]).astype(BF16)
            dsc_ref[:, cols] += jnp.sum(dyv * jnp.dot(p, wp_ref[gi], preferred_element_type=F32), axis=0, keepdims=True)
            dwp_ref[gi] += _mm_tn(p, dpre)
            dp = _mm_nt(dpre, wp_ref[gi])
            dps.append(dp)
            ext[0:tm, cols] = dp / _pool_cnt(tile, tm, w)
        for gi, w in enumerate(POOL_WINDOWS):
            cols = slice(gi * gd, (gi + 1) * gd)
            tot = ext[pl.ds(0, tm), cols]
            for fwd in range(1, w):
                tot = tot + ext[pl.ds(fwd, tm), cols]
            du_ref[:, cols] = (tot - dps[gi]).astype(BF16)
        ext[tm:tm + POOL_HALO, :] = ext[0:POOL_HALO, :]

    row = lambda i: (nt - 1 - i, 0)
    return _tc_call(
        body, name=name, grid=(nt,),
        in_specs=[pl.BlockSpec((tm, pw), row), pl.BlockSpec((tm, pw), row),
                  pl.BlockSpec(wp.shape, lambda i: (0, 0, 0)), pl.BlockSpec((1, pw), lambda i: (0, 0))],
        out_specs=[pl.BlockSpec((tm, pw), row), pl.BlockSpec(wp.shape, lambda i: (0, 0, 0)),
                   pl.BlockSpec((1, pw), lambda i: (0, 0))],
        out_shape=[jax.ShapeDtypeStruct((t, pw), BF16), jax.ShapeDtypeStruct(wp.shape, F32),
                   jax.ShapeDtypeStruct((1, pw), F32)],
        scratch_shapes=[pltpu.VMEM((tm + POOL_HALO, pw), F32)],
        compiler_params=_seq(1),
    )(dy, pb, wp, scale)


def _gla_masks(kw, vw, nh):
    dk, dv = kw // nh, vw // nh
    lane_k = lax.broadcasted_iota(jnp.int32, (1, kw), 1)
    lane_v = lax.broadcasted_iota(jnp.int32, (1, vw), 1)
    hk = [((lane_k >= h * dk) & (lane_k < (h + 1) * dk)).astype(F32) for h in range(nh)]
    hv = [((lane_v >= h * dv) & (lane_v < (h + 1) * dv)).astype(F32) for h in range(nh)]
    r = lax.broadcasted_iota(jnp.int32, (CHUNK, CHUNK), 0)
    c = lax.broadcasted_iota(jnp.int32, (CHUNK, CHUNK), 1)
    tril = r >= c
    rs = lax.broadcasted_iota(jnp.int32, (nh * CHUNK, CHUNK), 0) & (CHUNK - 1)
    stril = rs >= lax.broadcasted_iota(jnp.int32, (nh * CHUNK, CHUNK), 1)
    return hk, hv, tril, stril


def _block_diag(x, hk, dv):
    return jnp.concatenate([x[h * dv:(h + 1) * dv, :] * hk[h] for h in range(len(hk))], axis=0)


def _gla_fwd(q, k, v, loga, r, gnorm, name):
    t, kw = q.shape
    vw = v.shape[1]
    dk, dv = kw // N_HEADS, vw // N_HEADS
    tm = _row_tile(t)
    nc = tm // CHUNK
    qscale = dk ** -0.5
    HEAD_GROUP = N_HEADS
    ngroup = N_HEADS // HEAD_GROUP
    kwg, vwg = kw // ngroup, vw // ngroup

    def body(q_ref, k_ref, v_ref, la_ref, r_ref, gn_ref, o_ref, y_ref, sall_ref, st):
        @pl.when(pl.program_id(0) == 0)
        def _():
            st[...] = jnp.zeros_like(st)

        hk, hv, tril, stril = _gla_masks(kwg, vwg, HEAD_GROUP)
        trif = tril.astype(F32)

        def chunk(c, carry):
            rows = pl.ds(pl.multiple_of(c * CHUNK, CHUNK), CHUNK)
            la = la_ref[rows, :]
            b = _mm_01(trif, la)
            bl = jnp.sum(la, axis=0, keepdims=True)
            el = jnp.exp(bl)
            qb_all = q_ref[rows, :] * (qscale * jnp.exp(b))
            kk = k_ref[rows, :]
            kb_all = kk * jnp.exp(-b)
            kl_all = kk * jnp.exp(bl - b)
            vv_all = v_ref[rows, :]
            compacts = []
            for gi in range(ngroup):
                ks, vg = slice(gi * kwg, (gi + 1) * kwg), slice(gi * vwg, (gi + 1) * vwg)
                qb, kb, kl, vv = qb_all[:, ks], kb_all[:, ks], kl_all[:, ks], vv_all[:, vg]
                s_t = st[gi]
                compact = s_t[0:dv, :]
                for h in range(1, HEAD_GROUP):
                    compact = compact + s_t[h * dv:(h + 1) * dv, :]
                compacts.append(compact)
                qx = jnp.concatenate([qb.astype(BF16) * hk[h].astype(BF16) for h in range(HEAD_GROUP)], axis=0)
                a = jnp.where(stril, _mm_nt(qx, kb), 0.0).astype(BF16)
                o_inter = _mm_nt(qb, s_t)
                for h in range(HEAD_GROUP):
                    vs = slice(h * dv, (h + 1) * dv)
                    o_ref[rows, gi * vwg + h * dv:gi * vwg + (h + 1) * dv] = (
                        o_inter[:, vs] + _mm(a[h * CHUNK:(h + 1) * CHUNK, :], vv[:, vs]))
                st[gi] = s_t * el[:, ks] + _block_diag(_mm_tn(vv, kl), hk, dv)
            sall_ref[c] = jnp.concatenate(compacts, axis=1)
            return carry

        lax.fori_loop(0, nc, chunk, 0, unroll=CHUNK_UNROLL)
        for h in range(N_HEADS):
            vs = slice(h * dv, (h + 1) * dv)
            oh = o_ref[:, vs]
            on = oh * lax.rsqrt(jnp.mean(oh * oh, axis=-1, keepdims=True) + RMS_EPS)
            _, sl = _silu_parts(r_ref[:, vs])
            y_ref[:, vs] = (on * gn_ref[:, vs] * sl).astype(BF16)

    row = lambda i: (i, 0)
    return _tc_call(
        body, name=name, grid=(t // tm,),
        in_specs=[pl.BlockSpec((tm, kw), row), pl.BlockSpec((tm, kw), row), pl.BlockSpec((tm, vw), row),
                  pl.BlockSpec((tm, kw), row), pl.BlockSpec((tm, vw), row), pl.BlockSpec((1, vw), lambda i: (0, 0))],
        out_specs=[pl.BlockSpec((tm, vw), row), pl.BlockSpec((tm, vw), row),
                   pl.BlockSpec((nc, dv, kw), lambda i: (i, 0, 0))],
        out_shape=[jax.ShapeDtypeStruct((t, vw), F32), jax.ShapeDtypeStruct((t, vw), BF16),
                   jax.ShapeDtypeStruct((t // CHUNK, dv, kw), F32)],
        scratch_shapes=[pltpu.VMEM((ngroup, vwg, kwg), F32)],
        compiler_params=_seq(1),
    )(q, k, v, loga, r, gnorm)


def _gla_bwd(dy, o, r, gnorm, q, k, v, loga, zg, sall, wgu, name):
    t, kw = q.shape
    vw = v.shape[1]
    dk, dv = kw // N_HEADS, vw // N_HEADS
    tm = _row_tile(t)
    nt = t // tm
    nc = tm // CHUNK
    qscale = dk ** -0.5
    ngroup = N_HEADS // HEAD_GROUP
    kwg, vwg = kw // ngroup, vw // ngroup

    def body(dy_ref, o_ref, r_ref, gn_ref, q_ref, k_ref, v_ref, la_ref, zg_ref, sall_ref, wgu_ref,
             dq_ref, dk_ref, dv_ref, dr_ref, dzg_ref, dwgu_ref, dbg_ref, dgn_ref, dst, do_s):
        @pl.when(pl.program_id(0) == 0)
        def _():
            dst[...] = jnp.zeros_like(dst)
            dwgu_ref[...] = jnp.zeros_like(dwgu_ref)
            dbg_ref[...] = jnp.zeros_like(dbg_ref)
            dgn_ref[...] = jnp.zeros_like(dgn_ref)

        for h in range(N_HEADS):
            vs = slice(h * dv, (h + 1) * dv)
            oh = o_ref[:, vs]
            rinv = lax.rsqrt(jnp.mean(oh * oh, axis=-1, keepdims=True) + RMS_EPS)
            on = oh * rinv
            rr = r_ref[:, vs]
            sg, sl = _silu_parts(rr)
            dyv = dy_ref[:, vs]
            gn = gn_ref[:, vs]
            dgn_ref[:, vs] += jnp.sum(dyv * on * sl, axis=0, keepdims=True)
            dr_ref[:, vs] = (dyv * on * gn * (sg + sl * (1.0 - sg))).astype(BF16)
            don = dyv * gn * sl
            do_s[:, vs] = rinv * (don - on * jnp.mean(don * on, axis=-1, keepdims=True))

        hk, hv, tril, stril = _gla_masks(kwg, vwg, HEAD_GROUP)
        trif = tril.astype(F32)
        triuf = (lax.broadcasted_iota(jnp.int32, (CHUNK, CHUNK), 0)
                 <= lax.broadcasted_iota(jnp.int32, (CHUNK, CHUNK), 1)).astype(F32)
        last_row = lax.broadcasted_iota(jnp.int32, (CHUNK, 1), 0) == CHUNK - 1

        def chunk(idx, carry):
            c = nc - 1 - idx
            rows = pl.ds(pl.multiple_of(c * CHUNK, CHUNK), CHUNK)
            la = la_ref[rows, :]
            b = _mm_01(trif, la)
            bl = jnp.sum(la, axis=0, keepdims=True)
            eb_all = jnp.exp(b)
            enb_all = jnp.exp(-b)
            ebl_all = jnp.exp(bl - b)
            el_all = jnp.exp(bl)
            qb_all = q_ref[rows, :] * (qscale * eb_all)
            kk = k_ref[rows, :]
            kb_all = kk * enb_all
            kl_all = kk * ebl_all
            vv_all = v_ref[rows, :]
            do_all = do_s[rows, :]
            compact_all = sall_ref[c]
            dbs = []
            for gi in range(ngroup):
                ks, vg = slice(gi * kwg, (gi + 1) * kwg), slice(gi * vwg, (gi + 1) * vwg)
                eb, enb, ebl, el = eb_all[:, ks], enb_all[:, ks], ebl_all[:, ks], el_all[:, ks]
                qb, kb, kl, vv, do = qb_all[:, ks], kb_all[:, ks], kl_all[:, ks], vv_all[:, vg], do_all[:, vg]
                s_t = jnp.concatenate([compact_all[:, ks] * hk[h] for h in range(HEAD_GROUP)], axis=0)
                ds_t = dst[gi]
                qx = jnp.concatenate([qb.astype(BF16) * hk[h].astype(BF16) for h in range(HEAD_GROUP)], axis=0)
                dox = jnp.concatenate([do.astype(BF16) * hv[h].astype(BF16) for h in range(HEAD_GROUP)], axis=0)
                a = jnp.where(stril, _mm_nt(qx, kb), 0.0).astype(BF16)
                da = jnp.where(stril, _mm_nt(dox, vv), 0.0).astype(BF16)
                dv_ref[rows, vg] = (_mm_tn(a, dox) + _mm_nt(kl, ds_t)).astype(BF16)
                dak = _mm(da, kb)
                dqb = _mm(do, s_t)
                for h in range(HEAD_GROUP):
                    dqb = dqb + dak[h * CHUNK:(h + 1) * CHUNK, :] * hk[h]
                dkb = _mm_tn(da, qx)
                dkl = _mm(vv, ds_t)
                dbl = jnp.sum(dkl * kl, axis=0, keepdims=True) + el * jnp.sum(ds_t * s_t, axis=0, keepdims=True)
                dst[gi] = ds_t * el + _block_diag(_mm_tn(do, qb), hk, dv)
                dq_ref[rows, ks] = (dqb * (qscale * eb)).astype(BF16)
                dk_ref[rows, ks] = (dkb * enb + dkl * ebl).astype(BF16)
                dbs.append(dqb * qb - dkb * kb - dkl * kl + jnp.where(last_row, dbl, 0.0))
            db = jnp.concatenate(dbs, axis=1)
            dla = _mm_01(triuf, db)
            dlogit = dla * (1.0 / GLA_GATE_TEMP) * (1.0 - jnp.exp(GLA_GATE_TEMP * la))
            dzg_ref[rows, :] = _mm_nt(dlogit, wgu_ref[...]).astype(BF16)
            dwgu_ref[...] += _mm_tn(zg_ref[rows, :], dlogit)
            dbg_ref[...] += jnp.sum(dlogit, axis=0, keepdims=True)
            return carry

        lax.fori_loop(0, nc, chunk, 0, unroll=CHUNK_UNROLL)

    row = lambda i: (nt - 1 - i, 0)
    const = lambda i: (0, 0)
    return _tc_call(
        body, name=name, grid=(nt,),
        in_specs=[pl.BlockSpec((tm, vw), row), pl.BlockSpec((tm, vw), row), pl.BlockSpec((tm, vw), row),
                  pl.BlockSpec((1, vw), const), pl.BlockSpec((tm, kw), row), pl.BlockSpec((tm, kw), row),
                  pl.BlockSpec((tm, vw), row), pl.BlockSpec((tm, kw), row), pl.BlockSpec((tm, LANE), row),
                  pl.BlockSpec((nc, dv, kw), lambda i: (nt - 1 - i, 0, 0)), pl.BlockSpec((LANE, kw), const)],
        out_specs=[pl.BlockSpec((tm, kw), row), pl.BlockSpec((tm, kw), row), pl.BlockSpec((tm, vw), row),
                   pl.BlockSpec((tm, vw), row), pl.BlockSpec((tm, LANE), row), pl.BlockSpec((LANE, kw), const),
                   pl.BlockSpec((1, kw), const), pl.BlockSpec((1, vw), const)],
        out_shape=[jax.ShapeDtypeStruct((t, kw), BF16), jax.ShapeDtypeStruct((t, kw), BF16),
                   jax.ShapeDtypeStruct((t, vw), BF16), jax.ShapeDtypeStruct((t, vw), BF16),
                   jax.ShapeDtypeStruct((t, LANE), BF16), jax.ShapeDtypeStruct((LANE, kw), F32),
                   jax.ShapeDtypeStruct((1, kw), F32), jax.ShapeDtypeStruct((1, vw), F32)],
        scratch_shapes=[pltpu.VMEM((ngroup, vwg, kwg), F32), pltpu.VMEM((tm, vw), F32)],
        compiler_params=_seq(1),
    )(dy, o, r, gnorm, q, k, v, loga, zg, sall, wgu)


def _outproj_fwd(yp, yg, w_out, xhat, gam, bet, alpha, name):
    t, d = xhat.shape
    pw = yp.shape[1]
    tm = _row_tile(t)

    def body(yp_ref, yg_ref, w_ref, x_ref, g_ref, b_ref, xhat_ref, rstd_ref):
        for part in range(ROW_GROUPS):
            rows = pl.ds(part * (tm // ROW_GROUPS), tm // ROW_GROUPS)
            h = x_ref[rows, :] * g_ref[...] + b_ref[...]
            y = (jnp.dot(yp_ref[rows, :], w_ref[0:pw, :], preferred_element_type=F32)
                 + jnp.dot(yg_ref[rows, :], w_ref[pw:, :], preferred_element_type=F32))
            xh, rs = _ln_stats(alpha * h + y)
            xhat_ref[rows, :] = xh
            rstd_ref[rows, :] = rs

    row = lambda i: (i, 0)
    vec = pl.BlockSpec((1, d), lambda i: (0, 0))
    return _tc_call(
        body, name=name, grid=(t // tm,),
        in_specs=[pl.BlockSpec((tm, pw), row), pl.BlockSpec((tm, yg.shape[1]), row),
                  pl.BlockSpec(w_out.shape, lambda i: (0, 0)), pl.BlockSpec((tm, d), row), vec, vec],
        out_specs=[pl.BlockSpec((tm, d), row), pl.BlockSpec((tm, 1), row)],
        out_shape=[jax.ShapeDtypeStruct((t, d), F32), jax.ShapeDtypeStruct((t, 1), F32)],
        compiler_params=_seq(1),
    )(yp, yg, w_out, xhat, gam, bet)


def _outproj_bwd(dh, xhat, rstd, ln_g, w_out, pw, alpha, name):
    t, d = dh.shape
    tm = _row_tile(t)
    gw = w_out.shape[0] - pw

    def body(dh_ref, xh_ref, rs_ref, g_ref, w_ref, dyb_ref, dyp_ref, dyg_ref, dres_ref, dgam_ref, dbet_ref):
        @pl.when(pl.program_id(0) == 0)
        def _():
            dgam_ref[...] = jnp.zeros_like(dgam_ref)
            dbet_ref[...] = jnp.zeros_like(dbet_ref)

        for part in range(ROW_GROUPS):
            rows = pl.ds(part * (tm // ROW_GROUPS), tm // ROW_GROUPS)
            dy = dh_ref[rows, :]
            xh = xh_ref[rows, :]
            dr = _ln_bwd(dy, xh, rs_ref[rows, :], g_ref[...])
            dgam_ref[...] += jnp.sum(dy * xh, axis=0, keepdims=True)
            dbet_ref[...] += jnp.sum(dy, axis=0, keepdims=True)
            drb = dr.astype(BF16)
            dyb_ref[rows, :] = drb
            dres_ref[rows, :] = alpha * dr
            dyp_ref[rows, :] = _mm_nt(drb, w_ref[0:pw, :])
            dyg_ref[rows, :] = _mm_nt(drb, w_ref[pw:, :])

    row = lambda i: (i, 0)
    vec = pl.BlockSpec((1, d), lambda i: (0, 0))
    return _tc_call(
        body, name=name, grid=(t // tm,),
        in_specs=[pl.BlockSpec((tm, d), row), pl.BlockSpec((tm, d), row), pl.BlockSpec((tm, 1), row), vec,
                  pl.BlockSpec(w_out.shape, lambda i: (0, 0))],
        out_specs=[pl.BlockSpec((tm, d), row), pl.BlockSpec((tm, pw), row), pl.BlockSpec((tm, gw), row),
                   pl.BlockSpec((tm, d), row), vec, vec],
        out_shape=[jax.ShapeDtypeStruct((t, d), BF16), jax.ShapeDtypeStruct((t, pw), F32),
                   jax.ShapeDtypeStruct((t, gw), F32), jax.ShapeDtypeStruct((t, d), F32),
                   jax.ShapeDtypeStruct((1, d), F32), jax.ShapeDtypeStruct((1, d), F32)],
        compiler_params=_seq(1),
    )(dh, xhat, rstd, ln_g, w_out)


def _rows_block(r, c):
    best = r
    for cand in range(BF16_ROWS, r, BF16_ROWS):
        if r % cand == 0 and cand * c * 4 <= (1 << 20):
            best = cand
    return best if best * c * 4 <= (4 << 20) else r


def _sum_slots(me, mine, recvs, name, layers_side_by_side=False):
    nl = len(recvs)
    ns, r, c = recvs[0].shape
    tr = _rows_block(r, c)

    def body(me_ref, *refs):
        o_ref = refs[nl * ns]
        for l in range(nl):
            acc = refs[l * ns][...].astype(F32)
            for s in range(1, ns):
                acc = acc + refs[l * ns + s][...].astype(F32)
            if layers_side_by_side:
                o_ref[0, :, l * c:(l + 1) * c] = acc.astype(o_ref.dtype)
            else:
                o_ref[l] = acc.astype(o_ref.dtype)

    def slot(s):
        return pl.BlockSpec((None, tr, c), lambda i, me_ref: ((me_ref[0] + s) % ns, i, 0))

    out = (1, r, nl * c) if layers_side_by_side else (nl, r, c)
    operands = []
    for l in range(nl):
        operands += [mine[l]] + [recvs[l]] * (ns - 1)
    return _tc_call(
        body, name=name,
        grid_spec=pltpu.PrefetchScalarGridSpec(
            num_scalar_prefetch=1, grid=(r // tr,),
            in_specs=[slot(s) for s in range(ns)] * nl,
            out_specs=pl.BlockSpec((out[0], tr, out[2]), lambda i, me_ref: (0, i, 0))),
        out_shape=jax.ShapeDtypeStruct(out, recvs[0].dtype),
        compiler_params=_seq(1),
    )(me, *operands)


def _adamw(w, terms, m, v, name):
    nl, r, c = w.shape
    tc = c
    while tc % (2 * LANE) == 0 and tc > 4 * LANE:
        tc //= 2
    tr = _rows_block(r, tc)
    nterm = len(terms)

    def body(*refs):
        w_ref = refs[0]
        t_refs = refs[1:1 + nterm]
        m_ref, v_ref, g_ref, d_ref, nm_ref, nv_ref = refs[1 + nterm:]
        g = t_refs[0][...].astype(F32)
        for tr_ in t_refs[1:]:
            g = g + tr_[...].astype(F32)
        nm = ADAM_B1 * m_ref[...] + (1.0 - ADAM_B1) * g
        nv = ADAM_B2 * v_ref[...] + (1.0 - ADAM_B2) * jnp.square(g)
        m_hat = nm / (1.0 - ADAM_B1 ** ADAM_STEP)
        v_hat = nv / (1.0 - ADAM_B2 ** ADAM_STEP)
        g_ref[...] = g
        d_ref[...] = -ADAM_LR * (m_hat / (jnp.sqrt(v_hat) + ADAM_EPS) + ADAM_WD * w_ref[...])
        nm_ref[...] = nm
        nv_ref[...] = nv

    spec = pl.BlockSpec((None, tr, tc), lambda l, i, j: (l, i, j))
    shp = jax.ShapeDtypeStruct((nl, r, c), F32)
    return _tc_call(
        body, name=name, grid=(nl, r // tr, c // tc),
        in_specs=[spec] * (3 + nterm), out_specs=[spec] * 4, out_shape=[shp] * 4,
        compiler_params=_seq(3),
    )(w, *terms, m, v)


XY_RELATIONS = ((1, 0, 0), (0, 1, 0), (1, 1, 0))
ALL_RELATIONS = tuple((fx, fy, fc) for fx in (0, 1) for fy in (0, 1) for fc in (0, 1) if fx or fy or fc)
SIBLING = ((0, 0, 1),)
HBM_SPEC = pl.BlockSpec(memory_space=pltpu.HBM)
SEM_SPEC = pl.BlockSpec(memory_space=pltpu.SEMAPHORE)
DATAFLOW = pltpu.SideEffectType.DATAFLOW_SIDE_EFFECTING


def _split_call(body, **kw):
    return pl.pallas_call(body, **kw)


def _flip(v, f):
    return 1 - v if f else v


def _any_spec(n):
    return [pl.BlockSpec(memory_space=pl.ANY)] * n


def _relations(kind):
    return {"bcast": ALL_RELATIONS, "swap": SIBLING}.get(kind, XY_RELATIONS)


def _copies(kind, arr, land, sems):
    x, y, c = lax.axis_index("x"), lax.axis_index("y"), lax.axis_index("c")
    out = []
    for (fx, fy, fc), (send_sem, recv_sem) in zip(_relations(kind), sems):
        px, py, pc = _flip(x, fx), _flip(y, fy), _flip(c, fc)
        if kind == "bcast":
            mine, theirs = 4 * x + 2 * y + c, 4 * px + 2 * py + pc
        else:
            mine, theirs = 2 * x + y, 2 * px + py
        src, to_mine, to_theirs = arr, land.at[mine], land.at[theirs]
        if kind == "swap":
            to_mine = to_theirs = land
        if kind == "scatter":
            src = arr.at[theirs]
        if kind == "gather_half":
            rows = _my_half(arr.shape[0], c)
            src, to_mine, to_theirs = arr.at[rows], land.at[mine, rows], land.at[theirs, rows]
        both = dict(src_ref=src, send_sem=send_sem, recv_sem=recv_sem, device_id=(px, py, pc), device_id_type=MESH)
        out.append((pltpu.make_async_remote_copy(dst_ref=to_mine, **both),
                    pltpu.make_async_remote_copy(dst_ref=to_theirs, **both)))
    return out


def _my_half(nrows, c):
    return pl.ds(c * (nrows // 2), nrows // 2)


def _share_halves(name, kinds, lands):
    ks = [k for k, kd in enumerate(kinds) if kd == "gather_half"]
    n = len(ks)

    def body(*refs):
        l_refs = refs[n:2 * n]
        send_sems, recv_sems = refs[2 * n:]
        x, y, c = lax.axis_index("x"), lax.axis_index("y"), lax.axis_index("c")
        copies = []
        for i in range(n):
            nrows = l_refs[i].shape[1]
            for r, (fx, fy, _) in enumerate(XY_RELATIONS):
                slot = 2 * _flip(x, fx) + _flip(y, fy)
                both = dict(src_ref=l_refs[i].at[slot, _my_half(nrows, c)], send_sem=send_sems.at[i, r],
                            recv_sem=recv_sems.at[i, r], device_id=(x, y, 1 - c), device_id_type=MESH)
                copies.append((pltpu.make_async_remote_copy(dst_ref=l_refs[i].at[slot, _my_half(nrows, c)], **both),
                               pltpu.make_async_remote_copy(dst_ref=l_refs[i].at[slot, _my_half(nrows, 1 - c)], **both)))
        for send, _ in copies:
            send.start()
        for _, arrival in copies:
            arrival.wait_recv()
        for send, _ in copies:
            send.wait_send()

    outs = _comm_call(
        body, name=name,
        in_specs=_any_spec(n), out_specs=_any_spec(n),
        out_shape=[jax.ShapeDtypeStruct(lands[k].shape, lands[k].dtype) for k in ks],
        input_output_aliases={i: i for i in range(n)},
        scratch_shapes=[pltpu.SemaphoreType.DMA((n, 3)), pltpu.SemaphoreType.DMA((n, 3))],
    )(*[lands[k] for k in ks])
    lands = list(lands)
    for k, o in zip(ks, outs):
        lands[k] = o
    return lands


def _sem_pairs(kinds, sems):
    out, at = [], 0
    for kind in kinds:
        nrel = len(_relations(kind))
        out.append([(sems[at + 2 * r], sems[at + 2 * r + 1]) for r in range(nrel)])
        at += 2 * nrel
    return out


def _part_of(st, ks):
    at = [0]
    for kind in st["kinds"]:
        at.append(at[-1] + 2 * len(_relations(kind)))
    return dict(kinds=[st["kinds"][k] for k in ks], arrs=[st["arrs"][k] for k in ks],
                lands=[st["lands"][k] for k in ks], sems=[s for k in ks for s in st["sems"][at[k]:at[k + 1]]])


def _exchange_start(name, kinds, arrs, lands):
    n = len(arrs)
    nsem = sum(2 * len(_relations(kd)) for kd in kinds)

    def body(*refs):
        a_refs, l_refs = refs[:n], refs[n:2 * n]
        pairs = _sem_pairs(kinds, refs[2 * n:2 * n + nsem])
        token = refs[-1]
        for k in range(n):
            for send, _ in _copies(kinds[k], a_refs[k], l_refs[k], pairs[k]):
                send.start()
        token[...] = jnp.zeros_like(token)

    thru = [pltpu.HBM(a.shape, a.dtype) for a in list(arrs) + list(lands)]
    outs = _split_call(
        body, name=name,
        out_shape=(*[pltpu.SemaphoreType.DMA(())] * nsem, *thru, jax.ShapeDtypeStruct((8, LANE), F32)),
        in_specs=[HBM_SPEC] * (2 * n),
        out_specs=(*[SEM_SPEC] * nsem, *[HBM_SPEC] * (2 * n), pl.BlockSpec(memory_space=pltpu.VMEM)),
        input_output_aliases={i: nsem + i for i in range(2 * n)},
        compiler_params=pltpu.CompilerParams(has_side_effects=DATAFLOW),
    )(*[pltpu.with_memory_space_constraint(a, pltpu.HBM) for a in list(arrs) + list(lands)])
    return dict(kinds=kinds, sems=outs[:nsem], arrs=outs[nsem:nsem + n], lands=outs[nsem + n:nsem + 2 * n],
                token=outs[-1])


def _exchange_wait(name, st, after):
    kinds = st["kinds"]
    n = len(kinds)
    nsem = len(st["sems"])

    def body(*refs):
        a_refs, l_refs = refs[:n], refs[n:2 * n]
        pairs = _sem_pairs(kinds, refs[2 * n:2 * n + nsem])
        for k in range(n):
            for _, arrival in _copies(kinds[k], a_refs[k], l_refs[k], pairs[k]):
                arrival.wait_send()
                arrival.wait_recv()
        refs[-1][...] = jnp.zeros_like(refs[-1])

    ins = list(st["arrs"]) + list(st["lands"])
    outs = _split_call(
        body, name=name,
        out_shape=[pltpu.HBM(a.shape, a.dtype) for a in ins] + [jax.ShapeDtypeStruct((8, LANE), F32)],
        in_specs=[HBM_SPEC] * (2 * n) + [SEM_SPEC] * nsem + [pl.BlockSpec(memory_space=pl.ANY)],
        out_specs=[HBM_SPEC] * (2 * n) + [pl.BlockSpec(memory_space=pltpu.VMEM)],
        input_output_aliases={i: i for i in range(2 * n)},
        compiler_params=pltpu.CompilerParams(has_side_effects=DATAFLOW),
    )(*ins, *st["sems"], after)
    return outs[:n], outs[n:2 * n], outs[-1]


def _landing(own, slot, nslot):
    return lax.dynamic_update_slice(lax.empty((nslot,) + own.shape, own.dtype), own[None], (slot,) + (0,) * own.ndim)


def _col_shards(a, n=N_SHARD):
    r, c = a.shape
    return a.reshape(r, n, c // n).transpose(1, 0, 2)


def _from_col_shards(a):
    n, r, cs = a.shape
    return a.transpose(1, 0, 2).reshape(r, n * cs)


def kernel(x, meta_tokens, ffn1_w_gate, ffn1_w_up, ffn1_w_down, ln1_g, ln1_b, w_in, w_gate_up, b_gate, w_pool, pool_scale, gla_norm_g, w_out, ln2_g, ln2_b, ffn2_w_gate, ffn2_w_up, ffn2_w_down, ln3_g, ln3_b, loss_target, m_meta_tokens, m_ffn1_w_gate, m_ffn1_w_up, m_ffn1_w_down, m_ln1_g, m_ln1_b, m_w_in, m_w_gate_up, m_b_gate, m_w_pool, m_pool_scale, m_gla_norm_g, m_w_out, m_ln2_g, m_ln2_b, m_ffn2_w_gate, m_ffn2_w_up, m_ffn2_w_down, m_ln3_g, m_ln3_b, v_meta_tokens, v_ffn1_w_gate, v_ffn1_w_up, v_ffn1_w_down, v_ln1_g, v_ln1_b, v_w_in, v_w_gate_up, v_b_gate, v_w_pool, v_pool_scale, v_gla_norm_g, v_w_out, v_ln2_g, v_ln2_b, v_ffn2_w_gate, v_ffn2_w_up, v_ffn2_w_down, v_ln3_g, v_ln3_b):
    w = dict(meta_tokens=meta_tokens, ffn1_w_gate=ffn1_w_gate, ffn1_w_up=ffn1_w_up, ffn1_w_down=ffn1_w_down,
             ln1_g=ln1_g, ln1_b=ln1_b, w_in=w_in, w_gate_up=w_gate_up, b_gate=b_gate, w_pool=w_pool,
             pool_scale=pool_scale, gla_norm_g=gla_norm_g, w_out=w_out, ln2_g=ln2_g, ln2_b=ln2_b,
             ffn2_w_gate=ffn2_w_gate, ffn2_w_up=ffn2_w_up, ffn2_w_down=ffn2_w_down, ln3_g=ln3_g, ln3_b=ln3_b)
    mom1 = dict(meta_tokens=m_meta_tokens, ffn1_w_gate=m_ffn1_w_gate, ffn1_w_up=m_ffn1_w_up,
                ffn1_w_down=m_ffn1_w_down, ln1_g=m_ln1_g, ln1_b=m_ln1_b, w_in=m_w_in, w_gate_up=m_w_gate_up,
                b_gate=m_b_gate, w_pool=m_w_pool, pool_scale=m_pool_scale, gla_norm_g=m_gla_norm_g, w_out=m_w_out,
                ln2_g=m_ln2_g, ln2_b=m_ln2_b, ffn2_w_gate=m_ffn2_w_gate, ffn2_w_up=m_ffn2_w_up,
                ffn2_w_down=m_ffn2_w_down, ln3_g=m_ln3_g, ln3_b=m_ln3_b)
    mom2 = dict(meta_tokens=v_meta_tokens, ffn1_w_gate=v_ffn1_w_gate, ffn1_w_up=v_ffn1_w_up,
                ffn1_w_down=v_ffn1_w_down, ln1_g=v_ln1_g, ln1_b=v_ln1_b, w_in=v_w_in, w_gate_up=v_w_gate_up,
                b_gate=v_b_gate, w_pool=v_w_pool, pool_scale=v_pool_scale, gla_norm_g=v_gla_norm_g, w_out=v_w_out,
                ln2_g=v_ln2_g, ln2_b=v_ln2_b, ffn2_w_gate=v_ffn2_w_gate, ffn2_w_up=v_ffn2_w_up,
                ffn2_w_down=v_ffn2_w_down, ln3_g=v_ln3_g, ln3_b=v_ln3_b)

    xs = x[0]
    s_len, d = xs.shape
    nl = ln1_g.shape[0]
    alpha = (2.0 * nl) ** 0.25
    t_real = N_META + s_len
    t_pad = -(-t_real // LANE) * LANE
    pw = pool_scale.shape[1]
    kw = b_gate.shape[1]
    vw = gla_norm_g.shape[1]
    rank = w_gate_up.shape[1]
    widths = (pw, kw, kw, vw, vw)
    n_main = sum(widths)
    dff_s = ffn1_w_gate.shape[2]
    dff_c = N_SHARD * dff_s // FFN_CHUNKS

    me_xy = 2 * lax.axis_index("x") + lax.axis_index("y")
    me_all = 2 * me_xy + lax.axis_index("c")
    ffn1_names = ("ffn1_w_gate", "ffn1_w_up", "ffn1_w_down")
    mix_names = ("w_out", "w_gate_up", "w_in")
    ffn2_names = ("ffn2_w_gate", "ffn2_w_up", "ffn2_w_down")

    gate_up = ("ffn1_w_gate", "ffn1_w_up", "ffn2_w_gate", "ffn2_w_up")

    def stored(n, a):
        if n in gate_up:
            return jnp.swapaxes(a, 1, 2)
        return jnp.transpose(a, (2, 0, 1)) if n == "w_in" else a

    def as_given(n, a):
        if n in gate_up:
            return jnp.swapaxes(a, 1, 2)
        if n == "w_in":
            return jnp.transpose(a.reshape(-1, nl, d), (1, 2, 0))
        return a.reshape(w[n].shape)

    stages = [[("meta_tokens", None)], [(n, 0) for n in ffn1_names], [(n, 0) for n in mix_names + ffn2_names]]
    stages += [[(n, l) for n in BIG] for l in range(1, nl)]
    gathers, wa = {}, {}

    halved = ffn1_names + ffn2_names + ("w_out",)

    def start_gathers():
        own = []
        for n, l in [item for items in stages for item in items]:
            a = meta_tokens if l is None else (stored(n, w[n])[:, l] if n == "w_in" else stored(n, w[n])[l])
            own.append(a if l is None else a.astype(BF16))
        kinds = ["gather_half" if n in halved else "gather" for items in stages for n, _ in items]
        return _exchange_start("gather_start", kinds, own, [_landing(a, me_xy, N_SHARD) for a in own])

    def arrive(si, after):
        first = sum(len(items) for items in stages[:si])
        part = _part_of(gathers, range(first, first + len(stages[si])))
        _, lands, _ = _exchange_wait(f"gather_wait_{si}", part, after)
        if any(kd == "gather_half" for kd in part["kinds"]):
            lands = _share_halves(f"gather_share_{si}", part["kinds"], lands)
        for item, a in zip(stages[si], lands):
            wa[item] = a.reshape(FFN_CHUNKS, -1, d) if item[0] in ffn1_names + ffn2_names else a

    def mixer_weights(l):
        wi = wa["w_in", l].reshape(-1, d)
        return dict(w_main=wi[:n_main], w_lr=jnp.pad(wi[n_main:], ((0, LANE - rank), (0, 0))),
                    wgu=jnp.pad(_from_col_shards(wa["w_gate_up", l]), ((0, LANE - rank), (0, 0))),
                    wout=wa["w_out", l].reshape(-1, d))

    wp16 = w_pool.astype(BF16)
    ones = jnp.ones((1, d), F32)
    zeros = jnp.zeros((1, d), F32)
    target = jnp.concatenate([jnp.zeros((N_META, d), F32), loss_target[0], jnp.zeros((t_pad - t_real, d), F32)], axis=0)

    gathers = start_gathers()
    arrive(0, gathers["token"])
    meta_full = _from_col_shards(wa["meta_tokens", None])
    h0 = jnp.concatenate([meta_full, xs, jnp.zeros((t_pad - t_real, d), F32)], axis=0)
    arrive(1, h0[:8, :LANE] + target[:8, :LANE])

    saved, mw = [], []
    cur, cur_g, cur_b = h0, ones, zeros
    for l in range(nl):
        s = {}
        xh1, rs1, hb0, g1, u1 = _ffn_fwd(cur, cur_g, cur_b, wa["ffn1_w_gate", l], wa["ffn1_w_up", l],
                                         wa["ffn1_w_down", l], alpha, f"ffn1_fwd_{l}")
        if l == 0:
            arrive(2, xh1)
        mw.append(mixer_weights(l))
        up, q, k, v, r, zg, la, hb1 = _inproj_fwd(xh1, ln1_g[l:l + 1], ln1_b[l:l + 1], mw[l]["w_main"], mw[l]["w_lr"],
                                                  mw[l]["wgu"], b_gate[l:l + 1], widths, f"inproj_fwd_{l}")
        yp, pb = _pool_fwd(up, wp16[l], pool_scale[l:l + 1], f"pool_fwd_{l}")
        o, yg, sall = _gla_fwd(q, k, v, la, r, gla_norm_g[l:l + 1], f"gla_fwd_{l}")
        xh2, rs2 = _outproj_fwd(yp, yg, mw[l]["wout"], xh1, ln1_g[l:l + 1], ln1_b[l:l + 1], alpha, f"outproj_fwd_{l}")
        if l + 1 < nl:
            arrive(l + 3, xh2)
        head = (ln3_g[l:l + 1], ln3_b[l:l + 1], target, s_len) if l == nl - 1 else None
        xh3, rs3, hb2, g2, u2, *at_head = _ffn_fwd(xh2, ln2_g[l:l + 1], ln2_b[l:l + 1], wa["ffn2_w_gate", l],
                                                   wa["ffn2_w_up", l], wa["ffn2_w_down", l], alpha, f"ffn2_fwd_{l}",
                                                   head)
        s.update(xh1=xh1, rs1=rs1, hb0=hb0, g1=g1, u1=u1, q=q, k=k, v=v, r=r, zg=zg, la=la, hb1=hb1, yp=yp, pb=pb,
                 o=o, yg=yg, sall=sall, xh2=xh2, rs2=rs2, xh3=xh3, rs3=rs3, hb2=hb2, g2=g2, u2=u2)
        saved.append(s)
        cur, cur_g, cur_b = xh3, ln3_g[l:l + 1], ln3_b[l:l + 1]

    dh, loss_acc = at_head
    loss = lax.psum(loss_acc[0, 0], ("x", "y", "c"))

    small_grads = {n: [None] * nl for n in SMALL}
    scatters = []

    def depart(name, items, grads, kinds=None):
        lands = [_landing(g, me_all, N_DEV) if kd == "bcast" else lax.empty(g.shape, g.dtype)
                 for g, kd in zip(grads, kinds or ["scatter"] * len(grads))]
        st = _exchange_start(name, kinds or ["scatter"] * len(grads), grads, lands)
        scatters.append((name, items, st))
        return st["token"]

    def pack(parts):
        flat = jnp.concatenate([parts[n].reshape(-1) for n in SMALL])
        return flat.reshape(-1, LANE)

    def ffn_wgrad(n, l, hb, dgb, dub, act, dfb, after=None):
        if n.endswith("down"):
            dw = _wgrad(act, dfb, dff_c, d, f"{n}_grad_{l}", after)
        else:
            dw = _wgrad(dgb if n.endswith("gate") else dub, hb, dff_c, d, f"{n}_grad_{l}", after)
        return dw.reshape(N_SHARD, dff_s, d)

    late = []
    for l in reversed(range(nl)):
        s = saved[l]
        dh, dfb, dgb, dub, act, dgam, dbet = _ffn_bwd(dh, s["xh3"], s["rs3"], ln3_g[l:l + 1], s["g2"], s["u2"],
                                                      wa["ffn2_w_gate", l], wa["ffn2_w_up", l], wa["ffn2_w_down", l],
                                                      alpha, f"ffn2_bwd_{l}")
        small_grads["ln3_g"][l], small_grads["ln3_b"][l] = dgam, dbet
        gone = depart(f"scatter_start_ffn2_{l}", [(n, l) for n in ffn2_names],
                      [ffn_wgrad(n, l, s["hb2"], dgb, dub, act, dfb) for n in ffn2_names])

        dyb, dyp, dyg, dres, dgam, dbet = _outproj_bwd(dh, s["xh2"], s["rs2"], ln2_g[l:l + 1] + gone[0:1, 0:1],
                                                       mw[l]["wout"], pw, alpha, f"outproj_bwd_{l}")
        small_grads["ln2_g"][l], small_grads["ln2_b"][l] = dgam, dbet
        dwo = jnp.concatenate([_wgrad(s["yp"], dyb, pw, d, f"dwout_pool_{l}"),
                               _wgrad(s["yg"], dyb, vw, d, f"dwout_gla_{l}")], axis=0)
        dq, dk, dv, dr, dzg, dwgu, dbg, dgn = _gla_bwd(dyg, s["o"], s["r"], gla_norm_g[l:l + 1], s["q"], s["k"],
                                                       s["v"], s["la"], s["zg"], s["sall"], mw[l]["wgu"],
                                                       f"gla_bwd_{l}")
        dup, dwp, dsc = _pool_bwd(dyp, s["pb"], wp16[l], pool_scale[l:l + 1], f"pool_bwd_{l}")
        small_grads["b_gate"][l], small_grads["gla_norm_g"][l] = dbg, dgn
        small_grads["w_pool"][l], small_grads["pool_scale"][l] = dwp, dsc
        dh, dz = _inproj_bwd(dres, [dup, dq, dk, dv, dr], dzg, mw[l]["w_main"], mw[l]["w_lr"], f"inproj_bwd_{l}")
        dwi = jnp.concatenate([_wgrad(dz, s["hb1"], 4 * LANE, d, f"dwin_main_{l}"),
                               _wgrad(dzg, s["hb1"], LANE, d, f"dwin_lr_{l}")[:rank]], axis=0)
        gone = depart(f"scatter_start_mix_{l}", [(n, l) for n in mix_names],
                      [dwo.reshape(N_SHARD, -1, d), _col_shards(dwgu[:rank]), dwi.reshape(N_SHARD, -1, d)])

        dh, dfb, dgb, dub, act, dgam, dbet = _ffn_bwd(dh, s["xh1"], s["rs1"], ln1_g[l:l + 1] + gone[0:1, 0:1],
                                                      s["g1"], s["u1"], wa["ffn1_w_gate", l], wa["ffn1_w_up", l],
                                                      wa["ffn1_w_down", l], alpha, f"ffn1_bwd_{l}")
        small_grads["ln1_g"][l], small_grads["ln1_b"][l] = dgam, dbet
        if l:
            gone = depart(f"scatter_start_ffn1_{l}", [(n, l) for n in ffn1_names],
                          [ffn_wgrad(n, l, s["hb0"], dgb, dub, act, dfb) for n in ffn1_names])
            ln3_g = ln3_g.at[l - 1:l].add(gone[0:1, 0:1])
            continue
        grad_x = dh[N_META:t_real][None]
        small_vec = pack({n: jnp.stack(small_grads[n]) for n in SMALL})
        gone = depart("scatter_start_rest", [("meta_tokens", 0), ("small", 0)],
                      [_col_shards(dh[:N_META]), small_vec], ["scatter", "bcast"])
        for n in ffn1_names:
            g = ffn_wgrad(n, l, s["hb0"], dgb, dub, act, dfb, after=gone)
            gone = depart(f"scatter_start_{n}", [(n, l)], [g])
            late.append(scatters.pop())

    sent, recv, results, firsts = {}, {}, {}, []
    my_slot = me_xy.reshape(1).astype(jnp.int32)

    def collect(group, after):
        for name, items, st in group:
            arrs, lands, _ = _exchange_wait(name.replace("start", "wait"), st, after)
            for item, a, b in zip(items, arrs, lands):
                sent[item], recv[item] = a, b

    def reduce_and_update(names, tag):
        partial = []
        for n in names:
            layers = [(n, l) for l in range(1 if n == "meta_tokens" else nl)]
            partial.append(_sum_slots(my_slot, [sent[it] for it in layers], [recv[it] for it in layers],
                                      f"sum_{n}", n == "w_in"))
        swap = _exchange_start(f"swap_start_{tag}", ["swap"] * len(names), partial,
                               [lax.empty(p.shape, p.dtype) for p in partial])
        after = swap["token"]
        for k, n in enumerate(names):
            (mine,), (theirs,), _ = _exchange_wait(f"swap_wait_{n}", _part_of(swap, [k]), after)
            fit = lambda a: stored(n, a).reshape(mine.shape)
            outs = _adamw(fit(w[n]), [mine, theirs], fit(mom1[n]), fit(mom2[n]), f"adamw_{n}")
            results[n] = [as_given(n, o) for o in outs]
            firsts.append(outs[1][0, 0, 0])
            after = outs[1]

    collect(scatters, gone)
    early = [n for n in ("meta_tokens",) + BIG if n not in ffn1_names]
    reduce_and_update(early, "early")
    small_terms = [recv["small", 0][i][None] for i in range(N_DEV)]
    souts = _adamw(pack(w)[None], small_terms, pack(mom1)[None], pack(mom2)[None], "adamw_small")
    off = 0
    for n in SMALL:
        size = w[n].size
        results[n] = [o.reshape(-1)[off:off + size].reshape(w[n].shape) for o in souts]
        off += size
    collect(late, souts[0][0, :8] + functools.reduce(jnp.add, firsts))
    reduce_and_update(ffn1_names, "late")

    out = [loss, grad_x]
    for part in range(4):
        out += [results[n][part] for n in WEIGHTS]
    return tuple(out)
```

```python
import functools

import jax
import jax.numpy as jnp
from jax import lax
from jax.experimental import pallas as pl
from jax.experimental.pallas import tpu as pltpu

F32 = jnp.float32
BF16 = jnp.bfloat16
MESH = pl.DeviceIdType.MESH

N_META = 16
POOL_WINDOWS = (2, 4, 8, 16)
POOL_HALO = 16
N_HEADS = 4
GLA_GATE_TEMP = 16.0
CHUNK = 128
CHUNK_UNROLL = 5
LN_EPS = 1e-5
RMS_EPS = 1e-6
ADAM_LR = 0.001
ADAM_B1 = 0.9
ADAM_B2 = 0.999
ADAM_EPS = 1e-08
ADAM_WD = 0.01
ADAM_STEP = 10
LANE = 128
BF16_ROWS = 16
ROW_TILE = 640
FFN_ROW_TILE = 640
FFN_CHUNKS = 4
FFN_SPLIT = 2
ROW_GROUPS = 2
WGRAD_K_MAX = 4224
N_SHARD = 4
N_DEV = 8

BIG = ("ffn1_w_gate", "ffn1_w_up", "ffn1_w_down", "w_in", "w_gate_up", "w_out",
       "ffn2_w_gate", "ffn2_w_up", "ffn2_w_down")
SMALL = ("ln1_g", "ln1_b", "b_gate", "w_pool", "pool_scale", "gla_norm_g", "ln2_g", "ln2_b", "ln3_g", "ln3_b")
WEIGHTS = ("meta_tokens", "ffn1_w_gate", "ffn1_w_up", "ffn1_w_down", "ln1_g", "ln1_b", "w_in", "w_gate_up",
           "b_gate", "w_pool", "pool_scale", "gla_norm_g", "w_out", "ln2_g", "ln2_b", "ffn2_w_gate",
           "ffn2_w_up", "ffn2_w_down", "ln3_g", "ln3_b")


def _tc_call(body, **kw):
    return pl.pallas_call(body, **kw)


def _comm_call(body, **kw):
    return pl.pallas_call(body, **kw)


def _seq(n):
    return pltpu.CompilerParams(dimension_semantics=("arbitrary",) * n)


def _mm(a, b):
    return jnp.dot(a.astype(BF16), b.astype(BF16), preferred_element_type=F32)


def _mm_nt(a, b):
    return lax.dot_general(a.astype(BF16), b.astype(BF16), (((1,), (1,)), ((), ())), preferred_element_type=F32)


def _mm_tn(a, b):
    return lax.dot_general(a.astype(BF16), b.astype(BF16), (((0,), (0,)), ((), ())), preferred_element_type=F32)


def _mm_01(a, b):
    hi = b.astype(BF16)
    lo = (b - hi.astype(F32)).astype(BF16)
    a = a.astype(BF16)
    return jnp.dot(a, hi, preferred_element_type=F32) + jnp.dot(a, lo, preferred_element_type=F32)


def _row_tile(t, most=None):
    tm = min(most or ROW_TILE, t)
    while t % tm:
        tm -= LANE
    return tm


def _silu_parts(g):
    sg = jax.nn.sigmoid(g)
    return sg, g * sg


def _ln_stats(r):
    mu = jnp.mean(r, axis=-1, keepdims=True)
    rc = r - mu
    var = jnp.mean(rc * rc, axis=-1, keepdims=True)
    rs = lax.rsqrt(var + LN_EPS)
    return rc * rs, rs


def _ln_bwd(dy, xh, rs, gam):
    dyg = dy * gam
    c1 = jnp.mean(dyg, axis=-1, keepdims=True)
    c2 = jnp.mean(dyg * xh, axis=-1, keepdims=True)
    return rs * (dyg - c1 - xh * c2)


def _ffn_fwd(xin, gam_in, bet_in, wg, wu, wd, alpha, name, loss_head=None):
    t, d = xin.shape
    nj, tf, _ = wg.shape
    tm = _row_tile(t, FFN_ROW_TILE)
    nt = t // tm
    share = tm // nj
    head = [] if loss_head is None else list(loss_head[:3])

    def body(x_ref, gi_ref, bi_ref, wg_ref, wu_ref, wd_ref, *rest):
        head_refs, rest = rest[:len(head)], rest[len(head):]
        xhat_ref, rstd_ref, hb_ref, go_ref, uo_ref = rest[:5]
        acc = rest[-1]
        i = pl.program_id(0)
        j = pl.program_id(1)
        cur = i % 2

        if head:
            @pl.when((i == 0) & (j == 0))
            def _():
                rest[6][...] = jnp.zeros_like(rest[6])

        def norm_previous():
            rows = pl.ds(pl.multiple_of(j * share, BF16_ROWS), share)
            xhat, rs = _ln_stats(0.5 * acc[1 - cur, rows, :])
            xhat_ref[rows, :] = xhat
            rstd_ref[rows, :] = rs
            if head:
                g_ref, b_ref, t_ref = head_refs
                dy_ref, loss_ref = rest[5], rest[6]
                rowi = (i - 1) * tm + j * share + lax.broadcasted_iota(jnp.int32, (share, 1), 0)
                live = (rowi >= N_META) & (rowi < N_META + loss_head[3])
                diff = jnp.where(live, xhat * g_ref[...] + b_ref[...] - t_ref[rows, :], 0.0)
                dy_ref[rows, :] = diff * (1.0 / d)
                loss_ref[...] += jnp.sum(diff * diff) * (0.5 / d)

        @pl.when(i < nt)
        def _():
            @pl.when(j == 0)
            def _():
                h = x_ref[...] * gi_ref[...] + bi_ref[...]
                hb_ref[...] = h.astype(BF16)
                acc[cur] = (2.0 * alpha) * h

                @pl.when(i == 0)
                def _():
                    acc[1] = jnp.zeros((tm, d), F32)

            norm_previous()
            hb = hb_ref[...]
            g = _mm_nt(hb, wg_ref[...])
            u = _mm_nt(hb, wu_ref[...])
            _, sl = _silu_parts(g)
            go_ref[...] = g.astype(BF16)
            uo_ref[...] = u.astype(BF16)
            acc[cur] += jnp.dot((sl * u).astype(BF16), wd_ref[...], preferred_element_type=F32)

        @pl.when(i == nt)
        def _():
            norm_previous()

    here = lambda i, j: (jnp.minimum(i, nt - 1), 0)
    before = lambda i, j: (jnp.maximum(i - 1, 0), 0)
    chunk = lambda i, j: (jnp.where(i < nt, j, nj - 1), 0, 0)
    col = pl.BlockSpec((None, tm, tf), lambda i, j: (jnp.where(i < nt, j, nj - 1), jnp.minimum(i, nt - 1), 0))
    vec = pl.BlockSpec((1, d), lambda i, j: (0, 0))
    head_in = [vec, vec, pl.BlockSpec((tm, d), before)] if head else []
    head_out = [pl.BlockSpec((tm, d), before), pl.BlockSpec((8, LANE), lambda i, j: (0, 0))] if head else []
    head_shape = [jax.ShapeDtypeStruct((t, d), F32), jax.ShapeDtypeStruct((8, LANE), F32)] if head else []
    return _tc_call(
        body, name=name, grid=(nt + 1, nj),
        in_specs=[pl.BlockSpec((tm, d), here), vec, vec] + [pl.BlockSpec((None, tf, d), chunk)] * 3 + head_in,
        out_specs=[pl.BlockSpec((tm, d), before), pl.BlockSpec((tm, 1), before), pl.BlockSpec((tm, d), here), col, col]
                  + head_out,
        out_shape=[jax.ShapeDtypeStruct((t, d), F32), jax.ShapeDtypeStruct((t, 1), F32),
                   jax.ShapeDtypeStruct((t, d), BF16), jax.ShapeDtypeStruct((nj, t, tf), BF16),
                   jax.ShapeDtypeStruct((nj, t, tf), BF16)] + head_shape,
        scratch_shapes=[pltpu.VMEM((2, tm, d), F32)],
        compiler_params=_seq(2),
    )(xin, gam_in, bet_in, wg, wu, wd, *head)


def _ffn_bwd(dh, xhat, rstd, ln_g, gb, ub, wg, wu, wd, alpha, name):
    t, d = dh.shape
    nj, tf, _ = wg.shape
    tm = _row_tile(t, FFN_ROW_TILE)
    nt = t // tm
    share = tm // nj

    def body(dh_ref, xh_ref, rs_ref, g_ref, gb_ref, ub_ref, wg_ref, wu_ref, wd_ref,
             dhin_ref, df_ref, dg_ref, du_ref, act_ref, dgam_ref, dbet_ref, df_s, dres_next, df_next):
        i = pl.program_id(0)
        j = pl.program_id(1)

        @pl.when((i == 0) & (j == 0))
        def _():
            dgam_ref[...] = jnp.zeros_like(dgam_ref)
            dbet_ref[...] = jnp.zeros_like(dbet_ref)

        def look_ahead():
            rows = pl.ds(pl.multiple_of(j * share, BF16_ROWS), share)
            dy = dh_ref[rows, :]
            xh = xh_ref[rows, :]
            dr = _ln_bwd(dy, xh, rs_ref[rows, :], g_ref[...])
            dres_next[rows, :] = alpha * dr
            df_next[rows, :] = (0.5 * dr).astype(BF16)
            live = jnp.where(i < nt, 1.0, 0.0)
            dgam_ref[...] += live * jnp.sum(dy * xh, axis=0, keepdims=True)
            dbet_ref[...] += live * jnp.sum(dy, axis=0, keepdims=True)

        @pl.when(i == 0)
        def _():
            look_ahead()

        @pl.when(i > 0)
        def _():
            @pl.when(j == 0)
            def _():
                dhin_ref[...] = dres_next[...]
                dfb = df_next[...]
                df_s[...] = dfb
                df_ref[...] = dfb

            look_ahead()
            for part in range(FFN_SPLIT):
                rows = pl.ds(part * (tm // FFN_SPLIT), tm // FFN_SPLIT)
                dact = _mm_nt(df_s[rows, :], wd_ref[...])
                g = gb_ref[rows, :].astype(F32)
                u = ub_ref[rows, :].astype(F32)
                sg, sl = _silu_parts(g)
                dg = (dact * u * (sg + sl * (1.0 - sg))).astype(BF16)
                du = (dact * sl).astype(BF16)
                dg_ref[rows, :] = dg
                du_ref[rows, :] = du
                act_ref[rows, :] = (sl * u).astype(BF16)
                dhin_ref[rows, :] += _mm(dg, wg_ref[...]) + _mm(du, wu_ref[...])

    ahead = lambda i, j: (jnp.minimum(i, nt - 1), 0)
    row = lambda i, j: (jnp.maximum(i - 1, 0), 0)
    chunk = lambda i, j: (jnp.where(i > 0, j, 0), 0, 0)
    col = pl.BlockSpec((None, tm, tf), lambda i, j: (jnp.where(i > 0, j, 0), jnp.maximum(i - 1, 0), 0))
    vec = pl.BlockSpec((1, d), lambda i, j: (0, 0))
    ff = jax.ShapeDtypeStruct((nj, t, tf), BF16)
    return _tc_call(
        body, name=name, grid=(nt + 1, nj),
        in_specs=[pl.BlockSpec((tm, d), ahead), pl.BlockSpec((tm, d), ahead), pl.BlockSpec((tm, 1), ahead), vec,
                  col, col] + [pl.BlockSpec((None, tf, d), chunk)] * 3,
        out_specs=[pl.BlockSpec((tm, d), row), pl.BlockSpec((tm, d), row), col, col, col, vec, vec],
        out_shape=[jax.ShapeDtypeStruct((t, d), F32), jax.ShapeDtypeStruct((t, d), BF16), ff, ff, ff,
                   jax.ShapeDtypeStruct((1, d), F32), jax.ShapeDtypeStruct((1, d), F32)],
        scratch_shapes=[pltpu.VMEM((tm, d), BF16), pltpu.VMEM((tm, d), F32), pltpu.VMEM((tm, d), BF16)],
        compiler_params=_seq(2),
    )(dh, xhat, rstd, ln_g, gb, ub, wg, wu, wd)


def _wgrad(a, b, tmm, tn, name, after=None):
    t = a.shape[-2]
    m = a.shape[-1] * (a.shape[0] if a.ndim == 3 else 1)
    n = b.shape[-1] * (b.shape[0] if b.ndim == 3 else 1)
    tk = max(k for k in range(BF16_ROWS, WGRAD_K_MAX + 1, BF16_ROWS) if t % k == 0)
    nk = t // tk
    extra = [] if after is None else [after]

    def body(a_ref, b_ref, *rest):
        o_ref, acc = rest[len(extra):]
        k = pl.program_id(2)

        @pl.when(k == 0)
        def _():
            acc[...] = jnp.zeros_like(acc)

        acc[...] += _mm_tn(a_ref[...], b_ref[...])

        @pl.when(k == nk - 1)
        def _():
            o_ref[...] = acc[...].astype(o_ref.dtype)

    a_spec = (pl.BlockSpec((None, tk, tmm), lambda i, j, k: (i, k, 0)) if a.ndim == 3
              else pl.BlockSpec((tk, tmm), lambda i, j, k: (k, i)))
    return _tc_call(
        body, name=name, grid=(m // tmm, n // tn, nk),
        in_specs=[a_spec, pl.BlockSpec((None, tk, tn), lambda i, j, k: (j, k, 0)) if b.ndim == 3
                  else pl.BlockSpec((tk, tn), lambda i, j, k: (k, j))] + [pl.BlockSpec(memory_space=pl.ANY)] * len(extra),
        out_specs=pl.BlockSpec((tmm, tn), lambda i, j, k: (i, j)),
        out_shape=jax.ShapeDtypeStruct((m, n), BF16),
        scratch_shapes=[pltpu.VMEM((tmm, tn), F32)],
        compiler_params=_seq(3),
    )(a, b, *extra)


def _inproj_fwd(xhat, gam, bet, w_main, w_lr, wgu, b_gate, widths, name):
    t, d = xhat.shape
    tm = _row_tile(t)
    kw = wgu.shape[1]
    offs = [0]
    for w in widths:
        offs.append(offs[-1] + w)

    def body(x_ref, g_ref, b_ref, wm_ref, wl_ref, wgu_ref, bg_ref, *outs):
        piece_refs, (zg_ref, la_ref, hb_ref) = outs[:len(widths)], outs[len(widths):]
        hb = (x_ref[...] * g_ref[...] + b_ref[...]).astype(BF16)
        hb_ref[...] = hb
        for p, ref in enumerate(piece_refs):
            ref[...] = _mm_nt(hb, wm_ref[offs[p]:offs[p + 1], :]).astype(ref.dtype)
        zg = _mm_nt(hb, wl_ref[...])
        zg_ref[...] = zg
        logit = _mm(zg, wgu_ref[...]) + bg_ref[...]
        la_ref[...] = (jnp.minimum(logit, 0.0) - jnp.log(1.0 + jnp.exp(-jnp.abs(logit)))) * (1.0 / GLA_GATE_TEMP)

    row = lambda i: (i, 0)
    full = lambda a: pl.BlockSpec(a.shape, lambda i: (0,) * a.ndim)
    out_w = list(widths) + [LANE, kw]
    out_t = [F32, F32, F32, BF16, F32, F32, F32]
    return _tc_call(
        body, name=name, grid=(t // tm,),
        in_specs=[pl.BlockSpec((tm, d), row), full(gam), full(bet), full(w_main), full(w_lr), full(wgu), full(b_gate)],
        out_specs=[pl.BlockSpec((tm, w), row) for w in out_w] + [pl.BlockSpec((tm, d), row)],
        out_shape=[jax.ShapeDtypeStruct((t, w), ty) for w, ty in zip(out_w, out_t)]
                  + [jax.ShapeDtypeStruct((t, d), BF16)],
        compiler_params=_seq(1),
    )(xhat, gam, bet, w_main, w_lr, wgu, b_gate)


def _inproj_bwd(dh_part, pieces, dzg, w_main, w_lr, name):
    t, d = dh_part.shape
    tm = _row_tile(t)
    widths = [p.shape[1] for p in pieces]
    offs = [0]
    for w in widths:
        offs.append(offs[-1] + w)

    def body(*refs):
        dhp_ref = refs[0]
        p_refs = refs[1:1 + len(widths)]
        dzg_ref, wm_ref, wl_ref, dh_ref, dz_ref = refs[1 + len(widths):]
        acc = dhp_ref[...] + _mm(dzg_ref[...], wl_ref[...])
        for p, ref in enumerate(p_refs):
            v = ref[...]
            dz_ref[:, offs[p]:offs[p + 1]] = v
            acc += _mm(v, wm_ref[offs[p]:offs[p + 1], :])
        dh_ref[...] = acc

    row = lambda i: (i, 0)
    full = lambda a: pl.BlockSpec(a.shape, lambda i: (0,) * a.ndim)
    return _tc_call(
        body, name=name, grid=(t // tm,),
        in_specs=[pl.BlockSpec((tm, d), row)] + [pl.BlockSpec((tm, w), row) for w in widths]
                 + [pl.BlockSpec((tm, LANE), row), full(w_main), full(w_lr)],
        out_specs=[pl.BlockSpec((tm, d), row), pl.BlockSpec((tm, offs[-1]), row)],
        out_shape=[jax.ShapeDtypeStruct((t, d), F32), jax.ShapeDtypeStruct((t, offs[-1]), BF16)],
        compiler_params=_seq(1),
    )(dh_part, *pieces, dzg, w_main, w_lr)


def _pool_cnt(tile, tm, w):
    t = tile * tm + lax.broadcasted_iota(jnp.int32, (tm, 1), 0)
    return jnp.minimum(t + 1, w).astype(F32)


def _pool_fwd(u, wp, scale, name):
    t, pw = u.shape
    tm = _row_tile(t)
    gd = wp.shape[1]

    def body(u_ref, wp_ref, sc_ref, y_ref, p_ref, ext):
        i = pl.program_id(0)

        @pl.when(i == 0)
        def _():
            ext[0:POOL_HALO, :] = jnp.zeros((POOL_HALO, pw), F32)

        ext[POOL_HALO:POOL_HALO + tm, :] = u_ref[...]
        for gi, w in enumerate(POOL_WINDOWS):
            cols = slice(gi * gd, (gi + 1) * gd)
            s = ext[pl.ds(POOL_HALO, tm), cols]
            tot = s
            for back in range(1, w):
                tot = tot + ext[pl.ds(POOL_HALO - back, tm), cols]
            p = (tot / _pool_cnt(i, tm, w) - s).astype(BF16)
            p_ref[:, cols] = p
            y_ref[:, cols] = (jnp.dot(p, wp_ref[gi], preferred_element_type=F32) * sc_ref[:, cols]).astype(BF16)
        ext[0:POOL_HALO, :] = ext[tm:tm + POOL_HALO, :]

    row = lambda i: (i, 0)
    return _tc_call(
        body, name=name, grid=(t // tm,),
        in_specs=[pl.BlockSpec((tm, pw), row), pl.BlockSpec(wp.shape, lambda i: (0, 0, 0)),
                  pl.BlockSpec((1, pw), lambda i: (0, 0))],
        out_specs=[pl.BlockSpec((tm, pw), row), pl.BlockSpec((tm, pw), row)],
        out_shape=[jax.ShapeDtypeStruct((t, pw), BF16), jax.ShapeDtypeStruct((t, pw), BF16)],
        scratch_shapes=[pltpu.VMEM((tm + POOL_HALO, pw), F32)],
        compiler_params=_seq(1),
    )(u, wp, scale)


def _pool_bwd(dy, pb, wp, scale, name):
    t, pw = dy.shape
    tm = _row_tile(t)
    nt = t // tm
    gd = wp.shape[1]

    def body(dy_ref, p_ref, wp_ref, sc_ref, du_ref, dwp_ref, dsc_ref, ext):
        i = pl.program_id(0)
        tile = nt - 1 - i

        @pl.when(i == 0)
        def _():
            ext[tm:tm + POOL_HALO, :] = jnp.zeros((POOL_HALO, pw), F32)
            dwp_ref[...] = jnp.zeros_like(dwp_ref)
            dsc_ref[...] = jnp.zeros_like(dsc_ref)

        dps = []
        for gi, w in enumerate(POOL_WINDOWS):
            cols = slice(gi * gd, (gi + 1) * gd)
            dyv = dy_ref[:, cols]
            p = p_ref[:, cols]
            dpre = (dyv * sc_ref[:, cols]).astype(BF16)
            dsc_ref[:, cols] += jnp.sum(dyv * jnp.dot(p, wp_ref[gi], preferred_element_type=F32), axis=0, keepdims=True)
            dwp_ref[gi] += _mm_tn(p, dpre)
            dp = _mm_nt(dpre, wp_ref[gi])
            dps.append(dp)
            ext[0:tm, cols] = dp / _pool_cnt(tile, tm, w)
        for gi, w in enumerate(POOL_WINDOWS):
            cols = slice(gi * gd, (gi + 1) * gd)
            tot = ext[pl.ds(0, tm), cols]
            for fwd in range(1, w):
                tot = tot + ext[pl.ds(fwd, tm), cols]
            du_ref[:, cols] = (tot - dps[gi]).astype(BF16)
        ext[tm:tm + POOL_HALO, :] = ext[0:POOL_HALO, :]

    row = lambda i: (nt - 1 - i, 0)
    return _tc_call(
        body, name=name, grid=(nt,),
        in_specs=[pl.BlockSpec((tm, pw), row), pl.BlockSpec((tm, pw), row),
                  pl.BlockSpec(wp.shape, lambda i: (0, 0, 0)), pl.BlockSpec((1, pw), lambda i: (0, 0))],
        out_specs=[pl.BlockSpec((tm, pw), row), pl.BlockSpec(wp.shape, lambda i: (0, 0, 0)),
                   pl.BlockSpec((1, pw), lambda i: (0, 0))],
        out_shape=[jax.ShapeDtypeStruct((t, pw), BF16), jax.ShapeDtypeStruct(wp.shape, F32),
                   jax.ShapeDtypeStruct((1, pw), F32)],
        scratch_shapes=[pltpu.VMEM((tm + POOL_HALO, pw), F32)],
        compiler_params=_seq(1),
    )(dy, pb, wp, scale)


def _gla_masks(kw, vw):
    dk, dv = kw // N_HEADS, vw // N_HEADS
    lane_k = lax.broadcasted_iota(jnp.int32, (1, kw), 1)
    lane_v = lax.broadcasted_iota(jnp.int32, (1, vw), 1)
    hk = [((lane_k >= h * dk) & (lane_k < (h + 1) * dk)).astype(F32) for h in range(N_HEADS)]
    hv = [((lane_v >= h * dv) & (lane_v < (h + 1) * dv)).astype(F32) for h in range(N_HEADS)]
    r = lax.broadcasted_iota(jnp.int32, (CHUNK, CHUNK), 0)
    c = lax.broadcasted_iota(jnp.int32, (CHUNK, CHUNK), 1)
    tril = r >= c
    rs = lax.broadcasted_iota(jnp.int32, (N_HEADS * CHUNK, CHUNK), 0) & (CHUNK - 1)
    stril = rs >= lax.broadcasted_iota(jnp.int32, (N_HEADS * CHUNK, CHUNK), 1)
    return hk, hv, tril, stril


def _block_diag(x, hk, dv):
    return jnp.concatenate([x[h * dv:(h + 1) * dv, :] * hk[h] for h in range(N_HEADS)], axis=0)


def _gla_fwd(q, k, v, loga, r, gnorm, name):
    t, kw = q.shape
    vw = v.shape[1]
    dk, dv = kw // N_HEADS, vw // N_HEADS
    tm = _row_tile(t)
    nc = tm // CHUNK
    qscale = dk ** -0.5

    def body(q_ref, k_ref, v_ref, la_ref, r_ref, gn_ref, o_ref, y_ref, sall_ref, st):
        @pl.when(pl.program_id(0) == 0)
        def _():
            st[...] = jnp.zeros_like(st)

        hk, hv, tril, stril = _gla_masks(kw, vw)
        trif = tril.astype(F32)

        def chunk(c, carry):
            rows = pl.ds(pl.multiple_of(c * CHUNK, CHUNK), CHUNK)
            la = la_ref[rows, :]
            b = _mm_01(trif, la)
            bl = jnp.sum(la, axis=0, keepdims=True)
            qb = q_ref[rows, :] * (qscale * jnp.exp(b))
            kk = k_ref[rows, :]
            kb = kk * jnp.exp(-b)
            kl = kk * jnp.exp(bl - b)
            vv = v_ref[rows, :]
            s_t = st[...]
            compact = s_t[0:dv, :]
            for h in range(1, N_HEADS):
                compact = compact + s_t[h * dv:(h + 1) * dv, :]
            sall_ref[c] = compact
            qx = jnp.concatenate([qb.astype(BF16) * hk[h].astype(BF16) for h in range(N_HEADS)], axis=0)
            a = jnp.where(stril, _mm_nt(qx, kb), 0.0).astype(BF16)
            o_inter = _mm_nt(qb, s_t)
            for h in range(N_HEADS):
                vs = slice(h * dv, (h + 1) * dv)
                o_ref[rows, vs] = o_inter[:, vs] + _mm(a[h * CHUNK:(h + 1) * CHUNK, :], vv[:, vs])
            st[...] = s_t * jnp.exp(bl) + _block_diag(_mm_tn(vv, kl), hk, dv)
            return carry

        lax.fori_loop(0, nc, chunk, 0, unroll=CHUNK_UNROLL)
        for h in range(N_HEADS):
            vs = slice(h * dv, (h + 1) * dv)
            oh = o_ref[:, vs]
            on = oh * lax.rsqrt(jnp.mean(oh * oh, axis=-1, keepdims=True) + RMS_EPS)
            _, sl = _silu_parts(r_ref[:, vs])
            y_ref[:, vs] = (on * gn_ref[:, vs] * sl).astype(BF16)

    row = lambda i: (i, 0)
    return _tc_call(
        body, name=name, grid=(t // tm,),
        in_specs=[pl.BlockSpec((tm, kw), row), pl.BlockSpec((tm, kw), row), pl.BlockSpec((tm, vw), row),
                  pl.BlockSpec((tm, kw), row), pl.BlockSpec((tm, vw), row), pl.BlockSpec((1, vw), lambda i: (0, 0))],
        out_specs=[pl.BlockSpec((tm, vw), row), pl.BlockSpec((tm, vw), row),
                   pl.BlockSpec((nc, dv, kw), lambda i: (i, 0, 0))],
        out_shape=[jax.ShapeDtypeStruct((t, vw), F32), jax.ShapeDtypeStruct((t, vw), BF16),
                   jax.ShapeDtypeStruct((t // CHUNK, dv, kw), F32)],
        scratch_shapes=[pltpu.VMEM((vw, kw), F32)],
        compiler_params=_seq(1),
    )(q, k, v, loga, r, gnorm)


def _gla_bwd(dy, o, r, gnorm, q, k, v, loga, zg, sall, wgu, name):
    t, kw = q.shape
    vw = v.shape[1]
    dk, dv = kw // N_HEADS, vw // N_HEADS
    tm = _row_tile(t)
    nt = t // tm
    nc = tm // CHUNK
    qscale = dk ** -0.5

    def body(dy_ref, o_ref, r_ref, gn_ref, q_ref, k_ref, v_ref, la_ref, zg_ref, sall_ref, wgu_ref,
             dq_ref, dk_ref, dv_ref, dr_ref, dzg_ref, dwgu_ref, dbg_ref, dgn_ref, dst, do_s):
        @pl.when(pl.program_id(0) == 0)
        def _():
            dst[...] = jnp.zeros_like(dst)
            dwgu_ref[...] = jnp.zeros_like(dwgu_ref)
            dbg_ref[...] = jnp.zeros_like(dbg_ref)
            dgn_ref[...] = jnp.zeros_like(dgn_ref)

        for h in range(N_HEADS):
            vs = slice(h * dv, (h + 1) * dv)
            oh = o_ref[:, vs]
            rinv = lax.rsqrt(jnp.mean(oh * oh, axis=-1, keepdims=True) + RMS_EPS)
            on = oh * rinv
            rr = r_ref[:, vs]
            sg, sl = _silu_parts(rr)
            dyv = dy_ref[:, vs]
            gn = gn_ref[:, vs]
            dgn_ref[:, vs] += jnp.sum(dyv * on * sl, axis=0, keepdims=True)
            dr_ref[:, vs] = (dyv * on * gn * (sg + sl * (1.0 - sg))).astype(BF16)
            don = dyv * gn * sl
            do_s[:, vs] = rinv * (don - on * jnp.mean(don * on, axis=-1, keepdims=True))

        hk, hv, tril, stril = _gla_masks(kw, vw)
        trif = tril.astype(F32)
        triuf = (lax.broadcasted_iota(jnp.int32, (CHUNK, CHUNK), 0)
                 <= lax.broadcasted_iota(jnp.int32, (CHUNK, CHUNK), 1)).astype(F32)
        last_row = lax.broadcasted_iota(jnp.int32, (CHUNK, 1), 0) == CHUNK - 1

        def chunk(idx, carry):
            c = nc - 1 - idx
            rows = pl.ds(pl.multiple_of(c * CHUNK, CHUNK), CHUNK)
            la = la_ref[rows, :]
            b = _mm_01(trif, la)
            bl = jnp.sum(la, axis=0, keepdims=True)
            eb = jnp.exp(b)
            enb = jnp.exp(-b)
            ebl = jnp.exp(bl - b)
            el = jnp.exp(bl)
            qb = q_ref[rows, :] * (qscale * eb)
            kk = k_ref[rows, :]
            kb = kk * enb
            kl = kk * ebl
            vv = v_ref[rows, :]
            do = do_s[rows, :]
            compact = sall_ref[c]
            s_t = jnp.concatenate([compact * hk[h] for h in range(N_HEADS)], axis=0)
            ds_t = dst[...]
            qx = jnp.concatenate([qb.astype(BF16) * hk[h].astype(BF16) for h in range(N_HEADS)], axis=0)
            dox = jnp.concatenate([do.astype(BF16) * hv[h].astype(BF16) for h in range(N_HEADS)], axis=0)
            a = jnp.where(stril, _mm_nt(qx, kb), 0.0).astype(BF16)
            da = jnp.where(stril, _mm_nt(dox, vv), 0.0).astype(BF16)
            dv_ref[rows, :] = (_mm_tn(a, dox) + _mm_nt(kl, ds_t)).astype(BF16)
            dak = _mm(da, kb)
            dqb = _mm(do, s_t)
            for h in range(N_HEADS):
                dqb = dqb + dak[h * CHUNK:(h + 1) * CHUNK, :] * hk[h]
            dkb = _mm_tn(da, qx)
            dkl = _mm(vv, ds_t)
            dbl = jnp.sum(dkl * kl, axis=0, keepdims=True) + el * jnp.sum(ds_t * s_t, axis=0, keepdims=True)
            dst[...] = ds_t * el + _block_diag(_mm_tn(do, qb), hk, dv)
            dq_ref[rows, :] = (dqb * (qscale * eb)).astype(BF16)
            dk_ref[rows, :] = (dkb * enb + dkl * ebl).astype(BF16)
            db = dqb * qb - dkb * kb - dkl * kl + jnp.where(last_row, dbl, 0.0)
            dla = _mm_01(triuf, db)
            dlogit = dla * (1.0 / GLA_GATE_TEMP) * (1.0 - jnp.exp(GLA_GATE_TEMP * la))
            dzg_ref[rows, :] = _mm_nt(dlogit, wgu_ref[...]).astype(BF16)
            dwgu_ref[...] += _mm_tn(zg_ref[rows, :], dlogit)
            dbg_ref[...] += jnp.sum(dlogit, axis=0, keepdims=True)
            return carry

        lax.fori_loop(0, nc, chunk, 0, unroll=CHUNK_UNROLL)

    row = lambda i: (nt - 1 - i, 0)
    const = lambda i: (0, 0)
    return _tc_call(
        body, name=name, grid=(nt,),
        in_specs=[pl.BlockSpec((tm, vw), row), pl.BlockSpec((tm, vw), row), pl.BlockSpec((tm, vw), row),
                  pl.BlockSpec((1, vw), const), pl.BlockSpec((tm, kw), row), pl.BlockSpec((tm, kw), row),
                  pl.BlockSpec((tm, vw), row), pl.BlockSpec((tm, kw), row), pl.BlockSpec((tm, LANE), row),
                  pl.BlockSpec((nc, dv, kw), lambda i: (nt - 1 - i, 0, 0)), pl.BlockSpec((LANE, kw), const)],
        out_specs=[pl.BlockSpec((tm, kw), row), pl.BlockSpec((tm, kw), row), pl.BlockSpec((tm, vw), row),
                   pl.BlockSpec((tm, vw), row), pl.BlockSpec((tm, LANE), row), pl.BlockSpec((LANE, kw), const),
                   pl.BlockSpec((1, kw), const), pl.BlockSpec((1, vw), const)],
        out_shape=[jax.ShapeDtypeStruct((t, kw), BF16), jax.ShapeDtypeStruct((t, kw), BF16),
                   jax.ShapeDtypeStruct((t, vw), BF16), jax.ShapeDtypeStruct((t, vw), BF16),
                   jax.ShapeDtypeStruct((t, LANE), BF16), jax.ShapeDtypeStruct((LANE, kw), F32),
                   jax.ShapeDtypeStruct((1, kw), F32), jax.ShapeDtypeStruct((1, vw), F32)],
        scratch_shapes=[pltpu.VMEM((vw, kw), F32), pltpu.VMEM((tm, vw), F32)],
        compiler_params=_seq(1),
    )(dy, o, r, gnorm, q, k, v, loga, zg, sall, wgu)


def _outproj_fwd(yp, yg, w_out, xhat, gam, bet, alpha, name):
    t, d = xhat.shape
    pw = yp.shape[1]
    tm = _row_tile(t)

    def body(yp_ref, yg_ref, w_ref, x_ref, g_ref, b_ref, xhat_ref, rstd_ref):
        for part in range(ROW_GROUPS):
            rows = pl.ds(part * (tm // ROW_GROUPS), tm // ROW_GROUPS)
            h = x_ref[rows, :] * g_ref[...] + b_ref[...]
            y = (jnp.dot(yp_ref[rows, :], w_ref[0:pw, :], preferred_element_type=F32)
                 + jnp.dot(yg_ref[rows, :], w_ref[pw:, :], preferred_element_type=F32))
            xh, rs = _ln_stats(alpha * h + y)
            xhat_ref[rows, :] = xh
            rstd_ref[rows, :] = rs

    row = lambda i: (i, 0)
    vec = pl.BlockSpec((1, d), lambda i: (0, 0))
    return _tc_call(
        body, name=name, grid=(t // tm,),
        in_specs=[pl.BlockSpec((tm, pw), row), pl.BlockSpec((tm, yg.shape[1]), row),
                  pl.BlockSpec(w_out.shape, lambda i: (0, 0)), pl.BlockSpec((tm, d), row), vec, vec],
        out_specs=[pl.BlockSpec((tm, d), row), pl.BlockSpec((tm, 1), row)],
        out_shape=[jax.ShapeDtypeStruct((t, d), F32), jax.ShapeDtypeStruct((t, 1), F32)],
        compiler_params=_seq(1),
    )(yp, yg, w_out, xhat, gam, bet)


def _outproj_bwd(dh, xhat, rstd, ln_g, w_out, pw, alpha, name):
    t, d = dh.shape
    tm = _row_tile(t)
    gw = w_out.shape[0] - pw

    def body(dh_ref, xh_ref, rs_ref, g_ref, w_ref, dyb_ref, dyp_ref, dyg_ref, dres_ref, dgam_ref, dbet_ref):
        @pl.when(pl.program_id(0) == 0)
        def _():
            dgam_ref[...] = jnp.zeros_like(dgam_ref)
            dbet_ref[...] = jnp.zeros_like(dbet_ref)

        for part in range(ROW_GROUPS):
            rows = pl.ds(part * (tm // ROW_GROUPS), tm // ROW_GROUPS)
            dy = dh_ref[rows, :]
            xh = xh_ref[rows, :]
            dr = _ln_bwd(dy, xh, rs_ref[rows, :], g_ref[...])
            dgam_ref[...] += jnp.sum(dy * xh, axis=0, keepdims=True)
            dbet_ref[...] += jnp.sum(dy, axis=0, keepdims=True)
            drb = dr.astype(BF16)
            dyb_ref[rows, :] = drb
            dres_ref[rows, :] = alpha * dr
            dyp_ref[rows, :] = _mm_nt(drb, w_ref[0:pw, :])
            dyg_ref[rows, :] = _mm_nt(drb, w_ref[pw:, :])

    row = lambda i: (i, 0)
    vec = pl.BlockSpec((1, d), lambda i: (0, 0))
    return _tc_call(
        body, name=name, grid=(t // tm,),
        in_specs=[pl.BlockSpec((tm, d), row), pl.BlockSpec((tm, d), row), pl.BlockSpec((tm, 1), row), vec,
                  pl.BlockSpec(w_out.shape, lambda i: (0, 0))],
        out_specs=[pl.BlockSpec((tm, d), row), pl.BlockSpec((tm, pw), row), pl.BlockSpec((tm, gw), row),
                   pl.BlockSpec((tm, d), row), vec, vec],
        out_shape=[jax.ShapeDtypeStruct((t, d), BF16), jax.ShapeDtypeStruct((t, pw), F32),
                   jax.ShapeDtypeStruct((t, gw), F32), jax.ShapeDtypeStruct((t, d), F32),
                   jax.ShapeDtypeStruct((1, d), F32), jax.ShapeDtypeStruct((1, d), F32)],
        compiler_params=_seq(1),
    )(dh, xhat, rstd, ln_g, w_out)


def _rows_block(r, c):
    best = r
    for cand in range(BF16_ROWS, r, BF16_ROWS):
        if r % cand == 0 and cand * c * 4 <= (1 << 20):
            best = cand
    return best if best * c * 4 <= (4 << 20) else r


def _sum_slots(me, mine, recvs, name, layers_side_by_side=False):
    nl = len(recvs)
    ns, r, c = recvs[0].shape
    tr = _rows_block(r, c)

    def body(me_ref, *refs):
        o_ref = refs[nl * ns]
        for l in range(nl):
            acc = refs[l * ns][...].astype(F32)
            for s in range(1, ns):
                acc = acc + refs[l * ns + s][...].astype(F32)
            if layers_side_by_side:
                o_ref[0, :, l * c:(l + 1) * c] = acc.astype(o_ref.dtype)
            else:
                o_ref[l] = acc.astype(o_ref.dtype)

    def slot(s):
        return pl.BlockSpec((None, tr, c), lambda i, me_ref: ((me_ref[0] + s) % ns, i, 0))

    out = (1, r, nl * c) if layers_side_by_side else (nl, r, c)
    operands = []
    for l in range(nl):
        operands += [mine[l]] + [recvs[l]] * (ns - 1)
    return _tc_call(
        body, name=name,
        grid_spec=pltpu.PrefetchScalarGridSpec(
            num_scalar_prefetch=1, grid=(r // tr,),
            in_specs=[slot(s) for s in range(ns)] * nl,
            out_specs=pl.BlockSpec((out[0], tr, out[2]), lambda i, me_ref: (0, i, 0))),
        out_shape=jax.ShapeDtypeStruct(out, recvs[0].dtype),
        compiler_params=_seq(1),
    )(me, *operands)


def _adamw(w, terms, m, v, name):
    nl, r, c = w.shape
    tc = c
    while tc % (2 * LANE) == 0 and tc > 4 * LANE:
        tc //= 2
    tr = _rows_block(r, tc)
    nterm = len(terms)

    def body(*refs):
        w_ref = refs[0]
        t_refs = refs[1:1 + nterm]
        m_ref, v_ref, g_ref, d_ref, nm_ref, nv_ref = refs[1 + nterm:]
        g = t_refs[0][...].astype(F32)
        for tr_ in t_refs[1:]:
            g = g + tr_[...].astype(F32)
        nm = ADAM_B1 * m_ref[...] + (1.0 - ADAM_B1) * g
        nv = ADAM_B2 * v_ref[...] + (1.0 - ADAM_B2) * jnp.square(g)
        m_hat = nm / (1.0 - ADAM_B1 ** ADAM_STEP)
        v_hat = nv / (1.0 - ADAM_B2 ** ADAM_STEP)
        g_ref[...] = g
        d_ref[...] = -ADAM_LR * (m_hat / (jnp.sqrt(v_hat) + ADAM_EPS) + ADAM_WD * w_ref[...])
        nm_ref[...] = nm
        nv_ref[...] = nv

    spec = pl.BlockSpec((None, tr, tc), lambda l, i, j: (l, i, j))
    shp = jax.ShapeDtypeStruct((nl, r, c), F32)
    return _tc_call(
        body, name=name, grid=(nl, r // tr, c // tc),
        in_specs=[spec] * (3 + nterm), out_specs=[spec] * 4, out_shape=[shp] * 4,
        compiler_params=_seq(3),
    )(w, *terms, m, v)


XY_RELATIONS = ((1, 0, 0), (0, 1, 0), (1, 1, 0))
ALL_RELATIONS = tuple((fx, fy, fc) for fx in (0, 1) for fy in (0, 1) for fc in (0, 1) if fx or fy or fc)
SIBLING = ((0, 0, 1),)
HBM_SPEC = pl.BlockSpec(memory_space=pltpu.HBM)
SEM_SPEC = pl.BlockSpec(memory_space=pltpu.SEMAPHORE)
DATAFLOW = pltpu.SideEffectType.DATAFLOW_SIDE_EFFECTING


def _split_call(body, **kw):
    return pl.pallas_call(body, **kw)


def _flip(v, f):
    return 1 - v if f else v


def _any_spec(n):
    return [pl.BlockSpec(memory_space=pl.ANY)] * n


def _relations(kind):
    return {"bcast": ALL_RELATIONS, "swap": SIBLING}.get(kind, XY_RELATIONS)


def _copies(kind, arr, land, sems):
    x, y, c = lax.axis_index("x"), lax.axis_index("y"), lax.axis_index("c")
    out = []
    for (fx, fy, fc), (send_sem, recv_sem) in zip(_relations(kind), sems):
        px, py, pc = _flip(x, fx), _flip(y, fy), _flip(c, fc)
        if kind == "bcast":
            mine, theirs = 4 * x + 2 * y + c, 4 * px + 2 * py + pc
        else:
            mine, theirs = 2 * x + y, 2 * px + py
        src, to_mine, to_theirs = arr, land.at[mine], land.at[theirs]
        if kind == "swap":
            to_mine = to_theirs = land
        if kind == "scatter":
            src = arr.at[theirs]
        if kind == "gather_half":
            rows = _my_half(arr.shape[0], c)
            src, to_mine, to_theirs = arr.at[rows], land.at[mine, rows], land.at[theirs, rows]
        both = dict(src_ref=src, send_sem=send_sem, recv_sem=recv_sem, device_id=(px, py, pc), device_id_type=MESH)
        out.append((pltpu.make_async_remote_copy(dst_ref=to_mine, **both),
                    pltpu.make_async_remote_copy(dst_ref=to_theirs, **both)))
    return out


def _my_half(nrows, c):
    return pl.ds(c * (nrows // 2), nrows // 2)


def _share_halves(name, kinds, lands):
    ks = [k for k, kd in enumerate(kinds) if kd == "gather_half"]
    n = len(ks)

    def body(*refs):
        l_refs = refs[n:2 * n]
        send_sems, recv_sems = refs[2 * n:]
        x, y, c = lax.axis_index("x"), lax.axis_index("y"), lax.axis_index("c")
        copies = []
        for i in range(n):
            nrows = l_refs[i].shape[1]
            for r, (fx, fy, _) in enumerate(XY_RELATIONS):
                slot = 2 * _flip(x, fx) + _flip(y, fy)
                both = dict(src_ref=l_refs[i].at[slot, _my_half(nrows, c)], send_sem=send_sems.at[i, r],
                            recv_sem=recv_sems.at[i, r], device_id=(x, y, 1 - c), device_id_type=MESH)
                copies.append((pltpu.make_async_remote_copy(dst_ref=l_refs[i].at[slot, _my_half(nrows, c)], **both),
                               pltpu.make_async_remote_copy(dst_ref=l_refs[i].at[slot, _my_half(nrows, 1 - c)], **both)))
        for send, _ in copies:
            send.start()
        for _, arrival in copies:
            arrival.wait_recv()
        for send, _ in copies:
            send.wait_send()

    outs = _comm_call(
        body, name=name,
        in_specs=_any_spec(n), out_specs=_any_spec(n),
        out_shape=[jax.ShapeDtypeStruct(lands[k].shape, lands[k].dtype) for k in ks],
        input_output_aliases={i: i for i in range(n)},
        scratch_shapes=[pltpu.SemaphoreType.DMA((n, 3)), pltpu.SemaphoreType.DMA((n, 3))],
    )(*[lands[k] for k in ks])
    lands = list(lands)
    for k, o in zip(ks, outs):
        lands[k] = o
    return lands


def _sem_pairs(kinds, sems):
    out, at = [], 0
    for kind in kinds:
        nrel = len(_relations(kind))
        out.append([(sems[at + 2 * r], sems[at + 2 * r + 1]) for r in range(nrel)])
        at += 2 * nrel
    return out


def _part_of(st, ks):
    at = [0]
    for kind in st["kinds"]:
        at.append(at[-1] + 2 * len(_relations(kind)))
    return dict(kinds=[st["kinds"][k] for k in ks], arrs=[st["arrs"][k] for k in ks],
                lands=[st["lands"][k] for k in ks], sems=[s for k in ks for s in st["sems"][at[k]:at[k + 1]]])


def _exchange_start(name, kinds, arrs, lands):
    n = len(arrs)
    nsem = sum(2 * len(_relations(kd)) for kd in kinds)

    def body(*refs):
        a_refs, l_refs = refs[:n], refs[n:2 * n]
        pairs = _sem_pairs(kinds, refs[2 * n:2 * n + nsem])
        token = refs[-1]
        for k in range(n):
            for send, _ in _copies(kinds[k], a_refs[k], l_refs[k], pairs[k]):
                send.start()
        token[...] = jnp.zeros_like(token)

    thru = [pltpu.HBM(a.shape, a.dtype) for a in list(arrs) + list(lands)]
    outs = _split_call(
        body, name=name,
        out_shape=(*[pltpu.SemaphoreType.DMA(())] * nsem, *thru, jax.ShapeDtypeStruct((8, LANE), F32)),
        in_specs=[HBM_SPEC] * (2 * n),
        out_specs=(*[SEM_SPEC] * nsem, *[HBM_SPEC] * (2 * n), pl.BlockSpec(memory_space=pltpu.VMEM)),
        input_output_aliases={i: nsem + i for i in range(2 * n)},
        compiler_params=pltpu.CompilerParams(has_side_effects=DATAFLOW),
    )(*[pltpu.with_memory_space_constraint(a, pltpu.HBM) for a in list(arrs) + list(lands)])
    return dict(kinds=kinds, sems=outs[:nsem], arrs=outs[nsem:nsem + n], lands=outs[nsem + n:nsem + 2 * n],
                token=outs[-1])


def _exchange_wait(name, st, after):
    kinds = st["kinds"]
    n = len(kinds)
    nsem = len(st["sems"])

    def body(*refs):
        a_refs, l_refs = refs[:n], refs[n:2 * n]
        pairs = _sem_pairs(kinds, refs[2 * n:2 * n + nsem])
        for k in range(n):
            for _, arrival in _copies(kinds[k], a_refs[k], l_refs[k], pairs[k]):
                arrival.wait_send()
                arrival.wait_recv()
        refs[-1][...] = jnp.zeros_like(refs[-1])

    ins = list(st["arrs"]) + list(st["lands"])
    outs = _split_call(
        body, name=name,
        out_shape=[pltpu.HBM(a.shape, a.dtype) for a in ins] + [jax.ShapeDtypeStruct((8, LANE), F32)],
        in_specs=[HBM_SPEC] * (2 * n) + [SEM_SPEC] * nsem + [pl.BlockSpec(memory_space=pl.ANY)],
        out_specs=[HBM_SPEC] * (2 * n) + [pl.BlockSpec(memory_space=pltpu.VMEM)],
        input_output_aliases={i: i for i in range(2 * n)},
        compiler_params=pltpu.CompilerParams(has_side_effects=DATAFLOW),
    )(*ins, *st["sems"], after)
    return outs[:n], outs[n:2 * n], outs[-1]


def _landing(own, slot, nslot):
    return lax.dynamic_update_slice(lax.empty((nslot,) + own.shape, own.dtype), own[None], (slot,) + (0,) * own.ndim)


def _col_shards(a, n=N_SHARD):
    r, c = a.shape
    return a.reshape(r, n, c // n).transpose(1, 0, 2)


def _from_col_shards(a):
    n, r, cs = a.shape
    return a.transpose(1, 0, 2).reshape(r, n * cs)


def kernel(x, meta_tokens, ffn1_w_gate, ffn1_w_up, ffn1_w_down, ln1_g, ln1_b, w_in, w_gate_up, b_gate, w_pool, pool_scale, gla_norm_g, w_out, ln2_g, ln2_b, ffn2_w_gate, ffn2_w_up, ffn2_w_down, ln3_g, ln3_b, loss_target, m_meta_tokens, m_ffn1_w_gate, m_ffn1_w_up, m_ffn1_w_down, m_ln1_g, m_ln1_b, m_w_in, m_w_gate_up, m_b_gate, m_w_pool, m_pool_scale, m_gla_norm_g, m_w_out, m_ln2_g, m_ln2_b, m_ffn2_w_gate, m_ffn2_w_up, m_ffn2_w_down, m_ln3_g, m_ln3_b, v_meta_tokens, v_ffn1_w_gate, v_ffn1_w_up, v_ffn1_w_down, v_ln1_g, v_ln1_b, v_w_in, v_w_gate_up, v_b_gate, v_w_pool, v_pool_scale, v_gla_norm_g, v_w_out, v_ln2_g, v_ln2_b, v_ffn2_w_gate, v_ffn2_w_up, v_ffn2_w_down, v_ln3_g, v_ln3_b):
    w = dict(meta_tokens=meta_tokens, ffn1_w_gate=ffn1_w_gate, ffn1_w_up=ffn1_w_up, ffn1_w_down=ffn1_w_down,
             ln1_g=ln1_g, ln1_b=ln1_b, w_in=w_in, w_gate_up=w_gate_up, b_gate=b_gate, w_pool=w_pool,
             pool_scale=pool_scale, gla_norm_g=gla_norm_g, w_out=w_out, ln2_g=ln2_g, ln2_b=ln2_b,
             ffn2_w_gate=ffn2_w_gate, ffn2_w_up=ffn2_w_up, ffn2_w_down=ffn2_w_down, ln3_g=ln3_g, ln3_b=ln3_b)
    mom1 = dict(meta_tokens=m_meta_tokens, ffn1_w_gate=m_ffn1_w_gate, ffn1_w_up=m_ffn1_w_up,
                ffn1_w_down=m_ffn1_w_down, ln1_g=m_ln1_g, ln1_b=m_ln1_b, w_in=m_w_in, w_gate_up=m_w_gate_up,
                b_gate=m_b_gate, w_pool=m_w_pool, pool_scale=m_pool_scale, gla_norm_g=m_gla_norm_g, w_out=m_w_out,
                ln2_g=m_ln2_g, ln2_b=m_ln2_b, ffn2_w_gate=m_ffn2_w_gate, ffn2_w_up=m_ffn2_w_up,
                ffn2_w_down=m_ffn2_w_down, ln3_g=m_ln3_g, ln3_b=m_ln3_b)
    mom2 = dict(meta_tokens=v_meta_tokens, ffn1_w_gate=v_ffn1_w_gate, ffn1_w_up=v_ffn1_w_up,
                ffn1_w_down=v_ffn1_w_down, ln1_g=v_ln1_g, ln1_b=v_ln1_b, w_in=v_w_in, w_gate_up=v_w_gate_up,
                b_gate=v_b_gate, w_pool=v_w_pool, pool_scale=v_pool_scale, gla_norm_g=v_gla_norm_g, w_out=v_w_out,
                ln2_g=v_ln2_g, ln2_b=v_ln2_b, ffn2_w_gate=v_ffn2_w_gate, ffn2_w_up=v_ffn2_w_up,
                ffn2_w_down=v_ffn2_w_down, ln3_g=v_ln3_g, ln3_b=v_ln3_b)

    xs = x[0]
    s_len, d = xs.shape
    nl = ln1_g.shape[0]
    alpha = (2.0 * nl) ** 0.25
    t_real = N_META + s_len
    t_pad = -(-t_real // LANE) * LANE
    pw = pool_scale.shape[1]
    kw = b_gate.shape[1]
    vw = gla_norm_g.shape[1]
    rank = w_gate_up.shape[1]
    widths = (pw, kw, kw, vw, vw)
    n_main = sum(widths)
    dff_s = ffn1_w_gate.shape[2]
    dff_c = N_SHARD * dff_s // FFN_CHUNKS

    me_xy = 2 * lax.axis_index("x") + lax.axis_index("y")
    me_all = 2 * me_xy + lax.axis_index("c")
    ffn1_names = ("ffn1_w_gate", "ffn1_w_up", "ffn1_w_down")
    mix_names = ("w_out", "w_gate_up", "w_in")
    ffn2_names = ("ffn2_w_gate", "ffn2_w_up", "ffn2_w_down")

    gate_up = ("ffn1_w_gate", "ffn1_w_up", "ffn2_w_gate", "ffn2_w_up")

    def stored(n, a):
        if n in gate_up:
            return jnp.swapaxes(a, 1, 2)
        return jnp.transpose(a, (2, 0, 1)) if n == "w_in" else a

    def as_given(n, a):
        if n in gate_up:
            return jnp.swapaxes(a, 1, 2)
        if n == "w_in":
            return jnp.transpose(a.reshape(-1, nl, d), (1, 2, 0))
        return a.reshape(w[n].shape)

    stages = [[("meta_tokens", None)], [(n, 0) for n in ffn1_names], [(n, 0) for n in mix_names + ffn2_names]]
    stages += [[(n, l) for n in BIG] for l in range(1, nl)]
    gathers, wa = {}, {}

    halved = ffn1_names + ffn2_names + ("w_out",)

    def start_gathers():
        own = []
        for n, l in [item for items in stages for item in items]:
            a = meta_tokens if l is None else (stored(n, w[n])[:, l] if n == "w_in" else stored(n, w[n])[l])
            own.append(a if l is None else a.astype(BF16))
        kinds = ["gather_half" if n in halved else "gather" for items in stages for n, _ in items]
        return _exchange_start("gather_start", kinds, own, [_landing(a, me_xy, N_SHARD) for a in own])

    def arrive(si, after):
        first = sum(len(items) for items in stages[:si])
        part = _part_of(gathers, range(first, first + len(stages[si])))
        _, lands, _ = _exchange_wait(f"gather_wait_{si}", part, after)
        if any(kd == "gather_half" for kd in part["kinds"]):
            lands = _share_halves(f"gather_share_{si}", part["kinds"], lands)
        for item, a in zip(stages[si], lands):
            wa[item] = a.reshape(FFN_CHUNKS, -1, d) if item[0] in ffn1_names + ffn2_names else a

    def mixer_weights(l):
        wi = wa["w_in", l].reshape(-1, d)
        return dict(w_main=wi[:n_main], w_lr=jnp.pad(wi[n_main:], ((0, LANE - rank), (0, 0))),
                    wgu=jnp.pad(_from_col_shards(wa["w_gate_up", l]), ((0, LANE - rank), (0, 0))),
                    wout=wa["w_out", l].reshape(-1, d))

    wp16 = w_pool.astype(BF16)
    ones = jnp.ones((1, d), F32)
    zeros = jnp.zeros((1, d), F32)
    target = jnp.concatenate([jnp.zeros((N_META, d), F32), loss_target[0], jnp.zeros((t_pad - t_real, d), F32)], axis=0)

    gathers = start_gathers()
    arrive(0, gathers["token"])
    meta_full = _from_col_shards(wa["meta_tokens", None])
    h0 = jnp.concatenate([meta_full, xs, jnp.zeros((t_pad - t_real, d), F32)], axis=0)
    arrive(1, h0[:8, :LANE] + target[:8, :LANE])

    saved, mw = [], []
    cur, cur_g, cur_b = h0, ones, zeros
    for l in range(nl):
        s = {}
        xh1, rs1, hb0, g1, u1 = _ffn_fwd(cur, cur_g, cur_b, wa["ffn1_w_gate", l], wa["ffn1_w_up", l],
                                         wa["ffn1_w_down", l], alpha, f"ffn1_fwd_{l}")
        if l == 0:
            arrive(2, xh1)
        mw.append(mixer_weights(l))
        up, q, k, v, r, zg, la, hb1 = _inproj_fwd(xh1, ln1_g[l:l + 1], ln1_b[l:l + 1], mw[l]["w_main"], mw[l]["w_lr"],
                                                  mw[l]["wgu"], b_gate[l:l + 1], widths, f"inproj_fwd_{l}")
        yp, pb = _pool_fwd(up, wp16[l], pool_scale[l:l + 1], f"pool_fwd_{l}")
        o, yg, sall = _gla_fwd(q, k, v, la, r, gla_norm_g[l:l + 1], f"gla_fwd_{l}")
        xh2, rs2 = _outproj_fwd(yp, yg, mw[l]["wout"], xh1, ln1_g[l:l + 1], ln1_b[l:l + 1], alpha, f"outproj_fwd_{l}")
        if l + 1 < nl:
            arrive(l + 3, xh2)
        head = (ln3_g[l:l + 1], ln3_b[l:l + 1], target, s_len) if l == nl - 1 else None
        xh3, rs3, hb2, g2, u2, *at_head = _ffn_fwd(xh2, ln2_g[l:l + 1], ln2_b[l:l + 1], wa["ffn2_w_gate", l],
                                                   wa["ffn2_w_up", l], wa["ffn2_w_down", l], alpha, f"ffn2_fwd_{l}",
                                                   head)
        s.update(xh1=xh1, rs1=rs1, hb0=hb0, g1=g1, u1=u1, q=q, k=k, v=v, r=r, zg=zg, la=la, hb1=hb1, yp=yp, pb=pb,
                 o=o, yg=yg, sall=sall, xh2=xh2, rs2=rs2, xh3=xh3, rs3=rs3, hb2=hb2, g2=g2, u2=u2)
        saved.append(s)
        cur, cur_g, cur_b = xh3, ln3_g[l:l + 1], ln3_b[l:l + 1]

    dh, loss_acc = at_head
    loss = lax.psum(loss_acc[0, 0], ("x", "y", "c"))

    small_grads = {n: [None] * nl for n in SMALL}
    scatters = []

    def depart(name, items, grads, kinds=None):
        lands = [_landing(g, me_all, N_DEV) if kd == "bcast" else lax.empty(g.shape, g.dtype)
                 for g, kd in zip(grads, kinds or ["scatter"] * len(grads))]
        st = _exchange_start(name, kinds or ["scatter"] * len(grads), grads, lands)
        scatters.append((name, items, st))
        return st["token"]

    def pack(parts):
        flat = jnp.concatenate([parts[n].reshape(-1) for n in SMALL])
        return flat.reshape(-1, LANE)

    def ffn_wgrad(n, l, hb, dgb, dub, act, dfb, after=None):
        if n.endswith("down"):
            dw = _wgrad(act, dfb, dff_c, d, f"{n}_grad_{l}", after)
        else:
            dw = _wgrad(dgb if n.endswith("gate") else dub, hb, dff_c, d, f"{n}_grad_{l}", after)
        return dw.reshape(N_SHARD, dff_s, d)

    late = []
    for l in reversed(range(nl)):
        s = saved[l]
        dh, dfb, dgb, dub, act, dgam, dbet = _ffn_bwd(dh, s["xh3"], s["rs3"], ln3_g[l:l + 1], s["g2"], s["u2"],
                                                      wa["ffn2_w_gate", l], wa["ffn2_w_up", l], wa["ffn2_w_down", l],
                                                      alpha, f"ffn2_bwd_{l}")
        small_grads["ln3_g"][l], small_grads["ln3_b"][l] = dgam, dbet
        gone = depart(f"scatter_start_ffn2_{l}", [(n, l) for n in ffn2_names],
                      [ffn_wgrad(n, l, s["hb2"], dgb, dub, act, dfb) for n in ffn2_names])

        dyb, dyp, dyg, dres, dgam, dbet = _outproj_bwd(dh, s["xh2"], s["rs2"], ln2_g[l:l + 1] + gone[0:1, 0:1],
                                                       mw[l]["wout"], pw, alpha, f"outproj_bwd_{l}")
        small_grads["ln2_g"][l], small_grads["ln2_b"][l] = dgam, dbet
        dwo = jnp.concatenate([_wgrad(s["yp"], dyb, pw, d, f"dwout_pool_{l}"),
                               _wgrad(s["yg"], dyb, vw, d, f"dwout_gla_{l}")], axis=0)
        dq, dk, dv, dr, dzg, dwgu, dbg, dgn = _gla_bwd(dyg, s["o"], s["r"], gla_norm_g[l:l + 1], s["q"], s["k"],
                                                       s["v"], s["la"], s["zg"], s["sall"], mw[l]["wgu"],
                                                       f"gla_bwd_{l}")
        dup, dwp, dsc = _pool_bwd(dyp, s["pb"], wp16[l], pool_scale[l:l + 1], f"pool_bwd_{l}")
        small_grads["b_gate"][l], small_grads["gla_norm_g"][l] = dbg, dgn
        small_grads["w_pool"][l], small_grads["pool_scale"][l] = dwp, dsc
        dh, dz = _inproj_bwd(dres, [dup, dq, dk, dv, dr], dzg, mw[l]["w_main"], mw[l]["w_lr"], f"inproj_bwd_{l}")
        dwi = jnp.concatenate([_wgrad(dz, s["hb1"], 4 * LANE, d, f"dwin_main_{l}"),
                               _wgrad(dzg, s["hb1"], LANE, d, f"dwin_lr_{l}")[:rank]], axis=0)
        gone = depart(f"scatter_start_mix_{l}", [(n, l) for n in mix_names],
                      [dwo.reshape(N_SHARD, -1, d), _col_shards(dwgu[:rank]), dwi.reshape(N_SHARD, -1, d)])

        dh, dfb, dgb, dub, act, dgam, dbet = _ffn_bwd(dh, s["xh1"], s["rs1"], ln1_g[l:l + 1] + gone[0:1, 0:1],
                                                      s["g1"], s["u1"], wa["ffn1_w_gate", l], wa["ffn1_w_up", l],
                                                      wa["ffn1_w_down", l], alpha, f"ffn1_bwd_{l}")
        small_grads["ln1_g"][l], small_grads["ln1_b"][l] = dgam, dbet
        if l:
            gone = depart(f"scatter_start_ffn1_{l}", [(n, l) for n in ffn1_names],
                          [ffn_wgrad(n, l, s["hb0"], dgb, dub, act, dfb) for n in ffn1_names])
            ln3_g = ln3_g.at[l - 1:l].add(gone[0:1, 0:1])
            continue
        grad_x = dh[N_META:t_real][None]
        small_vec = pack({n: jnp.stack(small_grads[n]) for n in SMALL})
        gone = depart("scatter_start_rest", [("meta_tokens", 0), ("small", 0)],
                      [_col_shards(dh[:N_META]), small_vec], ["scatter", "bcast"])
        for n in ffn1_names:
            g = ffn_wgrad(n, l, s["hb0"], dgb, dub, act, dfb, after=gone)
            gone = depart(f"scatter_start_{n}", [(n, l)], [g])
            late.append(scatters.pop())

    sent, recv, results, firsts = {}, {}, {}, []
    my_slot = me_xy.reshape(1).astype(jnp.int32)

    def collect(group, after):
        for name, items, st in group:
            arrs, lands, _ = _exchange_wait(name.replace("start", "wait"), st, after)
            for item, a, b in zip(items, arrs, lands):
                sent[item], recv[item] = a, b

    def reduce_and_update(names, tag):
        partial = []
        for n in names:
            layers = [(n, l) for l in range(1 if n == "meta_tokens" else nl)]
            partial.append(_sum_slots(my_slot, [sent[it] for it in layers], [recv[it] for it in layers],
                                      f"sum_{n}", n == "w_in"))
        swap = _exchange_start(f"swap_start_{tag}", ["swap"] * len(names), partial,
                               [lax.empty(p.shape, p.dtype) for p in partial])
        after = swap["token"]
        for k, n in enumerate(names):
            (mine,), (theirs,), _ = _exchange_wait(f"swap_wait_{n}", _part_of(swap, [k]), after)
            fit = lambda a: stored(n, a).reshape(mine.shape)
            outs = _adamw(fit(w[n]), [mine, theirs], fit(mom1[n]), fit(mom2[n]), f"adamw_{n}")
            results[n] = [as_given(n, o) for o in outs]
            firsts.append(outs[1][0, 0, 0])
            after = outs[1]

    collect(scatters, gone)
    early = [n for n in ("meta_tokens",) + BIG if n not in ffn1_names]
    reduce_and_update(early, "early")
    small_terms = [recv["small", 0][i][None] for i in range(N_DEV)]
    souts = _adamw(pack(w)[None], small_terms, pack(mom1)[None], pack(mom2)[None], "adamw_small")
    off = 0
    for n in SMALL:
        size = w[n].size
        results[n] = [o.reshape(-1)[off:off + size].reshape(w[n].shape) for o in souts]
        off += size
    collect(late, souts[0][0, :8] + functools.reduce(jnp.add, firsts))
    reduce_and_update(ffn1_names, "late")

    out = [loss, grad_x]
    for part in range(4):
        out += [results[n][part] for n in WEIGHTS]
    return tuple(out)
```
